```python
import math
import jax, jax.numpy as jnp
from jax import lax
import numpy as np

D_MODEL = 1024
BATCH = 16
SEQ = 2048
DEPTH = 4

N_MLA_HEADS = 4
MLA_NOPE_DIM = 64
MLA_ROPE_DIM = 32
MLA_QK_DIM = MLA_NOPE_DIM + MLA_ROPE_DIM
MLA_V_DIM = 64
MLA_Q_RANK = 384
MLA_KV_RANK = 256
MLA_WIDTH = N_MLA_HEADS * MLA_V_DIM
N_FOX_HEADS = 4
FOX_HEAD_DIM = 64
FOX_WIDTH = N_FOX_HEADS * FOX_HEAD_DIM
FOX_FORGET_BIAS_INIT = 3.0
S5_GROUPS = 16
S5_GROUP_CH = 16
S5_STATE = 64
S5_WIDTH = S5_GROUPS * S5_GROUP_CH
N_BRANCHES = 3
BRANCH_WIDTH = 256
D_FF = 2816
CONV_WIDTH = 3
Q_BLOCK = 128
ROPE_THETA = 10000.0
NORM_EPS = 1e-6
NEG_INF = -1e30
IN_WIDTHS = (MLA_Q_RANK, MLA_KV_RANK, MLA_ROPE_DIM,
             FOX_WIDTH, FOX_WIDTH, FOX_WIDTH, N_FOX_HEADS,
             S5_WIDTH, N_BRANCHES * D_MODEL)
D_IN = sum(IN_WIDTHS)

kernel_name = "hybrid_mla_fox_s5_gated_trunk"


def rms_norm(x, gain):
    x32 = x.astype(jnp.float32)
    y = x32 * lax.rsqrt(jnp.mean(x32 * x32, axis=-1, keepdims=True) + NORM_EPS)
    return (y * gain.astype(jnp.float32)).astype(x.dtype)


def rope_tables(positions):
    inv_freq = ROPE_THETA ** (-jnp.arange(0, MLA_ROPE_DIM, 2, dtype=jnp.float32) / MLA_ROPE_DIM)
    ang = positions.astype(jnp.float32)[..., None] * inv_freq
    return jnp.cos(ang)[:, :, None, :], jnp.sin(ang)[:, :, None, :]


def apply_rope_tail(x, cos, sin):
    x_pass, x_rot = x[..., :MLA_NOPE_DIM], x[..., MLA_NOPE_DIM:]
    x1, x2 = jnp.split(x_rot, 2, axis=-1)
    rot = jnp.concatenate([x1 * cos - x2 * sin, x2 * cos + x1 * sin], axis=-1)
    return jnp.concatenate([x_pass, rot.astype(x.dtype)], axis=-1)


def causal_block_attention(q, k, v, log_forget_cum=None):
    seq = q.shape[1]
    scale = q.shape[-1] ** -0.5
    outs = []
    for i in range(seq // Q_BLOCK):
        q_lo, q_hi = i * Q_BLOCK, (i + 1) * Q_BLOCK
        qb, kb, vb = q[:, q_lo:q_hi], k[:, :q_hi], v[:, :q_hi]
        s = jnp.einsum('bqhd,bkhd->bhqk', qb, kb).astype(jnp.float32) * scale
        if log_forget_cum is not None:
            c_q = jnp.transpose(log_forget_cum[:, q_lo:q_hi], (0, 2, 1))[..., :, None]
            c_k = jnp.transpose(log_forget_cum[:, :q_hi], (0, 2, 1))[..., None, :]
            s = s + (c_q - c_k)
        mask = (q_lo + jnp.arange(Q_BLOCK))[:, None] >= jnp.arange(q_hi)[None, :]
        p = jax.nn.softmax(jnp.where(mask, s, NEG_INF), axis=-1).astype(v.dtype)
        outs.append(jnp.einsum('bhqk,bkhd->bqhd', p, vb))
    return jnp.concatenate(outs, axis=1)


def s5_branch(u, lam_re, lam_im, b_re, b_im, c_re, c_im, d, log_step, w_glu, b_glu):
    bsz, seq, _ = u.shape
    u32 = u.astype(jnp.float32).reshape(bsz, seq, S5_GROUPS, S5_GROUP_CH)
    lam = lax.complex(lam_re.astype(jnp.float32), lam_im.astype(jnp.float32))
    step = jnp.exp(log_step.astype(jnp.float32))[:, None]
    lam_bar = jnp.exp(lam * step)
    b_mat = lax.complex(b_re.astype(jnp.float32), b_im.astype(jnp.float32))
    b_bar = ((lam_bar - 1.0) / lam)[..., None] * b_mat
    bu = jnp.einsum('gph,bsgh->bsgp', b_bar, u32.astype(jnp.complex64))
    a = jnp.broadcast_to(lam_bar, bu.shape)

    def combine(e1, e2):
        a1, x1 = e1
        a2, x2 = e2
        return a1 * a2, a2 * x1 + x2

    _, state = lax.associative_scan(combine, (a, bu), axis=1)
    c_mat = lax.complex(c_re.astype(jnp.float32), c_im.astype(jnp.float32))
    y = jnp.real(jnp.einsum('ghp,bsgp->bsgh', c_mat, state)) + d.astype(jnp.float32) * u32
    y = jax.nn.gelu(y.reshape(bsz, seq, S5_WIDTH))
    y = y * jax.nn.sigmoid(y @ w_glu.astype(jnp.float32) + b_glu.astype(jnp.float32))
    return y.astype(u.dtype)


def causal_depthwise_conv(h, w):
    seq = h.shape[1]
    hp = jnp.pad(h, ((0, 0), (CONV_WIDTH - 1, 0), (0, 0)))
    return sum(w[j] * hp[:, j:j + seq] for j in range(CONV_WIDTH))


def mixer_block(h, cos, sin, w_in, q_lat_norm_g, w_uq, kv_lat_norm_g, w_ukv,
                mla_q_norm_g, mla_k_norm_g, fox_q_norm_g, fox_k_norm_g, fox_f_bias,
                s5_lambda_re, s5_lambda_im, s5_b_re, s5_b_im, s5_c_re, s5_c_im, s5_d,
                s5_log_step, s5_w_glu, s5_b_glu, w_branch, w_out):
    bsz, seq, _ = h.shape
    proj = h @ w_in
    split_points = np.cumsum(IN_WIDTHS)[:-1].tolist()
    c_q, c_kv, k_r, fq, fk, fv, f_logit, u, gate_logits = jnp.split(proj, split_points, axis=-1)

    q = jnp.einsum('bsr,rhd->bshd', rms_norm(c_q, q_lat_norm_g), w_uq)
    kv = jnp.einsum('bsr,rhd->bshd', rms_norm(c_kv, kv_lat_norm_g), w_ukv)
    k_nope, v_mla = kv[..., :MLA_NOPE_DIM], kv[..., MLA_NOPE_DIM:]
    k_rope = jnp.broadcast_to(k_r[:, :, None, :], (bsz, seq, N_MLA_HEADS, MLA_ROPE_DIM))
    k_mla = jnp.concatenate([k_nope, k_rope], axis=-1)
    q_mla = apply_rope_tail(rms_norm(q, mla_q_norm_g), cos, sin)
    k_mla = apply_rope_tail(rms_norm(k_mla, mla_k_norm_g), cos, sin)
    o_mla = causal_block_attention(q_mla, k_mla, v_mla).reshape(bsz, seq, MLA_WIDTH)

    q_fox = rms_norm(fq.reshape(bsz, seq, N_FOX_HEADS, FOX_HEAD_DIM), fox_q_norm_g)
    k_fox = rms_norm(fk.reshape(bsz, seq, N_FOX_HEADS, FOX_HEAD_DIM), fox_k_norm_g)
    v_fox = fv.reshape(bsz, seq, N_FOX_HEADS, FOX_HEAD_DIM)
    log_f = jax.nn.log_sigmoid((f_logit + fox_f_bias).astype(jnp.float32))
    cum_log_f = lax.cumsum(log_f, axis=1)
    o_fox = causal_block_attention(q_fox, k_fox, v_fox, cum_log_f).reshape(bsz, seq, FOX_WIDTH)

    o_s5 = s5_branch(u, s5_lambda_re, s5_lambda_im, s5_b_re, s5_b_im, s5_c_re, s5_c_im,
                     s5_d, s5_log_step, s5_w_glu, s5_b_glu)

    branches = jnp.stack([o_mla, o_fox, o_s5], axis=2)
    gates = jax.nn.sigmoid(gate_logits.reshape(bsz, seq, N_BRANCHES, D_MODEL))
    projected = jnp.einsum('bsnw,nwd->bsnd', branches, w_branch)
    merged = jnp.einsum('bsnd,bsnd->bsd', gates, projected)
    return merged @ w_out


def conv_gated_ffn(h, w_up, conv_w, w_down):
    up = causal_depthwise_conv(h @ w_up, conv_w)
    gate, val = jnp.split(up, 2, axis=-1)
    return (jax.nn.silu(gate) * val) @ w_down


def _fwd_setup_inputs(seed: int = 0) -> dict:
    key = jax.random.key(seed)
    ks = iter(jax.random.split(key, 40))
    f32 = jnp.float32
    L = DEPTH

    def nrm(shape, scale):
        return scale * jax.random.normal(next(ks), shape, f32)

    def gain(shape):
        return 1.0 + nrm(shape, 0.02)

    x = nrm((BATCH, SEQ, D_MODEL), 1.0)
    offset = jax.random.randint(next(ks), (BATCH, 1), 0, 4096, dtype=jnp.int32)
    positions = offset + jnp.arange(SEQ, dtype=jnp.int32)[None, :]

    attn_norm_g = gain((L, D_MODEL))
    w_in = nrm((L, D_MODEL, D_IN), D_MODEL ** -0.5)
    q_lat_norm_g = gain((L, MLA_Q_RANK))
    w_uq = nrm((L, MLA_Q_RANK, N_MLA_HEADS, MLA_QK_DIM), MLA_Q_RANK ** -0.5)
    kv_lat_norm_g = gain((L, MLA_KV_RANK))
    w_ukv = nrm((L, MLA_KV_RANK, N_MLA_HEADS, MLA_NOPE_DIM + MLA_V_DIM), MLA_KV_RANK ** -0.5)
    mla_q_norm_g = gain((L, MLA_QK_DIM))
    mla_k_norm_g = gain((L, MLA_QK_DIM))
    fox_q_norm_g = gain((L, FOX_HEAD_DIM))
    fox_k_norm_g = gain((L, FOX_HEAD_DIM))
    fox_f_bias = FOX_FORGET_BIAS_INIT + nrm((L, N_FOX_HEADS), 0.5)
    s5_lambda_re = -0.5 + nrm((L, S5_GROUPS, S5_STATE), 0.01)
    s5_lambda_im = math.pi * jnp.arange(S5_STATE, dtype=f32) + nrm((L, S5_GROUPS, S5_STATE), 0.01)
    s5_b_re = nrm((L, S5_GROUPS, S5_STATE, S5_GROUP_CH), (2 * S5_GROUP_CH) ** -0.5)
    s5_b_im = nrm((L, S5_GROUPS, S5_STATE, S5_GROUP_CH), (2 * S5_GROUP_CH) ** -0.5)
    s5_c_re = nrm((L, S5_GROUPS, S5_GROUP_CH, S5_STATE), (2 * S5_STATE) ** -0.5)
    s5_c_im = nrm((L, S5_GROUPS, S5_GROUP_CH, S5_STATE), (2 * S5_STATE) ** -0.5)
    s5_d = nrm((L, S5_GROUPS, S5_GROUP_CH), 0.5)
    s5_log_step = jnp.log(jax.random.uniform(next(ks), (L, S5_GROUPS), f32, minval=0.001, maxval=0.1))
    s5_w_glu = nrm((L, S5_WIDTH, S5_WIDTH), S5_WIDTH ** -0.5)
    s5_b_glu = nrm((L, S5_WIDTH), 0.02)
    w_branch = nrm((L, N_BRANCHES, BRANCH_WIDTH, D_MODEL), BRANCH_WIDTH ** -0.5)
    w_out = nrm((L, D_MODEL, D_MODEL), D_MODEL ** -0.5)
    ffn_norm_g = gain((L, D_MODEL))
    w_up = nrm((L, D_MODEL, 2 * D_FF), D_MODEL ** -0.5)
    ffn_conv_w = nrm((L, CONV_WIDTH, 2 * D_FF), 0.2).at[:, CONV_WIDTH - 1].add(1.0)
    w_down = nrm((L, D_FF, D_MODEL), D_FF ** -0.5)
    return {"x": x, "positions": positions, "attn_norm_g": attn_norm_g, "w_in": w_in,
            "q_lat_norm_g": q_lat_norm_g, "w_uq": w_uq, "kv_lat_norm_g": kv_lat_norm_g,
            "w_ukv": w_ukv, "mla_q_norm_g": mla_q_norm_g, "mla_k_norm_g": mla_k_norm_g,
            "fox_q_norm_g": fox_q_norm_g, "fox_k_norm_g": fox_k_norm_g, "fox_f_bias": fox_f_bias,
            "s5_lambda_re": s5_lambda_re, "s5_lambda_im": s5_lambda_im, "s5_b_re": s5_b_re,
            "s5_b_im": s5_b_im, "s5_c_re": s5_c_re, "s5_c_im": s5_c_im, "s5_d": s5_d,
            "s5_log_step": s5_log_step, "s5_w_glu": s5_w_glu, "s5_b_glu": s5_b_glu,
            "w_branch": w_branch, "w_out": w_out, "ffn_norm_g": ffn_norm_g, "w_up": w_up,
            "ffn_conv_w": ffn_conv_w, "w_down": w_down}


def _fwd_reference(x, positions, attn_norm_g, w_in, q_lat_norm_g, w_uq, kv_lat_norm_g, w_ukv,
              mla_q_norm_g, mla_k_norm_g, fox_q_norm_g, fox_k_norm_g, fox_f_bias,
              s5_lambda_re, s5_lambda_im, s5_b_re, s5_b_im, s5_c_re, s5_c_im, s5_d,
              s5_log_step, s5_w_glu, s5_b_glu, w_branch, w_out, ffn_norm_g, w_up,
              ffn_conv_w, w_down):
    cos, sin = rope_tables(positions)
    for l in range(DEPTH):
        h = rms_norm(x, attn_norm_g[l])
        x = x + mixer_block(h, cos, sin, w_in[l], q_lat_norm_g[l], w_uq[l], kv_lat_norm_g[l],
                            w_ukv[l], mla_q_norm_g[l], mla_k_norm_g[l], fox_q_norm_g[l],
                            fox_k_norm_g[l], fox_f_bias[l], s5_lambda_re[l], s5_lambda_im[l],
                            s5_b_re[l], s5_b_im[l], s5_c_re[l], s5_c_im[l], s5_d[l],
                            s5_log_step[l], s5_w_glu[l], s5_b_glu[l], w_branch[l], w_out[l])
        h = rms_norm(x, ffn_norm_g[l])
        x = x + conv_gated_ffn(h, w_up[l], ffn_conv_w[l], w_down[l])
    return x


import jax as _jax
import jax.numpy as _jnp

TWIN_FORMAT = 'train_step'
FWD_PARAMS = ['x', 'positions', 'attn_norm_g', 'w_in', 'q_lat_norm_g', 'w_uq', 'kv_lat_norm_g', 'w_ukv', 'mla_q_norm_g', 'mla_k_norm_g', 'fox_q_norm_g', 'fox_k_norm_g', 'fox_f_bias', 's5_lambda_re', 's5_lambda_im', 's5_b_re', 's5_b_im', 's5_c_re', 's5_c_im', 's5_d', 's5_log_step', 's5_w_glu', 's5_b_glu', 'w_branch', 'w_out', 'ffn_norm_g', 'w_up', 'ffn_conv_w', 'w_down']
TWIN_WEIGHTS = ['attn_norm_g', 'w_in', 'q_lat_norm_g', 'w_uq', 'kv_lat_norm_g', 'w_ukv', 'mla_q_norm_g', 'mla_k_norm_g', 'fox_q_norm_g', 'fox_k_norm_g', 'fox_f_bias', 's5_lambda_re', 's5_lambda_im', 's5_b_re', 's5_b_im', 's5_c_re', 's5_c_im', 's5_d', 's5_log_step', 's5_w_glu', 's5_b_glu', 'w_branch', 'w_out', 'ffn_norm_g', 'w_up', 'ffn_conv_w', 'w_down']
TWIN_DIFF_INPUT = 'x'
TWIN_INPUTS = ['x', 'positions', 'attn_norm_g', 'w_in', 'q_lat_norm_g', 'w_uq', 'kv_lat_norm_g', 'w_ukv', 'mla_q_norm_g', 'mla_k_norm_g', 'fox_q_norm_g', 'fox_k_norm_g', 'fox_f_bias', 's5_lambda_re', 's5_lambda_im', 's5_b_re', 's5_b_im', 's5_c_re', 's5_c_im', 's5_d', 's5_log_step', 's5_w_glu', 's5_b_glu', 'w_branch', 'w_out', 'ffn_norm_g', 'w_up', 'ffn_conv_w', 'w_down', 'loss_target', 'm_attn_norm_g', 'm_w_in', 'm_q_lat_norm_g', 'm_w_uq', 'm_kv_lat_norm_g', 'm_w_ukv', 'm_mla_q_norm_g', 'm_mla_k_norm_g', 'm_fox_q_norm_g', 'm_fox_k_norm_g', 'm_fox_f_bias', 'm_s5_lambda_re', 'm_s5_lambda_im', 'm_s5_b_re', 'm_s5_b_im', 'm_s5_c_re', 'm_s5_c_im', 'm_s5_d', 'm_s5_log_step', 'm_s5_w_glu', 'm_s5_b_glu', 'm_w_branch', 'm_w_out', 'm_ffn_norm_g', 'm_w_up', 'm_ffn_conv_w', 'm_w_down', 'v_attn_norm_g', 'v_w_in', 'v_q_lat_norm_g', 'v_w_uq', 'v_kv_lat_norm_g', 'v_w_ukv', 'v_mla_q_norm_g', 'v_mla_k_norm_g', 'v_fox_q_norm_g', 'v_fox_k_norm_g', 'v_fox_f_bias', 'v_s5_lambda_re', 'v_s5_lambda_im', 'v_s5_b_re', 'v_s5_b_im', 'v_s5_c_re', 'v_s5_c_im', 'v_s5_d', 'v_s5_log_step', 'v_s5_w_glu', 'v_s5_b_glu', 'v_w_branch', 'v_w_out', 'v_ffn_norm_g', 'v_w_up', 'v_ffn_conv_w', 'v_w_down']
TWIN_OUTPUTS = ['loss', 'grad_x', 'grad_attn_norm_g', 'grad_w_in', 'grad_q_lat_norm_g', 'grad_w_uq', 'grad_kv_lat_norm_g', 'grad_w_ukv', 'grad_mla_q_norm_g', 'grad_mla_k_norm_g', 'grad_fox_q_norm_g', 'grad_fox_k_norm_g', 'grad_fox_f_bias', 'grad_s5_lambda_re', 'grad_s5_lambda_im', 'grad_s5_b_re', 'grad_s5_b_im', 'grad_s5_c_re', 'grad_s5_c_im', 'grad_s5_d', 'grad_s5_log_step', 'grad_s5_w_glu', 'grad_s5_b_glu', 'grad_w_branch', 'grad_w_out', 'grad_ffn_norm_g', 'grad_w_up', 'grad_ffn_conv_w', 'grad_w_down', 'delta_attn_norm_g', 'delta_w_in', 'delta_q_lat_norm_g', 'delta_w_uq', 'delta_kv_lat_norm_g', 'delta_w_ukv', 'delta_mla_q_norm_g', 'delta_mla_k_norm_g', 'delta_fox_q_norm_g', 'delta_fox_k_norm_g', 'delta_fox_f_bias', 'delta_s5_lambda_re', 'delta_s5_lambda_im', 'delta_s5_b_re', 'delta_s5_b_im', 'delta_s5_c_re', 'delta_s5_c_im', 'delta_s5_d', 'delta_s5_log_step', 'delta_s5_w_glu', 'delta_s5_b_glu', 'delta_w_branch', 'delta_w_out', 'delta_ffn_norm_g', 'delta_w_up', 'delta_ffn_conv_w', 'delta_w_down', 'new_m_attn_norm_g', 'new_m_w_in', 'new_m_q_lat_norm_g', 'new_m_w_uq', 'new_m_kv_lat_norm_g', 'new_m_w_ukv', 'new_m_mla_q_norm_g', 'new_m_mla_k_norm_g', 'new_m_fox_q_norm_g', 'new_m_fox_k_norm_g', 'new_m_fox_f_bias', 'new_m_s5_lambda_re', 'new_m_s5_lambda_im', 'new_m_s5_b_re', 'new_m_s5_b_im', 'new_m_s5_c_re', 'new_m_s5_c_im', 'new_m_s5_d', 'new_m_s5_log_step', 'new_m_s5_w_glu', 'new_m_s5_b_glu', 'new_m_w_branch', 'new_m_w_out', 'new_m_ffn_norm_g', 'new_m_w_up', 'new_m_ffn_conv_w', 'new_m_w_down', 'new_v_attn_norm_g', 'new_v_w_in', 'new_v_q_lat_norm_g', 'new_v_w_uq', 'new_v_kv_lat_norm_g', 'new_v_w_ukv', 'new_v_mla_q_norm_g', 'new_v_mla_k_norm_g', 'new_v_fox_q_norm_g', 'new_v_fox_k_norm_g', 'new_v_fox_f_bias', 'new_v_s5_lambda_re', 'new_v_s5_lambda_im', 'new_v_s5_b_re', 'new_v_s5_b_im', 'new_v_s5_c_re', 'new_v_s5_c_im', 'new_v_s5_d', 'new_v_s5_log_step', 'new_v_s5_w_glu', 'new_v_s5_b_glu', 'new_v_w_branch', 'new_v_w_out', 'new_v_ffn_norm_g', 'new_v_w_up', 'new_v_ffn_conv_w', 'new_v_w_down']
TWIN_LEAF_KINDS = {'loss': 'loss', 'grad_x': 'grad_x', 'grad_attn_norm_g': 'grad_w', 'grad_w_in': 'grad_w', 'grad_q_lat_norm_g': 'grad_w', 'grad_w_uq': 'grad_w', 'grad_kv_lat_norm_g': 'grad_w', 'grad_w_ukv': 'grad_w', 'grad_mla_q_norm_g': 'grad_w', 'grad_mla_k_norm_g': 'grad_w', 'grad_fox_q_norm_g': 'grad_w', 'grad_fox_k_norm_g': 'grad_w', 'grad_fox_f_bias': 'grad_w', 'grad_s5_lambda_re': 'grad_w', 'grad_s5_lambda_im': 'grad_w', 'grad_s5_b_re': 'grad_w', 'grad_s5_b_im': 'grad_w', 'grad_s5_c_re': 'grad_w', 'grad_s5_c_im': 'grad_w', 'grad_s5_d': 'grad_w', 'grad_s5_log_step': 'grad_w', 'grad_s5_w_glu': 'grad_w', 'grad_s5_b_glu': 'grad_w', 'grad_w_branch': 'grad_w', 'grad_w_out': 'grad_w', 'grad_ffn_norm_g': 'grad_w', 'grad_w_up': 'grad_w', 'grad_ffn_conv_w': 'grad_w', 'grad_w_down': 'grad_w', 'delta_attn_norm_g': 'delta_w', 'delta_w_in': 'delta_w', 'delta_q_lat_norm_g': 'delta_w', 'delta_w_uq': 'delta_w', 'delta_kv_lat_norm_g': 'delta_w', 'delta_w_ukv': 'delta_w', 'delta_mla_q_norm_g': 'delta_w', 'delta_mla_k_norm_g': 'delta_w', 'delta_fox_q_norm_g': 'delta_w', 'delta_fox_k_norm_g': 'delta_w', 'delta_fox_f_bias': 'delta_w', 'delta_s5_lambda_re': 'delta_w', 'delta_s5_lambda_im': 'delta_w', 'delta_s5_b_re': 'delta_w', 'delta_s5_b_im': 'delta_w', 'delta_s5_c_re': 'delta_w', 'delta_s5_c_im': 'delta_w', 'delta_s5_d': 'delta_w', 'delta_s5_log_step': 'delta_w', 'delta_s5_w_glu': 'delta_w', 'delta_s5_b_glu': 'delta_w', 'delta_w_branch': 'delta_w', 'delta_w_out': 'delta_w', 'delta_ffn_norm_g': 'delta_w', 'delta_w_up': 'delta_w', 'delta_ffn_conv_w': 'delta_w', 'delta_w_down': 'delta_w', 'new_m_attn_norm_g': 'new_m', 'new_m_w_in': 'new_m', 'new_m_q_lat_norm_g': 'new_m', 'new_m_w_uq': 'new_m', 'new_m_kv_lat_norm_g': 'new_m', 'new_m_w_ukv': 'new_m', 'new_m_mla_q_norm_g': 'new_m', 'new_m_mla_k_norm_g': 'new_m', 'new_m_fox_q_norm_g': 'new_m', 'new_m_fox_k_norm_g': 'new_m', 'new_m_fox_f_bias': 'new_m', 'new_m_s5_lambda_re': 'new_m', 'new_m_s5_lambda_im': 'new_m', 'new_m_s5_b_re': 'new_m', 'new_m_s5_b_im': 'new_m', 'new_m_s5_c_re': 'new_m', 'new_m_s5_c_im': 'new_m', 'new_m_s5_d': 'new_m', 'new_m_s5_log_step': 'new_m', 'new_m_s5_w_glu': 'new_m', 'new_m_s5_b_glu': 'new_m', 'new_m_w_branch': 'new_m', 'new_m_w_out': 'new_m', 'new_m_ffn_norm_g': 'new_m', 'new_m_w_up': 'new_m', 'new_m_ffn_conv_w': 'new_m', 'new_m_w_down': 'new_m', 'new_v_attn_norm_g': 'new_v', 'new_v_w_in': 'new_v', 'new_v_q_lat_norm_g': 'new_v', 'new_v_w_uq': 'new_v', 'new_v_kv_lat_norm_g': 'new_v', 'new_v_w_ukv': 'new_v', 'new_v_mla_q_norm_g': 'new_v', 'new_v_mla_k_norm_g': 'new_v', 'new_v_fox_q_norm_g': 'new_v', 'new_v_fox_k_norm_g': 'new_v', 'new_v_fox_f_bias': 'new_v', 'new_v_s5_lambda_re': 'new_v', 'new_v_s5_lambda_im': 'new_v', 'new_v_s5_b_re': 'new_v', 'new_v_s5_b_im': 'new_v', 'new_v_s5_c_re': 'new_v', 'new_v_s5_c_im': 'new_v', 'new_v_s5_d': 'new_v', 'new_v_s5_log_step': 'new_v', 'new_v_s5_w_glu': 'new_v', 'new_v_s5_b_glu': 'new_v', 'new_v_w_branch': 'new_v', 'new_v_w_out': 'new_v', 'new_v_ffn_norm_g': 'new_v', 'new_v_w_up': 'new_v', 'new_v_ffn_conv_w': 'new_v', 'new_v_w_down': 'new_v'}


def _forward(args):
    return _fwd_reference(*[args[k] for k in FWD_PARAMS])


def _output_shape():
    out = _jax.eval_shape(lambda: _forward(_fwd_setup_inputs(0)))
    return out.shape, out.dtype

N_MICROBATCH = 1
ADAM_LR = 0.001
ADAM_B1 = 0.9
ADAM_B2 = 0.999
ADAM_EPS = 1e-08
ADAM_WD = 0.01
ADAM_STEP = 10
PER_EXAMPLE_BATCH_AXIS = {'x': 0, 'positions': 0, 'loss_target': 0}
SHARED_INPUTS = []
_WEIGHT_DTYPES = {'attn_norm_g': _jnp.float32, 'w_in': _jnp.float32, 'q_lat_norm_g': _jnp.float32, 'w_uq': _jnp.float32, 'kv_lat_norm_g': _jnp.float32, 'w_ukv': _jnp.float32, 'mla_q_norm_g': _jnp.float32, 'mla_k_norm_g': _jnp.float32, 'fox_q_norm_g': _jnp.float32, 'fox_k_norm_g': _jnp.float32, 'fox_f_bias': _jnp.float32, 's5_lambda_re': _jnp.float32, 's5_lambda_im': _jnp.float32, 's5_b_re': _jnp.float32, 's5_b_im': _jnp.float32, 's5_c_re': _jnp.float32, 's5_c_im': _jnp.float32, 's5_d': _jnp.float32, 's5_log_step': _jnp.float32, 's5_w_glu': _jnp.float32, 's5_b_glu': _jnp.float32, 'w_branch': _jnp.float32, 'w_out': _jnp.float32, 'ffn_norm_g': _jnp.float32, 'w_up': _jnp.float32, 'ffn_conv_w': _jnp.float32, 'w_down': _jnp.float32}
MOMENT_SCALE = {'attn_norm_g': 1.044045e+00, 'w_in': 1.549758e-01, 'q_lat_norm_g': 1.220903e-01, 'w_uq': 1.328098e-01, 'kv_lat_norm_g': 5.884804e-01, 'w_ukv': 2.507072e-01, 'mla_q_norm_g': 5.101660e-01, 'mla_k_norm_g': 5.084700e-01, 'fox_q_norm_g': 6.000707e+00, 'fox_k_norm_g': 5.977619e+00, 'fox_f_bias': 6.961536e+01, 's5_lambda_re': 2.624045e-02, 's5_lambda_im': 2.031457e-02, 's5_b_re': 1.349481e-02, 's5_b_im': 1.321467e-02, 's5_c_re': 2.676848e-02, 's5_c_im': 2.518502e-02, 's5_d': 2.707870e+00, 's5_log_step': 9.305171e+00, 's5_w_glu': 1.479226e-01, 's5_b_glu': 9.630421e-01, 'w_branch': 1.473964e-01, 'w_out': 2.578451e-01, 'ffn_norm_g': 3.212418e+01, 'w_up': 3.123481e-01, 'ffn_conv_w': 3.245652e+00, 'w_down': 4.827771e-01}


def _to_microbatches(a, axis):
    t = _jnp.moveaxis(a, axis, 0)
    t = t.reshape((N_MICROBATCH, t.shape[0] // N_MICROBATCH) + t.shape[1:])
    return _jnp.moveaxis(t, 1, axis + 1)


def setup_inputs(seed: int = 0) -> dict:
    inp = _fwd_setup_inputs(seed)
    key = _jax.random.fold_in(_jax.random.key(seed), 7919)
    shape, _ = _output_shape()
    out = dict(inp)
    out["loss_target"] = _jax.random.normal(_jax.random.fold_in(key, 0), shape, _jnp.float32)
    for i, name in enumerate(TWIN_WEIGHTS):
        w = inp[name].astype(_jnp.float32)
        if MOMENT_SCALE is None:
            s = _jnp.sqrt(_jnp.mean(_jnp.square(w)) + 1e-30)
        else:
            s = MOMENT_SCALE[name]
        km, kv = _jax.random.split(_jax.random.fold_in(key, i + 1))
        out[name] = w
        out["m_" + name] = s * _jax.random.normal(km, w.shape, _jnp.float32)
        out["v_" + name] = (s * s) * _jax.random.uniform(kv, w.shape, _jnp.float32, 0.5, 1.5)
    if N_MICROBATCH > 1:
        for name, axis in PER_EXAMPLE_BATCH_AXIS.items():
            out[name] = _to_microbatches(out[name], axis)
    return {'x': out['x'], 'positions': out['positions'], 'attn_norm_g': out['attn_norm_g'], 'w_in': out['w_in'], 'q_lat_norm_g': out['q_lat_norm_g'], 'w_uq': out['w_uq'], 'kv_lat_norm_g': out['kv_lat_norm_g'], 'w_ukv': out['w_ukv'], 'mla_q_norm_g': out['mla_q_norm_g'], 'mla_k_norm_g': out['mla_k_norm_g'], 'fox_q_norm_g': out['fox_q_norm_g'], 'fox_k_norm_g': out['fox_k_norm_g'], 'fox_f_bias': out['fox_f_bias'], 's5_lambda_re': out['s5_lambda_re'], 's5_lambda_im': out['s5_lambda_im'], 's5_b_re': out['s5_b_re'], 's5_b_im': out['s5_b_im'], 's5_c_re': out['s5_c_re'], 's5_c_im': out['s5_c_im'], 's5_d': out['s5_d'], 's5_log_step': out['s5_log_step'], 's5_w_glu': out['s5_w_glu'], 's5_b_glu': out['s5_b_glu'], 'w_branch': out['w_branch'], 'w_out': out['w_out'], 'ffn_norm_g': out['ffn_norm_g'], 'w_up': out['w_up'], 'ffn_conv_w': out['ffn_conv_w'], 'w_down': out['w_down'], 'loss_target': out['loss_target'], 'm_attn_norm_g': out['m_attn_norm_g'], 'm_w_in': out['m_w_in'], 'm_q_lat_norm_g': out['m_q_lat_norm_g'], 'm_w_uq': out['m_w_uq'], 'm_kv_lat_norm_g': out['m_kv_lat_norm_g'], 'm_w_ukv': out['m_w_ukv'], 'm_mla_q_norm_g': out['m_mla_q_norm_g'], 'm_mla_k_norm_g': out['m_mla_k_norm_g'], 'm_fox_q_norm_g': out['m_fox_q_norm_g'], 'm_fox_k_norm_g': out['m_fox_k_norm_g'], 'm_fox_f_bias': out['m_fox_f_bias'], 'm_s5_lambda_re': out['m_s5_lambda_re'], 'm_s5_lambda_im': out['m_s5_lambda_im'], 'm_s5_b_re': out['m_s5_b_re'], 'm_s5_b_im': out['m_s5_b_im'], 'm_s5_c_re': out['m_s5_c_re'], 'm_s5_c_im': out['m_s5_c_im'], 'm_s5_d': out['m_s5_d'], 'm_s5_log_step': out['m_s5_log_step'], 'm_s5_w_glu': out['m_s5_w_glu'], 'm_s5_b_glu': out['m_s5_b_glu'], 'm_w_branch': out['m_w_branch'], 'm_w_out': out['m_w_out'], 'm_ffn_norm_g': out['m_ffn_norm_g'], 'm_w_up': out['m_w_up'], 'm_ffn_conv_w': out['m_ffn_conv_w'], 'm_w_down': out['m_w_down'], 'v_attn_norm_g': out['v_attn_norm_g'], 'v_w_in': out['v_w_in'], 'v_q_lat_norm_g': out['v_q_lat_norm_g'], 'v_w_uq': out['v_w_uq'], 'v_kv_lat_norm_g': out['v_kv_lat_norm_g'], 'v_w_ukv': out['v_w_ukv'], 'v_mla_q_norm_g': out['v_mla_q_norm_g'], 'v_mla_k_norm_g': out['v_mla_k_norm_g'], 'v_fox_q_norm_g': out['v_fox_q_norm_g'], 'v_fox_k_norm_g': out['v_fox_k_norm_g'], 'v_fox_f_bias': out['v_fox_f_bias'], 'v_s5_lambda_re': out['v_s5_lambda_re'], 'v_s5_lambda_im': out['v_s5_lambda_im'], 'v_s5_b_re': out['v_s5_b_re'], 'v_s5_b_im': out['v_s5_b_im'], 'v_s5_c_re': out['v_s5_c_re'], 'v_s5_c_im': out['v_s5_c_im'], 'v_s5_d': out['v_s5_d'], 'v_s5_log_step': out['v_s5_log_step'], 'v_s5_w_glu': out['v_s5_w_glu'], 'v_s5_b_glu': out['v_s5_b_glu'], 'v_w_branch': out['v_w_branch'], 'v_w_out': out['v_w_out'], 'v_ffn_norm_g': out['v_ffn_norm_g'], 'v_w_up': out['v_w_up'], 'v_ffn_conv_w': out['v_ffn_conv_w'], 'v_w_down': out['v_w_down']}


def _loss(weights, diff, rest, loss_target):
    with _jax.named_scope("forward"):
        args = {**rest, TWIN_DIFF_INPUT: diff, **{k: w.astype(_WEIGHT_DTYPES[k]) for k, w in weights.items()}}
        y = _forward(args)
    with _jax.named_scope("loss_head"):
        err = _jnp.square(y.astype(_jnp.float32) - loss_target)
        return 0.5 * _jnp.sum(_jnp.mean(err, axis=-1)) if err.ndim else 0.5 * err


def _adamw(w, g, m, v):
    m = ADAM_B1 * m + (1.0 - ADAM_B1) * g
    v = ADAM_B2 * v + (1.0 - ADAM_B2) * _jnp.square(g)
    m_hat = m / (1.0 - ADAM_B1 ** ADAM_STEP)
    v_hat = v / (1.0 - ADAM_B2 ** ADAM_STEP)
    delta = -ADAM_LR * (m_hat / (_jnp.sqrt(v_hat) + ADAM_EPS) + ADAM_WD * w)
    return delta, m, v


def reference(x, positions, attn_norm_g, w_in, q_lat_norm_g, w_uq, kv_lat_norm_g, w_ukv, mla_q_norm_g, mla_k_norm_g, fox_q_norm_g, fox_k_norm_g, fox_f_bias, s5_lambda_re, s5_lambda_im, s5_b_re, s5_b_im, s5_c_re, s5_c_im, s5_d, s5_log_step, s5_w_glu, s5_b_glu, w_branch, w_out, ffn_norm_g, w_up, ffn_conv_w, w_down, loss_target, m_attn_norm_g, m_w_in, m_q_lat_norm_g, m_w_uq, m_kv_lat_norm_g, m_w_ukv, m_mla_q_norm_g, m_mla_k_norm_g, m_fox_q_norm_g, m_fox_k_norm_g, m_fox_f_bias, m_s5_lambda_re, m_s5_lambda_im, m_s5_b_re, m_s5_b_im, m_s5_c_re, m_s5_c_im, m_s5_d, m_s5_log_step, m_s5_w_glu, m_s5_b_glu, m_w_branch, m_w_out, m_ffn_norm_g, m_w_up, m_ffn_conv_w, m_w_down, v_attn_norm_g, v_w_in, v_q_lat_norm_g, v_w_uq, v_kv_lat_norm_g, v_w_ukv, v_mla_q_norm_g, v_mla_k_norm_g, v_fox_q_norm_g, v_fox_k_norm_g, v_fox_f_bias, v_s5_lambda_re, v_s5_lambda_im, v_s5_b_re, v_s5_b_im, v_s5_c_re, v_s5_c_im, v_s5_d, v_s5_log_step, v_s5_w_glu, v_s5_b_glu, v_w_branch, v_w_out, v_ffn_norm_g, v_w_up, v_ffn_conv_w, v_w_down):
    given = dict(x=x, positions=positions, attn_norm_g=attn_norm_g, w_in=w_in, q_lat_norm_g=q_lat_norm_g, w_uq=w_uq, kv_lat_norm_g=kv_lat_norm_g, w_ukv=w_ukv, mla_q_norm_g=mla_q_norm_g, mla_k_norm_g=mla_k_norm_g, fox_q_norm_g=fox_q_norm_g, fox_k_norm_g=fox_k_norm_g, fox_f_bias=fox_f_bias, s5_lambda_re=s5_lambda_re, s5_lambda_im=s5_lambda_im, s5_b_re=s5_b_re, s5_b_im=s5_b_im, s5_c_re=s5_c_re, s5_c_im=s5_c_im, s5_d=s5_d, s5_log_step=s5_log_step, s5_w_glu=s5_w_glu, s5_b_glu=s5_b_glu, w_branch=w_branch, w_out=w_out, ffn_norm_g=ffn_norm_g, w_up=w_up, ffn_conv_w=ffn_conv_w, w_down=w_down, loss_target=loss_target, m_attn_norm_g=m_attn_norm_g, m_w_in=m_w_in, m_q_lat_norm_g=m_q_lat_norm_g, m_w_uq=m_w_uq, m_kv_lat_norm_g=m_kv_lat_norm_g, m_w_ukv=m_w_ukv, m_mla_q_norm_g=m_mla_q_norm_g, m_mla_k_norm_g=m_mla_k_norm_g, m_fox_q_norm_g=m_fox_q_norm_g, m_fox_k_norm_g=m_fox_k_norm_g, m_fox_f_bias=m_fox_f_bias, m_s5_lambda_re=m_s5_lambda_re, m_s5_lambda_im=m_s5_lambda_im, m_s5_b_re=m_s5_b_re, m_s5_b_im=m_s5_b_im, m_s5_c_re=m_s5_c_re, m_s5_c_im=m_s5_c_im, m_s5_d=m_s5_d, m_s5_log_step=m_s5_log_step, m_s5_w_glu=m_s5_w_glu, m_s5_b_glu=m_s5_b_glu, m_w_branch=m_w_branch, m_w_out=m_w_out, m_ffn_norm_g=m_ffn_norm_g, m_w_up=m_w_up, m_ffn_conv_w=m_ffn_conv_w, m_w_down=m_w_down, v_attn_norm_g=v_attn_norm_g, v_w_in=v_w_in, v_q_lat_norm_g=v_q_lat_norm_g, v_w_uq=v_w_uq, v_kv_lat_norm_g=v_kv_lat_norm_g, v_w_ukv=v_w_ukv, v_mla_q_norm_g=v_mla_q_norm_g, v_mla_k_norm_g=v_mla_k_norm_g, v_fox_q_norm_g=v_fox_q_norm_g, v_fox_k_norm_g=v_fox_k_norm_g, v_fox_f_bias=v_fox_f_bias, v_s5_lambda_re=v_s5_lambda_re, v_s5_lambda_im=v_s5_lambda_im, v_s5_b_re=v_s5_b_re, v_s5_b_im=v_s5_b_im, v_s5_c_re=v_s5_c_re, v_s5_c_im=v_s5_c_im, v_s5_d=v_s5_d, v_s5_log_step=v_s5_log_step, v_s5_w_glu=v_s5_w_glu, v_s5_b_glu=v_s5_b_glu, v_w_branch=v_w_branch, v_w_out=v_w_out, v_ffn_norm_g=v_ffn_norm_g, v_w_up=v_w_up, v_ffn_conv_w=v_ffn_conv_w, v_w_down=v_w_down)
    weights = {n: given[n] for n in TWIN_WEIGHTS}
    shared = {n: given[n] for n in SHARED_INPUTS}
    per_example = {n: given[n] for n in ['x', 'positions']}
    grad_fn = _jax.value_and_grad(_loss, argnums=(0, 1))

    def one_microbatch(ex, loss_target):
        ex = dict(ex)
        diff = ex.pop(TWIN_DIFF_INPUT)
        return grad_fn(weights, diff, {**shared, **ex}, loss_target)

    if N_MICROBATCH == 1:
        loss, (grad_w, grad_x) = one_microbatch(per_example, given["loss_target"])
    else:
        def body(carry, xs):
            loss_sum, grad_sum = carry
            l_k, (gw_k, gx_k) = one_microbatch(xs[0], xs[1])
            with _jax.named_scope("update"):
                return (loss_sum + l_k, _jax.tree.map(_jnp.add, grad_sum, gw_k)), gx_k

        init = (_jnp.zeros((), _jnp.float32), _jax.tree.map(_jnp.zeros_like, weights))
        (loss, grad_w), grad_x = _jax.lax.scan(body, init, (per_example, given["loss_target"]))
    with _jax.named_scope("update"):
        delta_w, new_m, new_v = {}, {}, {}
        for n in TWIN_WEIGHTS:
            delta_w[n], new_m[n], new_v[n] = _adamw(weights[n], grad_w[n], given["m_" + n], given["v_" + n])
    return (loss, grad_x, *[grad_w[n] for n in TWIN_WEIGHTS], *[delta_w[n] for n in TWIN_WEIGHTS],
            *[new_m[n] for n in TWIN_WEIGHTS], *[new_v[n] for n in TWIN_WEIGHTS])
```

```python
import functools
import math

import numpy as np
import jax
import jax.numpy as jnp
from jax import lax
from jax.experimental import pallas as pl
from jax.experimental.pallas import tpu as pltpu

F32, BF16 = jnp.float32, jnp.bfloat16

D_MODEL = 1024
N_DEV = 8
MLA_Q_RANK, MLA_KV_RANK, MLA_ROPE_DIM, MLA_NOPE_DIM, MLA_QK_DIM = 384, 256, 32, 64, 96
N_HEADS, HEAD_V = 4, 64
S5_GROUPS, S5_GROUP_CH, S5_STATE, S5_WIDTH = 16, 16, 64, 256
S5_N = S5_GROUPS * S5_STATE
D_FF = 2816
D_IN = 4772
ROPE_THETA = 10000.0
NORM_EPS = 1e-6
NEG_INF = -1e30
ADAM_LR, ADAM_B1, ADAM_B2, ADAM_EPS, ADAM_WD, ADAM_STEP = 0.001, 0.9, 0.999, 1e-08, 0.01, 10

VMEM_LIMIT_BYTES = 56 * 1024 * 1024
LANES = 128
TM = 256
TQ = 256
HP = 128
QKW = N_HEADS * HP

PW = 5120
PA = 2048
IN_SEGS = ((0, 384, 256),
           (256, 672, 256),
           (512, 928, 256),
           (768, 1184, 256),
           (1024, 1444, 256),
           (1280, 640, 32),
           (1408, 1440, 4),
           (1536, 0, 384),
           (2048, 1700, 3072))
O_CKV, O_FQ, O_FK, O_FV, O_U, O_KR, O_FL, O_CQ = 0, 256, 512, 768, 1024, 1280, 1408, 1536

FF_BLK = 384
FF_HALF = D_FF // 8
FF_PW = 8 * FF_BLK
UP_W = 2 * FF_PW

NN = (((1,), (0,)), ((), ()))
NT = (((1,), (1,)), ((), ()))
TN = (((0,), (0,)), ((), ()))


def _pcall(body, *, name, grid, in_specs, out_specs, out_shape, scratch_shapes=(), aliases=None):
    params = pltpu.CompilerParams(dimension_semantics=("arbitrary",) * len(grid), vmem_limit_bytes=VMEM_LIMIT_BYTES)
    return pl.pallas_call(body, name=name, grid=grid, in_specs=in_specs, out_specs=out_specs, out_shape=out_shape,
                          scratch_shapes=list(scratch_shapes), input_output_aliases=aliases or {},
                          compiler_params=params)


def _tile_call(fn, name, grid, ins, outs):
    n_in = len(ins)

    def body(*refs):
        res = fn(*[r[...] for r in refs[:n_in]])
        if not isinstance(res, (tuple, list)):
            res = (res,)
        assert len(res) == len(outs), (name, len(res), len(outs))
        for r, v, o in zip(refs[n_in:], res, outs):
            v = v.astype(r.dtype)
            if o[4] is None:
                r[...] = v
            else:
                first = functools.reduce(jnp.logical_and, [pl.program_id(a) == 0 for a in o[4]])

                @pl.when(first)
                def _():
                    r[...] = v

                @pl.when(jnp.logical_not(first))
                def _():
                    r[...] += v

    res = _pcall(body, name=name, grid=grid,
                 in_specs=[pl.BlockSpec(b, m) for _, b, m in ins],
                 out_specs=[pl.BlockSpec(o[2], o[3]) for o in outs],
                 out_shape=[jax.ShapeDtypeStruct(o[0], o[1]) for o in outs])(*[a for a, _, _ in ins])
    return res


def _row(a, width=None, col_block=0, tm=TM):
    w = a.shape[1] if width is None else width
    return (a, (tm, w), lambda i, c=col_block: (i, c))


def _par(a):
    nd = a.ndim
    return (a, a.shape, lambda i, nd=nd: (0,) * nd)


def _orow(t, w, dtype=F32, tm=TM):
    return ((t, w), dtype, (tm, w), lambda i: (i, 0), None)


def _oacc(shape):
    nd = len(shape)
    return (tuple(shape), F32, tuple(shape), lambda i, nd=nd: (0,) * nd, (0,))


def _pick(n, target):
    best = None
    for d in range(LANES, min(n, target) + 1, LANES):
        if n % d == 0:
            best = d
    return n if best is None else best


def _mm(a, b, mode, name, tm=512, tn=2048, tk=1024):
    if mode == "nn":
        (m, k), (_, n) = a.shape, b.shape
    elif mode == "nt":
        (m, k), (n, _) = a.shape, b.shape
    else:
        (k, m), (_, n) = a.shape, b.shape
    tm, tn, tk = _pick(m, tm), _pick(n, tn), _pick(k, tk)
    nk = k // tk
    dims = {"nn": NN, "nt": NT, "tn": TN}[mode]
    a_spec = (pl.BlockSpec((tk, tm), lambda i, j, l: (l, i)) if mode == "tn"
              else pl.BlockSpec((tm, tk), lambda i, j, l: (i, l)))
    b_spec = (pl.BlockSpec((tn, tk), lambda i, j, l: (j, l)) if mode == "nt"
              else pl.BlockSpec((tk, tn), lambda i, j, l: (l, j)))

    def body(a_ref, b_ref, o_ref, acc_ref):
        part = lax.dot_general(a_ref[...].astype(BF16), b_ref[...].astype(BF16), dims, preferred_element_type=F32)
        if nk == 1:
            o_ref[...] = part
        else:
            l = pl.program_id(2)

            @pl.when(l == 0)
            def _():
                acc_ref[...] = part

            @pl.when(l > 0)
            def _():
                acc_ref[...] += part

            @pl.when(l == nk - 1)
            def _():
                o_ref[...] = acc_ref[...]

    return _pcall(body, name=name, grid=(m // tm, n // tn, nk), in_specs=[a_spec, b_spec],
                  out_specs=pl.BlockSpec((tm, tn), lambda i, j, l: (i, j)),
                  out_shape=jax.ShapeDtypeStruct((m, n), F32),
                  scratch_shapes=[pltpu.VMEM((tm, tn) if nk > 1 else (8, LANES), F32)])(a, b)


def _dot(x, w, dims):
    return lax.dot_general(x.astype(BF16), w.astype(BF16), dims, preferred_element_type=F32)


@jax.custom_vjp
def bdot(x, w):
    return _dot(x, w, NN)


def _bdot_fwd(x, w):
    return _dot(x, w, NN), (x, w)


def _bdot_bwd(res, g):
    x, w = res
    return _dot(g, w, NT), _dot(x, g, TN)


bdot.defvjp(_bdot_fwd, _bdot_bwd)


def _split3(x):
    x1 = x.astype(BF16)
    r1 = x - x1.astype(F32)
    x2 = r1.astype(BF16)
    x3 = (r1 - x2.astype(F32)).astype(BF16)
    return x1, x2, x3


def _cdot(x, m, dims):
    return sum(lax.dot_general(p, m, dims, preferred_element_type=F32) for p in _split3(x))


@jax.custom_vjp
def cdot(x, m):
    return _cdot(x, m, NN)


def _cdot_fwd(x, m):
    return _cdot(x, m, NN), m


def _cdot_bwd(m, g):
    return _cdot(g, m, NT), jnp.zeros_like(m)


cdot.defvjp(_cdot_fwd, _cdot_bwd)


def _shift_rows_impl(x, s, reverse):
    n = x.shape[0]
    idx = lax.broadcasted_iota(jnp.int32, x.shape, 0)
    if reverse:
        return jnp.where(idx < n - s, pltpu.roll(x, n - s, 0), 0.0)
    return jnp.where(idx >= s, pltpu.roll(x, s, 0), 0.0)


@functools.partial(jax.custom_vjp, nondiff_argnums=(1,))
def shift_rows(x, s):
    return _shift_rows_impl(x, s, False)


def _shift_rows_fwd(x, s):
    return _shift_rows_impl(x, s, False), None


def _shift_rows_bwd(s, _, g):
    return (_shift_rows_impl(g, s, True),)


shift_rows.defvjp(_shift_rows_fwd, _shift_rows_bwd)


def _rms(x, g):
    return x * lax.rsqrt(jnp.mean(x * x, axis=-1, keepdims=True) + NORM_EPS) * g


def _head_rms(x, e_mat, inv_n, g):
    ms = cdot(x * x, e_mat) * inv_n
    return x * lax.rsqrt(ms + NORM_EPS) * g


def _rope(y, p_mat, cos, sin):
    return y * cos + cdot(y, p_mat) * sin


def _constants():
    e = np.zeros((QKW, QKW), np.float32)
    p = np.zeros((QKW, QKW), np.float32)
    r = np.zeros((LANES, QKW), np.float32)
    pl64 = np.zeros((N_HEADS * HEAD_V, QKW), np.float32)
    half = MLA_ROPE_DIM // 2
    for h in range(N_HEADS):
        e[h * HP:(h + 1) * HP, h * HP:(h + 1) * HP] = 1.0
        for i in range(half):
            x1, x2 = h * HP + MLA_NOPE_DIM + i, h * HP + MLA_NOPE_DIM + half + i
            p[x2, x1] = -1.0
            p[x1, x2] = 1.0
        for i in range(MLA_ROPE_DIM):
            r[i, h * HP + MLA_NOPE_DIM + i] = 1.0
        for i in range(HEAD_V):
            pl64[h * HEAD_V + i, h * HP + i] = 1.0
    return tuple(jnp.asarray(a, BF16) for a in (e, p, r, pl64))


def _inv_freq_row():
    inv = ROPE_THETA ** (-jnp.arange(0, MLA_ROPE_DIM, 2, dtype=F32) / MLA_ROPE_DIM)
    head = jnp.concatenate([jnp.zeros((MLA_NOPE_DIM,), F32), inv, inv, jnp.zeros((HP - MLA_QK_DIM,), F32)])
    return jnp.tile(head, N_HEADS).reshape(1, QKW)


def _mixer_pre(c_q, c_kv, kr, fq, fk, qlg, w_uq, kvlg, w_uk, w_uv, mqg, mkg, fqg, fkg, cos, sin, e_mat, p_mat, r_mat, pl_mat):
    q = bdot(_rms(c_q, qlg), w_uq)
    ckvn = _rms(c_kv, kvlg)
    k = bdot(ckvn, w_uk) + cdot(kr, r_mat)
    v = bdot(ckvn, w_uv)
    q = _rope(_head_rms(q, e_mat, 1.0 / MLA_QK_DIM, mqg), p_mat, cos, sin)
    k = _rope(_head_rms(k, e_mat, 1.0 / MLA_QK_DIM, mkg), p_mat, cos, sin)
    qf = _head_rms(cdot(fq, pl_mat), e_mat, 1.0 / HEAD_V, fqg)
    kf = _head_rms(cdot(fk, pl_mat), e_mat, 1.0 / HEAD_V, fkg)
    return q, k, v, qf, kf


def _s5_post(state, u, c_blk, d_row, w_glu, b_glu):
    y = bdot(state, c_blk) + d_row * u
    y = jax.nn.gelu(y)
    return y * jax.nn.sigmoid(bdot(y, w_glu) + b_glu)


def _merge(o_a, o_b, o_c, g_a, g_b, g_c, w_a, w_b, w_c):
    return (jax.nn.sigmoid(g_a) * bdot(o_a, w_a) + jax.nn.sigmoid(g_b) * bdot(o_b, w_b)
            + jax.nn.sigmoid(g_c) * bdot(o_c, w_c))


def _ffn_mid(gate, val, wg0, wg1, wg2, wv0, wv1, wv2):
    cg = wg0 * shift_rows(gate, 2) + wg1 * shift_rows(gate, 1) + wg2 * gate
    cv = wv0 * shift_rows(val, 2) + wv1 * shift_rows(val, 1) + wv2 * val
    return jax.nn.silu(cg) * cv


def _s5_params(lam_re, lam_im, log_step, b_re, b_im):
    step = jnp.exp(log_step)
    zr, zi = lam_re * step, lam_im * step
    mag = jnp.exp(zr)
    lr, li = mag * jnp.cos(zi), mag * jnp.sin(zi)
    nr, ni = lr - 1.0, li
    den = lam_re * lam_re + lam_im * lam_im
    cr = (nr * lam_re + ni * lam_im) / den
    ci = (ni * lam_re - nr * lam_im) / den
    return lr, li, cr * b_re - ci * b_im, cr * b_im + ci * b_re


def _adamw(w, g, m, v):
    m = ADAM_B1 * m + (1.0 - ADAM_B1) * g
    v = ADAM_B2 * v + (1.0 - ADAM_B2) * (g * g)
    m_hat = m / (1.0 - ADAM_B1 ** ADAM_STEP)
    v_hat = v / (1.0 - ADAM_B2 ** ADAM_STEP)
    delta = -ADAM_LR * (m_hat / (jnp.sqrt(v_hat) + ADAM_EPS) + ADAM_WD * w)
    return delta, m, v


SCAN_W = 256


def _scan_seq(xr, xi, ar, ai, reverse):
    n = xr.shape[0]
    s = 1
    while s < n:
        sr, si = _shift_rows_impl(xr, s, reverse), _shift_rows_impl(xi, s, reverse)
        xr, xi = xr + ar * sr - ai * si, xi + ar * si + ai * sr
        ar, ai = ar * ar - ai * ai, 2.0 * ar * ai
        s *= 2
    return xr, xi


def _scan_fwd(bu, lbar, bsz, name):
    t = bu.shape[0]
    seq = t // bsz
    nb = bu.shape[1] // (2 * SCAN_W)

    def fn(b, a):
        xr, xi = _scan_seq(b[:, :SCAN_W], b[:, SCAN_W:], a[:, :SCAN_W], a[:, SCAN_W:], False)
        return jnp.concatenate([xr, xi], axis=1)

    blk = (seq, 2 * SCAN_W)
    return _tile_call(fn, name, (bsz, nb),
                      [(bu, blk, lambda i, j: (i, j)), (lbar, (1, 2 * SCAN_W), lambda i, j: (0, j))],
                      [((t, bu.shape[1]), F32, blk, lambda i, j: (i, j), None)])[0]


def _scan_bwd(dstate, state, lbar, bsz, name):
    t = state.shape[0]
    seq = t // bsz
    nb = state.shape[1] // (2 * SCAN_W)

    def fn(g, x, a):
        ar, ai = a[:, :SCAN_W], a[:, SCAN_W:]
        gr, gi = _scan_seq(g[:, :SCAN_W], g[:, SCAN_W:], ar, -ai, True)
        pr, pi = _shift_rows_impl(x[:, :SCAN_W], 1, False), _shift_rows_impl(x[:, SCAN_W:], 1, False)
        dar = jnp.sum(gr * pr + gi * pi, axis=0, keepdims=True)
        dai = jnp.sum(gi * pr - gr * pi, axis=0, keepdims=True)
        return jnp.concatenate([gr, gi], axis=1), jnp.concatenate([dar, dai], axis=1)

    blk = (seq, 2 * SCAN_W)
    return _tile_call(fn, name, (nb, bsz),
                      [(dstate, blk, lambda j, i: (i, j)), (state, blk, lambda j, i: (i, j)),
                       (lbar, (1, 2 * SCAN_W), lambda j, i: (0, j))],
                      [((t, state.shape[1]), F32, blk, lambda j, i: (i, j), None),
                       ((1, state.shape[1]), F32, (1, 2 * SCAN_W), lambda j, i: (0, j), (1,))])


def _shift_lanes(x, s, reverse):
    n = x.shape[1]
    idx = lax.broadcasted_iota(jnp.int32, x.shape, 1)
    if reverse:
        return jnp.where(idx < n - s, pltpu.roll(x, n - s, 1), 0.0)
    return jnp.where(idx >= s, pltpu.roll(x, s, 1), 0.0)


def _cumsum_lanes(x, reverse):
    s = 1
    while s < x.shape[1]:
        x = x + _shift_lanes(x, s, reverse)
        s *= 2
    return x


def _forget_fwd(z, bias, name):
    def fn(zv, bv):
        x = zv + bv
        logf = jnp.minimum(x, 0.0) - jnp.log(1.0 + jnp.exp(-jnp.abs(x)))
        return _cumsum_lanes(logf, False)

    return _tile_call(fn, name, (1,), [_par(z), _par(bias)],
                      [(z.shape, F32, z.shape, lambda i: (0, 0), None)])[0]


def _forget_bwd(dc, z, bias, name):
    def fn(dcv, zv, bv):
        x = zv + bv
        dlogf = _cumsum_lanes(dcv, True)
        dz = dlogf * jax.nn.sigmoid(-x)
        return dz, jnp.sum(dz, axis=1, keepdims=True)

    rows = z.shape[0]
    return _tile_call(fn, name, (1,), [_par(dc), _par(z), _par(bias)],
                      [(z.shape, F32, z.shape, lambda i: (0, 0), None),
                       ((rows, 1), F32, (rows, 1), lambda i: (0, 0), None)])


def _col(v2, x):
    idx = lax.broadcasted_iota(jnp.int32, v2.shape, 1)
    return jnp.sum(jnp.where(idx == x, v2, 0.0), axis=1, keepdims=True)


def _two_cols(c0, c1):
    idx = lax.broadcasted_iota(jnp.int32, (c0.shape[0], 2), 1)
    return jnp.where(idx == 0, c0, c1)


def _attn_fwd(q, k, v, v_blk0, cq, ck, scale, name):
    bsz, seq, _ = q.shape
    nq = seq // TQ
    has_bias = cq is not None

    def body(*refs):
        if has_bias:
            q_ref, k_ref, v_ref, cq_ref, ck_ref, o_ref, lse_ref = refs
        else:
            q_ref, k_ref, v_ref, o_ref, lse_ref = refs
        qi = pl.program_id(2)
        rows = qi * TQ + lax.broadcasted_iota(jnp.int32, (TQ, TQ), 0)
        cols0 = lax.broadcasted_iota(jnp.int32, (TQ, TQ), 1)
        lane = lax.broadcasted_iota(jnp.int32, (TQ, LANES), 1)
        outs, lses = [], []
        for x in range(2):
            qx = q_ref[:, x * HP:(x + 1) * HP].astype(BF16)
            cqx = _col(cq_ref[...], x) if has_bias else None

            def step(j, carry, x=x, qx=qx, cqx=cqx):
                m, l, acc = carry
                ks = pl.multiple_of(j * TQ, TQ)
                kx = k_ref[pl.ds(ks, TQ), x * HP:(x + 1) * HP].astype(BF16)
                s = lax.dot_general(qx, kx, NT, preferred_element_type=F32) * scale
                if has_bias:
                    s = s + (cqx - ck_ref[x:x + 1, pl.ds(ks, TQ)])
                s = jnp.where(rows >= cols0 + ks, s, NEG_INF)
                m_new = jnp.maximum(m, jnp.max(s, axis=1, keepdims=True))
                alpha = jnp.exp(m - m_new)
                p = jnp.exp(s - m_new)
                l = alpha * l + jnp.sum(p, axis=1, keepdims=True)
                acc = alpha * acc + lax.dot_general(p.astype(BF16), v_ref[pl.ds(ks, TQ), :].astype(BF16), NN,
                                                    preferred_element_type=F32)
                return m_new, l, acc

            init = (jnp.full((TQ, 1), NEG_INF, F32), jnp.zeros((TQ, 1), F32), jnp.zeros((TQ, LANES), F32))
            m, l, acc = lax.fori_loop(0, qi + 1, step, init)
            outs.append(acc / l)
            lses.append(m + jnp.log(l))
        o_ref[...] = jnp.where(lane < HEAD_V, outs[0], outs[1])
        lse_ref[...] = _two_cols(lses[0], lses[1])

    in_specs = [pl.BlockSpec((None, TQ, 2 * HP), lambda b, p, i: (b, i, p)),
                pl.BlockSpec((None, seq, 2 * HP), lambda b, p, i: (b, 0, p)),
                pl.BlockSpec((None, seq, LANES), lambda b, p, i: (b, 0, v_blk0 + p))]
    args = [q, k, v]
    if has_bias:
        in_specs += [pl.BlockSpec((None, None, TQ, 2), lambda b, p, i: (b, p, i, 0)),
                     pl.BlockSpec((None, None, 2, seq), lambda b, p, i: (b, p, 0, 0))]
        args += [cq, ck]
    return _pcall(body, name=name, grid=(bsz, 2, nq), in_specs=in_specs,
                  out_specs=[pl.BlockSpec((None, TQ, LANES), lambda b, p, i: (b, i, p)),
                             pl.BlockSpec((None, None, TQ, 2), lambda b, p, i: (b, p, i, 0))],
                  out_shape=[jax.ShapeDtypeStruct((bsz, seq, 2 * LANES), F32),
                             jax.ShapeDtypeStruct((bsz, 2, seq, 2), F32)])(*args)


def _attn_bwd(q, k, v, v_blk0, o, do, lse, cq, ck, scale, name):
    bsz, seq, _ = q.shape
    nq = seq // TQ
    has_bias = cq is not None

    def body(*refs):
        if has_bias:
            q_ref, k_ref, v_ref, o_ref, do_ref, lse_ref, cq_ref, ck_ref, dq_ref, dk_ref, dv_ref, dcq_ref, dck_ref = refs
        else:
            q_ref, k_ref, v_ref, o_ref, do_ref, lse_ref, dq_ref, dk_ref, dv_ref = refs
        qi = pl.program_id(2)

        @pl.when(qi == 0)
        def _():
            dk_ref[...] = jnp.zeros_like(dk_ref)
            dv_ref[...] = jnp.zeros_like(dv_ref)
            if has_bias:
                dck_ref[...] = jnp.zeros_like(dck_ref)

        rows = qi * TQ + lax.broadcasted_iota(jnp.int32, (TQ, TQ), 0)
        cols0 = lax.broadcasted_iota(jnp.int32, (TQ, TQ), 1)
        lane = lax.broadcasted_iota(jnp.int32, (TQ, LANES), 1)
        do = do_ref[...]
        prod = do * o_ref[...]
        lse2 = lse_ref[...]
        dcqs = []
        for x in range(2):
            hm = jnp.logical_and(lane >= x * HEAD_V, lane < (x + 1) * HEAD_V)
            dox = jnp.where(hm, do, 0.0).astype(BF16)
            delta = jnp.sum(jnp.where(hm, prod, 0.0), axis=1, keepdims=True)
            qx = q_ref[:, x * HP:(x + 1) * HP].astype(BF16)
            lsex = _col(lse2, x)
            cqx = _col(cq_ref[...], x) if has_bias else None

            def step(j, carry, x=x, dox=dox, delta=delta, qx=qx, lsex=lsex, cqx=cqx):
                dq_acc, dcq_acc = carry
                ks = pl.multiple_of(j * TQ, TQ)
                kx = k_ref[pl.ds(ks, TQ), x * HP:(x + 1) * HP].astype(BF16)
                vb = v_ref[pl.ds(ks, TQ), :].astype(BF16)
                s = lax.dot_general(qx, kx, NT, preferred_element_type=F32) * scale
                if has_bias:
                    s = s + (cqx - ck_ref[x:x + 1, pl.ds(ks, TQ)])
                s = jnp.where(rows >= cols0 + ks, s, NEG_INF)
                p = jnp.exp(s - lsex)
                dp = lax.dot_general(dox, vb, NT, preferred_element_type=F32)
                ds = p * (dp - delta)
                dsb = ds.astype(BF16)
                dq_acc = dq_acc + lax.dot_general(dsb, kx, NN, preferred_element_type=F32) * scale
                dk_ref[pl.ds(ks, TQ), x * HP:(x + 1) * HP] += (
                    lax.dot_general(dsb, qx, TN, preferred_element_type=F32) * scale)
                dv_ref[pl.ds(ks, TQ), :] += lax.dot_general(p.astype(BF16), dox, TN, preferred_element_type=F32)
                if has_bias:
                    dcq_acc = dcq_acc + jnp.sum(ds, axis=1, keepdims=True)
                    dck_ref[x:x + 1, pl.ds(ks, TQ)] -= jnp.sum(ds, axis=0, keepdims=True)
                return dq_acc, dcq_acc

            dq_x, dcq_x = lax.fori_loop(0, qi + 1, step, (jnp.zeros((TQ, HP), F32), jnp.zeros((TQ, 1), F32)))
            dq_ref[:, x * HP:(x + 1) * HP] = dq_x
            dcqs.append(dcq_x)
        if has_bias:
            dcq_ref[...] = _two_cols(dcqs[0], dcqs[1])

    qblk = pl.BlockSpec((None, TQ, 2 * HP), lambda b, p, i: (b, i, p))
    kblk = pl.BlockSpec((None, seq, 2 * HP), lambda b, p, i: (b, 0, p))
    oblk = pl.BlockSpec((None, TQ, LANES), lambda b, p, i: (b, i, p))
    cqblk = pl.BlockSpec((None, None, TQ, 2), lambda b, p, i: (b, p, i, 0))
    ckblk = pl.BlockSpec((None, None, 2, seq), lambda b, p, i: (b, p, 0, 0))
    in_specs = [qblk, kblk, pl.BlockSpec((None, seq, LANES), lambda b, p, i: (b, 0, v_blk0 + p)), oblk, oblk, cqblk]
    out_specs = [qblk, kblk, pl.BlockSpec((None, seq, LANES), lambda b, p, i: (b, 0, p))]
    out_shape = [jax.ShapeDtypeStruct((bsz, seq, QKW), F32), jax.ShapeDtypeStruct((bsz, seq, QKW), F32),
                 jax.ShapeDtypeStruct((bsz, seq, 2 * LANES), F32)]
    args = [q, k, v, o, do, lse]
    if has_bias:
        in_specs += [cqblk, ckblk]
        out_specs += [cqblk, ckblk]
        out_shape += [jax.ShapeDtypeStruct((bsz, 2, seq, 2), F32), jax.ShapeDtypeStruct((bsz, 2, 2, seq), F32)]
        args += [cq, ck]
    return _pcall(body, name=name, grid=(bsz, 2, nq), in_specs=in_specs, out_specs=out_specs,
                  out_shape=out_shape)(*args)


def _coords(idx):
    return (idx // 4, (idx // 2) % 2, idx % 2)


def _exchange(name, srcs, out_shapes, plan):
    n = len(srcs)
    me0 = jnp.int32(0)
    n_pieces = [len(plan(a, me0, me0)) for a in range(n)]
    offs = np.concatenate([[0], np.cumsum(n_pieces)]).astype(int)
    total = int(offs[-1])

    def body(*refs):
        src_refs, out_refs = refs[:n], refs[n:2 * n]
        send_sems, recv_sems, local_sems = refs[2 * n:]
        me = 4 * lax.axis_index("x") + 2 * lax.axis_index("y") + lax.axis_index("c")
        started = []
        for a in range(n):
            for pi, (si, di) in enumerate(plan(a, me, me)):
                cp = pltpu.make_async_copy(src_refs[a].at[si], out_refs[a].at[di], local_sems.at[offs[a] + pi])
                cp.start()
                started.append((cp, None))
        for kk in range(1, N_DEV):
            dest = (me + kk) % N_DEV
            src_dev = (me + N_DEV - kk) % N_DEV
            for a in range(n):
                pieces = plan(a, me, dest)
                landing = plan(a, src_dev, me)
                for pi, ((si, di), (_, li)) in enumerate(zip(pieces, landing)):
                    sem = (kk - 1) * total + offs[a] + pi
                    cp = pltpu.make_async_remote_copy(src_ref=src_refs[a].at[si], dst_ref=out_refs[a].at[di],
                                                      send_sem=send_sems.at[sem], recv_sem=recv_sems.at[sem],
                                                      device_id=_coords(dest), device_id_type=pl.DeviceIdType.MESH)
                    cp.start()
                    rc = pltpu.make_async_remote_copy(src_ref=src_refs[a].at[si], dst_ref=out_refs[a].at[li],
                                                      send_sem=send_sems.at[sem], recv_sem=recv_sems.at[sem],
                                                      device_id=_coords(dest), device_id_type=pl.DeviceIdType.MESH)
                    started.append((cp, rc))
        for cp, rc in started:
            if rc is None:
                cp.wait()
            else:
                cp.wait_send()
                rc.wait_recv()

    any_spec = pl.BlockSpec(memory_space=pl.ANY)
    return pl.pallas_call(body, name=name, in_specs=[any_spec] * n, out_specs=[any_spec] * n,
        out_shape=[jax.ShapeDtypeStruct(s, d) for s, d in out_shapes],
        scratch_shapes=[pltpu.SemaphoreType.DMA(((N_DEV - 1) * total,)), pltpu.SemaphoreType.DMA(((N_DEV - 1) * total,)),
                        pltpu.SemaphoreType.DMA((total,))],
        compiler_params=pltpu.CompilerParams(has_side_effects=True))(*srcs)


def _rows_of(j, n):
    return pl.ds(pl.multiple_of(j * n, n), n)


def _ffn_cols(j, half):
    return pl.multiple_of((2 * (j % 4) + half) * (2 * FF_BLK) + (j // 4) * FF_BLK, FF_BLK)


def _gather_plan(shards):
    names = list(shards)

    def plan(a, src_dev, _dest):
        nm = names[a]
        j = src_dev
        if nm in ("w_in", "w_uq", "w_ukv", "w_glu", "w_out", "w_down"):
            rows = shards[nm].shape[0]
            return [((slice(None), slice(None)), (_rows_of(j, rows), slice(None)))]
        if nm == "w_branch":
            return [((slice(None), slice(None), slice(None)), (slice(None), slice(None), _rows_of(j, LANES)))]
        if nm in ("w_up", "conv"):
            return [((slice(None), pl.ds(h * FF_BLK, FF_BLK)), (slice(None), pl.ds(_ffn_cols(j, h), FF_BLK)))
                    for h in range(2)]
        raise KeyError(nm)

    return names, plan


def _scatter_plan(names, shard_shapes):
    def plan(a, src_dev, dest):
        nm = names[a]
        j = dest
        if nm in ("w_in", "w_uq", "w_ukv", "w_glu", "w_out", "w_down"):
            rows = shard_shapes[nm][0]
            return [((_rows_of(j, rows), slice(None)), (src_dev, slice(None), slice(None)))]
        if nm == "w_branch":
            return [((slice(None), slice(None), _rows_of(j, LANES)), (src_dev, slice(None), slice(None), slice(None)))]
        if nm in ("w_up", "conv"):
            return [((slice(None), pl.ds(_ffn_cols(j, h), FF_BLK)), (src_dev, slice(None), pl.ds(h * FF_BLK, FF_BLK)))
                    for h in range(2)]
        if nm == "small":
            return [((slice(None), slice(None)), (src_dev, slice(None), slice(None)))]
        raise KeyError(nm)

    return plan


def _pad_in_cols(w):
    lead = w.shape[:-1]
    parts, pos = [], 0
    for dst, src, width in IN_SEGS:
        if dst > pos:
            parts.append(jnp.zeros(lead + (dst - pos,), w.dtype))
        parts.append(w[..., src:src + width])
        pos = dst + width
    return jnp.concatenate(parts, axis=-1)


def _unpad_in_cols(w):
    order = sorted(IN_SEGS, key=lambda s: s[1])
    return jnp.concatenate([w[..., dst:dst + width] for dst, _, width in order], axis=-1)


def _pad_ff(w, axis):
    n = w.shape[axis] // FF_HALF
    parts = []
    for h in range(n):
        piece = lax.slice_in_dim(w, h * FF_HALF, (h + 1) * FF_HALF, axis=axis)
        zshape = list(w.shape)
        zshape[axis] = FF_BLK - FF_HALF
        parts += [piece, jnp.zeros(zshape, w.dtype)]
    return jnp.concatenate(parts, axis=axis)


def _unpad_ff(w, axis):
    n = w.shape[axis] // FF_BLK
    return jnp.concatenate([lax.slice_in_dim(w, h * FF_BLK, h * FF_BLK + FF_HALF, axis=axis) for h in range(n)],
                           axis=axis)


def _s5_perm():
    i = np.arange(S5_N)
    re = (i // SCAN_W) * (2 * SCAN_W) + (i % SCAN_W)
    return re, re + SCAN_W


def _to_scan_cols(re_part, im_part):
    lead = re_part.shape[:-1]
    nb = S5_N // SCAN_W
    r = re_part.reshape(lead + (nb, SCAN_W))
    i = im_part.reshape(lead + (nb, SCAN_W))
    return jnp.concatenate([r, i], axis=-1).reshape(lead + (2 * S5_N,))


def _from_scan_cols(x):
    lead = x.shape[:-1]
    nb = S5_N // SCAN_W
    y = x.reshape(lead + (nb, 2, SCAN_W))
    return y[..., 0, :].reshape(lead + (S5_N,)), y[..., 1, :].reshape(lead + (S5_N,))


def _block_diag_in(b):
    eye = jnp.eye(S5_GROUPS, dtype=b.dtype)
    return jnp.einsum("gph,gk->ghkp", b, eye).reshape(S5_WIDTH, S5_N)


def _block_diag_in_t(m):
    eye = jnp.eye(S5_GROUPS, dtype=m.dtype)
    return jnp.einsum("ghkp,gk->gph", m.reshape(S5_GROUPS, S5_GROUP_CH, S5_GROUPS, S5_STATE), eye)


def _block_diag_out(c):
    eye = jnp.eye(S5_GROUPS, dtype=c.dtype)
    return jnp.einsum("ghp,gk->gpkh", c, eye).reshape(S5_N, S5_WIDTH)


def _block_diag_out_t(m):
    eye = jnp.eye(S5_GROUPS, dtype=m.dtype)
    return jnp.einsum("gpkh,gk->ghp", m.reshape(S5_GROUPS, S5_STATE, S5_GROUPS, S5_GROUP_CH), eye)


def _pad_heads(g, width):
    g = jnp.broadcast_to(g.reshape(-1, width), (N_HEADS, width))
    return jnp.pad(g, ((0, 0), (0, HP - width))).reshape(1, QKW)


def _prep_layer(l, gw, small, consts):
    e_mat, p_mat, r_mat, pl_mat = consts
    p = {}
    p["attn_g"] = small["attn_norm_g"][l].reshape(1, D_MODEL)
    p["ffn_g"] = small["ffn_norm_g"][l].reshape(1, D_MODEL)
    p["qlg"] = small["q_lat_norm_g"][l].reshape(1, MLA_Q_RANK)
    p["kvlg"] = small["kv_lat_norm_g"][l].reshape(1, MLA_KV_RANK)
    p["mqg"] = _pad_heads(small["mla_q_norm_g"][l], MLA_QK_DIM)
    p["mkg"] = _pad_heads(small["mla_k_norm_g"][l], MLA_QK_DIM)
    p["fqg"] = _pad_heads(small["fox_q_norm_g"][l], HEAD_V)
    p["fkg"] = _pad_heads(small["fox_k_norm_g"][l], HEAD_V)
    w_uq = gw["w_uq"].reshape(MLA_Q_RANK, N_HEADS, MLA_QK_DIM)
    p["w_uq"] = jnp.pad(w_uq, ((0, 0), (0, 0), (0, HP - MLA_QK_DIM))).reshape(MLA_Q_RANK, QKW)
    w_ukv = gw["w_ukv"].reshape(MLA_KV_RANK, N_HEADS, MLA_NOPE_DIM + HEAD_V)
    p["w_uk"] = jnp.pad(w_ukv[..., :MLA_NOPE_DIM], ((0, 0), (0, 0), (0, HP - MLA_NOPE_DIM))).reshape(MLA_KV_RANK, QKW)
    p["w_uv"] = w_ukv[..., MLA_NOPE_DIM:].reshape(MLA_KV_RANK, N_HEADS * HEAD_V)
    p["w_glu"] = gw["w_glu"]
    p["b_glu"] = small["s5_b_glu"][l].reshape(1, S5_WIDTH)
    p["d_row"] = small["s5_d"][l].reshape(1, S5_WIDTH)
    p["lam_re"] = small["s5_lambda_re"][l].reshape(S5_N, 1)
    p["lam_im"] = small["s5_lambda_im"][l].reshape(S5_N, 1)
    p["log_step"] = jnp.repeat(small["s5_log_step"][l], S5_STATE).reshape(S5_N, 1)
    p["b_re"] = small["s5_b_re"][l].reshape(S5_N, S5_GROUP_CH)
    p["b_im"] = small["s5_b_im"][l].reshape(S5_N, S5_GROUP_CH)
    c_re, c_im = small["s5_c_re"][l], small["s5_c_im"][l]
    c_blk = jnp.concatenate([_block_diag_out(c_re), -_block_diag_out(c_im)], axis=0)
    re_idx, im_idx = _s5_perm()
    inv = np.empty(2 * S5_N, np.int64)
    inv[re_idx] = np.arange(S5_N)
    inv[im_idx] = S5_N + np.arange(S5_N)
    p["c_blk"] = c_blk[inv]
    p["fbias"] = small["fox_f_bias"][l]
    for nm in ("w_in", "w_out", "w_up", "w_down", "w_branch", "conv"):
        p[nm] = gw[nm]
    return p


def _s5_param_call(p, l):
    ins = [p["lam_re"], p["lam_im"], p["log_step"], p["b_re"], p["b_im"]]
    outs = [((S5_N, 1), F32, (S5_N, 1), lambda i: (0, 0), None)] * 2 + \
           [((S5_N, S5_GROUP_CH), F32, (S5_N, S5_GROUP_CH), lambda i: (0, 0), None)] * 2
    return _tile_call(_s5_params, "s5_params", (1,), [_par(a) for a in ins], outs)


def _s5_param_bwd_call(p, cts):
    ins = [p["lam_re"], p["lam_im"], p["log_step"], p["b_re"], p["b_im"]]

    def fn(lr, li, ls, br, bi, g0, g1, g2, g3):
        _, vjp = jax.vjp(_s5_params, lr, li, ls, br, bi)
        return vjp((g0, g1, g2, g3))

    outs = [((S5_N, 1), F32, (S5_N, 1), lambda i: (0, 0), None)] * 3 + \
           [((S5_N, S5_GROUP_CH), F32, (S5_N, S5_GROUP_CH), lambda i: (0, 0), None)] * 2
    return _tile_call(fn, "s5_params_bwd", (1,), [_par(a) for a in ins + list(cts)], outs)


def _layer_fwd(x_prev, f_prev, p, tabs, consts, bsz):
    t = x_prev.shape[0]
    seq = t // bsz
    nt = t // TM
    cos, sin = tabs
    e_mat, p_mat, r_mat, pl_mat = consts
    sv = {}

    if f_prev is None:
        x = x_prev
        h = _tile_call(lambda xv, g: _rms(xv, g), "norm_first", (nt,), [_row(x), _par(p["attn_g"])],
                       [_orow(t, D_MODEL)])[0]
    else:
        x, h = _tile_call(lambda xv, fv, g: (xv + fv, _rms(xv + fv, g)), "norm_attn", (nt,),
                          [_row(x_prev), _row(f_prev), _par(p["attn_g"])], [_orow(t, D_MODEL), _orow(t, D_MODEL)])
    sv["x"], sv["h"] = x, h
    proj = _mm(h, p["w_in"], "nn", "mm_in")
    sv["proj"] = proj

    def pre_fn(pa, cosv, sinv, qlg, w_uq, kvlg, w_uk, w_uv, mqg, mkg, fqg, fkg, em, pm, rm, plm):
        return _mixer_pre(pa[:, O_CQ:O_CQ + MLA_Q_RANK], pa[:, O_CKV:O_CKV + MLA_KV_RANK], pa[:, O_KR:O_KR + LANES],
                          pa[:, O_FQ:O_FQ + 256], pa[:, O_FK:O_FK + 256], qlg, w_uq, kvlg, w_uk, w_uv,
                          mqg, mkg, fqg, fkg, cosv, sinv, em, pm, rm, plm)

    pre_params = [p["qlg"], p["w_uq"], p["kvlg"], p["w_uk"], p["w_uv"], p["mqg"], p["mkg"], p["fqg"], p["fkg"],
                  e_mat, p_mat, r_mat, pl_mat]
    q_m, k_m, v_m, q_f, k_f = _tile_call(
        pre_fn, "mixer_pre", (nt,), [_row(proj, PA), _row(cos), _row(sin)] + [_par(a) for a in pre_params],
        [_orow(t, QKW), _orow(t, QKW), _orow(t, 2 * LANES), _orow(t, QKW), _orow(t, QKW)])
    sv.update(q_m=q_m, k_m=k_m, v_m=v_m, q_f=q_f, k_f=k_f)

    z = proj[:, O_FL:O_FL + N_HEADS].reshape(bsz, seq, N_HEADS).transpose(0, 2, 1).reshape(bsz * N_HEADS, seq)
    fb = jnp.tile(p["fbias"], bsz).reshape(bsz * N_HEADS, 1)
    cum = _forget_fwd(z, fb, "forget_fwd")
    ck = cum.reshape(bsz, 2, 2, seq)
    cq = ck.transpose(0, 1, 3, 2)
    sv.update(z=z, fb=fb, cq=cq, ck=ck)

    r3 = lambda a: a.reshape(bsz, seq, a.shape[-1])
    o_m, lse_m = _attn_fwd(r3(q_m), r3(k_m), r3(v_m), 0, None, None, MLA_QK_DIM ** -0.5, "attn_mla_fwd")
    o_f, lse_f = _attn_fwd(r3(q_f), r3(k_f), r3(proj), O_FV // LANES, cq, ck, HEAD_V ** -0.5, "attn_fox_fwd")
    o_m, o_f = o_m.reshape(t, 2 * LANES), o_f.reshape(t, 2 * LANES)
    sv.update(o_m=o_m, o_f=o_f, lse_m=lse_m, lse_f=lse_f)

    lbar_re, lbar_im, bbar_re, bbar_im = _s5_param_call(p, 0)
    lbar = _to_scan_cols(lbar_re.reshape(1, S5_N), lbar_im.reshape(1, S5_N))
    b_blk = _to_scan_cols(_block_diag_in(bbar_re.reshape(S5_GROUPS, S5_STATE, S5_GROUP_CH)),
                          _block_diag_in(bbar_im.reshape(S5_GROUPS, S5_STATE, S5_GROUP_CH)))
    sv.update(lbar=lbar, b_blk=b_blk)
    bu = _tile_call(lambda u, bb: bdot(u, bb), "s5_bu", (nt,), [_row(proj, S5_WIDTH, O_U // S5_WIDTH), _par(b_blk)],
                    [_orow(t, 2 * S5_N)])[0]
    state = _scan_fwd(bu, lbar, bsz, "s5_scan_fwd")
    sv["state"] = state
    post_params = [p["c_blk"], p["d_row"], p["w_glu"], p["b_glu"]]
    o_s = _tile_call(_s5_post, "s5_post", (nt,),
                     [_row(state), _row(proj, S5_WIDTH, O_U // S5_WIDTH)] + [_par(a) for a in post_params],
                     [_orow(t, S5_WIDTH)])[0]
    sv["o_s"] = o_s

    def merge_fn(oa, ob, oc, ga, gb, gc, wb):
        return _merge(oa, ob, oc, ga, gb, gc, wb[0], wb[1], wb[2])

    gate_specs = [_row(proj, D_MODEL, PA // D_MODEL + n) for n in range(3)]
    merged = _tile_call(merge_fn, "merge", (nt,), [_row(o_m), _row(o_f), _row(o_s)] + gate_specs + [_par(p["w_branch"])],
                        [_orow(t, D_MODEL)])[0]
    sv["merged"] = merged
    attn_out = _mm(merged, p["w_out"], "nn", "mm_out")

    x1, h2 = _tile_call(lambda xv, av, g: (xv + av, _rms(xv + av, g)), "norm_ffn", (nt,),
                        [_row(x), _row(attn_out), _par(p["ffn_g"])], [_orow(t, D_MODEL), _orow(t, D_MODEL)])
    sv["x1"], sv["h2"] = x1, h2
    up = _mm(h2, p["w_up"], "nn", "mm_up")
    sv["up"] = up

    def ffn_fn(upv, cw):
        return _ffn_mid(upv[:, :FF_BLK], upv[:, FF_BLK:], cw[0:1, :FF_BLK], cw[1:2, :FF_BLK], cw[2:3, :FF_BLK],
                        cw[0:1, FF_BLK:], cw[1:2, FF_BLK:], cw[2:3, FF_BLK:])

    nblk = FF_PW // FF_BLK
    act = _tile_call(ffn_fn, "ffn_mid", (nblk, bsz),
                     [(up, (seq, 2 * FF_BLK), lambda j, b: (b, j)), (p["conv"], (3, 2 * FF_BLK), lambda j, b: (0, j))],
                     [((t, FF_PW), F32, (seq, FF_BLK), lambda j, b: (b, j), None)])[0]
    sv["act"] = act
    ffn_out = _mm(act, p["w_down"], "nn", "mm_down")
    return x1, ffn_out, sv


def _layer_bwd(dx2, p, sv, tabs, consts, bsz, first_layer):
    t = dx2.shape[0]
    seq = t // bsz
    nt = t // TM
    cos, sin = tabs
    e_mat, p_mat, r_mat, pl_mat = consts
    g = {}

    g["w_down"] = _mm(sv["act"], dx2, "tn", "mm_down_dw")
    dact = _mm(dx2, p["w_down"], "nt", "mm_down_dx")

    def ffn_bwd_fn(upv, cw, da):
        args = (upv[:, :FF_BLK], upv[:, FF_BLK:], cw[0:1, :FF_BLK], cw[1:2, :FF_BLK], cw[2:3, :FF_BLK],
                cw[0:1, FF_BLK:], cw[1:2, FF_BLK:], cw[2:3, FF_BLK:])
        _, vjp = jax.vjp(_ffn_mid, *args)
        dg, dv, g0, g1, g2, v0, v1, v2 = vjp(da)
        return (jnp.concatenate([dg, dv], axis=1), jnp.concatenate([g0, v0], axis=1),
                jnp.concatenate([g1, v1], axis=1), jnp.concatenate([g2, v2], axis=1))

    nblk = FF_PW // FF_BLK
    cw_out = ((1, UP_W), F32, (1, 2 * FF_BLK), lambda j, b: (0, j), (1,))
    dup, dc0, dc1, dc2 = _tile_call(
        ffn_bwd_fn, "ffn_mid_bwd", (nblk, bsz),
        [(sv["up"], (seq, 2 * FF_BLK), lambda j, b: (b, j)), (p["conv"], (3, 2 * FF_BLK), lambda j, b: (0, j)),
         (dact, (seq, FF_BLK), lambda j, b: (b, j))],
        [((t, UP_W), F32, (seq, 2 * FF_BLK), lambda j, b: (b, j), None), cw_out, cw_out, cw_out])
    g["conv"] = jnp.concatenate([dc0, dc1, dc2], axis=0)
    g["w_up"] = _mm(sv["h2"], dup, "tn", "mm_up_dw")
    dh2 = _mm(dup, p["w_up"], "nt", "mm_up_dx")

    def norm_bwd_fn(xv, gv, dh, dres):
        _, vjp = jax.vjp(_rms, xv, gv)
        dxv, dg = vjp(dh)
        return dres + dxv, dg

    d1, g["ffn_g"] = _tile_call(norm_bwd_fn, "norm_bwd", (nt,),
                                [_row(sv["x1"]), _par(p["ffn_g"]), _row(dh2), _row(dx2)],
                                [_orow(t, D_MODEL), _oacc((1, D_MODEL))])

    g["w_out"] = _mm(sv["merged"], d1, "tn", "mm_out_dw")
    dmerged = _mm(d1, p["w_out"], "nt", "mm_out_dx")

    def merge_bwd_fn(oa, ob, oc, ga, gb, gc, wb, dm):
        wf = wb.astype(F32)
        _, vjp = jax.vjp(_merge, oa, ob, oc, ga, gb, gc, wf[0], wf[1], wf[2])
        doa, dob, doc, dga, dgb, dgc, dwa, dwb, dwc = vjp(dm)
        return doa, dob, doc, jnp.concatenate([dga, dgb, dgc], axis=1), jnp.stack([dwa, dwb, dwc])

    proj = sv["proj"]
    gate_specs = [_row(proj, D_MODEL, PA // D_MODEL + n) for n in range(3)]
    do_m, do_f, do_s, dgl, g["w_branch"] = _tile_call(
        merge_bwd_fn, "merge_bwd", (nt,),
        [_row(sv["o_m"]), _row(sv["o_f"]), _row(sv["o_s"])] + gate_specs + [_par(p["w_branch"]), _row(dmerged)],
        [_orow(t, 2 * LANES), _orow(t, 2 * LANES), _orow(t, S5_WIDTH), _orow(t, 3 * D_MODEL),
         _oacc((3, S5_WIDTH, D_MODEL))])

    def post_bwd_fn(st, u, cb, dr, wg, bg, do):
        _, vjp = jax.vjp(_s5_post, st, u, cb.astype(F32), dr, wg.astype(F32), bg)
        return vjp(do)

    u_spec = _row(proj, S5_WIDTH, O_U // S5_WIDTH)
    dstate, du1, dc_blk, g_d, g["w_glu"], g_bglu = _tile_call(
        post_bwd_fn, "s5_post_bwd", (nt,),
        [_row(sv["state"]), u_spec, _par(p["c_blk"]), _par(p["d_row"]), _par(p["w_glu"]), _par(p["b_glu"]), _row(do_s)],
        [_orow(t, 2 * S5_N), _orow(t, S5_WIDTH), _oacc((2 * S5_N, S5_WIDTH)), _oacc((1, S5_WIDTH)),
         _oacc((S5_WIDTH, S5_WIDTH)), _oacc((1, S5_WIDTH))])
    dbu, dlbar = _scan_bwd(dstate, sv["state"], sv["lbar"], bsz, "s5_scan_bwd")

    def bu_bwd_fn(u, bb, dbv, du_in):
        _, vjp = jax.vjp(bdot, u, bb)
        du, dbb = vjp(dbv)
        return du_in + du, dbb

    du, db_blk = _tile_call(bu_bwd_fn, "s5_bu_bwd", (nt,), [u_spec, _par(sv["b_blk"]), _row(dbu), _row(du1)],
                            [_orow(t, S5_WIDTH), _oacc((S5_WIDTH, 2 * S5_N))])
    dlr, dli = _from_scan_cols(dlbar)
    dbr, dbi = _from_scan_cols(db_blk)
    cts = (dlr.reshape(S5_N, 1), dli.reshape(S5_N, 1),
           _block_diag_in_t(dbr).reshape(S5_N, S5_GROUP_CH), _block_diag_in_t(dbi).reshape(S5_N, S5_GROUP_CH))
    g_lr, g_li, g_ls, g_br, g_bi = _s5_param_bwd_call(p, cts)
    re_idx, im_idx = _s5_perm()
    g["s5_lambda_re"] = g_lr.reshape(S5_GROUPS, S5_STATE)
    g["s5_lambda_im"] = g_li.reshape(S5_GROUPS, S5_STATE)
    g["s5_log_step"] = jnp.sum(g_ls.reshape(S5_GROUPS, S5_STATE), axis=1)
    g["s5_b_re"] = g_br.reshape(S5_GROUPS, S5_STATE, S5_GROUP_CH)
    g["s5_b_im"] = g_bi.reshape(S5_GROUPS, S5_STATE, S5_GROUP_CH)
    g["s5_c_re"] = _block_diag_out_t(dc_blk[re_idx])
    g["s5_c_im"] = -_block_diag_out_t(dc_blk[im_idx])
    g["s5_d"] = g_d.reshape(S5_GROUPS, S5_GROUP_CH)
    g["s5_b_glu"] = g_bglu.reshape(S5_WIDTH)

    r3 = lambda a: a.reshape(bsz, seq, a.shape[-1])
    dq_m, dk_m, dv_m = _attn_bwd(r3(sv["q_m"]), r3(sv["k_m"]), r3(sv["v_m"]), 0, r3(sv["o_m"]), r3(do_m), sv["lse_m"],
                                 None, None, MLA_QK_DIM ** -0.5, "attn_mla_bwd")
    dq_f, dk_f, dv_f, dcq, dck = _attn_bwd(r3(sv["q_f"]), r3(sv["k_f"]), r3(proj), O_FV // LANES, r3(sv["o_f"]),
                                           r3(do_f), sv["lse_f"], sv["cq"], sv["ck"], HEAD_V ** -0.5, "attn_fox_bwd")
    dcum = (dck + dcq.transpose(0, 1, 3, 2)).reshape(bsz * N_HEADS, seq)
    dz, dfb = _forget_bwd(dcum, sv["z"], sv["fb"], "forget_bwd")
    g["fox_f_bias"] = jnp.sum(dfb.reshape(bsz, N_HEADS), axis=0)
    dfl = jnp.pad(dz.reshape(bsz, N_HEADS, seq).transpose(0, 2, 1).reshape(t, N_HEADS), ((0, 0), (0, LANES - N_HEADS)))

    def pre_bwd_fn(pa, cosv, sinv, qlg, w_uq, kvlg, w_uk, w_uv, mqg, mkg, fqg, fkg, em, pm, rm, plm,
                   gq, gk, gv, gqf, gkf, gvf, gu, gfl, ggl):
        f = functools.partial(_mixer_pre, cos=cosv, sin=sinv, e_mat=em, p_mat=pm, r_mat=rm, pl_mat=plm)
        prim = (pa[:, O_CQ:O_CQ + MLA_Q_RANK], pa[:, O_CKV:O_CKV + MLA_KV_RANK], pa[:, O_KR:O_KR + LANES],
                pa[:, O_FQ:O_FQ + 256], pa[:, O_FK:O_FK + 256], qlg, w_uq.astype(F32), kvlg, w_uk.astype(F32),
                w_uv.astype(F32), mqg, mkg, fqg, fkg)
        _, vjp = jax.vjp(lambda *a: f(*a), *prim)
        dcq_, dckv, dkr, dfq, dfk, dqlg, dwuq, dkvlg, dwuk, dwuv, dmqg, dmkg, dfqg, dfkg = vjp((gq, gk, gv, gqf, gkf))
        zpad = jnp.zeros((pa.shape[0], PA - O_CQ - MLA_Q_RANK), F32)
        dproj = jnp.concatenate([dckv, dfq, dfk, gvf, gu, dkr, gfl, dcq_, zpad, ggl], axis=1)
        return dproj, dqlg, dwuq, dkvlg, dwuk, dwuv, dmqg, dmkg, dfqg, dfkg

    pre_params = [p["qlg"], p["w_uq"], p["kvlg"], p["w_uk"], p["w_uv"], p["mqg"], p["mkg"], p["fqg"], p["fkg"],
                  e_mat, p_mat, r_mat, pl_mat]
    cts_in = [dq_m.reshape(t, QKW), dk_m.reshape(t, QKW), dv_m.reshape(t, 2 * LANES), dq_f.reshape(t, QKW),
              dk_f.reshape(t, QKW), dv_f.reshape(t, 2 * LANES), du, dfl, dgl]
    (dproj, g_qlg, g_wuq, g_kvlg, g_wuk, g_wuv, g_mqg, g_mkg, g_fqg, g_fkg) = _tile_call(
        pre_bwd_fn, "mixer_pre_bwd", (nt,),
        [_row(proj, PA), _row(cos), _row(sin)] + [_par(a) for a in pre_params] + [_row(a) for a in cts_in],
        [_orow(t, PW), _oacc((1, MLA_Q_RANK)), _oacc((MLA_Q_RANK, QKW)), _oacc((1, MLA_KV_RANK)),
         _oacc((MLA_KV_RANK, QKW)), _oacc((MLA_KV_RANK, N_HEADS * HEAD_V)), _oacc((1, QKW)), _oacc((1, QKW)),
         _oacc((1, QKW)), _oacc((1, QKW))])
    g["q_lat_norm_g"] = g_qlg.reshape(MLA_Q_RANK)
    g["kv_lat_norm_g"] = g_kvlg.reshape(MLA_KV_RANK)
    heads = lambda a, w: jnp.sum(a.reshape(N_HEADS, HP)[:, :w], axis=0)
    g["mla_q_norm_g"], g["mla_k_norm_g"] = heads(g_mqg, MLA_QK_DIM), heads(g_mkg, MLA_QK_DIM)
    g["fox_q_norm_g"], g["fox_k_norm_g"] = heads(g_fqg, HEAD_V), heads(g_fkg, HEAD_V)
    g["w_uq"] = g_wuq.reshape(MLA_Q_RANK, N_HEADS, HP)[..., :MLA_QK_DIM].reshape(MLA_Q_RANK, N_HEADS * MLA_QK_DIM)
    g["w_ukv"] = jnp.concatenate([g_wuk.reshape(MLA_KV_RANK, N_HEADS, HP)[..., :MLA_NOPE_DIM],
                                  g_wuv.reshape(MLA_KV_RANK, N_HEADS, HEAD_V)], axis=-1).reshape(MLA_KV_RANK, QKW)

    g["w_in"] = _mm(sv["h"], dproj, "tn", "mm_in_dw")
    dh = _mm(dproj, p["w_in"], "nt", "mm_in_dx")
    dx, g["attn_g"] = _tile_call(norm_bwd_fn, "norm_bwd", (nt,),
                                 [_row(sv["x"]), _par(p["attn_g"]), _row(dh), _row(d1)],
                                 [_orow(t, D_MODEL), _oacc((1, D_MODEL))])
    return dx, g


def _row_tile(r, c, max_elems):
    if r * c <= max_elems:
        return r
    best = None
    for d in range(8, r, 8):
        if r % d == 0 and d * c <= max_elems:
            best = d
    assert best is not None, (r, c)
    return best


def _sum8(land, name):
    _, r, c = land.shape
    tr = _row_tile(r, N_DEV * c, 1 << 20)

    def fn(lv):
        acc = lv[0]
        for i in range(1, N_DEV):
            acc = acc + lv[i]
        return acc

    return _tile_call(fn, name, (r // tr,), [(land, (N_DEV, tr, c), lambda i: (0, i, 0))],
                      [((r, c), F32, (tr, c), lambda i: (i, 0), None)])[0]


def _adamw_call(w, g, m, v, name):
    r, c = w.shape
    tr = _row_tile(r, c, 1 << 19)
    spec = lambda a: (a, (tr, c), lambda i: (i, 0))
    o = ((r, c), F32, (tr, c), lambda i: (i, 0), None)
    return _tile_call(_adamw, name, (r // tr,), [spec(w), spec(g), spec(m), spec(v)], [o, o, o])


SMALL_NAMES = ("attn_norm_g", "q_lat_norm_g", "kv_lat_norm_g", "mla_q_norm_g", "mla_k_norm_g", "fox_q_norm_g",
               "fox_k_norm_g", "fox_f_bias", "s5_lambda_re", "s5_lambda_im", "s5_b_re", "s5_b_im", "s5_c_re", "s5_c_im",
               "s5_d", "s5_log_step", "s5_b_glu", "ffn_norm_g")
BIG_NAMES = ("w_in", "w_uq", "w_ukv", "s5_w_glu", "w_branch", "w_out", "w_up", "ffn_conv_w", "w_down")
ALL_NAMES = ("attn_norm_g", "w_in", "q_lat_norm_g", "w_uq", "kv_lat_norm_g", "w_ukv", "mla_q_norm_g", "mla_k_norm_g",
             "fox_q_norm_g", "fox_k_norm_g", "fox_f_bias", "s5_lambda_re", "s5_lambda_im", "s5_b_re", "s5_b_im",
             "s5_c_re", "s5_c_im", "s5_d", "s5_log_step", "s5_w_glu", "s5_b_glu", "w_branch", "w_out", "ffn_norm_g",
             "w_up", "ffn_conv_w", "w_down")


def _pack(arrs):
    flat = jnp.concatenate([a.reshape(-1) for a in arrs])
    n = flat.shape[0]
    tile = 8 * LANES
    tot = -(-n // tile) * tile
    return jnp.pad(flat, (0, tot - n)).reshape(tot // LANES, LANES)


def _unpack(packed, like):
    flat = packed.reshape(-1)
    out, pos = [], 0
    for a in like:
        n = int(np.prod(a.shape))
        out.append(flat[pos:pos + n].reshape(a.shape))
        pos += n
    return out


def _rope_tables(positions, t):
    pos = positions.reshape(t, 1)
    inv = _inv_freq_row()

    def fn(pv, iv):
        ang = pv.astype(F32) * iv
        return jnp.cos(ang), jnp.sin(ang)

    return _tile_call(fn, "rope_tables", (t // TM,), [_row(pos), _par(inv)], [_orow(t, QKW), _orow(t, QKW)])


def _loss_call(x1, f, target):
    t = x1.shape[0]

    def fn(xv, fv, tv):
        e = xv + fv - tv
        per_tok = jnp.sum(e * e, axis=1, keepdims=True) * (1.0 / D_MODEL)
        return 0.5 * jnp.sum(per_tok, axis=0, keepdims=True), e * (1.0 / D_MODEL)

    return _tile_call(fn, "loss", (t // TM,), [_row(x1), _row(f), _row(target)], [_oacc((1, 1)), _orow(t, D_MODEL)])


def _local_shards(l, w):
    return {
        "w_in": _pad_in_cols(w["w_in"][l]).astype(BF16),
        "w_uq": w["w_uq"][l].reshape(-1, N_HEADS * MLA_QK_DIM).astype(BF16),
        "w_ukv": w["w_ukv"][l].reshape(-1, QKW).astype(BF16),
        "w_glu": w["s5_w_glu"][l].astype(BF16),
        "w_branch": w["w_branch"][l].astype(BF16),
        "w_out": w["w_out"][l].astype(BF16),
        "w_up": _pad_ff(w["w_up"][l], 1).astype(BF16),
        "conv": _pad_ff(w["ffn_conv_w"][l], 1),
        "w_down": _pad_ff(w["w_down"][l], 0).astype(BF16),
    }


GATHERED_SHAPES = {"w_in": (D_MODEL, PW), "w_uq": (MLA_Q_RANK, N_HEADS * MLA_QK_DIM), "w_ukv": (MLA_KV_RANK, QKW),
                   "w_glu": (S5_WIDTH, S5_WIDTH), "w_branch": (3, S5_WIDTH, D_MODEL), "w_out": (D_MODEL, D_MODEL),
                   "w_up": (D_MODEL, UP_W), "conv": (3, UP_W), "w_down": (FF_PW, D_MODEL)}
GRAD_KEY = {"w_in": "w_in", "w_uq": "w_uq", "w_ukv": "w_ukv", "w_glu": "w_glu", "w_branch": "w_branch", "w_out": "w_out",
            "w_up": "w_up", "conv": "conv", "w_down": "w_down"}
WEIGHT_OF = {"w_in": "w_in", "w_uq": "w_uq", "w_ukv": "w_ukv", "w_glu": "s5_w_glu", "w_branch": "w_branch",
             "w_out": "w_out", "w_up": "w_up", "conv": "ffn_conv_w", "w_down": "w_down"}


def _unshard_layout(nm, g2):
    if nm == "w_in":
        return _unpad_in_cols(g2)
    if nm == "w_uq":
        return g2.reshape(-1, N_HEADS, MLA_QK_DIM)
    if nm == "w_ukv":
        return g2.reshape(-1, N_HEADS, MLA_NOPE_DIM + HEAD_V)
    if nm in ("w_up", "conv"):
        return _unpad_ff(g2, 1)
    if nm == "w_down":
        return _unpad_ff(g2, 0)
    return g2


def kernel(x, positions, attn_norm_g, w_in, q_lat_norm_g, w_uq, kv_lat_norm_g, w_ukv, mla_q_norm_g, mla_k_norm_g, fox_q_norm_g, fox_k_norm_g, fox_f_bias, s5_lambda_re, s5_lambda_im, s5_b_re, s5_b_im, s5_c_re, s5_c_im, s5_d, s5_log_step, s5_w_glu, s5_b_glu, w_branch, w_out, ffn_norm_g, w_up, ffn_conv_w, w_down, loss_target, m_attn_norm_g, m_w_in, m_q_lat_norm_g, m_w_uq, m_kv_lat_norm_g, m_w_ukv, m_mla_q_norm_g, m_mla_k_norm_g, m_fox_q_norm_g, m_fox_k_norm_g, m_fox_f_bias, m_s5_lambda_re, m_s5_lambda_im, m_s5_b_re, m_s5_b_im, m_s5_c_re, m_s5_c_im, m_s5_d, m_s5_log_step, m_s5_w_glu, m_s5_b_glu, m_w_branch, m_w_out, m_ffn_norm_g, m_w_up, m_ffn_conv_w, m_w_down, v_attn_norm_g, v_w_in, v_q_lat_norm_g, v_w_uq, v_kv_lat_norm_g, v_w_ukv, v_mla_q_norm_g, v_mla_k_norm_g, v_fox_q_norm_g, v_fox_k_norm_g, v_fox_f_bias, v_s5_lambda_re, v_s5_lambda_im, v_s5_b_re, v_s5_b_im, v_s5_c_re, v_s5_c_im, v_s5_d, v_s5_log_step, v_s5_w_glu, v_s5_b_glu, v_w_branch, v_w_out, v_ffn_norm_g, v_w_up, v_ffn_conv_w, v_w_down):
    args = locals()
    w = {n: args[n] for n in ALL_NAMES}
    m = {n: args["m_" + n] for n in ALL_NAMES}
    v = {n: args["v_" + n] for n in ALL_NAMES}
    depth = w_in.shape[0]
    bsz, seq, _ = x.shape
    t = bsz * seq
    consts = _constants()
    tabs = _rope_tables(positions, t)
    small = {n: w[n] for n in SMALL_NAMES}

    xs = x.reshape(t, D_MODEL)
    f_prev = None
    saved, params = [], []
    for l in range(depth):
        shards = _local_shards(l, w)
        names, plan = _gather_plan(shards)
        full = _exchange("gather_weights", [shards[n] for n in names],
                         [(GATHERED_SHAPES[n], shards[n].dtype) for n in names], plan)
        p = _prep_layer(l, dict(zip(names, full)), small, consts)
        xs, f_prev, sv = _layer_fwd(xs, f_prev, p, tabs, consts, bsz)
        saved.append(sv)
        params.append(p)

    loss_part, dy = _loss_call(xs, f_prev, loss_target.reshape(t, D_MODEL))
    loss = lax.psum(loss_part[0, 0], ("x", "y", "c"))

    small_grads = {n: [None] * depth for n in SMALL_NAMES}
    big = {n: {"g": [None] * depth, "d": [None] * depth, "m": [None] * depth, "v": [None] * depth} for n in BIG_NAMES}
    dx = dy
    for l in reversed(range(depth)):
        dx, g = _layer_bwd(dx, params[l], saved[l], tabs, consts, bsz, l == 0)
        small_grads["attn_norm_g"][l] = g["attn_g"].reshape(D_MODEL)
        small_grads["ffn_norm_g"][l] = g["ffn_g"].reshape(D_MODEL)
        for n in SMALL_NAMES:
            if n not in ("attn_norm_g", "ffn_norm_g"):
                small_grads[n][l] = g[n]
        names = list(GATHERED_SHAPES)
        shard_shapes = {n: _local_shards_shape(n) for n in names}
        land = _exchange("scatter_grads", [g[GRAD_KEY[n]] for n in names],
                         [((N_DEV,) + shard_shapes[n], F32) for n in names], _scatter_plan(names, shard_shapes))
        for n, ld in zip(names, land):
            wn = WEIGHT_OF[n]
            ld2 = ld.reshape(N_DEV, -1, ld.shape[-1])
            gsum = _unshard_layout(n, _sum8(ld2, "sum_" + n).reshape(shard_shapes[n]))
            shp = gsum.shape
            two = lambda a: a.reshape(-1, shp[-1])
            dl, ml, vl = _adamw_call(two(w[wn][l]), two(gsum), two(m[wn][l]), two(v[wn][l]), "adamw_" + n)
            big[wn]["g"][l] = gsum.reshape(w[wn].shape[1:])
            big[wn]["d"][l], big[wn]["m"][l], big[wn]["v"][l] = (a.reshape(w[wn].shape[1:]) for a in (dl, ml, vl))

    sg = [jnp.stack(small_grads[n]).reshape(w[n].shape) for n in SMALL_NAMES]
    packed = _pack(sg)
    land = _exchange("gather_small_grads", [packed], [((N_DEV,) + packed.shape, F32)],
                     _scatter_plan(["small"], {}))[0]
    gs = _sum8(land, "sum_small")
    ds, ms, vs = _adamw_call(_pack([w[n] for n in SMALL_NAMES]), gs, _pack([m[n] for n in SMALL_NAMES]),
                             _pack([v[n] for n in SMALL_NAMES]), "adamw_small")
    like = [w[n] for n in SMALL_NAMES]
    small_out = {k: dict(zip(SMALL_NAMES, _unpack(a, like))) for k, a in (("g", gs), ("d", ds), ("m", ms), ("v", vs))}

    def out_of(kind, n):
        if n in SMALL_NAMES:
            return small_out[kind][n]
        return jnp.stack(big[n][kind])

    outs = [loss, dx.reshape(bsz, seq, D_MODEL)]
    for kind in ("g", "d", "m", "v"):
        outs += [out_of(kind, n) for n in ALL_NAMES]
    return tuple(outs)


def _local_shards_shape(nm):
    return {"w_in": (D_MODEL // N_DEV, PW), "w_uq": (MLA_Q_RANK // N_DEV, N_HEADS * MLA_QK_DIM),
            "w_ukv": (MLA_KV_RANK // N_DEV, QKW), "w_glu": (S5_WIDTH // N_DEV, S5_WIDTH),
            "w_branch": (3, S5_WIDTH, LANES), "w_out": (D_MODEL // N_DEV, D_MODEL), "w_up": (D_MODEL, 2 * FF_BLK),
            "conv": (3, 2 * FF_BLK), "w_down": (FF_BLK, D_MODEL)}[nm]
```

```python
import functools
import math

import numpy as np
import jax
import jax.numpy as jnp
from jax import lax
from jax.experimental import pallas as pl
from jax.experimental.pallas import tpu as pltpu

F32, BF16 = jnp.float32, jnp.bfloat16

D_MODEL = 1024
N_DEV = 8
MLA_Q_RANK, MLA_KV_RANK, MLA_ROPE_DIM, MLA_NOPE_DIM, MLA_QK_DIM = 384, 256, 32, 64, 96
N_HEADS, HEAD_V = 4, 64
S5_GROUPS, S5_GROUP_CH, S5_STATE, S5_WIDTH = 16, 16, 64, 256
S5_N = S5_GROUPS * S5_STATE
D_FF = 2816
D_IN = 4772
ROPE_THETA = 10000.0
NORM_EPS = 1e-6
NEG_INF = -1e30
ADAM_LR, ADAM_B1, ADAM_B2, ADAM_EPS, ADAM_WD, ADAM_STEP = 0.001, 0.9, 0.999, 1e-08, 0.01, 10

VMEM_LIMIT_BYTES = 56 * 1024 * 1024
LANES = 128
TM = 256
TQ = 256
HP = 128
QKW = N_HEADS * HP

PW = 5120
PA = 2048
IN_SEGS = ((0, 384, 256),
           (256, 672, 256),
           (512, 928, 256),
           (768, 1184, 256),
           (1024, 1444, 256),
           (1280, 640, 32),
           (1408, 1440, 4),
           (1536, 0, 384),
           (2048, 1700, 3072))
O_CKV, O_FQ, O_FK, O_FV, O_U, O_KR, O_FL, O_CQ = 0, 256, 512, 768, 1024, 1280, 1408, 1536

FF_BLK = 384
FF_HALF = D_FF // 8
FF_PW = 8 * FF_BLK
UP_W = 2 * FF_PW

NN = (((1,), (0,)), ((), ()))
NT = (((1,), (1,)), ((), ()))
TN = (((0,), (0,)), ((), ()))


def _pcall(body, *, name, grid, in_specs, out_specs, out_shape, scratch_shapes=(), rider=None):
    params = pltpu.CompilerParams(dimension_semantics=("arbitrary",) * len(grid), vmem_limit_bytes=VMEM_LIMIT_BYTES)
    if rider is None:
        return pl.pallas_call(body, name=name, grid=grid, in_specs=in_specs, out_specs=out_specs, out_shape=out_shape,
                              scratch_shapes=list(scratch_shapes), compiler_params=params)
    in_specs, out_specs, out_shape = list(in_specs), list(out_specs), list(out_shape)
    n_in, n_out, n_scr, nx = len(in_specs), len(out_specs), len(scratch_shapes), rider.n
    any_spec = pl.BlockSpec(memory_space=pl.ANY)

    def at_step(last):
        conds = [pl.program_id(a) == (g - 1 if last else 0) for a, g in enumerate(grid)]
        return functools.reduce(jnp.logical_and, conds)

    def body_with_rider(*refs):
        ins, xsrc = refs[:n_in], refs[n_in:n_in + nx]
        outs, xout = refs[n_in + nx:n_in + nx + n_out], refs[n_in + nx + n_out:n_in + 2 * nx + n_out]
        scr = refs[n_in + 2 * nx + n_out:n_in + 2 * nx + n_out + n_scr]
        sems = refs[n_in + 2 * nx + n_out + n_scr:]

        @pl.when(at_step(False))
        def _():
            rider.start(rider.copies(xsrc, xout, *sems))

        body(*ins, *outs, *scr)

        @pl.when(at_step(True))
        def _():
            rider.wait(rider.copies(xsrc, xout, *sems))

    call = pl.pallas_call(body_with_rider, name=name, grid=grid, in_specs=in_specs + [any_spec] * nx,
                          out_specs=out_specs + [any_spec] * nx,
                          out_shape=out_shape + [jax.ShapeDtypeStruct(s, d) for s, d in rider.out_shapes],
                          scratch_shapes=list(scratch_shapes) + rider.sems(), compiler_params=params)
    return lambda *args: call(*args, *rider.srcs)


def _tile_call(fn, name, grid, ins, outs, rider=None):
    n_in = len(ins)

    def body(*refs):
        res = fn(*[r[...] for r in refs[:n_in]])
        if not isinstance(res, (tuple, list)):
            res = (res,)
        assert len(res) == len(outs), (name, len(res), len(outs))
        for r, v, o in zip(refs[n_in:], res, outs):
            v = v.astype(r.dtype)
            if o[4] is None:
                r[...] = v
            else:
                first = functools.reduce(jnp.logical_and, [pl.program_id(a) == 0 for a in o[4]])

                @pl.when(first)
                def _():
                    r[...] = v

                @pl.when(jnp.logical_not(first))
                def _():
                    r[...] += v

    res = _pcall(body, name=name, grid=grid,
                 in_specs=[pl.BlockSpec(b, m) for _, b, m in ins],
                 out_specs=[pl.BlockSpec(o[2], o[3]) for o in outs],
                 out_shape=[jax.ShapeDtypeStruct(o[0], o[1]) for o in outs], rider=rider)(*[a for a, _, _ in ins])
    return res


def _row(a, width=None, col_block=0, tm=TM):
    w = a.shape[1] if width is None else width
    return (a, (tm, w), lambda i, c=col_block: (i, c))


def _par(a):
    nd = a.ndim
    return (a, a.shape, lambda i, nd=nd: (0,) * nd)


def _orow(t, w, dtype=F32, tm=TM):
    return ((t, w), dtype, (tm, w), lambda i: (i, 0), None)


def _oacc(shape):
    nd = len(shape)
    return (tuple(shape), F32, tuple(shape), lambda i, nd=nd: (0,) * nd, (0,))


def _pick(n, target):
    best = None
    for d in range(LANES, min(n, target) + 1, LANES):
        if n % d == 0:
            best = d
    return n if best is None else best


def _mm(a, b, mode, name, out_dtype=F32, tm=512, tn=2048, tk=1024):
    if mode == "nn":
        (m, k), (_, n) = a.shape, b.shape
    elif mode == "nt":
        (m, k), (n, _) = a.shape, b.shape
    else:
        (k, m), (_, n) = a.shape, b.shape
    tm, tn, tk = _pick(m, tm), _pick(n, tn), _pick(k, tk)
    nk = k // tk
    dims = {"nn": NN, "nt": NT, "tn": TN}[mode]
    a_spec = (pl.BlockSpec((tk, tm), lambda i, j, l: (l, i)) if mode == "tn"
              else pl.BlockSpec((tm, tk), lambda i, j, l: (i, l)))
    b_spec = (pl.BlockSpec((tn, tk), lambda i, j, l: (j, l)) if mode == "nt"
              else pl.BlockSpec((tk, tn), lambda i, j, l: (l, j)))

    def body(a_ref, b_ref, o_ref, acc_ref):
        part = lax.dot_general(a_ref[...].astype(BF16), b_ref[...].astype(BF16), dims, preferred_element_type=F32)
        if nk == 1:
            o_ref[...] = part.astype(o_ref.dtype)
        else:
            l = pl.program_id(2)

            @pl.when(l == 0)
            def _():
                acc_ref[...] = part

            @pl.when(l > 0)
            def _():
                acc_ref[...] += part

            @pl.when(l == nk - 1)
            def _():
                o_ref[...] = acc_ref[...].astype(o_ref.dtype)

    return _pcall(body, name=name, grid=(m // tm, n // tn, nk), in_specs=[a_spec, b_spec],
                  out_specs=pl.BlockSpec((tm, tn), lambda i, j, l: (i, j)),
                  out_shape=jax.ShapeDtypeStruct((m, n), out_dtype),
                  scratch_shapes=[pltpu.VMEM((tm, tn) if nk > 1 else (8, LANES), F32)])(a, b)


def _dot(x, w, dims):
    return lax.dot_general(x.astype(BF16), w.astype(BF16), dims, preferred_element_type=F32)


@jax.custom_vjp
def bdot(x, w):
    return _dot(x, w, NN)


def _bdot_fwd(x, w):
    return _dot(x, w, NN), (x, w)


def _bdot_bwd(res, g):
    x, w = res
    return _dot(g, w, NT), _dot(x, g, TN)


bdot.defvjp(_bdot_fwd, _bdot_bwd)


def _split3(x):
    x1 = x.astype(BF16)
    r1 = x - x1.astype(F32)
    x2 = r1.astype(BF16)
    x3 = (r1 - x2.astype(F32)).astype(BF16)
    return x1, x2, x3


def _cdot(x, m, dims):
    return sum(lax.dot_general(p, m, dims, preferred_element_type=F32) for p in _split3(x))


@jax.custom_vjp
def cdot(x, m):
    return _cdot(x, m, NN)


def _cdot_fwd(x, m):
    return _cdot(x, m, NN), m


def _cdot_bwd(m, g):
    return _cdot(g, m, NT), jnp.zeros_like(m)


cdot.defvjp(_cdot_fwd, _cdot_bwd)


def _shift_rows_impl(x, s, reverse):
    n = x.shape[0]
    idx = lax.broadcasted_iota(jnp.int32, x.shape, 0)
    if reverse:
        return jnp.where(idx < n - s, pltpu.roll(x, n - s, 0), 0.0)
    return jnp.where(idx >= s, pltpu.roll(x, s, 0), 0.0)


@functools.partial(jax.custom_vjp, nondiff_argnums=(1,))
def shift_rows(x, s):
    return _shift_rows_impl(x, s, False)


def _shift_rows_fwd(x, s):
    return _shift_rows_impl(x, s, False), None


def _shift_rows_bwd(s, _, g):
    return (_shift_rows_impl(g, s, True),)


shift_rows.defvjp(_shift_rows_fwd, _shift_rows_bwd)


def _rms(x, g):
    return x * lax.rsqrt(jnp.mean(x * x, axis=-1, keepdims=True) + NORM_EPS) * g


def _head_rms(x, e_mat, inv_n, g):
    ms = cdot(x * x, e_mat) * inv_n
    return x * lax.rsqrt(ms + NORM_EPS) * g


def _rope(y, p_mat, cos, sin):
    return y * cos + cdot(y, p_mat) * sin


def _constants():
    e = np.zeros((QKW, QKW), np.float32)
    p = np.zeros((QKW, QKW), np.float32)
    r = np.zeros((LANES, QKW), np.float32)
    pl64 = np.zeros((N_HEADS * HEAD_V, QKW), np.float32)
    half = MLA_ROPE_DIM // 2
    for h in range(N_HEADS):
        e[h * HP:(h + 1) * HP, h * HP:(h + 1) * HP] = 1.0
        for i in range(half):
            x1, x2 = h * HP + MLA_NOPE_DIM + i, h * HP + MLA_NOPE_DIM + half + i
            p[x2, x1] = -1.0
            p[x1, x2] = 1.0
        for i in range(MLA_ROPE_DIM):
            r[i, h * HP + MLA_NOPE_DIM + i] = 1.0
        for i in range(HEAD_V):
            pl64[h * HEAD_V + i, h * HP + i] = 1.0
    return tuple(jnp.asarray(a, BF16) for a in (e, p, r, pl64))


def _inv_freq_row():
    inv = ROPE_THETA ** (-jnp.arange(0, MLA_ROPE_DIM, 2, dtype=F32) / MLA_ROPE_DIM)
    head = jnp.concatenate([jnp.zeros((MLA_NOPE_DIM,), F32), inv, inv, jnp.zeros((HP - MLA_QK_DIM,), F32)])
    return jnp.tile(head, N_HEADS).reshape(1, QKW)


def _mixer_pre(c_q, c_kv, kr, fq, fk, qlg, w_uq, kvlg, w_uk, w_uv, mqg, mkg, fqg, fkg, cos, sin, e_mat, p_mat, r_mat, pl_mat):
    q = bdot(_rms(c_q, qlg), w_uq)
    ckvn = _rms(c_kv, kvlg)
    k = bdot(ckvn, w_uk) + cdot(kr, r_mat)
    v = bdot(ckvn, w_uv)
    q = _rope(_head_rms(q, e_mat, 1.0 / MLA_QK_DIM, mqg), p_mat, cos, sin)
    k = _rope(_head_rms(k, e_mat, 1.0 / MLA_QK_DIM, mkg), p_mat, cos, sin)
    qf = _head_rms(cdot(fq, pl_mat), e_mat, 1.0 / HEAD_V, fqg)
    kf = _head_rms(cdot(fk, pl_mat), e_mat, 1.0 / HEAD_V, fkg)
    return q, k, v, qf, kf


def _s5_post(state, u, c_blk, d_row, w_glu, b_glu):
    y = bdot(state, c_blk) + d_row * u
    y = jax.nn.gelu(y)
    return y * jax.nn.sigmoid(bdot(y, w_glu) + b_glu)


def _merge(o_a, o_b, o_c, g_a, g_b, g_c, w_a, w_b, w_c):
    return (jax.nn.sigmoid(g_a) * bdot(o_a, w_a) + jax.nn.sigmoid(g_b) * bdot(o_b, w_b)
            + jax.nn.sigmoid(g_c) * bdot(o_c, w_c))


def _ffn_mid(gate, val, wg0, wg1, wg2, wv0, wv1, wv2):
    cg = wg0 * shift_rows(gate, 2) + wg1 * shift_rows(gate, 1) + wg2 * gate
    cv = wv0 * shift_rows(val, 2) + wv1 * shift_rows(val, 1) + wv2 * val
    return jax.nn.silu(cg) * cv


def _s5_params(lam_re, lam_im, log_step, b_re, b_im):
    step = jnp.exp(log_step)
    zr, zi = lam_re * step, lam_im * step
    mag = jnp.exp(zr)
    lr, li = mag * jnp.cos(zi), mag * jnp.sin(zi)
    nr, ni = lr - 1.0, li
    den = lam_re * lam_re + lam_im * lam_im
    cr = (nr * lam_re + ni * lam_im) / den
    ci = (ni * lam_re - nr * lam_im) / den
    return lr, li, cr * b_re - ci * b_im, cr * b_im + ci * b_re


def _adamw(w, g, m, v):
    m = ADAM_B1 * m + (1.0 - ADAM_B1) * g
    v = ADAM_B2 * v + (1.0 - ADAM_B2) * (g * g)
    m_hat = m / (1.0 - ADAM_B1 ** ADAM_STEP)
    v_hat = v / (1.0 - ADAM_B2 ** ADAM_STEP)
    delta = -ADAM_LR * (m_hat / (jnp.sqrt(v_hat) + ADAM_EPS) + ADAM_WD * w)
    return delta, m, v


SCAN_W = 256


def _scan_seq(xr, xi, ar, ai, reverse):
    n = xr.shape[0]
    s = 1
    while s < n:
        sr, si = _shift_rows_impl(xr, s, reverse), _shift_rows_impl(xi, s, reverse)
        xr, xi = xr + ar * sr - ai * si, xi + ar * si + ai * sr
        ar, ai = ar * ar - ai * ai, 2.0 * ar * ai
        s *= 2
    return xr, xi


def _scan_fwd(bu, lbar, bsz, name, rider=None):
    t = bu.shape[0]
    seq = t // bsz
    nb = bu.shape[1] // (2 * SCAN_W)

    def fn(b, a):
        xr, xi = _scan_seq(b[:, :SCAN_W], b[:, SCAN_W:], a[:, :SCAN_W], a[:, SCAN_W:], False)
        return jnp.concatenate([xr, xi], axis=1)

    blk = (seq, 2 * SCAN_W)
    return _tile_call(fn, name, (bsz, nb),
                      [(bu, blk, lambda i, j: (i, j)), (lbar, (1, 2 * SCAN_W), lambda i, j: (0, j))],
                      [((t, bu.shape[1]), F32, blk, lambda i, j: (i, j), None)], rider=rider)


def _scan_bwd(dstate, state, lbar, bsz, name, rider=None):
    t = state.shape[0]
    seq = t // bsz
    nb = state.shape[1] // (2 * SCAN_W)

    def fn(g, x, a):
        ar, ai = a[:, :SCAN_W], a[:, SCAN_W:]
        gr, gi = _scan_seq(g[:, :SCAN_W], g[:, SCAN_W:], ar, -ai, True)
        pr, pi = _shift_rows_impl(x[:, :SCAN_W], 1, False), _shift_rows_impl(x[:, SCAN_W:], 1, False)
        dar = jnp.sum(gr * pr + gi * pi, axis=0, keepdims=True)
        dai = jnp.sum(gi * pr - gr * pi, axis=0, keepdims=True)
        return jnp.concatenate([gr, gi], axis=1), jnp.concatenate([dar, dai], axis=1)

    blk = (seq, 2 * SCAN_W)
    return _tile_call(fn, name, (nb, bsz),
                      [(dstate, blk, lambda j, i: (i, j)), (state, blk, lambda j, i: (i, j)),
                       (lbar, (1, 2 * SCAN_W), lambda j, i: (0, j))],
                      [((t, state.shape[1]), F32, blk, lambda j, i: (i, j), None),
                       ((1, state.shape[1]), F32, (1, 2 * SCAN_W), lambda j, i: (0, j), (1,))], rider=rider)


def _shift_lanes(x, s, reverse):
    n = x.shape[1]
    idx = lax.broadcasted_iota(jnp.int32, x.shape, 1)
    if reverse:
        return jnp.where(idx < n - s, pltpu.roll(x, n - s, 1), 0.0)
    return jnp.where(idx >= s, pltpu.roll(x, s, 1), 0.0)


def _cumsum_lanes(x, reverse):
    s = 1
    while s < x.shape[1]:
        x = x + _shift_lanes(x, s, reverse)
        s *= 2
    return x


def _forget_fwd(z, bias, name):
    def fn(zv, bv):
        x = zv + bv
        logf = jnp.minimum(x, 0.0) - jnp.log(1.0 + jnp.exp(-jnp.abs(x)))
        return _cumsum_lanes(logf, False)

    return _tile_call(fn, name, (1,), [_par(z), _par(bias)],
                      [(z.shape, F32, z.shape, lambda i: (0, 0), None)])[0]


def _forget_bwd(dc, z, bias, name):
    def fn(dcv, zv, bv):
        x = zv + bv
        dlogf = _cumsum_lanes(dcv, True)
        dz = dlogf * jax.nn.sigmoid(-x)
        return dz, jnp.sum(dz, axis=1, keepdims=True)

    rows = z.shape[0]
    return _tile_call(fn, name, (1,), [_par(dc), _par(z), _par(bias)],
                      [(z.shape, F32, z.shape, lambda i: (0, 0), None),
                       ((rows, 1), F32, (rows, 1), lambda i: (0, 0), None)])


def _col(v2, x):
    idx = lax.broadcasted_iota(jnp.int32, v2.shape, 1)
    return jnp.sum(jnp.where(idx == x, v2, 0.0), axis=1, keepdims=True)


def _two_cols(c0, c1):
    idx = lax.broadcasted_iota(jnp.int32, (c0.shape[0], 2), 1)
    return jnp.where(idx == 0, c0, c1)


def _attn_fwd(q, k, v, v_blk0, cq, ck, scale, name, rider=None):
    bsz, seq, _ = q.shape
    nq = seq // TQ
    has_bias = cq is not None

    def body(*refs):
        if has_bias:
            q_ref, k_ref, v_ref, cq_ref, ck_ref, o_ref, lse_ref = refs
        else:
            q_ref, k_ref, v_ref, o_ref, lse_ref = refs
        qi = pl.program_id(2)
        rows = qi * TQ + lax.broadcasted_iota(jnp.int32, (TQ, TQ), 0)
        cols0 = lax.broadcasted_iota(jnp.int32, (TQ, TQ), 1)
        lane = lax.broadcasted_iota(jnp.int32, (TQ, LANES), 1)
        outs, lses = [], []
        for x in range(2):
            qx = q_ref[:, x * HP:(x + 1) * HP].astype(BF16)
            cqx = _col(cq_ref[...], x) if has_bias else None

            def step(j, carry, x=x, qx=qx, cqx=cqx):
                m, l, acc = carry
                ks = pl.multiple_of(j * TQ, TQ)
                kx = k_ref[pl.ds(ks, TQ), x * HP:(x + 1) * HP].astype(BF16)
                s = lax.dot_general(qx, kx, NT, preferred_element_type=F32) * scale
                if has_bias:
                    s = s + (cqx - ck_ref[x:x + 1, pl.ds(ks, TQ)])
                s = jnp.where(rows >= cols0 + ks, s, NEG_INF)
                m_new = jnp.maximum(m, jnp.max(s, axis=1, keepdims=True))
                alpha = jnp.exp(m - m_new)
                p = jnp.exp(s - m_new)
                l = alpha * l + jnp.sum(p, axis=1, keepdims=True)
                acc = alpha * acc + lax.dot_general(p.astype(BF16), v_ref[pl.ds(ks, TQ), :].astype(BF16), NN,
                                                    preferred_element_type=F32)
                return m_new, l, acc

            init = (jnp.full((TQ, 1), NEG_INF, F32), jnp.zeros((TQ, 1), F32), jnp.zeros((TQ, LANES), F32))
            m, l, acc = lax.fori_loop(0, qi + 1, step, init)
            outs.append(acc / l)
            lses.append(m + jnp.log(l))
        o_ref[...] = jnp.where(lane < HEAD_V, outs[0], outs[1])
        lse_ref[...] = _two_cols(lses[0], lses[1])

    in_specs = [pl.BlockSpec((None, TQ, 2 * HP), lambda b, p, i: (b, i, p)),
                pl.BlockSpec((None, seq, 2 * HP), lambda b, p, i: (b, 0, p)),
                pl.BlockSpec((None, seq, LANES), lambda b, p, i: (b, 0, v_blk0 + p))]
    args = [q, k, v]
    if has_bias:
        in_specs += [pl.BlockSpec((None, None, TQ, 2), lambda b, p, i: (b, p, i, 0)),
                     pl.BlockSpec((None, None, 2, seq), lambda b, p, i: (b, p, 0, 0))]
        args += [cq, ck]
    return _pcall(body, name=name, grid=(bsz, 2, nq), in_specs=in_specs,
                  out_specs=[pl.BlockSpec((None, TQ, LANES), lambda b, p, i: (b, i, p)),
                             pl.BlockSpec((None, None, TQ, 2), lambda b, p, i: (b, p, i, 0))],
                  out_shape=[jax.ShapeDtypeStruct((bsz, seq, 2 * LANES), F32),
                             jax.ShapeDtypeStruct((bsz, 2, seq, 2), F32)], rider=rider)(*args)


def _attn_bwd(q, k, v, v_blk0, o, do, lse, cq, ck, scale, name, rider=None):
    bsz, seq, _ = q.shape
    nq = seq // TQ
    has_bias = cq is not None

    def body(*refs):
        if has_bias:
            q_ref, k_ref, v_ref, o_ref, do_ref, lse_ref, cq_ref, ck_ref, dq_ref, dk_ref, dv_ref, dcq_ref, dck_ref = refs
        else:
            q_ref, k_ref, v_ref, o_ref, do_ref, lse_ref, dq_ref, dk_ref, dv_ref = refs
        qi = pl.program_id(2)

        @pl.when(qi == 0)
        def _():
            dk_ref[...] = jnp.zeros_like(dk_ref)
            dv_ref[...] = jnp.zeros_like(dv_ref)
            if has_bias:
                dck_ref[...] = jnp.zeros_like(dck_ref)

        rows = qi * TQ + lax.broadcasted_iota(jnp.int32, (TQ, TQ), 0)
        cols0 = lax.broadcasted_iota(jnp.int32, (TQ, TQ), 1)
        lane = lax.broadcasted_iota(jnp.int32, (TQ, LANES), 1)
        do = do_ref[...]
        prod = do * o_ref[...]
        lse2 = lse_ref[...]
        dcqs = []
        for x in range(2):
            hm = jnp.logical_and(lane >= x * HEAD_V, lane < (x + 1) * HEAD_V)
            dox = jnp.where(hm, do, 0.0).astype(BF16)
            delta = jnp.sum(jnp.where(hm, prod, 0.0), axis=1, keepdims=True)
            qx = q_ref[:, x * HP:(x + 1) * HP].astype(BF16)
            lsex = _col(lse2, x)
            cqx = _col(cq_ref[...], x) if has_bias else None

            def step(j, carry, x=x, dox=dox, delta=delta, qx=qx, lsex=lsex, cqx=cqx):
                dq_acc, dcq_acc = carry
                ks = pl.multiple_of(j * TQ, TQ)
                kx = k_ref[pl.ds(ks, TQ), x * HP:(x + 1) * HP].astype(BF16)
                vb = v_ref[pl.ds(ks, TQ), :].astype(BF16)
                s = lax.dot_general(qx, kx, NT, preferred_element_type=F32) * scale
                if has_bias:
                    s = s + (cqx - ck_ref[x:x + 1, pl.ds(ks, TQ)])
                s = jnp.where(rows >= cols0 + ks, s, NEG_INF)
                p = jnp.exp(s - lsex)
                dp = lax.dot_general(dox, vb, NT, preferred_element_type=F32)
                ds = p * (dp - delta)
                dsb = ds.astype(BF16)
                dq_acc = dq_acc + lax.dot_general(dsb, kx, NN, preferred_element_type=F32) * scale
                dk_ref[pl.ds(ks, TQ), x * HP:(x + 1) * HP] += (
                    lax.dot_general(dsb, qx, TN, preferred_element_type=F32) * scale)
                dv_ref[pl.ds(ks, TQ), :] += lax.dot_general(p.astype(BF16), dox, TN, preferred_element_type=F32)
                if has_bias:
                    dcq_acc = dcq_acc + jnp.sum(ds, axis=1, keepdims=True)
                    dck_ref[x:x + 1, pl.ds(ks, TQ)] -= jnp.sum(ds, axis=0, keepdims=True)
                return dq_acc, dcq_acc

            dq_x, dcq_x = lax.fori_loop(0, qi + 1, step, (jnp.zeros((TQ, HP), F32), jnp.zeros((TQ, 1), F32)))
            dq_ref[:, x * HP:(x + 1) * HP] = dq_x
            dcqs.append(dcq_x)
        if has_bias:
            dcq_ref[...] = _two_cols(dcqs[0], dcqs[1])

    qblk = pl.BlockSpec((None, TQ, 2 * HP), lambda b, p, i: (b, i, p))
    kblk = pl.BlockSpec((None, seq, 2 * HP), lambda b, p, i: (b, 0, p))
    oblk = pl.BlockSpec((None, TQ, LANES), lambda b, p, i: (b, i, p))
    cqblk = pl.BlockSpec((None, None, TQ, 2), lambda b, p, i: (b, p, i, 0))
    ckblk = pl.BlockSpec((None, None, 2, seq), lambda b, p, i: (b, p, 0, 0))
    in_specs = [qblk, kblk, pl.BlockSpec((None, seq, LANES), lambda b, p, i: (b, 0, v_blk0 + p)), oblk, oblk, cqblk]
    out_specs = [qblk, kblk, pl.BlockSpec((None, seq, LANES), lambda b, p, i: (b, 0, p))]
    out_shape = [jax.ShapeDtypeStruct((bsz, seq, QKW), F32), jax.ShapeDtypeStruct((bsz, seq, QKW), F32),
                 jax.ShapeDtypeStruct((bsz, seq, 2 * LANES), F32)]
    args = [q, k, v, o, do, lse]
    if has_bias:
        in_specs += [cqblk, ckblk]
        out_specs += [cqblk, ckblk]
        out_shape += [jax.ShapeDtypeStruct((bsz, 2, seq, 2), F32), jax.ShapeDtypeStruct((bsz, 2, 2, seq), F32)]
        args += [cq, ck]
    return _pcall(body, name=name, grid=(bsz, 2, nq), in_specs=in_specs, out_specs=out_specs,
                  out_shape=out_shape, rider=rider)(*args)


def _coords(idx):
    return (idx // 4, (idx // 2) % 2, idx % 2)


class _Xfer:
    def __init__(self, srcs, out_shapes, plan):
        self.srcs, self.out_shapes, self.plan = list(srcs), list(out_shapes), plan
        self.n = len(self.srcs)
        me0 = jnp.int32(0)
        n_pieces = [len(plan(a, me0, me0)) for a in range(self.n)]
        self.offs = np.concatenate([[0], np.cumsum(n_pieces)]).astype(int)
        self.total = int(self.offs[-1])

    def sems(self):
        remote = (N_DEV - 1) * self.total
        return [pltpu.SemaphoreType.DMA((remote,)), pltpu.SemaphoreType.DMA((remote,)),
                pltpu.SemaphoreType.DMA((self.total,))]

    def copies(self, src_refs, out_refs, send_sems, recv_sems, local_sems):
        me = 4 * lax.axis_index("x") + 2 * lax.axis_index("y") + lax.axis_index("c")
        out = []
        for a in range(self.n):
            for pi, (si, di) in enumerate(self.plan(a, me, me)):
                out.append((pltpu.make_async_copy(src_refs[a].at[si], out_refs[a].at[di],
                                                  local_sems.at[self.offs[a] + pi]), None))
        for kk in range(1, N_DEV):
            dest = (me + kk) % N_DEV
            src_dev = (me + N_DEV - kk) % N_DEV
            for a in range(self.n):
                pieces = self.plan(a, me, dest)
                landing = self.plan(a, src_dev, me)
                for pi, ((si, di), (_, li)) in enumerate(zip(pieces, landing)):
                    sem = (kk - 1) * self.total + self.offs[a] + pi
                    mk = functools.partial(pltpu.make_async_remote_copy, src_ref=src_refs[a].at[si],
                                           send_sem=send_sems.at[sem], recv_sem=recv_sems.at[sem],
                                           device_id=_coords(dest), device_id_type=pl.DeviceIdType.MESH)
                    out.append((mk(dst_ref=out_refs[a].at[di]), mk(dst_ref=out_refs[a].at[li])))
        return out

    @staticmethod
    def start(copies):
        for cp, _ in copies:
            cp.start()

    @staticmethod
    def wait(copies):
        for cp, rc in copies:
            if rc is None:
                cp.wait()
            else:
                cp.wait_send()
                rc.wait_recv()


def _exchange(name, xf):
    n = xf.n

    def body(*refs):
        copies = xf.copies(refs[:n], refs[n:2 * n], *refs[2 * n:])
        xf.start(copies)
        xf.wait(copies)

    any_spec = pl.BlockSpec(memory_space=pl.ANY)
    return pl.pallas_call(body, name=name, in_specs=[any_spec] * n, out_specs=[any_spec] * n,
                          out_shape=[jax.ShapeDtypeStruct(s, d) for s, d in xf.out_shapes],
                          scratch_shapes=xf.sems(),
                          compiler_params=pltpu.CompilerParams(has_side_effects=True))(*xf.srcs)


def _rows_of(j, n):
    return pl.ds(pl.multiple_of(j * n, n), n)


def _ffn_cols(j, half):
    return pl.multiple_of((2 * (j % 4) + half) * (2 * FF_BLK) + (j // 4) * FF_BLK, FF_BLK)


def _gather_plan(shards):
    names = list(shards)

    def plan(a, src_dev, _dest):
        nm = names[a]
        j = src_dev
        if nm in ("w_in", "w_uq", "w_ukv", "w_glu", "w_out", "w_down"):
            rows = shards[nm].shape[0]
            return [((slice(None), slice(None)), (_rows_of(j, rows), slice(None)))]
        if nm == "w_branch":
            return [((slice(None), slice(None), slice(None)), (slice(None), slice(None), _rows_of(j, LANES)))]
        if nm in ("w_up", "conv"):
            return [((slice(None), pl.ds(h * FF_BLK, FF_BLK)), (slice(None), pl.ds(_ffn_cols(j, h), FF_BLK)))
                    for h in range(2)]
        raise KeyError(nm)

    return names, plan


def _scatter_plan(names, shard_shapes):
    def plan(a, src_dev, dest):
        nm = names[a]
        j = dest
        if nm in ("w_in", "w_uq", "w_ukv", "w_glu", "w_out", "w_down"):
            rows = shard_shapes[nm][0]
            return [((_rows_of(j, rows), slice(None)), (src_dev, slice(None), slice(None)))]
        if nm == "w_branch":
            return [((slice(None), slice(None), _rows_of(j, LANES)), (src_dev, slice(None), slice(None), slice(None)))]
        if nm in ("w_up", "conv"):
            return [((slice(None), pl.ds(_ffn_cols(j, h), FF_BLK)), (src_dev, slice(None), pl.ds(h * FF_BLK, FF_BLK)))
                    for h in range(2)]
        if nm == "small":
            return [((slice(None), slice(None)), (src_dev, slice(None), slice(None)))]
        raise KeyError(nm)

    return plan


def _pad_in_cols(w):
    lead = w.shape[:-1]
    parts, pos = [], 0
    for dst, src, width in IN_SEGS:
        if dst > pos:
            parts.append(jnp.zeros(lead + (dst - pos,), w.dtype))
        parts.append(w[..., src:src + width])
        pos = dst + width
    return jnp.concatenate(parts, axis=-1)


def _unpad_in_cols(w):
    order = sorted(IN_SEGS, key=lambda s: s[1])
    return jnp.concatenate([w[..., dst:dst + width] for dst, _, width in order], axis=-1)


def _pad_ff(w, axis):
    n = w.shape[axis] // FF_HALF
    parts = []
    for h in range(n):
        piece = lax.slice_in_dim(w, h * FF_HALF, (h + 1) * FF_HALF, axis=axis)
        zshape = list(w.shape)
        zshape[axis] = FF_BLK - FF_HALF
        parts += [piece, jnp.zeros(zshape, w.dtype)]
    return jnp.concatenate(parts, axis=axis)


def _unpad_ff(w, axis):
    n = w.shape[axis] // FF_BLK
    return jnp.concatenate([lax.slice_in_dim(w, h * FF_BLK, h * FF_BLK + FF_HALF, axis=axis) for h in range(n)],
                           axis=axis)


def _s5_perm():
    i = np.arange(S5_N)
    re = (i // SCAN_W) * (2 * SCAN_W) + (i % SCAN_W)
    return re, re + SCAN_W


def _to_scan_cols(re_part, im_part):
    lead = re_part.shape[:-1]
    nb = S5_N // SCAN_W
    r = re_part.reshape(lead + (nb, SCAN_W))
    i = im_part.reshape(lead + (nb, SCAN_W))
    return jnp.concatenate([r, i], axis=-1).reshape(lead + (2 * S5_N,))


def _from_scan_cols(x):
    lead = x.shape[:-1]
    nb = S5_N // SCAN_W
    y = x.reshape(lead + (nb, 2, SCAN_W))
    return y[..., 0, :].reshape(lead + (S5_N,)), y[..., 1, :].reshape(lead + (S5_N,))


def _block_diag_in(b):
    eye = jnp.eye(S5_GROUPS, dtype=b.dtype)
    return jnp.einsum("gph,gk->ghkp", b, eye).reshape(S5_WIDTH, S5_N)


def _block_diag_in_t(m):
    eye = jnp.eye(S5_GROUPS, dtype=m.dtype)
    return jnp.einsum("ghkp,gk->gph", m.reshape(S5_GROUPS, S5_GROUP_CH, S5_GROUPS, S5_STATE), eye)


def _block_diag_out(c):
    eye = jnp.eye(S5_GROUPS, dtype=c.dtype)
    return jnp.einsum("ghp,gk->gpkh", c, eye).reshape(S5_N, S5_WIDTH)


def _block_diag_out_t(m):
    eye = jnp.eye(S5_GROUPS, dtype=m.dtype)
    return jnp.einsum("gpkh,gk->ghp", m.reshape(S5_GROUPS, S5_STATE, S5_GROUPS, S5_GROUP_CH), eye)


def _pad_heads(g, width):
    g = jnp.broadcast_to(g.reshape(-1, width), (N_HEADS, width))
    return jnp.pad(g, ((0, 0), (0, HP - width))).reshape(1, QKW)


def _prep_layer(l, gw, small, consts):
    e_mat, p_mat, r_mat, pl_mat = consts
    p = {}
    p["attn_g"] = small["attn_norm_g"][l].reshape(1, D_MODEL)
    p["ffn_g"] = small["ffn_norm_g"][l].reshape(1, D_MODEL)
    p["qlg"] = small["q_lat_norm_g"][l].reshape(1, MLA_Q_RANK)
    p["kvlg"] = small["kv_lat_norm_g"][l].reshape(1, MLA_KV_RANK)
    p["mqg"] = _pad_heads(small["mla_q_norm_g"][l], MLA_QK_DIM)
    p["mkg"] = _pad_heads(small["mla_k_norm_g"][l], MLA_QK_DIM)
    p["fqg"] = _pad_heads(small["fox_q_norm_g"][l], HEAD_V)
    p["fkg"] = _pad_heads(small["fox_k_norm_g"][l], HEAD_V)
    w_uq = gw["w_uq"].reshape(MLA_Q_RANK, N_HEADS, MLA_QK_DIM)
    p["w_uq"] = jnp.pad(w_uq, ((0, 0), (0, 0), (0, HP - MLA_QK_DIM))).reshape(MLA_Q_RANK, QKW)
    w_ukv = gw["w_ukv"].reshape(MLA_KV_RANK, N_HEADS, MLA_NOPE_DIM + HEAD_V)
    p["w_uk"] = jnp.pad(w_ukv[..., :MLA_NOPE_DIM], ((0, 0), (0, 0), (0, HP - MLA_NOPE_DIM))).reshape(MLA_KV_RANK, QKW)
    p["w_uv"] = w_ukv[..., MLA_NOPE_DIM:].reshape(MLA_KV_RANK, N_HEADS * HEAD_V)
    p["w_glu"] = gw["w_glu"]
    p["b_glu"] = small["s5_b_glu"][l].reshape(1, S5_WIDTH)
    p["d_row"] = small["s5_d"][l].reshape(1, S5_WIDTH)
    p["lam_re"] = small["s5_lambda_re"][l].reshape(S5_N, 1)
    p["lam_im"] = small["s5_lambda_im"][l].reshape(S5_N, 1)
    p["log_step"] = jnp.repeat(small["s5_log_step"][l], S5_STATE).reshape(S5_N, 1)
    p["b_re"] = small["s5_b_re"][l].reshape(S5_N, S5_GROUP_CH)
    p["b_im"] = small["s5_b_im"][l].reshape(S5_N, S5_GROUP_CH)
    c_re, c_im = small["s5_c_re"][l], small["s5_c_im"][l]
    c_blk = jnp.concatenate([_block_diag_out(c_re), -_block_diag_out(c_im)], axis=0)
    re_idx, im_idx = _s5_perm()
    inv = np.empty(2 * S5_N, np.int64)
    inv[re_idx] = np.arange(S5_N)
    inv[im_idx] = S5_N + np.arange(S5_N)
    p["c_blk"] = c_blk[inv]
    p["fbias"] = small["fox_f_bias"][l]
    for nm in ("w_in", "w_out", "w_up", "w_down", "w_branch", "conv"):
        p[nm] = gw[nm]
    return p


def _s5_param_call(p, l):
    ins = [p["lam_re"], p["lam_im"], p["log_step"], p["b_re"], p["b_im"]]
    outs = [((S5_N, 1), F32, (S5_N, 1), lambda i: (0, 0), None)] * 2 + \
           [((S5_N, S5_GROUP_CH), F32, (S5_N, S5_GROUP_CH), lambda i: (0, 0), None)] * 2
    return _tile_call(_s5_params, "s5_params", (1,), [_par(a) for a in ins], outs)


def _s5_param_bwd_call(p, cts):
    ins = [p["lam_re"], p["lam_im"], p["log_step"], p["b_re"], p["b_im"]]

    def fn(lr, li, ls, br, bi, g0, g1, g2, g3):
        _, vjp = jax.vjp(_s5_params, lr, li, ls, br, bi)
        return vjp((g0, g1, g2, g3))

    outs = [((S5_N, 1), F32, (S5_N, 1), lambda i: (0, 0), None)] * 3 + \
           [((S5_N, S5_GROUP_CH), F32, (S5_N, S5_GROUP_CH), lambda i: (0, 0), None)] * 2
    return _tile_call(fn, "s5_params_bwd", (1,), [_par(a) for a in ins + list(cts)], outs)


def _layer_fwd(x_prev, f_prev, p, tabs, consts, bsz, riders):
    xout = {}

    def hosted(name, res, n_out):
        xout[name] = res[n_out:]
        return res[:n_out]

    t = x_prev.shape[0]
    seq = t // bsz
    nt = t // TM
    cos, sin = tabs
    e_mat, p_mat, r_mat, pl_mat = consts
    sv = {}

    if f_prev is None:
        x = x_prev
        h = _tile_call(lambda xv, g: _rms(xv, g), "norm_first", (nt,), [_row(x), _par(p["attn_g"])],
                       [_orow(t, D_MODEL)])[0]
    else:
        x, h = _tile_call(lambda xv, fv, g: (xv + fv, _rms(xv + fv, g)), "norm_attn", (nt,),
                          [_row(x_prev), _row(f_prev), _par(p["attn_g"])], [_orow(t, D_MODEL), _orow(t, D_MODEL)])
    sv["x"], sv["h"] = x, h
    proj = _mm(h, p["w_in"], "nn", "mm_in")
    sv["proj"] = proj

    def pre_fn(pa, cosv, sinv, qlg, w_uq, kvlg, w_uk, w_uv, mqg, mkg, fqg, fkg, em, pm, rm, plm):
        return _mixer_pre(pa[:, O_CQ:O_CQ + MLA_Q_RANK], pa[:, O_CKV:O_CKV + MLA_KV_RANK], pa[:, O_KR:O_KR + LANES],
                          pa[:, O_FQ:O_FQ + 256], pa[:, O_FK:O_FK + 256], qlg, w_uq, kvlg, w_uk, w_uv,
                          mqg, mkg, fqg, fkg, cosv, sinv, em, pm, rm, plm)

    pre_params = [p["qlg"], p["w_uq"], p["kvlg"], p["w_uk"], p["w_uv"], p["mqg"], p["mkg"], p["fqg"], p["fkg"],
                  e_mat, p_mat, r_mat, pl_mat]
    q_m, k_m, v_m, q_f, k_f = hosted("mixer_pre", _tile_call(
        pre_fn, "mixer_pre", (nt,), [_row(proj, PA), _row(cos), _row(sin)] + [_par(a) for a in pre_params],
        [_orow(t, QKW), _orow(t, QKW), _orow(t, 2 * LANES), _orow(t, QKW), _orow(t, QKW)],
        rider=riders.get("mixer_pre")), 5)
    sv.update(q_m=q_m, k_m=k_m, v_m=v_m, q_f=q_f, k_f=k_f)

    z = proj[:, O_FL:O_FL + N_HEADS].reshape(bsz, seq, N_HEADS).transpose(0, 2, 1).reshape(bsz * N_HEADS, seq)
    fb = jnp.tile(p["fbias"], bsz).reshape(bsz * N_HEADS, 1)
    cum = _forget_fwd(z, fb, "forget_fwd")
    ck = cum.reshape(bsz, 2, 2, seq)
    cq = ck.transpose(0, 1, 3, 2)
    sv.update(z=z, fb=fb, cq=cq, ck=ck)

    r3 = lambda a: a.reshape(bsz, seq, a.shape[-1])
    o_m, lse_m = hosted("attn_mla_fwd", _attn_fwd(r3(q_m), r3(k_m), r3(v_m), 0, None, None, MLA_QK_DIM ** -0.5,
                                                  "attn_mla_fwd", riders.get("attn_mla_fwd")), 2)
    o_f, lse_f = hosted("attn_fox_fwd", _attn_fwd(r3(q_f), r3(k_f), r3(proj), O_FV // LANES, cq, ck, HEAD_V ** -0.5,
                                                  "attn_fox_fwd", riders.get("attn_fox_fwd")), 2)
    o_m, o_f = o_m.reshape(t, 2 * LANES), o_f.reshape(t, 2 * LANES)
    sv.update(o_m=o_m, o_f=o_f, lse_m=lse_m, lse_f=lse_f)

    lbar_re, lbar_im, bbar_re, bbar_im = _s5_param_call(p, 0)
    lbar = _to_scan_cols(lbar_re.reshape(1, S5_N), lbar_im.reshape(1, S5_N))
    b_blk = _to_scan_cols(_block_diag_in(bbar_re.reshape(S5_GROUPS, S5_STATE, S5_GROUP_CH)),
                          _block_diag_in(bbar_im.reshape(S5_GROUPS, S5_STATE, S5_GROUP_CH)))
    sv.update(lbar=lbar, b_blk=b_blk)
    bu = _tile_call(lambda u, bb: bdot(u, bb), "s5_bu", (nt,), [_row(proj, S5_WIDTH, O_U // S5_WIDTH), _par(b_blk)],
                    [_orow(t, 2 * S5_N)])[0]
    state = hosted("s5_scan_fwd", _scan_fwd(bu, lbar, bsz, "s5_scan_fwd", riders.get("s5_scan_fwd")), 1)[0]
    sv["state"] = state
    post_params = [p["c_blk"], p["d_row"], p["w_glu"], p["b_glu"]]
    o_s = _tile_call(_s5_post, "s5_post", (nt,),
                     [_row(state), _row(proj, S5_WIDTH, O_U // S5_WIDTH)] + [_par(a) for a in post_params],
                     [_orow(t, S5_WIDTH)])[0]
    sv["o_s"] = o_s

    def merge_fn(oa, ob, oc, ga, gb, gc, wb):
        return _merge(oa, ob, oc, ga, gb, gc, wb[0], wb[1], wb[2])

    gate_specs = [_row(proj, D_MODEL, PA // D_MODEL + n) for n in range(3)]
    merged = _tile_call(merge_fn, "merge", (nt,), [_row(o_m), _row(o_f), _row(o_s)] + gate_specs + [_par(p["w_branch"])],
                        [_orow(t, D_MODEL)])[0]
    sv["merged"] = merged
    attn_out = _mm(merged, p["w_out"], "nn", "mm_out")

    x1, h2 = _tile_call(lambda xv, av, g: (xv + av, _rms(xv + av, g)), "norm_ffn", (nt,),
                        [_row(x), _row(attn_out), _par(p["ffn_g"])], [_orow(t, D_MODEL), _orow(t, D_MODEL)])
    sv["x1"], sv["h2"] = x1, h2
    up = _mm(h2, p["w_up"], "nn", "mm_up")
    sv["up"] = up

    def ffn_fn(upv, cw):
        return _ffn_mid(upv[:, :FF_BLK], upv[:, FF_BLK:], cw[0:1, :FF_BLK], cw[1:2, :FF_BLK], cw[2:3, :FF_BLK],
                        cw[0:1, FF_BLK:], cw[1:2, FF_BLK:], cw[2:3, FF_BLK:])

    nblk = FF_PW // FF_BLK
    act = _tile_call(ffn_fn, "ffn_mid", (nblk, bsz),
                     [(up, (seq, 2 * FF_BLK), lambda j, b: (b, j)), (p["conv"], (3, 2 * FF_BLK), lambda j, b: (0, j))],
                     [((t, FF_PW), F32, (seq, FF_BLK), lambda j, b: (b, j), None)])[0]
    sv["act"] = act
    ffn_out = _mm(act, p["w_down"], "nn", "mm_down")
    return x1, ffn_out, sv, xout


def _layer_bwd(dx2, p, sv, tabs, consts, bsz, riders):
    xout = {}

    def hosted(name, res, n_out):
        xout[name] = res[n_out:]
        return res[:n_out]

    t = dx2.shape[0]
    seq = t // bsz
    nt = t // TM
    cos, sin = tabs
    e_mat, p_mat, r_mat, pl_mat = consts
    g = {}

    g["w_down"] = _mm(sv["act"], dx2, "tn", "mm_down_dw", out_dtype=BF16)
    dact = _mm(dx2, p["w_down"], "nt", "mm_down_dx")

    def ffn_bwd_fn(upv, cw, da):
        args = (upv[:, :FF_BLK], upv[:, FF_BLK:], cw[0:1, :FF_BLK], cw[1:2, :FF_BLK], cw[2:3, :FF_BLK],
                cw[0:1, FF_BLK:], cw[1:2, FF_BLK:], cw[2:3, FF_BLK:])
        _, vjp = jax.vjp(_ffn_mid, *args)
        dg, dv, g0, g1, g2, v0, v1, v2 = vjp(da)
        return (jnp.concatenate([dg, dv], axis=1), jnp.concatenate([g0, v0], axis=1),
                jnp.concatenate([g1, v1], axis=1), jnp.concatenate([g2, v2], axis=1))

    nblk = FF_PW // FF_BLK
    cw_out = ((1, UP_W), F32, (1, 2 * FF_BLK), lambda j, b: (0, j), (1,))
    dup, dc0, dc1, dc2 = hosted("ffn_mid_bwd", _tile_call(
        ffn_bwd_fn, "ffn_mid_bwd", (nblk, bsz),
        [(sv["up"], (seq, 2 * FF_BLK), lambda j, b: (b, j)), (p["conv"], (3, 2 * FF_BLK), lambda j, b: (0, j)),
         (dact, (seq, FF_BLK), lambda j, b: (b, j))],
        [((t, UP_W), F32, (seq, 2 * FF_BLK), lambda j, b: (b, j), None), cw_out, cw_out, cw_out],
        rider=riders.get("ffn_mid_bwd")), 4)
    g["conv"] = jnp.concatenate([dc0, dc1, dc2], axis=0)
    g["w_up"] = _mm(sv["h2"], dup, "tn", "mm_up_dw", out_dtype=BF16)
    dh2 = _mm(dup, p["w_up"], "nt", "mm_up_dx")

    def norm_bwd_fn(xv, gv, dh, dres):
        _, vjp = jax.vjp(_rms, xv, gv)
        dxv, dg = vjp(dh)
        return dres + dxv, dg

    d1, g["ffn_g"] = _tile_call(norm_bwd_fn, "norm_bwd", (nt,),
                                [_row(sv["x1"]), _par(p["ffn_g"]), _row(dh2), _row(dx2)],
                                [_orow(t, D_MODEL), _oacc((1, D_MODEL))])

    g["w_out"] = _mm(sv["merged"], d1, "tn", "mm_out_dw", out_dtype=BF16)
    dmerged = _mm(d1, p["w_out"], "nt", "mm_out_dx")

    def merge_bwd_fn(oa, ob, oc, ga, gb, gc, wb, dm):
        wf = wb.astype(F32)
        _, vjp = jax.vjp(_merge, oa, ob, oc, ga, gb, gc, wf[0], wf[1], wf[2])
        doa, dob, doc, dga, dgb, dgc, dwa, dwb, dwc = vjp(dm)
        return doa, dob, doc, jnp.concatenate([dga, dgb, dgc], axis=1), jnp.stack([dwa, dwb, dwc])

    proj = sv["proj"]
    gate_specs = [_row(proj, D_MODEL, PA // D_MODEL + n) for n in range(3)]
    do_m, do_f, do_s, dgl, g["w_branch"] = _tile_call(
        merge_bwd_fn, "merge_bwd", (nt,),
        [_row(sv["o_m"]), _row(sv["o_f"]), _row(sv["o_s"])] + gate_specs + [_par(p["w_branch"]), _row(dmerged)],
        [_orow(t, 2 * LANES), _orow(t, 2 * LANES), _orow(t, S5_WIDTH), _orow(t, 3 * D_MODEL),
         _oacc((3, S5_WIDTH, D_MODEL))])

    def post_bwd_fn(st, u, cb, dr, wg, bg, do):
        _, vjp = jax.vjp(_s5_post, st, u, cb.astype(F32), dr, wg.astype(F32), bg)
        return vjp(do)

    u_spec = _row(proj, S5_WIDTH, O_U // S5_WIDTH)
    dstate, du1, dc_blk, g_d, g["w_glu"], g_bglu = _tile_call(
        post_bwd_fn, "s5_post_bwd", (nt,),
        [_row(sv["state"]), u_spec, _par(p["c_blk"]), _par(p["d_row"]), _par(p["w_glu"]), _par(p["b_glu"]), _row(do_s)],
        [_orow(t, 2 * S5_N), _orow(t, S5_WIDTH), _oacc((2 * S5_N, S5_WIDTH)), _oacc((1, S5_WIDTH)),
         _oacc((S5_WIDTH, S5_WIDTH)), _oacc((1, S5_WIDTH))])
    dbu, dlbar = hosted("s5_scan_bwd", _scan_bwd(dstate, sv["state"], sv["lbar"], bsz, "s5_scan_bwd",
                                                 riders.get("s5_scan_bwd")), 2)

    def bu_bwd_fn(u, bb, dbv, du_in):
        _, vjp = jax.vjp(bdot, u, bb)
        du, dbb = vjp(dbv)
        return du_in + du, dbb

    du, db_blk = _tile_call(bu_bwd_fn, "s5_bu_bwd", (nt,), [u_spec, _par(sv["b_blk"]), _row(dbu), _row(du1)],
                            [_orow(t, S5_WIDTH), _oacc((S5_WIDTH, 2 * S5_N))])
    dlr, dli = _from_scan_cols(dlbar)
    dbr, dbi = _from_scan_cols(db_blk)
    cts = (dlr.reshape(S5_N, 1), dli.reshape(S5_N, 1),
           _block_diag_in_t(dbr).reshape(S5_N, S5_GROUP_CH), _block_diag_in_t(dbi).reshape(S5_N, S5_GROUP_CH))
    g_lr, g_li, g_ls, g_br, g_bi = _s5_param_bwd_call(p, cts)
    re_idx, im_idx = _s5_perm()
    g["s5_lambda_re"] = g_lr.reshape(S5_GROUPS, S5_STATE)
    g["s5_lambda_im"] = g_li.reshape(S5_GROUPS, S5_STATE)
    g["s5_log_step"] = jnp.sum(g_ls.reshape(S5_GROUPS, S5_STATE), axis=1)
    g["s5_b_re"] = g_br.reshape(S5_GROUPS, S5_STATE, S5_GROUP_CH)
    g["s5_b_im"] = g_bi.reshape(S5_GROUPS, S5_STATE, S5_GROUP_CH)
    g["s5_c_re"] = _block_diag_out_t(dc_blk[re_idx])
    g["s5_c_im"] = -_block_diag_out_t(dc_blk[im_idx])
    g["s5_d"] = g_d.reshape(S5_GROUPS, S5_GROUP_CH)
    g["s5_b_glu"] = g_bglu.reshape(S5_WIDTH)

    r3 = lambda a: a.reshape(bsz, seq, a.shape[-1])
    dq_m, dk_m, dv_m = hosted("attn_mla_bwd", _attn_bwd(
        r3(sv["q_m"]), r3(sv["k_m"]), r3(sv["v_m"]), 0, r3(sv["o_m"]), r3(do_m), sv["lse_m"], None, None,
        MLA_QK_DIM ** -0.5, "attn_mla_bwd", riders.get("attn_mla_bwd")), 3)
    dq_f, dk_f, dv_f, dcq, dck = hosted("attn_fox_bwd", _attn_bwd(
        r3(sv["q_f"]), r3(sv["k_f"]), r3(proj), O_FV // LANES, r3(sv["o_f"]), r3(do_f), sv["lse_f"], sv["cq"], sv["ck"],
        HEAD_V ** -0.5, "attn_fox_bwd", riders.get("attn_fox_bwd")), 5)
    dcum = (dck + dcq.transpose(0, 1, 3, 2)).reshape(bsz * N_HEADS, seq)
    dz, dfb = _forget_bwd(dcum, sv["z"], sv["fb"], "forget_bwd")
    g["fox_f_bias"] = jnp.sum(dfb.reshape(bsz, N_HEADS), axis=0)
    dfl = jnp.pad(dz.reshape(bsz, N_HEADS, seq).transpose(0, 2, 1).reshape(t, N_HEADS), ((0, 0), (0, LANES - N_HEADS)))

    def pre_bwd_fn(pa, cosv, sinv, qlg, w_uq, kvlg, w_uk, w_uv, mqg, mkg, fqg, fkg, em, pm, rm, plm,
                   gq, gk, gv, gqf, gkf, gvf, gu, gfl, ggl):
        f = functools.partial(_mixer_pre, cos=cosv, sin=sinv, e_mat=em, p_mat=pm, r_mat=rm, pl_mat=plm)
        prim = (pa[:, O_CQ:O_CQ + MLA_Q_RANK], pa[:, O_CKV:O_CKV + MLA_KV_RANK], pa[:, O_KR:O_KR + LANES],
                pa[:, O_FQ:O_FQ + 256], pa[:, O_FK:O_FK + 256], qlg, w_uq.astype(F32), kvlg, w_uk.astype(F32),
                w_uv.astype(F32), mqg, mkg, fqg, fkg)
        _, vjp = jax.vjp(lambda *a: f(*a), *prim)
        dcq_, dckv, dkr, dfq, dfk, dqlg, dwuq, dkvlg, dwuk, dwuv, dmqg, dmkg, dfqg, dfkg = vjp((gq, gk, gv, gqf, gkf))
        zpad = jnp.zeros((pa.shape[0], PA - O_CQ - MLA_Q_RANK), F32)
        dproj = jnp.concatenate([dckv, dfq, dfk, gvf, gu, dkr, gfl, dcq_, zpad, ggl], axis=1)
        return dproj, dqlg, dwuq, dkvlg, dwuk, dwuv, dmqg, dmkg, dfqg, dfkg

    pre_params = [p["qlg"], p["w_uq"], p["kvlg"], p["w_uk"], p["w_uv"], p["mqg"], p["mkg"], p["fqg"], p["fkg"],
                  e_mat, p_mat, r_mat, pl_mat]
    cts_in = [dq_m.reshape(t, QKW), dk_m.reshape(t, QKW), dv_m.reshape(t, 2 * LANES), dq_f.reshape(t, QKW),
              dk_f.reshape(t, QKW), dv_f.reshape(t, 2 * LANES), du, dfl, dgl]
    (dproj, g_qlg, g_wuq, g_kvlg, g_wuk, g_wuv, g_mqg, g_mkg, g_fqg, g_fkg) = _tile_call(
        pre_bwd_fn, "mixer_pre_bwd", (nt,),
        [_row(proj, PA), _row(cos), _row(sin)] + [_par(a) for a in pre_params] + [_row(a) for a in cts_in],
        [_orow(t, PW), _oacc((1, MLA_Q_RANK)), _oacc((MLA_Q_RANK, QKW)), _oacc((1, MLA_KV_RANK)),
         _oacc((MLA_KV_RANK, QKW)), _oacc((MLA_KV_RANK, N_HEADS * HEAD_V)), _oacc((1, QKW)), _oacc((1, QKW)),
         _oacc((1, QKW)), _oacc((1, QKW))])
    g["q_lat_norm_g"] = g_qlg.reshape(MLA_Q_RANK)
    g["kv_lat_norm_g"] = g_kvlg.reshape(MLA_KV_RANK)
    heads = lambda a, w: jnp.sum(a.reshape(N_HEADS, HP)[:, :w], axis=0)
    g["mla_q_norm_g"], g["mla_k_norm_g"] = heads(g_mqg, MLA_QK_DIM), heads(g_mkg, MLA_QK_DIM)
    g["fox_q_norm_g"], g["fox_k_norm_g"] = heads(g_fqg, HEAD_V), heads(g_fkg, HEAD_V)
    g["w_uq"] = g_wuq.reshape(MLA_Q_RANK, N_HEADS, HP)[..., :MLA_QK_DIM].reshape(MLA_Q_RANK, N_HEADS * MLA_QK_DIM)
    g["w_ukv"] = jnp.concatenate([g_wuk.reshape(MLA_KV_RANK, N_HEADS, HP)[..., :MLA_NOPE_DIM],
                                  g_wuv.reshape(MLA_KV_RANK, N_HEADS, HEAD_V)], axis=-1).reshape(MLA_KV_RANK, QKW)

    g["w_in"] = _mm(sv["h"], dproj, "tn", "mm_in_dw", out_dtype=BF16)
    dh = _mm(dproj, p["w_in"], "nt", "mm_in_dx")
    dx, g["attn_g"] = _tile_call(norm_bwd_fn, "norm_bwd", (nt,),
                                 [_row(sv["x"]), _par(p["attn_g"]), _row(dh), _row(d1)],
                                 [_orow(t, D_MODEL), _oacc((1, D_MODEL))])
    return dx, g, xout


def _row_tile(r, c, max_elems):
    if r * c <= max_elems:
        return r
    best = None
    for d in range(8, r, 8):
        if r % d == 0 and d * c <= max_elems:
            best = d
    assert best is not None, (r, c)
    return best


def _sum8(land, name):
    _, r, c = land.shape
    tr = _row_tile(r, N_DEV * c, 1 << 20)

    def fn(lv):
        acc = lv[0].astype(F32)
        for i in range(1, N_DEV):
            acc = acc + lv[i].astype(F32)
        return acc

    return _tile_call(fn, name, (r // tr,), [(land, (N_DEV, tr, c), lambda i: (0, i, 0))],
                      [((r, c), F32, (tr, c), lambda i: (i, 0), None)])[0]


def _adamw_call(w, g, m, v, name):
    r, c = w.shape
    tr = _row_tile(r, c, 1 << 19)
    spec = lambda a: (a, (tr, c), lambda i: (i, 0))
    o = ((r, c), F32, (tr, c), lambda i: (i, 0), None)
    return _tile_call(_adamw, name, (r // tr,), [spec(w), spec(g), spec(m), spec(v)], [o, o, o])


SMALL_NAMES = ("attn_norm_g", "q_lat_norm_g", "kv_lat_norm_g", "mla_q_norm_g", "mla_k_norm_g", "fox_q_norm_g",
               "fox_k_norm_g", "fox_f_bias", "s5_lambda_re", "s5_lambda_im", "s5_b_re", "s5_b_im", "s5_c_re", "s5_c_im",
               "s5_d", "s5_log_step", "s5_b_glu", "ffn_norm_g")
BIG_NAMES = ("w_in", "w_uq", "w_ukv", "s5_w_glu", "w_branch", "w_out", "w_up", "ffn_conv_w", "w_down")
ALL_NAMES = ("attn_norm_g", "w_in", "q_lat_norm_g", "w_uq", "kv_lat_norm_g", "w_ukv", "mla_q_norm_g", "mla_k_norm_g",
             "fox_q_norm_g", "fox_k_norm_g", "fox_f_bias", "s5_lambda_re", "s5_lambda_im", "s5_b_re", "s5_b_im",
             "s5_c_re", "s5_c_im", "s5_d", "s5_log_step", "s5_w_glu", "s5_b_glu", "w_branch", "w_out", "ffn_norm_g",
             "w_up", "ffn_conv_w", "w_down")


def _pack(arrs):
    flat = jnp.concatenate([a.reshape(-1) for a in arrs])
    n = flat.shape[0]
    tile = 8 * LANES
    tot = -(-n // tile) * tile
    return jnp.pad(flat, (0, tot - n)).reshape(tot // LANES, LANES)


def _unpack(packed, like):
    flat = packed.reshape(-1)
    out, pos = [], 0
    for a in like:
        n = int(np.prod(a.shape))
        out.append(flat[pos:pos + n].reshape(a.shape))
        pos += n
    return out


def _rope_tables(positions, t):
    pos = positions.reshape(t, 1)
    inv = _inv_freq_row()

    def fn(pv, iv):
        ang = pv.astype(F32) * iv
        return jnp.cos(ang), jnp.sin(ang)

    return _tile_call(fn, "rope_tables", (t // TM,), [_row(pos), _par(inv)], [_orow(t, QKW), _orow(t, QKW)])


def _loss_call(x1, f, target):
    t = x1.shape[0]

    def fn(xv, fv, tv):
        e = xv + fv - tv
        per_tok = jnp.sum(e * e, axis=1, keepdims=True) * (1.0 / D_MODEL)
        return 0.5 * jnp.sum(per_tok, axis=0, keepdims=True), e * (1.0 / D_MODEL)

    return _tile_call(fn, "loss", (t // TM,), [_row(x1), _row(f), _row(target)], [_oacc((1, 1)), _orow(t, D_MODEL)])


def _local_shards(l, w):
    return {
        "w_in": _pad_in_cols(w["w_in"][l]).astype(BF16),
        "w_uq": w["w_uq"][l].reshape(-1, N_HEADS * MLA_QK_DIM).astype(BF16),
        "w_ukv": w["w_ukv"][l].reshape(-1, QKW).astype(BF16),
        "w_glu": w["s5_w_glu"][l].astype(BF16),
        "w_branch": w["w_branch"][l].astype(BF16),
        "w_out": w["w_out"][l].astype(BF16),
        "w_up": _pad_ff(w["w_up"][l], 1).astype(BF16),
        "conv": _pad_ff(w["ffn_conv_w"][l], 1),
        "w_down": _pad_ff(w["w_down"][l], 0).astype(BF16),
    }


GATHERED_SHAPES = {"w_in": (D_MODEL, PW), "w_uq": (MLA_Q_RANK, N_HEADS * MLA_QK_DIM), "w_ukv": (MLA_KV_RANK, QKW),
                   "w_glu": (S5_WIDTH, S5_WIDTH), "w_branch": (3, S5_WIDTH, D_MODEL), "w_out": (D_MODEL, D_MODEL),
                   "w_up": (D_MODEL, UP_W), "conv": (3, UP_W), "w_down": (FF_PW, D_MODEL)}
FWD_RIDERS = {"mixer_pre": ("w_out", "w_branch"), "attn_mla_fwd": ("w_in", "w_uq", "w_ukv", "w_glu"),
              "attn_fox_fwd": ("w_up", "conv"), "s5_scan_fwd": ("w_down",)}
BWD_RIDERS = {"ffn_mid_bwd": ("w_down",), "s5_scan_bwd": ("w_out", "w_branch", "w_uq", "w_ukv", "w_glu", "conv"),
              "attn_mla_bwd": ("w_in",), "attn_fox_bwd": ("w_up",)}
WEIGHT_OF = {"w_in": "w_in", "w_uq": "w_uq", "w_ukv": "w_ukv", "w_glu": "s5_w_glu", "w_branch": "w_branch",
             "w_out": "w_out", "w_up": "w_up", "conv": "ffn_conv_w", "w_down": "w_down"}


def _unshard_layout(nm, g2):
    if nm == "w_in":
        return _unpad_in_cols(g2)
    if nm == "w_uq":
        return g2.reshape(-1, N_HEADS, MLA_QK_DIM)
    if nm == "w_ukv":
        return g2.reshape(-1, N_HEADS, MLA_NOPE_DIM + HEAD_V)
    if nm in ("w_up", "conv"):
        return _unpad_ff(g2, 1)
    if nm == "w_down":
        return _unpad_ff(g2, 0)
    return g2


def kernel(x, positions, attn_norm_g, w_in, q_lat_norm_g, w_uq, kv_lat_norm_g, w_ukv, mla_q_norm_g, mla_k_norm_g, fox_q_norm_g, fox_k_norm_g, fox_f_bias, s5_lambda_re, s5_lambda_im, s5_b_re, s5_b_im, s5_c_re, s5_c_im, s5_d, s5_log_step, s5_w_glu, s5_b_glu, w_branch, w_out, ffn_norm_g, w_up, ffn_conv_w, w_down, loss_target, m_attn_norm_g, m_w_in, m_q_lat_norm_g, m_w_uq, m_kv_lat_norm_g, m_w_ukv, m_mla_q_norm_g, m_mla_k_norm_g, m_fox_q_norm_g, m_fox_k_norm_g, m_fox_f_bias, m_s5_lambda_re, m_s5_lambda_im, m_s5_b_re, m_s5_b_im, m_s5_c_re, m_s5_c_im, m_s5_d, m_s5_log_step, m_s5_w_glu, m_s5_b_glu, m_w_branch, m_w_out, m_ffn_norm_g, m_w_up, m_ffn_conv_w, m_w_down, v_attn_norm_g, v_w_in, v_q_lat_norm_g, v_w_uq, v_kv_lat_norm_g, v_w_ukv, v_mla_q_norm_g, v_mla_k_norm_g, v_fox_q_norm_g, v_fox_k_norm_g, v_fox_f_bias, v_s5_lambda_re, v_s5_lambda_im, v_s5_b_re, v_s5_b_im, v_s5_c_re, v_s5_c_im, v_s5_d, v_s5_log_step, v_s5_w_glu, v_s5_b_glu, v_w_branch, v_w_out, v_ffn_norm_g, v_w_up, v_ffn_conv_w, v_w_down):
    args = locals()
    w = {n: args[n] for n in ALL_NAMES}
    m = {n: args["m_" + n] for n in ALL_NAMES}
    v = {n: args["v_" + n] for n in ALL_NAMES}
    depth = w_in.shape[0]
    bsz, seq, _ = x.shape
    t = bsz * seq
    consts = _constants()
    tabs = _rope_tables(positions, t)
    small = {n: w[n] for n in SMALL_NAMES}

    all_names = list(GATHERED_SHAPES)
    all_shards = [_local_shards(l, w) for l in range(depth)]

    def gather_xfer(l, names):
        sub = {n: all_shards[l][n] for n in names}
        order, plan = _gather_plan(sub)
        return _Xfer([sub[n] for n in order], [(GATHERED_SHAPES[n], sub[n].dtype) for n in order], plan)

    def scatter_xfer(g, names):
        shapes = {n: _local_shards_shape(n) for n in names}
        return _Xfer([g[n] for n in names], [((N_DEV,) + shapes[n], g[n].dtype) for n in names],
                     _scatter_plan(list(names), shapes))

    def riding(groups, xout):
        return {n: a for host, names in groups.items() for n, a in zip(names, xout[host])}

    xs = x.reshape(t, D_MODEL)
    f_prev = None
    saved, params = [], []
    gathered = dict(zip(all_names, _exchange("gather_weights", gather_xfer(0, all_names))))
    for l in range(depth):
        p = _prep_layer(l, gathered, small, consts)
        riders = {} if l + 1 == depth else {host: gather_xfer(l + 1, names) for host, names in FWD_RIDERS.items()}
        xs, f_prev, sv, xout = _layer_fwd(xs, f_prev, p, tabs, consts, bsz, riders)
        if riders:
            gathered = riding(FWD_RIDERS, xout)
        saved.append(sv)
        params.append(p)

    loss_part, dy = _loss_call(xs, f_prev, loss_target.reshape(t, D_MODEL))
    loss = lax.psum(loss_part[0, 0], ("x", "y", "c"))

    small_grads = {n: [None] * depth for n in SMALL_NAMES}
    big = {n: {"g": [None] * depth, "d": [None] * depth, "m": [None] * depth, "v": [None] * depth} for n in BIG_NAMES}

    def finish(l, lands):
        for n in all_names:
            wn, ld = WEIGHT_OF[n], lands[n]
            ld2 = ld.reshape(N_DEV, -1, ld.shape[-1])
            gsum = _unshard_layout(n, _sum8(ld2, "sum_" + n).reshape(_local_shards_shape(n)))
            two = lambda a, c=gsum.shape[-1]: a.reshape(-1, c)
            dl, ml, vl = _adamw_call(two(w[wn][l]), two(gsum), two(m[wn][l]), two(v[wn][l]), "adamw_" + n)
            big[wn]["g"][l] = gsum.reshape(w[wn].shape[1:])
            big[wn]["d"][l], big[wn]["m"][l], big[wn]["v"][l] = (a.reshape(w[wn].shape[1:]) for a in (dl, ml, vl))

    dx = dy
    pending = None
    for l in reversed(range(depth)):
        riders = {} if pending is None else {host: scatter_xfer(pending, names) for host, names in BWD_RIDERS.items()}
        dx, g, xout = _layer_bwd(dx, params[l], saved[l], tabs, consts, bsz, riders)
        if riders:
            finish(l + 1, riding(BWD_RIDERS, xout))
        small_grads["attn_norm_g"][l] = g["attn_g"].reshape(D_MODEL)
        small_grads["ffn_norm_g"][l] = g["ffn_g"].reshape(D_MODEL)
        for n in SMALL_NAMES:
            if n not in ("attn_norm_g", "ffn_norm_g"):
                small_grads[n][l] = g[n]
        pending = g
    finish(0, dict(zip(all_names, _exchange("scatter_grads", scatter_xfer(pending, all_names)))))

    sg = [jnp.stack(small_grads[n]).reshape(w[n].shape) for n in SMALL_NAMES]
    packed = _pack(sg)
    land = _exchange("gather_small_grads", _Xfer([packed], [((N_DEV,) + packed.shape, F32)],
                                                 _scatter_plan(["small"], {})))[0]
    gs = _sum8(land, "sum_small")
    ds, ms, vs = _adamw_call(_pack([w[n] for n in SMALL_NAMES]), gs, _pack([m[n] for n in SMALL_NAMES]),
                             _pack([v[n] for n in SMALL_NAMES]), "adamw_small")
    like = [w[n] for n in SMALL_NAMES]
    small_out = {k: dict(zip(SMALL_NAMES, _unpack(a, like))) for k, a in (("g", gs), ("d", ds), ("m", ms), ("v", vs))}

    def out_of(kind, n):
        if n in SMALL_NAMES:
            return small_out[kind][n]
        return jnp.stack(big[n][kind])

    outs = [loss, dx.reshape(bsz, seq, D_MODEL)]
    for kind in ("g", "d", "m", "v"):
        outs += [out_of(kind, n) for n in ALL_NAMES]
    return tuple(outs)


def _local_shards_shape(nm):
    return {"w_in": (D_MODEL // N_DEV, PW), "w_uq": (MLA_Q_RANK // N_DEV, N_HEADS * MLA_QK_DIM),
            "w_ukv": (MLA_KV_RANK // N_DEV, QKW), "w_glu": (S5_WIDTH // N_DEV, S5_WIDTH),
            "w_branch": (3, S5_WIDTH, LANES), "w_out": (D_MODEL // N_DEV, D_MODEL), "w_up": (D_MODEL, 2 * FF_BLK),
            "conv": (3, 2 * FF_BLK), "w_down": (FF_BLK, D_MODEL)}[nm]
```

```python
import functools
import math

import numpy as np
import jax
import jax.numpy as jnp
from jax import lax
from jax.experimental import pallas as pl
from jax.experimental.pallas import tpu as pltpu

F32, BF16 = jnp.float32, jnp.bfloat16

D_MODEL = 1024
N_DEV = 8
MLA_Q_RANK, MLA_KV_RANK, MLA_ROPE_DIM, MLA_NOPE_DIM, MLA_QK_DIM = 384, 256, 32, 64, 96
N_HEADS, HEAD_V = 4, 64
S5_GROUPS, S5_GROUP_CH, S5_STATE, S5_WIDTH = 16, 16, 64, 256
S5_N = S5_GROUPS * S5_STATE
D_FF = 2816
D_IN = 4772
ROPE_THETA = 10000.0
NORM_EPS = 1e-6
NEG_INF = -1e30
ADAM_LR, ADAM_B1, ADAM_B2, ADAM_EPS, ADAM_WD, ADAM_STEP = 0.001, 0.9, 0.999, 1e-08, 0.01, 10

VMEM_LIMIT_BYTES = 56 * 1024 * 1024
LANES = 128
TM = 256
TQ = 256
HP = 128
QKW = N_HEADS * HP

PW = 5120
PA = 2048
IN_SEGS = ((0, 384, 256),
           (256, 672, 256),
           (512, 928, 256),
           (768, 1184, 256),
           (1024, 1444, 256),
           (1280, 640, 32),
           (1408, 1440, 4),
           (1536, 0, 384),
           (2048, 1700, 3072))
O_CKV, O_FQ, O_FK, O_FV, O_U, O_KR, O_FL, O_CQ = 0, 256, 512, 768, 1024, 1280, 1408, 1536

FF_BLK = 384
FF_HALF = D_FF // 8
FF_PW = 8 * FF_BLK
UP_W = 2 * FF_PW

NN = (((1,), (0,)), ((), ()))
NT = (((1,), (1,)), ((), ()))
TN = (((0,), (0,)), ((), ()))


def _pcall(body, *, name, grid, in_specs, out_specs, out_shape, scratch_shapes=(), rider=None, aliases=None):
    params = pltpu.CompilerParams(dimension_semantics=("arbitrary",) * len(grid), vmem_limit_bytes=VMEM_LIMIT_BYTES)
    if rider is None:
        return pl.pallas_call(body, name=name, grid=grid, in_specs=in_specs, out_specs=out_specs, out_shape=out_shape,
                              scratch_shapes=list(scratch_shapes), input_output_aliases=aliases or {},
                              compiler_params=params)
    assert not aliases
    in_specs, out_specs, out_shape = list(in_specs), list(out_specs), list(out_shape)
    n_in, n_out, n_scr, nx = len(in_specs), len(out_specs), len(scratch_shapes), rider.n
    any_spec = pl.BlockSpec(memory_space=pl.ANY)

    def at_step(last):
        conds = [pl.program_id(a) == (g - 1 if last else 0) for a, g in enumerate(grid)]
        return functools.reduce(jnp.logical_and, conds)

    def body_with_rider(*refs):
        ins, xsrc = refs[:n_in], refs[n_in:n_in + nx]
        outs, xout = refs[n_in + nx:n_in + nx + n_out], refs[n_in + nx + n_out:n_in + 2 * nx + n_out]
        scr = refs[n_in + 2 * nx + n_out:n_in + 2 * nx + n_out + n_scr]
        sems = refs[n_in + 2 * nx + n_out + n_scr:]

        @pl.when(at_step(False))
        def _():
            rider.start(rider.copies(xsrc, xout, *sems))

        body(*ins, *outs, *scr)

        @pl.when(at_step(True))
        def _():
            rider.wait(rider.copies(xsrc, xout, *sems))

    call = pl.pallas_call(body_with_rider, name=name, grid=grid, in_specs=in_specs + [any_spec] * nx,
                          out_specs=out_specs + [any_spec] * nx,
                          out_shape=out_shape + [jax.ShapeDtypeStruct(s, d) for s, d in rider.out_shapes],
                          scratch_shapes=list(scratch_shapes) + rider.sems(), compiler_params=params)
    return lambda *args: call(*args, *rider.srcs)


def _tile_call(fn, name, grid, ins, outs, rider=None):
    n_in = len(ins)

    def body(*refs):
        res = fn(*[r[...] for r in refs[:n_in]])
        if not isinstance(res, (tuple, list)):
            res = (res,)
        assert len(res) == len(outs), (name, len(res), len(outs))
        for r, v, o in zip(refs[n_in:], res, outs):
            v = v.astype(r.dtype)
            if o[4] is None:
                r[...] = v
            else:
                first = functools.reduce(jnp.logical_and, [pl.program_id(a) == 0 for a in o[4]])

                @pl.when(first)
                def _():
                    r[...] = v

                @pl.when(jnp.logical_not(first))
                def _():
                    r[...] += v

    res = _pcall(body, name=name, grid=grid,
                 in_specs=[pl.BlockSpec(b, m) for _, b, m in ins],
                 out_specs=[pl.BlockSpec(o[2], o[3]) for o in outs],
                 out_shape=[jax.ShapeDtypeStruct(o[0], o[1]) for o in outs], rider=rider)(*[a for a, _, _ in ins])
    return res


def _row(a, width=None, col_block=0, tm=TM):
    w = a.shape[1] if width is None else width
    return (a, (tm, w), lambda i, c=col_block: (i, c))


def _par(a):
    nd = a.ndim
    return (a, a.shape, lambda i, nd=nd: (0,) * nd)


def _orow(t, w, dtype=F32, tm=TM):
    return ((t, w), dtype, (tm, w), lambda i: (i, 0), None)


def _oacc(shape):
    nd = len(shape)
    return (tuple(shape), F32, tuple(shape), lambda i, nd=nd: (0,) * nd, (0,))


def _pick(n, target):
    best = None
    for d in range(LANES, min(n, target) + 1, LANES):
        if n % d == 0:
            best = d
    return n if best is None else best


def _mm(a, b, mode, name, out_dtype=F32, tm=1024, tn=2048, tk=1024):
    if mode == "nn":
        (m, k), (_, n) = a.shape, b.shape
    elif mode == "nt":
        (m, k), (n, _) = a.shape, b.shape
    else:
        (k, m), (_, n) = a.shape, b.shape
    tm, tn, tk = _pick(m, tm), _pick(n, tn), _pick(k, tk)
    nk = k // tk
    dims = {"nn": NN, "nt": NT, "tn": TN}[mode]
    a_spec = (pl.BlockSpec((tk, tm), lambda i, j, l: (l, i)) if mode == "tn"
              else pl.BlockSpec((tm, tk), lambda i, j, l: (i, l)))
    b_spec = (pl.BlockSpec((tn, tk), lambda i, j, l: (j, l)) if mode == "nt"
              else pl.BlockSpec((tk, tn), lambda i, j, l: (l, j)))

    def body(a_ref, b_ref, o_ref, acc_ref):
        part = lax.dot_general(a_ref[...].astype(BF16), b_ref[...].astype(BF16), dims, preferred_element_type=F32)
        if nk == 1:
            o_ref[...] = part.astype(o_ref.dtype)
        else:
            l = pl.program_id(2)

            @pl.when(l == 0)
            def _():
                acc_ref[...] = part

            @pl.when(l > 0)
            def _():
                acc_ref[...] += part

            @pl.when(l == nk - 1)
            def _():
                o_ref[...] = acc_ref[...].astype(o_ref.dtype)

    return _pcall(body, name=name, grid=(m // tm, n // tn, nk), in_specs=[a_spec, b_spec],
                  out_specs=pl.BlockSpec((tm, tn), lambda i, j, l: (i, j)),
                  out_shape=jax.ShapeDtypeStruct((m, n), out_dtype),
                  scratch_shapes=[pltpu.VMEM((tm, tn) if nk > 1 else (8, LANES), F32)])(a, b)


def _dot(x, w, dims):
    return lax.dot_general(x.astype(BF16), w.astype(BF16), dims, preferred_element_type=F32)


@jax.custom_vjp
def bdot(x, w):
    return _dot(x, w, NN)


def _bdot_fwd(x, w):
    return _dot(x, w, NN), (x, w)


def _bdot_bwd(res, g):
    x, w = res
    return _dot(g, w, NT), _dot(x, g, TN)


bdot.defvjp(_bdot_fwd, _bdot_bwd)


def _split3(x):
    x1 = x.astype(BF16)
    r1 = x - x1.astype(F32)
    x2 = r1.astype(BF16)
    x3 = (r1 - x2.astype(F32)).astype(BF16)
    return x1, x2, x3


def _cdot(x, m, dims):
    return sum(lax.dot_general(p, m, dims, preferred_element_type=F32) for p in _split3(x))


def _block_dot(x, m, transposed):
    k_in = m.shape[1] if transposed else m.shape[0]
    dims = NT if transposed else NN
    return jnp.concatenate([_cdot(x[:, b * k_in:(b + 1) * k_in], m, dims) for b in range(x.shape[1] // k_in)], axis=1)


@jax.custom_vjp
def cdot(x, m):
    return _block_dot(x, m, False)


def _cdot_fwd(x, m):
    return _block_dot(x, m, False), m


def _cdot_bwd(m, g):
    return _block_dot(g, m, True), jnp.zeros_like(m)


cdot.defvjp(_cdot_fwd, _cdot_bwd)


@jax.custom_vjp
def tile_heads(y):
    return jnp.concatenate([y] * N_HEADS, axis=1)


def _tile_heads_fwd(y):
    return jnp.concatenate([y] * N_HEADS, axis=1), None


def _tile_heads_bwd(_, g):
    return (sum(g[:, h * HP:(h + 1) * HP] for h in range(N_HEADS)),)


tile_heads.defvjp(_tile_heads_fwd, _tile_heads_bwd)


def _shift_rows_impl(x, s, reverse):
    n = x.shape[0]
    idx = lax.broadcasted_iota(jnp.int32, x.shape, 0)
    if reverse:
        return jnp.where(idx < n - s, pltpu.roll(x, n - s, 0), 0.0)
    return jnp.where(idx >= s, pltpu.roll(x, s, 0), 0.0)


@functools.partial(jax.custom_vjp, nondiff_argnums=(1,))
def shift_rows(x, s):
    return _shift_rows_impl(x, s, False)


def _shift_rows_fwd(x, s):
    return _shift_rows_impl(x, s, False), None


def _shift_rows_bwd(s, _, g):
    return (_shift_rows_impl(g, s, True),)


shift_rows.defvjp(_shift_rows_fwd, _shift_rows_bwd)


def _rms(x, g):
    return x * lax.rsqrt(jnp.mean(x * x, axis=-1, keepdims=True) + NORM_EPS) * g


def _head_rms(x, e_mat, inv_n, g):
    ms = cdot(x * x, e_mat) * inv_n
    return x * lax.rsqrt(ms + NORM_EPS) * g


def _rope(y, p_mat, cos, sin):
    return y * cos + cdot(y, p_mat) * sin


def _constants():
    e = np.ones((HP, HP), np.float32)
    p = np.zeros((HP, HP), np.float32)
    r = np.zeros((LANES, HP), np.float32)
    pl64 = np.zeros((2 * HEAD_V, 2 * HP), np.float32)
    half = MLA_ROPE_DIM // 2
    for i in range(half):
        x1, x2 = MLA_NOPE_DIM + i, MLA_NOPE_DIM + half + i
        p[x2, x1] = -1.0
        p[x1, x2] = 1.0
    for i in range(MLA_ROPE_DIM):
        r[i, MLA_NOPE_DIM + i] = 1.0
    for h in range(2):
        for i in range(HEAD_V):
            pl64[h * HEAD_V + i, h * HP + i] = 1.0
    return tuple(jnp.asarray(a, BF16) for a in (e, p, r, pl64))


def _inv_freq_row():
    inv = ROPE_THETA ** (-jnp.arange(0, MLA_ROPE_DIM, 2, dtype=F32) / MLA_ROPE_DIM)
    head = jnp.concatenate([jnp.zeros((MLA_NOPE_DIM,), F32), inv, inv, jnp.zeros((HP - MLA_QK_DIM,), F32)])
    return jnp.tile(head, N_HEADS).reshape(1, QKW)


def _mixer_pre(c_q, c_kv, kr, fq, fk, qlg, w_uq, kvlg, w_uk, w_uv, mqg, mkg, fqg, fkg, cos, sin, e_mat, p_mat, r_mat, pl_mat):
    q = bdot(_rms(c_q, qlg), w_uq)
    ckvn = _rms(c_kv, kvlg)
    k = bdot(ckvn, w_uk) + tile_heads(cdot(kr, r_mat))
    v = bdot(ckvn, w_uv)
    q = _rope(_head_rms(q, e_mat, 1.0 / MLA_QK_DIM, mqg), p_mat, cos, sin)
    k = _rope(_head_rms(k, e_mat, 1.0 / MLA_QK_DIM, mkg), p_mat, cos, sin)
    qf = _head_rms(cdot(fq, pl_mat), e_mat, 1.0 / HEAD_V, fqg)
    kf = _head_rms(cdot(fk, pl_mat), e_mat, 1.0 / HEAD_V, fkg)
    return q, k, v, qf, kf


def _s5_post(state, u, c_blk, d_row, w_glu, b_glu):
    y = bdot(state, c_blk) + d_row * u
    y = jax.nn.gelu(y)
    return y * jax.nn.sigmoid(bdot(y, w_glu) + b_glu)


def _merge(o_a, o_b, o_c, g_a, g_b, g_c, w_a, w_b, w_c):
    return (jax.nn.sigmoid(g_a) * bdot(o_a, w_a) + jax.nn.sigmoid(g_b) * bdot(o_b, w_b)
            + jax.nn.sigmoid(g_c) * bdot(o_c, w_c))


def _ffn_mid(gate, val, wg0, wg1, wg2, wv0, wv1, wv2):
    cg = wg0 * shift_rows(gate, 2) + wg1 * shift_rows(gate, 1) + wg2 * gate
    cv = wv0 * shift_rows(val, 2) + wv1 * shift_rows(val, 1) + wv2 * val
    return jax.nn.silu(cg) * cv


def _s5_params(lam_re, lam_im, log_step, b_re, b_im):
    step = jnp.exp(log_step)
    zr, zi = lam_re * step, lam_im * step
    mag = jnp.exp(zr)
    lr, li = mag * jnp.cos(zi), mag * jnp.sin(zi)
    nr, ni = lr - 1.0, li
    den = lam_re * lam_re + lam_im * lam_im
    cr = (nr * lam_re + ni * lam_im) / den
    ci = (ni * lam_re - nr * lam_im) / den
    return lr, li, cr * b_re - ci * b_im, cr * b_im + ci * b_re


def _adamw(w, g, m, v):
    m = ADAM_B1 * m + (1.0 - ADAM_B1) * g
    v = ADAM_B2 * v + (1.0 - ADAM_B2) * (g * g)
    m_hat = m / (1.0 - ADAM_B1 ** ADAM_STEP)
    v_hat = v / (1.0 - ADAM_B2 ** ADAM_STEP)
    delta = -ADAM_LR * (m_hat / (jnp.sqrt(v_hat) + ADAM_EPS) + ADAM_WD * w)
    return delta, m, v


SCAN_W = 256


def _scan_seq(xr, xi, ar, ai, reverse):
    n = xr.shape[0]
    s = 1
    while s < n:
        sr, si = _shift_rows_impl(xr, s, reverse), _shift_rows_impl(xi, s, reverse)
        xr, xi = xr + ar * sr - ai * si, xi + ar * si + ai * sr
        ar, ai = ar * ar - ai * ai, 2.0 * ar * ai
        s *= 2
    return xr, xi


def _scan_fwd(bu, lbar, bsz, name, rider=None):
    t = bu.shape[0]
    seq = t // bsz
    nb = bu.shape[1] // (2 * SCAN_W)

    def fn(b, a):
        xr, xi = _scan_seq(b[:, :SCAN_W], b[:, SCAN_W:], a[:, :SCAN_W], a[:, SCAN_W:], False)
        return jnp.concatenate([xr, xi], axis=1)

    blk = (seq, 2 * SCAN_W)
    return _tile_call(fn, name, (bsz, nb),
                      [(bu, blk, lambda i, j: (i, j)), (lbar, (1, 2 * SCAN_W), lambda i, j: (0, j))],
                      [((t, bu.shape[1]), F32, blk, lambda i, j: (i, j), None)], rider=rider)


def _scan_bwd(dstate, state, lbar, bsz, name, rider=None):
    t = state.shape[0]
    seq = t // bsz
    nb = state.shape[1] // (2 * SCAN_W)

    def fn(g, x, a):
        ar, ai = a[:, :SCAN_W], a[:, SCAN_W:]
        gr, gi = _scan_seq(g[:, :SCAN_W], g[:, SCAN_W:], ar, -ai, True)
        pr, pi = _shift_rows_impl(x[:, :SCAN_W], 1, False), _shift_rows_impl(x[:, SCAN_W:], 1, False)
        dar = jnp.sum(gr * pr + gi * pi, axis=0, keepdims=True)
        dai = jnp.sum(gi * pr - gr * pi, axis=0, keepdims=True)
        return jnp.concatenate([gr, gi], axis=1), jnp.concatenate([dar, dai], axis=1)

    blk = (seq, 2 * SCAN_W)
    return _tile_call(fn, name, (nb, bsz),
                      [(dstate, blk, lambda j, i: (i, j)), (state, blk, lambda j, i: (i, j)),
                       (lbar, (1, 2 * SCAN_W), lambda j, i: (0, j))],
                      [((t, state.shape[1]), F32, blk, lambda j, i: (i, j), None),
                       ((1, state.shape[1]), F32, (1, 2 * SCAN_W), lambda j, i: (0, j), (1,))], rider=rider)


def _shift_lanes(x, s, reverse):
    n = x.shape[1]
    idx = lax.broadcasted_iota(jnp.int32, x.shape, 1)
    if reverse:
        return jnp.where(idx < n - s, pltpu.roll(x, n - s, 1), 0.0)
    return jnp.where(idx >= s, pltpu.roll(x, s, 1), 0.0)


def _cumsum_lanes(x, reverse):
    s = 1
    while s < x.shape[1]:
        x = x + _shift_lanes(x, s, reverse)
        s *= 2
    return x


def _forget_fwd(z, bias, name):
    def fn(zv, bv):
        x = zv + bv
        logf = jnp.minimum(x, 0.0) - jnp.log(1.0 + jnp.exp(-jnp.abs(x)))
        return _cumsum_lanes(logf, False)

    return _tile_call(fn, name, (1,), [_par(z), _par(bias)],
                      [(z.shape, F32, z.shape, lambda i: (0, 0), None)])[0]


def _forget_bwd(dc, z, bias, name):
    def fn(dcv, zv, bv):
        x = zv + bv
        dlogf = _cumsum_lanes(dcv, True)
        dz = dlogf * jax.nn.sigmoid(-x)
        return dz, jnp.sum(dz, axis=1, keepdims=True)

    rows = z.shape[0]
    return _tile_call(fn, name, (1,), [_par(dc), _par(z), _par(bias)],
                      [(z.shape, F32, z.shape, lambda i: (0, 0), None),
                       ((rows, 1), F32, (rows, 1), lambda i: (0, 0), None)])


def _col(v2, x):
    idx = lax.broadcasted_iota(jnp.int32, v2.shape, 1)
    return jnp.sum(jnp.where(idx == x, v2, 0.0), axis=1, keepdims=True)


def _two_cols(c0, c1):
    idx = lax.broadcasted_iota(jnp.int32, (c0.shape[0], 2), 1)
    return jnp.where(idx == 0, c0, c1)


def _attn_fwd(q, k, v, v_blk0, cq, ck, scale, name, rider=None):
    bsz, seq, _ = q.shape
    nq = seq // TQ
    has_bias = cq is not None

    def body(*refs):
        if has_bias:
            q_ref, k_ref, v_ref, cq_ref, ck_ref, o_ref, lse_ref = refs
        else:
            q_ref, k_ref, v_ref, o_ref, lse_ref = refs
        qi = pl.program_id(2)
        rows = qi * TQ + lax.broadcasted_iota(jnp.int32, (TQ, TQ), 0)
        cols0 = lax.broadcasted_iota(jnp.int32, (TQ, TQ), 1)
        lane = lax.broadcasted_iota(jnp.int32, (TQ, LANES), 1)
        outs, lses = [], []
        for x in range(2):
            qx = q_ref[:, x * HP:(x + 1) * HP].astype(BF16)
            cqx = _col(cq_ref[...], x) if has_bias else None

            def step(j, carry, x=x, qx=qx, cqx=cqx):
                m, l, acc = carry
                ks = pl.multiple_of(j * TQ, TQ)
                kx = k_ref[pl.ds(ks, TQ), x * HP:(x + 1) * HP].astype(BF16)
                s = lax.dot_general(qx, kx, NT, preferred_element_type=F32) * scale
                if has_bias:
                    s = s + (cqx - ck_ref[x:x + 1, pl.ds(ks, TQ)])
                s = jnp.where(rows >= cols0 + ks, s, NEG_INF)
                m_new = jnp.maximum(m, jnp.max(s, axis=1, keepdims=True))
                alpha = jnp.exp(m - m_new)
                p = jnp.exp(s - m_new)
                l = alpha * l + jnp.sum(p, axis=1, keepdims=True)
                acc = alpha * acc + lax.dot_general(p.astype(BF16), v_ref[pl.ds(ks, TQ), :].astype(BF16), NN,
                                                    preferred_element_type=F32)
                return m_new, l, acc

            init = (jnp.full((TQ, 1), NEG_INF, F32), jnp.zeros((TQ, 1), F32), jnp.zeros((TQ, LANES), F32))
            m, l, acc = lax.fori_loop(0, qi + 1, step, init)
            outs.append(acc / l)
            lses.append(m + jnp.log(l))
        o_ref[...] = jnp.where(lane < HEAD_V, outs[0], outs[1])
        lse_ref[...] = _two_cols(lses[0], lses[1])

    in_specs = [pl.BlockSpec((None, TQ, 2 * HP), lambda b, p, i: (b, i, p)),
                pl.BlockSpec((None, seq, 2 * HP), lambda b, p, i: (b, 0, p)),
                pl.BlockSpec((None, seq, LANES), lambda b, p, i: (b, 0, v_blk0 + p))]
    args = [q, k, v]
    if has_bias:
        in_specs += [pl.BlockSpec((None, None, TQ, 2), lambda b, p, i: (b, p, i, 0)),
                     pl.BlockSpec((None, None, 2, seq), lambda b, p, i: (b, p, 0, 0))]
        args += [cq, ck]
    return _pcall(body, name=name, grid=(bsz, 2, nq), in_specs=in_specs,
                  out_specs=[pl.BlockSpec((None, TQ, LANES), lambda b, p, i: (b, i, p)),
                             pl.BlockSpec((None, None, TQ, 2), lambda b, p, i: (b, p, i, 0))],
                  out_shape=[jax.ShapeDtypeStruct((bsz, seq, 2 * LANES), F32),
                             jax.ShapeDtypeStruct((bsz, 2, seq, 2), F32)], rider=rider)(*args)


def _attn_bwd(q, k, v, v_blk0, o, do, lse, cq, ck, scale, name, rider=None):
    bsz, seq, _ = q.shape
    nq = seq // TQ
    has_bias = cq is not None

    def body(*refs):
        if has_bias:
            q_ref, k_ref, v_ref, o_ref, do_ref, lse_ref, cq_ref, ck_ref, dq_ref, dk_ref, dv_ref, dcq_ref, dck_ref = refs
        else:
            q_ref, k_ref, v_ref, o_ref, do_ref, lse_ref, dq_ref, dk_ref, dv_ref = refs
        qi = pl.program_id(2)

        @pl.when(qi == 0)
        def _():
            dk_ref[...] = jnp.zeros_like(dk_ref)
            dv_ref[...] = jnp.zeros_like(dv_ref)
            if has_bias:
                dck_ref[...] = jnp.zeros_like(dck_ref)

        rows = qi * TQ + lax.broadcasted_iota(jnp.int32, (TQ, TQ), 0)
        cols0 = lax.broadcasted_iota(jnp.int32, (TQ, TQ), 1)
        lane = lax.broadcasted_iota(jnp.int32, (TQ, LANES), 1)
        do = do_ref[...]
        prod = do * o_ref[...]
        lse2 = lse_ref[...]
        dcqs = []
        for x in range(2):
            hm = jnp.logical_and(lane >= x * HEAD_V, lane < (x + 1) * HEAD_V)
            dox = jnp.where(hm, do, 0.0).astype(BF16)
            delta = jnp.sum(jnp.where(hm, prod, 0.0), axis=1, keepdims=True)
            qx = q_ref[:, x * HP:(x + 1) * HP].astype(BF16)
            lsex = _col(lse2, x)
            cqx = _col(cq_ref[...], x) if has_bias else None

            def step(j, carry, x=x, dox=dox, delta=delta, qx=qx, lsex=lsex, cqx=cqx):
                dq_acc, dcq_acc = carry
                ks = pl.multiple_of(j * TQ, TQ)
                kx = k_ref[pl.ds(ks, TQ), x * HP:(x + 1) * HP].astype(BF16)
                vb = v_ref[pl.ds(ks, TQ), :].astype(BF16)
                s = lax.dot_general(qx, kx, NT, preferred_element_type=F32) * scale
                if has_bias:
                    s = s + (cqx - ck_ref[x:x + 1, pl.ds(ks, TQ)])
                s = jnp.where(rows >= cols0 + ks, s, NEG_INF)
                p = jnp.exp(s - lsex)
                dp = lax.dot_general(dox, vb, NT, preferred_element_type=F32)
                ds = p * (dp - delta)
                dsb = ds.astype(BF16)
                dq_acc = dq_acc + lax.dot_general(dsb, kx, NN, preferred_element_type=F32) * scale
                dk_ref[pl.ds(ks, TQ), x * HP:(x + 1) * HP] += (
                    lax.dot_general(dsb, qx, TN, preferred_element_type=F32) * scale)
                dv_ref[pl.ds(ks, TQ), :] += lax.dot_general(p.astype(BF16), dox, TN, preferred_element_type=F32)
                if has_bias:
                    dcq_acc = dcq_acc + jnp.sum(ds, axis=1, keepdims=True)
                    dck_ref[x:x + 1, pl.ds(ks, TQ)] -= jnp.sum(ds, axis=0, keepdims=True)
                return dq_acc, dcq_acc

            dq_x, dcq_x = lax.fori_loop(0, qi + 1, step, (jnp.zeros((TQ, HP), F32), jnp.zeros((TQ, 1), F32)))
            dq_ref[:, x * HP:(x + 1) * HP] = dq_x
            dcqs.append(dcq_x)
        if has_bias:
            dcq_ref[...] = _two_cols(dcqs[0], dcqs[1])

    qblk = pl.BlockSpec((None, TQ, 2 * HP), lambda b, p, i: (b, i, p))
    kblk = pl.BlockSpec((None, seq, 2 * HP), lambda b, p, i: (b, 0, p))
    oblk = pl.BlockSpec((None, TQ, LANES), lambda b, p, i: (b, i, p))
    cqblk = pl.BlockSpec((None, None, TQ, 2), lambda b, p, i: (b, p, i, 0))
    ckblk = pl.BlockSpec((None, None, 2, seq), lambda b, p, i: (b, p, 0, 0))
    in_specs = [qblk, kblk, pl.BlockSpec((None, seq, LANES), lambda b, p, i: (b, 0, v_blk0 + p)), oblk, oblk, cqblk]
    out_specs = [qblk, kblk, pl.BlockSpec((None, seq, LANES), lambda b, p, i: (b, 0, p))]
    out_shape = [jax.ShapeDtypeStruct((bsz, seq, QKW), F32), jax.ShapeDtypeStruct((bsz, seq, QKW), F32),
                 jax.ShapeDtypeStruct((bsz, seq, 2 * LANES), F32)]
    args = [q, k, v, o, do, lse]
    if has_bias:
        in_specs += [cqblk, ckblk]
        out_specs += [cqblk, ckblk]
        out_shape += [jax.ShapeDtypeStruct((bsz, 2, seq, 2), F32), jax.ShapeDtypeStruct((bsz, 2, 2, seq), F32)]
        args += [cq, ck]
    return _pcall(body, name=name, grid=(bsz, 2, nq), in_specs=in_specs, out_specs=out_specs,
                  out_shape=out_shape, rider=rider)(*args)


def _coords(idx):
    return (idx // 4, (idx // 2) % 2, idx % 2)


class _Xfer:
    def __init__(self, srcs, out_shapes, plan):
        self.srcs, self.out_shapes, self.plan = list(srcs), list(out_shapes), plan
        self.n = len(self.srcs)
        me0 = jnp.int32(0)
        n_pieces = [len(plan(a, me0, me0)) for a in range(self.n)]
        self.offs = np.concatenate([[0], np.cumsum(n_pieces)]).astype(int)
        self.total = int(self.offs[-1])

    def sems(self):
        remote = (N_DEV - 1) * self.total
        return [pltpu.SemaphoreType.DMA((remote,)), pltpu.SemaphoreType.DMA((remote,)),
                pltpu.SemaphoreType.DMA((self.total,))]

    def copies(self, src_refs, out_refs, send_sems, recv_sems, local_sems):
        me = 4 * lax.axis_index("x") + 2 * lax.axis_index("y") + lax.axis_index("c")
        out = []
        for a in range(self.n):
            for pi, (si, di) in enumerate(self.plan(a, me, me)):
                out.append((pltpu.make_async_copy(src_refs[a].at[si], out_refs[a].at[di],
                                                  local_sems.at[self.offs[a] + pi]), None))
        for kk in range(1, N_DEV):
            dest = (me + kk) % N_DEV
            src_dev = (me + N_DEV - kk) % N_DEV
            for a in range(self.n):
                pieces = self.plan(a, me, dest)
                landing = self.plan(a, src_dev, me)
                for pi, ((si, di), (_, li)) in enumerate(zip(pieces, landing)):
                    sem = (kk - 1) * self.total + self.offs[a] + pi
                    mk = functools.partial(pltpu.make_async_remote_copy, src_ref=src_refs[a].at[si],
                                           send_sem=send_sems.at[sem], recv_sem=recv_sems.at[sem],
                                           device_id=_coords(dest), device_id_type=pl.DeviceIdType.MESH)
                    out.append((mk(dst_ref=out_refs[a].at[di]), mk(dst_ref=out_refs[a].at[li])))
        return out

    @staticmethod
    def start(copies):
        for cp, _ in copies:
            cp.start()

    @staticmethod
    def wait(copies):
        for cp, rc in copies:
            if rc is None:
                cp.wait()
            else:
                cp.wait_send()
                rc.wait_recv()


def _exchange(name, xf):
    n = xf.n

    def body(*refs):
        copies = xf.copies(refs[:n], refs[n:2 * n], *refs[2 * n:])
        xf.start(copies)
        xf.wait(copies)

    any_spec = pl.BlockSpec(memory_space=pl.ANY)
    return pl.pallas_call(body, name=name, in_specs=[any_spec] * n, out_specs=[any_spec] * n,
                          out_shape=[jax.ShapeDtypeStruct(s, d) for s, d in xf.out_shapes],
                          scratch_shapes=xf.sems(),
                          compiler_params=pltpu.CompilerParams(has_side_effects=True))(*xf.srcs)


def _rows_of(j, n):
    return pl.ds(pl.multiple_of(j * n, n), n)


def _ffn_cols(j, half):
    return pl.multiple_of((2 * (j % 4) + half) * (2 * FF_BLK) + (j // 4) * FF_BLK, FF_BLK)


def _gather_plan(shards):
    names = list(shards)

    def plan(a, src_dev, _dest):
        nm = names[a]
        j = src_dev
        if nm in ("w_in", "w_uq", "w_ukv", "w_glu", "w_out", "w_down"):
            rows = shards[nm].shape[0]
            return [((slice(None), slice(None)), (_rows_of(j, rows), slice(None)))]
        if nm == "w_branch":
            return [((slice(None), slice(None), slice(None)), (slice(None), slice(None), _rows_of(j, LANES)))]
        if nm in ("w_up", "conv"):
            return [((slice(None), pl.ds(h * FF_BLK, FF_BLK)), (slice(None), pl.ds(_ffn_cols(j, h), FF_BLK)))
                    for h in range(2)]
        raise KeyError(nm)

    return names, plan


def _scatter_plan(names, shard_shapes):
    def plan(a, src_dev, dest):
        nm = names[a]
        j = dest
        if nm in ("w_in", "w_uq", "w_ukv", "w_glu", "w_out", "w_down"):
            rows = shard_shapes[nm][0]
            return [((_rows_of(j, rows), slice(None)), (src_dev, slice(None), slice(None)))]
        if nm == "w_branch":
            return [((slice(None), slice(None), _rows_of(j, LANES)), (src_dev, slice(None), slice(None), slice(None)))]
        if nm in ("w_up", "conv"):
            return [((slice(None), pl.ds(_ffn_cols(j, h), FF_BLK)), (src_dev, slice(None), pl.ds(h * FF_BLK, FF_BLK)))
                    for h in range(2)]
        if nm == "small":
            return [((slice(None), slice(None)), (src_dev, slice(None), slice(None)))]
        raise KeyError(nm)

    return plan


def _pad_in_cols(w):
    lead = w.shape[:-1]
    parts, pos = [], 0
    for dst, src, width in IN_SEGS:
        if dst > pos:
            parts.append(jnp.zeros(lead + (dst - pos,), w.dtype))
        parts.append(w[..., src:src + width])
        pos = dst + width
    return jnp.concatenate(parts, axis=-1)


def _unpad_in_cols(w):
    order = sorted(IN_SEGS, key=lambda s: s[1])
    return jnp.concatenate([w[..., dst:dst + width] for dst, _, width in order], axis=-1)


def _pad_ff(w, axis):
    n = w.shape[axis] // FF_HALF
    parts = []
    for h in range(n):
        piece = lax.slice_in_dim(w, h * FF_HALF, (h + 1) * FF_HALF, axis=axis)
        zshape = list(w.shape)
        zshape[axis] = FF_BLK - FF_HALF
        parts += [piece, jnp.zeros(zshape, w.dtype)]
    return jnp.concatenate(parts, axis=axis)


def _unpad_ff(w, axis):
    n = w.shape[axis] // FF_BLK
    return jnp.concatenate([lax.slice_in_dim(w, h * FF_BLK, h * FF_BLK + FF_HALF, axis=axis) for h in range(n)],
                           axis=axis)


def _to_scan_cols(re_part, im_part):
    lead = re_part.shape[:-1]
    nb = S5_N // SCAN_W
    r = re_part.reshape(lead + (nb, SCAN_W))
    i = im_part.reshape(lead + (nb, SCAN_W))
    return jnp.concatenate([r, i], axis=-1).reshape(lead + (2 * S5_N,))


def _from_scan_cols(x):
    lead = x.shape[:-1]
    nb = S5_N // SCAN_W
    y = x.reshape(lead + (nb, 2, SCAN_W))
    return y[..., 0, :].reshape(lead + (S5_N,)), y[..., 1, :].reshape(lead + (S5_N,))


def _block_diag_in(b):
    eye = jnp.eye(S5_GROUPS, dtype=b.dtype)
    return jnp.einsum("gph,gk->ghkp", b, eye).reshape(S5_WIDTH, S5_N)


def _block_diag_in_t(m):
    eye = jnp.eye(S5_GROUPS, dtype=m.dtype)
    return jnp.einsum("ghkp,gk->gph", m.reshape(S5_GROUPS, S5_GROUP_CH, S5_GROUPS, S5_STATE), eye)


def _block_diag_out(c):
    eye = jnp.eye(S5_GROUPS, dtype=c.dtype)
    return jnp.einsum("ghp,gk->gpkh", c, eye).reshape(S5_N, S5_WIDTH)


def _block_diag_out_t(m):
    eye = jnp.eye(S5_GROUPS, dtype=m.dtype)
    return jnp.einsum("gpkh,gk->ghp", m.reshape(S5_GROUPS, S5_STATE, S5_GROUPS, S5_GROUP_CH), eye)


def _pad_heads(g, width):
    g = jnp.broadcast_to(g.reshape(-1, width), (N_HEADS, width))
    return jnp.pad(g, ((0, 0), (0, HP - width))).reshape(1, QKW)


def _prep_layer(l, gw, small, consts):
    e_mat, p_mat, r_mat, pl_mat = consts
    p = {}
    p["attn_g"] = small["attn_norm_g"][l].reshape(1, D_MODEL)
    p["ffn_g"] = small["ffn_norm_g"][l].reshape(1, D_MODEL)
    p["qlg"] = small["q_lat_norm_g"][l].reshape(1, MLA_Q_RANK)
    p["kvlg"] = small["kv_lat_norm_g"][l].reshape(1, MLA_KV_RANK)
    p["mqg"] = _pad_heads(small["mla_q_norm_g"][l], MLA_QK_DIM)
    p["mkg"] = _pad_heads(small["mla_k_norm_g"][l], MLA_QK_DIM)
    p["fqg"] = _pad_heads(small["fox_q_norm_g"][l], HEAD_V)
    p["fkg"] = _pad_heads(small["fox_k_norm_g"][l], HEAD_V)
    w_uq = gw["w_uq"].reshape(MLA_Q_RANK, N_HEADS, MLA_QK_DIM)
    p["w_uq"] = jnp.pad(w_uq, ((0, 0), (0, 0), (0, HP - MLA_QK_DIM))).reshape(MLA_Q_RANK, QKW)
    w_ukv = gw["w_ukv"].reshape(MLA_KV_RANK, N_HEADS, MLA_NOPE_DIM + HEAD_V)
    p["w_uk"] = jnp.pad(w_ukv[..., :MLA_NOPE_DIM], ((0, 0), (0, 0), (0, HP - MLA_NOPE_DIM))).reshape(MLA_KV_RANK, QKW)
    p["w_uv"] = w_ukv[..., MLA_NOPE_DIM:].reshape(MLA_KV_RANK, N_HEADS * HEAD_V)
    p["w_glu"] = gw["w_glu"]
    p["b_glu"] = small["s5_b_glu"][l].reshape(1, S5_WIDTH)
    p["d_row"] = small["s5_d"][l].reshape(1, S5_WIDTH)
    p["lam_re"] = small["s5_lambda_re"][l].reshape(S5_N, 1)
    p["lam_im"] = small["s5_lambda_im"][l].reshape(S5_N, 1)
    p["log_step"] = jnp.repeat(small["s5_log_step"][l], S5_STATE).reshape(S5_N, 1)
    p["b_re"] = small["s5_b_re"][l].reshape(S5_N, S5_GROUP_CH)
    p["b_im"] = small["s5_b_im"][l].reshape(S5_N, S5_GROUP_CH)
    c_re, c_im = small["s5_c_re"][l], small["s5_c_im"][l]
    p["c_blk"] = _to_scan_cols(_block_diag_out(c_re).T, -_block_diag_out(c_im).T).T
    p["fbias"] = small["fox_f_bias"][l]
    for nm in ("w_in", "w_out", "w_up", "w_down", "w_branch", "conv"):
        p[nm] = gw[nm]
    return p


def _s5_param_call(p, l):
    ins = [p["lam_re"], p["lam_im"], p["log_step"], p["b_re"], p["b_im"]]
    outs = [((S5_N, 1), F32, (S5_N, 1), lambda i: (0, 0), None)] * 2 + \
           [((S5_N, S5_GROUP_CH), F32, (S5_N, S5_GROUP_CH), lambda i: (0, 0), None)] * 2
    return _tile_call(_s5_params, "s5_params", (1,), [_par(a) for a in ins], outs)


def _s5_param_bwd_call(p, cts):
    ins = [p["lam_re"], p["lam_im"], p["log_step"], p["b_re"], p["b_im"]]

    def fn(lr, li, ls, br, bi, g0, g1, g2, g3):
        _, vjp = jax.vjp(_s5_params, lr, li, ls, br, bi)
        return vjp((g0, g1, g2, g3))

    outs = [((S5_N, 1), F32, (S5_N, 1), lambda i: (0, 0), None)] * 3 + \
           [((S5_N, S5_GROUP_CH), F32, (S5_N, S5_GROUP_CH), lambda i: (0, 0), None)] * 2
    return _tile_call(fn, "s5_params_bwd", (1,), [_par(a) for a in ins + list(cts)], outs)


def _layer_fwd(x_prev, f_prev, p, tabs, consts, bsz, riders):
    xout = {}

    def hosted(name, res, n_out):
        xout[name] = res[n_out:]
        return res[:n_out]

    t = x_prev.shape[0]
    seq = t // bsz
    nt = t // TM
    cos, sin = tabs
    e_mat, p_mat, r_mat, pl_mat = consts
    sv = {}

    if f_prev is None:
        x = x_prev
        h = _tile_call(lambda xv, g: _rms(xv, g), "norm_first", (nt,), [_row(x), _par(p["attn_g"])],
                       [_orow(t, D_MODEL, BF16)])[0]
    else:
        x, h = _tile_call(lambda xv, fv, g: (xv + fv, _rms(xv + fv, g)), "norm_attn", (nt,),
                          [_row(x_prev), _row(f_prev), _par(p["attn_g"])], [_orow(t, D_MODEL), _orow(t, D_MODEL, BF16)])
    sv["x"], sv["h"] = x, h
    proj = _mm(h, p["w_in"], "nn", "mm_in")
    sv["proj"] = proj

    def pre_fn(pa, cosv, sinv, qlg, w_uq, kvlg, w_uk, w_uv, mqg, mkg, fqg, fkg, em, pm, rm, plm):
        return _mixer_pre(pa[:, O_CQ:O_CQ + MLA_Q_RANK], pa[:, O_CKV:O_CKV + MLA_KV_RANK], pa[:, O_KR:O_KR + LANES],
                          pa[:, O_FQ:O_FQ + 256], pa[:, O_FK:O_FK + 256], qlg, w_uq, kvlg, w_uk, w_uv,
                          mqg, mkg, fqg, fkg, cosv, sinv, em, pm, rm, plm)

    pre_params = [p["qlg"], p["w_uq"], p["kvlg"], p["w_uk"], p["w_uv"], p["mqg"], p["mkg"], p["fqg"], p["fkg"],
                  e_mat, p_mat, r_mat, pl_mat]
    q_m, k_m, v_m, q_f, k_f = hosted("mixer_pre", _tile_call(
        pre_fn, "mixer_pre", (nt,), [_row(proj, PA), _row(cos), _row(sin)] + [_par(a) for a in pre_params],
        [_orow(t, QKW), _orow(t, QKW), _orow(t, 2 * LANES), _orow(t, QKW), _orow(t, QKW)],
        rider=riders.get("mixer_pre")), 5)
    sv.update(q_m=q_m, k_m=k_m, v_m=v_m, q_f=q_f, k_f=k_f)

    z = proj[:, O_FL:O_FL + N_HEADS].reshape(bsz, seq, N_HEADS).transpose(0, 2, 1).reshape(bsz * N_HEADS, seq)
    fb = jnp.tile(p["fbias"], bsz).reshape(bsz * N_HEADS, 1)
    cum = _forget_fwd(z, fb, "forget_fwd")
    ck = cum.reshape(bsz, 2, 2, seq)
    cq = ck.transpose(0, 1, 3, 2)
    sv.update(z=z, fb=fb, cq=cq, ck=ck)

    r3 = lambda a: a.reshape(bsz, seq, a.shape[-1])
    o_m, lse_m = hosted("attn_mla_fwd", _attn_fwd(r3(q_m), r3(k_m), r3(v_m), 0, None, None, MLA_QK_DIM ** -0.5,
                                                  "attn_mla_fwd", riders.get("attn_mla_fwd")), 2)
    o_f, lse_f = hosted("attn_fox_fwd", _attn_fwd(r3(q_f), r3(k_f), r3(proj), O_FV // LANES, cq, ck, HEAD_V ** -0.5,
                                                  "attn_fox_fwd", riders.get("attn_fox_fwd")), 2)
    o_m, o_f = o_m.reshape(t, 2 * LANES), o_f.reshape(t, 2 * LANES)
    sv.update(o_m=o_m, o_f=o_f, lse_m=lse_m, lse_f=lse_f)

    lbar_re, lbar_im, bbar_re, bbar_im = _s5_param_call(p, 0)
    lbar = _to_scan_cols(lbar_re.reshape(1, S5_N), lbar_im.reshape(1, S5_N))
    b_blk = _to_scan_cols(_block_diag_in(bbar_re.reshape(S5_GROUPS, S5_STATE, S5_GROUP_CH)),
                          _block_diag_in(bbar_im.reshape(S5_GROUPS, S5_STATE, S5_GROUP_CH)))
    sv.update(lbar=lbar, b_blk=b_blk)
    bu = _tile_call(lambda u, bb: bdot(u, bb), "s5_bu", (nt,), [_row(proj, S5_WIDTH, O_U // S5_WIDTH), _par(b_blk)],
                    [_orow(t, 2 * S5_N)])[0]
    state = hosted("s5_scan_fwd", _scan_fwd(bu, lbar, bsz, "s5_scan_fwd", riders.get("s5_scan_fwd")), 1)[0]
    sv["state"] = state
    post_params = [p["c_blk"], p["d_row"], p["w_glu"], p["b_glu"]]
    o_s = _tile_call(_s5_post, "s5_post", (nt,),
                     [_row(state), _row(proj, S5_WIDTH, O_U // S5_WIDTH)] + [_par(a) for a in post_params],
                     [_orow(t, S5_WIDTH)])[0]
    sv["o_s"] = o_s

    def merge_fn(oa, ob, oc, ga, gb, gc, wb):
        return _merge(oa, ob, oc, ga, gb, gc, wb[0], wb[1], wb[2])

    gate_specs = [_row(proj, D_MODEL, PA // D_MODEL + n) for n in range(3)]
    merged = _tile_call(merge_fn, "merge", (nt,), [_row(o_m), _row(o_f), _row(o_s)] + gate_specs + [_par(p["w_branch"])],
                        [_orow(t, D_MODEL, BF16)])[0]
    sv["merged"] = merged
    attn_out = _mm(merged, p["w_out"], "nn", "mm_out")

    x1, h2 = _tile_call(lambda xv, av, g: (xv + av, _rms(xv + av, g)), "norm_ffn", (nt,),
                        [_row(x), _row(attn_out), _par(p["ffn_g"])], [_orow(t, D_MODEL), _orow(t, D_MODEL, BF16)])
    sv["x1"], sv["h2"] = x1, h2
    up = _mm(h2, p["w_up"], "nn", "mm_up")
    sv["up"] = up

    def ffn_fn(upv, cw):
        return _ffn_mid(upv[:, :FF_BLK], upv[:, FF_BLK:], cw[0:1, :FF_BLK], cw[1:2, :FF_BLK], cw[2:3, :FF_BLK],
                        cw[0:1, FF_BLK:], cw[1:2, FF_BLK:], cw[2:3, FF_BLK:])

    nblk = FF_PW // FF_BLK
    act = _tile_call(ffn_fn, "ffn_mid", (nblk, bsz),
                     [(up, (seq, 2 * FF_BLK), lambda j, b: (b, j)), (p["conv"], (3, 2 * FF_BLK), lambda j, b: (0, j))],
                     [((t, FF_PW), BF16, (seq, FF_BLK), lambda j, b: (b, j), None)])[0]
    sv["act"] = act
    ffn_out = _mm(act, p["w_down"], "nn", "mm_down")
    return x1, ffn_out, sv, xout


def _layer_bwd(dx2, p, sv, tabs, consts, bsz, riders):
    xout = {}

    def hosted(name, res, n_out):
        xout[name] = res[n_out:]
        return res[:n_out]

    t = dx2.shape[0]
    seq = t // bsz
    nt = t // TM
    cos, sin = tabs
    e_mat, p_mat, r_mat, pl_mat = consts
    g = {}

    g["w_down"] = _mm(sv["act"], dx2, "tn", "mm_down_dw", out_dtype=BF16)
    dact = _mm(dx2, p["w_down"], "nt", "mm_down_dx")

    def ffn_bwd_fn(upv, cw, da):
        args = (upv[:, :FF_BLK], upv[:, FF_BLK:], cw[0:1, :FF_BLK], cw[1:2, :FF_BLK], cw[2:3, :FF_BLK],
                cw[0:1, FF_BLK:], cw[1:2, FF_BLK:], cw[2:3, FF_BLK:])
        _, vjp = jax.vjp(_ffn_mid, *args)
        dg, dv, g0, g1, g2, v0, v1, v2 = vjp(da)
        return (jnp.concatenate([dg, dv], axis=1), jnp.concatenate([g0, v0], axis=1),
                jnp.concatenate([g1, v1], axis=1), jnp.concatenate([g2, v2], axis=1))

    nblk = FF_PW // FF_BLK
    cw_out = ((1, UP_W), F32, (1, 2 * FF_BLK), lambda j, b: (0, j), (1,))
    dup, dc0, dc1, dc2 = hosted("ffn_mid_bwd", _tile_call(
        ffn_bwd_fn, "ffn_mid_bwd", (nblk, bsz),
        [(sv["up"], (seq, 2 * FF_BLK), lambda j, b: (b, j)), (p["conv"], (3, 2 * FF_BLK), lambda j, b: (0, j)),
         (dact, (seq, FF_BLK), lambda j, b: (b, j))],
        [((t, UP_W), BF16, (seq, 2 * FF_BLK), lambda j, b: (b, j), None), cw_out, cw_out, cw_out],
        rider=riders.get("ffn_mid_bwd")), 4)
    g["conv"] = jnp.concatenate([dc0, dc1, dc2], axis=0)
    g["w_up"] = _mm(sv["h2"], dup, "tn", "mm_up_dw", out_dtype=BF16)
    dh2 = _mm(dup, p["w_up"], "nt", "mm_up_dx")

    def norm_bwd_fn(xv, gv, dh, dres):
        _, vjp = jax.vjp(_rms, xv, gv)
        dxv, dg = vjp(dh)
        return dres + dxv, dg

    d1, g["ffn_g"] = _tile_call(norm_bwd_fn, "norm_bwd", (nt,),
                                [_row(sv["x1"]), _par(p["ffn_g"]), _row(dh2), _row(dx2)],
                                [_orow(t, D_MODEL), _oacc((1, D_MODEL))])

    g["w_out"] = _mm(sv["merged"], d1, "tn", "mm_out_dw", out_dtype=BF16)
    dmerged = _mm(d1, p["w_out"], "nt", "mm_out_dx")

    def merge_bwd_fn(oa, ob, oc, ga, gb, gc, wb, dm):
        wf = wb.astype(F32)
        _, vjp = jax.vjp(_merge, oa, ob, oc, ga, gb, gc, wf[0], wf[1], wf[2])
        doa, dob, doc, dga, dgb, dgc, dwa, dwb, dwc = vjp(dm)
        return doa, dob, doc, jnp.concatenate([dga, dgb, dgc], axis=1), jnp.stack([dwa, dwb, dwc])

    proj = sv["proj"]
    gate_specs = [_row(proj, D_MODEL, PA // D_MODEL + n) for n in range(3)]
    do_m, do_f, do_s, dgl, g["w_branch"] = _tile_call(
        merge_bwd_fn, "merge_bwd", (nt,),
        [_row(sv["o_m"]), _row(sv["o_f"]), _row(sv["o_s"])] + gate_specs + [_par(p["w_branch"]), _row(dmerged)],
        [_orow(t, 2 * LANES), _orow(t, 2 * LANES), _orow(t, S5_WIDTH), _orow(t, 3 * D_MODEL),
         _oacc((3, S5_WIDTH, D_MODEL))])

    def post_bwd_fn(st, u, cb, dr, wg, bg, do):
        _, vjp = jax.vjp(_s5_post, st, u, cb.astype(F32), dr, wg.astype(F32), bg)
        return vjp(do)

    u_spec = _row(proj, S5_WIDTH, O_U // S5_WIDTH)
    dstate, du1, dc_blk, g_d, g["w_glu"], g_bglu = _tile_call(
        post_bwd_fn, "s5_post_bwd", (nt,),
        [_row(sv["state"]), u_spec, _par(p["c_blk"]), _par(p["d_row"]), _par(p["w_glu"]), _par(p["b_glu"]), _row(do_s)],
        [_orow(t, 2 * S5_N), _orow(t, S5_WIDTH), _oacc((2 * S5_N, S5_WIDTH)), _oacc((1, S5_WIDTH)),
         _oacc((S5_WIDTH, S5_WIDTH)), _oacc((1, S5_WIDTH))])
    dbu, dlbar = hosted("s5_scan_bwd", _scan_bwd(dstate, sv["state"], sv["lbar"], bsz, "s5_scan_bwd",
                                                 riders.get("s5_scan_bwd")), 2)

    def bu_bwd_fn(u, bb, dbv, du_in):
        _, vjp = jax.vjp(bdot, u, bb)
        du, dbb = vjp(dbv)
        return du_in + du, dbb

    du, db_blk = _tile_call(bu_bwd_fn, "s5_bu_bwd", (nt,), [u_spec, _par(sv["b_blk"]), _row(dbu), _row(du1)],
                            [_orow(t, S5_WIDTH), _oacc((S5_WIDTH, 2 * S5_N))])
    dlr, dli = _from_scan_cols(dlbar)
    dbr, dbi = _from_scan_cols(db_blk)
    cts = (dlr.reshape(S5_N, 1), dli.reshape(S5_N, 1),
           _block_diag_in_t(dbr).reshape(S5_N, S5_GROUP_CH), _block_diag_in_t(dbi).reshape(S5_N, S5_GROUP_CH))
    g_lr, g_li, g_ls, g_br, g_bi = _s5_param_bwd_call(p, cts)
    dc_re, dc_im = _from_scan_cols(dc_blk.T)
    g["s5_lambda_re"] = g_lr.reshape(S5_GROUPS, S5_STATE)
    g["s5_lambda_im"] = g_li.reshape(S5_GROUPS, S5_STATE)
    g["s5_log_step"] = jnp.sum(g_ls.reshape(S5_GROUPS, S5_STATE), axis=1)
    g["s5_b_re"] = g_br.reshape(S5_GROUPS, S5_STATE, S5_GROUP_CH)
    g["s5_b_im"] = g_bi.reshape(S5_GROUPS, S5_STATE, S5_GROUP_CH)
    g["s5_c_re"] = _block_diag_out_t(dc_re.T)
    g["s5_c_im"] = -_block_diag_out_t(dc_im.T)
    g["s5_d"] = g_d.reshape(S5_GROUPS, S5_GROUP_CH)
    g["s5_b_glu"] = g_bglu.reshape(S5_WIDTH)

    r3 = lambda a: a.reshape(bsz, seq, a.shape[-1])
    dq_m, dk_m, dv_m = hosted("attn_mla_bwd", _attn_bwd(
        r3(sv["q_m"]), r3(sv["k_m"]), r3(sv["v_m"]), 0, r3(sv["o_m"]), r3(do_m), sv["lse_m"], None, None,
        MLA_QK_DIM ** -0.5, "attn_mla_bwd", riders.get("attn_mla_bwd")), 3)
    dq_f, dk_f, dv_f, dcq, dck = hosted("attn_fox_bwd", _attn_bwd(
        r3(sv["q_f"]), r3(sv["k_f"]), r3(proj), O_FV // LANES, r3(sv["o_f"]), r3(do_f), sv["lse_f"], sv["cq"], sv["ck"],
        HEAD_V ** -0.5, "attn_fox_bwd", riders.get("attn_fox_bwd")), 5)
    dcum = (dck + dcq.transpose(0, 1, 3, 2)).reshape(bsz * N_HEADS, seq)
    dz, dfb = _forget_bwd(dcum, sv["z"], sv["fb"], "forget_bwd")
    g["fox_f_bias"] = jnp.sum(dfb.reshape(bsz, N_HEADS), axis=0)
    dfl = jnp.pad(dz.reshape(bsz, N_HEADS, seq).transpose(0, 2, 1).reshape(t, N_HEADS), ((0, 0), (0, LANES - N_HEADS)))

    def pre_bwd_fn(pa, cosv, sinv, qlg, w_uq, kvlg, w_uk, w_uv, mqg, mkg, fqg, fkg, em, pm, rm, plm,
                   gq, gk, gv, gqf, gkf, gvf, gu, gfl, ggl):
        f = functools.partial(_mixer_pre, cos=cosv, sin=sinv, e_mat=em, p_mat=pm, r_mat=rm, pl_mat=plm)
        prim = (pa[:, O_CQ:O_CQ + MLA_Q_RANK], pa[:, O_CKV:O_CKV + MLA_KV_RANK], pa[:, O_KR:O_KR + LANES],
                pa[:, O_FQ:O_FQ + 256], pa[:, O_FK:O_FK + 256], qlg, w_uq.astype(F32), kvlg, w_uk.astype(F32),
                w_uv.astype(F32), mqg, mkg, fqg, fkg)
        _, vjp = jax.vjp(lambda *a: f(*a), *prim)
        dcq_, dckv, dkr, dfq, dfk, dqlg, dwuq, dkvlg, dwuk, dwuv, dmqg, dmkg, dfqg, dfkg = vjp((gq, gk, gv, gqf, gkf))
        zpad = jnp.zeros((pa.shape[0], PA - O_CQ - MLA_Q_RANK), F32)
        dproj = jnp.concatenate([dckv, dfq, dfk, gvf, gu, dkr, gfl, dcq_, zpad, ggl], axis=1)
        return dproj, dqlg, dwuq, dkvlg, dwuk, dwuv, dmqg, dmkg, dfqg, dfkg

    pre_params = [p["qlg"], p["w_uq"], p["kvlg"], p["w_uk"], p["w_uv"], p["mqg"], p["mkg"], p["fqg"], p["fkg"],
                  e_mat, p_mat, r_mat, pl_mat]
    cts_in = [dq_m.reshape(t, QKW), dk_m.reshape(t, QKW), dv_m.reshape(t, 2 * LANES), dq_f.reshape(t, QKW),
              dk_f.reshape(t, QKW), dv_f.reshape(t, 2 * LANES), du, dfl, dgl]
    (dproj, g_qlg, g_wuq, g_kvlg, g_wuk, g_wuv, g_mqg, g_mkg, g_fqg, g_fkg) = _tile_call(
        pre_bwd_fn, "mixer_pre_bwd", (nt,),
        [_row(proj, PA), _row(cos), _row(sin)] + [_par(a) for a in pre_params] + [_row(a) for a in cts_in],
        [_orow(t, PW, BF16), _oacc((1, MLA_Q_RANK)), _oacc((MLA_Q_RANK, QKW)), _oacc((1, MLA_KV_RANK)),
         _oacc((MLA_KV_RANK, QKW)), _oacc((MLA_KV_RANK, N_HEADS * HEAD_V)), _oacc((1, QKW)), _oacc((1, QKW)),
         _oacc((1, QKW)), _oacc((1, QKW))])
    g["q_lat_norm_g"] = g_qlg.reshape(MLA_Q_RANK)
    g["kv_lat_norm_g"] = g_kvlg.reshape(MLA_KV_RANK)
    heads = lambda a, w: jnp.sum(a.reshape(N_HEADS, HP)[:, :w], axis=0)
    g["mla_q_norm_g"], g["mla_k_norm_g"] = heads(g_mqg, MLA_QK_DIM), heads(g_mkg, MLA_QK_DIM)
    g["fox_q_norm_g"], g["fox_k_norm_g"] = heads(g_fqg, HEAD_V), heads(g_fkg, HEAD_V)
    g["w_uq"] = g_wuq.reshape(MLA_Q_RANK, N_HEADS, HP)[..., :MLA_QK_DIM].reshape(MLA_Q_RANK, N_HEADS * MLA_QK_DIM)
    g["w_ukv"] = jnp.concatenate([g_wuk.reshape(MLA_KV_RANK, N_HEADS, HP)[..., :MLA_NOPE_DIM],
                                  g_wuv.reshape(MLA_KV_RANK, N_HEADS, HEAD_V)], axis=-1).reshape(MLA_KV_RANK, QKW)

    g["w_in"] = _mm(sv["h"], dproj, "tn", "mm_in_dw", out_dtype=BF16)
    dh = _mm(dproj, p["w_in"], "nt", "mm_in_dx")
    dx, g["attn_g"] = _tile_call(norm_bwd_fn, "norm_bwd", (nt,),
                                 [_row(sv["x"]), _par(p["attn_g"]), _row(dh), _row(d1)],
                                 [_orow(t, D_MODEL), _oacc((1, D_MODEL))])
    return dx, g, xout


def _row_tile(r, c, max_elems):
    if r * c <= max_elems:
        return r
    best = None
    for d in range(8, r, 8):
        if r % d == 0 and d * c <= max_elems:
            best = d
    assert best is not None, (r, c)
    return best


def _sum8(land, name):
    _, r, c = land.shape
    tr = _row_tile(r, N_DEV * c, 1 << 20)

    def fn(lv):
        acc = lv[0].astype(F32)
        for i in range(1, N_DEV):
            acc = acc + lv[i].astype(F32)
        return acc

    return _tile_call(fn, name, (r // tr,), [(land, (N_DEV, tr, c), lambda i: (0, i, 0))],
                      [((r, c), F32, (tr, c), lambda i: (i, 0), None)])[0]


def _adamw_layer(w, g, m, v, l, prev, name):
    _, r, c = w.shape
    tr = _row_tile(r, c, 1 << 19)
    lay = pl.BlockSpec((None, tr, c), lambda i: (l, i, 0))

    def body(w_ref, g_ref, m_ref, v_ref, *rest):
        g_out, d_out, m_out, v_out = rest[-4:]
        gv = g_ref[...]
        d_out[...], m_out[...], v_out[...] = _adamw(w_ref[...], gv, m_ref[...], v_ref[...])
        g_out[...] = gv

    in_specs = [lay, pl.BlockSpec((tr, c), lambda i: (i, 0)), lay, lay]
    args = [w, g, m, v]
    aliases = {}
    if prev is not None:
        in_specs += [pl.BlockSpec(memory_space=pl.ANY)] * 4
        args += list(prev)
        aliases = {4 + k: k for k in range(4)}
    return _pcall(body, name=name, grid=(r // tr,), in_specs=in_specs, out_specs=[lay] * 4,
                  out_shape=[jax.ShapeDtypeStruct(w.shape, F32)] * 4, aliases=aliases)(*args)


SMALL_NAMES = ("attn_norm_g", "q_lat_norm_g", "kv_lat_norm_g", "mla_q_norm_g", "mla_k_norm_g", "fox_q_norm_g",
               "fox_k_norm_g", "fox_f_bias", "s5_lambda_re", "s5_lambda_im", "s5_b_re", "s5_b_im", "s5_c_re", "s5_c_im",
               "s5_d", "s5_log_step", "s5_b_glu", "ffn_norm_g")
BIG_NAMES = ("w_in", "w_uq", "w_ukv", "s5_w_glu", "w_branch", "w_out", "w_up", "ffn_conv_w", "w_down")
ALL_NAMES = ("attn_norm_g", "w_in", "q_lat_norm_g", "w_uq", "kv_lat_norm_g", "w_ukv", "mla_q_norm_g", "mla_k_norm_g",
             "fox_q_norm_g", "fox_k_norm_g", "fox_f_bias", "s5_lambda_re", "s5_lambda_im", "s5_b_re", "s5_b_im",
             "s5_c_re", "s5_c_im", "s5_d", "s5_log_step", "s5_w_glu", "s5_b_glu", "w_branch", "w_out", "ffn_norm_g",
             "w_up", "ffn_conv_w", "w_down")


def _pack(arrs):
    tile = 8 * LANES
    parts = []
    for a in arrs:
        n = int(np.prod(a.shape))
        tot = -(-n // tile) * tile
        parts.append(jnp.pad(a.reshape(-1), (0, tot - n)).reshape(tot // LANES, LANES))
    return jnp.concatenate(parts, axis=0)


def _unpack(packed, like):
    tile = 8 * LANES
    out, pos = [], 0
    for a in like:
        n = int(np.prod(a.shape))
        rows = -(-n // tile) * 8
        out.append(packed[pos:pos + rows].reshape(-1)[:n].reshape(a.shape))
        pos += rows
    return out


def _rope_tables(positions, t):
    pos = positions.reshape(t, 1)
    inv = _inv_freq_row()

    def fn(pv, iv):
        ang = pv.astype(F32) * iv
        return jnp.cos(ang), jnp.sin(ang)

    return _tile_call(fn, "rope_tables", (t // TM,), [_row(pos), _par(inv)], [_orow(t, QKW), _orow(t, QKW)])


def _loss_call(x1, f, target):
    t = x1.shape[0]

    def fn(xv, fv, tv):
        e = xv + fv - tv
        per_tok = jnp.sum(e * e, axis=1, keepdims=True) * (1.0 / D_MODEL)
        return 0.5 * jnp.sum(per_tok, axis=0, keepdims=True), e * (1.0 / D_MODEL)

    return _tile_call(fn, "loss", (t // TM,), [_row(x1), _row(f), _row(target)], [_oacc((1, 1)), _orow(t, D_MODEL)])


def _local_shards(l, w):
    return {
        "w_in": _pad_in_cols(w["w_in"][l]).astype(BF16),
        "w_uq": w["w_uq"][l].reshape(-1, N_HEADS * MLA_QK_DIM).astype(BF16),
        "w_ukv": w["w_ukv"][l].reshape(-1, QKW).astype(BF16),
        "w_glu": w["s5_w_glu"][l].astype(BF16),
        "w_branch": w["w_branch"][l].astype(BF16),
        "w_out": w["w_out"][l].astype(BF16),
        "w_up": _pad_ff(w["w_up"][l], 1).astype(BF16),
        "conv": _pad_ff(w["ffn_conv_w"][l], 1),
        "w_down": _pad_ff(w["w_down"][l], 0).astype(BF16),
    }


GATHERED_SHAPES = {"w_in": (D_MODEL, PW), "w_uq": (MLA_Q_RANK, N_HEADS * MLA_QK_DIM), "w_ukv": (MLA_KV_RANK, QKW),
                   "w_glu": (S5_WIDTH, S5_WIDTH), "w_branch": (3, S5_WIDTH, D_MODEL), "w_out": (D_MODEL, D_MODEL),
                   "w_up": (D_MODEL, UP_W), "conv": (3, UP_W), "w_down": (FF_PW, D_MODEL)}
FWD_RIDERS = {"mixer_pre": ("w_out", "w_branch"), "attn_mla_fwd": ("w_in", "w_uq", "w_ukv", "w_glu"),
              "attn_fox_fwd": ("w_up", "conv"), "s5_scan_fwd": ("w_down",)}
BWD_RIDERS = {"ffn_mid_bwd": ("w_down",), "s5_scan_bwd": ("w_out", "w_branch", "w_uq", "w_ukv", "w_glu", "conv"),
              "attn_mla_bwd": ("w_in",), "attn_fox_bwd": ("w_up",)}
WEIGHT_OF = {"w_in": "w_in", "w_uq": "w_uq", "w_ukv": "w_ukv", "w_glu": "s5_w_glu", "w_branch": "w_branch",
             "w_out": "w_out", "w_up": "w_up", "conv": "ffn_conv_w", "w_down": "w_down"}


def _unshard_layout(nm, g2):
    if nm == "w_in":
        return _unpad_in_cols(g2)
    if nm == "w_uq":
        return g2.reshape(-1, N_HEADS, MLA_QK_DIM)
    if nm == "w_ukv":
        return g2.reshape(-1, N_HEADS, MLA_NOPE_DIM + HEAD_V)
    if nm in ("w_up", "conv"):
        return _unpad_ff(g2, 1)
    if nm == "w_down":
        return _unpad_ff(g2, 0)
    return g2


def kernel(x, positions, attn_norm_g, w_in, q_lat_norm_g, w_uq, kv_lat_norm_g, w_ukv, mla_q_norm_g, mla_k_norm_g, fox_q_norm_g, fox_k_norm_g, fox_f_bias, s5_lambda_re, s5_lambda_im, s5_b_re, s5_b_im, s5_c_re, s5_c_im, s5_d, s5_log_step, s5_w_glu, s5_b_glu, w_branch, w_out, ffn_norm_g, w_up, ffn_conv_w, w_down, loss_target, m_attn_norm_g, m_w_in, m_q_lat_norm_g, m_w_uq, m_kv_lat_norm_g, m_w_ukv, m_mla_q_norm_g, m_mla_k_norm_g, m_fox_q_norm_g, m_fox_k_norm_g, m_fox_f_bias, m_s5_lambda_re, m_s5_lambda_im, m_s5_b_re, m_s5_b_im, m_s5_c_re, m_s5_c_im, m_s5_d, m_s5_log_step, m_s5_w_glu, m_s5_b_glu, m_w_branch, m_w_out, m_ffn_norm_g, m_w_up, m_ffn_conv_w, m_w_down, v_attn_norm_g, v_w_in, v_q_lat_norm_g, v_w_uq, v_kv_lat_norm_g, v_w_ukv, v_mla_q_norm_g, v_mla_k_norm_g, v_fox_q_norm_g, v_fox_k_norm_g, v_fox_f_bias, v_s5_lambda_re, v_s5_lambda_im, v_s5_b_re, v_s5_b_im, v_s5_c_re, v_s5_c_im, v_s5_d, v_s5_log_step, v_s5_w_glu, v_s5_b_glu, v_w_branch, v_w_out, v_ffn_norm_g, v_w_up, v_ffn_conv_w, v_w_down):
    args = locals()
    w = {n: args[n] for n in ALL_NAMES}
    m = {n: args["m_" + n] for n in ALL_NAMES}
    v = {n: args["v_" + n] for n in ALL_NAMES}
    depth = w_in.shape[0]
    bsz, seq, _ = x.shape
    t = bsz * seq
    consts = _constants()
    tabs = _rope_tables(positions, t)
    small = {n: w[n] for n in SMALL_NAMES}

    all_names = list(GATHERED_SHAPES)
    all_shards = [_local_shards(l, w) for l in range(depth)]

    def gather_xfer(l, names):
        sub = {n: all_shards[l][n] for n in names}
        order, plan = _gather_plan(sub)
        return _Xfer([sub[n] for n in order], [(GATHERED_SHAPES[n], sub[n].dtype) for n in order], plan)

    def scatter_xfer(g, names):
        shapes = {n: _local_shards_shape(n) for n in names}
        return _Xfer([g[n] for n in names], [((N_DEV,) + shapes[n], g[n].dtype) for n in names],
                     _scatter_plan(list(names), shapes))

    def riding(groups, xout):
        return {n: a for host, names in groups.items() for n, a in zip(names, xout[host])}

    xs = x.reshape(t, D_MODEL)
    f_prev = None
    saved, params = [], []
    gathered = dict(zip(all_names, _exchange("gather_weights", gather_xfer(0, all_names))))
    for l in range(depth):
        p = _prep_layer(l, gathered, small, consts)
        riders = {} if l + 1 == depth else {host: gather_xfer(l + 1, names) for host, names in FWD_RIDERS.items()}
        xs, f_prev, sv, xout = _layer_fwd(xs, f_prev, p, tabs, consts, bsz, riders)
        if riders:
            gathered = riding(FWD_RIDERS, xout)
        saved.append(sv)
        params.append(p)

    loss_part, dy = _loss_call(xs, f_prev, loss_target.reshape(t, D_MODEL))
    loss = lax.psum(loss_part[0, 0], ("x", "y", "c"))

    small_grads = {n: [None] * depth for n in SMALL_NAMES}
    big = {n: None for n in BIG_NAMES}

    def finish(l, lands):
        for n in all_names:
            wn, ld = WEIGHT_OF[n], lands[n]
            ld2 = ld.reshape(N_DEV, -1, ld.shape[-1])
            gsum = _unshard_layout(n, _sum8(ld2, "sum_" + n).reshape(_local_shards_shape(n)))
            c = gsum.shape[-1]
            three = lambda a: a.reshape(depth, -1, c)
            big[wn] = _adamw_layer(three(w[wn]), gsum.reshape(-1, c), three(m[wn]), three(v[wn]), l, big[wn],
                                   "adamw_" + n)

    dx = dy
    pending = None
    for l in reversed(range(depth)):
        riders = {} if pending is None else {host: scatter_xfer(pending, names) for host, names in BWD_RIDERS.items()}
        dx, g, xout = _layer_bwd(dx, params[l], saved[l], tabs, consts, bsz, riders)
        if riders:
            finish(l + 1, riding(BWD_RIDERS, xout))
        small_grads["attn_norm_g"][l] = g["attn_g"].reshape(D_MODEL)
        small_grads["ffn_norm_g"][l] = g["ffn_g"].reshape(D_MODEL)
        for n in SMALL_NAMES:
            if n not in ("attn_norm_g", "ffn_norm_g"):
                small_grads[n][l] = g[n]
        pending = g
    finish(0, dict(zip(all_names, _exchange("scatter_grads", scatter_xfer(pending, all_names)))))

    sg = [jnp.stack(small_grads[n]).reshape(w[n].shape) for n in SMALL_NAMES]
    packed = _pack(sg)
    land = _exchange("gather_small_grads", _Xfer([packed], [((N_DEV,) + packed.shape, F32)],
                                                 _scatter_plan(["small"], {})))[0]
    gs = _sum8(land, "sum_small")
    one = lambda names_of: _pack([names_of[n] for n in SMALL_NAMES])[None]
    small_res = _adamw_layer(one(w), gs, one(m), one(v), 0, None, "adamw_small")
    like = [w[n] for n in SMALL_NAMES]
    small_out = [dict(zip(SMALL_NAMES, _unpack(a[0], like))) for a in small_res]

    def out_of(kind, n):
        if n in SMALL_NAMES:
            return small_out[kind][n]
        return big[n][kind].reshape(w[n].shape)

    outs = [loss, dx.reshape(bsz, seq, D_MODEL)]
    for kind in range(4):
        outs += [out_of(kind, n) for n in ALL_NAMES]
    return tuple(outs)


def _local_shards_shape(nm):
    return {"w_in": (D_MODEL // N_DEV, PW), "w_uq": (MLA_Q_RANK // N_DEV, N_HEADS * MLA_QK_DIM),
            "w_ukv": (MLA_KV_RANK // N_DEV, QKW), "w_glu": (S5_WIDTH // N_DEV, S5_WIDTH),
            "w_branch": (3, S5_WIDTH, LANES), "w_out": (D_MODEL // N_DEV, D_MODEL), "w_up": (D_MODEL, 2 * FF_BLK),
            "conv": (3, 2 * FF_BLK), "w_down": (FF_BLK, D_MODEL)}[nm]
```

```python
import functools
import math

import numpy as np
import jax
import jax.numpy as jnp
from jax import lax
from jax.experimental import pallas as pl
from jax.experimental.pallas import tpu as pltpu

F32, BF16 = jnp.float32, jnp.bfloat16

D_MODEL = 1024
N_DEV = 8
MLA_Q_RANK, MLA_KV_RANK, MLA_ROPE_DIM, MLA_NOPE_DIM, MLA_QK_DIM = 384, 256, 32, 64, 96
N_HEADS, HEAD_V = 4, 64
S5_GROUPS, S5_GROUP_CH, S5_STATE, S5_WIDTH = 16, 16, 64, 256
S5_N = S5_GROUPS * S5_STATE
D_FF = 2816
D_IN = 4772
ROPE_THETA = 10000.0
NORM_EPS = 1e-6
NEG_INF = -1e30
ADAM_LR, ADAM_B1, ADAM_B2, ADAM_EPS, ADAM_WD, ADAM_STEP = 0.001, 0.9, 0.999, 1e-08, 0.01, 10

VMEM_LIMIT_BYTES = 56 * 1024 * 1024
LANES = 128
TM = 256
TQ = 256
QC = 128
HP = 128
QKW = N_HEADS * HP

PW = 5120
PA = 2048
IN_SEGS = ((0, 384, 256),
           (256, 672, 256),
           (512, 928, 256),
           (768, 1184, 256),
           (1024, 1444, 256),
           (1280, 640, 32),
           (1408, 1440, 4),
           (1536, 0, 384),
           (2048, 1700, 3072))
O_CKV, O_FQ, O_FK, O_FV, O_U, O_KR, O_FL, O_CQ = 0, 256, 512, 768, 1024, 1280, 1408, 1536

FF_BLK = 384
FF_HALF = D_FF // 8
FF_PW = 8 * FF_BLK
UP_W = 2 * FF_PW

NN = (((1,), (0,)), ((), ()))
NT = (((1,), (1,)), ((), ()))
TN = (((0,), (0,)), ((), ()))


def _pcall(body, *, name, grid, in_specs, out_specs, out_shape, scratch_shapes=(), rider=None, aliases=None):
    params = pltpu.CompilerParams(dimension_semantics=("arbitrary",) * len(grid), vmem_limit_bytes=VMEM_LIMIT_BYTES)
    if rider is None:
        return pl.pallas_call(body, name=name, grid=grid, in_specs=in_specs, out_specs=out_specs, out_shape=out_shape,
                              scratch_shapes=list(scratch_shapes), input_output_aliases=aliases or {},
                              compiler_params=params)
    assert not aliases
    in_specs, out_specs, out_shape = list(in_specs), list(out_specs), list(out_shape)
    n_in, n_out, n_scr, nx = len(in_specs), len(out_specs), len(scratch_shapes), rider.n
    any_spec = pl.BlockSpec(memory_space=pl.ANY)

    def at_step(last):
        conds = [pl.program_id(a) == (g - 1 if last else 0) for a, g in enumerate(grid)]
        return functools.reduce(jnp.logical_and, conds)

    def body_with_rider(*refs):
        ins, xsrc = refs[:n_in], refs[n_in:n_in + nx]
        outs, xout = refs[n_in + nx:n_in + nx + n_out], refs[n_in + nx + n_out:n_in + 2 * nx + n_out]
        scr = refs[n_in + 2 * nx + n_out:n_in + 2 * nx + n_out + n_scr]
        sems = refs[n_in + 2 * nx + n_out + n_scr:]

        @pl.when(at_step(False))
        def _():
            rider.start(rider.copies(xsrc, xout, *sems))

        body(*ins, *outs, *scr)

        @pl.when(at_step(True))
        def _():
            rider.wait(rider.copies(xsrc, xout, *sems))

    call = pl.pallas_call(body_with_rider, name=name, grid=grid, in_specs=in_specs + [any_spec] * nx,
                          out_specs=out_specs + [any_spec] * nx,
                          out_shape=out_shape + [jax.ShapeDtypeStruct(s, d) for s, d in rider.out_shapes],
                          scratch_shapes=list(scratch_shapes) + rider.sems(), compiler_params=params)
    return lambda *args: call(*args, *rider.srcs)


def _tile_call(fn, name, grid, ins, outs, rider=None):
    n_in = len(ins)

    def body(*refs):
        res = fn(*[r[...] for r in refs[:n_in]])
        if not isinstance(res, (tuple, list)):
            res = (res,)
        assert len(res) == len(outs), (name, len(res), len(outs))
        for r, v, o in zip(refs[n_in:], res, outs):
            v = v.astype(r.dtype)
            if o[4] is None:
                r[...] = v
            else:
                first = functools.reduce(jnp.logical_and, [pl.program_id(a) == 0 for a in o[4]])

                @pl.when(first)
                def _():
                    r[...] = v

                @pl.when(jnp.logical_not(first))
                def _():
                    r[...] += v

    res = _pcall(body, name=name, grid=grid,
                 in_specs=[pl.BlockSpec(b, m) for _, b, m in ins],
                 out_specs=[pl.BlockSpec(o[2], o[3]) for o in outs],
                 out_shape=[jax.ShapeDtypeStruct(o[0], o[1]) for o in outs], rider=rider)(*[a for a, _, _ in ins])
    return res


def _row(a, width=None, col_block=0, tm=TM):
    w = a.shape[1] if width is None else width
    return (a, (tm, w), lambda i, c=col_block: (i, c))


def _par(a):
    nd = a.ndim
    return (a, a.shape, lambda i, nd=nd: (0,) * nd)


def _orow(t, w, dtype=F32, tm=TM):
    return ((t, w), dtype, (tm, w), lambda i: (i, 0), None)


def _oacc(shape):
    nd = len(shape)
    return (tuple(shape), F32, tuple(shape), lambda i, nd=nd: (0,) * nd, (0,))


def _pick(n, target):
    best = None
    for d in range(LANES, min(n, target) + 1, LANES):
        if n % d == 0:
            best = d
    return n if best is None else best


def _mm(a, b, mode, name, out_dtype=F32, tm=1024, tn=2048, tk=1024):
    if mode == "nn":
        (m, k), (_, n) = a.shape, b.shape
    elif mode == "nt":
        (m, k), (n, _) = a.shape, b.shape
    else:
        (k, m), (_, n) = a.shape, b.shape
    tm, tn, tk = _pick(m, tm), _pick(n, tn), _pick(k, tk)
    nk = k // tk
    dims = {"nn": NN, "nt": NT, "tn": TN}[mode]
    a_spec = (pl.BlockSpec((tk, tm), lambda i, j, l: (l, i)) if mode == "tn"
              else pl.BlockSpec((tm, tk), lambda i, j, l: (i, l)))
    b_spec = (pl.BlockSpec((tn, tk), lambda i, j, l: (j, l)) if mode == "nt"
              else pl.BlockSpec((tk, tn), lambda i, j, l: (l, j)))

    def body(a_ref, b_ref, o_ref, acc_ref):
        part = lax.dot_general(a_ref[...].astype(BF16), b_ref[...].astype(BF16), dims, preferred_element_type=F32)
        if nk == 1:
            o_ref[...] = part.astype(o_ref.dtype)
        else:
            l = pl.program_id(2)

            @pl.when(l == 0)
            def _():
                acc_ref[...] = part

            @pl.when(l > 0)
            def _():
                acc_ref[...] += part

            @pl.when(l == nk - 1)
            def _():
                o_ref[...] = acc_ref[...].astype(o_ref.dtype)

    return _pcall(body, name=name, grid=(m // tm, n // tn, nk), in_specs=[a_spec, b_spec],
                  out_specs=pl.BlockSpec((tm, tn), lambda i, j, l: (i, j)),
                  out_shape=jax.ShapeDtypeStruct((m, n), out_dtype),
                  scratch_shapes=[pltpu.VMEM((tm, tn) if nk > 1 else (8, LANES), F32)])(a, b)


def _dot(x, w, dims):
    return lax.dot_general(x.astype(BF16), w.astype(BF16), dims, preferred_element_type=F32)


@jax.custom_vjp
def bdot(x, w):
    return _dot(x, w, NN)


def _bdot_fwd(x, w):
    return _dot(x, w, NN), (x, w)


def _bdot_bwd(res, g):
    x, w = res
    return _dot(g, w, NT), _dot(x, g, TN)


bdot.defvjp(_bdot_fwd, _bdot_bwd)


def _split3(x):
    x1 = x.astype(BF16)
    r1 = x - x1.astype(F32)
    x2 = r1.astype(BF16)
    x3 = (r1 - x2.astype(F32)).astype(BF16)
    return x1, x2, x3


def _cdot(x, m, dims):
    return sum(lax.dot_general(p, m, dims, preferred_element_type=F32) for p in _split3(x))


def _block_dot(x, m, transposed):
    k_in = m.shape[1] if transposed else m.shape[0]
    dims = NT if transposed else NN
    return jnp.concatenate([_cdot(x[:, b * k_in:(b + 1) * k_in], m, dims) for b in range(x.shape[1] // k_in)], axis=1)


@jax.custom_vjp
def cdot(x, m):
    return _block_dot(x, m, False)


def _cdot_fwd(x, m):
    return _block_dot(x, m, False), m


def _cdot_bwd(m, g):
    return _block_dot(g, m, True), jnp.zeros_like(m)


cdot.defvjp(_cdot_fwd, _cdot_bwd)


@jax.custom_vjp
def tile_heads(y):
    return jnp.concatenate([y] * N_HEADS, axis=1)


def _tile_heads_fwd(y):
    return jnp.concatenate([y] * N_HEADS, axis=1), None


def _tile_heads_bwd(_, g):
    return (sum(g[:, h * HP:(h + 1) * HP] for h in range(N_HEADS)),)


tile_heads.defvjp(_tile_heads_fwd, _tile_heads_bwd)


def _shift_rows_impl(x, s, reverse):
    n = x.shape[0]
    idx = lax.broadcasted_iota(jnp.int32, x.shape, 0)
    if reverse:
        return jnp.where(idx < n - s, pltpu.roll(x, n - s, 0), 0.0)
    return jnp.where(idx >= s, pltpu.roll(x, s, 0), 0.0)


@functools.partial(jax.custom_vjp, nondiff_argnums=(1,))
def shift_rows(x, s):
    return _shift_rows_impl(x, s, False)


def _shift_rows_fwd(x, s):
    return _shift_rows_impl(x, s, False), None


def _shift_rows_bwd(s, _, g):
    return (_shift_rows_impl(g, s, True),)


shift_rows.defvjp(_shift_rows_fwd, _shift_rows_bwd)


def _rms(x, g):
    return x * lax.rsqrt(jnp.mean(x * x, axis=-1, keepdims=True) + NORM_EPS) * g


def _head_rms(x, e_mat, inv_n, g):
    ms = cdot(x * x, e_mat) * inv_n
    return x * lax.rsqrt(ms + NORM_EPS) * g


def _rope(y, p_mat, cos, sin):
    return y * cos + cdot(y, p_mat) * sin


def _constants():
    e = np.ones((HP, HP), np.float32)
    p = np.zeros((HP, HP), np.float32)
    r = np.zeros((LANES, HP), np.float32)
    pl64 = np.zeros((2 * HEAD_V, 2 * HP), np.float32)
    half = MLA_ROPE_DIM // 2
    for i in range(half):
        x1, x2 = MLA_NOPE_DIM + i, MLA_NOPE_DIM + half + i
        p[x2, x1] = -1.0
        p[x1, x2] = 1.0
    for i in range(MLA_ROPE_DIM):
        r[i, MLA_NOPE_DIM + i] = 1.0
    for h in range(2):
        for i in range(HEAD_V):
            pl64[h * HEAD_V + i, h * HP + i] = 1.0
    return tuple(jnp.asarray(a, BF16) for a in (e, p, r, pl64))


def _inv_freq_row():
    inv = ROPE_THETA ** (-jnp.arange(0, MLA_ROPE_DIM, 2, dtype=F32) / MLA_ROPE_DIM)
    head = jnp.concatenate([jnp.zeros((MLA_NOPE_DIM,), F32), inv, inv, jnp.zeros((HP - MLA_QK_DIM,), F32)])
    return jnp.tile(head, N_HEADS).reshape(1, QKW)


def _mixer_pre(c_q, c_kv, kr, fq, fk, qlg, w_uq, kvlg, w_uk, w_uv, mqg, mkg, fqg, fkg, cos, sin, e_mat, p_mat, r_mat, pl_mat):
    q = bdot(_rms(c_q, qlg), w_uq)
    ckvn = _rms(c_kv, kvlg)
    k = bdot(ckvn, w_uk) + tile_heads(cdot(kr, r_mat))
    v = bdot(ckvn, w_uv)
    q = _rope(_head_rms(q, e_mat, 1.0 / MLA_QK_DIM, mqg), p_mat, cos, sin)
    k = _rope(_head_rms(k, e_mat, 1.0 / MLA_QK_DIM, mkg), p_mat, cos, sin)
    qf = _head_rms(cdot(fq, pl_mat), e_mat, 1.0 / HEAD_V, fqg)
    kf = _head_rms(cdot(fk, pl_mat), e_mat, 1.0 / HEAD_V, fkg)
    return q, k, v, qf, kf


def _s5_post(state, u, c_blk, d_row, w_glu, b_glu):
    y = bdot(state, c_blk) + d_row * u
    y = jax.nn.gelu(y)
    return y * jax.nn.sigmoid(bdot(y, w_glu) + b_glu)


def _merge(o_a, o_b, o_c, g_a, g_b, g_c, w_a, w_b, w_c):
    return (jax.nn.sigmoid(g_a) * bdot(o_a, w_a) + jax.nn.sigmoid(g_b) * bdot(o_b, w_b)
            + jax.nn.sigmoid(g_c) * bdot(o_c, w_c))


def _ffn_mid(gate, val, wg0, wg1, wg2, wv0, wv1, wv2):
    cg = wg0 * shift_rows(gate, 2) + wg1 * shift_rows(gate, 1) + wg2 * gate
    cv = wv0 * shift_rows(val, 2) + wv1 * shift_rows(val, 1) + wv2 * val
    return jax.nn.silu(cg) * cv


def _s5_params(lam_re, lam_im, log_step, b_re, b_im):
    step = jnp.exp(log_step)
    zr, zi = lam_re * step, lam_im * step
    mag = jnp.exp(zr)
    lr, li = mag * jnp.cos(zi), mag * jnp.sin(zi)
    nr, ni = lr - 1.0, li
    den = lam_re * lam_re + lam_im * lam_im
    cr = (nr * lam_re + ni * lam_im) / den
    ci = (ni * lam_re - nr * lam_im) / den
    return lr, li, cr * b_re - ci * b_im, cr * b_im + ci * b_re


def _adamw(w, g, m, v):
    m = ADAM_B1 * m + (1.0 - ADAM_B1) * g
    v = ADAM_B2 * v + (1.0 - ADAM_B2) * (g * g)
    m_hat = m / (1.0 - ADAM_B1 ** ADAM_STEP)
    v_hat = v / (1.0 - ADAM_B2 ** ADAM_STEP)
    delta = -ADAM_LR * (m_hat / (jnp.sqrt(v_hat) + ADAM_EPS) + ADAM_WD * w)
    return delta, m, v


SCAN_W = 256


def _scan_seq(xr, xi, ar, ai, reverse):
    n = xr.shape[0]
    s = 1
    while s < n:
        sr, si = _shift_rows_impl(xr, s, reverse), _shift_rows_impl(xi, s, reverse)
        xr, xi = xr + ar * sr - ai * si, xi + ar * si + ai * sr
        ar, ai = ar * ar - ai * ai, 2.0 * ar * ai
        s *= 2
    return xr, xi


def _scan_fwd(bu, lbar, bsz, name, rider=None):
    t = bu.shape[0]
    seq = t // bsz
    nb = bu.shape[1] // (2 * SCAN_W)

    def fn(b, a):
        xr, xi = _scan_seq(b[:, :SCAN_W], b[:, SCAN_W:], a[:, :SCAN_W], a[:, SCAN_W:], False)
        return jnp.concatenate([xr, xi], axis=1)

    blk = (seq, 2 * SCAN_W)
    return _tile_call(fn, name, (bsz, nb),
                      [(bu, blk, lambda i, j: (i, j)), (lbar, (1, 2 * SCAN_W), lambda i, j: (0, j))],
                      [((t, bu.shape[1]), F32, blk, lambda i, j: (i, j), None)], rider=rider)


def _scan_bwd(dstate, state, lbar, bsz, name, rider=None):
    t = state.shape[0]
    seq = t // bsz
    nb = state.shape[1] // (2 * SCAN_W)

    def fn(g, x, a):
        ar, ai = a[:, :SCAN_W], a[:, SCAN_W:]
        gr, gi = _scan_seq(g[:, :SCAN_W], g[:, SCAN_W:], ar, -ai, True)
        pr, pi = _shift_rows_impl(x[:, :SCAN_W], 1, False), _shift_rows_impl(x[:, SCAN_W:], 1, False)
        dar = jnp.sum(gr * pr + gi * pi, axis=0, keepdims=True)
        dai = jnp.sum(gi * pr - gr * pi, axis=0, keepdims=True)
        return jnp.concatenate([gr, gi], axis=1), jnp.concatenate([dar, dai], axis=1)

    blk = (seq, 2 * SCAN_W)
    return _tile_call(fn, name, (nb, bsz),
                      [(dstate, blk, lambda j, i: (i, j)), (state, blk, lambda j, i: (i, j)),
                       (lbar, (1, 2 * SCAN_W), lambda j, i: (0, j))],
                      [((t, state.shape[1]), F32, blk, lambda j, i: (i, j), None),
                       ((1, state.shape[1]), F32, (1, 2 * SCAN_W), lambda j, i: (0, j), (1,))], rider=rider)


def _shift_lanes(x, s, reverse):
    n = x.shape[1]
    idx = lax.broadcasted_iota(jnp.int32, x.shape, 1)
    if reverse:
        return jnp.where(idx < n - s, pltpu.roll(x, n - s, 1), 0.0)
    return jnp.where(idx >= s, pltpu.roll(x, s, 1), 0.0)


def _cumsum_lanes(x, reverse):
    s = 1
    while s < x.shape[1]:
        x = x + _shift_lanes(x, s, reverse)
        s *= 2
    return x


def _forget_fwd(z, bias, name):
    def fn(zv, bv):
        x = zv + bv
        logf = jnp.minimum(x, 0.0) - jnp.log(1.0 + jnp.exp(-jnp.abs(x)))
        return _cumsum_lanes(logf, False)

    return _tile_call(fn, name, (1,), [_par(z), _par(bias)],
                      [(z.shape, F32, z.shape, lambda i: (0, 0), None)])[0]


def _forget_bwd(dc, z, bias, name):
    def fn(dcv, zv, bv):
        x = zv + bv
        dlogf = _cumsum_lanes(dcv, True)
        dz = dlogf * jax.nn.sigmoid(-x)
        return dz, jnp.sum(dz, axis=1, keepdims=True)

    rows = z.shape[0]
    return _tile_call(fn, name, (1,), [_par(dc), _par(z), _par(bias)],
                      [(z.shape, F32, z.shape, lambda i: (0, 0), None),
                       ((rows, 1), F32, (rows, 1), lambda i: (0, 0), None)])


def _col(v2, x):
    idx = lax.broadcasted_iota(jnp.int32, v2.shape, 1)
    return jnp.sum(jnp.where(idx == x, v2, 0.0), axis=1, keepdims=True)


def _two_cols(c0, c1):
    idx = lax.broadcasted_iota(jnp.int32, (c0.shape[0], 2), 1)
    return jnp.where(idx == 0, c0, c1)


def _rows2(r0, r1):
    idx = lax.broadcasted_iota(jnp.int32, (2, r0.shape[1]), 0)
    return jnp.where(idx == 0, r0, r1)


def _attn_specs(seq, v_blk0, has_bias):
    qblk = pl.BlockSpec((None, TQ, 2 * HP), lambda b, p, i: (b, i, p))
    kblk = pl.BlockSpec((None, seq, 2 * HP), lambda b, p, i: (b, 0, p))
    vblk = pl.BlockSpec((None, seq, LANES), lambda b, p, i: (b, 0, v_blk0 + p))
    oblk = pl.BlockSpec((None, TQ, LANES), lambda b, p, i: (b, i, p))
    rowblk = pl.BlockSpec((None, None, 2, TQ), lambda b, p, i: (b, p, 0, i))
    colblk = pl.BlockSpec((None, None, seq, 2), lambda b, p, i: (b, p, 0, 0))
    return qblk, kblk, vblk, oblk, rowblk, colblk


def _attn_fwd(q, k, v, v_blk0, c_cols, c_rows, scale, name, rider=None):
    bsz, seq, _ = q.shape
    nq = seq // TQ
    has_bias = c_cols is not None
    n_in = 5 if has_bias else 3
    nc = TQ // QC

    def body(*refs):
        q_ref, k_ref, v_ref = refs[:3]
        cq_ref, ck_ref = refs[3:5] if has_bias else (None, None)
        o_ref, lse_ref = refs[n_in:]
        qi = pl.program_id(2)
        qb = [[q_ref[c * QC:(c + 1) * QC, x * HP:(x + 1) * HP].astype(BF16) for c in range(nc)] for x in range(2)]
        key_in = lax.broadcasted_iota(jnp.int32, (TQ, QC), 0)
        qry_in = lax.broadcasted_iota(jnp.int32, (TQ, QC), 1)

        def block(j, carry, masked):
            ks = pl.multiple_of(j * TQ, TQ)
            vt = v_ref[pl.ds(ks, TQ), :].T.astype(BF16)
            new = []
            for x in range(2):
                kx = k_ref[pl.ds(ks, TQ), x * HP:(x + 1) * HP].astype(BF16)
                ckx = _col(ck_ref[pl.ds(ks, TQ), :], x) if has_bias else None
                for c in range(nc):
                    m, l, acc = carry[x * nc + c]
                    s = lax.dot_general(kx, qb[x][c], NT, preferred_element_type=F32) * scale
                    if has_bias:
                        s = s + (cq_ref[x:x + 1, c * QC:(c + 1) * QC] - ckx)
                    if masked:
                        s = jnp.where(qry_in + c * QC >= key_in, s, NEG_INF)
                    m_new = jnp.maximum(m, jnp.max(s, axis=0, keepdims=True))
                    alpha = jnp.exp(m - m_new)
                    p = jnp.exp(s - m_new)
                    l = alpha * l + jnp.sum(p, axis=0, keepdims=True)
                    acc = alpha * acc + lax.dot_general(vt, p.astype(BF16), NN, preferred_element_type=F32)
                    new.append((m_new, l, acc))
            return tuple(new)

        init = tuple((jnp.full((1, QC), NEG_INF, F32), jnp.zeros((1, QC), F32), jnp.zeros((LANES, QC), F32))
                     for _ in range(2 * nc))
        carry = lax.fori_loop(0, qi, lambda j, cr: block(j, cr, False), init)
        carry = block(qi, carry, True)
        lane = lax.broadcasted_iota(jnp.int32, (QC, LANES), 1)
        lse_rows = []
        for x in range(2):
            lse_rows.append(jnp.concatenate([carry[x * nc + c][0] + jnp.log(carry[x * nc + c][1]) for c in range(nc)],
                                            axis=1))
        for c in range(nc):
            (_, l0, a0), (_, l1, a1) = carry[c], carry[nc + c]
            o_ref[c * QC:(c + 1) * QC, :] = jnp.where(lane < HEAD_V, (a0 / l0).T, (a1 / l1).T)
        lse_ref[...] = _rows2(lse_rows[0], lse_rows[1])

    qblk, kblk, vblk, oblk, rowblk, colblk = _attn_specs(seq, v_blk0, has_bias)
    in_specs, args = [qblk, kblk, vblk], [q, k, v]
    if has_bias:
        in_specs += [rowblk, colblk]
        args += [c_rows, c_cols]
    return _pcall(body, name=name, grid=(bsz, 2, nq), in_specs=in_specs, out_specs=[oblk, rowblk],
                  out_shape=[jax.ShapeDtypeStruct((bsz, seq, 2 * LANES), F32),
                             jax.ShapeDtypeStruct((bsz, 2, 2, seq), F32)], rider=rider)(*args)


def _attn_bwd(q, k, v, v_blk0, o, do, lse, c_cols, c_rows, scale, name, rider=None):
    bsz, seq, _ = q.shape
    nq = seq // TQ
    has_bias = c_cols is not None
    n_in = 8 if has_bias else 6
    nc = TQ // QC

    def body(*refs):
        q_ref, k_ref, v_ref, o_ref, do_ref, lse_ref = refs[:6]
        cq_ref, ck_ref = refs[6:8] if has_bias else (None, None)
        dq_ref, dk_ref, dv_ref = refs[n_in:n_in + 3]
        dck_ref, dcq_ref = refs[n_in + 3:n_in + 5] if has_bias else (None, None)
        qi = pl.program_id(2)

        @pl.when(qi == 0)
        def _():
            dk_ref[...] = jnp.zeros_like(dk_ref)
            dv_ref[...] = jnp.zeros_like(dv_ref)
            if has_bias:
                dck_ref[...] = jnp.zeros_like(dck_ref)

        lane = lax.broadcasted_iota(jnp.int32, (QC, LANES), 1)
        ones8 = jnp.ones((8, LANES), BF16)
        qb, dob, delta, lse_r = [], [], [], []
        for x in range(2):
            hm = jnp.logical_and(lane >= x * HEAD_V, lane < (x + 1) * HEAD_V)
            qb.append([]), dob.append([]), delta.append([]), lse_r.append([])
            for c in range(nc):
                rows = slice(c * QC, (c + 1) * QC)
                do_c = jnp.where(hm, do_ref[rows, :], 0.0)
                prod = do_c * o_ref[rows, :]
                hi = prod.astype(BF16)
                lo = (prod - hi.astype(F32)).astype(BF16)
                d8 = (lax.dot_general(ones8, hi, NT, preferred_element_type=F32)
                      + lax.dot_general(ones8, lo, NT, preferred_element_type=F32))
                delta[x].append(d8[0:1, :])
                dob[x].append(do_c.astype(BF16))
                qb[x].append(q_ref[rows, x * HP:(x + 1) * HP].astype(BF16))
                lse_r[x].append(lse_ref[x:x + 1, rows])
        key_in = lax.broadcasted_iota(jnp.int32, (TQ, QC), 0)
        qry_in = lax.broadcasted_iota(jnp.int32, (TQ, QC), 1)

        def block(j, carry, masked):
            ks = pl.multiple_of(j * TQ, TQ)
            vb = v_ref[pl.ds(ks, TQ), :].astype(BF16)
            new, key_sums = [], []
            for x in range(2):
                k32 = k_ref[pl.ds(ks, TQ), x * HP:(x + 1) * HP]
                kx, kt = k32.astype(BF16), k32.T.astype(BF16)
                ckx = _col(ck_ref[pl.ds(ks, TQ), :], x) if has_bias else None
                dk_acc = jnp.zeros((TQ, HP), F32)
                dv_acc = jnp.zeros((TQ, LANES), F32)
                key_sum = jnp.zeros((TQ, 1), F32)
                for c in range(nc):
                    dqt, dcq = carry[x * nc + c]
                    s = lax.dot_general(kx, qb[x][c], NT, preferred_element_type=F32) * scale
                    if has_bias:
                        s = s + (cq_ref[x:x + 1, c * QC:(c + 1) * QC] - ckx)
                    if masked:
                        s = jnp.where(qry_in + c * QC >= key_in, s, NEG_INF)
                    p = jnp.exp(s - lse_r[x][c])
                    dp = lax.dot_general(vb, dob[x][c], NT, preferred_element_type=F32)
                    ds = p * (dp - delta[x][c])
                    dsb = ds.astype(BF16)
                    dk_acc = dk_acc + lax.dot_general(dsb, qb[x][c], NN, preferred_element_type=F32)
                    dv_acc = dv_acc + lax.dot_general(p.astype(BF16), dob[x][c], NN, preferred_element_type=F32)
                    dqt = dqt + lax.dot_general(kt, dsb, NN, preferred_element_type=F32)
                    if has_bias:
                        dcq = dcq + jnp.sum(ds, axis=0, keepdims=True)
                        key_sum = key_sum + jnp.sum(ds, axis=1, keepdims=True)
                    new.append((dqt, dcq))
                dk_ref[pl.ds(ks, TQ), x * HP:(x + 1) * HP] += dk_acc * scale
                dv_ref[pl.ds(ks, TQ), :] += dv_acc
                key_sums.append(key_sum)
            if has_bias:
                dck_ref[pl.ds(ks, TQ), :] -= _two_cols(key_sums[0], key_sums[1])
            return tuple(new)

        init = tuple((jnp.zeros((HP, QC), F32), jnp.zeros((1, QC), F32)) for _ in range(2 * nc))
        carry = lax.fori_loop(0, qi, lambda j, cr: block(j, cr, False), init)
        carry = block(qi, carry, True)
        for x in range(2):
            for c in range(nc):
                dq_ref[c * QC:(c + 1) * QC, x * HP:(x + 1) * HP] = carry[x * nc + c][0].T * scale
        if has_bias:
            dcq_ref[...] = _rows2(jnp.concatenate([carry[c][1] for c in range(nc)], axis=1),
                                  jnp.concatenate([carry[nc + c][1] for c in range(nc)], axis=1))

    qblk, kblk, vblk, oblk, rowblk, colblk = _attn_specs(seq, v_blk0, has_bias)
    in_specs = [qblk, kblk, vblk, oblk, oblk, rowblk]
    out_specs = [qblk, kblk, pl.BlockSpec((None, seq, LANES), lambda b, p, i: (b, 0, p))]
    out_shape = [jax.ShapeDtypeStruct((bsz, seq, QKW), F32), jax.ShapeDtypeStruct((bsz, seq, QKW), F32),
                 jax.ShapeDtypeStruct((bsz, seq, 2 * LANES), F32)]
    args = [q, k, v, o, do, lse]
    if has_bias:
        in_specs += [rowblk, colblk]
        out_specs += [colblk, rowblk]
        out_shape += [jax.ShapeDtypeStruct((bsz, 2, seq, 2), F32), jax.ShapeDtypeStruct((bsz, 2, 2, seq), F32)]
        args += [c_rows, c_cols]
    return _pcall(body, name=name, grid=(bsz, 2, nq), in_specs=in_specs, out_specs=out_specs,
                  out_shape=out_shape, rider=rider)(*args)


def _coords(idx):
    return (idx // 4, (idx // 2) % 2, idx % 2)


class _Xfer:
    def __init__(self, srcs, out_shapes, plan):
        self.srcs, self.out_shapes, self.plan = list(srcs), list(out_shapes), plan
        self.n = len(self.srcs)
        me0 = jnp.int32(0)
        n_pieces = [len(plan(a, me0, me0)) for a in range(self.n)]
        self.offs = np.concatenate([[0], np.cumsum(n_pieces)]).astype(int)
        self.total = int(self.offs[-1])

    def sems(self):
        remote = (N_DEV - 1) * self.total
        return [pltpu.SemaphoreType.DMA((remote,)), pltpu.SemaphoreType.DMA((remote,)),
                pltpu.SemaphoreType.DMA((self.total,))]

    def copies(self, src_refs, out_refs, send_sems, recv_sems, local_sems):
        me = 4 * lax.axis_index("x") + 2 * lax.axis_index("y") + lax.axis_index("c")
        out = []
        for a in range(self.n):
            for pi, (si, di) in enumerate(self.plan(a, me, me)):
                out.append((pltpu.make_async_copy(src_refs[a].at[si], out_refs[a].at[di],
                                                  local_sems.at[self.offs[a] + pi]), None))
        for kk in range(1, N_DEV):
            dest = (me + kk) % N_DEV
            src_dev = (me + N_DEV - kk) % N_DEV
            for a in range(self.n):
                pieces = self.plan(a, me, dest)
                landing = self.plan(a, src_dev, me)
                for pi, ((si, di), (_, li)) in enumerate(zip(pieces, landing)):
                    sem = (kk - 1) * self.total + self.offs[a] + pi
                    mk = functools.partial(pltpu.make_async_remote_copy, src_ref=src_refs[a].at[si],
                                           send_sem=send_sems.at[sem], recv_sem=recv_sems.at[sem],
                                           device_id=_coords(dest), device_id_type=pl.DeviceIdType.MESH)
                    out.append((mk(dst_ref=out_refs[a].at[di]), mk(dst_ref=out_refs[a].at[li])))
        return out

    @staticmethod
    def start(copies):
        for cp, _ in copies:
            cp.start()

    @staticmethod
    def wait(copies):
        for cp, rc in copies:
            if rc is None:
                cp.wait()
            else:
                cp.wait_send()
                rc.wait_recv()


def _exchange(name, xf):
    n = xf.n

    def body(*refs):
        copies = xf.copies(refs[:n], refs[n:2 * n], *refs[2 * n:])
        xf.start(copies)
        xf.wait(copies)

    any_spec = pl.BlockSpec(memory_space=pl.ANY)
    return pl.pallas_call(body, name=name, in_specs=[any_spec] * n, out_specs=[any_spec] * n,
                          out_shape=[jax.ShapeDtypeStruct(s, d) for s, d in xf.out_shapes],
                          scratch_shapes=xf.sems(),
                          compiler_params=pltpu.CompilerParams(has_side_effects=True))(*xf.srcs)


def _rows_of(j, n):
    return pl.ds(pl.multiple_of(j * n, n), n)


def _ffn_cols(j, half):
    return pl.multiple_of((2 * (j % 4) + half) * (2 * FF_BLK) + (j // 4) * FF_BLK, FF_BLK)


def _gather_plan(shards):
    names = list(shards)

    def plan(a, src_dev, _dest):
        nm = names[a]
        j = src_dev
        if nm in ("w_in", "w_uq", "w_ukv", "w_glu", "w_out", "w_down"):
            rows = shards[nm].shape[0]
            return [((slice(None), slice(None)), (_rows_of(j, rows), slice(None)))]
        if nm == "w_branch":
            return [((slice(None), slice(None), slice(None)), (slice(None), slice(None), _rows_of(j, LANES)))]
        if nm in ("w_up", "conv"):
            return [((slice(None), pl.ds(h * FF_BLK, FF_BLK)), (slice(None), pl.ds(_ffn_cols(j, h), FF_BLK)))
                    for h in range(2)]
        raise KeyError(nm)

    return names, plan


def _scatter_plan(names, shard_shapes):
    def plan(a, src_dev, dest):
        nm = names[a]
        j = dest
        if nm in ("w_in", "w_uq", "w_ukv", "w_glu", "w_out", "w_down"):
            rows = shard_shapes[nm][0]
            return [((_rows_of(j, rows), slice(None)), (src_dev, slice(None), slice(None)))]
        if nm == "w_branch":
            return [((slice(None), slice(None), _rows_of(j, LANES)), (src_dev, slice(None), slice(None), slice(None)))]
        if nm in ("w_up", "conv"):
            return [((slice(None), pl.ds(_ffn_cols(j, h), FF_BLK)), (src_dev, slice(None), pl.ds(h * FF_BLK, FF_BLK)))
                    for h in range(2)]
        if nm == "small":
            return [((slice(None), slice(None)), (src_dev, slice(None), slice(None)))]
        raise KeyError(nm)

    return plan


def _pad_in_cols(w):
    lead = w.shape[:-1]
    parts, pos = [], 0
    for dst, src, width in IN_SEGS:
        if dst > pos:
            parts.append(jnp.zeros(lead + (dst - pos,), w.dtype))
        parts.append(w[..., src:src + width])
        pos = dst + width
    return jnp.concatenate(parts, axis=-1)


def _unpad_in_cols(w):
    order = sorted(IN_SEGS, key=lambda s: s[1])
    return jnp.concatenate([w[..., dst:dst + width] for dst, _, width in order], axis=-1)


def _pad_ff(w, axis):
    n = w.shape[axis] // FF_HALF
    parts = []
    for h in range(n):
        piece = lax.slice_in_dim(w, h * FF_HALF, (h + 1) * FF_HALF, axis=axis)
        zshape = list(w.shape)
        zshape[axis] = FF_BLK - FF_HALF
        parts += [piece, jnp.zeros(zshape, w.dtype)]
    return jnp.concatenate(parts, axis=axis)


def _unpad_ff(w, axis):
    n = w.shape[axis] // FF_BLK
    return jnp.concatenate([lax.slice_in_dim(w, h * FF_BLK, h * FF_BLK + FF_HALF, axis=axis) for h in range(n)],
                           axis=axis)


def _to_scan_cols(re_part, im_part):
    lead = re_part.shape[:-1]
    nb = S5_N // SCAN_W
    r = re_part.reshape(lead + (nb, SCAN_W))
    i = im_part.reshape(lead + (nb, SCAN_W))
    return jnp.concatenate([r, i], axis=-1).reshape(lead + (2 * S5_N,))


def _from_scan_cols(x):
    lead = x.shape[:-1]
    nb = S5_N // SCAN_W
    y = x.reshape(lead + (nb, 2, SCAN_W))
    return y[..., 0, :].reshape(lead + (S5_N,)), y[..., 1, :].reshape(lead + (S5_N,))


def _block_diag_in(b):
    eye = jnp.eye(S5_GROUPS, dtype=b.dtype)
    return jnp.einsum("gph,gk->ghkp", b, eye).reshape(S5_WIDTH, S5_N)


def _block_diag_in_t(m):
    eye = jnp.eye(S5_GROUPS, dtype=m.dtype)
    return jnp.einsum("ghkp,gk->gph", m.reshape(S5_GROUPS, S5_GROUP_CH, S5_GROUPS, S5_STATE), eye)


def _block_diag_out(c):
    eye = jnp.eye(S5_GROUPS, dtype=c.dtype)
    return jnp.einsum("ghp,gk->gpkh", c, eye).reshape(S5_N, S5_WIDTH)


def _block_diag_out_t(m):
    eye = jnp.eye(S5_GROUPS, dtype=m.dtype)
    return jnp.einsum("gpkh,gk->ghp", m.reshape(S5_GROUPS, S5_STATE, S5_GROUPS, S5_GROUP_CH), eye)


def _pad_heads(g, width):
    g = jnp.broadcast_to(g.reshape(-1, width), (N_HEADS, width))
    return jnp.pad(g, ((0, 0), (0, HP - width))).reshape(1, QKW)


def _prep_layer(l, gw, small, consts):
    e_mat, p_mat, r_mat, pl_mat = consts
    p = {}
    p["attn_g"] = small["attn_norm_g"][l].reshape(1, D_MODEL)
    p["ffn_g"] = small["ffn_norm_g"][l].reshape(1, D_MODEL)
    p["qlg"] = small["q_lat_norm_g"][l].reshape(1, MLA_Q_RANK)
    p["kvlg"] = small["kv_lat_norm_g"][l].reshape(1, MLA_KV_RANK)
    p["mqg"] = _pad_heads(small["mla_q_norm_g"][l], MLA_QK_DIM)
    p["mkg"] = _pad_heads(small["mla_k_norm_g"][l], MLA_QK_DIM)
    p["fqg"] = _pad_heads(small["fox_q_norm_g"][l], HEAD_V)
    p["fkg"] = _pad_heads(small["fox_k_norm_g"][l], HEAD_V)
    w_uq = gw["w_uq"].reshape(MLA_Q_RANK, N_HEADS, MLA_QK_DIM)
    p["w_uq"] = jnp.pad(w_uq, ((0, 0), (0, 0), (0, HP - MLA_QK_DIM))).reshape(MLA_Q_RANK, QKW)
    w_ukv = gw["w_ukv"].reshape(MLA_KV_RANK, N_HEADS, MLA_NOPE_DIM + HEAD_V)
    p["w_uk"] = jnp.pad(w_ukv[..., :MLA_NOPE_DIM], ((0, 0), (0, 0), (0, HP - MLA_NOPE_DIM))).reshape(MLA_KV_RANK, QKW)
    p["w_uv"] = w_ukv[..., MLA_NOPE_DIM:].reshape(MLA_KV_RANK, N_HEADS * HEAD_V)
    p["w_glu"] = gw["w_glu"]
    p["b_glu"] = small["s5_b_glu"][l].reshape(1, S5_WIDTH)
    p["d_row"] = small["s5_d"][l].reshape(1, S5_WIDTH)
    p["lam_re"] = small["s5_lambda_re"][l].reshape(S5_N, 1)
    p["lam_im"] = small["s5_lambda_im"][l].reshape(S5_N, 1)
    p["log_step"] = jnp.repeat(small["s5_log_step"][l], S5_STATE).reshape(S5_N, 1)
    p["b_re"] = small["s5_b_re"][l].reshape(S5_N, S5_GROUP_CH)
    p["b_im"] = small["s5_b_im"][l].reshape(S5_N, S5_GROUP_CH)
    c_re, c_im = small["s5_c_re"][l], small["s5_c_im"][l]
    p["c_blk"] = _to_scan_cols(_block_diag_out(c_re).T, -_block_diag_out(c_im).T).T
    p["fbias"] = small["fox_f_bias"][l]
    for nm in ("w_in", "w_out", "w_up", "w_down", "w_branch", "conv"):
        p[nm] = gw[nm]
    return p


def _s5_param_call(p, l):
    ins = [p["lam_re"], p["lam_im"], p["log_step"], p["b_re"], p["b_im"]]
    outs = [((S5_N, 1), F32, (S5_N, 1), lambda i: (0, 0), None)] * 2 + \
           [((S5_N, S5_GROUP_CH), F32, (S5_N, S5_GROUP_CH), lambda i: (0, 0), None)] * 2
    return _tile_call(_s5_params, "s5_params", (1,), [_par(a) for a in ins], outs)


def _s5_param_bwd_call(p, cts):
    ins = [p["lam_re"], p["lam_im"], p["log_step"], p["b_re"], p["b_im"]]

    def fn(lr, li, ls, br, bi, g0, g1, g2, g3):
        _, vjp = jax.vjp(_s5_params, lr, li, ls, br, bi)
        return vjp((g0, g1, g2, g3))

    outs = [((S5_N, 1), F32, (S5_N, 1), lambda i: (0, 0), None)] * 3 + \
           [((S5_N, S5_GROUP_CH), F32, (S5_N, S5_GROUP_CH), lambda i: (0, 0), None)] * 2
    return _tile_call(fn, "s5_params_bwd", (1,), [_par(a) for a in ins + list(cts)], outs)


def _layer_fwd(x_prev, f_prev, p, tabs, consts, bsz, riders):
    xout = {}

    def hosted(name, res, n_out):
        xout[name] = res[n_out:]
        return res[:n_out]

    t = x_prev.shape[0]
    seq = t // bsz
    nt = t // TM
    cos, sin = tabs
    e_mat, p_mat, r_mat, pl_mat = consts
    sv = {}

    if f_prev is None:
        x = x_prev
        h = _tile_call(lambda xv, g: _rms(xv, g), "norm_first", (nt,), [_row(x), _par(p["attn_g"])],
                       [_orow(t, D_MODEL, BF16)])[0]
    else:
        x, h = _tile_call(lambda xv, fv, g: (xv + fv, _rms(xv + fv, g)), "norm_attn", (nt,),
                          [_row(x_prev), _row(f_prev), _par(p["attn_g"])], [_orow(t, D_MODEL), _orow(t, D_MODEL, BF16)])
    sv["x"], sv["h"] = x, h
    proj = _mm(h, p["w_in"], "nn", "mm_in")
    sv["proj"] = proj

    def pre_fn(pa, cosv, sinv, qlg, w_uq, kvlg, w_uk, w_uv, mqg, mkg, fqg, fkg, em, pm, rm, plm):
        return _mixer_pre(pa[:, O_CQ:O_CQ + MLA_Q_RANK], pa[:, O_CKV:O_CKV + MLA_KV_RANK], pa[:, O_KR:O_KR + LANES],
                          pa[:, O_FQ:O_FQ + 256], pa[:, O_FK:O_FK + 256], qlg, w_uq, kvlg, w_uk, w_uv,
                          mqg, mkg, fqg, fkg, cosv, sinv, em, pm, rm, plm)

    pre_params = [p["qlg"], p["w_uq"], p["kvlg"], p["w_uk"], p["w_uv"], p["mqg"], p["mkg"], p["fqg"], p["fkg"],
                  e_mat, p_mat, r_mat, pl_mat]
    q_m, k_m, v_m, q_f, k_f = hosted("mixer_pre", _tile_call(
        pre_fn, "mixer_pre", (nt,), [_row(proj, PA), _row(cos), _row(sin)] + [_par(a) for a in pre_params],
        [_orow(t, QKW), _orow(t, QKW), _orow(t, 2 * LANES), _orow(t, QKW), _orow(t, QKW)],
        rider=riders.get("mixer_pre")), 5)
    sv.update(q_m=q_m, k_m=k_m, v_m=v_m, q_f=q_f, k_f=k_f)

    z = proj[:, O_FL:O_FL + N_HEADS].reshape(bsz, seq, N_HEADS).transpose(0, 2, 1).reshape(bsz * N_HEADS, seq)
    fb = jnp.tile(p["fbias"], bsz).reshape(bsz * N_HEADS, 1)
    cum = _forget_fwd(z, fb, "forget_fwd")
    ck = cum.reshape(bsz, 2, 2, seq)
    cq = ck.transpose(0, 1, 3, 2)
    sv.update(z=z, fb=fb, cq=cq, ck=ck)

    r3 = lambda a: a.reshape(bsz, seq, a.shape[-1])
    o_m, lse_m = hosted("attn_mla_fwd", _attn_fwd(r3(q_m), r3(k_m), r3(v_m), 0, None, None, MLA_QK_DIM ** -0.5,
                                                  "attn_mla_fwd", riders.get("attn_mla_fwd")), 2)
    o_f, lse_f = hosted("attn_fox_fwd", _attn_fwd(r3(q_f), r3(k_f), r3(proj), O_FV // LANES, cq, ck, HEAD_V ** -0.5,
                                                  "attn_fox_fwd", riders.get("attn_fox_fwd")), 2)
    o_m, o_f = o_m.reshape(t, 2 * LANES), o_f.reshape(t, 2 * LANES)
    sv.update(o_m=o_m, o_f=o_f, lse_m=lse_m, lse_f=lse_f)

    lbar_re, lbar_im, bbar_re, bbar_im = _s5_param_call(p, 0)
    lbar = _to_scan_cols(lbar_re.reshape(1, S5_N), lbar_im.reshape(1, S5_N))
    b_blk = _to_scan_cols(_block_diag_in(bbar_re.reshape(S5_GROUPS, S5_STATE, S5_GROUP_CH)),
                          _block_diag_in(bbar_im.reshape(S5_GROUPS, S5_STATE, S5_GROUP_CH)))
    sv.update(lbar=lbar, b_blk=b_blk)
    bu = _tile_call(lambda u, bb: bdot(u, bb), "s5_bu", (nt,), [_row(proj, S5_WIDTH, O_U // S5_WIDTH), _par(b_blk)],
                    [_orow(t, 2 * S5_N)])[0]
    state = hosted("s5_scan_fwd", _scan_fwd(bu, lbar, bsz, "s5_scan_fwd", riders.get("s5_scan_fwd")), 1)[0]
    sv["state"] = state
    post_params = [p["c_blk"], p["d_row"], p["w_glu"], p["b_glu"]]
    o_s = _tile_call(_s5_post, "s5_post", (nt,),
                     [_row(state), _row(proj, S5_WIDTH, O_U // S5_WIDTH)] + [_par(a) for a in post_params],
                     [_orow(t, S5_WIDTH)])[0]
    sv["o_s"] = o_s

    def merge_fn(oa, ob, oc, ga, gb, gc, wb):
        return _merge(oa, ob, oc, ga, gb, gc, wb[0], wb[1], wb[2])

    gate_specs = [_row(proj, D_MODEL, PA // D_MODEL + n) for n in range(3)]
    merged = _tile_call(merge_fn, "merge", (nt,), [_row(o_m), _row(o_f), _row(o_s)] + gate_specs + [_par(p["w_branch"])],
                        [_orow(t, D_MODEL, BF16)])[0]
    sv["merged"] = merged
    attn_out = _mm(merged, p["w_out"], "nn", "mm_out")

    x1, h2 = _tile_call(lambda xv, av, g: (xv + av, _rms(xv + av, g)), "norm_ffn", (nt,),
                        [_row(x), _row(attn_out), _par(p["ffn_g"])], [_orow(t, D_MODEL), _orow(t, D_MODEL, BF16)])
    sv["x1"], sv["h2"] = x1, h2
    up = _mm(h2, p["w_up"], "nn", "mm_up")
    sv["up"] = up

    def ffn_fn(upv, cw):
        return _ffn_mid(upv[:, :FF_BLK], upv[:, FF_BLK:], cw[0:1, :FF_BLK], cw[1:2, :FF_BLK], cw[2:3, :FF_BLK],
                        cw[0:1, FF_BLK:], cw[1:2, FF_BLK:], cw[2:3, FF_BLK:])

    nblk = FF_PW // FF_BLK
    act = _tile_call(ffn_fn, "ffn_mid", (nblk, bsz),
                     [(up, (seq, 2 * FF_BLK), lambda j, b: (b, j)), (p["conv"], (3, 2 * FF_BLK), lambda j, b: (0, j))],
                     [((t, FF_PW), BF16, (seq, FF_BLK), lambda j, b: (b, j), None)])[0]
    sv["act"] = act
    ffn_out = _mm(act, p["w_down"], "nn", "mm_down")
    return x1, ffn_out, sv, xout


def _layer_bwd(dx2, p, sv, tabs, consts, bsz, riders):
    xout = {}

    def hosted(name, res, n_out):
        xout[name] = res[n_out:]
        return res[:n_out]

    t = dx2.shape[0]
    seq = t // bsz
    nt = t // TM
    cos, sin = tabs
    e_mat, p_mat, r_mat, pl_mat = consts
    g = {}

    g["w_down"] = _mm(sv["act"], dx2, "tn", "mm_down_dw", out_dtype=BF16)
    dact = _mm(dx2, p["w_down"], "nt", "mm_down_dx")

    def ffn_bwd_fn(upv, cw, da):
        args = (upv[:, :FF_BLK], upv[:, FF_BLK:], cw[0:1, :FF_BLK], cw[1:2, :FF_BLK], cw[2:3, :FF_BLK],
                cw[0:1, FF_BLK:], cw[1:2, FF_BLK:], cw[2:3, FF_BLK:])
        _, vjp = jax.vjp(_ffn_mid, *args)
        dg, dv, g0, g1, g2, v0, v1, v2 = vjp(da)
        return (jnp.concatenate([dg, dv], axis=1), jnp.concatenate([g0, v0], axis=1),
                jnp.concatenate([g1, v1], axis=1), jnp.concatenate([g2, v2], axis=1))

    nblk = FF_PW // FF_BLK
    cw_out = ((1, UP_W), F32, (1, 2 * FF_BLK), lambda j, b: (0, j), (1,))
    dup, dc0, dc1, dc2 = hosted("ffn_mid_bwd", _tile_call(
        ffn_bwd_fn, "ffn_mid_bwd", (nblk, bsz),
        [(sv["up"], (seq, 2 * FF_BLK), lambda j, b: (b, j)), (p["conv"], (3, 2 * FF_BLK), lambda j, b: (0, j)),
         (dact, (seq, FF_BLK), lambda j, b: (b, j))],
        [((t, UP_W), BF16, (seq, 2 * FF_BLK), lambda j, b: (b, j), None), cw_out, cw_out, cw_out],
        rider=riders.get("ffn_mid_bwd")), 4)
    g["conv"] = jnp.concatenate([dc0, dc1, dc2], axis=0)
    g["w_up"] = _mm(sv["h2"], dup, "tn", "mm_up_dw", out_dtype=BF16)
    dh2 = _mm(dup, p["w_up"], "nt", "mm_up_dx")

    def norm_bwd_fn(xv, gv, dh, dres):
        _, vjp = jax.vjp(_rms, xv, gv)
        dxv, dg = vjp(dh)
        return dres + dxv, dg

    d1, g["ffn_g"] = _tile_call(norm_bwd_fn, "norm_bwd", (nt,),
                                [_row(sv["x1"]), _par(p["ffn_g"]), _row(dh2), _row(dx2)],
                                [_orow(t, D_MODEL), _oacc((1, D_MODEL))])

    g["w_out"] = _mm(sv["merged"], d1, "tn", "mm_out_dw", out_dtype=BF16)
    dmerged = _mm(d1, p["w_out"], "nt", "mm_out_dx")

    def merge_bwd_fn(oa, ob, oc, ga, gb, gc, wb, dm):
        wf = wb.astype(F32)
        _, vjp = jax.vjp(_merge, oa, ob, oc, ga, gb, gc, wf[0], wf[1], wf[2])
        doa, dob, doc, dga, dgb, dgc, dwa, dwb, dwc = vjp(dm)
        return doa, dob, doc, jnp.concatenate([dga, dgb, dgc], axis=1), jnp.stack([dwa, dwb, dwc])

    proj = sv["proj"]
    gate_specs = [_row(proj, D_MODEL, PA // D_MODEL + n) for n in range(3)]
    do_m, do_f, do_s, dgl, g["w_branch"] = _tile_call(
        merge_bwd_fn, "merge_bwd", (nt,),
        [_row(sv["o_m"]), _row(sv["o_f"]), _row(sv["o_s"])] + gate_specs + [_par(p["w_branch"]), _row(dmerged)],
        [_orow(t, 2 * LANES), _orow(t, 2 * LANES), _orow(t, S5_WIDTH), _orow(t, 3 * D_MODEL),
         _oacc((3, S5_WIDTH, D_MODEL))])

    def post_bwd_fn(st, u, cb, dr, wg, bg, do):
        _, vjp = jax.vjp(_s5_post, st, u, cb.astype(F32), dr, wg.astype(F32), bg)
        return vjp(do)

    u_spec = _row(proj, S5_WIDTH, O_U // S5_WIDTH)
    dstate, du1, dc_blk, g_d, g["w_glu"], g_bglu = _tile_call(
        post_bwd_fn, "s5_post_bwd", (nt,),
        [_row(sv["state"]), u_spec, _par(p["c_blk"]), _par(p["d_row"]), _par(p["w_glu"]), _par(p["b_glu"]), _row(do_s)],
        [_orow(t, 2 * S5_N), _orow(t, S5_WIDTH), _oacc((2 * S5_N, S5_WIDTH)), _oacc((1, S5_WIDTH)),
         _oacc((S5_WIDTH, S5_WIDTH)), _oacc((1, S5_WIDTH))])
    dbu, dlbar = hosted("s5_scan_bwd", _scan_bwd(dstate, sv["state"], sv["lbar"], bsz, "s5_scan_bwd",
                                                 riders.get("s5_scan_bwd")), 2)

    def bu_bwd_fn(u, bb, dbv, du_in):
        _, vjp = jax.vjp(bdot, u, bb)
        du, dbb = vjp(dbv)
        return du_in + du, dbb

    du, db_blk = _tile_call(bu_bwd_fn, "s5_bu_bwd", (nt,), [u_spec, _par(sv["b_blk"]), _row(dbu), _row(du1)],
                            [_orow(t, S5_WIDTH), _oacc((S5_WIDTH, 2 * S5_N))])
    dlr, dli = _from_scan_cols(dlbar)
    dbr, dbi = _from_scan_cols(db_blk)
    cts = (dlr.reshape(S5_N, 1), dli.reshape(S5_N, 1),
           _block_diag_in_t(dbr).reshape(S5_N, S5_GROUP_CH), _block_diag_in_t(dbi).reshape(S5_N, S5_GROUP_CH))
    g_lr, g_li, g_ls, g_br, g_bi = _s5_param_bwd_call(p, cts)
    dc_re, dc_im = _from_scan_cols(dc_blk.T)
    g["s5_lambda_re"] = g_lr.reshape(S5_GROUPS, S5_STATE)
    g["s5_lambda_im"] = g_li.reshape(S5_GROUPS, S5_STATE)
    g["s5_log_step"] = jnp.sum(g_ls.reshape(S5_GROUPS, S5_STATE), axis=1)
    g["s5_b_re"] = g_br.reshape(S5_GROUPS, S5_STATE, S5_GROUP_CH)
    g["s5_b_im"] = g_bi.reshape(S5_GROUPS, S5_STATE, S5_GROUP_CH)
    g["s5_c_re"] = _block_diag_out_t(dc_re.T)
    g["s5_c_im"] = -_block_diag_out_t(dc_im.T)
    g["s5_d"] = g_d.reshape(S5_GROUPS, S5_GROUP_CH)
    g["s5_b_glu"] = g_bglu.reshape(S5_WIDTH)

    r3 = lambda a: a.reshape(bsz, seq, a.shape[-1])
    dq_m, dk_m, dv_m = hosted("attn_mla_bwd", _attn_bwd(
        r3(sv["q_m"]), r3(sv["k_m"]), r3(sv["v_m"]), 0, r3(sv["o_m"]), r3(do_m), sv["lse_m"], None, None,
        MLA_QK_DIM ** -0.5, "attn_mla_bwd", riders.get("attn_mla_bwd")), 3)
    dq_f, dk_f, dv_f, dcq, dck = hosted("attn_fox_bwd", _attn_bwd(
        r3(sv["q_f"]), r3(sv["k_f"]), r3(proj), O_FV // LANES, r3(sv["o_f"]), r3(do_f), sv["lse_f"], sv["cq"], sv["ck"],
        HEAD_V ** -0.5, "attn_fox_bwd", riders.get("attn_fox_bwd")), 5)
    dcum = (dck + dcq.transpose(0, 1, 3, 2)).reshape(bsz * N_HEADS, seq)
    dz, dfb = _forget_bwd(dcum, sv["z"], sv["fb"], "forget_bwd")
    g["fox_f_bias"] = jnp.sum(dfb.reshape(bsz, N_HEADS), axis=0)
    dfl = jnp.pad(dz.reshape(bsz, N_HEADS, seq).transpose(0, 2, 1).reshape(t, N_HEADS), ((0, 0), (0, LANES - N_HEADS)))

    def pre_bwd_fn(pa, cosv, sinv, qlg, w_uq, kvlg, w_uk, w_uv, mqg, mkg, fqg, fkg, em, pm, rm, plm,
                   gq, gk, gv, gqf, gkf, gvf, gu, gfl, ggl):
        f = functools.partial(_mixer_pre, cos=cosv, sin=sinv, e_mat=em, p_mat=pm, r_mat=rm, pl_mat=plm)
        prim = (pa[:, O_CQ:O_CQ + MLA_Q_RANK], pa[:, O_CKV:O_CKV + MLA_KV_RANK], pa[:, O_KR:O_KR + LANES],
                pa[:, O_FQ:O_FQ + 256], pa[:, O_FK:O_FK + 256], qlg, w_uq.astype(F32), kvlg, w_uk.astype(F32),
                w_uv.astype(F32), mqg, mkg, fqg, fkg)
        _, vjp = jax.vjp(lambda *a: f(*a), *prim)
        dcq_, dckv, dkr, dfq, dfk, dqlg, dwuq, dkvlg, dwuk, dwuv, dmqg, dmkg, dfqg, dfkg = vjp((gq, gk, gv, gqf, gkf))
        zpad = jnp.zeros((pa.shape[0], PA - O_CQ - MLA_Q_RANK), F32)
        dproj = jnp.concatenate([dckv, dfq, dfk, gvf, gu, dkr, gfl, dcq_, zpad, ggl], axis=1)
        return dproj, dqlg, dwuq, dkvlg, dwuk, dwuv, dmqg, dmkg, dfqg, dfkg

    pre_params = [p["qlg"], p["w_uq"], p["kvlg"], p["w_uk"], p["w_uv"], p["mqg"], p["mkg"], p["fqg"], p["fkg"],
                  e_mat, p_mat, r_mat, pl_mat]
    cts_in = [dq_m.reshape(t, QKW), dk_m.reshape(t, QKW), dv_m.reshape(t, 2 * LANES), dq_f.reshape(t, QKW),
              dk_f.reshape(t, QKW), dv_f.reshape(t, 2 * LANES), du, dfl, dgl]
    (dproj, g_qlg, g_wuq, g_kvlg, g_wuk, g_wuv, g_mqg, g_mkg, g_fqg, g_fkg) = _tile_call(
        pre_bwd_fn, "mixer_pre_bwd", (nt,),
        [_row(proj, PA), _row(cos), _row(sin)] + [_par(a) for a in pre_params] + [_row(a) for a in cts_in],
        [_orow(t, PW, BF16), _oacc((1, MLA_Q_RANK)), _oacc((MLA_Q_RANK, QKW)), _oacc((1, MLA_KV_RANK)),
         _oacc((MLA_KV_RANK, QKW)), _oacc((MLA_KV_RANK, N_HEADS * HEAD_V)), _oacc((1, QKW)), _oacc((1, QKW)),
         _oacc((1, QKW)), _oacc((1, QKW))])
    g["q_lat_norm_g"] = g_qlg.reshape(MLA_Q_RANK)
    g["kv_lat_norm_g"] = g_kvlg.reshape(MLA_KV_RANK)
    heads = lambda a, w: jnp.sum(a.reshape(N_HEADS, HP)[:, :w], axis=0)
    g["mla_q_norm_g"], g["mla_k_norm_g"] = heads(g_mqg, MLA_QK_DIM), heads(g_mkg, MLA_QK_DIM)
    g["fox_q_norm_g"], g["fox_k_norm_g"] = heads(g_fqg, HEAD_V), heads(g_fkg, HEAD_V)
    g["w_uq"] = g_wuq.reshape(MLA_Q_RANK, N_HEADS, HP)[..., :MLA_QK_DIM].reshape(MLA_Q_RANK, N_HEADS * MLA_QK_DIM)
    g["w_ukv"] = jnp.concatenate([g_wuk.reshape(MLA_KV_RANK, N_HEADS, HP)[..., :MLA_NOPE_DIM],
                                  g_wuv.reshape(MLA_KV_RANK, N_HEADS, HEAD_V)], axis=-1).reshape(MLA_KV_RANK, QKW)

    g["w_in"] = _mm(sv["h"], dproj, "tn", "mm_in_dw", out_dtype=BF16)
    dh = _mm(dproj, p["w_in"], "nt", "mm_in_dx")
    dx, g["attn_g"] = _tile_call(norm_bwd_fn, "norm_bwd", (nt,),
                                 [_row(sv["x"]), _par(p["attn_g"]), _row(dh), _row(d1)],
                                 [_orow(t, D_MODEL), _oacc((1, D_MODEL))])
    return dx, g, xout


def _row_tile(r, c, max_elems):
    if r * c <= max_elems:
        return r
    best = None
    for d in range(8, r, 8):
        if r % d == 0 and d * c <= max_elems:
            best = d
    assert best is not None, (r, c)
    return best


def _sum8(land, name):
    _, r, c = land.shape
    tr = _row_tile(r, N_DEV * c, 1 << 20)

    def fn(lv):
        acc = lv[0].astype(F32)
        for i in range(1, N_DEV):
            acc = acc + lv[i].astype(F32)
        return acc

    return _tile_call(fn, name, (r // tr,), [(land, (N_DEV, tr, c), lambda i: (0, i, 0))],
                      [((r, c), F32, (tr, c), lambda i: (i, 0), None)])[0]


def _adamw_layer(w, g, m, v, l, prev, name):
    _, r, c = w.shape
    tr = _row_tile(r, c, 1 << 19)
    lay = pl.BlockSpec((None, tr, c), lambda i: (l, i, 0))

    def body(w_ref, g_ref, m_ref, v_ref, *rest):
        g_out, d_out, m_out, v_out = rest[-4:]
        gv = g_ref[...]
        d_out[...], m_out[...], v_out[...] = _adamw(w_ref[...], gv, m_ref[...], v_ref[...])
        g_out[...] = gv

    in_specs = [lay, pl.BlockSpec((tr, c), lambda i: (i, 0)), lay, lay]
    args = [w, g, m, v]
    aliases = {}
    if prev is not None:
        in_specs += [pl.BlockSpec(memory_space=pl.ANY)] * 4
        args += list(prev)
        aliases = {4 + k: k for k in range(4)}
    return _pcall(body, name=name, grid=(r // tr,), in_specs=in_specs, out_specs=[lay] * 4,
                  out_shape=[jax.ShapeDtypeStruct(w.shape, F32)] * 4, aliases=aliases)(*args)


SMALL_NAMES = ("attn_norm_g", "q_lat_norm_g", "kv_lat_norm_g", "mla_q_norm_g", "mla_k_norm_g", "fox_q_norm_g",
               "fox_k_norm_g", "fox_f_bias", "s5_lambda_re", "s5_lambda_im", "s5_b_re", "s5_b_im", "s5_c_re", "s5_c_im",
               "s5_d", "s5_log_step", "s5_b_glu", "ffn_norm_g")
BIG_NAMES = ("w_in", "w_uq", "w_ukv", "s5_w_glu", "w_branch", "w_out", "w_up", "ffn_conv_w", "w_down")
ALL_NAMES = ("attn_norm_g", "w_in", "q_lat_norm_g", "w_uq", "kv_lat_norm_g", "w_ukv", "mla_q_norm_g", "mla_k_norm_g",
             "fox_q_norm_g", "fox_k_norm_g", "fox_f_bias", "s5_lambda_re", "s5_lambda_im", "s5_b_re", "s5_b_im",
             "s5_c_re", "s5_c_im", "s5_d", "s5_log_step", "s5_w_glu", "s5_b_glu", "w_branch", "w_out", "ffn_norm_g",
             "w_up", "ffn_conv_w", "w_down")


def _pack(arrs):
    tile = 8 * LANES
    parts = []
    for a in arrs:
        n = int(np.prod(a.shape))
        tot = -(-n // tile) * tile
        parts.append(jnp.pad(a.reshape(-1), (0, tot - n)).reshape(tot // LANES, LANES))
    return jnp.concatenate(parts, axis=0)


def _unpack(packed, like):
    tile = 8 * LANES
    out, pos = [], 0
    for a in like:
        n = int(np.prod(a.shape))
        rows = -(-n // tile) * 8
        out.append(packed[pos:pos + rows].reshape(-1)[:n].reshape(a.shape))
        pos += rows
    return out


def _rope_tables(positions, t):
    pos = positions.reshape(t, 1)
    inv = _inv_freq_row()

    def fn(pv, iv):
        ang = pv.astype(F32) * iv
        return jnp.cos(ang), jnp.sin(ang)

    return _tile_call(fn, "rope_tables", (t // TM,), [_row(pos), _par(inv)], [_orow(t, QKW), _orow(t, QKW)])


def _loss_call(x1, f, target):
    t = x1.shape[0]

    def fn(xv, fv, tv):
        e = xv + fv - tv
        per_tok = jnp.sum(e * e, axis=1, keepdims=True) * (1.0 / D_MODEL)
        return 0.5 * jnp.sum(per_tok, axis=0, keepdims=True), e * (1.0 / D_MODEL)

    return _tile_call(fn, "loss", (t // TM,), [_row(x1), _row(f), _row(target)], [_oacc((1, 1)), _orow(t, D_MODEL)])


def _local_shards(l, w):
    return {
        "w_in": _pad_in_cols(w["w_in"][l]).astype(BF16),
        "w_uq": w["w_uq"][l].reshape(-1, N_HEADS * MLA_QK_DIM).astype(BF16),
        "w_ukv": w["w_ukv"][l].reshape(-1, QKW).astype(BF16),
        "w_glu": w["s5_w_glu"][l].astype(BF16),
        "w_branch": w["w_branch"][l].astype(BF16),
        "w_out": w["w_out"][l].astype(BF16),
        "w_up": _pad_ff(w["w_up"][l], 1).astype(BF16),
        "conv": _pad_ff(w["ffn_conv_w"][l], 1),
        "w_down": _pad_ff(w["w_down"][l], 0).astype(BF16),
    }


GATHERED_SHAPES = {"w_in": (D_MODEL, PW), "w_uq": (MLA_Q_RANK, N_HEADS * MLA_QK_DIM), "w_ukv": (MLA_KV_RANK, QKW),
                   "w_glu": (S5_WIDTH, S5_WIDTH), "w_branch": (3, S5_WIDTH, D_MODEL), "w_out": (D_MODEL, D_MODEL),
                   "w_up": (D_MODEL, UP_W), "conv": (3, UP_W), "w_down": (FF_PW, D_MODEL)}
FWD_RIDERS = {"mixer_pre": ("w_out", "w_branch"), "attn_mla_fwd": ("w_in", "w_uq", "w_ukv", "w_glu"),
              "attn_fox_fwd": ("w_up", "conv"), "s5_scan_fwd": ("w_down",)}
BWD_RIDERS = {"ffn_mid_bwd": ("w_down",), "s5_scan_bwd": ("w_out", "w_branch", "w_uq", "w_ukv", "w_glu", "conv"),
              "attn_mla_bwd": ("w_in",), "attn_fox_bwd": ("w_up",)}
WEIGHT_OF = {"w_in": "w_in", "w_uq": "w_uq", "w_ukv": "w_ukv", "w_glu": "s5_w_glu", "w_branch": "w_branch",
             "w_out": "w_out", "w_up": "w_up", "conv": "ffn_conv_w", "w_down": "w_down"}


def _unshard_layout(nm, g2):
    if nm == "w_in":
        return _unpad_in_cols(g2)
    if nm == "w_uq":
        return g2.reshape(-1, N_HEADS, MLA_QK_DIM)
    if nm == "w_ukv":
        return g2.reshape(-1, N_HEADS, MLA_NOPE_DIM + HEAD_V)
    if nm in ("w_up", "conv"):
        return _unpad_ff(g2, 1)
    if nm == "w_down":
        return _unpad_ff(g2, 0)
    return g2


def kernel(x, positions, attn_norm_g, w_in, q_lat_norm_g, w_uq, kv_lat_norm_g, w_ukv, mla_q_norm_g, mla_k_norm_g, fox_q_norm_g, fox_k_norm_g, fox_f_bias, s5_lambda_re, s5_lambda_im, s5_b_re, s5_b_im, s5_c_re, s5_c_im, s5_d, s5_log_step, s5_w_glu, s5_b_glu, w_branch, w_out, ffn_norm_g, w_up, ffn_conv_w, w_down, loss_target, m_attn_norm_g, m_w_in, m_q_lat_norm_g, m_w_uq, m_kv_lat_norm_g, m_w_ukv, m_mla_q_norm_g, m_mla_k_norm_g, m_fox_q_norm_g, m_fox_k_norm_g, m_fox_f_bias, m_s5_lambda_re, m_s5_lambda_im, m_s5_b_re, m_s5_b_im, m_s5_c_re, m_s5_c_im, m_s5_d, m_s5_log_step, m_s5_w_glu, m_s5_b_glu, m_w_branch, m_w_out, m_ffn_norm_g, m_w_up, m_ffn_conv_w, m_w_down, v_attn_norm_g, v_w_in, v_q_lat_norm_g, v_w_uq, v_kv_lat_norm_g, v_w_ukv, v_mla_q_norm_g, v_mla_k_norm_g, v_fox_q_norm_g, v_fox_k_norm_g, v_fox_f_bias, v_s5_lambda_re, v_s5_lambda_im, v_s5_b_re, v_s5_b_im, v_s5_c_re, v_s5_c_im, v_s5_d, v_s5_log_step, v_s5_w_glu, v_s5_b_glu, v_w_branch, v_w_out, v_ffn_norm_g, v_w_up, v_ffn_conv_w, v_w_down):
    args = locals()
    w = {n: args[n] for n in ALL_NAMES}
    m = {n: args["m_" + n] for n in ALL_NAMES}
    v = {n: args["v_" + n] for n in ALL_NAMES}
    depth = w_in.shape[0]
    bsz, seq, _ = x.shape
    t = bsz * seq
    consts = _constants()
    tabs = _rope_tables(positions, t)
    small = {n: w[n] for n in SMALL_NAMES}

    all_names = list(GATHERED_SHAPES)
    all_shards = [_local_shards(l, w) for l in range(depth)]

    def gather_xfer(l, names):
        sub = {n: all_shards[l][n] for n in names}
        order, plan = _gather_plan(sub)
        return _Xfer([sub[n] for n in order], [(GATHERED_SHAPES[n], sub[n].dtype) for n in order], plan)

    def scatter_xfer(g, names):
        shapes = {n: _local_shards_shape(n) for n in names}
        return _Xfer([g[n] for n in names], [((N_DEV,) + shapes[n], g[n].dtype) for n in names],
                     _scatter_plan(list(names), shapes))

    def riding(groups, xout):
        return {n: a for host, names in groups.items() for n, a in zip(names, xout[host])}

    xs = x.reshape(t, D_MODEL)
    f_prev = None
    saved, params = [], []
    gathered = dict(zip(all_names, _exchange("gather_weights", gather_xfer(0, all_names))))
    for l in range(depth):
        p = _prep_layer(l, gathered, small, consts)
        riders = {} if l + 1 == depth else {host: gather_xfer(l + 1, names) for host, names in FWD_RIDERS.items()}
        xs, f_prev, sv, xout = _layer_fwd(xs, f_prev, p, tabs, consts, bsz, riders)
        if riders:
            gathered = riding(FWD_RIDERS, xout)
        saved.append(sv)
        params.append(p)

    loss_part, dy = _loss_call(xs, f_prev, loss_target.reshape(t, D_MODEL))
    loss = lax.psum(loss_part[0, 0], ("x", "y", "c"))

    small_grads = {n: [None] * depth for n in SMALL_NAMES}
    big = {n: None for n in BIG_NAMES}

    def finish(l, lands):
        for n in all_names:
            wn, ld = WEIGHT_OF[n], lands[n]
            ld2 = ld.reshape(N_DEV, -1, ld.shape[-1])
            gsum = _unshard_layout(n, _sum8(ld2, "sum_" + n).reshape(_local_shards_shape(n)))
            c = gsum.shape[-1]
            three = lambda a: a.reshape(depth, -1, c)
            big[wn] = _adamw_layer(three(w[wn]), gsum.reshape(-1, c), three(m[wn]), three(v[wn]), l, big[wn],
                                   "adamw_" + n)

    dx = dy
    pending = None
    for l in reversed(range(depth)):
        riders = {} if pending is None else {host: scatter_xfer(pending, names) for host, names in BWD_RIDERS.items()}
        dx, g, xout = _layer_bwd(dx, params[l], saved[l], tabs, consts, bsz, riders)
        if riders:
            finish(l + 1, riding(BWD_RIDERS, xout))
        small_grads["attn_norm_g"][l] = g["attn_g"].reshape(D_MODEL)
        small_grads["ffn_norm_g"][l] = g["ffn_g"].reshape(D_MODEL)
        for n in SMALL_NAMES:
            if n not in ("attn_norm_g", "ffn_norm_g"):
                small_grads[n][l] = g[n]
        pending = g
    finish(0, dict(zip(all_names, _exchange("scatter_grads", scatter_xfer(pending, all_names)))))

    sg = [jnp.stack(small_grads[n]).reshape(w[n].shape) for n in SMALL_NAMES]
    packed = _pack(sg)
    land = _exchange("gather_small_grads", _Xfer([packed], [((N_DEV,) + packed.shape, F32)],
                                                 _scatter_plan(["small"], {})))[0]
    gs = _sum8(land, "sum_small")
    one = lambda names_of: _pack([names_of[n] for n in SMALL_NAMES])[None]
    small_res = _adamw_layer(one(w), gs, one(m), one(v), 0, None, "adamw_small")
    like = [w[n] for n in SMALL_NAMES]
    small_out = [dict(zip(SMALL_NAMES, _unpack(a[0], like))) for a in small_res]

    def out_of(kind, n):
        if n in SMALL_NAMES:
            return small_out[kind][n]
        return big[n][kind].reshape(w[n].shape)

    outs = [loss, dx.reshape(bsz, seq, D_MODEL)]
    for kind in range(4):
        outs += [out_of(kind, n) for n in ALL_NAMES]
    return tuple(outs)


def _local_shards_shape(nm):
    return {"w_in": (D_MODEL // N_DEV, PW), "w_uq": (MLA_Q_RANK // N_DEV, N_HEADS * MLA_QK_DIM),
            "w_ukv": (MLA_KV_RANK // N_DEV, QKW), "w_glu": (S5_WIDTH // N_DEV, S5_WIDTH),
            "w_branch": (3, S5_WIDTH, LANES), "w_out": (D_MODEL // N_DEV, D_MODEL), "w_up": (D_MODEL, 2 * FF_BLK),
            "conv": (3, 2 * FF_BLK), "w_down": (FF_BLK, D_MODEL)}[nm]
```

```python
import functools
import math

import numpy as np
import jax
import jax.numpy as jnp
from jax import lax
from jax.experimental import pallas as pl
from jax.experimental.pallas import tpu as pltpu

F32, BF16 = jnp.float32, jnp.bfloat16

D_MODEL = 1024
N_DEV = 8
MLA_Q_RANK, MLA_KV_RANK, MLA_ROPE_DIM, MLA_NOPE_DIM, MLA_QK_DIM = 384, 256, 32, 64, 96
N_HEADS, HEAD_V = 4, 64
S5_GROUPS, S5_GROUP_CH, S5_STATE, S5_WIDTH = 16, 16, 64, 256
S5_N = S5_GROUPS * S5_STATE
D_FF = 2816
D_IN = 4772
ROPE_THETA = 10000.0
NORM_EPS = 1e-6
NEG_INF = -1e30
ADAM_LR, ADAM_B1, ADAM_B2, ADAM_EPS, ADAM_WD, ADAM_STEP = 0.001, 0.9, 0.999, 1e-08, 0.01, 10

VMEM_LIMIT_BYTES = 56 * 1024 * 1024
LANES = 128
TM = 256
TQ = 256
QC = 128
HP = 128
QKW = N_HEADS * HP

PW = 5120
PA = 2048
IN_SEGS = ((0, 384, 256),
           (256, 672, 256),
           (512, 928, 256),
           (768, 1184, 256),
           (1024, 1444, 256),
           (1280, 640, 32),
           (1408, 1440, 4),
           (1536, 0, 384),
           (2048, 1700, 3072))
O_CKV, O_FQ, O_FK, O_FV, O_U, O_KR, O_FL, O_CQ = 0, 256, 512, 768, 1024, 1280, 1408, 1536

FF_BLK = 384
FF_HALF = D_FF // 8
FF_PW = 8 * FF_BLK
UP_W = 2 * FF_PW

NN = (((1,), (0,)), ((), ()))
NT = (((1,), (1,)), ((), ()))
TN = (((0,), (0,)), ((), ()))


def _pcall(body, *, name, grid, in_specs, out_specs, out_shape, scratch_shapes=(), rider=None, aliases=None):
    params = pltpu.CompilerParams(dimension_semantics=("arbitrary",) * len(grid), vmem_limit_bytes=VMEM_LIMIT_BYTES)
    if rider is None:
        return pl.pallas_call(body, name=name, grid=grid, in_specs=in_specs, out_specs=out_specs, out_shape=out_shape,
                              scratch_shapes=list(scratch_shapes), input_output_aliases=aliases or {},
                              compiler_params=params)
    assert not aliases
    in_specs, out_specs, out_shape = list(in_specs), list(out_specs), list(out_shape)
    n_in, n_out, n_scr, nx = len(in_specs), len(out_specs), len(scratch_shapes), rider.n
    any_spec = pl.BlockSpec(memory_space=pl.ANY)

    def at_step(last):
        conds = [pl.program_id(a) == (g - 1 if last else 0) for a, g in enumerate(grid)]
        return functools.reduce(jnp.logical_and, conds)

    def body_with_rider(*refs):
        ins, xsrc = refs[:n_in], refs[n_in:n_in + nx]
        outs, xout = refs[n_in + nx:n_in + nx + n_out], refs[n_in + nx + n_out:n_in + 2 * nx + n_out]
        scr = refs[n_in + 2 * nx + n_out:n_in + 2 * nx + n_out + n_scr]
        sems = refs[n_in + 2 * nx + n_out + n_scr:]

        @pl.when(at_step(False))
        def _():
            rider.start(rider.copies(xsrc, xout, *sems))

        body(*ins, *outs, *scr)

        @pl.when(at_step(True))
        def _():
            rider.wait(rider.copies(xsrc, xout, *sems))

    call = pl.pallas_call(body_with_rider, name=name, grid=grid, in_specs=in_specs + [any_spec] * nx,
                          out_specs=out_specs + [any_spec] * nx,
                          out_shape=out_shape + [jax.ShapeDtypeStruct(s, d) for s, d in rider.out_shapes],
                          scratch_shapes=list(scratch_shapes) + rider.sems(), compiler_params=params)
    return lambda *args: call(*args, *rider.srcs)


def _tile_call(fn, name, grid, ins, outs, rider=None):
    n_in = len(ins)

    def body(*refs):
        res = fn(*[r[...] for r in refs[:n_in]])
        if not isinstance(res, (tuple, list)):
            res = (res,)
        assert len(res) == len(outs), (name, len(res), len(outs))
        for r, v, o in zip(refs[n_in:], res, outs):
            v = v.astype(r.dtype)
            if o[4] is None:
                r[...] = v
            else:
                first = functools.reduce(jnp.logical_and, [pl.program_id(a) == 0 for a in o[4]])

                @pl.when(first)
                def _():
                    r[...] = v

                @pl.when(jnp.logical_not(first))
                def _():
                    r[...] += v

    res = _pcall(body, name=name, grid=grid,
                 in_specs=[pl.BlockSpec(b, m) for _, b, m in ins],
                 out_specs=[pl.BlockSpec(o[2], o[3]) for o in outs],
                 out_shape=[jax.ShapeDtypeStruct(o[0], o[1]) for o in outs], rider=rider)(*[a for a, _, _ in ins])
    return res


def _row(a, width=None, col_block=0, tm=TM):
    w = a.shape[1] if width is None else width
    return (a, (tm, w), lambda i, c=col_block: (i, c))


def _par(a):
    nd = a.ndim
    return (a, a.shape, lambda i, nd=nd: (0,) * nd)


def _orow(t, w, dtype=F32, tm=TM):
    return ((t, w), dtype, (tm, w), lambda i: (i, 0), None)


def _oacc(shape):
    nd = len(shape)
    return (tuple(shape), F32, tuple(shape), lambda i, nd=nd: (0,) * nd, (0,))


def _pick(n, target):
    best = None
    for d in range(LANES, min(n, target) + 1, LANES):
        if n % d == 0:
            best = d
    return n if best is None else best


def _mm(a, b, mode, name, out_dtype=F32, tm=1024, tn=2048, tk=1024, rider=None):
    if mode == "nn":
        (m, k), (_, n) = a.shape, b.shape
    elif mode == "nt":
        (m, k), (n, _) = a.shape, b.shape
    else:
        (k, m), (_, n) = a.shape, b.shape
    tm, tn, tk = _pick(m, tm), _pick(n, tn), _pick(k, tk)
    nk = k // tk
    dims = {"nn": NN, "nt": NT, "tn": TN}[mode]
    a_spec = (pl.BlockSpec((tk, tm), lambda i, j, l: (l, i)) if mode == "tn"
              else pl.BlockSpec((tm, tk), lambda i, j, l: (i, l)))
    b_spec = (pl.BlockSpec((tn, tk), lambda i, j, l: (j, l)) if mode == "nt"
              else pl.BlockSpec((tk, tn), lambda i, j, l: (l, j)))

    def body(a_ref, b_ref, o_ref, acc_ref):
        part = lax.dot_general(a_ref[...].astype(BF16), b_ref[...].astype(BF16), dims, preferred_element_type=F32)
        if nk == 1:
            o_ref[...] = part.astype(o_ref.dtype)
        else:
            l = pl.program_id(2)

            @pl.when(l == 0)
            def _():
                acc_ref[...] = part

            @pl.when(l > 0)
            def _():
                acc_ref[...] += part

            @pl.when(l == nk - 1)
            def _():
                o_ref[...] = acc_ref[...].astype(o_ref.dtype)

    res = _pcall(body, name=name, grid=(m // tm, n // tn, nk), in_specs=[a_spec, b_spec],
                 out_specs=[pl.BlockSpec((tm, tn), lambda i, j, l: (i, j))],
                 out_shape=[jax.ShapeDtypeStruct((m, n), out_dtype)],
                 scratch_shapes=[pltpu.VMEM((tm, tn) if nk > 1 else (8, LANES), F32)], rider=rider)(a, b)
    return res[0] if rider is None else res


def _dot(x, w, dims):
    return lax.dot_general(x.astype(BF16), w.astype(BF16), dims, preferred_element_type=F32)


@jax.custom_vjp
def bdot(x, w):
    return _dot(x, w, NN)


def _bdot_fwd(x, w):
    return _dot(x, w, NN), (x, w)


def _bdot_bwd(res, g):
    x, w = res
    return _dot(g, w, NT), _dot(x, g, TN)


bdot.defvjp(_bdot_fwd, _bdot_bwd)


def _split3(x):
    x1 = x.astype(BF16)
    r1 = x - x1.astype(F32)
    x2 = r1.astype(BF16)
    x3 = (r1 - x2.astype(F32)).astype(BF16)
    return x1, x2, x3


def _cdot(x, m, dims):
    return sum(lax.dot_general(p, m, dims, preferred_element_type=F32) for p in _split3(x))


def _block_dot(x, m, transposed):
    k_in = m.shape[1] if transposed else m.shape[0]
    dims = NT if transposed else NN
    return jnp.concatenate([_cdot(x[:, b * k_in:(b + 1) * k_in], m, dims) for b in range(x.shape[1] // k_in)], axis=1)


@jax.custom_vjp
def cdot(x, m):
    return _block_dot(x, m, False)


def _cdot_fwd(x, m):
    return _block_dot(x, m, False), m


def _cdot_bwd(m, g):
    return _block_dot(g, m, True), jnp.zeros_like(m)


cdot.defvjp(_cdot_fwd, _cdot_bwd)


@jax.custom_vjp
def tile_heads(y):
    return jnp.concatenate([y] * N_HEADS, axis=1)


def _tile_heads_fwd(y):
    return jnp.concatenate([y] * N_HEADS, axis=1), None


def _tile_heads_bwd(_, g):
    return (sum(g[:, h * HP:(h + 1) * HP] for h in range(N_HEADS)),)


tile_heads.defvjp(_tile_heads_fwd, _tile_heads_bwd)


def _shift_rows_impl(x, s, reverse):
    n = x.shape[0]
    idx = lax.broadcasted_iota(jnp.int32, x.shape, 0)
    if reverse:
        return jnp.where(idx < n - s, pltpu.roll(x, n - s, 0), 0.0)
    return jnp.where(idx >= s, pltpu.roll(x, s, 0), 0.0)


@functools.partial(jax.custom_vjp, nondiff_argnums=(1,))
def shift_rows(x, s):
    return _shift_rows_impl(x, s, False)


def _shift_rows_fwd(x, s):
    return _shift_rows_impl(x, s, False), None


def _shift_rows_bwd(s, _, g):
    return (_shift_rows_impl(g, s, True),)


shift_rows.defvjp(_shift_rows_fwd, _shift_rows_bwd)


def _rms(x, g):
    return x * lax.rsqrt(jnp.mean(x * x, axis=-1, keepdims=True) + NORM_EPS) * g


def _head_rms(x, e_mat, inv_n, g):
    ms = cdot(x * x, e_mat) * inv_n
    return x * lax.rsqrt(ms + NORM_EPS) * g


def _rope(y, p_mat, cos, sin):
    return y * cos + cdot(y, p_mat) * sin


def _constants():
    e = np.ones((HP, HP), np.float32)
    p = np.zeros((HP, HP), np.float32)
    r = np.zeros((LANES, HP), np.float32)
    pl64 = np.zeros((2 * HEAD_V, 2 * HP), np.float32)
    half = MLA_ROPE_DIM // 2
    for i in range(half):
        x1, x2 = MLA_NOPE_DIM + i, MLA_NOPE_DIM + half + i
        p[x2, x1] = -1.0
        p[x1, x2] = 1.0
    for i in range(MLA_ROPE_DIM):
        r[i, MLA_NOPE_DIM + i] = 1.0
    for h in range(2):
        for i in range(HEAD_V):
            pl64[h * HEAD_V + i, h * HP + i] = 1.0
    return tuple(jnp.asarray(a, BF16) for a in (e, p, r, pl64))


def _inv_freq_row():
    inv = ROPE_THETA ** (-jnp.arange(0, MLA_ROPE_DIM, 2, dtype=F32) / MLA_ROPE_DIM)
    head = jnp.concatenate([jnp.zeros((MLA_NOPE_DIM,), F32), inv, inv, jnp.zeros((HP - MLA_QK_DIM,), F32)])
    return jnp.tile(head, N_HEADS).reshape(1, QKW)


def _mixer_pre(c_q, c_kv, kr, fq, fk, qlg, w_uq, kvlg, w_uk, w_uv, mqg, mkg, fqg, fkg, cos, sin, e_mat, p_mat, r_mat, pl_mat):
    q = bdot(_rms(c_q, qlg), w_uq)
    ckvn = _rms(c_kv, kvlg)
    k = bdot(ckvn, w_uk) + tile_heads(cdot(kr, r_mat))
    v = bdot(ckvn, w_uv)
    q = _rope(_head_rms(q, e_mat, 1.0 / MLA_QK_DIM, mqg), p_mat, cos, sin)
    k = _rope(_head_rms(k, e_mat, 1.0 / MLA_QK_DIM, mkg), p_mat, cos, sin)
    qf = _head_rms(cdot(fq, pl_mat), e_mat, 1.0 / HEAD_V, fqg)
    kf = _head_rms(cdot(fk, pl_mat), e_mat, 1.0 / HEAD_V, fkg)
    return q, k, v, qf, kf


def _s5_post(state, u, c_blk, d_row, w_glu, b_glu):
    y = bdot(state, c_blk) + d_row * u
    y = jax.nn.gelu(y)
    return y * jax.nn.sigmoid(bdot(y, w_glu) + b_glu)


def _merge(o_a, o_b, o_c, g_a, g_b, g_c, w_a, w_b, w_c):
    return (jax.nn.sigmoid(g_a) * bdot(o_a, w_a) + jax.nn.sigmoid(g_b) * bdot(o_b, w_b)
            + jax.nn.sigmoid(g_c) * bdot(o_c, w_c))


def _ffn_mid(gate, val, wg0, wg1, wg2, wv0, wv1, wv2):
    cg = wg0 * shift_rows(gate, 2) + wg1 * shift_rows(gate, 1) + wg2 * gate
    cv = wv0 * shift_rows(val, 2) + wv1 * shift_rows(val, 1) + wv2 * val
    return jax.nn.silu(cg) * cv


def _s5_params(lam_re, lam_im, log_step, b_re, b_im):
    step = jnp.exp(log_step)
    zr, zi = lam_re * step, lam_im * step
    mag = jnp.exp(zr)
    lr, li = mag * jnp.cos(zi), mag * jnp.sin(zi)
    nr, ni = lr - 1.0, li
    den = lam_re * lam_re + lam_im * lam_im
    cr = (nr * lam_re + ni * lam_im) / den
    ci = (ni * lam_re - nr * lam_im) / den
    return lr, li, cr * b_re - ci * b_im, cr * b_im + ci * b_re


def _adamw(w, g, m, v):
    m = ADAM_B1 * m + (1.0 - ADAM_B1) * g
    v = ADAM_B2 * v + (1.0 - ADAM_B2) * (g * g)
    m_hat = m / (1.0 - ADAM_B1 ** ADAM_STEP)
    v_hat = v / (1.0 - ADAM_B2 ** ADAM_STEP)
    delta = -ADAM_LR * (m_hat / (jnp.sqrt(v_hat) + ADAM_EPS) + ADAM_WD * w)
    return delta, m, v


SCAN_W = 256


def _scan_seq(xr, xi, ar, ai, reverse):
    n = xr.shape[0]
    s = 1
    while s < n:
        sr, si = _shift_rows_impl(xr, s, reverse), _shift_rows_impl(xi, s, reverse)
        xr, xi = xr + ar * sr - ai * si, xi + ar * si + ai * sr
        ar, ai = ar * ar - ai * ai, 2.0 * ar * ai
        s *= 2
    return xr, xi


def _scan_fwd(bu, lbar, bsz, name, rider=None):
    t = bu.shape[0]
    seq = t // bsz
    nb = bu.shape[1] // (2 * SCAN_W)

    def fn(b, a):
        xr, xi = _scan_seq(b[:, :SCAN_W], b[:, SCAN_W:], a[:, :SCAN_W], a[:, SCAN_W:], False)
        return jnp.concatenate([xr, xi], axis=1)

    blk = (seq, 2 * SCAN_W)
    return _tile_call(fn, name, (bsz, nb),
                      [(bu, blk, lambda i, j: (i, j)), (lbar, (1, 2 * SCAN_W), lambda i, j: (0, j))],
                      [((t, bu.shape[1]), F32, blk, lambda i, j: (i, j), None)], rider=rider)


def _scan_bwd(dstate, state, lbar, bsz, name, rider=None):
    t = state.shape[0]
    seq = t // bsz
    nb = state.shape[1] // (2 * SCAN_W)

    def fn(g, x, a):
        ar, ai = a[:, :SCAN_W], a[:, SCAN_W:]
        gr, gi = _scan_seq(g[:, :SCAN_W], g[:, SCAN_W:], ar, -ai, True)
        pr, pi = _shift_rows_impl(x[:, :SCAN_W], 1, False), _shift_rows_impl(x[:, SCAN_W:], 1, False)
        dar = jnp.sum(gr * pr + gi * pi, axis=0, keepdims=True)
        dai = jnp.sum(gi * pr - gr * pi, axis=0, keepdims=True)
        return jnp.concatenate([gr, gi], axis=1), jnp.concatenate([dar, dai], axis=1)

    blk = (seq, 2 * SCAN_W)
    return _tile_call(fn, name, (nb, bsz),
                      [(dstate, blk, lambda j, i: (i, j)), (state, blk, lambda j, i: (i, j)),
                       (lbar, (1, 2 * SCAN_W), lambda j, i: (0, j))],
                      [((t, state.shape[1]), F32, blk, lambda j, i: (i, j), None),
                       ((1, state.shape[1]), F32, (1, 2 * SCAN_W), lambda j, i: (0, j), (1,))], rider=rider)


def _shift_lanes(x, s, reverse):
    n = x.shape[1]
    idx = lax.broadcasted_iota(jnp.int32, x.shape, 1)
    if reverse:
        return jnp.where(idx < n - s, pltpu.roll(x, n - s, 1), 0.0)
    return jnp.where(idx >= s, pltpu.roll(x, s, 1), 0.0)


def _cumsum_lanes(x, reverse):
    s = 1
    while s < x.shape[1]:
        x = x + _shift_lanes(x, s, reverse)
        s *= 2
    return x


def _forget_fwd(z, bias, name):
    def fn(zv, bv):
        x = zv + bv
        logf = jnp.minimum(x, 0.0) - jnp.log(1.0 + jnp.exp(-jnp.abs(x)))
        return _cumsum_lanes(logf, False)

    return _tile_call(fn, name, (1,), [_par(z), _par(bias)],
                      [(z.shape, F32, z.shape, lambda i: (0, 0), None)])[0]


def _forget_bwd(dc, z, bias, name):
    def fn(dcv, zv, bv):
        x = zv + bv
        dlogf = _cumsum_lanes(dcv, True)
        dz = dlogf * jax.nn.sigmoid(-x)
        return dz, jnp.sum(dz, axis=1, keepdims=True)

    rows = z.shape[0]
    return _tile_call(fn, name, (1,), [_par(dc), _par(z), _par(bias)],
                      [(z.shape, F32, z.shape, lambda i: (0, 0), None),
                       ((rows, 1), F32, (rows, 1), lambda i: (0, 0), None)])


def _col(v2, x):
    idx = lax.broadcasted_iota(jnp.int32, v2.shape, 1)
    return jnp.sum(jnp.where(idx == x, v2, 0.0), axis=1, keepdims=True)


def _two_cols(c0, c1):
    idx = lax.broadcasted_iota(jnp.int32, (c0.shape[0], 2), 1)
    return jnp.where(idx == 0, c0, c1)


def _rows2(r0, r1):
    idx = lax.broadcasted_iota(jnp.int32, (2, r0.shape[1]), 0)
    return jnp.where(idx == 0, r0, r1)


def _attn_specs(seq, v_blk0, has_bias):
    qblk = pl.BlockSpec((None, TQ, 2 * HP), lambda b, p, i: (b, i, p))
    kblk = pl.BlockSpec((None, seq, 2 * HP), lambda b, p, i: (b, 0, p))
    vblk = pl.BlockSpec((None, seq, LANES), lambda b, p, i: (b, 0, v_blk0 + p))
    oblk = pl.BlockSpec((None, TQ, LANES), lambda b, p, i: (b, i, p))
    rowblk = pl.BlockSpec((None, None, 2, TQ), lambda b, p, i: (b, p, 0, i))
    colblk = pl.BlockSpec((None, None, seq, 2), lambda b, p, i: (b, p, 0, 0))
    return qblk, kblk, vblk, oblk, rowblk, colblk


def _attn_fwd(q, k, v, v_blk0, c_cols, c_rows, scale, name, rider=None):
    bsz, seq, _ = q.shape
    nq = seq // TQ
    has_bias = c_cols is not None
    n_in = 5 if has_bias else 3
    nc = TQ // QC

    def body(*refs):
        q_ref, k_ref, v_ref = refs[:3]
        cq_ref, ck_ref = refs[3:5] if has_bias else (None, None)
        o_ref, lse_ref = refs[n_in:]
        qi = pl.program_id(2)
        qb = [[q_ref[c * QC:(c + 1) * QC, x * HP:(x + 1) * HP].astype(BF16) for c in range(nc)] for x in range(2)]
        key_in = lax.broadcasted_iota(jnp.int32, (TQ, QC), 0)
        qry_in = lax.broadcasted_iota(jnp.int32, (TQ, QC), 1)

        def block(j, carry, masked):
            ks = pl.multiple_of(j * TQ, TQ)
            vt = v_ref[pl.ds(ks, TQ), :].T.astype(BF16)
            new = []
            for x in range(2):
                kx = k_ref[pl.ds(ks, TQ), x * HP:(x + 1) * HP].astype(BF16)
                ckx = _col(ck_ref[pl.ds(ks, TQ), :], x) if has_bias else None
                for c in range(nc):
                    m, l, acc = carry[x * nc + c]
                    s = lax.dot_general(kx, qb[x][c], NT, preferred_element_type=F32) * scale
                    if has_bias:
                        s = s + (cq_ref[x:x + 1, c * QC:(c + 1) * QC] - ckx)
                    if masked:
                        s = jnp.where(qry_in + c * QC >= key_in, s, NEG_INF)
                    m_new = jnp.maximum(m, jnp.max(s, axis=0, keepdims=True))
                    alpha = jnp.exp(m - m_new)
                    p = jnp.exp(s - m_new)
                    l = alpha * l + jnp.sum(p, axis=0, keepdims=True)
                    acc = alpha * acc + lax.dot_general(vt, p.astype(BF16), NN, preferred_element_type=F32)
                    new.append((m_new, l, acc))
            return tuple(new)

        init = tuple((jnp.full((1, QC), NEG_INF, F32), jnp.zeros((1, QC), F32), jnp.zeros((LANES, QC), F32))
                     for _ in range(2 * nc))
        carry = lax.fori_loop(0, qi, lambda j, cr: block(j, cr, False), init)
        carry = block(qi, carry, True)
        lane = lax.broadcasted_iota(jnp.int32, (QC, LANES), 1)
        lse_rows = []
        for x in range(2):
            lse_rows.append(jnp.concatenate([carry[x * nc + c][0] + jnp.log(carry[x * nc + c][1]) for c in range(nc)],
                                            axis=1))
        for c in range(nc):
            (_, l0, a0), (_, l1, a1) = carry[c], carry[nc + c]
            o_ref[c * QC:(c + 1) * QC, :] = jnp.where(lane < HEAD_V, (a0 / l0).T, (a1 / l1).T)
        lse_ref[...] = _rows2(lse_rows[0], lse_rows[1])

    qblk, kblk, vblk, oblk, rowblk, colblk = _attn_specs(seq, v_blk0, has_bias)
    in_specs, args = [qblk, kblk, vblk], [q, k, v]
    if has_bias:
        in_specs += [rowblk, colblk]
        args += [c_rows, c_cols]
    return _pcall(body, name=name, grid=(bsz, 2, nq), in_specs=in_specs, out_specs=[oblk, rowblk],
                  out_shape=[jax.ShapeDtypeStruct((bsz, seq, 2 * LANES), F32),
                             jax.ShapeDtypeStruct((bsz, 2, 2, seq), F32)], rider=rider)(*args)


def _attn_bwd(q, k, v, v_blk0, o, do, lse, c_cols, c_rows, scale, name, rider=None):
    bsz, seq, _ = q.shape
    nq = seq // TQ
    has_bias = c_cols is not None
    n_in = 8 if has_bias else 6
    nc = TQ // QC

    def body(*refs):
        q_ref, k_ref, v_ref, o_ref, do_ref, lse_ref = refs[:6]
        cq_ref, ck_ref = refs[6:8] if has_bias else (None, None)
        dq_ref, dk_ref, dv_ref = refs[n_in:n_in + 3]
        dck_ref, dcq_ref = refs[n_in + 3:n_in + 5] if has_bias else (None, None)
        qi = pl.program_id(2)

        @pl.when(qi == 0)
        def _():
            dk_ref[...] = jnp.zeros_like(dk_ref)
            dv_ref[...] = jnp.zeros_like(dv_ref)
            if has_bias:
                dck_ref[...] = jnp.zeros_like(dck_ref)

        lane = lax.broadcasted_iota(jnp.int32, (QC, LANES), 1)
        ones8 = jnp.ones((8, LANES), BF16)
        qb, dob, delta, lse_r = [], [], [], []
        for x in range(2):
            hm = jnp.logical_and(lane >= x * HEAD_V, lane < (x + 1) * HEAD_V)
            qb.append([]), dob.append([]), delta.append([]), lse_r.append([])
            for c in range(nc):
                rows = slice(c * QC, (c + 1) * QC)
                do_c = jnp.where(hm, do_ref[rows, :], 0.0)
                prod = do_c * o_ref[rows, :]
                hi = prod.astype(BF16)
                lo = (prod - hi.astype(F32)).astype(BF16)
                d8 = (lax.dot_general(ones8, hi, NT, preferred_element_type=F32)
                      + lax.dot_general(ones8, lo, NT, preferred_element_type=F32))
                delta[x].append(d8[0:1, :])
                dob[x].append(do_c.astype(BF16))
                qb[x].append(q_ref[rows, x * HP:(x + 1) * HP].astype(BF16))
                lse_r[x].append(lse_ref[x:x + 1, rows])
        key_in = lax.broadcasted_iota(jnp.int32, (TQ, QC), 0)
        qry_in = lax.broadcasted_iota(jnp.int32, (TQ, QC), 1)

        def block(j, carry, masked):
            ks = pl.multiple_of(j * TQ, TQ)
            vb = v_ref[pl.ds(ks, TQ), :].astype(BF16)
            new, key_sums = [], []
            for x in range(2):
                k32 = k_ref[pl.ds(ks, TQ), x * HP:(x + 1) * HP]
                kx, kt = k32.astype(BF16), k32.T.astype(BF16)
                ckx = _col(ck_ref[pl.ds(ks, TQ), :], x) if has_bias else None
                dk_acc = jnp.zeros((TQ, HP), F32)
                dv_acc = jnp.zeros((TQ, LANES), F32)
                key_sum = jnp.zeros((TQ, 1), F32)
                for c in range(nc):
                    dqt, dcq = carry[x * nc + c]
                    s = lax.dot_general(kx, qb[x][c], NT, preferred_element_type=F32) * scale
                    if has_bias:
                        s = s + (cq_ref[x:x + 1, c * QC:(c + 1) * QC] - ckx)
                    if masked:
                        s = jnp.where(qry_in + c * QC >= key_in, s, NEG_INF)
                    p = jnp.exp(s - lse_r[x][c])
                    dp = lax.dot_general(vb, dob[x][c], NT, preferred_element_type=F32)
                    ds = p * (dp - delta[x][c])
                    dsb = ds.astype(BF16)
                    dk_acc = dk_acc + lax.dot_general(dsb, qb[x][c], NN, preferred_element_type=F32)
                    dv_acc = dv_acc + lax.dot_general(p.astype(BF16), dob[x][c], NN, preferred_element_type=F32)
                    dqt = dqt + lax.dot_general(kt, dsb, NN, preferred_element_type=F32)
                    if has_bias:
                        dcq = dcq + jnp.sum(ds, axis=0, keepdims=True)
                        key_sum = key_sum + jnp.sum(ds, axis=1, keepdims=True)
                    new.append((dqt, dcq))
                dk_ref[pl.ds(ks, TQ), x * HP:(x + 1) * HP] += dk_acc * scale
                dv_ref[pl.ds(ks, TQ), :] += dv_acc
                key_sums.append(key_sum)
            if has_bias:
                dck_ref[pl.ds(ks, TQ), :] -= _two_cols(key_sums[0], key_sums[1])
            return tuple(new)

        init = tuple((jnp.zeros((HP, QC), F32), jnp.zeros((1, QC), F32)) for _ in range(2 * nc))
        carry = lax.fori_loop(0, qi, lambda j, cr: block(j, cr, False), init)
        carry = block(qi, carry, True)
        for x in range(2):
            for c in range(nc):
                dq_ref[c * QC:(c + 1) * QC, x * HP:(x + 1) * HP] = carry[x * nc + c][0].T * scale
        if has_bias:
            dcq_ref[...] = _rows2(jnp.concatenate([carry[c][1] for c in range(nc)], axis=1),
                                  jnp.concatenate([carry[nc + c][1] for c in range(nc)], axis=1))

    qblk, kblk, vblk, oblk, rowblk, colblk = _attn_specs(seq, v_blk0, has_bias)
    in_specs = [qblk, kblk, vblk, oblk, oblk, rowblk]
    out_specs = [qblk, kblk, pl.BlockSpec((None, seq, LANES), lambda b, p, i: (b, 0, p))]
    out_shape = [jax.ShapeDtypeStruct((bsz, seq, QKW), F32), jax.ShapeDtypeStruct((bsz, seq, QKW), F32),
                 jax.ShapeDtypeStruct((bsz, seq, 2 * LANES), F32)]
    args = [q, k, v, o, do, lse]
    if has_bias:
        in_specs += [rowblk, colblk]
        out_specs += [colblk, rowblk]
        out_shape += [jax.ShapeDtypeStruct((bsz, 2, seq, 2), F32), jax.ShapeDtypeStruct((bsz, 2, 2, seq), F32)]
        args += [c_rows, c_cols]
    return _pcall(body, name=name, grid=(bsz, 2, nq), in_specs=in_specs, out_specs=out_specs,
                  out_shape=out_shape, rider=rider)(*args)


def _coords(idx):
    return (idx // 4, (idx // 2) % 2, idx % 2)


class _Xfer:
    def __init__(self, srcs, out_shapes, plan):
        self.srcs, self.out_shapes, self.plan = list(srcs), list(out_shapes), plan
        self.n = len(self.srcs)
        me0 = jnp.int32(0)
        n_pieces = [len(plan(a, me0, me0)) for a in range(self.n)]
        self.offs = np.concatenate([[0], np.cumsum(n_pieces)]).astype(int)
        self.total = int(self.offs[-1])

    def sems(self):
        remote = (N_DEV - 1) * self.total
        return [pltpu.SemaphoreType.DMA((remote,)), pltpu.SemaphoreType.DMA((remote,)),
                pltpu.SemaphoreType.DMA((self.total,))]

    def copies(self, src_refs, out_refs, send_sems, recv_sems, local_sems):
        me = 4 * lax.axis_index("x") + 2 * lax.axis_index("y") + lax.axis_index("c")
        out = []
        for a in range(self.n):
            for pi, (si, di) in enumerate(self.plan(a, me, me)):
                out.append((pltpu.make_async_copy(src_refs[a].at[si], out_refs[a].at[di],
                                                  local_sems.at[self.offs[a] + pi]), None))
        for kk in range(1, N_DEV):
            dest = (me + kk) % N_DEV
            src_dev = (me + N_DEV - kk) % N_DEV
            for a in range(self.n):
                pieces = self.plan(a, me, dest)
                landing = self.plan(a, src_dev, me)
                for pi, ((si, di), (_, li)) in enumerate(zip(pieces, landing)):
                    sem = (kk - 1) * self.total + self.offs[a] + pi
                    mk = functools.partial(pltpu.make_async_remote_copy, src_ref=src_refs[a].at[si],
                                           send_sem=send_sems.at[sem], recv_sem=recv_sems.at[sem],
                                           device_id=_coords(dest), device_id_type=pl.DeviceIdType.MESH)
                    out.append((mk(dst_ref=out_refs[a].at[di]), mk(dst_ref=out_refs[a].at[li])))
        return out

    @staticmethod
    def start(copies):
        for cp, _ in copies:
            cp.start()

    @staticmethod
    def wait(copies):
        for cp, rc in copies:
            if rc is None:
                cp.wait()
            else:
                cp.wait_send()
                rc.wait_recv()


def _exchange(name, xf):
    n = xf.n

    def body(*refs):
        copies = xf.copies(refs[:n], refs[n:2 * n], *refs[2 * n:])
        xf.start(copies)
        xf.wait(copies)

    any_spec = pl.BlockSpec(memory_space=pl.ANY)
    return pl.pallas_call(body, name=name, in_specs=[any_spec] * n, out_specs=[any_spec] * n,
                          out_shape=[jax.ShapeDtypeStruct(s, d) for s, d in xf.out_shapes],
                          scratch_shapes=xf.sems(),
                          compiler_params=pltpu.CompilerParams(has_side_effects=True))(*xf.srcs)


def _rows_of(j, n):
    return pl.ds(pl.multiple_of(j * n, n), n)


def _ffn_cols(j, half):
    return pl.multiple_of((2 * (j % 4) + half) * (2 * FF_BLK) + (j // 4) * FF_BLK, FF_BLK)


def _gather_plan(shards):
    names = list(shards)

    def plan(a, src_dev, _dest):
        nm = names[a]
        j = src_dev
        if nm in ("w_in", "w_uq", "w_ukv", "w_glu", "w_out", "w_down"):
            rows = shards[nm].shape[0]
            return [((slice(None), slice(None)), (_rows_of(j, rows), slice(None)))]
        if nm == "w_branch":
            return [((slice(None), slice(None), slice(None)), (slice(None), slice(None), _rows_of(j, LANES)))]
        if nm in ("w_up", "conv"):
            return [((slice(None), pl.ds(h * FF_BLK, FF_BLK)), (slice(None), pl.ds(_ffn_cols(j, h), FF_BLK)))
                    for h in range(2)]
        raise KeyError(nm)

    return names, plan


def _scatter_plan(names, shard_shapes):
    def plan(a, src_dev, dest):
        nm = names[a]
        j = dest
        if nm in ("w_in", "w_uq", "w_ukv", "w_glu", "w_out", "w_down"):
            rows = shard_shapes[nm][0]
            return [((_rows_of(j, rows), slice(None)), (src_dev, slice(None), slice(None)))]
        if nm == "w_branch":
            return [((slice(None), slice(None), _rows_of(j, LANES)), (src_dev, slice(None), slice(None), slice(None)))]
        if nm in ("w_up", "conv"):
            return [((slice(None), pl.ds(_ffn_cols(j, h), FF_BLK)), (src_dev, slice(None), pl.ds(h * FF_BLK, FF_BLK)))
                    for h in range(2)]
        if nm == "small":
            return [((slice(None), slice(None)), (src_dev, slice(None), slice(None)))]
        raise KeyError(nm)

    return plan


def _pad_in_cols(w):
    lead = w.shape[:-1]
    parts, pos = [], 0
    for dst, src, width in IN_SEGS:
        if dst > pos:
            parts.append(jnp.zeros(lead + (dst - pos,), w.dtype))
        parts.append(w[..., src:src + width])
        pos = dst + width
    return jnp.concatenate(parts, axis=-1)


def _unpad_in_cols(w):
    order = sorted(IN_SEGS, key=lambda s: s[1])
    return jnp.concatenate([w[..., dst:dst + width] for dst, _, width in order], axis=-1)


def _pad_ff(w, axis):
    n = w.shape[axis] // FF_HALF
    parts = []
    for h in range(n):
        piece = lax.slice_in_dim(w, h * FF_HALF, (h + 1) * FF_HALF, axis=axis)
        zshape = list(w.shape)
        zshape[axis] = FF_BLK - FF_HALF
        parts += [piece, jnp.zeros(zshape, w.dtype)]
    return jnp.concatenate(parts, axis=axis)


def _unpad_ff(w, axis):
    n = w.shape[axis] // FF_BLK
    return jnp.concatenate([lax.slice_in_dim(w, h * FF_BLK, h * FF_BLK + FF_HALF, axis=axis) for h in range(n)],
                           axis=axis)


def _to_scan_cols(re_part, im_part):
    lead = re_part.shape[:-1]
    nb = S5_N // SCAN_W
    r = re_part.reshape(lead + (nb, SCAN_W))
    i = im_part.reshape(lead + (nb, SCAN_W))
    return jnp.concatenate([r, i], axis=-1).reshape(lead + (2 * S5_N,))


def _from_scan_cols(x):
    lead = x.shape[:-1]
    nb = S5_N // SCAN_W
    y = x.reshape(lead + (nb, 2, SCAN_W))
    return y[..., 0, :].reshape(lead + (S5_N,)), y[..., 1, :].reshape(lead + (S5_N,))


def _block_diag_in(b):
    eye = jnp.eye(S5_GROUPS, dtype=b.dtype)
    return jnp.einsum("gph,gk->ghkp", b, eye).reshape(S5_WIDTH, S5_N)


def _block_diag_in_t(m):
    eye = jnp.eye(S5_GROUPS, dtype=m.dtype)
    return jnp.einsum("ghkp,gk->gph", m.reshape(S5_GROUPS, S5_GROUP_CH, S5_GROUPS, S5_STATE), eye)


def _block_diag_out(c):
    eye = jnp.eye(S5_GROUPS, dtype=c.dtype)
    return jnp.einsum("ghp,gk->gpkh", c, eye).reshape(S5_N, S5_WIDTH)


def _block_diag_out_t(m):
    eye = jnp.eye(S5_GROUPS, dtype=m.dtype)
    return jnp.einsum("gpkh,gk->ghp", m.reshape(S5_GROUPS, S5_STATE, S5_GROUPS, S5_GROUP_CH), eye)


def _pad_heads(g, width):
    g = jnp.broadcast_to(g.reshape(-1, width), (N_HEADS, width))
    return jnp.pad(g, ((0, 0), (0, HP - width))).reshape(1, QKW)


def _prep_layer(l, gw, small, consts):
    e_mat, p_mat, r_mat, pl_mat = consts
    p = {}
    p["attn_g"] = small["attn_norm_g"][l].reshape(1, D_MODEL)
    p["ffn_g"] = small["ffn_norm_g"][l].reshape(1, D_MODEL)
    p["qlg"] = small["q_lat_norm_g"][l].reshape(1, MLA_Q_RANK)
    p["kvlg"] = small["kv_lat_norm_g"][l].reshape(1, MLA_KV_RANK)
    p["mqg"] = _pad_heads(small["mla_q_norm_g"][l], MLA_QK_DIM)
    p["mkg"] = _pad_heads(small["mla_k_norm_g"][l], MLA_QK_DIM)
    p["fqg"] = _pad_heads(small["fox_q_norm_g"][l], HEAD_V)
    p["fkg"] = _pad_heads(small["fox_k_norm_g"][l], HEAD_V)
    w_uq = gw["w_uq"].reshape(MLA_Q_RANK, N_HEADS, MLA_QK_DIM)
    p["w_uq"] = jnp.pad(w_uq, ((0, 0), (0, 0), (0, HP - MLA_QK_DIM))).reshape(MLA_Q_RANK, QKW)
    w_ukv = gw["w_ukv"].reshape(MLA_KV_RANK, N_HEADS, MLA_NOPE_DIM + HEAD_V)
    p["w_uk"] = jnp.pad(w_ukv[..., :MLA_NOPE_DIM], ((0, 0), (0, 0), (0, HP - MLA_NOPE_DIM))).reshape(MLA_KV_RANK, QKW)
    p["w_uv"] = w_ukv[..., MLA_NOPE_DIM:].reshape(MLA_KV_RANK, N_HEADS * HEAD_V)
    p["w_glu"] = gw["w_glu"]
    p["b_glu"] = small["s5_b_glu"][l].reshape(1, S5_WIDTH)
    p["d_row"] = small["s5_d"][l].reshape(1, S5_WIDTH)
    p["lam_re"] = small["s5_lambda_re"][l].reshape(S5_N, 1)
    p["lam_im"] = small["s5_lambda_im"][l].reshape(S5_N, 1)
    p["log_step"] = jnp.repeat(small["s5_log_step"][l], S5_STATE).reshape(S5_N, 1)
    p["b_re"] = small["s5_b_re"][l].reshape(S5_N, S5_GROUP_CH)
    p["b_im"] = small["s5_b_im"][l].reshape(S5_N, S5_GROUP_CH)
    c_re, c_im = small["s5_c_re"][l], small["s5_c_im"][l]
    p["c_blk"] = _to_scan_cols(_block_diag_out(c_re).T, -_block_diag_out(c_im).T).T
    p["fbias"] = small["fox_f_bias"][l]
    for nm in ("w_in", "w_out", "w_branch"):
        p[nm] = gw[nm]
    return p


def _s5_param_call(p, l):
    ins = [p["lam_re"], p["lam_im"], p["log_step"], p["b_re"], p["b_im"]]
    outs = [((S5_N, 1), F32, (S5_N, 1), lambda i: (0, 0), None)] * 2 + \
           [((S5_N, S5_GROUP_CH), F32, (S5_N, S5_GROUP_CH), lambda i: (0, 0), None)] * 2
    return _tile_call(_s5_params, "s5_params", (1,), [_par(a) for a in ins], outs)


def _s5_param_bwd_call(p, cts):
    ins = [p["lam_re"], p["lam_im"], p["log_step"], p["b_re"], p["b_im"]]

    def fn(lr, li, ls, br, bi, g0, g1, g2, g3):
        _, vjp = jax.vjp(_s5_params, lr, li, ls, br, bi)
        return vjp((g0, g1, g2, g3))

    outs = [((S5_N, 1), F32, (S5_N, 1), lambda i: (0, 0), None)] * 3 + \
           [((S5_N, S5_GROUP_CH), F32, (S5_N, S5_GROUP_CH), lambda i: (0, 0), None)] * 2
    return _tile_call(fn, "s5_params_bwd", (1,), [_par(a) for a in ins + list(cts)], outs)


def _layer_fwd(x_prev, f_prev, p, tabs, consts, bsz, riders):
    arrived = {}

    def rider_of(name):
        return riders[name][0] if name in riders else None

    def hosted(name, res, n_out):
        if name in riders:
            arrived.update(zip(riders[name][1], res[n_out:]))
        return res[:n_out]

    t = x_prev.shape[0]
    seq = t // bsz
    nt = t // TM
    cos, sin = tabs
    e_mat, p_mat, r_mat, pl_mat = consts
    sv = {}

    if f_prev is None:
        x = x_prev
        h = _tile_call(lambda xv, g: _rms(xv, g), "norm_first", (nt,), [_row(x), _par(p["attn_g"])],
                       [_orow(t, D_MODEL, BF16)])[0]
    else:
        x, h = _tile_call(lambda xv, fv, g: (xv + fv, _rms(xv + fv, g)), "norm_attn", (nt,),
                          [_row(x_prev), _row(f_prev), _par(p["attn_g"])], [_orow(t, D_MODEL), _orow(t, D_MODEL, BF16)])
    sv["x"], sv["h"] = x, h
    proj = _mm(h, p["w_in"], "nn", "mm_in")
    sv["proj"] = proj

    def pre_fn(pa, cosv, sinv, qlg, w_uq, kvlg, w_uk, w_uv, mqg, mkg, fqg, fkg, em, pm, rm, plm):
        return _mixer_pre(pa[:, O_CQ:O_CQ + MLA_Q_RANK], pa[:, O_CKV:O_CKV + MLA_KV_RANK], pa[:, O_KR:O_KR + LANES],
                          pa[:, O_FQ:O_FQ + 256], pa[:, O_FK:O_FK + 256], qlg, w_uq, kvlg, w_uk, w_uv,
                          mqg, mkg, fqg, fkg, cosv, sinv, em, pm, rm, plm)

    pre_params = [p["qlg"], p["w_uq"], p["kvlg"], p["w_uk"], p["w_uv"], p["mqg"], p["mkg"], p["fqg"], p["fkg"],
                  e_mat, p_mat, r_mat, pl_mat]
    q_m, k_m, v_m, q_f, k_f = hosted("mixer_pre", _tile_call(
        pre_fn, "mixer_pre", (nt,), [_row(proj, PA), _row(cos), _row(sin)] + [_par(a) for a in pre_params],
        [_orow(t, QKW), _orow(t, QKW), _orow(t, 2 * LANES), _orow(t, QKW), _orow(t, QKW)],
        rider=rider_of("mixer_pre")), 5)
    sv.update(q_m=q_m, k_m=k_m, v_m=v_m, q_f=q_f, k_f=k_f)

    z = proj[:, O_FL:O_FL + N_HEADS].reshape(bsz, seq, N_HEADS).transpose(0, 2, 1).reshape(bsz * N_HEADS, seq)
    fb = jnp.tile(p["fbias"], bsz).reshape(bsz * N_HEADS, 1)
    cum = _forget_fwd(z, fb, "forget_fwd")
    ck = cum.reshape(bsz, 2, 2, seq)
    cq = ck.transpose(0, 1, 3, 2)
    sv.update(z=z, fb=fb, cq=cq, ck=ck)

    r3 = lambda a: a.reshape(bsz, seq, a.shape[-1])
    o_m, lse_m = hosted("attn_mla_fwd", _attn_fwd(r3(q_m), r3(k_m), r3(v_m), 0, None, None, MLA_QK_DIM ** -0.5,
                                                  "attn_mla_fwd", rider_of("attn_mla_fwd")), 2)
    o_f, lse_f = hosted("attn_fox_fwd", _attn_fwd(r3(q_f), r3(k_f), r3(proj), O_FV // LANES, cq, ck, HEAD_V ** -0.5,
                                                  "attn_fox_fwd", rider_of("attn_fox_fwd")), 2)
    o_m, o_f = o_m.reshape(t, 2 * LANES), o_f.reshape(t, 2 * LANES)
    sv.update(o_m=o_m, o_f=o_f, lse_m=lse_m, lse_f=lse_f)

    lbar_re, lbar_im, bbar_re, bbar_im = _s5_param_call(p, 0)
    lbar = _to_scan_cols(lbar_re.reshape(1, S5_N), lbar_im.reshape(1, S5_N))
    b_blk = _to_scan_cols(_block_diag_in(bbar_re.reshape(S5_GROUPS, S5_STATE, S5_GROUP_CH)),
                          _block_diag_in(bbar_im.reshape(S5_GROUPS, S5_STATE, S5_GROUP_CH)))
    sv.update(lbar=lbar, b_blk=b_blk)
    bu = _tile_call(lambda u, bb: bdot(u, bb), "s5_bu", (nt,), [_row(proj, S5_WIDTH, O_U // S5_WIDTH), _par(b_blk)],
                    [_orow(t, 2 * S5_N)])[0]
    state = hosted("s5_scan_fwd", _scan_fwd(bu, lbar, bsz, "s5_scan_fwd", rider_of("s5_scan_fwd")), 1)[0]
    sv["state"] = state
    post_params = [p["c_blk"], p["d_row"], p["w_glu"], p["b_glu"]]
    o_s = _tile_call(_s5_post, "s5_post", (nt,),
                     [_row(state), _row(proj, S5_WIDTH, O_U // S5_WIDTH)] + [_par(a) for a in post_params],
                     [_orow(t, S5_WIDTH)])[0]
    sv["o_s"] = o_s

    def merge_fn(oa, ob, oc, ga, gb, gc, wb):
        return _merge(oa, ob, oc, ga, gb, gc, wb[0], wb[1], wb[2])

    gate_specs = [_row(proj, D_MODEL, PA // D_MODEL + n) for n in range(3)]
    merged = _tile_call(merge_fn, "merge", (nt,), [_row(o_m), _row(o_f), _row(o_s)] + gate_specs + [_par(p["w_branch"])],
                        [_orow(t, D_MODEL, BF16)])[0]
    sv["merged"] = merged
    attn_out = _mm(merged, p["w_out"], "nn", "mm_out")

    x1, h2 = _tile_call(lambda xv, av, g: (xv + av, _rms(xv + av, g)), "norm_ffn", (nt,),
                        [_row(x), _row(attn_out), _par(p["ffn_g"])], [_orow(t, D_MODEL), _orow(t, D_MODEL, BF16)])
    sv["x1"], sv["h2"] = x1, h2
    p.update({n: arrived.pop(n) for n in FFN_WEIGHTS})
    up = _mm(h2, p["w_up"], "nn", "mm_up")
    sv["up"] = up

    def ffn_fn(upv, cw):
        return _ffn_mid(upv[:, :FF_BLK], upv[:, FF_BLK:], cw[0:1, :FF_BLK], cw[1:2, :FF_BLK], cw[2:3, :FF_BLK],
                        cw[0:1, FF_BLK:], cw[1:2, FF_BLK:], cw[2:3, FF_BLK:])

    nblk = FF_PW // FF_BLK
    act = _tile_call(ffn_fn, "ffn_mid", (nblk, bsz),
                     [(up, (seq, 2 * FF_BLK), lambda j, b: (b, j)), (p["conv"], (3, 2 * FF_BLK), lambda j, b: (0, j))],
                     [((t, FF_PW), BF16, (seq, FF_BLK), lambda j, b: (b, j), None)])[0]
    sv["act"] = act
    ffn_out = _mm(act, p["w_down"], "nn", "mm_down")
    return x1, ffn_out, sv, arrived


def _layer_bwd(dx2, p, sv, tabs, consts, bsz, scatter, carry):
    lands, carry_lands = {}, {}

    def riding(host, g):
        names = BWD_RIDERS[host]
        return scatter({n: g[n] for n in names}), names

    def landed(host, res, n_out):
        lands.update(zip(BWD_RIDERS[host], res[n_out:]))
        return res[:n_out]

    t = dx2.shape[0]
    seq = t // bsz
    nt = t // TM
    cos, sin = tabs
    e_mat, p_mat, r_mat, pl_mat = consts
    g = {}

    g["w_down"] = _mm(sv["act"], dx2, "tn", "mm_down_dw", out_dtype=BF16)
    dact = _mm(dx2, p["w_down"], "nt", "mm_down_dx")

    def ffn_bwd_fn(upv, cw, da):
        args = (upv[:, :FF_BLK], upv[:, FF_BLK:], cw[0:1, :FF_BLK], cw[1:2, :FF_BLK], cw[2:3, :FF_BLK],
                cw[0:1, FF_BLK:], cw[1:2, FF_BLK:], cw[2:3, FF_BLK:])
        _, vjp = jax.vjp(_ffn_mid, *args)
        dg, dv, g0, g1, g2, v0, v1, v2 = vjp(da)
        return (jnp.concatenate([dg, dv], axis=1), jnp.concatenate([g0, v0], axis=1),
                jnp.concatenate([g1, v1], axis=1), jnp.concatenate([g2, v2], axis=1))

    nblk = FF_PW // FF_BLK
    cw_out = ((1, UP_W), F32, (1, 2 * FF_BLK), lambda j, b: (0, j), (1,))
    res = _tile_call(
        ffn_bwd_fn, "ffn_mid_bwd", (nblk, bsz),
        [(sv["up"], (seq, 2 * FF_BLK), lambda j, b: (b, j)), (p["conv"], (3, 2 * FF_BLK), lambda j, b: (0, j)),
         (dact, (seq, FF_BLK), lambda j, b: (b, j))],
        [((t, UP_W), BF16, (seq, 2 * FF_BLK), lambda j, b: (b, j), None), cw_out, cw_out, cw_out],
        rider=scatter(carry) if carry else None)
    dup, dc0, dc1, dc2 = res[:4]
    carry_lands.update(zip(carry, res[4:]))
    g["conv"] = jnp.concatenate([dc0, dc1, dc2], axis=0)
    g["w_up"] = _mm(sv["h2"], dup, "tn", "mm_up_dw", out_dtype=BF16)
    dh2 = _mm(dup, p["w_up"], "nt", "mm_up_dx")

    def norm_bwd_fn(xv, gv, dh, dres):
        _, vjp = jax.vjp(_rms, xv, gv)
        dxv, dg = vjp(dh)
        return dres + dxv, dg

    d1, g["ffn_g"] = _tile_call(norm_bwd_fn, "norm_bwd", (nt,),
                                [_row(sv["x1"]), _par(p["ffn_g"]), _row(dh2), _row(dx2)],
                                [_orow(t, D_MODEL), _oacc((1, D_MODEL))])

    g["w_out"] = _mm(sv["merged"], d1, "tn", "mm_out_dw", out_dtype=BF16)
    dmerged = _mm(d1, p["w_out"], "nt", "mm_out_dx")

    def merge_bwd_fn(oa, ob, oc, ga, gb, gc, wb, dm):
        wf = wb.astype(F32)
        _, vjp = jax.vjp(_merge, oa, ob, oc, ga, gb, gc, wf[0], wf[1], wf[2])
        doa, dob, doc, dga, dgb, dgc, dwa, dwb, dwc = vjp(dm)
        return doa, dob, doc, jnp.concatenate([dga, dgb, dgc], axis=1), jnp.stack([dwa, dwb, dwc])

    proj = sv["proj"]
    gate_specs = [_row(proj, D_MODEL, PA // D_MODEL + n) for n in range(3)]
    do_m, do_f, do_s, dgl, g["w_branch"] = _tile_call(
        merge_bwd_fn, "merge_bwd", (nt,),
        [_row(sv["o_m"]), _row(sv["o_f"]), _row(sv["o_s"])] + gate_specs + [_par(p["w_branch"]), _row(dmerged)],
        [_orow(t, 2 * LANES), _orow(t, 2 * LANES), _orow(t, S5_WIDTH), _orow(t, 3 * D_MODEL),
         _oacc((3, S5_WIDTH, D_MODEL))])

    def post_bwd_fn(st, u, cb, dr, wg, bg, do):
        _, vjp = jax.vjp(_s5_post, st, u, cb.astype(F32), dr, wg.astype(F32), bg)
        return vjp(do)

    u_spec = _row(proj, S5_WIDTH, O_U // S5_WIDTH)
    dstate, du1, dc_blk, g_d, g["w_glu"], g_bglu = _tile_call(
        post_bwd_fn, "s5_post_bwd", (nt,),
        [_row(sv["state"]), u_spec, _par(p["c_blk"]), _par(p["d_row"]), _par(p["w_glu"]), _par(p["b_glu"]), _row(do_s)],
        [_orow(t, 2 * S5_N), _orow(t, S5_WIDTH), _oacc((2 * S5_N, S5_WIDTH)), _oacc((1, S5_WIDTH)),
         _oacc((S5_WIDTH, S5_WIDTH)), _oacc((1, S5_WIDTH))])
    dbu, dlbar = landed("s5_scan_bwd", _scan_bwd(dstate, sv["state"], sv["lbar"], bsz, "s5_scan_bwd",
                                                 riding("s5_scan_bwd", g)[0]), 2)

    def bu_bwd_fn(u, bb, dbv, du_in):
        _, vjp = jax.vjp(bdot, u, bb)
        du, dbb = vjp(dbv)
        return du_in + du, dbb

    du, db_blk = _tile_call(bu_bwd_fn, "s5_bu_bwd", (nt,), [u_spec, _par(sv["b_blk"]), _row(dbu), _row(du1)],
                            [_orow(t, S5_WIDTH), _oacc((S5_WIDTH, 2 * S5_N))])
    dlr, dli = _from_scan_cols(dlbar)
    dbr, dbi = _from_scan_cols(db_blk)
    cts = (dlr.reshape(S5_N, 1), dli.reshape(S5_N, 1),
           _block_diag_in_t(dbr).reshape(S5_N, S5_GROUP_CH), _block_diag_in_t(dbi).reshape(S5_N, S5_GROUP_CH))
    g_lr, g_li, g_ls, g_br, g_bi = _s5_param_bwd_call(p, cts)
    dc_re, dc_im = _from_scan_cols(dc_blk.T)
    g["s5_lambda_re"] = g_lr.reshape(S5_GROUPS, S5_STATE)
    g["s5_lambda_im"] = g_li.reshape(S5_GROUPS, S5_STATE)
    g["s5_log_step"] = jnp.sum(g_ls.reshape(S5_GROUPS, S5_STATE), axis=1)
    g["s5_b_re"] = g_br.reshape(S5_GROUPS, S5_STATE, S5_GROUP_CH)
    g["s5_b_im"] = g_bi.reshape(S5_GROUPS, S5_STATE, S5_GROUP_CH)
    g["s5_c_re"] = _block_diag_out_t(dc_re.T)
    g["s5_c_im"] = -_block_diag_out_t(dc_im.T)
    g["s5_d"] = g_d.reshape(S5_GROUPS, S5_GROUP_CH)
    g["s5_b_glu"] = g_bglu.reshape(S5_WIDTH)

    r3 = lambda a: a.reshape(bsz, seq, a.shape[-1])
    dq_m, dk_m, dv_m = landed("attn_mla_bwd", _attn_bwd(
        r3(sv["q_m"]), r3(sv["k_m"]), r3(sv["v_m"]), 0, r3(sv["o_m"]), r3(do_m), sv["lse_m"], None, None,
        MLA_QK_DIM ** -0.5, "attn_mla_bwd", riding("attn_mla_bwd", g)[0]), 3)
    dq_f, dk_f, dv_f, dcq, dck = landed("attn_fox_bwd", _attn_bwd(
        r3(sv["q_f"]), r3(sv["k_f"]), r3(proj), O_FV // LANES, r3(sv["o_f"]), r3(do_f), sv["lse_f"], sv["cq"], sv["ck"],
        HEAD_V ** -0.5, "attn_fox_bwd", riding("attn_fox_bwd", g)[0]), 5)
    dcum = (dck + dcq.transpose(0, 1, 3, 2)).reshape(bsz * N_HEADS, seq)
    dz, dfb = _forget_bwd(dcum, sv["z"], sv["fb"], "forget_bwd")
    g["fox_f_bias"] = jnp.sum(dfb.reshape(bsz, N_HEADS), axis=0)
    dfl = jnp.pad(dz.reshape(bsz, N_HEADS, seq).transpose(0, 2, 1).reshape(t, N_HEADS), ((0, 0), (0, LANES - N_HEADS)))

    def pre_bwd_fn(pa, cosv, sinv, qlg, w_uq, kvlg, w_uk, w_uv, mqg, mkg, fqg, fkg, em, pm, rm, plm,
                   gq, gk, gv, gqf, gkf, gvf, gu, gfl, ggl):
        f = functools.partial(_mixer_pre, cos=cosv, sin=sinv, e_mat=em, p_mat=pm, r_mat=rm, pl_mat=plm)
        prim = (pa[:, O_CQ:O_CQ + MLA_Q_RANK], pa[:, O_CKV:O_CKV + MLA_KV_RANK], pa[:, O_KR:O_KR + LANES],
                pa[:, O_FQ:O_FQ + 256], pa[:, O_FK:O_FK + 256], qlg, w_uq.astype(F32), kvlg, w_uk.astype(F32),
                w_uv.astype(F32), mqg, mkg, fqg, fkg)
        _, vjp = jax.vjp(lambda *a: f(*a), *prim)
        dcq_, dckv, dkr, dfq, dfk, dqlg, dwuq, dkvlg, dwuk, dwuv, dmqg, dmkg, dfqg, dfkg = vjp((gq, gk, gv, gqf, gkf))
        zpad = jnp.zeros((pa.shape[0], PA - O_CQ - MLA_Q_RANK), F32)
        dproj = jnp.concatenate([dckv, dfq, dfk, gvf, gu, dkr, gfl, dcq_, zpad, ggl], axis=1)
        return dproj, dqlg, dwuq, dkvlg, dwuk, dwuv, dmqg, dmkg, dfqg, dfkg

    pre_params = [p["qlg"], p["w_uq"], p["kvlg"], p["w_uk"], p["w_uv"], p["mqg"], p["mkg"], p["fqg"], p["fkg"],
                  e_mat, p_mat, r_mat, pl_mat]
    cts_in = [dq_m.reshape(t, QKW), dk_m.reshape(t, QKW), dv_m.reshape(t, 2 * LANES), dq_f.reshape(t, QKW),
              dk_f.reshape(t, QKW), dv_f.reshape(t, 2 * LANES), du, dfl, dgl]
    (dproj, g_qlg, g_wuq, g_kvlg, g_wuk, g_wuv, g_mqg, g_mkg, g_fqg, g_fkg) = _tile_call(
        pre_bwd_fn, "mixer_pre_bwd", (nt,),
        [_row(proj, PA), _row(cos), _row(sin)] + [_par(a) for a in pre_params] + [_row(a) for a in cts_in],
        [_orow(t, PW, BF16), _oacc((1, MLA_Q_RANK)), _oacc((MLA_Q_RANK, QKW)), _oacc((1, MLA_KV_RANK)),
         _oacc((MLA_KV_RANK, QKW)), _oacc((MLA_KV_RANK, N_HEADS * HEAD_V)), _oacc((1, QKW)), _oacc((1, QKW)),
         _oacc((1, QKW)), _oacc((1, QKW))])
    g["q_lat_norm_g"] = g_qlg.reshape(MLA_Q_RANK)
    g["kv_lat_norm_g"] = g_kvlg.reshape(MLA_KV_RANK)
    heads = lambda a, w: jnp.sum(a.reshape(N_HEADS, HP)[:, :w], axis=0)
    g["mla_q_norm_g"], g["mla_k_norm_g"] = heads(g_mqg, MLA_QK_DIM), heads(g_mkg, MLA_QK_DIM)
    g["fox_q_norm_g"], g["fox_k_norm_g"] = heads(g_fqg, HEAD_V), heads(g_fkg, HEAD_V)
    g["w_uq"] = g_wuq.reshape(MLA_Q_RANK, N_HEADS, HP)[..., :MLA_QK_DIM].reshape(MLA_Q_RANK, N_HEADS * MLA_QK_DIM)
    g["w_ukv"] = jnp.concatenate([g_wuk.reshape(MLA_KV_RANK, N_HEADS, HP)[..., :MLA_NOPE_DIM],
                                  g_wuv.reshape(MLA_KV_RANK, N_HEADS, HEAD_V)], axis=-1).reshape(MLA_KV_RANK, QKW)

    g["w_in"] = _mm(sv["h"], dproj, "tn", "mm_in_dw", out_dtype=BF16)
    dh = landed("mm_in_dx", _mm(dproj, p["w_in"], "nt", "mm_in_dx", rider=riding("mm_in_dx", g)[0]), 1)[0]
    dx, g["attn_g"] = _tile_call(norm_bwd_fn, "norm_bwd", (nt,),
                                 [_row(sv["x"]), _par(p["attn_g"]), _row(dh), _row(d1)],
                                 [_orow(t, D_MODEL), _oacc((1, D_MODEL))])
    return dx, g, lands, carry_lands


def _row_tile(r, c, max_elems):
    if r * c <= max_elems:
        return r
    best = None
    for d in range(8, r, 8):
        if r % d == 0 and d * c <= max_elems:
            best = d
    assert best is not None, (r, c)
    return best


def _sum8(land, name):
    _, r, c = land.shape
    tr = _row_tile(r, N_DEV * c, 1 << 20)

    def fn(lv):
        acc = lv[0].astype(F32)
        for i in range(1, N_DEV):
            acc = acc + lv[i].astype(F32)
        return acc

    return _tile_call(fn, name, (r // tr,), [(land, (N_DEV, tr, c), lambda i: (0, i, 0))],
                      [((r, c), F32, (tr, c), lambda i: (i, 0), None)])[0]


def _adamw_layer(w, g, m, v, l, prev, name):
    _, r, c = w.shape
    tr = _row_tile(r, c, 1 << 19)
    lay = pl.BlockSpec((None, tr, c), lambda i: (l, i, 0))

    def body(w_ref, g_ref, m_ref, v_ref, *rest):
        g_out, d_out, m_out, v_out = rest[-4:]
        gv = g_ref[...]
        d_out[...], m_out[...], v_out[...] = _adamw(w_ref[...], gv, m_ref[...], v_ref[...])
        g_out[...] = gv

    in_specs = [lay, pl.BlockSpec((tr, c), lambda i: (i, 0)), lay, lay]
    args = [w, g, m, v]
    aliases = {}
    if prev is not None:
        in_specs += [pl.BlockSpec(memory_space=pl.ANY)] * 4
        args += list(prev)
        aliases = {4 + k: k for k in range(4)}
    return _pcall(body, name=name, grid=(r // tr,), in_specs=in_specs, out_specs=[lay] * 4,
                  out_shape=[jax.ShapeDtypeStruct(w.shape, F32)] * 4, aliases=aliases)(*args)


SMALL_NAMES = ("attn_norm_g", "q_lat_norm_g", "kv_lat_norm_g", "mla_q_norm_g", "mla_k_norm_g", "fox_q_norm_g",
               "fox_k_norm_g", "fox_f_bias", "s5_lambda_re", "s5_lambda_im", "s5_b_re", "s5_b_im", "s5_c_re", "s5_c_im",
               "s5_d", "s5_log_step", "s5_b_glu", "ffn_norm_g")
BIG_NAMES = ("w_in", "w_uq", "w_ukv", "s5_w_glu", "w_branch", "w_out", "w_up", "ffn_conv_w", "w_down")
ALL_NAMES = ("attn_norm_g", "w_in", "q_lat_norm_g", "w_uq", "kv_lat_norm_g", "w_ukv", "mla_q_norm_g", "mla_k_norm_g",
             "fox_q_norm_g", "fox_k_norm_g", "fox_f_bias", "s5_lambda_re", "s5_lambda_im", "s5_b_re", "s5_b_im",
             "s5_c_re", "s5_c_im", "s5_d", "s5_log_step", "s5_w_glu", "s5_b_glu", "w_branch", "w_out", "ffn_norm_g",
             "w_up", "ffn_conv_w", "w_down")


def _pack(arrs):
    tile = 8 * LANES
    parts = []
    for a in arrs:
        n = int(np.prod(a.shape))
        tot = -(-n // tile) * tile
        parts.append(jnp.pad(a.reshape(-1), (0, tot - n)).reshape(tot // LANES, LANES))
    rows = sum(a.shape[0] for a in parts)
    parts.append(jnp.zeros((-rows % 256, LANES), parts[0].dtype))
    return jnp.concatenate(parts, axis=0)


def _unpack(packed, like):
    tile = 8 * LANES
    out, pos = [], 0
    for a in like:
        n = int(np.prod(a.shape))
        rows = -(-n // tile) * 8
        out.append(packed[pos:pos + rows].reshape(-1)[:n].reshape(a.shape))
        pos += rows
    return out


def _rope_tables(positions, t):
    pos = positions.reshape(t, 1)
    inv = _inv_freq_row()

    def fn(pv, iv):
        ang = pv.astype(F32) * iv
        return jnp.cos(ang), jnp.sin(ang)

    return _tile_call(fn, "rope_tables", (t // TM,), [_row(pos), _par(inv)], [_orow(t, QKW), _orow(t, QKW)])


def _loss_call(x1, f, target):
    t = x1.shape[0]

    def fn(xv, fv, tv):
        e = xv + fv - tv
        per_tok = jnp.sum(e * e, axis=1, keepdims=True) * (1.0 / D_MODEL)
        return 0.5 * jnp.sum(per_tok, axis=0, keepdims=True), e * (1.0 / D_MODEL)

    return _tile_call(fn, "loss", (t // TM,), [_row(x1), _row(f), _row(target)], [_oacc((1, 1)), _orow(t, D_MODEL)])


def _local_shards(l, w):
    return {
        "w_in": _pad_in_cols(w["w_in"][l]).astype(BF16),
        "w_uq": w["w_uq"][l].reshape(-1, N_HEADS * MLA_QK_DIM).astype(BF16),
        "w_ukv": w["w_ukv"][l].reshape(-1, QKW).astype(BF16),
        "w_glu": w["s5_w_glu"][l].astype(BF16),
        "w_branch": w["w_branch"][l].astype(BF16),
        "w_out": w["w_out"][l].astype(BF16),
        "w_up": _pad_ff(w["w_up"][l], 1).astype(BF16),
        "conv": _pad_ff(w["ffn_conv_w"][l], 1),
        "w_down": _pad_ff(w["w_down"][l], 0).astype(BF16),
    }


GATHERED_SHAPES = {"w_in": (D_MODEL, PW), "w_uq": (MLA_Q_RANK, N_HEADS * MLA_QK_DIM), "w_ukv": (MLA_KV_RANK, QKW),
                   "w_glu": (S5_WIDTH, S5_WIDTH), "w_branch": (3, S5_WIDTH, D_MODEL), "w_out": (D_MODEL, D_MODEL),
                   "w_up": (D_MODEL, UP_W), "conv": (3, UP_W), "w_down": (FF_PW, D_MODEL)}
FFN_WEIGHTS = ("w_up", "conv", "w_down")
MIXER_WEIGHTS = ("w_in", "w_uq", "w_ukv", "w_glu", "w_out", "w_branch")
FWD_RIDERS_OWN = {"attn_mla_fwd": ("w_up", "conv"), "s5_scan_fwd": ("w_down",)}
FWD_RIDERS_NEXT = {"attn_fox_fwd": ("w_in", "w_uq", "w_ukv", "w_glu"), "mixer_pre": ("w_out", "w_branch")}
BWD_RIDERS = {"s5_scan_bwd": ("w_down",), "attn_mla_bwd": ("w_up",),
              "attn_fox_bwd": ("w_out", "w_branch", "w_glu", "conv"), "mm_in_dx": ("w_uq", "w_ukv")}
WEIGHT_OF = {"w_in": "w_in", "w_uq": "w_uq", "w_ukv": "w_ukv", "w_glu": "s5_w_glu", "w_branch": "w_branch",
             "w_out": "w_out", "w_up": "w_up", "conv": "ffn_conv_w", "w_down": "w_down"}


def _unshard_layout(nm, g2):
    if nm == "w_in":
        return _unpad_in_cols(g2)
    if nm == "w_uq":
        return g2.reshape(-1, N_HEADS, MLA_QK_DIM)
    if nm == "w_ukv":
        return g2.reshape(-1, N_HEADS, MLA_NOPE_DIM + HEAD_V)
    if nm in ("w_up", "conv"):
        return _unpad_ff(g2, 1)
    if nm == "w_down":
        return _unpad_ff(g2, 0)
    return g2


def kernel(x, positions, attn_norm_g, w_in, q_lat_norm_g, w_uq, kv_lat_norm_g, w_ukv, mla_q_norm_g, mla_k_norm_g, fox_q_norm_g, fox_k_norm_g, fox_f_bias, s5_lambda_re, s5_lambda_im, s5_b_re, s5_b_im, s5_c_re, s5_c_im, s5_d, s5_log_step, s5_w_glu, s5_b_glu, w_branch, w_out, ffn_norm_g, w_up, ffn_conv_w, w_down, loss_target, m_attn_norm_g, m_w_in, m_q_lat_norm_g, m_w_uq, m_kv_lat_norm_g, m_w_ukv, m_mla_q_norm_g, m_mla_k_norm_g, m_fox_q_norm_g, m_fox_k_norm_g, m_fox_f_bias, m_s5_lambda_re, m_s5_lambda_im, m_s5_b_re, m_s5_b_im, m_s5_c_re, m_s5_c_im, m_s5_d, m_s5_log_step, m_s5_w_glu, m_s5_b_glu, m_w_branch, m_w_out, m_ffn_norm_g, m_w_up, m_ffn_conv_w, m_w_down, v_attn_norm_g, v_w_in, v_q_lat_norm_g, v_w_uq, v_kv_lat_norm_g, v_w_ukv, v_mla_q_norm_g, v_mla_k_norm_g, v_fox_q_norm_g, v_fox_k_norm_g, v_fox_f_bias, v_s5_lambda_re, v_s5_lambda_im, v_s5_b_re, v_s5_b_im, v_s5_c_re, v_s5_c_im, v_s5_d, v_s5_log_step, v_s5_w_glu, v_s5_b_glu, v_w_branch, v_w_out, v_ffn_norm_g, v_w_up, v_ffn_conv_w, v_w_down):
    args = locals()
    w = {n: args[n] for n in ALL_NAMES}
    m = {n: args["m_" + n] for n in ALL_NAMES}
    v = {n: args["v_" + n] for n in ALL_NAMES}
    depth = w_in.shape[0]
    bsz, seq, _ = x.shape
    t = bsz * seq
    consts = _constants()
    tabs = _rope_tables(positions, t)
    small = {n: w[n] for n in SMALL_NAMES}

    all_names = list(GATHERED_SHAPES)
    all_shards = [_local_shards(l, w) for l in range(depth)]

    def gather_xfer(l, names):
        sub = {n: all_shards[l][n] for n in names}
        order, plan = _gather_plan(sub)
        return _Xfer([sub[n] for n in order], [(GATHERED_SHAPES[n], sub[n].dtype) for n in order], plan)

    def scatter_xfer(g, names):
        shapes = {n: _local_shards_shape(n) for n in names}
        return _Xfer([g[n] for n in names], [((N_DEV,) + shapes[n], g[n].dtype) for n in names],
                     _scatter_plan(list(names), shapes))

    xs = x.reshape(t, D_MODEL)
    f_prev = None
    saved, params = [], []
    gathered = dict(zip(MIXER_WEIGHTS, _exchange("gather_weights", gather_xfer(0, MIXER_WEIGHTS))))
    for l in range(depth):
        p = _prep_layer(l, gathered, small, consts)
        riders = {host: (gather_xfer(l, names), names) for host, names in FWD_RIDERS_OWN.items()}
        if l + 1 < depth:
            riders.update({host: (gather_xfer(l + 1, names), names) for host, names in FWD_RIDERS_NEXT.items()})
        xs, f_prev, sv, gathered = _layer_fwd(xs, f_prev, p, tabs, consts, bsz, riders)
        saved.append(sv)
        params.append(p)

    loss_part, dy = _loss_call(xs, f_prev, loss_target.reshape(t, D_MODEL))
    loss = lax.psum(loss_part[0, 0], ("x", "y", "c"))

    small_grads = {n: [None] * depth for n in SMALL_NAMES}
    big = {n: None for n in BIG_NAMES}

    def finish(l, lands):
        for n in all_names:
            wn, ld = WEIGHT_OF[n], lands[n]
            ld2 = ld.reshape(N_DEV, -1, ld.shape[-1])
            gsum = _unshard_layout(n, _sum8(ld2, "sum_" + n).reshape(_local_shards_shape(n)))
            c = gsum.shape[-1]
            three = lambda a: a.reshape(depth, -1, c)
            big[wn] = _adamw_layer(three(w[wn]), gsum.reshape(-1, c), three(m[wn]), three(v[wn]), l, big[wn],
                                   "adamw_" + n)

    def scatter(grads):
        return scatter_xfer(grads, list(grads))

    dx = dy
    carry, lands_above = {}, None
    for l in reversed(range(depth)):
        dx, g, lands, carry_lands = _layer_bwd(dx, params[l], saved[l], tabs, consts, bsz, scatter, carry)
        if lands_above is not None:
            finish(l + 1, {**lands_above, **carry_lands})
        carry, lands_above = {"w_in": g["w_in"]}, lands
        small_grads["attn_norm_g"][l] = g["attn_g"].reshape(D_MODEL)
        small_grads["ffn_norm_g"][l] = g["ffn_g"].reshape(D_MODEL)
        for n in SMALL_NAMES:
            if n not in ("attn_norm_g", "ffn_norm_g"):
                small_grads[n][l] = g[n]
    finish(0, {**lands_above, "w_in": _exchange("scatter_grads", scatter(carry))[0]})

    sg = [jnp.stack(small_grads[n]).reshape(w[n].shape) for n in SMALL_NAMES]
    packed = _pack(sg)
    land = _exchange("gather_small_grads", _Xfer([packed], [((N_DEV,) + packed.shape, F32)],
                                                 _scatter_plan(["small"], {})))[0]
    gs = _sum8(land, "sum_small")
    one = lambda names_of: _pack([names_of[n] for n in SMALL_NAMES])[None]
    small_res = _adamw_layer(one(w), gs, one(m), one(v), 0, None, "adamw_small")
    like = [w[n] for n in SMALL_NAMES]
    small_out = [dict(zip(SMALL_NAMES, _unpack(a[0], like))) for a in small_res]

    def out_of(kind, n):
        if n in SMALL_NAMES:
            return small_out[kind][n]
        return big[n][kind].reshape(w[n].shape)

    outs = [loss, dx.reshape(bsz, seq, D_MODEL)]
    for kind in range(4):
        outs += [out_of(kind, n) for n in ALL_NAMES]
    return tuple(outs)


def _local_shards_shape(nm):
    return {"w_in": (D_MODEL // N_DEV, PW), "w_uq": (MLA_Q_RANK // N_DEV, N_HEADS * MLA_QK_DIM),
            "w_ukv": (MLA_KV_RANK // N_DEV, QKW), "w_glu": (S5_WIDTH // N_DEV, S5_WIDTH),
            "w_branch": (3, S5_WIDTH, LANES), "w_out": (D_MODEL // N_DEV, D_MODEL), "w_up": (D_MODEL, 2 * FF_BLK),
            "conv": (3, 2 * FF_BLK), "w_down": (FF_BLK, D_MODEL)}[nm]
```

```python
import functools
import math

import numpy as np
import jax
import jax.numpy as jnp
from jax import lax
from jax.experimental import pallas as pl
from jax.experimental.pallas import tpu as pltpu

F32, BF16 = jnp.float32, jnp.bfloat16

D_MODEL = 1024
N_DEV = 8
MLA_Q_RANK, MLA_KV_RANK, MLA_ROPE_DIM, MLA_NOPE_DIM, MLA_QK_DIM = 384, 256, 32, 64, 96
N_HEADS, HEAD_V = 4, 64
S5_GROUPS, S5_GROUP_CH, S5_STATE, S5_WIDTH = 16, 16, 64, 256
S5_N = S5_GROUPS * S5_STATE
D_FF = 2816
D_IN = 4772
ROPE_THETA = 10000.0
NORM_EPS = 1e-6
NEG_INF = -1e30
ADAM_LR, ADAM_B1, ADAM_B2, ADAM_EPS, ADAM_WD, ADAM_STEP = 0.001, 0.9, 0.999, 1e-08, 0.01, 10

VMEM_LIMIT_BYTES = 56 * 1024 * 1024
LANES = 128
TM = 256
TQ = 256
QC = 128
HP = 128
QKW = N_HEADS * HP

PW = 5120
PA = 2048
IN_SEGS = ((0, 384, 256),
           (256, 672, 256),
           (512, 928, 256),
           (768, 1184, 256),
           (1024, 1444, 256),
           (1280, 640, 32),
           (1408, 1440, 4),
           (1536, 0, 384),
           (2048, 1700, 3072))
O_CKV, O_FQ, O_FK, O_FV, O_U, O_KR, O_FL, O_CQ = 0, 256, 512, 768, 1024, 1280, 1408, 1536

FF_BLK = 384
FF_HALF = D_FF // 8
FF_PW = 8 * FF_BLK
UP_W = 2 * FF_PW

NN = (((1,), (0,)), ((), ()))
NT = (((1,), (1,)), ((), ()))
TN = (((0,), (0,)), ((), ()))


def _pcall(body, *, name, grid, in_specs, out_specs, out_shape, scratch_shapes=(), rider=None, aliases=None):
    params = pltpu.CompilerParams(dimension_semantics=("arbitrary",) * len(grid), vmem_limit_bytes=VMEM_LIMIT_BYTES)
    if rider is None:
        return pl.pallas_call(body, name=name, grid=grid, in_specs=in_specs, out_specs=out_specs, out_shape=out_shape,
                              scratch_shapes=list(scratch_shapes), input_output_aliases=aliases or {},
                              compiler_params=params)
    assert not aliases
    in_specs, out_specs, out_shape = list(in_specs), list(out_specs), list(out_shape)
    n_in, n_out, n_scr, nx = len(in_specs), len(out_specs), len(scratch_shapes), rider.n
    any_spec = pl.BlockSpec(memory_space=pl.ANY)

    def at_step(last):
        conds = [pl.program_id(a) == (g - 1 if last else 0) for a, g in enumerate(grid)]
        return functools.reduce(jnp.logical_and, conds)

    def body_with_rider(*refs):
        ins, xsrc = refs[:n_in], refs[n_in:n_in + nx]
        outs, xout = refs[n_in + nx:n_in + nx + n_out], refs[n_in + nx + n_out:n_in + 2 * nx + n_out]
        scr = refs[n_in + 2 * nx + n_out:n_in + 2 * nx + n_out + n_scr]
        sems = refs[n_in + 2 * nx + n_out + n_scr:]

        @pl.when(at_step(False))
        def _():
            rider.start(rider.copies(xsrc, xout, *sems))

        body(*ins, *outs, *scr)

        @pl.when(at_step(True))
        def _():
            rider.wait(rider.copies(xsrc, xout, *sems))

    call = pl.pallas_call(body_with_rider, name=name, grid=grid, in_specs=in_specs + [any_spec] * nx,
                          out_specs=out_specs + [any_spec] * nx,
                          out_shape=out_shape + [jax.ShapeDtypeStruct(s, d) for s, d in rider.out_shapes],
                          scratch_shapes=list(scratch_shapes) + rider.sems(), compiler_params=params)
    return lambda *args: call(*args, *rider.srcs)


def _tile_call(fn, name, grid, ins, outs, rider=None):
    n_in = len(ins)

    def body(*refs):
        res = fn(*[r[...] for r in refs[:n_in]])
        if not isinstance(res, (tuple, list)):
            res = (res,)
        assert len(res) == len(outs), (name, len(res), len(outs))
        for r, v, o in zip(refs[n_in:], res, outs):
            v = v.astype(r.dtype)
            if o[4] is None:
                r[...] = v
            else:
                first = functools.reduce(jnp.logical_and, [pl.program_id(a) == 0 for a in o[4]])

                @pl.when(first)
                def _():
                    r[...] = v

                @pl.when(jnp.logical_not(first))
                def _():
                    r[...] += v

    res = _pcall(body, name=name, grid=grid,
                 in_specs=[pl.BlockSpec(b, m) for _, b, m in ins],
                 out_specs=[pl.BlockSpec(o[2], o[3]) for o in outs],
                 out_shape=[jax.ShapeDtypeStruct(o[0], o[1]) for o in outs], rider=rider)(*[a for a, _, _ in ins])
    return res


def _row(a, width=None, col_block=0, tm=TM):
    w = a.shape[1] if width is None else width
    return (a, (tm, w), lambda i, c=col_block: (i, c))


def _par(a):
    nd = a.ndim
    return (a, a.shape, lambda i, nd=nd: (0,) * nd)


def _orow(t, w, dtype=F32, tm=TM):
    return ((t, w), dtype, (tm, w), lambda i: (i, 0), None)


def _oacc(shape):
    nd = len(shape)
    return (tuple(shape), F32, tuple(shape), lambda i, nd=nd: (0,) * nd, (0,))


def _pick(n, target):
    best = None
    for d in range(LANES, min(n, target) + 1, LANES):
        if n % d == 0:
            best = d
    return n if best is None else best


MM_VMEM_BUDGET = 44 * 1024 * 1024


def _mm_tiles(m, n, k, sa, sb, so, tm, tn, tk):
    for cm, cn in ((1024, 2048), (1024, 1024), (512, 2048), (512, 1024), (512, 512), (256, 1024), (256, 512)):
        pm, pn = _pick(m, cm), _pick(n, cn)
        if 2 * (pm * k * sa + k * pn * sb + pm * pn * so) <= MM_VMEM_BUDGET:
            return pm, pn, k
    return _pick(m, tm), _pick(n, tn), _pick(k, tk)


def _mm(a, b, mode, name, out_dtype=F32, tm=1024, tn=2048, tk=1024, rider=None):
    if mode == "nn":
        (m, k), (_, n) = a.shape, b.shape
    elif mode == "nt":
        (m, k), (n, _) = a.shape, b.shape
    else:
        (k, m), (_, n) = a.shape, b.shape
    tm, tn, tk = _mm_tiles(m, n, k, a.dtype.itemsize, b.dtype.itemsize, jnp.dtype(out_dtype).itemsize, tm, tn, tk)
    nk = k // tk
    dims = {"nn": NN, "nt": NT, "tn": TN}[mode]
    a_spec = (pl.BlockSpec((tk, tm), lambda i, j, l: (l, i)) if mode == "tn"
              else pl.BlockSpec((tm, tk), lambda i, j, l: (i, l)))
    b_spec = (pl.BlockSpec((tn, tk), lambda i, j, l: (j, l)) if mode == "nt"
              else pl.BlockSpec((tk, tn), lambda i, j, l: (l, j)))

    def body(a_ref, b_ref, o_ref, acc_ref):
        part = lax.dot_general(a_ref[...].astype(BF16), b_ref[...].astype(BF16), dims, preferred_element_type=F32)
        if nk == 1:
            o_ref[...] = part.astype(o_ref.dtype)
        else:
            l = pl.program_id(2)

            @pl.when(l == 0)
            def _():
                acc_ref[...] = part

            @pl.when(l > 0)
            def _():
                acc_ref[...] += part

            @pl.when(l == nk - 1)
            def _():
                o_ref[...] = acc_ref[...].astype(o_ref.dtype)

    res = _pcall(body, name=name, grid=(m // tm, n // tn, nk), in_specs=[a_spec, b_spec],
                 out_specs=[pl.BlockSpec((tm, tn), lambda i, j, l: (i, j))],
                 out_shape=[jax.ShapeDtypeStruct((m, n), out_dtype)],
                 scratch_shapes=[pltpu.VMEM((tm, tn) if nk > 1 else (8, LANES), F32)], rider=rider)(a, b)
    return res[0] if rider is None else res


def _dot(x, w, dims):
    return lax.dot_general(x.astype(BF16), w.astype(BF16), dims, preferred_element_type=F32)


@jax.custom_vjp
def bdot(x, w):
    return _dot(x, w, NN)


def _bdot_fwd(x, w):
    return _dot(x, w, NN), (x, w)


def _bdot_bwd(res, g):
    x, w = res
    return _dot(g, w, NT), _dot(x, g, TN)


bdot.defvjp(_bdot_fwd, _bdot_bwd)


def _split3(x):
    x1 = x.astype(BF16)
    r1 = x - x1.astype(F32)
    x2 = r1.astype(BF16)
    x3 = (r1 - x2.astype(F32)).astype(BF16)
    return x1, x2, x3


def _cdot(x, m, dims):
    return sum(lax.dot_general(p, m, dims, preferred_element_type=F32) for p in _split3(x))


def _block_dot(x, m, transposed):
    k_in = m.shape[1] if transposed else m.shape[0]
    dims = NT if transposed else NN
    return jnp.concatenate([_cdot(x[:, b * k_in:(b + 1) * k_in], m, dims) for b in range(x.shape[1] // k_in)], axis=1)


@jax.custom_vjp
def cdot(x, m):
    return _block_dot(x, m, False)


def _cdot_fwd(x, m):
    return _block_dot(x, m, False), m


def _cdot_bwd(m, g):
    return _block_dot(g, m, True), jnp.zeros_like(m)


cdot.defvjp(_cdot_fwd, _cdot_bwd)


@jax.custom_vjp
def tile_heads(y):
    return jnp.concatenate([y] * N_HEADS, axis=1)


def _tile_heads_fwd(y):
    return jnp.concatenate([y] * N_HEADS, axis=1), None


def _tile_heads_bwd(_, g):
    return (sum(g[:, h * HP:(h + 1) * HP] for h in range(N_HEADS)),)


tile_heads.defvjp(_tile_heads_fwd, _tile_heads_bwd)


def _shift_rows_impl(x, s, reverse):
    n = x.shape[0]
    idx = lax.broadcasted_iota(jnp.int32, x.shape, 0)
    if reverse:
        return jnp.where(idx < n - s, pltpu.roll(x, n - s, 0), 0.0)
    return jnp.where(idx >= s, pltpu.roll(x, s, 0), 0.0)


@functools.partial(jax.custom_vjp, nondiff_argnums=(1,))
def shift_rows(x, s):
    return _shift_rows_impl(x, s, False)


def _shift_rows_fwd(x, s):
    return _shift_rows_impl(x, s, False), None


def _shift_rows_bwd(s, _, g):
    return (_shift_rows_impl(g, s, True),)


shift_rows.defvjp(_shift_rows_fwd, _shift_rows_bwd)


def _rms(x, g):
    return x * lax.rsqrt(jnp.mean(x * x, axis=-1, keepdims=True) + NORM_EPS) * g


def _head_rms(x, e_mat, inv_n, g):
    ms = cdot(x * x, e_mat) * inv_n
    return x * lax.rsqrt(ms + NORM_EPS) * g


def _rope(y, p_mat, cos, sin):
    return y * cos + cdot(y, p_mat) * sin


def _constants():
    e = np.ones((HP, HP), np.float32)
    p = np.zeros((HP, HP), np.float32)
    r = np.zeros((LANES, HP), np.float32)
    pl64 = np.zeros((2 * HEAD_V, 2 * HP), np.float32)
    half = MLA_ROPE_DIM // 2
    for i in range(half):
        x1, x2 = MLA_NOPE_DIM + i, MLA_NOPE_DIM + half + i
        p[x2, x1] = -1.0
        p[x1, x2] = 1.0
    for i in range(MLA_ROPE_DIM):
        r[i, MLA_NOPE_DIM + i] = 1.0
    for h in range(2):
        for i in range(HEAD_V):
            pl64[h * HEAD_V + i, h * HP + i] = 1.0
    return tuple(jnp.asarray(a, BF16) for a in (e, p, r, pl64))


def _inv_freq_row():
    inv = ROPE_THETA ** (-jnp.arange(0, MLA_ROPE_DIM, 2, dtype=F32) / MLA_ROPE_DIM)
    head = jnp.concatenate([jnp.zeros((MLA_NOPE_DIM,), F32), inv, inv, jnp.zeros((HP - MLA_QK_DIM,), F32)])
    return jnp.tile(head, N_HEADS).reshape(1, QKW)


def _mixer_pre(c_q, c_kv, kr, fq, fk, qlg, w_uq, kvlg, w_uk, w_uv, mqg, mkg, fqg, fkg, cos, sin, e_mat, p_mat, r_mat, pl_mat):
    q = bdot(_rms(c_q, qlg), w_uq)
    ckvn = _rms(c_kv, kvlg)
    k = bdot(ckvn, w_uk) + tile_heads(cdot(kr, r_mat))
    v = bdot(ckvn, w_uv)
    q = _rope(_head_rms(q, e_mat, 1.0 / MLA_QK_DIM, mqg), p_mat, cos, sin)
    k = _rope(_head_rms(k, e_mat, 1.0 / MLA_QK_DIM, mkg), p_mat, cos, sin)
    qf = _head_rms(cdot(fq, pl_mat), e_mat, 1.0 / HEAD_V, fqg)
    kf = _head_rms(cdot(fk, pl_mat), e_mat, 1.0 / HEAD_V, fkg)
    return q, k, v, qf, kf


def _s5_post(state, u, c_blk, d_row, w_glu, b_glu):
    y = bdot(state, c_blk) + d_row * u
    y = jax.nn.gelu(y)
    return y * jax.nn.sigmoid(bdot(y, w_glu) + b_glu)


def _merge(o_a, o_b, o_c, g_a, g_b, g_c, w_a, w_b, w_c):
    return (jax.nn.sigmoid(g_a) * bdot(o_a, w_a) + jax.nn.sigmoid(g_b) * bdot(o_b, w_b)
            + jax.nn.sigmoid(g_c) * bdot(o_c, w_c))


def _ffn_mid(gate, val, wg0, wg1, wg2, wv0, wv1, wv2):
    cg = wg0 * shift_rows(gate, 2) + wg1 * shift_rows(gate, 1) + wg2 * gate
    cv = wv0 * shift_rows(val, 2) + wv1 * shift_rows(val, 1) + wv2 * val
    return jax.nn.silu(cg) * cv


def _s5_params(lam_re, lam_im, log_step, b_re, b_im):
    step = jnp.exp(log_step)
    zr, zi = lam_re * step, lam_im * step
    mag = jnp.exp(zr)
    lr, li = mag * jnp.cos(zi), mag * jnp.sin(zi)
    nr, ni = lr - 1.0, li
    den = lam_re * lam_re + lam_im * lam_im
    cr = (nr * lam_re + ni * lam_im) / den
    ci = (ni * lam_re - nr * lam_im) / den
    return lr, li, cr * b_re - ci * b_im, cr * b_im + ci * b_re


def _adamw(w, g, m, v):
    m = ADAM_B1 * m + (1.0 - ADAM_B1) * g
    v = ADAM_B2 * v + (1.0 - ADAM_B2) * (g * g)
    m_hat = m / (1.0 - ADAM_B1 ** ADAM_STEP)
    v_hat = v / (1.0 - ADAM_B2 ** ADAM_STEP)
    delta = -ADAM_LR * (m_hat / (jnp.sqrt(v_hat) + ADAM_EPS) + ADAM_WD * w)
    return delta, m, v


SCAN_W = 256


def _scan_seq(xr, xi, ar, ai, reverse):
    n = xr.shape[0]
    s = 1
    while s < n:
        sr, si = _shift_rows_impl(xr, s, reverse), _shift_rows_impl(xi, s, reverse)
        xr, xi = xr + ar * sr - ai * si, xi + ar * si + ai * sr
        ar, ai = ar * ar - ai * ai, 2.0 * ar * ai
        s *= 2
    return xr, xi


def _scan_fwd(bu, lbar, bsz, name, rider=None):
    t = bu.shape[0]
    seq = t // bsz
    nb = bu.shape[1] // (2 * SCAN_W)

    def fn(b, a):
        xr, xi = _scan_seq(b[:, :SCAN_W], b[:, SCAN_W:], a[:, :SCAN_W], a[:, SCAN_W:], False)
        return jnp.concatenate([xr, xi], axis=1)

    blk = (seq, 2 * SCAN_W)
    return _tile_call(fn, name, (bsz, nb),
                      [(bu, blk, lambda i, j: (i, j)), (lbar, (1, 2 * SCAN_W), lambda i, j: (0, j))],
                      [((t, bu.shape[1]), F32, blk, lambda i, j: (i, j), None)], rider=rider)


def _scan_bwd(dstate, state, lbar, bsz, name, rider=None):
    t = state.shape[0]
    seq = t // bsz
    nb = state.shape[1] // (2 * SCAN_W)

    def fn(g, x, a):
        ar, ai = a[:, :SCAN_W], a[:, SCAN_W:]
        gr, gi = _scan_seq(g[:, :SCAN_W], g[:, SCAN_W:], ar, -ai, True)
        pr, pi = _shift_rows_impl(x[:, :SCAN_W], 1, False), _shift_rows_impl(x[:, SCAN_W:], 1, False)
        dar = jnp.sum(gr * pr + gi * pi, axis=0, keepdims=True)
        dai = jnp.sum(gi * pr - gr * pi, axis=0, keepdims=True)
        return jnp.concatenate([gr, gi], axis=1), jnp.concatenate([dar, dai], axis=1)

    blk = (seq, 2 * SCAN_W)
    return _tile_call(fn, name, (nb, bsz),
                      [(dstate, blk, lambda j, i: (i, j)), (state, blk, lambda j, i: (i, j)),
                       (lbar, (1, 2 * SCAN_W), lambda j, i: (0, j))],
                      [((t, state.shape[1]), F32, blk, lambda j, i: (i, j), None),
                       ((1, state.shape[1]), F32, (1, 2 * SCAN_W), lambda j, i: (0, j), (1,))], rider=rider)


def _shift_lanes(x, s, reverse):
    n = x.shape[1]
    idx = lax.broadcasted_iota(jnp.int32, x.shape, 1)
    if reverse:
        return jnp.where(idx < n - s, pltpu.roll(x, n - s, 1), 0.0)
    return jnp.where(idx >= s, pltpu.roll(x, s, 1), 0.0)


def _cumsum_lanes(x, reverse):
    s = 1
    while s < x.shape[1]:
        x = x + _shift_lanes(x, s, reverse)
        s *= 2
    return x


def _forget_fwd(z, bias, name):
    def fn(zv, bv):
        x = zv + bv
        logf = jnp.minimum(x, 0.0) - jnp.log(1.0 + jnp.exp(-jnp.abs(x)))
        return _cumsum_lanes(logf, False)

    return _tile_call(fn, name, (1,), [_par(z), _par(bias)],
                      [(z.shape, F32, z.shape, lambda i: (0, 0), None)])[0]


def _forget_bwd(dc, z, bias, name):
    def fn(dcv, zv, bv):
        x = zv + bv
        dlogf = _cumsum_lanes(dcv, True)
        dz = dlogf * jax.nn.sigmoid(-x)
        return dz, jnp.sum(dz, axis=1, keepdims=True)

    rows = z.shape[0]
    return _tile_call(fn, name, (1,), [_par(dc), _par(z), _par(bias)],
                      [(z.shape, F32, z.shape, lambda i: (0, 0), None),
                       ((rows, 1), F32, (rows, 1), lambda i: (0, 0), None)])


def _col(v2, x):
    idx = lax.broadcasted_iota(jnp.int32, v2.shape, 1)
    return jnp.sum(jnp.where(idx == x, v2, 0.0), axis=1, keepdims=True)


def _two_cols(c0, c1):
    idx = lax.broadcasted_iota(jnp.int32, (c0.shape[0], 2), 1)
    return jnp.where(idx == 0, c0, c1)


def _rows2(r0, r1):
    idx = lax.broadcasted_iota(jnp.int32, (2, r0.shape[1]), 0)
    return jnp.where(idx == 0, r0, r1)


def _attn_specs(seq, v_blk0, has_bias):
    qblk = pl.BlockSpec((None, TQ, 2 * HP), lambda b, p, i: (b, i, p))
    kblk = pl.BlockSpec((None, seq, 2 * HP), lambda b, p, i: (b, 0, p))
    vblk = pl.BlockSpec((None, seq, LANES), lambda b, p, i: (b, 0, v_blk0 + p))
    oblk = pl.BlockSpec((None, TQ, LANES), lambda b, p, i: (b, i, p))
    rowblk = pl.BlockSpec((None, None, 2, TQ), lambda b, p, i: (b, p, 0, i))
    colblk = pl.BlockSpec((None, None, seq, 2), lambda b, p, i: (b, p, 0, 0))
    return qblk, kblk, vblk, oblk, rowblk, colblk


def _attn_fwd(q, k, v, v_blk0, c_cols, c_rows, scale, name, rider=None):
    bsz, seq, _ = q.shape
    nq = seq // TQ
    has_bias = c_cols is not None
    n_in = 5 if has_bias else 3
    nc = TQ // QC

    def body(*refs):
        q_ref, k_ref, v_ref = refs[:3]
        cq_ref, ck_ref = refs[3:5] if has_bias else (None, None)
        o_ref, lse_ref = refs[n_in:]
        qi = pl.program_id(2)
        qb = [[q_ref[c * QC:(c + 1) * QC, x * HP:(x + 1) * HP].astype(BF16) for c in range(nc)] for x in range(2)]
        key_in = lax.broadcasted_iota(jnp.int32, (TQ, QC), 0)
        qry_in = lax.broadcasted_iota(jnp.int32, (TQ, QC), 1)

        def block(j, carry, masked):
            ks = pl.multiple_of(j * TQ, TQ)
            vt = v_ref[pl.ds(ks, TQ), :].T.astype(BF16)
            new = []
            for x in range(2):
                kx = k_ref[pl.ds(ks, TQ), x * HP:(x + 1) * HP].astype(BF16)
                ckx = _col(ck_ref[pl.ds(ks, TQ), :], x) if has_bias else None
                for c in range(nc):
                    m, l, acc = carry[x * nc + c]
                    s = lax.dot_general(kx, qb[x][c], NT, preferred_element_type=F32) * scale
                    if has_bias:
                        s = s + (cq_ref[x:x + 1, c * QC:(c + 1) * QC] - ckx)
                    if masked:
                        s = jnp.where(qry_in + c * QC >= key_in, s, NEG_INF)
                    m_new = jnp.maximum(m, jnp.max(s, axis=0, keepdims=True))
                    alpha = jnp.exp(m - m_new)
                    p = jnp.exp(s - m_new)
                    l = alpha * l + jnp.sum(p, axis=0, keepdims=True)
                    acc = alpha * acc + lax.dot_general(vt, p.astype(BF16), NN, preferred_element_type=F32)
                    new.append((m_new, l, acc))
            return tuple(new)

        init = tuple((jnp.full((1, QC), NEG_INF, F32), jnp.zeros((1, QC), F32), jnp.zeros((LANES, QC), F32))
                     for _ in range(2 * nc))
        carry = lax.fori_loop(0, qi, lambda j, cr: block(j, cr, False), init)
        carry = block(qi, carry, True)
        lane = lax.broadcasted_iota(jnp.int32, (QC, LANES), 1)
        lse_rows = []
        for x in range(2):
            lse_rows.append(jnp.concatenate([carry[x * nc + c][0] + jnp.log(carry[x * nc + c][1]) for c in range(nc)],
                                            axis=1))
        for c in range(nc):
            (_, l0, a0), (_, l1, a1) = carry[c], carry[nc + c]
            o_ref[c * QC:(c + 1) * QC, :] = jnp.where(lane < HEAD_V, (a0 / l0).T, (a1 / l1).T)
        lse_ref[...] = _rows2(lse_rows[0], lse_rows[1])

    qblk, kblk, vblk, oblk, rowblk, colblk = _attn_specs(seq, v_blk0, has_bias)
    in_specs, args = [qblk, kblk, vblk], [q, k, v]
    if has_bias:
        in_specs += [rowblk, colblk]
        args += [c_rows, c_cols]
    return _pcall(body, name=name, grid=(bsz, 2, nq), in_specs=in_specs, out_specs=[oblk, rowblk],
                  out_shape=[jax.ShapeDtypeStruct((bsz, seq, 2 * LANES), F32),
                             jax.ShapeDtypeStruct((bsz, 2, 2, seq), F32)], rider=rider)(*args)


def _attn_bwd(q, k, v, v_blk0, o, do, lse, c_cols, c_rows, scale, name, rider=None):
    bsz, seq, _ = q.shape
    nq = seq // TQ
    has_bias = c_cols is not None
    n_in = 8 if has_bias else 6
    nc = TQ // QC

    def body(*refs):
        q_ref, k_ref, v_ref, o_ref, do_ref, lse_ref = refs[:6]
        cq_ref, ck_ref = refs[6:8] if has_bias else (None, None)
        dq_ref, dk_ref, dv_ref = refs[n_in:n_in + 3]
        dck_ref, dcq_ref = refs[n_in + 3:n_in + 5] if has_bias else (None, None)
        qi = pl.program_id(2)

        @pl.when(qi == 0)
        def _():
            dk_ref[...] = jnp.zeros_like(dk_ref)
            dv_ref[...] = jnp.zeros_like(dv_ref)
            if has_bias:
                dck_ref[...] = jnp.zeros_like(dck_ref)

        lane = lax.broadcasted_iota(jnp.int32, (QC, LANES), 1)
        ones8 = jnp.ones((8, LANES), BF16)
        qb, dob, delta, lse_r = [], [], [], []
        for x in range(2):
            hm = jnp.logical_and(lane >= x * HEAD_V, lane < (x + 1) * HEAD_V)
            qb.append([]), dob.append([]), delta.append([]), lse_r.append([])
            for c in range(nc):
                rows = slice(c * QC, (c + 1) * QC)
                do_c = jnp.where(hm, do_ref[rows, :], 0.0)
                prod = do_c * o_ref[rows, :]
                hi = prod.astype(BF16)
                lo = (prod - hi.astype(F32)).astype(BF16)
                d8 = (lax.dot_general(ones8, hi, NT, preferred_element_type=F32)
                      + lax.dot_general(ones8, lo, NT, preferred_element_type=F32))
                delta[x].append(d8[0:1, :])
                dob[x].append(do_c.astype(BF16))
                qb[x].append(q_ref[rows, x * HP:(x + 1) * HP].astype(BF16))
                lse_r[x].append(lse_ref[x:x + 1, rows])
        key_in = lax.broadcasted_iota(jnp.int32, (TQ, QC), 0)
        qry_in = lax.broadcasted_iota(jnp.int32, (TQ, QC), 1)

        def block(j, carry, masked):
            ks = pl.multiple_of(j * TQ, TQ)
            vb = v_ref[pl.ds(ks, TQ), :].astype(BF16)
            new, key_sums = [], []
            for x in range(2):
                k32 = k_ref[pl.ds(ks, TQ), x * HP:(x + 1) * HP]
                kx, kt = k32.astype(BF16), k32.T.astype(BF16)
                ckx = _col(ck_ref[pl.ds(ks, TQ), :], x) if has_bias else None
                dk_acc = jnp.zeros((TQ, HP), F32)
                dv_acc = jnp.zeros((TQ, LANES), F32)
                key_sum = jnp.zeros((TQ, 1), F32)
                for c in range(nc):
                    dqt, dcq = carry[x * nc + c]
                    s = lax.dot_general(kx, qb[x][c], NT, preferred_element_type=F32) * scale
                    if has_bias:
                        s = s + (cq_ref[x:x + 1, c * QC:(c + 1) * QC] - ckx)
                    if masked:
                        s = jnp.where(qry_in + c * QC >= key_in, s, NEG_INF)
                    p = jnp.exp(s - lse_r[x][c])
                    dp = lax.dot_general(vb, dob[x][c], NT, preferred_element_type=F32)
                    ds = p * (dp - delta[x][c])
                    dsb = ds.astype(BF16)
                    dk_acc = dk_acc + lax.dot_general(dsb, qb[x][c], NN, preferred_element_type=F32)
                    dv_acc = dv_acc + lax.dot_general(p.astype(BF16), dob[x][c], NN, preferred_element_type=F32)
                    dqt = dqt + lax.dot_general(kt, dsb, NN, preferred_element_type=F32)
                    if has_bias:
                        dcq = dcq + jnp.sum(ds, axis=0, keepdims=True)
                        key_sum = key_sum + jnp.sum(ds, axis=1, keepdims=True)
                    new.append((dqt, dcq))
                dk_ref[pl.ds(ks, TQ), x * HP:(x + 1) * HP] += dk_acc * scale
                dv_ref[pl.ds(ks, TQ), :] += dv_acc
                key_sums.append(key_sum)
            if has_bias:
                dck_ref[pl.ds(ks, TQ), :] -= _two_cols(key_sums[0], key_sums[1])
            return tuple(new)

        init = tuple((jnp.zeros((HP, QC), F32), jnp.zeros((1, QC), F32)) for _ in range(2 * nc))
        carry = lax.fori_loop(0, qi, lambda j, cr: block(j, cr, False), init)
        carry = block(qi, carry, True)
        for x in range(2):
            for c in range(nc):
                dq_ref[c * QC:(c + 1) * QC, x * HP:(x + 1) * HP] = carry[x * nc + c][0].T * scale
        if has_bias:
            dcq_ref[...] = _rows2(jnp.concatenate([carry[c][1] for c in range(nc)], axis=1),
                                  jnp.concatenate([carry[nc + c][1] for c in range(nc)], axis=1))

    qblk, kblk, vblk, oblk, rowblk, colblk = _attn_specs(seq, v_blk0, has_bias)
    in_specs = [qblk, kblk, vblk, oblk, oblk, rowblk]
    out_specs = [qblk, kblk, pl.BlockSpec((None, seq, LANES), lambda b, p, i: (b, 0, p))]
    out_shape = [jax.ShapeDtypeStruct((bsz, seq, QKW), F32), jax.ShapeDtypeStruct((bsz, seq, QKW), F32),
                 jax.ShapeDtypeStruct((bsz, seq, 2 * LANES), F32)]
    args = [q, k, v, o, do, lse]
    if has_bias:
        in_specs += [rowblk, colblk]
        out_specs += [colblk, rowblk]
        out_shape += [jax.ShapeDtypeStruct((bsz, 2, seq, 2), F32), jax.ShapeDtypeStruct((bsz, 2, 2, seq), F32)]
        args += [c_rows, c_cols]
    return _pcall(body, name=name, grid=(bsz, 2, nq), in_specs=in_specs, out_specs=out_specs,
                  out_shape=out_shape, rider=rider)(*args)


def _coords(idx):
    return (idx // 4, (idx // 2) % 2, idx % 2)


class _Xfer:
    def __init__(self, srcs, out_shapes, plan):
        self.srcs, self.out_shapes, self.plan = list(srcs), list(out_shapes), plan
        self.n = len(self.srcs)
        me0 = jnp.int32(0)
        n_pieces = [len(plan(a, me0, me0)) for a in range(self.n)]
        self.offs = np.concatenate([[0], np.cumsum(n_pieces)]).astype(int)
        self.total = int(self.offs[-1])

    def sems(self):
        remote = (N_DEV - 1) * self.total
        return [pltpu.SemaphoreType.DMA((remote,)), pltpu.SemaphoreType.DMA((remote,)),
                pltpu.SemaphoreType.DMA((self.total,))]

    def copies(self, src_refs, out_refs, send_sems, recv_sems, local_sems):
        me = 4 * lax.axis_index("x") + 2 * lax.axis_index("y") + lax.axis_index("c")
        out = []
        for a in range(self.n):
            for pi, (si, di) in enumerate(self.plan(a, me, me)):
                out.append((pltpu.make_async_copy(src_refs[a].at[si], out_refs[a].at[di],
                                                  local_sems.at[self.offs[a] + pi]), None))
        for kk in range(1, N_DEV):
            dest = (me + kk) % N_DEV
            src_dev = (me + N_DEV - kk) % N_DEV
            for a in range(self.n):
                pieces = self.plan(a, me, dest)
                landing = self.plan(a, src_dev, me)
                for pi, ((si, di), (_, li)) in enumerate(zip(pieces, landing)):
                    sem = (kk - 1) * self.total + self.offs[a] + pi
                    mk = functools.partial(pltpu.make_async_remote_copy, src_ref=src_refs[a].at[si],
                                           send_sem=send_sems.at[sem], recv_sem=recv_sems.at[sem],
                                           device_id=_coords(dest), device_id_type=pl.DeviceIdType.MESH)
                    out.append((mk(dst_ref=out_refs[a].at[di]), mk(dst_ref=out_refs[a].at[li])))
        return out

    @staticmethod
    def start(copies):
        for cp, _ in copies:
            cp.start()

    @staticmethod
    def wait(copies):
        for cp, rc in copies:
            if rc is None:
                cp.wait()
            else:
                cp.wait_send()
                rc.wait_recv()


def _exchange(name, xf):
    n = xf.n

    def body(*refs):
        copies = xf.copies(refs[:n], refs[n:2 * n], *refs[2 * n:])
        xf.start(copies)
        xf.wait(copies)

    any_spec = pl.BlockSpec(memory_space=pl.ANY)
    return pl.pallas_call(body, name=name, in_specs=[any_spec] * n, out_specs=[any_spec] * n,
                          out_shape=[jax.ShapeDtypeStruct(s, d) for s, d in xf.out_shapes],
                          scratch_shapes=xf.sems(),
                          compiler_params=pltpu.CompilerParams(has_side_effects=True))(*xf.srcs)


def _rows_of(j, n):
    return pl.ds(pl.multiple_of(j * n, n), n)


def _ffn_cols(j, half):
    return pl.multiple_of((2 * (j % 4) + half) * (2 * FF_BLK) + (j // 4) * FF_BLK, FF_BLK)


def _gather_plan(shards):
    names = list(shards)

    def plan(a, src_dev, _dest):
        nm = names[a]
        j = src_dev
        if nm in ("w_in", "w_uq", "w_ukv", "w_glu", "w_out", "w_down"):
            rows = shards[nm].shape[0]
            return [((slice(None), slice(None)), (_rows_of(j, rows), slice(None)))]
        if nm == "w_branch":
            return [((slice(None), slice(None), slice(None)), (slice(None), slice(None), _rows_of(j, LANES)))]
        if nm in ("w_up", "conv"):
            return [((slice(None), pl.ds(h * FF_BLK, FF_BLK)), (slice(None), pl.ds(_ffn_cols(j, h), FF_BLK)))
                    for h in range(2)]
        raise KeyError(nm)

    return names, plan


def _scatter_plan(names, shard_shapes):
    def plan(a, src_dev, dest):
        nm = names[a]
        j = dest
        if nm in ("w_in", "w_uq", "w_ukv", "w_glu", "w_out", "w_down"):
            rows = shard_shapes[nm][0]
            return [((_rows_of(j, rows), slice(None)), (src_dev, slice(None), slice(None)))]
        if nm == "w_branch":
            return [((slice(None), slice(None), _rows_of(j, LANES)), (src_dev, slice(None), slice(None), slice(None)))]
        if nm in ("w_up", "conv"):
            return [((slice(None), pl.ds(_ffn_cols(j, h), FF_BLK)), (src_dev, slice(None), pl.ds(h * FF_BLK, FF_BLK)))
                    for h in range(2)]
        if nm == "small":
            return [((slice(None), slice(None)), (src_dev, slice(None), slice(None)))]
        raise KeyError(nm)

    return plan


def _pad_in_cols(w):
    lead = w.shape[:-1]
    parts, pos = [], 0
    for dst, src, width in IN_SEGS:
        if dst > pos:
            parts.append(jnp.zeros(lead + (dst - pos,), w.dtype))
        parts.append(w[..., src:src + width])
        pos = dst + width
    return jnp.concatenate(parts, axis=-1)


def _unpad_in_cols(w):
    order = sorted(IN_SEGS, key=lambda s: s[1])
    return jnp.concatenate([w[..., dst:dst + width] for dst, _, width in order], axis=-1)


def _pad_ff(w, axis):
    n = w.shape[axis] // FF_HALF
    parts = []
    for h in range(n):
        piece = lax.slice_in_dim(w, h * FF_HALF, (h + 1) * FF_HALF, axis=axis)
        zshape = list(w.shape)
        zshape[axis] = FF_BLK - FF_HALF
        parts += [piece, jnp.zeros(zshape, w.dtype)]
    return jnp.concatenate(parts, axis=axis)


def _unpad_ff(w, axis):
    n = w.shape[axis] // FF_BLK
    return jnp.concatenate([lax.slice_in_dim(w, h * FF_BLK, h * FF_BLK + FF_HALF, axis=axis) for h in range(n)],
                           axis=axis)


def _to_scan_cols(re_part, im_part):
    lead = re_part.shape[:-1]
    nb = S5_N // SCAN_W
    r = re_part.reshape(lead + (nb, SCAN_W))
    i = im_part.reshape(lead + (nb, SCAN_W))
    return jnp.concatenate([r, i], axis=-1).reshape(lead + (2 * S5_N,))


def _from_scan_cols(x):
    lead = x.shape[:-1]
    nb = S5_N // SCAN_W
    y = x.reshape(lead + (nb, 2, SCAN_W))
    return y[..., 0, :].reshape(lead + (S5_N,)), y[..., 1, :].reshape(lead + (S5_N,))


def _block_diag_in(b):
    eye = jnp.eye(S5_GROUPS, dtype=b.dtype)
    return jnp.einsum("gph,gk->ghkp", b, eye).reshape(S5_WIDTH, S5_N)


def _block_diag_in_t(m):
    eye = jnp.eye(S5_GROUPS, dtype=m.dtype)
    return jnp.einsum("ghkp,gk->gph", m.reshape(S5_GROUPS, S5_GROUP_CH, S5_GROUPS, S5_STATE), eye)


def _block_diag_out(c):
    eye = jnp.eye(S5_GROUPS, dtype=c.dtype)
    return jnp.einsum("ghp,gk->gpkh", c, eye).reshape(S5_N, S5_WIDTH)


def _block_diag_out_t(m):
    eye = jnp.eye(S5_GROUPS, dtype=m.dtype)
    return jnp.einsum("gpkh,gk->ghp", m.reshape(S5_GROUPS, S5_STATE, S5_GROUPS, S5_GROUP_CH), eye)


def _pad_heads(g, width):
    g = jnp.broadcast_to(g.reshape(-1, width), (N_HEADS, width))
    return jnp.pad(g, ((0, 0), (0, HP - width))).reshape(1, QKW)


def _prep_layer(l, gw, small, consts):
    e_mat, p_mat, r_mat, pl_mat = consts
    p = {}
    p["attn_g"] = small["attn_norm_g"][l].reshape(1, D_MODEL)
    p["ffn_g"] = small["ffn_norm_g"][l].reshape(1, D_MODEL)
    p["qlg"] = small["q_lat_norm_g"][l].reshape(1, MLA_Q_RANK)
    p["kvlg"] = small["kv_lat_norm_g"][l].reshape(1, MLA_KV_RANK)
    p["mqg"] = _pad_heads(small["mla_q_norm_g"][l], MLA_QK_DIM)
    p["mkg"] = _pad_heads(small["mla_k_norm_g"][l], MLA_QK_DIM)
    p["fqg"] = _pad_heads(small["fox_q_norm_g"][l], HEAD_V)
    p["fkg"] = _pad_heads(small["fox_k_norm_g"][l], HEAD_V)
    w_uq = gw["w_uq"].reshape(MLA_Q_RANK, N_HEADS, MLA_QK_DIM)
    p["w_uq"] = jnp.pad(w_uq, ((0, 0), (0, 0), (0, HP - MLA_QK_DIM))).reshape(MLA_Q_RANK, QKW)
    w_ukv = gw["w_ukv"].reshape(MLA_KV_RANK, N_HEADS, MLA_NOPE_DIM + HEAD_V)
    p["w_uk"] = jnp.pad(w_ukv[..., :MLA_NOPE_DIM], ((0, 0), (0, 0), (0, HP - MLA_NOPE_DIM))).reshape(MLA_KV_RANK, QKW)
    p["w_uv"] = w_ukv[..., MLA_NOPE_DIM:].reshape(MLA_KV_RANK, N_HEADS * HEAD_V)
    p["w_glu"] = gw["w_glu"]
    p["b_glu"] = small["s5_b_glu"][l].reshape(1, S5_WIDTH)
    p["d_row"] = small["s5_d"][l].reshape(1, S5_WIDTH)
    p["lam_re"] = small["s5_lambda_re"][l].reshape(S5_N, 1)
    p["lam_im"] = small["s5_lambda_im"][l].reshape(S5_N, 1)
    p["log_step"] = jnp.repeat(small["s5_log_step"][l], S5_STATE).reshape(S5_N, 1)
    p["b_re"] = small["s5_b_re"][l].reshape(S5_N, S5_GROUP_CH)
    p["b_im"] = small["s5_b_im"][l].reshape(S5_N, S5_GROUP_CH)
    c_re, c_im = small["s5_c_re"][l], small["s5_c_im"][l]
    p["c_blk"] = _to_scan_cols(_block_diag_out(c_re).T, -_block_diag_out(c_im).T).T
    p["fbias"] = small["fox_f_bias"][l]
    for nm in ("w_in", "w_out", "w_branch"):
        p[nm] = gw[nm]
    return p


def _s5_param_call(p, l):
    ins = [p["lam_re"], p["lam_im"], p["log_step"], p["b_re"], p["b_im"]]
    outs = [((S5_N, 1), F32, (S5_N, 1), lambda i: (0, 0), None)] * 2 + \
           [((S5_N, S5_GROUP_CH), F32, (S5_N, S5_GROUP_CH), lambda i: (0, 0), None)] * 2
    return _tile_call(_s5_params, "s5_params", (1,), [_par(a) for a in ins], outs)


def _s5_param_bwd_call(p, cts):
    ins = [p["lam_re"], p["lam_im"], p["log_step"], p["b_re"], p["b_im"]]

    def fn(lr, li, ls, br, bi, g0, g1, g2, g3):
        _, vjp = jax.vjp(_s5_params, lr, li, ls, br, bi)
        return vjp((g0, g1, g2, g3))

    outs = [((S5_N, 1), F32, (S5_N, 1), lambda i: (0, 0), None)] * 3 + \
           [((S5_N, S5_GROUP_CH), F32, (S5_N, S5_GROUP_CH), lambda i: (0, 0), None)] * 2
    return _tile_call(fn, "s5_params_bwd", (1,), [_par(a) for a in ins + list(cts)], outs)


def _layer_fwd(x_prev, f_prev, p, tabs, consts, bsz, riders):
    arrived = {}

    def rider_of(name):
        return riders[name][0] if name in riders else None

    def hosted(name, res, n_out):
        if name in riders:
            arrived.update(zip(riders[name][1], res[n_out:]))
        return res[:n_out]

    t = x_prev.shape[0]
    seq = t // bsz
    nt = t // TM
    cos, sin = tabs
    e_mat, p_mat, r_mat, pl_mat = consts
    sv = {}

    if f_prev is None:
        x = x_prev
        h = _tile_call(lambda xv, g: _rms(xv, g), "norm_first", (nt,), [_row(x), _par(p["attn_g"])],
                       [_orow(t, D_MODEL, BF16)])[0]
    else:
        x, h = _tile_call(lambda xv, fv, g: (xv + fv, _rms(xv + fv, g)), "norm_attn", (nt,),
                          [_row(x_prev), _row(f_prev), _par(p["attn_g"])], [_orow(t, D_MODEL), _orow(t, D_MODEL, BF16)])
    sv["x"], sv["h"] = x, h
    proj = _mm(h, p["w_in"], "nn", "mm_in")
    sv["proj"] = proj

    def pre_fn(pa, cosv, sinv, qlg, w_uq, kvlg, w_uk, w_uv, mqg, mkg, fqg, fkg, em, pm, rm, plm):
        return _mixer_pre(pa[:, O_CQ:O_CQ + MLA_Q_RANK], pa[:, O_CKV:O_CKV + MLA_KV_RANK], pa[:, O_KR:O_KR + LANES],
                          pa[:, O_FQ:O_FQ + 256], pa[:, O_FK:O_FK + 256], qlg, w_uq, kvlg, w_uk, w_uv,
                          mqg, mkg, fqg, fkg, cosv, sinv, em, pm, rm, plm)

    pre_params = [p["qlg"], p["w_uq"], p["kvlg"], p["w_uk"], p["w_uv"], p["mqg"], p["mkg"], p["fqg"], p["fkg"],
                  e_mat, p_mat, r_mat, pl_mat]
    q_m, k_m, v_m, q_f, k_f = hosted("mixer_pre", _tile_call(
        pre_fn, "mixer_pre", (nt,), [_row(proj, PA), _row(cos), _row(sin)] + [_par(a) for a in pre_params],
        [_orow(t, QKW), _orow(t, QKW), _orow(t, 2 * LANES), _orow(t, QKW), _orow(t, QKW)],
        rider=rider_of("mixer_pre")), 5)
    sv.update(q_m=q_m, k_m=k_m, v_m=v_m, q_f=q_f, k_f=k_f)

    z = proj[:, O_FL:O_FL + N_HEADS].reshape(bsz, seq, N_HEADS).transpose(0, 2, 1).reshape(bsz * N_HEADS, seq)
    fb = jnp.tile(p["fbias"], bsz).reshape(bsz * N_HEADS, 1)
    cum = _forget_fwd(z, fb, "forget_fwd")
    ck = cum.reshape(bsz, 2, 2, seq)
    cq = ck.transpose(0, 1, 3, 2)
    sv.update(z=z, fb=fb, cq=cq, ck=ck)

    r3 = lambda a: a.reshape(bsz, seq, a.shape[-1])
    o_m, lse_m = hosted("attn_mla_fwd", _attn_fwd(r3(q_m), r3(k_m), r3(v_m), 0, None, None, MLA_QK_DIM ** -0.5,
                                                  "attn_mla_fwd", rider_of("attn_mla_fwd")), 2)
    o_f, lse_f = hosted("attn_fox_fwd", _attn_fwd(r3(q_f), r3(k_f), r3(proj), O_FV // LANES, cq, ck, HEAD_V ** -0.5,
                                                  "attn_fox_fwd", rider_of("attn_fox_fwd")), 2)
    o_m, o_f = o_m.reshape(t, 2 * LANES), o_f.reshape(t, 2 * LANES)
    sv.update(o_m=o_m, o_f=o_f, lse_m=lse_m, lse_f=lse_f)

    lbar_re, lbar_im, bbar_re, bbar_im = _s5_param_call(p, 0)
    lbar = _to_scan_cols(lbar_re.reshape(1, S5_N), lbar_im.reshape(1, S5_N))
    b_blk = _to_scan_cols(_block_diag_in(bbar_re.reshape(S5_GROUPS, S5_STATE, S5_GROUP_CH)),
                          _block_diag_in(bbar_im.reshape(S5_GROUPS, S5_STATE, S5_GROUP_CH)))
    sv.update(lbar=lbar, b_blk=b_blk)
    bu = _tile_call(lambda u, bb: bdot(u, bb), "s5_bu", (nt,), [_row(proj, S5_WIDTH, O_U // S5_WIDTH), _par(b_blk)],
                    [_orow(t, 2 * S5_N)])[0]
    state = hosted("s5_scan_fwd", _scan_fwd(bu, lbar, bsz, "s5_scan_fwd", rider_of("s5_scan_fwd")), 1)[0]
    sv["state"] = state
    post_params = [p["c_blk"], p["d_row"], p["w_glu"], p["b_glu"]]
    o_s = _tile_call(_s5_post, "s5_post", (nt,),
                     [_row(state), _row(proj, S5_WIDTH, O_U // S5_WIDTH)] + [_par(a) for a in post_params],
                     [_orow(t, S5_WIDTH)])[0]
    sv["o_s"] = o_s

    def merge_fn(oa, ob, oc, ga, gb, gc, wb):
        return _merge(oa, ob, oc, ga, gb, gc, wb[0], wb[1], wb[2])

    gate_specs = [_row(proj, D_MODEL, PA // D_MODEL + n) for n in range(3)]
    merged = _tile_call(merge_fn, "merge", (nt,), [_row(o_m), _row(o_f), _row(o_s)] + gate_specs + [_par(p["w_branch"])],
                        [_orow(t, D_MODEL, BF16)])[0]
    sv["merged"] = merged
    attn_out = _mm(merged, p["w_out"], "nn", "mm_out")

    x1, h2 = _tile_call(lambda xv, av, g: (xv + av, _rms(xv + av, g)), "norm_ffn", (nt,),
                        [_row(x), _row(attn_out), _par(p["ffn_g"])], [_orow(t, D_MODEL), _orow(t, D_MODEL, BF16)])
    sv["x1"], sv["h2"] = x1, h2
    p.update({n: arrived.pop(n) for n in FFN_WEIGHTS})
    up = _mm(h2, p["w_up"], "nn", "mm_up")
    sv["up"] = up

    def ffn_fn(upv, cw):
        return _ffn_mid(upv[:, :FF_BLK], upv[:, FF_BLK:], cw[0:1, :FF_BLK], cw[1:2, :FF_BLK], cw[2:3, :FF_BLK],
                        cw[0:1, FF_BLK:], cw[1:2, FF_BLK:], cw[2:3, FF_BLK:])

    nblk = FF_PW // FF_BLK
    act = _tile_call(ffn_fn, "ffn_mid", (nblk, bsz),
                     [(up, (seq, 2 * FF_BLK), lambda j, b: (b, j)), (p["conv"], (3, 2 * FF_BLK), lambda j, b: (0, j))],
                     [((t, FF_PW), BF16, (seq, FF_BLK), lambda j, b: (b, j), None)])[0]
    sv["act"] = act
    ffn_out = _mm(act, p["w_down"], "nn", "mm_down")
    return x1, ffn_out, sv, arrived


def _layer_bwd(dx2, p, sv, tabs, consts, bsz, scatter, carry):
    lands, carry_lands = {}, {}

    def riding(host, g):
        names = BWD_RIDERS[host]
        return scatter({n: g[n] for n in names}), names

    def landed(host, res, n_out):
        lands.update(zip(BWD_RIDERS[host], res[n_out:]))
        return res[:n_out]

    t = dx2.shape[0]
    seq = t // bsz
    nt = t // TM
    cos, sin = tabs
    e_mat, p_mat, r_mat, pl_mat = consts
    g = {}

    g["w_down"] = _mm(sv["act"], dx2, "tn", "mm_down_dw", out_dtype=BF16)
    dact = _mm(dx2, p["w_down"], "nt", "mm_down_dx")

    def ffn_bwd_fn(upv, cw, da):
        args = (upv[:, :FF_BLK], upv[:, FF_BLK:], cw[0:1, :FF_BLK], cw[1:2, :FF_BLK], cw[2:3, :FF_BLK],
                cw[0:1, FF_BLK:], cw[1:2, FF_BLK:], cw[2:3, FF_BLK:])
        _, vjp = jax.vjp(_ffn_mid, *args)
        dg, dv, g0, g1, g2, v0, v1, v2 = vjp(da)
        return (jnp.concatenate([dg, dv], axis=1), jnp.concatenate([g0, v0], axis=1),
                jnp.concatenate([g1, v1], axis=1), jnp.concatenate([g2, v2], axis=1))

    nblk = FF_PW // FF_BLK
    cw_out = ((1, UP_W), F32, (1, 2 * FF_BLK), lambda j, b: (0, j), (1,))
    res = _tile_call(
        ffn_bwd_fn, "ffn_mid_bwd", (nblk, bsz),
        [(sv["up"], (seq, 2 * FF_BLK), lambda j, b: (b, j)), (p["conv"], (3, 2 * FF_BLK), lambda j, b: (0, j)),
         (dact, (seq, FF_BLK), lambda j, b: (b, j))],
        [((t, UP_W), BF16, (seq, 2 * FF_BLK), lambda j, b: (b, j), None), cw_out, cw_out, cw_out],
        rider=scatter(carry) if carry else None)
    dup, dc0, dc1, dc2 = res[:4]
    carry_lands.update(zip(carry, res[4:]))
    g["conv"] = jnp.concatenate([dc0, dc1, dc2], axis=0)
    g["w_up"] = _mm(sv["h2"], dup, "tn", "mm_up_dw", out_dtype=BF16)
    dh2 = _mm(dup, p["w_up"], "nt", "mm_up_dx")

    def norm_bwd_fn(xv, gv, dh, dres):
        _, vjp = jax.vjp(_rms, xv, gv)
        dxv, dg = vjp(dh)
        return dres + dxv, dg

    d1, g["ffn_g"] = _tile_call(norm_bwd_fn, "norm_bwd", (nt,),
                                [_row(sv["x1"]), _par(p["ffn_g"]), _row(dh2), _row(dx2)],
                                [_orow(t, D_MODEL), _oacc((1, D_MODEL))])

    g["w_out"] = _mm(sv["merged"], d1, "tn", "mm_out_dw", out_dtype=BF16)
    dmerged = _mm(d1, p["w_out"], "nt", "mm_out_dx")

    def merge_bwd_fn(oa, ob, oc, ga, gb, gc, wb, dm):
        wf = wb.astype(F32)
        _, vjp = jax.vjp(_merge, oa, ob, oc, ga, gb, gc, wf[0], wf[1], wf[2])
        doa, dob, doc, dga, dgb, dgc, dwa, dwb, dwc = vjp(dm)
        return doa, dob, doc, jnp.concatenate([dga, dgb, dgc], axis=1), jnp.stack([dwa, dwb, dwc])

    proj = sv["proj"]
    gate_specs = [_row(proj, D_MODEL, PA // D_MODEL + n) for n in range(3)]
    do_m, do_f, do_s, dgl, g["w_branch"] = _tile_call(
        merge_bwd_fn, "merge_bwd", (nt,),
        [_row(sv["o_m"]), _row(sv["o_f"]), _row(sv["o_s"])] + gate_specs + [_par(p["w_branch"]), _row(dmerged)],
        [_orow(t, 2 * LANES), _orow(t, 2 * LANES), _orow(t, S5_WIDTH), _orow(t, 3 * D_MODEL),
         _oacc((3, S5_WIDTH, D_MODEL))])

    def post_bwd_fn(st, u, cb, dr, wg, bg, do):
        _, vjp = jax.vjp(_s5_post, st, u, cb.astype(F32), dr, wg.astype(F32), bg)
        return vjp(do)

    u_spec = _row(proj, S5_WIDTH, O_U // S5_WIDTH)
    dstate, du1, dc_blk, g_d, g["w_glu"], g_bglu = _tile_call(
        post_bwd_fn, "s5_post_bwd", (nt,),
        [_row(sv["state"]), u_spec, _par(p["c_blk"]), _par(p["d_row"]), _par(p["w_glu"]), _par(p["b_glu"]), _row(do_s)],
        [_orow(t, 2 * S5_N), _orow(t, S5_WIDTH), _oacc((2 * S5_N, S5_WIDTH)), _oacc((1, S5_WIDTH)),
         _oacc((S5_WIDTH, S5_WIDTH)), _oacc((1, S5_WIDTH))])
    dbu, dlbar = landed("s5_scan_bwd", _scan_bwd(dstate, sv["state"], sv["lbar"], bsz, "s5_scan_bwd",
                                                 riding("s5_scan_bwd", g)[0]), 2)

    def bu_bwd_fn(u, bb, dbv, du_in):
        _, vjp = jax.vjp(bdot, u, bb)
        du, dbb = vjp(dbv)
        return du_in + du, dbb

    du, db_blk = _tile_call(bu_bwd_fn, "s5_bu_bwd", (nt,), [u_spec, _par(sv["b_blk"]), _row(dbu), _row(du1)],
                            [_orow(t, S5_WIDTH), _oacc((S5_WIDTH, 2 * S5_N))])
    dlr, dli = _from_scan_cols(dlbar)
    dbr, dbi = _from_scan_cols(db_blk)
    cts = (dlr.reshape(S5_N, 1), dli.reshape(S5_N, 1),
           _block_diag_in_t(dbr).reshape(S5_N, S5_GROUP_CH), _block_diag_in_t(dbi).reshape(S5_N, S5_GROUP_CH))
    g_lr, g_li, g_ls, g_br, g_bi = _s5_param_bwd_call(p, cts)
    dc_re, dc_im = _from_scan_cols(dc_blk.T)
    g["s5_lambda_re"] = g_lr.reshape(S5_GROUPS, S5_STATE)
    g["s5_lambda_im"] = g_li.reshape(S5_GROUPS, S5_STATE)
    g["s5_log_step"] = jnp.sum(g_ls.reshape(S5_GROUPS, S5_STATE), axis=1)
    g["s5_b_re"] = g_br.reshape(S5_GROUPS, S5_STATE, S5_GROUP_CH)
    g["s5_b_im"] = g_bi.reshape(S5_GROUPS, S5_STATE, S5_GROUP_CH)
    g["s5_c_re"] = _block_diag_out_t(dc_re.T)
    g["s5_c_im"] = -_block_diag_out_t(dc_im.T)
    g["s5_d"] = g_d.reshape(S5_GROUPS, S5_GROUP_CH)
    g["s5_b_glu"] = g_bglu.reshape(S5_WIDTH)

    r3 = lambda a: a.reshape(bsz, seq, a.shape[-1])
    dq_m, dk_m, dv_m = landed("attn_mla_bwd", _attn_bwd(
        r3(sv["q_m"]), r3(sv["k_m"]), r3(sv["v_m"]), 0, r3(sv["o_m"]), r3(do_m), sv["lse_m"], None, None,
        MLA_QK_DIM ** -0.5, "attn_mla_bwd", riding("attn_mla_bwd", g)[0]), 3)
    dq_f, dk_f, dv_f, dcq, dck = landed("attn_fox_bwd", _attn_bwd(
        r3(sv["q_f"]), r3(sv["k_f"]), r3(proj), O_FV // LANES, r3(sv["o_f"]), r3(do_f), sv["lse_f"], sv["cq"], sv["ck"],
        HEAD_V ** -0.5, "attn_fox_bwd", riding("attn_fox_bwd", g)[0]), 5)
    dcum = (dck + dcq.transpose(0, 1, 3, 2)).reshape(bsz * N_HEADS, seq)
    dz, dfb = _forget_bwd(dcum, sv["z"], sv["fb"], "forget_bwd")
    g["fox_f_bias"] = jnp.sum(dfb.reshape(bsz, N_HEADS), axis=0)
    dfl = jnp.pad(dz.reshape(bsz, N_HEADS, seq).transpose(0, 2, 1).reshape(t, N_HEADS), ((0, 0), (0, LANES - N_HEADS)))

    def pre_bwd_fn(pa, cosv, sinv, qlg, w_uq, kvlg, w_uk, w_uv, mqg, mkg, fqg, fkg, em, pm, rm, plm,
                   gq, gk, gv, gqf, gkf, gvf, gu, gfl, ggl):
        f = functools.partial(_mixer_pre, cos=cosv, sin=sinv, e_mat=em, p_mat=pm, r_mat=rm, pl_mat=plm)
        prim = (pa[:, O_CQ:O_CQ + MLA_Q_RANK], pa[:, O_CKV:O_CKV + MLA_KV_RANK], pa[:, O_KR:O_KR + LANES],
                pa[:, O_FQ:O_FQ + 256], pa[:, O_FK:O_FK + 256], qlg, w_uq.astype(F32), kvlg, w_uk.astype(F32),
                w_uv.astype(F32), mqg, mkg, fqg, fkg)
        _, vjp = jax.vjp(lambda *a: f(*a), *prim)
        dcq_, dckv, dkr, dfq, dfk, dqlg, dwuq, dkvlg, dwuk, dwuv, dmqg, dmkg, dfqg, dfkg = vjp((gq, gk, gv, gqf, gkf))
        zpad = jnp.zeros((pa.shape[0], PA - O_CQ - MLA_Q_RANK), F32)
        dproj = jnp.concatenate([dckv, dfq, dfk, gvf, gu, dkr, gfl, dcq_, zpad, ggl], axis=1)
        return dproj, dqlg, dwuq, dkvlg, dwuk, dwuv, dmqg, dmkg, dfqg, dfkg

    pre_params = [p["qlg"], p["w_uq"], p["kvlg"], p["w_uk"], p["w_uv"], p["mqg"], p["mkg"], p["fqg"], p["fkg"],
                  e_mat, p_mat, r_mat, pl_mat]
    cts_in = [dq_m.reshape(t, QKW), dk_m.reshape(t, QKW), dv_m.reshape(t, 2 * LANES), dq_f.reshape(t, QKW),
              dk_f.reshape(t, QKW), dv_f.reshape(t, 2 * LANES), du, dfl, dgl]
    (dproj, g_qlg, g_wuq, g_kvlg, g_wuk, g_wuv, g_mqg, g_mkg, g_fqg, g_fkg) = _tile_call(
        pre_bwd_fn, "mixer_pre_bwd", (nt,),
        [_row(proj, PA), _row(cos), _row(sin)] + [_par(a) for a in pre_params] + [_row(a) for a in cts_in],
        [_orow(t, PW, BF16), _oacc((1, MLA_Q_RANK)), _oacc((MLA_Q_RANK, QKW)), _oacc((1, MLA_KV_RANK)),
         _oacc((MLA_KV_RANK, QKW)), _oacc((MLA_KV_RANK, N_HEADS * HEAD_V)), _oacc((1, QKW)), _oacc((1, QKW)),
         _oacc((1, QKW)), _oacc((1, QKW))])
    g["q_lat_norm_g"] = g_qlg.reshape(MLA_Q_RANK)
    g["kv_lat_norm_g"] = g_kvlg.reshape(MLA_KV_RANK)
    heads = lambda a, w: jnp.sum(a.reshape(N_HEADS, HP)[:, :w], axis=0)
    g["mla_q_norm_g"], g["mla_k_norm_g"] = heads(g_mqg, MLA_QK_DIM), heads(g_mkg, MLA_QK_DIM)
    g["fox_q_norm_g"], g["fox_k_norm_g"] = heads(g_fqg, HEAD_V), heads(g_fkg, HEAD_V)
    g["w_uq"] = g_wuq.reshape(MLA_Q_RANK, N_HEADS, HP)[..., :MLA_QK_DIM].reshape(MLA_Q_RANK, N_HEADS * MLA_QK_DIM)
    g["w_ukv"] = jnp.concatenate([g_wuk.reshape(MLA_KV_RANK, N_HEADS, HP)[..., :MLA_NOPE_DIM],
                                  g_wuv.reshape(MLA_KV_RANK, N_HEADS, HEAD_V)], axis=-1).reshape(MLA_KV_RANK, QKW)

    g["w_in"] = _mm(sv["h"], dproj, "tn", "mm_in_dw", out_dtype=BF16)
    dh = landed("mm_in_dx", _mm(dproj, p["w_in"], "nt", "mm_in_dx", rider=riding("mm_in_dx", g)[0]), 1)[0]
    dx, g["attn_g"] = _tile_call(norm_bwd_fn, "norm_bwd", (nt,),
                                 [_row(sv["x"]), _par(p["attn_g"]), _row(dh), _row(d1)],
                                 [_orow(t, D_MODEL), _oacc((1, D_MODEL))])
    return dx, g, lands, carry_lands


def _row_tile(r, c, max_elems):
    if r * c <= max_elems:
        return r
    best = None
    for d in range(8, r, 8):
        if r % d == 0 and d * c <= max_elems:
            best = d
    assert best is not None, (r, c)
    return best


def _sum8(land, name):
    _, r, c = land.shape
    tr = _row_tile(r, N_DEV * c, 1 << 20)

    def fn(lv):
        acc = lv[0].astype(F32)
        for i in range(1, N_DEV):
            acc = acc + lv[i].astype(F32)
        return acc

    return _tile_call(fn, name, (r // tr,), [(land, (N_DEV, tr, c), lambda i: (0, i, 0))],
                      [((r, c), F32, (tr, c), lambda i: (i, 0), None)])[0]


def _adamw_layer(w, g, m, v, l, prev, name):
    _, r, c = w.shape
    tr = _row_tile(r, c, 1 << 19)
    lay = pl.BlockSpec((None, tr, c), lambda i: (l, i, 0))

    def body(w_ref, g_ref, m_ref, v_ref, *rest):
        g_out, d_out, m_out, v_out = rest[-4:]
        gv = g_ref[...]
        d_out[...], m_out[...], v_out[...] = _adamw(w_ref[...], gv, m_ref[...], v_ref[...])
        g_out[...] = gv

    in_specs = [lay, pl.BlockSpec((tr, c), lambda i: (i, 0)), lay, lay]
    args = [w, g, m, v]
    aliases = {}
    if prev is not None:
        in_specs += [pl.BlockSpec(memory_space=pl.ANY)] * 4
        args += list(prev)
        aliases = {4 + k: k for k in range(4)}
    return _pcall(body, name=name, grid=(r // tr,), in_specs=in_specs, out_specs=[lay] * 4,
                  out_shape=[jax.ShapeDtypeStruct(w.shape, F32)] * 4, aliases=aliases)(*args)


SMALL_NAMES = ("attn_norm_g", "q_lat_norm_g", "kv_lat_norm_g", "mla_q_norm_g", "mla_k_norm_g", "fox_q_norm_g",
               "fox_k_norm_g", "fox_f_bias", "s5_lambda_re", "s5_lambda_im", "s5_b_re", "s5_b_im", "s5_c_re", "s5_c_im",
               "s5_d", "s5_log_step", "s5_b_glu", "ffn_norm_g")
BIG_NAMES = ("w_in", "w_uq", "w_ukv", "s5_w_glu", "w_branch", "w_out", "w_up", "ffn_conv_w", "w_down")
ALL_NAMES = ("attn_norm_g", "w_in", "q_lat_norm_g", "w_uq", "kv_lat_norm_g", "w_ukv", "mla_q_norm_g", "mla_k_norm_g",
             "fox_q_norm_g", "fox_k_norm_g", "fox_f_bias", "s5_lambda_re", "s5_lambda_im", "s5_b_re", "s5_b_im",
             "s5_c_re", "s5_c_im", "s5_d", "s5_log_step", "s5_w_glu", "s5_b_glu", "w_branch", "w_out", "ffn_norm_g",
             "w_up", "ffn_conv_w", "w_down")


def _pack(arrs):
    tile = 8 * LANES
    parts = []
    for a in arrs:
        n = int(np.prod(a.shape))
        tot = -(-n // tile) * tile
        parts.append(jnp.pad(a.reshape(-1), (0, tot - n)).reshape(tot // LANES, LANES))
    rows = sum(a.shape[0] for a in parts)
    parts.append(jnp.zeros((-rows % 256, LANES), parts[0].dtype))
    return jnp.concatenate(parts, axis=0)


def _unpack(packed, like):
    tile = 8 * LANES
    out, pos = [], 0
    for a in like:
        n = int(np.prod(a.shape))
        rows = -(-n // tile) * 8
        out.append(packed[pos:pos + rows].reshape(-1)[:n].reshape(a.shape))
        pos += rows
    return out


def _rope_tables(positions, t):
    pos = positions.reshape(t, 1)
    inv = _inv_freq_row()

    def fn(pv, iv):
        ang = pv.astype(F32) * iv
        return jnp.cos(ang), jnp.sin(ang)

    return _tile_call(fn, "rope_tables", (t // TM,), [_row(pos), _par(inv)], [_orow(t, QKW), _orow(t, QKW)])


def _loss_call(x1, f, target):
    t = x1.shape[0]

    def fn(xv, fv, tv):
        e = xv + fv - tv
        per_tok = jnp.sum(e * e, axis=1, keepdims=True) * (1.0 / D_MODEL)
        return 0.5 * jnp.sum(per_tok, axis=0, keepdims=True), e * (1.0 / D_MODEL)

    return _tile_call(fn, "loss", (t // TM,), [_row(x1), _row(f), _row(target)], [_oacc((1, 1)), _orow(t, D_MODEL)])


def _local_shards(l, w):
    return {
        "w_in": _pad_in_cols(w["w_in"][l]).astype(BF16),
        "w_uq": w["w_uq"][l].reshape(-1, N_HEADS * MLA_QK_DIM).astype(BF16),
        "w_ukv": w["w_ukv"][l].reshape(-1, QKW).astype(BF16),
        "w_glu": w["s5_w_glu"][l].astype(BF16),
        "w_branch": w["w_branch"][l].astype(BF16),
        "w_out": w["w_out"][l].astype(BF16),
        "w_up": _pad_ff(w["w_up"][l], 1).astype(BF16),
        "conv": _pad_ff(w["ffn_conv_w"][l], 1),
        "w_down": _pad_ff(w["w_down"][l], 0).astype(BF16),
    }


GATHERED_SHAPES = {"w_in": (D_MODEL, PW), "w_uq": (MLA_Q_RANK, N_HEADS * MLA_QK_DIM), "w_ukv": (MLA_KV_RANK, QKW),
                   "w_glu": (S5_WIDTH, S5_WIDTH), "w_branch": (3, S5_WIDTH, D_MODEL), "w_out": (D_MODEL, D_MODEL),
                   "w_up": (D_MODEL, UP_W), "conv": (3, UP_W), "w_down": (FF_PW, D_MODEL)}
FFN_WEIGHTS = ("w_up", "conv", "w_down")
MIXER_WEIGHTS = ("w_in", "w_uq", "w_ukv", "w_glu", "w_out", "w_branch")
FWD_RIDERS_OWN = {"attn_mla_fwd": ("w_up", "conv"), "s5_scan_fwd": ("w_down",)}
FWD_RIDERS_NEXT = {"attn_fox_fwd": ("w_in", "w_uq", "w_ukv", "w_glu"), "mixer_pre": ("w_out", "w_branch")}
BWD_RIDERS = {"s5_scan_bwd": ("w_down",), "attn_mla_bwd": ("w_up",),
              "attn_fox_bwd": ("w_out", "w_branch", "w_glu", "conv"), "mm_in_dx": ("w_uq", "w_ukv")}
UPDATED_TRANSPOSED = ("w_up",)
WEIGHT_OF = {"w_in": "w_in", "w_uq": "w_uq", "w_ukv": "w_ukv", "w_glu": "s5_w_glu", "w_branch": "w_branch",
             "w_out": "w_out", "w_up": "w_up", "conv": "ffn_conv_w", "w_down": "w_down"}


def _unshard_layout(nm, g2):
    if nm == "w_in":
        return _unpad_in_cols(g2)
    if nm == "w_uq":
        return g2.reshape(-1, N_HEADS, MLA_QK_DIM)
    if nm == "w_ukv":
        return g2.reshape(-1, N_HEADS, MLA_NOPE_DIM + HEAD_V)
    if nm in ("w_up", "conv"):
        return _unpad_ff(g2, 1)
    if nm == "w_down":
        return _unpad_ff(g2, 0)
    return g2


def kernel(x, positions, attn_norm_g, w_in, q_lat_norm_g, w_uq, kv_lat_norm_g, w_ukv, mla_q_norm_g, mla_k_norm_g, fox_q_norm_g, fox_k_norm_g, fox_f_bias, s5_lambda_re, s5_lambda_im, s5_b_re, s5_b_im, s5_c_re, s5_c_im, s5_d, s5_log_step, s5_w_glu, s5_b_glu, w_branch, w_out, ffn_norm_g, w_up, ffn_conv_w, w_down, loss_target, m_attn_norm_g, m_w_in, m_q_lat_norm_g, m_w_uq, m_kv_lat_norm_g, m_w_ukv, m_mla_q_norm_g, m_mla_k_norm_g, m_fox_q_norm_g, m_fox_k_norm_g, m_fox_f_bias, m_s5_lambda_re, m_s5_lambda_im, m_s5_b_re, m_s5_b_im, m_s5_c_re, m_s5_c_im, m_s5_d, m_s5_log_step, m_s5_w_glu, m_s5_b_glu, m_w_branch, m_w_out, m_ffn_norm_g, m_w_up, m_ffn_conv_w, m_w_down, v_attn_norm_g, v_w_in, v_q_lat_norm_g, v_w_uq, v_kv_lat_norm_g, v_w_ukv, v_mla_q_norm_g, v_mla_k_norm_g, v_fox_q_norm_g, v_fox_k_norm_g, v_fox_f_bias, v_s5_lambda_re, v_s5_lambda_im, v_s5_b_re, v_s5_b_im, v_s5_c_re, v_s5_c_im, v_s5_d, v_s5_log_step, v_s5_w_glu, v_s5_b_glu, v_w_branch, v_w_out, v_ffn_norm_g, v_w_up, v_ffn_conv_w, v_w_down):
    args = locals()
    w = {n: args[n] for n in ALL_NAMES}
    m = {n: args["m_" + n] for n in ALL_NAMES}
    v = {n: args["v_" + n] for n in ALL_NAMES}
    depth = w_in.shape[0]
    bsz, seq, _ = x.shape
    t = bsz * seq
    consts = _constants()
    tabs = _rope_tables(positions, t)
    small = {n: w[n] for n in SMALL_NAMES}

    all_names = list(GATHERED_SHAPES)
    all_shards = [_local_shards(l, w) for l in range(depth)]

    def gather_xfer(l, names):
        sub = {n: all_shards[l][n] for n in names}
        order, plan = _gather_plan(sub)
        return _Xfer([sub[n] for n in order], [(GATHERED_SHAPES[n], sub[n].dtype) for n in order], plan)

    def scatter_xfer(g, names):
        shapes = {n: _local_shards_shape(n) for n in names}
        return _Xfer([g[n] for n in names], [((N_DEV,) + shapes[n], g[n].dtype) for n in names],
                     _scatter_plan(list(names), shapes))

    xs = x.reshape(t, D_MODEL)
    f_prev = None
    saved, params = [], []
    gathered = dict(zip(MIXER_WEIGHTS, _exchange("gather_weights", gather_xfer(0, MIXER_WEIGHTS))))
    for l in range(depth):
        p = _prep_layer(l, gathered, small, consts)
        riders = {host: (gather_xfer(l, names), names) for host, names in FWD_RIDERS_OWN.items()}
        if l + 1 < depth:
            riders.update({host: (gather_xfer(l + 1, names), names) for host, names in FWD_RIDERS_NEXT.items()})
        xs, f_prev, sv, gathered = _layer_fwd(xs, f_prev, p, tabs, consts, bsz, riders)
        saved.append(sv)
        params.append(p)

    loss_part, dy = _loss_call(xs, f_prev, loss_target.reshape(t, D_MODEL))
    loss = lax.psum(loss_part[0, 0], ("x", "y", "c"))

    small_grads = {n: [None] * depth for n in SMALL_NAMES}
    big = {n: None for n in BIG_NAMES}

    def finish(l, lands):
        for n in all_names:
            wn, ld = WEIGHT_OF[n], lands[n]
            ld2 = ld.reshape(N_DEV, -1, ld.shape[-1])
            gsum = _unshard_layout(n, _sum8(ld2, "sum_" + n).reshape(_local_shards_shape(n)))
            if n in UPDATED_TRANSPOSED:
                gsum = gsum.T
            c = gsum.shape[-1]
            three = lambda a: (jnp.swapaxes(a, 1, 2) if n in UPDATED_TRANSPOSED else a).reshape(depth, -1, c)
            big[wn] = _adamw_layer(three(w[wn]), gsum.reshape(-1, c), three(m[wn]), three(v[wn]), l, big[wn],
                                   "adamw_" + n)

    def scatter(grads):
        return scatter_xfer(grads, list(grads))

    dx = dy
    carry, lands_above = {}, None
    for l in reversed(range(depth)):
        dx, g, lands, carry_lands = _layer_bwd(dx, params[l], saved[l], tabs, consts, bsz, scatter, carry)
        if lands_above is not None:
            finish(l + 1, {**lands_above, **carry_lands})
        carry, lands_above = {"w_in": g["w_in"]}, lands
        small_grads["attn_norm_g"][l] = g["attn_g"].reshape(D_MODEL)
        small_grads["ffn_norm_g"][l] = g["ffn_g"].reshape(D_MODEL)
        for n in SMALL_NAMES:
            if n not in ("attn_norm_g", "ffn_norm_g"):
                small_grads[n][l] = g[n]
    finish(0, {**lands_above, "w_in": _exchange("scatter_grads", scatter(carry))[0]})

    sg = [jnp.stack(small_grads[n]).reshape(w[n].shape) for n in SMALL_NAMES]
    packed = _pack(sg)
    land = _exchange("gather_small_grads", _Xfer([packed], [((N_DEV,) + packed.shape, F32)],
                                                 _scatter_plan(["small"], {})))[0]
    gs = _sum8(land, "sum_small")
    one = lambda names_of: _pack([names_of[n] for n in SMALL_NAMES])[None]
    small_res = _adamw_layer(one(w), gs, one(m), one(v), 0, None, "adamw_small")
    like = [w[n] for n in SMALL_NAMES]
    small_out = [dict(zip(SMALL_NAMES, _unpack(a[0], like))) for a in small_res]

    def out_of(kind, n):
        if n in SMALL_NAMES:
            return small_out[kind][n]
        if n in UPDATED_TRANSPOSED:
            return jnp.swapaxes(big[n][kind], 1, 2)
        return big[n][kind].reshape(w[n].shape)

    outs = [loss, dx.reshape(bsz, seq, D_MODEL)]
    for kind in range(4):
        outs += [out_of(kind, n) for n in ALL_NAMES]
    return tuple(outs)


def _local_shards_shape(nm):
    return {"w_in": (D_MODEL // N_DEV, PW), "w_uq": (MLA_Q_RANK // N_DEV, N_HEADS * MLA_QK_DIM),
            "w_ukv": (MLA_KV_RANK // N_DEV, QKW), "w_glu": (S5_WIDTH // N_DEV, S5_WIDTH),
            "w_branch": (3, S5_WIDTH, LANES), "w_out": (D_MODEL // N_DEV, D_MODEL), "w_up": (D_MODEL, 2 * FF_BLK),
            "conv": (3, 2 * FF_BLK), "w_down": (FF_BLK, D_MODEL)}[nm]
```

```python
import functools
import math

import numpy as np
import jax
import jax.numpy as jnp
from jax import lax
from jax.experimental import pallas as pl
from jax.experimental.pallas import tpu as pltpu

F32, BF16 = jnp.float32, jnp.bfloat16

D_MODEL = 1024
N_DEV = 8
MLA_Q_RANK, MLA_KV_RANK, MLA_ROPE_DIM, MLA_NOPE_DIM, MLA_QK_DIM = 384, 256, 32, 64, 96
N_HEADS, HEAD_V = 4, 64
S5_GROUPS, S5_GROUP_CH, S5_STATE, S5_WIDTH = 16, 16, 64, 256
S5_N = S5_GROUPS * S5_STATE
D_FF = 2816
D_IN = 4772
ROPE_THETA = 10000.0
NORM_EPS = 1e-6
NEG_INF = -1e30
ADAM_LR, ADAM_B1, ADAM_B2, ADAM_EPS, ADAM_WD, ADAM_STEP = 0.001, 0.9, 0.999, 1e-08, 0.01, 10

VMEM_LIMIT_BYTES = 56 * 1024 * 1024
LANES = 128
TM = 256
TQ = 512
QF = 128
QC = 512
HP = 128
QKW = N_HEADS * HP

PW = 5120
PA = 2048
IN_SEGS = ((0, 384, 256),
           (256, 672, 256),
           (512, 928, 256),
           (768, 1184, 256),
           (1024, 1444, 256),
           (1280, 640, 32),
           (1408, 1440, 4),
           (1536, 0, 384),
           (2048, 1700, 3072))
O_CKV, O_FQ, O_FK, O_FV, O_U, O_KR, O_FL, O_CQ = 0, 256, 512, 768, 1024, 1280, 1408, 1536

FF_BLK = 384
FF_HALF = D_FF // 8
FF_PW = 8 * FF_BLK
UP_W = 2 * FF_PW

NN = (((1,), (0,)), ((), ()))
NT = (((1,), (1,)), ((), ()))
TN = (((0,), (0,)), ((), ()))


def _pcall(body, *, name, grid, in_specs, out_specs, out_shape, scratch_shapes=(), rider=None, aliases=None):
    params = pltpu.CompilerParams(dimension_semantics=("arbitrary",) * len(grid), vmem_limit_bytes=VMEM_LIMIT_BYTES)
    if rider is None:
        return pl.pallas_call(body, name=name, grid=grid, in_specs=in_specs, out_specs=out_specs, out_shape=out_shape,
                              scratch_shapes=list(scratch_shapes), input_output_aliases=aliases or {},
                              compiler_params=params)
    assert not aliases
    in_specs, out_specs, out_shape = list(in_specs), list(out_specs), list(out_shape)
    n_in, n_out, n_scr, nx = len(in_specs), len(out_specs), len(scratch_shapes), rider.n
    any_spec = pl.BlockSpec(memory_space=pl.ANY)

    def at_step(last):
        conds = [pl.program_id(a) == (g - 1 if last else 0) for a, g in enumerate(grid)]
        return functools.reduce(jnp.logical_and, conds)

    def body_with_rider(*refs):
        ins, xsrc = refs[:n_in], refs[n_in:n_in + nx]
        outs, xout = refs[n_in + nx:n_in + nx + n_out], refs[n_in + nx + n_out:n_in + 2 * nx + n_out]
        scr = refs[n_in + 2 * nx + n_out:n_in + 2 * nx + n_out + n_scr]
        sems = refs[n_in + 2 * nx + n_out + n_scr:]

        @pl.when(at_step(False))
        def _():
            rider.start(rider.copies(xsrc, xout, *sems))

        body(*ins, *outs, *scr)

        @pl.when(at_step(True))
        def _():
            rider.wait(rider.copies(xsrc, xout, *sems))

    call = pl.pallas_call(body_with_rider, name=name, grid=grid, in_specs=in_specs + [any_spec] * nx,
                          out_specs=out_specs + [any_spec] * nx,
                          out_shape=out_shape + [jax.ShapeDtypeStruct(s, d) for s, d in rider.out_shapes],
                          scratch_shapes=list(scratch_shapes) + rider.sems(), compiler_params=params)
    return lambda *args: call(*args, *rider.srcs)


def _tile_call(fn, name, grid, ins, outs, rider=None):
    n_in = len(ins)

    def body(*refs):
        res = fn(*[r[...] for r in refs[:n_in]])
        if not isinstance(res, (tuple, list)):
            res = (res,)
        assert len(res) == len(outs), (name, len(res), len(outs))
        for r, v, o in zip(refs[n_in:], res, outs):
            v = v.astype(r.dtype)
            if o[4] is None:
                r[...] = v
            else:
                first = functools.reduce(jnp.logical_and, [pl.program_id(a) == 0 for a in o[4]])

                @pl.when(first)
                def _():
                    r[...] = v

                @pl.when(jnp.logical_not(first))
                def _():
                    r[...] += v

    res = _pcall(body, name=name, grid=grid,
                 in_specs=[pl.BlockSpec(b, m) for _, b, m in ins],
                 out_specs=[pl.BlockSpec(o[2], o[3]) for o in outs],
                 out_shape=[jax.ShapeDtypeStruct(o[0], o[1]) for o in outs], rider=rider)(*[a for a, _, _ in ins])
    return res


def _row(a, width=None, col_block=0, tm=TM):
    w = a.shape[1] if width is None else width
    return (a, (tm, w), lambda i, c=col_block: (i, c))


def _par(a):
    nd = a.ndim
    return (a, a.shape, lambda i, nd=nd: (0,) * nd)


def _orow(t, w, dtype=F32, tm=TM):
    return ((t, w), dtype, (tm, w), lambda i: (i, 0), None)


def _oacc(shape):
    nd = len(shape)
    return (tuple(shape), F32, tuple(shape), lambda i, nd=nd: (0,) * nd, (0,))


def _pick(n, target):
    best = None
    for d in range(LANES, min(n, target) + 1, LANES):
        if n % d == 0:
            best = d
    return n if best is None else best


MM_VMEM_BUDGET = 44 * 1024 * 1024


def _mm_tiles(m, n, k, sa, sb, so, tm, tn, tk):
    for cm, cn in ((1024, 2048), (1024, 1024), (512, 2048), (512, 1024), (512, 512), (256, 1024), (256, 512)):
        pm, pn = _pick(m, cm), _pick(n, cn)
        if 2 * (pm * k * sa + k * pn * sb + pm * pn * so) <= MM_VMEM_BUDGET:
            return pm, pn, k
    return _pick(m, tm), _pick(n, tn), _pick(k, tk)


def _mm(a, b, mode, name, out_dtype=F32, tm=1024, tn=2048, tk=1024, rider=None):
    if mode == "nn":
        (m, k), (_, n) = a.shape, b.shape
    elif mode == "nt":
        (m, k), (n, _) = a.shape, b.shape
    else:
        (k, m), (_, n) = a.shape, b.shape
    tm, tn, tk = _mm_tiles(m, n, k, a.dtype.itemsize, b.dtype.itemsize, jnp.dtype(out_dtype).itemsize, tm, tn, tk)
    nk = k // tk
    dims = {"nn": NN, "nt": NT, "tn": TN}[mode]
    a_spec = (pl.BlockSpec((tk, tm), lambda i, j, l: (l, i)) if mode == "tn"
              else pl.BlockSpec((tm, tk), lambda i, j, l: (i, l)))
    b_spec = (pl.BlockSpec((tn, tk), lambda i, j, l: (j, l)) if mode == "nt"
              else pl.BlockSpec((tk, tn), lambda i, j, l: (l, j)))

    def body(a_ref, b_ref, o_ref, acc_ref):
        part = lax.dot_general(a_ref[...].astype(BF16), b_ref[...].astype(BF16), dims, preferred_element_type=F32)
        if nk == 1:
            o_ref[...] = part.astype(o_ref.dtype)
        else:
            l = pl.program_id(2)

            @pl.when(l == 0)
            def _():
                acc_ref[...] = part

            @pl.when(l > 0)
            def _():
                acc_ref[...] += part

            @pl.when(l == nk - 1)
            def _():
                o_ref[...] = acc_ref[...].astype(o_ref.dtype)

    res = _pcall(body, name=name, grid=(m // tm, n // tn, nk), in_specs=[a_spec, b_spec],
                 out_specs=[pl.BlockSpec((tm, tn), lambda i, j, l: (i, j))],
                 out_shape=[jax.ShapeDtypeStruct((m, n), out_dtype)],
                 scratch_shapes=[pltpu.VMEM((tm, tn) if nk > 1 else (8, LANES), F32)], rider=rider)(a, b)
    return res[0] if rider is None else res


def _dot(x, w, dims):
    return lax.dot_general(x.astype(BF16), w.astype(BF16), dims, preferred_element_type=F32)


@jax.custom_vjp
def bdot(x, w):
    return _dot(x, w, NN)


def _bdot_fwd(x, w):
    return _dot(x, w, NN), (x, w)


def _bdot_bwd(res, g):
    x, w = res
    return _dot(g, w, NT), _dot(x, g, TN)


bdot.defvjp(_bdot_fwd, _bdot_bwd)


def _split3(x):
    x1 = x.astype(BF16)
    r1 = x - x1.astype(F32)
    x2 = r1.astype(BF16)
    x3 = (r1 - x2.astype(F32)).astype(BF16)
    return x1, x2, x3


def _cdot(x, m, dims):
    return sum(lax.dot_general(p, m, dims, preferred_element_type=F32) for p in _split3(x))


def _block_dot(x, m, transposed):
    k_in = m.shape[1] if transposed else m.shape[0]
    dims = NT if transposed else NN
    return jnp.concatenate([_cdot(x[:, b * k_in:(b + 1) * k_in], m, dims) for b in range(x.shape[1] // k_in)], axis=1)


@jax.custom_vjp
def cdot(x, m):
    return _block_dot(x, m, False)


def _cdot_fwd(x, m):
    return _block_dot(x, m, False), m


def _cdot_bwd(m, g):
    return _block_dot(g, m, True), jnp.zeros_like(m)


cdot.defvjp(_cdot_fwd, _cdot_bwd)


@jax.custom_vjp
def tile_heads(y):
    return jnp.concatenate([y] * N_HEADS, axis=1)


def _tile_heads_fwd(y):
    return jnp.concatenate([y] * N_HEADS, axis=1), None


def _tile_heads_bwd(_, g):
    return (sum(g[:, h * HP:(h + 1) * HP] for h in range(N_HEADS)),)


tile_heads.defvjp(_tile_heads_fwd, _tile_heads_bwd)


def _shift_rows_impl(x, s, reverse):
    n = x.shape[0]
    idx = lax.broadcasted_iota(jnp.int32, x.shape, 0)
    if reverse:
        return jnp.where(idx < n - s, pltpu.roll(x, n - s, 0), 0.0)
    return jnp.where(idx >= s, pltpu.roll(x, s, 0), 0.0)


@functools.partial(jax.custom_vjp, nondiff_argnums=(1,))
def shift_rows(x, s):
    return _shift_rows_impl(x, s, False)


def _shift_rows_fwd(x, s):
    return _shift_rows_impl(x, s, False), None


def _shift_rows_bwd(s, _, g):
    return (_shift_rows_impl(g, s, True),)


shift_rows.defvjp(_shift_rows_fwd, _shift_rows_bwd)


def _rms(x, g):
    return x * lax.rsqrt(jnp.mean(x * x, axis=-1, keepdims=True) + NORM_EPS) * g


def _head_rms(x, e_mat, inv_n, g):
    ms = cdot(x * x, e_mat) * inv_n
    return x * lax.rsqrt(ms + NORM_EPS) * g


def _rope(y, p_mat, cos, sin):
    return y * cos + cdot(y, p_mat) * sin


def _constants():
    e = np.ones((HP, HP), np.float32)
    p = np.zeros((HP, HP), np.float32)
    r = np.zeros((LANES, HP), np.float32)
    pl64 = np.zeros((2 * HEAD_V, 2 * HP), np.float32)
    half = MLA_ROPE_DIM // 2
    for i in range(half):
        x1, x2 = MLA_NOPE_DIM + i, MLA_NOPE_DIM + half + i
        p[x2, x1] = -1.0
        p[x1, x2] = 1.0
    for i in range(MLA_ROPE_DIM):
        r[i, MLA_NOPE_DIM + i] = 1.0
    for h in range(2):
        for i in range(HEAD_V):
            pl64[h * HEAD_V + i, h * HP + i] = 1.0
    return tuple(jnp.asarray(a, BF16) for a in (e, p, r, pl64))


def _inv_freq_row():
    inv = ROPE_THETA ** (-jnp.arange(0, MLA_ROPE_DIM, 2, dtype=F32) / MLA_ROPE_DIM)
    head = jnp.concatenate([jnp.zeros((MLA_NOPE_DIM,), F32), inv, inv, jnp.zeros((HP - MLA_QK_DIM,), F32)])
    return jnp.tile(head, N_HEADS).reshape(1, QKW)


def _mixer_pre(c_q, c_kv, kr, fq, fk, qlg, w_uq, kvlg, w_uk, w_uv, mqg, mkg, fqg, fkg, cos, sin, e_mat, p_mat, r_mat, pl_mat):
    q = bdot(_rms(c_q, qlg), w_uq)
    ckvn = _rms(c_kv, kvlg)
    k = bdot(ckvn, w_uk) + tile_heads(cdot(kr, r_mat))
    v = bdot(ckvn, w_uv)
    q = _rope(_head_rms(q, e_mat, 1.0 / MLA_QK_DIM, mqg), p_mat, cos, sin)
    k = _rope(_head_rms(k, e_mat, 1.0 / MLA_QK_DIM, mkg), p_mat, cos, sin)
    qf = _head_rms(cdot(fq, pl_mat), e_mat, 1.0 / HEAD_V, fqg)
    kf = _head_rms(cdot(fk, pl_mat), e_mat, 1.0 / HEAD_V, fkg)
    return q, k, v, qf, kf


def _s5_post(state, u, c_blk, d_row, w_glu, b_glu):
    y = bdot(state, c_blk) + d_row * u
    y = jax.nn.gelu(y)
    return y * jax.nn.sigmoid(bdot(y, w_glu) + b_glu)


def _merge(o_a, o_b, o_c, g_a, g_b, g_c, w_a, w_b, w_c):
    return (jax.nn.sigmoid(g_a) * bdot(o_a, w_a) + jax.nn.sigmoid(g_b) * bdot(o_b, w_b)
            + jax.nn.sigmoid(g_c) * bdot(o_c, w_c))


def _ffn_mid(gate, val, wg0, wg1, wg2, wv0, wv1, wv2):
    cg = wg0 * shift_rows(gate, 2) + wg1 * shift_rows(gate, 1) + wg2 * gate
    cv = wv0 * shift_rows(val, 2) + wv1 * shift_rows(val, 1) + wv2 * val
    return jax.nn.silu(cg) * cv


def _s5_params(lam_re, lam_im, log_step, b_re, b_im):
    step = jnp.exp(log_step)
    zr, zi = lam_re * step, lam_im * step
    mag = jnp.exp(zr)
    lr, li = mag * jnp.cos(zi), mag * jnp.sin(zi)
    nr, ni = lr - 1.0, li
    den = lam_re * lam_re + lam_im * lam_im
    cr = (nr * lam_re + ni * lam_im) / den
    ci = (ni * lam_re - nr * lam_im) / den
    return lr, li, cr * b_re - ci * b_im, cr * b_im + ci * b_re


def _adamw(w, g, m, v):
    m = ADAM_B1 * m + (1.0 - ADAM_B1) * g
    v = ADAM_B2 * v + (1.0 - ADAM_B2) * (g * g)
    m_hat = m / (1.0 - ADAM_B1 ** ADAM_STEP)
    v_hat = v / (1.0 - ADAM_B2 ** ADAM_STEP)
    delta = -ADAM_LR * (m_hat / (jnp.sqrt(v_hat) + ADAM_EPS) + ADAM_WD * w)
    return delta, m, v


SCAN_W = 256


def _scan_seq(xr, xi, ar, ai, reverse):
    n = xr.shape[0]
    s = 1
    while s < n:
        sr, si = _shift_rows_impl(xr, s, reverse), _shift_rows_impl(xi, s, reverse)
        xr, xi = xr + ar * sr - ai * si, xi + ar * si + ai * sr
        ar, ai = ar * ar - ai * ai, 2.0 * ar * ai
        s *= 2
    return xr, xi


def _scan_fwd(bu, lbar, bsz, name, rider=None):
    t = bu.shape[0]
    seq = t // bsz
    nb = bu.shape[1] // (2 * SCAN_W)

    def fn(b, a):
        xr, xi = _scan_seq(b[:, :SCAN_W], b[:, SCAN_W:], a[:, :SCAN_W], a[:, SCAN_W:], False)
        return jnp.concatenate([xr, xi], axis=1)

    blk = (seq, 2 * SCAN_W)
    return _tile_call(fn, name, (bsz, nb),
                      [(bu, blk, lambda i, j: (i, j)), (lbar, (1, 2 * SCAN_W), lambda i, j: (0, j))],
                      [((t, bu.shape[1]), F32, blk, lambda i, j: (i, j), None)], rider=rider)


def _scan_bwd(dstate, state, lbar, bsz, name, rider=None):
    t = state.shape[0]
    seq = t // bsz
    nb = state.shape[1] // (2 * SCAN_W)

    def fn(g, x, a):
        ar, ai = a[:, :SCAN_W], a[:, SCAN_W:]
        gr, gi = _scan_seq(g[:, :SCAN_W], g[:, SCAN_W:], ar, -ai, True)
        pr, pi = _shift_rows_impl(x[:, :SCAN_W], 1, False), _shift_rows_impl(x[:, SCAN_W:], 1, False)
        dar = jnp.sum(gr * pr + gi * pi, axis=0, keepdims=True)
        dai = jnp.sum(gi * pr - gr * pi, axis=0, keepdims=True)
        return jnp.concatenate([gr, gi], axis=1), jnp.concatenate([dar, dai], axis=1)

    blk = (seq, 2 * SCAN_W)
    return _tile_call(fn, name, (nb, bsz),
                      [(dstate, blk, lambda j, i: (i, j)), (state, blk, lambda j, i: (i, j)),
                       (lbar, (1, 2 * SCAN_W), lambda j, i: (0, j))],
                      [((t, state.shape[1]), F32, blk, lambda j, i: (i, j), None),
                       ((1, state.shape[1]), F32, (1, 2 * SCAN_W), lambda j, i: (0, j), (1,))], rider=rider)


def _shift_lanes(x, s, reverse):
    n = x.shape[1]
    idx = lax.broadcasted_iota(jnp.int32, x.shape, 1)
    if reverse:
        return jnp.where(idx < n - s, pltpu.roll(x, n - s, 1), 0.0)
    return jnp.where(idx >= s, pltpu.roll(x, s, 1), 0.0)


def _cumsum_lanes(x, reverse):
    s = 1
    while s < x.shape[1]:
        x = x + _shift_lanes(x, s, reverse)
        s *= 2
    return x


def _forget_fwd(z, bias, name):
    def fn(zv, bv):
        x = zv + bv
        logf = jnp.minimum(x, 0.0) - jnp.log(1.0 + jnp.exp(-jnp.abs(x)))
        return _cumsum_lanes(logf, False)

    return _tile_call(fn, name, (1,), [_par(z), _par(bias)],
                      [(z.shape, F32, z.shape, lambda i: (0, 0), None)])[0]


def _forget_bwd(dc, z, bias, name):
    def fn(dcv, zv, bv):
        x = zv + bv
        dlogf = _cumsum_lanes(dcv, True)
        dz = dlogf * jax.nn.sigmoid(-x)
        return dz, jnp.sum(dz, axis=1, keepdims=True)

    rows = z.shape[0]
    return _tile_call(fn, name, (1,), [_par(dc), _par(z), _par(bias)],
                      [(z.shape, F32, z.shape, lambda i: (0, 0), None),
                       ((rows, 1), F32, (rows, 1), lambda i: (0, 0), None)])


def _col(v2, x):
    idx = lax.broadcasted_iota(jnp.int32, v2.shape, 1)
    return jnp.sum(jnp.where(idx == x, v2, 0.0), axis=1, keepdims=True)


def _two_cols(c0, c1):
    idx = lax.broadcasted_iota(jnp.int32, (c0.shape[0], 2), 1)
    return jnp.where(idx == 0, c0, c1)


def _rows2(r0, r1):
    idx = lax.broadcasted_iota(jnp.int32, (2, r0.shape[1]), 0)
    return jnp.where(idx == 0, r0, r1)


def _attn_specs(seq, v_blk0, has_bias):
    qblk = pl.BlockSpec((None, TQ, 2 * HP), lambda b, p, i: (b, i, p))
    kblk = pl.BlockSpec((None, seq, 2 * HP), lambda b, p, i: (b, 0, p))
    vblk = pl.BlockSpec((None, seq, LANES), lambda b, p, i: (b, 0, v_blk0 + p))
    oblk = pl.BlockSpec((None, TQ, LANES), lambda b, p, i: (b, i, p))
    rowblk = pl.BlockSpec((None, None, 2, TQ), lambda b, p, i: (b, p, 0, i))
    colblk = pl.BlockSpec((None, None, seq, 2), lambda b, p, i: (b, p, 0, 0))
    return qblk, kblk, vblk, oblk, rowblk, colblk


def _attn_fwd(q, k, v, v_blk0, c_cols, c_rows, scale, name, rider=None):
    bsz, seq, _ = q.shape
    nq = seq // TQ
    has_bias = c_cols is not None
    n_in = 5 if has_bias else 3
    nc = TQ // QF

    def body(*refs):
        q_ref, k_ref, v_ref = refs[:3]
        cq_ref, ck_ref = refs[3:5] if has_bias else (None, None)
        o_ref, lse_ref = refs[n_in:]
        qi = pl.program_id(2)
        qb = [[q_ref[c * QF:(c + 1) * QF, x * HP:(x + 1) * HP].astype(BF16) for c in range(nc)] for x in range(2)]
        key_in = lax.broadcasted_iota(jnp.int32, (TQ, QF), 0)
        qry_in = lax.broadcasted_iota(jnp.int32, (TQ, QF), 1)

        def block(j, carry, masked):
            ks = pl.multiple_of(j * TQ, TQ)
            vt = v_ref[pl.ds(ks, TQ), :].T.astype(BF16)
            new = []
            for x in range(2):
                kx = k_ref[pl.ds(ks, TQ), x * HP:(x + 1) * HP].astype(BF16)
                ckx = _col(ck_ref[pl.ds(ks, TQ), :], x) if has_bias else None
                for c in range(nc):
                    m, l, acc = carry[x * nc + c]
                    s = lax.dot_general(kx, qb[x][c], NT, preferred_element_type=F32) * scale
                    if has_bias:
                        s = s + (cq_ref[x:x + 1, c * QF:(c + 1) * QF] - ckx)
                    if masked:
                        s = jnp.where(qry_in + c * QF >= key_in, s, NEG_INF)
                    m_new = jnp.maximum(m, jnp.max(s, axis=0, keepdims=True))
                    alpha = jnp.exp(m - m_new)
                    p = jnp.exp(s - m_new)
                    l = alpha * l + jnp.sum(p, axis=0, keepdims=True)
                    acc = alpha * acc + lax.dot_general(vt, p.astype(BF16), NN, preferred_element_type=F32)
                    new.append((m_new, l, acc))
            return tuple(new)

        init = tuple((jnp.full((1, QF), NEG_INF, F32), jnp.zeros((1, QF), F32), jnp.zeros((LANES, QF), F32))
                     for _ in range(2 * nc))
        carry = lax.fori_loop(0, qi, lambda j, cr: block(j, cr, False), init)
        carry = block(qi, carry, True)
        lane = lax.broadcasted_iota(jnp.int32, (QF, LANES), 1)
        lse_rows = []
        for x in range(2):
            lse_rows.append(jnp.concatenate([carry[x * nc + c][0] + jnp.log(carry[x * nc + c][1]) for c in range(nc)],
                                            axis=1))
        for c in range(nc):
            (_, l0, a0), (_, l1, a1) = carry[c], carry[nc + c]
            o_ref[c * QF:(c + 1) * QF, :] = jnp.where(lane < HEAD_V, (a0 / l0).T, (a1 / l1).T)
        lse_ref[...] = _rows2(lse_rows[0], lse_rows[1])

    qblk, kblk, vblk, oblk, rowblk, colblk = _attn_specs(seq, v_blk0, has_bias)
    in_specs, args = [qblk, kblk, vblk], [q, k, v]
    if has_bias:
        in_specs += [rowblk, colblk]
        args += [c_rows, c_cols]
    return _pcall(body, name=name, grid=(bsz, 2, nq), in_specs=in_specs, out_specs=[oblk, rowblk],
                  out_shape=[jax.ShapeDtypeStruct((bsz, seq, 2 * LANES), F32),
                             jax.ShapeDtypeStruct((bsz, 2, 2, seq), F32)], rider=rider)(*args)


def _attn_bwd(q, k, v, v_blk0, o, do, lse, c_cols, c_rows, scale, name, rider=None):
    bsz, seq, _ = q.shape
    nq = seq // TQ
    has_bias = c_cols is not None
    n_in = 8 if has_bias else 6
    nc = TQ // QC

    def body(*refs):
        q_ref, k_ref, v_ref, o_ref, do_ref, lse_ref = refs[:6]
        cq_ref, ck_ref = refs[6:8] if has_bias else (None, None)
        dq_ref, dk_ref, dv_ref = refs[n_in:n_in + 3]
        dck_ref, dcq_ref = refs[n_in + 3:n_in + 5] if has_bias else (None, None)
        qi = pl.program_id(2)

        @pl.when(qi == 0)
        def _():
            dk_ref[...] = jnp.zeros_like(dk_ref)
            dv_ref[...] = jnp.zeros_like(dv_ref)
            if has_bias:
                dck_ref[...] = jnp.zeros_like(dck_ref)

        lane = lax.broadcasted_iota(jnp.int32, (QC, LANES), 1)
        ones8 = jnp.ones((8, LANES), BF16)
        qb, dob, delta, lse_r = [], [], [], []
        for x in range(2):
            hm = jnp.logical_and(lane >= x * HEAD_V, lane < (x + 1) * HEAD_V)
            qb.append([]), dob.append([]), delta.append([]), lse_r.append([])
            for c in range(nc):
                rows = slice(c * QC, (c + 1) * QC)
                do_c = jnp.where(hm, do_ref[rows, :], 0.0)
                prod = do_c * o_ref[rows, :]
                hi = prod.astype(BF16)
                lo = (prod - hi.astype(F32)).astype(BF16)
                d8 = (lax.dot_general(ones8, hi, NT, preferred_element_type=F32)
                      + lax.dot_general(ones8, lo, NT, preferred_element_type=F32))
                delta[x].append(d8[0:1, :])
                dob[x].append(do_c.astype(BF16))
                qb[x].append(q_ref[rows, x * HP:(x + 1) * HP].astype(BF16))
                lse_r[x].append(lse_ref[x:x + 1, rows])
        key_in = lax.broadcasted_iota(jnp.int32, (TQ, QC), 0)
        qry_in = lax.broadcasted_iota(jnp.int32, (TQ, QC), 1)

        def block(j, carry, masked):
            ks = pl.multiple_of(j * TQ, TQ)
            vb = v_ref[pl.ds(ks, TQ), :].astype(BF16)
            new, key_sums = [], []
            for x in range(2):
                k32 = k_ref[pl.ds(ks, TQ), x * HP:(x + 1) * HP]
                kx, kt = k32.astype(BF16), k32.T.astype(BF16)
                ckx = _col(ck_ref[pl.ds(ks, TQ), :], x) if has_bias else None
                dk_acc = jnp.zeros((TQ, HP), F32)
                dv_acc = jnp.zeros((TQ, LANES), F32)
                key_sum = jnp.zeros((TQ, 1), F32)
                for c in range(nc):
                    dqt, dcq = carry[x * nc + c]
                    s = lax.dot_general(kx, qb[x][c], NT, preferred_element_type=F32) * scale
                    if has_bias:
                        s = s + (cq_ref[x:x + 1, c * QC:(c + 1) * QC] - ckx)
                    if masked:
                        s = jnp.where(qry_in + c * QC >= key_in, s, NEG_INF)
                    p = jnp.exp(s - lse_r[x][c])
                    dp = lax.dot_general(vb, dob[x][c], NT, preferred_element_type=F32)
                    ds = p * (dp - delta[x][c])
                    dsb = ds.astype(BF16)
                    dk_acc = dk_acc + lax.dot_general(dsb, qb[x][c], NN, preferred_element_type=F32)
                    dv_acc = dv_acc + lax.dot_general(p.astype(BF16), dob[x][c], NN, preferred_element_type=F32)
                    dqt = dqt + lax.dot_general(kt, dsb, NN, preferred_element_type=F32)
                    if has_bias:
                        dcq = dcq + jnp.sum(ds, axis=0, keepdims=True)
                        key_sum = key_sum + jnp.sum(ds, axis=1, keepdims=True)
                    new.append((dqt, dcq))
                dk_ref[pl.ds(ks, TQ), x * HP:(x + 1) * HP] += dk_acc * scale
                dv_ref[pl.ds(ks, TQ), :] += dv_acc
                key_sums.append(key_sum)
            if has_bias:
                dck_ref[pl.ds(ks, TQ), :] -= _two_cols(key_sums[0], key_sums[1])
            return tuple(new)

        init = tuple((jnp.zeros((HP, QC), F32), jnp.zeros((1, QC), F32)) for _ in range(2 * nc))
        carry = lax.fori_loop(0, qi, lambda j, cr: block(j, cr, False), init)
        carry = block(qi, carry, True)
        for x in range(2):
            for c in range(nc):
                dq_ref[c * QC:(c + 1) * QC, x * HP:(x + 1) * HP] = carry[x * nc + c][0].T * scale
        if has_bias:
            dcq_ref[...] = _rows2(jnp.concatenate([carry[c][1] for c in range(nc)], axis=1),
                                  jnp.concatenate([carry[nc + c][1] for c in range(nc)], axis=1))

    qblk, kblk, vblk, oblk, rowblk, colblk = _attn_specs(seq, v_blk0, has_bias)
    in_specs = [qblk, kblk, vblk, oblk, oblk, rowblk]
    out_specs = [qblk, kblk, pl.BlockSpec((None, seq, LANES), lambda b, p, i: (b, 0, p))]
    out_shape = [jax.ShapeDtypeStruct((bsz, seq, QKW), F32), jax.ShapeDtypeStruct((bsz, seq, QKW), F32),
                 jax.ShapeDtypeStruct((bsz, seq, 2 * LANES), F32)]
    args = [q, k, v, o, do, lse]
    if has_bias:
        in_specs += [rowblk, colblk]
        out_specs += [colblk, rowblk]
        out_shape += [jax.ShapeDtypeStruct((bsz, 2, seq, 2), F32), jax.ShapeDtypeStruct((bsz, 2, 2, seq), F32)]
        args += [c_rows, c_cols]
    return _pcall(body, name=name, grid=(bsz, 2, nq), in_specs=in_specs, out_specs=out_specs,
                  out_shape=out_shape, rider=rider)(*args)


def _coords(idx):
    return (idx // 4, (idx // 2) % 2, idx % 2)


class _Xfer:
    def __init__(self, srcs, out_shapes, plan):
        self.srcs, self.out_shapes, self.plan = list(srcs), list(out_shapes), plan
        self.n = len(self.srcs)
        me0 = jnp.int32(0)
        n_pieces = [len(plan(a, me0, me0)) for a in range(self.n)]
        self.offs = np.concatenate([[0], np.cumsum(n_pieces)]).astype(int)
        self.total = int(self.offs[-1])

    def sems(self):
        remote = (N_DEV - 1) * self.total
        return [pltpu.SemaphoreType.DMA((remote,)), pltpu.SemaphoreType.DMA((remote,)),
                pltpu.SemaphoreType.DMA((self.total,))]

    def copies(self, src_refs, out_refs, send_sems, recv_sems, local_sems):
        me = 4 * lax.axis_index("x") + 2 * lax.axis_index("y") + lax.axis_index("c")
        out = []
        for a in range(self.n):
            for pi, (si, di) in enumerate(self.plan(a, me, me)):
                out.append((pltpu.make_async_copy(src_refs[a].at[si], out_refs[a].at[di],
                                                  local_sems.at[self.offs[a] + pi]), None))
        for kk in range(1, N_DEV):
            dest = (me + kk) % N_DEV
            src_dev = (me + N_DEV - kk) % N_DEV
            for a in range(self.n):
                pieces = self.plan(a, me, dest)
                landing = self.plan(a, src_dev, me)
                for pi, ((si, di), (_, li)) in enumerate(zip(pieces, landing)):
                    sem = (kk - 1) * self.total + self.offs[a] + pi
                    mk = functools.partial(pltpu.make_async_remote_copy, src_ref=src_refs[a].at[si],
                                           send_sem=send_sems.at[sem], recv_sem=recv_sems.at[sem],
                                           device_id=_coords(dest), device_id_type=pl.DeviceIdType.MESH)
                    out.append((mk(dst_ref=out_refs[a].at[di]), mk(dst_ref=out_refs[a].at[li])))
        return out

    @staticmethod
    def start(copies):
        for cp, _ in copies:
            cp.start()

    @staticmethod
    def wait(copies):
        for cp, rc in copies:
            if rc is None:
                cp.wait()
            else:
                cp.wait_send()
                rc.wait_recv()


def _exchange(name, xf):
    n = xf.n

    def body(*refs):
        copies = xf.copies(refs[:n], refs[n:2 * n], *refs[2 * n:])
        xf.start(copies)
        xf.wait(copies)

    any_spec = pl.BlockSpec(memory_space=pl.ANY)
    return pl.pallas_call(body, name=name, in_specs=[any_spec] * n, out_specs=[any_spec] * n,
                          out_shape=[jax.ShapeDtypeStruct(s, d) for s, d in xf.out_shapes],
                          scratch_shapes=xf.sems(),
                          compiler_params=pltpu.CompilerParams(has_side_effects=True))(*xf.srcs)


def _rows_of(j, n):
    return pl.ds(pl.multiple_of(j * n, n), n)


def _ffn_cols(j, half):
    return pl.multiple_of((2 * (j % 4) + half) * (2 * FF_BLK) + (j // 4) * FF_BLK, FF_BLK)


def _gather_plan(shards):
    names = list(shards)

    def plan(a, src_dev, _dest):
        nm = names[a]
        j = src_dev
        if nm in ("w_in", "w_uq", "w_ukv", "w_glu", "w_out", "w_down"):
            rows = shards[nm].shape[0]
            return [((slice(None), slice(None)), (_rows_of(j, rows), slice(None)))]
        if nm == "w_branch":
            return [((slice(None), slice(None), slice(None)), (slice(None), slice(None), _rows_of(j, LANES)))]
        if nm in ("w_up", "conv"):
            return [((slice(None), pl.ds(h * FF_BLK, FF_BLK)), (slice(None), pl.ds(_ffn_cols(j, h), FF_BLK)))
                    for h in range(2)]
        raise KeyError(nm)

    return names, plan


def _scatter_plan(names, shard_shapes):
    def plan(a, src_dev, dest):
        nm = names[a]
        j = dest
        if nm in ("w_in", "w_uq", "w_ukv", "w_glu", "w_out", "w_down"):
            rows = shard_shapes[nm][0]
            return [((_rows_of(j, rows), slice(None)), (src_dev, slice(None), slice(None)))]
        if nm == "w_branch":
            return [((slice(None), slice(None), _rows_of(j, LANES)), (src_dev, slice(None), slice(None), slice(None)))]
        if nm in ("w_up", "conv"):
            return [((slice(None), pl.ds(_ffn_cols(j, h), FF_BLK)), (src_dev, slice(None), pl.ds(h * FF_BLK, FF_BLK)))
                    for h in range(2)]
        if nm == "small":
            return [((slice(None), slice(None)), (src_dev, slice(None), slice(None)))]
        raise KeyError(nm)

    return plan


def _pad_in_cols(w):
    lead = w.shape[:-1]
    parts, pos = [], 0
    for dst, src, width in IN_SEGS:
        if dst > pos:
            parts.append(jnp.zeros(lead + (dst - pos,), w.dtype))
        parts.append(w[..., src:src + width])
        pos = dst + width
    return jnp.concatenate(parts, axis=-1)


def _unpad_in_cols(w):
    order = sorted(IN_SEGS, key=lambda s: s[1])
    return jnp.concatenate([w[..., dst:dst + width] for dst, _, width in order], axis=-1)


def _pad_ff(w, axis):
    n = w.shape[axis] // FF_HALF
    parts = []
    for h in range(n):
        piece = lax.slice_in_dim(w, h * FF_HALF, (h + 1) * FF_HALF, axis=axis)
        zshape = list(w.shape)
        zshape[axis] = FF_BLK - FF_HALF
        parts += [piece, jnp.zeros(zshape, w.dtype)]
    return jnp.concatenate(parts, axis=axis)


def _unpad_ff(w, axis):
    n = w.shape[axis] // FF_BLK
    return jnp.concatenate([lax.slice_in_dim(w, h * FF_BLK, h * FF_BLK + FF_HALF, axis=axis) for h in range(n)],
                           axis=axis)


def _to_scan_cols(re_part, im_part):
    lead = re_part.shape[:-1]
    nb = S5_N // SCAN_W
    r = re_part.reshape(lead + (nb, SCAN_W))
    i = im_part.reshape(lead + (nb, SCAN_W))
    return jnp.concatenate([r, i], axis=-1).reshape(lead + (2 * S5_N,))


def _from_scan_cols(x):
    lead = x.shape[:-1]
    nb = S5_N // SCAN_W
    y = x.reshape(lead + (nb, 2, SCAN_W))
    return y[..., 0, :].reshape(lead + (S5_N,)), y[..., 1, :].reshape(lead + (S5_N,))


def _block_diag_in(b):
    eye = jnp.eye(S5_GROUPS, dtype=b.dtype)
    return jnp.einsum("gph,gk->ghkp", b, eye).reshape(S5_WIDTH, S5_N)


def _block_diag_in_t(m):
    eye = jnp.eye(S5_GROUPS, dtype=m.dtype)
    return jnp.einsum("ghkp,gk->gph", m.reshape(S5_GROUPS, S5_GROUP_CH, S5_GROUPS, S5_STATE), eye)


def _block_diag_out(c):
    eye = jnp.eye(S5_GROUPS, dtype=c.dtype)
    return jnp.einsum("ghp,gk->gpkh", c, eye).reshape(S5_N, S5_WIDTH)


def _block_diag_out_t(m):
    eye = jnp.eye(S5_GROUPS, dtype=m.dtype)
    return jnp.einsum("gpkh,gk->ghp", m.reshape(S5_GROUPS, S5_STATE, S5_GROUPS, S5_GROUP_CH), eye)


def _pad_heads(g, width):
    g = jnp.broadcast_to(g.reshape(-1, width), (N_HEADS, width))
    return jnp.pad(g, ((0, 0), (0, HP - width))).reshape(1, QKW)


def _prep_layer(l, gw, small, consts):
    e_mat, p_mat, r_mat, pl_mat = consts
    p = {}
    p["attn_g"] = small["attn_norm_g"][l].reshape(1, D_MODEL)
    p["ffn_g"] = small["ffn_norm_g"][l].reshape(1, D_MODEL)
    p["qlg"] = small["q_lat_norm_g"][l].reshape(1, MLA_Q_RANK)
    p["kvlg"] = small["kv_lat_norm_g"][l].reshape(1, MLA_KV_RANK)
    p["mqg"] = _pad_heads(small["mla_q_norm_g"][l], MLA_QK_DIM)
    p["mkg"] = _pad_heads(small["mla_k_norm_g"][l], MLA_QK_DIM)
    p["fqg"] = _pad_heads(small["fox_q_norm_g"][l], HEAD_V)
    p["fkg"] = _pad_heads(small["fox_k_norm_g"][l], HEAD_V)
    w_uq = gw["w_uq"].reshape(MLA_Q_RANK, N_HEADS, MLA_QK_DIM)
    p["w_uq"] = jnp.pad(w_uq, ((0, 0), (0, 0), (0, HP - MLA_QK_DIM))).reshape(MLA_Q_RANK, QKW)
    w_ukv = gw["w_ukv"].reshape(MLA_KV_RANK, N_HEADS, MLA_NOPE_DIM + HEAD_V)
    p["w_uk"] = jnp.pad(w_ukv[..., :MLA_NOPE_DIM], ((0, 0), (0, 0), (0, HP - MLA_NOPE_DIM))).reshape(MLA_KV_RANK, QKW)
    p["w_uv"] = w_ukv[..., MLA_NOPE_DIM:].reshape(MLA_KV_RANK, N_HEADS * HEAD_V)
    p["w_glu"] = gw["w_glu"]
    p["b_glu"] = small["s5_b_glu"][l].reshape(1, S5_WIDTH)
    p["d_row"] = small["s5_d"][l].reshape(1, S5_WIDTH)
    p["lam_re"] = small["s5_lambda_re"][l].reshape(S5_N, 1)
    p["lam_im"] = small["s5_lambda_im"][l].reshape(S5_N, 1)
    p["log_step"] = jnp.repeat(small["s5_log_step"][l], S5_STATE).reshape(S5_N, 1)
    p["b_re"] = small["s5_b_re"][l].reshape(S5_N, S5_GROUP_CH)
    p["b_im"] = small["s5_b_im"][l].reshape(S5_N, S5_GROUP_CH)
    c_re, c_im = small["s5_c_re"][l], small["s5_c_im"][l]
    p["c_blk"] = _to_scan_cols(_block_diag_out(c_re).T, -_block_diag_out(c_im).T).T
    p["fbias"] = small["fox_f_bias"][l]
    for nm in ("w_in", "w_out", "w_branch"):
        p[nm] = gw[nm]
    return p


def _s5_param_call(p, l):
    ins = [p["lam_re"], p["lam_im"], p["log_step"], p["b_re"], p["b_im"]]
    outs = [((S5_N, 1), F32, (S5_N, 1), lambda i: (0, 0), None)] * 2 + \
           [((S5_N, S5_GROUP_CH), F32, (S5_N, S5_GROUP_CH), lambda i: (0, 0), None)] * 2
    return _tile_call(_s5_params, "s5_params", (1,), [_par(a) for a in ins], outs)


def _s5_param_bwd_call(p, cts):
    ins = [p["lam_re"], p["lam_im"], p["log_step"], p["b_re"], p["b_im"]]

    def fn(lr, li, ls, br, bi, g0, g1, g2, g3):
        _, vjp = jax.vjp(_s5_params, lr, li, ls, br, bi)
        return vjp((g0, g1, g2, g3))

    outs = [((S5_N, 1), F32, (S5_N, 1), lambda i: (0, 0), None)] * 3 + \
           [((S5_N, S5_GROUP_CH), F32, (S5_N, S5_GROUP_CH), lambda i: (0, 0), None)] * 2
    return _tile_call(fn, "s5_params_bwd", (1,), [_par(a) for a in ins + list(cts)], outs)


def _layer_fwd(x_prev, f_prev, p, tabs, consts, bsz, riders):
    arrived = {}

    def rider_of(name):
        return riders[name][0] if name in riders else None

    def hosted(name, res, n_out):
        if name in riders:
            arrived.update(zip(riders[name][1], res[n_out:]))
        return res[:n_out]

    t = x_prev.shape[0]
    seq = t // bsz
    nt = t // TM
    cos, sin = tabs
    e_mat, p_mat, r_mat, pl_mat = consts
    sv = {}

    if f_prev is None:
        x = x_prev
        h = _tile_call(lambda xv, g: _rms(xv, g), "norm_first", (nt,), [_row(x), _par(p["attn_g"])],
                       [_orow(t, D_MODEL, BF16)])[0]
    else:
        x, h = _tile_call(lambda xv, fv, g: (xv + fv, _rms(xv + fv, g)), "norm_attn", (nt,),
                          [_row(x_prev), _row(f_prev), _par(p["attn_g"])], [_orow(t, D_MODEL), _orow(t, D_MODEL, BF16)])
    sv["x"], sv["h"] = x, h
    proj = _mm(h, p["w_in"], "nn", "mm_in")
    sv["proj"] = proj

    def pre_fn(pa, cosv, sinv, qlg, w_uq, kvlg, w_uk, w_uv, mqg, mkg, fqg, fkg, em, pm, rm, plm):
        return _mixer_pre(pa[:, O_CQ:O_CQ + MLA_Q_RANK], pa[:, O_CKV:O_CKV + MLA_KV_RANK], pa[:, O_KR:O_KR + LANES],
                          pa[:, O_FQ:O_FQ + 256], pa[:, O_FK:O_FK + 256], qlg, w_uq, kvlg, w_uk, w_uv,
                          mqg, mkg, fqg, fkg, cosv, sinv, em, pm, rm, plm)

    pre_params = [p["qlg"], p["w_uq"], p["kvlg"], p["w_uk"], p["w_uv"], p["mqg"], p["mkg"], p["fqg"], p["fkg"],
                  e_mat, p_mat, r_mat, pl_mat]
    q_m, k_m, v_m, q_f, k_f = hosted("mixer_pre", _tile_call(
        pre_fn, "mixer_pre", (nt,), [_row(proj, PA), _row(cos), _row(sin)] + [_par(a) for a in pre_params],
        [_orow(t, QKW), _orow(t, QKW), _orow(t, 2 * LANES), _orow(t, QKW), _orow(t, QKW)],
        rider=rider_of("mixer_pre")), 5)
    sv.update(q_m=q_m, k_m=k_m, v_m=v_m, q_f=q_f, k_f=k_f)

    z = proj[:, O_FL:O_FL + N_HEADS].reshape(bsz, seq, N_HEADS).transpose(0, 2, 1).reshape(bsz * N_HEADS, seq)
    fb = jnp.tile(p["fbias"], bsz).reshape(bsz * N_HEADS, 1)
    cum = _forget_fwd(z, fb, "forget_fwd")
    ck = cum.reshape(bsz, 2, 2, seq)
    cq = ck.transpose(0, 1, 3, 2)
    sv.update(z=z, fb=fb, cq=cq, ck=ck)

    r3 = lambda a: a.reshape(bsz, seq, a.shape[-1])
    o_m, lse_m = hosted("attn_mla_fwd", _attn_fwd(r3(q_m), r3(k_m), r3(v_m), 0, None, None, MLA_QK_DIM ** -0.5,
                                                  "attn_mla_fwd", rider_of("attn_mla_fwd")), 2)
    o_f, lse_f = hosted("attn_fox_fwd", _attn_fwd(r3(q_f), r3(k_f), r3(proj), O_FV // LANES, cq, ck, HEAD_V ** -0.5,
                                                  "attn_fox_fwd", rider_of("attn_fox_fwd")), 2)
    o_m, o_f = o_m.reshape(t, 2 * LANES), o_f.reshape(t, 2 * LANES)
    sv.update(o_m=o_m, o_f=o_f, lse_m=lse_m, lse_f=lse_f)

    lbar_re, lbar_im, bbar_re, bbar_im = _s5_param_call(p, 0)
    lbar = _to_scan_cols(lbar_re.reshape(1, S5_N), lbar_im.reshape(1, S5_N))
    b_blk = _to_scan_cols(_block_diag_in(bbar_re.reshape(S5_GROUPS, S5_STATE, S5_GROUP_CH)),
                          _block_diag_in(bbar_im.reshape(S5_GROUPS, S5_STATE, S5_GROUP_CH)))
    sv.update(lbar=lbar, b_blk=b_blk)
    bu = _tile_call(lambda u, bb: bdot(u, bb), "s5_bu", (nt,), [_row(proj, S5_WIDTH, O_U // S5_WIDTH), _par(b_blk)],
                    [_orow(t, 2 * S5_N)])[0]
    state = hosted("s5_scan_fwd", _scan_fwd(bu, lbar, bsz, "s5_scan_fwd", rider_of("s5_scan_fwd")), 1)[0]
    sv["state"] = state
    post_params = [p["c_blk"], p["d_row"], p["w_glu"], p["b_glu"]]
    o_s = _tile_call(_s5_post, "s5_post", (nt,),
                     [_row(state), _row(proj, S5_WIDTH, O_U // S5_WIDTH)] + [_par(a) for a in post_params],
                     [_orow(t, S5_WIDTH)])[0]
    sv["o_s"] = o_s

    def merge_fn(oa, ob, oc, ga, gb, gc, wb):
        return _merge(oa, ob, oc, ga, gb, gc, wb[0], wb[1], wb[2])

    gate_specs = [_row(proj, D_MODEL, PA // D_MODEL + n) for n in range(3)]
    merged = _tile_call(merge_fn, "merge", (nt,), [_row(o_m), _row(o_f), _row(o_s)] + gate_specs + [_par(p["w_branch"])],
                        [_orow(t, D_MODEL, BF16)])[0]
    sv["merged"] = merged
    attn_out = _mm(merged, p["w_out"], "nn", "mm_out")

    x1, h2 = _tile_call(lambda xv, av, g: (xv + av, _rms(xv + av, g)), "norm_ffn", (nt,),
                        [_row(x), _row(attn_out), _par(p["ffn_g"])], [_orow(t, D_MODEL), _orow(t, D_MODEL, BF16)])
    sv["x1"], sv["h2"] = x1, h2
    p.update({n: arrived.pop(n) for n in FFN_WEIGHTS})
    up = _mm(h2, p["w_up"], "nn", "mm_up")
    sv["up"] = up

    def ffn_fn(upv, cw):
        return _ffn_mid(upv[:, :FF_BLK], upv[:, FF_BLK:], cw[0:1, :FF_BLK], cw[1:2, :FF_BLK], cw[2:3, :FF_BLK],
                        cw[0:1, FF_BLK:], cw[1:2, FF_BLK:], cw[2:3, FF_BLK:])

    nblk = FF_PW // FF_BLK
    act = _tile_call(ffn_fn, "ffn_mid", (nblk, bsz),
                     [(up, (seq, 2 * FF_BLK), lambda j, b: (b, j)), (p["conv"], (3, 2 * FF_BLK), lambda j, b: (0, j))],
                     [((t, FF_PW), BF16, (seq, FF_BLK), lambda j, b: (b, j), None)])[0]
    sv["act"] = act
    ffn_out = _mm(act, p["w_down"], "nn", "mm_down")
    return x1, ffn_out, sv, arrived


def _layer_bwd(dx2, p, sv, tabs, consts, bsz, scatter, carry):
    lands, carry_lands = {}, {}

    def riding(host, g):
        names = BWD_RIDERS[host]
        return scatter({n: g[n] for n in names}), names

    def landed(host, res, n_out):
        lands.update(zip(BWD_RIDERS[host], res[n_out:]))
        return res[:n_out]

    t = dx2.shape[0]
    seq = t // bsz
    nt = t // TM
    cos, sin = tabs
    e_mat, p_mat, r_mat, pl_mat = consts
    g = {}

    g["w_down"] = _mm(sv["act"], dx2, "tn", "mm_down_dw", out_dtype=BF16)
    dact = _mm(dx2, p["w_down"], "nt", "mm_down_dx")

    def ffn_bwd_fn(upv, cw, da):
        args = (upv[:, :FF_BLK], upv[:, FF_BLK:], cw[0:1, :FF_BLK], cw[1:2, :FF_BLK], cw[2:3, :FF_BLK],
                cw[0:1, FF_BLK:], cw[1:2, FF_BLK:], cw[2:3, FF_BLK:])
        _, vjp = jax.vjp(_ffn_mid, *args)
        dg, dv, g0, g1, g2, v0, v1, v2 = vjp(da)
        return (jnp.concatenate([dg, dv], axis=1), jnp.concatenate([g0, v0], axis=1),
                jnp.concatenate([g1, v1], axis=1), jnp.concatenate([g2, v2], axis=1))

    nblk = FF_PW // FF_BLK
    cw_out = ((1, UP_W), F32, (1, 2 * FF_BLK), lambda j, b: (0, j), (1,))
    res = _tile_call(
        ffn_bwd_fn, "ffn_mid_bwd", (nblk, bsz),
        [(sv["up"], (seq, 2 * FF_BLK), lambda j, b: (b, j)), (p["conv"], (3, 2 * FF_BLK), lambda j, b: (0, j)),
         (dact, (seq, FF_BLK), lambda j, b: (b, j))],
        [((t, UP_W), BF16, (seq, 2 * FF_BLK), lambda j, b: (b, j), None), cw_out, cw_out, cw_out],
        rider=scatter(carry) if carry else None)
    dup, dc0, dc1, dc2 = res[:4]
    carry_lands.update(zip(carry, res[4:]))
    g["conv"] = jnp.concatenate([dc0, dc1, dc2], axis=0)
    g["w_up"] = _mm(sv["h2"], dup, "tn", "mm_up_dw", out_dtype=BF16)
    dh2 = _mm(dup, p["w_up"], "nt", "mm_up_dx")

    def norm_bwd_fn(xv, gv, dh, dres):
        _, vjp = jax.vjp(_rms, xv, gv)
        dxv, dg = vjp(dh)
        return dres + dxv, dg

    d1, g["ffn_g"] = _tile_call(norm_bwd_fn, "norm_bwd", (nt,),
                                [_row(sv["x1"]), _par(p["ffn_g"]), _row(dh2), _row(dx2)],
                                [_orow(t, D_MODEL), _oacc((1, D_MODEL))])

    g["w_out"] = _mm(sv["merged"], d1, "tn", "mm_out_dw", out_dtype=BF16)
    dmerged = _mm(d1, p["w_out"], "nt", "mm_out_dx")

    def merge_bwd_fn(oa, ob, oc, ga, gb, gc, wb, dm):
        wf = wb.astype(F32)
        _, vjp = jax.vjp(_merge, oa, ob, oc, ga, gb, gc, wf[0], wf[1], wf[2])
        doa, dob, doc, dga, dgb, dgc, dwa, dwb, dwc = vjp(dm)
        return doa, dob, doc, jnp.concatenate([dga, dgb, dgc], axis=1), jnp.stack([dwa, dwb, dwc])

    proj = sv["proj"]
    gate_specs = [_row(proj, D_MODEL, PA // D_MODEL + n) for n in range(3)]
    do_m, do_f, do_s, dgl, g["w_branch"] = _tile_call(
        merge_bwd_fn, "merge_bwd", (nt,),
        [_row(sv["o_m"]), _row(sv["o_f"]), _row(sv["o_s"])] + gate_specs + [_par(p["w_branch"]), _row(dmerged)],
        [_orow(t, 2 * LANES), _orow(t, 2 * LANES), _orow(t, S5_WIDTH), _orow(t, 3 * D_MODEL),
         _oacc((3, S5_WIDTH, D_MODEL))])

    def post_bwd_fn(st, u, cb, dr, wg, bg, do):
        _, vjp = jax.vjp(_s5_post, st, u, cb.astype(F32), dr, wg.astype(F32), bg)
        return vjp(do)

    u_spec = _row(proj, S5_WIDTH, O_U // S5_WIDTH)
    dstate, du1, dc_blk, g_d, g["w_glu"], g_bglu = _tile_call(
        post_bwd_fn, "s5_post_bwd", (nt,),
        [_row(sv["state"]), u_spec, _par(p["c_blk"]), _par(p["d_row"]), _par(p["w_glu"]), _par(p["b_glu"]), _row(do_s)],
        [_orow(t, 2 * S5_N), _orow(t, S5_WIDTH), _oacc((2 * S5_N, S5_WIDTH)), _oacc((1, S5_WIDTH)),
         _oacc((S5_WIDTH, S5_WIDTH)), _oacc((1, S5_WIDTH))])
    dbu, dlbar = landed("s5_scan_bwd", _scan_bwd(dstate, sv["state"], sv["lbar"], bsz, "s5_scan_bwd",
                                                 riding("s5_scan_bwd", g)[0]), 2)

    def bu_bwd_fn(u, bb, dbv, du_in):
        _, vjp = jax.vjp(bdot, u, bb)
        du, dbb = vjp(dbv)
        return du_in + du, dbb

    du, db_blk = _tile_call(bu_bwd_fn, "s5_bu_bwd", (nt,), [u_spec, _par(sv["b_blk"]), _row(dbu), _row(du1)],
                            [_orow(t, S5_WIDTH), _oacc((S5_WIDTH, 2 * S5_N))])
    dlr, dli = _from_scan_cols(dlbar)
    dbr, dbi = _from_scan_cols(db_blk)
    cts = (dlr.reshape(S5_N, 1), dli.reshape(S5_N, 1),
           _block_diag_in_t(dbr).reshape(S5_N, S5_GROUP_CH), _block_diag_in_t(dbi).reshape(S5_N, S5_GROUP_CH))
    g_lr, g_li, g_ls, g_br, g_bi = _s5_param_bwd_call(p, cts)
    dc_re, dc_im = _from_scan_cols(dc_blk.T)
    g["s5_lambda_re"] = g_lr.reshape(S5_GROUPS, S5_STATE)
    g["s5_lambda_im"] = g_li.reshape(S5_GROUPS, S5_STATE)
    g["s5_log_step"] = jnp.sum(g_ls.reshape(S5_GROUPS, S5_STATE), axis=1)
    g["s5_b_re"] = g_br.reshape(S5_GROUPS, S5_STATE, S5_GROUP_CH)
    g["s5_b_im"] = g_bi.reshape(S5_GROUPS, S5_STATE, S5_GROUP_CH)
    g["s5_c_re"] = _block_diag_out_t(dc_re.T)
    g["s5_c_im"] = -_block_diag_out_t(dc_im.T)
    g["s5_d"] = g_d.reshape(S5_GROUPS, S5_GROUP_CH)
    g["s5_b_glu"] = g_bglu.reshape(S5_WIDTH)

    r3 = lambda a: a.reshape(bsz, seq, a.shape[-1])
    dq_m, dk_m, dv_m = landed("attn_mla_bwd", _attn_bwd(
        r3(sv["q_m"]), r3(sv["k_m"]), r3(sv["v_m"]), 0, r3(sv["o_m"]), r3(do_m), sv["lse_m"], None, None,
        MLA_QK_DIM ** -0.5, "attn_mla_bwd", riding("attn_mla_bwd", g)[0]), 3)
    dq_f, dk_f, dv_f, dcq, dck = landed("attn_fox_bwd", _attn_bwd(
        r3(sv["q_f"]), r3(sv["k_f"]), r3(proj), O_FV // LANES, r3(sv["o_f"]), r3(do_f), sv["lse_f"], sv["cq"], sv["ck"],
        HEAD_V ** -0.5, "attn_fox_bwd", riding("attn_fox_bwd", g)[0]), 5)
    dcum = (dck + dcq.transpose(0, 1, 3, 2)).reshape(bsz * N_HEADS, seq)
    dz, dfb = _forget_bwd(dcum, sv["z"], sv["fb"], "forget_bwd")
    g["fox_f_bias"] = jnp.sum(dfb.reshape(bsz, N_HEADS), axis=0)
    dfl = jnp.pad(dz.reshape(bsz, N_HEADS, seq).transpose(0, 2, 1).reshape(t, N_HEADS), ((0, 0), (0, LANES - N_HEADS)))

    def pre_bwd_fn(pa, cosv, sinv, qlg, w_uq, kvlg, w_uk, w_uv, mqg, mkg, fqg, fkg, em, pm, rm, plm,
                   gq, gk, gv, gqf, gkf, gvf, gu, gfl, ggl):
        f = functools.partial(_mixer_pre, cos=cosv, sin=sinv, e_mat=em, p_mat=pm, r_mat=rm, pl_mat=plm)
        prim = (pa[:, O_CQ:O_CQ + MLA_Q_RANK], pa[:, O_CKV:O_CKV + MLA_KV_RANK], pa[:, O_KR:O_KR + LANES],
                pa[:, O_FQ:O_FQ + 256], pa[:, O_FK:O_FK + 256], qlg, w_uq.astype(F32), kvlg, w_uk.astype(F32),
                w_uv.astype(F32), mqg, mkg, fqg, fkg)
        _, vjp = jax.vjp(lambda *a: f(*a), *prim)
        dcq_, dckv, dkr, dfq, dfk, dqlg, dwuq, dkvlg, dwuk, dwuv, dmqg, dmkg, dfqg, dfkg = vjp((gq, gk, gv, gqf, gkf))
        zpad = jnp.zeros((pa.shape[0], PA - O_CQ - MLA_Q_RANK), F32)
        dproj = jnp.concatenate([dckv, dfq, dfk, gvf, gu, dkr, gfl, dcq_, zpad, ggl], axis=1)
        return dproj, dqlg, dwuq, dkvlg, dwuk, dwuv, dmqg, dmkg, dfqg, dfkg

    pre_params = [p["qlg"], p["w_uq"], p["kvlg"], p["w_uk"], p["w_uv"], p["mqg"], p["mkg"], p["fqg"], p["fkg"],
                  e_mat, p_mat, r_mat, pl_mat]
    cts_in = [dq_m.reshape(t, QKW), dk_m.reshape(t, QKW), dv_m.reshape(t, 2 * LANES), dq_f.reshape(t, QKW),
              dk_f.reshape(t, QKW), dv_f.reshape(t, 2 * LANES), du, dfl, dgl]
    (dproj, g_qlg, g_wuq, g_kvlg, g_wuk, g_wuv, g_mqg, g_mkg, g_fqg, g_fkg) = _tile_call(
        pre_bwd_fn, "mixer_pre_bwd", (nt,),
        [_row(proj, PA), _row(cos), _row(sin)] + [_par(a) for a in pre_params] + [_row(a) for a in cts_in],
        [_orow(t, PW, BF16), _oacc((1, MLA_Q_RANK)), _oacc((MLA_Q_RANK, QKW)), _oacc((1, MLA_KV_RANK)),
         _oacc((MLA_KV_RANK, QKW)), _oacc((MLA_KV_RANK, N_HEADS * HEAD_V)), _oacc((1, QKW)), _oacc((1, QKW)),
         _oacc((1, QKW)), _oacc((1, QKW))])
    g["q_lat_norm_g"] = g_qlg.reshape(MLA_Q_RANK)
    g["kv_lat_norm_g"] = g_kvlg.reshape(MLA_KV_RANK)
    heads = lambda a, w: jnp.sum(a.reshape(N_HEADS, HP)[:, :w], axis=0)
    g["mla_q_norm_g"], g["mla_k_norm_g"] = heads(g_mqg, MLA_QK_DIM), heads(g_mkg, MLA_QK_DIM)
    g["fox_q_norm_g"], g["fox_k_norm_g"] = heads(g_fqg, HEAD_V), heads(g_fkg, HEAD_V)
    g["w_uq"] = g_wuq.reshape(MLA_Q_RANK, N_HEADS, HP)[..., :MLA_QK_DIM].reshape(MLA_Q_RANK, N_HEADS * MLA_QK_DIM)
    g["w_ukv"] = jnp.concatenate([g_wuk.reshape(MLA_KV_RANK, N_HEADS, HP)[..., :MLA_NOPE_DIM],
                                  g_wuv.reshape(MLA_KV_RANK, N_HEADS, HEAD_V)], axis=-1).reshape(MLA_KV_RANK, QKW)

    g["w_in"] = _mm(sv["h"], dproj, "tn", "mm_in_dw", out_dtype=BF16)
    dh = landed("mm_in_dx", _mm(dproj, p["w_in"], "nt", "mm_in_dx", rider=riding("mm_in_dx", g)[0]), 1)[0]
    dx, g["attn_g"] = _tile_call(norm_bwd_fn, "norm_bwd", (nt,),
                                 [_row(sv["x"]), _par(p["attn_g"]), _row(dh), _row(d1)],
                                 [_orow(t, D_MODEL), _oacc((1, D_MODEL))])
    return dx, g, lands, carry_lands


def _row_tile(r, c, max_elems):
    if r * c <= max_elems:
        return r
    best = None
    for d in range(8, r, 8):
        if r % d == 0 and d * c <= max_elems:
            best = d
    assert best is not None, (r, c)
    return best


def _sum8(land, name):
    _, r, c = land.shape
    tr = _row_tile(r, N_DEV * c, 1 << 20)

    def fn(lv):
        acc = lv[0].astype(F32)
        for i in range(1, N_DEV):
            acc = acc + lv[i].astype(F32)
        return acc

    return _tile_call(fn, name, (r // tr,), [(land, (N_DEV, tr, c), lambda i: (0, i, 0))],
                      [((r, c), F32, (tr, c), lambda i: (i, 0), None)])[0]


def _adamw_layer(w, g, m, v, l, prev, name):
    _, r, c = w.shape
    tr = _row_tile(r, c, 1 << 19)
    lay = pl.BlockSpec((None, tr, c), lambda i: (l, i, 0))

    def body(w_ref, g_ref, m_ref, v_ref, *rest):
        g_out, d_out, m_out, v_out = rest[-4:]
        gv = g_ref[...]
        d_out[...], m_out[...], v_out[...] = _adamw(w_ref[...], gv, m_ref[...], v_ref[...])
        g_out[...] = gv

    in_specs = [lay, pl.BlockSpec((tr, c), lambda i: (i, 0)), lay, lay]
    args = [w, g, m, v]
    aliases = {}
    if prev is not None:
        in_specs += [pl.BlockSpec(memory_space=pl.ANY)] * 4
        args += list(prev)
        aliases = {4 + k: k for k in range(4)}
    return _pcall(body, name=name, grid=(r // tr,), in_specs=in_specs, out_specs=[lay] * 4,
                  out_shape=[jax.ShapeDtypeStruct(w.shape, F32)] * 4, aliases=aliases)(*args)


SMALL_NAMES = ("attn_norm_g", "q_lat_norm_g", "kv_lat_norm_g", "mla_q_norm_g", "mla_k_norm_g", "fox_q_norm_g",
               "fox_k_norm_g", "fox_f_bias", "s5_lambda_re", "s5_lambda_im", "s5_b_re", "s5_b_im", "s5_c_re", "s5_c_im",
               "s5_d", "s5_log_step", "s5_b_glu", "ffn_norm_g")
BIG_NAMES = ("w_in", "w_uq", "w_ukv", "s5_w_glu", "w_branch", "w_out", "w_up", "ffn_conv_w", "w_down")
ALL_NAMES = ("attn_norm_g", "w_in", "q_lat_norm_g", "w_uq", "kv_lat_norm_g", "w_ukv", "mla_q_norm_g", "mla_k_norm_g",
             "fox_q_norm_g", "fox_k_norm_g", "fox_f_bias", "s5_lambda_re", "s5_lambda_im", "s5_b_re", "s5_b_im",
             "s5_c_re", "s5_c_im", "s5_d", "s5_log_step", "s5_w_glu", "s5_b_glu", "w_branch", "w_out", "ffn_norm_g",
             "w_up", "ffn_conv_w", "w_down")


def _pack(arrs):
    tile = 8 * LANES
    parts = []
    for a in arrs:
        n = int(np.prod(a.shape))
        tot = -(-n // tile) * tile
        parts.append(jnp.pad(a.reshape(-1), (0, tot - n)).reshape(tot // LANES, LANES))
    rows = sum(a.shape[0] for a in parts)
    parts.append(jnp.zeros((-rows % 256, LANES), parts[0].dtype))
    return jnp.concatenate(parts, axis=0)


def _unpack(packed, like):
    tile = 8 * LANES
    out, pos = [], 0
    for a in like:
        n = int(np.prod(a.shape))
        rows = -(-n // tile) * 8
        out.append(packed[pos:pos + rows].reshape(-1)[:n].reshape(a.shape))
        pos += rows
    return out


def _rope_tables(positions, t):
    pos = positions.reshape(t, 1)
    inv = _inv_freq_row()

    def fn(pv, iv):
        ang = pv.astype(F32) * iv
        return jnp.cos(ang), jnp.sin(ang)

    return _tile_call(fn, "rope_tables", (t // TM,), [_row(pos), _par(inv)], [_orow(t, QKW), _orow(t, QKW)])


def _loss_call(x1, f, target):
    t = x1.shape[0]

    def fn(xv, fv, tv):
        e = xv + fv - tv
        per_tok = jnp.sum(e * e, axis=1, keepdims=True) * (1.0 / D_MODEL)
        return 0.5 * jnp.sum(per_tok, axis=0, keepdims=True), e * (1.0 / D_MODEL)

    return _tile_call(fn, "loss", (t // TM,), [_row(x1), _row(f), _row(target)], [_oacc((1, 1)), _orow(t, D_MODEL)])


def _local_shards(l, w):
    return {
        "w_in": _pad_in_cols(w["w_in"][l]).astype(BF16),
        "w_uq": w["w_uq"][l].reshape(-1, N_HEADS * MLA_QK_DIM).astype(BF16),
        "w_ukv": w["w_ukv"][l].reshape(-1, QKW).astype(BF16),
        "w_glu": w["s5_w_glu"][l].astype(BF16),
        "w_branch": w["w_branch"][l].astype(BF16),
        "w_out": w["w_out"][l].astype(BF16),
        "w_up": _pad_ff(w["w_up"][l], 1).astype(BF16),
        "conv": _pad_ff(w["ffn_conv_w"][l], 1),
        "w_down": _pad_ff(w["w_down"][l], 0).astype(BF16),
    }


GATHERED_SHAPES = {"w_in": (D_MODEL, PW), "w_uq": (MLA_Q_RANK, N_HEADS * MLA_QK_DIM), "w_ukv": (MLA_KV_RANK, QKW),
                   "w_glu": (S5_WIDTH, S5_WIDTH), "w_branch": (3, S5_WIDTH, D_MODEL), "w_out": (D_MODEL, D_MODEL),
                   "w_up": (D_MODEL, UP_W), "conv": (3, UP_W), "w_down": (FF_PW, D_MODEL)}
FFN_WEIGHTS = ("w_up", "conv", "w_down")
MIXER_WEIGHTS = ("w_in", "w_uq", "w_ukv", "w_glu", "w_out", "w_branch")
FWD_RIDERS_OWN = {"attn_mla_fwd": ("w_up", "conv"), "s5_scan_fwd": ("w_down",)}
FWD_RIDERS_NEXT = {"attn_fox_fwd": ("w_in", "w_uq", "w_ukv", "w_glu"), "mixer_pre": ("w_out", "w_branch")}
BWD_RIDERS = {"s5_scan_bwd": ("w_down",), "attn_mla_bwd": ("w_up",),
              "attn_fox_bwd": ("w_out", "w_branch", "w_glu", "conv"), "mm_in_dx": ("w_uq", "w_ukv")}
UPDATED_TRANSPOSED = ("w_up",)
WEIGHT_OF = {"w_in": "w_in", "w_uq": "w_uq", "w_ukv": "w_ukv", "w_glu": "s5_w_glu", "w_branch": "w_branch",
             "w_out": "w_out", "w_up": "w_up", "conv": "ffn_conv_w", "w_down": "w_down"}


def _unshard_layout(nm, g2):
    if nm == "w_in":
        return _unpad_in_cols(g2)
    if nm == "w_uq":
        return g2.reshape(-1, N_HEADS, MLA_QK_DIM)
    if nm == "w_ukv":
        return g2.reshape(-1, N_HEADS, MLA_NOPE_DIM + HEAD_V)
    if nm in ("w_up", "conv"):
        return _unpad_ff(g2, 1)
    if nm == "w_down":
        return _unpad_ff(g2, 0)
    return g2


def kernel(x, positions, attn_norm_g, w_in, q_lat_norm_g, w_uq, kv_lat_norm_g, w_ukv, mla_q_norm_g, mla_k_norm_g, fox_q_norm_g, fox_k_norm_g, fox_f_bias, s5_lambda_re, s5_lambda_im, s5_b_re, s5_b_im, s5_c_re, s5_c_im, s5_d, s5_log_step, s5_w_glu, s5_b_glu, w_branch, w_out, ffn_norm_g, w_up, ffn_conv_w, w_down, loss_target, m_attn_norm_g, m_w_in, m_q_lat_norm_g, m_w_uq, m_kv_lat_norm_g, m_w_ukv, m_mla_q_norm_g, m_mla_k_norm_g, m_fox_q_norm_g, m_fox_k_norm_g, m_fox_f_bias, m_s5_lambda_re, m_s5_lambda_im, m_s5_b_re, m_s5_b_im, m_s5_c_re, m_s5_c_im, m_s5_d, m_s5_log_step, m_s5_w_glu, m_s5_b_glu, m_w_branch, m_w_out, m_ffn_norm_g, m_w_up, m_ffn_conv_w, m_w_down, v_attn_norm_g, v_w_in, v_q_lat_norm_g, v_w_uq, v_kv_lat_norm_g, v_w_ukv, v_mla_q_norm_g, v_mla_k_norm_g, v_fox_q_norm_g, v_fox_k_norm_g, v_fox_f_bias, v_s5_lambda_re, v_s5_lambda_im, v_s5_b_re, v_s5_b_im, v_s5_c_re, v_s5_c_im, v_s5_d, v_s5_log_step, v_s5_w_glu, v_s5_b_glu, v_w_branch, v_w_out, v_ffn_norm_g, v_w_up, v_ffn_conv_w, v_w_down):
    args = locals()
    w = {n: args[n] for n in ALL_NAMES}
    m = {n: args["m_" + n] for n in ALL_NAMES}
    v = {n: args["v_" + n] for n in ALL_NAMES}
    depth = w_in.shape[0]
    bsz, seq, _ = x.shape
    t = bsz * seq
    consts = _constants()
    tabs = _rope_tables(positions, t)
    small = {n: w[n] for n in SMALL_NAMES}

    all_names = list(GATHERED_SHAPES)
    all_shards = [_local_shards(l, w) for l in range(depth)]

    def gather_xfer(l, names):
        sub = {n: all_shards[l][n] for n in names}
        order, plan = _gather_plan(sub)
        return _Xfer([sub[n] for n in order], [(GATHERED_SHAPES[n], sub[n].dtype) for n in order], plan)

    def scatter_xfer(g, names):
        shapes = {n: _local_shards_shape(n) for n in names}
        return _Xfer([g[n] for n in names], [((N_DEV,) + shapes[n], g[n].dtype) for n in names],
                     _scatter_plan(list(names), shapes))

    xs = x.reshape(t, D_MODEL)
    f_prev = None
    saved, params = [], []
    gathered = dict(zip(MIXER_WEIGHTS, _exchange("gather_weights", gather_xfer(0, MIXER_WEIGHTS))))
    for l in range(depth):
        p = _prep_layer(l, gathered, small, consts)
        riders = {host: (gather_xfer(l, names), names) for host, names in FWD_RIDERS_OWN.items()}
        if l + 1 < depth:
            riders.update({host: (gather_xfer(l + 1, names), names) for host, names in FWD_RIDERS_NEXT.items()})
        xs, f_prev, sv, gathered = _layer_fwd(xs, f_prev, p, tabs, consts, bsz, riders)
        saved.append(sv)
        params.append(p)

    loss_part, dy = _loss_call(xs, f_prev, loss_target.reshape(t, D_MODEL))
    loss = lax.psum(loss_part[0, 0], ("x", "y", "c"))

    small_grads = {n: [None] * depth for n in SMALL_NAMES}
    big = {n: None for n in BIG_NAMES}

    def finish(l, lands):
        for n in all_names:
            wn, ld = WEIGHT_OF[n], lands[n]
            ld2 = ld.reshape(N_DEV, -1, ld.shape[-1])
            gsum = _unshard_layout(n, _sum8(ld2, "sum_" + n).reshape(_local_shards_shape(n)))
            if n in UPDATED_TRANSPOSED:
                gsum = gsum.T
            c = gsum.shape[-1]
            three = lambda a: (jnp.swapaxes(a, 1, 2) if n in UPDATED_TRANSPOSED else a).reshape(depth, -1, c)
            big[wn] = _adamw_layer(three(w[wn]), gsum.reshape(-1, c), three(m[wn]), three(v[wn]), l, big[wn],
                                   "adamw_" + n)

    def scatter(grads):
        return scatter_xfer(grads, list(grads))

    dx = dy
    carry, lands_above = {}, None
    for l in reversed(range(depth)):
        dx, g, lands, carry_lands = _layer_bwd(dx, params[l], saved[l], tabs, consts, bsz, scatter, carry)
        if lands_above is not None:
            finish(l + 1, {**lands_above, **carry_lands})
        carry, lands_above = {"w_in": g["w_in"]}, lands
        small_grads["attn_norm_g"][l] = g["attn_g"].reshape(D_MODEL)
        small_grads["ffn_norm_g"][l] = g["ffn_g"].reshape(D_MODEL)
        for n in SMALL_NAMES:
            if n not in ("attn_norm_g", "ffn_norm_g"):
                small_grads[n][l] = g[n]
    finish(0, {**lands_above, "w_in": _exchange("scatter_grads", scatter(carry))[0]})

    sg = [jnp.stack(small_grads[n]).reshape(w[n].shape) for n in SMALL_NAMES]
    packed = _pack(sg)
    land = _exchange("gather_small_grads", _Xfer([packed], [((N_DEV,) + packed.shape, F32)],
                                                 _scatter_plan(["small"], {})))[0]
    gs = _sum8(land, "sum_small")
    one = lambda names_of: _pack([names_of[n] for n in SMALL_NAMES])[None]
    small_res = _adamw_layer(one(w), gs, one(m), one(v), 0, None, "adamw_small")
    like = [w[n] for n in SMALL_NAMES]
    small_out = [dict(zip(SMALL_NAMES, _unpack(a[0], like))) for a in small_res]

    def out_of(kind, n):
        if n in SMALL_NAMES:
            return small_out[kind][n]
        if n in UPDATED_TRANSPOSED:
            return jnp.swapaxes(big[n][kind], 1, 2)
        return big[n][kind].reshape(w[n].shape)

    outs = [loss, dx.reshape(bsz, seq, D_MODEL)]
    for kind in range(4):
        outs += [out_of(kind, n) for n in ALL_NAMES]
    return tuple(outs)


def _local_shards_shape(nm):
    return {"w_in": (D_MODEL // N_DEV, PW), "w_uq": (MLA_Q_RANK // N_DEV, N_HEADS * MLA_QK_DIM),
            "w_ukv": (MLA_KV_RANK // N_DEV, QKW), "w_glu": (S5_WIDTH // N_DEV, S5_WIDTH),
            "w_branch": (3, S5_WIDTH, LANES), "w_out": (D_MODEL // N_DEV, D_MODEL), "w_up": (D_MODEL, 2 * FF_BLK),
            "conv": (3, 2 * FF_BLK), "w_down": (FF_BLK, D_MODEL)}[nm]
```

```python
import functools
import math

import numpy as np
import jax
import jax.numpy as jnp
from jax import lax
from jax.experimental import pallas as pl
from jax.experimental.pallas import tpu as pltpu

F32, BF16 = jnp.float32, jnp.bfloat16

D_MODEL = 1024
N_DEV = 8
MLA_Q_RANK, MLA_KV_RANK, MLA_ROPE_DIM, MLA_NOPE_DIM, MLA_QK_DIM = 384, 256, 32, 64, 96
N_HEADS, HEAD_V = 4, 64
S5_GROUPS, S5_GROUP_CH, S5_STATE, S5_WIDTH = 16, 16, 64, 256
S5_N = S5_GROUPS * S5_STATE
D_FF = 2816
D_IN = 4772
ROPE_THETA = 10000.0
NORM_EPS = 1e-6
NEG_INF = -1e30
ADAM_LR, ADAM_B1, ADAM_B2, ADAM_EPS, ADAM_WD, ADAM_STEP = 0.001, 0.9, 0.999, 1e-08, 0.01, 10

VMEM_LIMIT_BYTES = 56 * 1024 * 1024
LANES = 128
TM = 256
TQ = 512
QF = 128
QC = 512
HP = 128
QKW = N_HEADS * HP

PW = 5120
PA = 2048
IN_SEGS = ((0, 384, 256),
           (256, 672, 256),
           (512, 928, 256),
           (768, 1184, 256),
           (1024, 1444, 256),
           (1280, 640, 32),
           (1408, 1440, 4),
           (1536, 0, 384),
           (2048, 1700, 3072))
O_CKV, O_FQ, O_FK, O_FV, O_U, O_KR, O_FL, O_CQ = 0, 256, 512, 768, 1024, 1280, 1408, 1536

FF_BLK = 384
FF_HALF = D_FF // 8
FF_PW = 8 * FF_BLK
UP_W = 2 * FF_PW

NN = (((1,), (0,)), ((), ()))
NT = (((1,), (1,)), ((), ()))
TN = (((0,), (0,)), ((), ()))


def _pcall(body, *, name, grid, in_specs, out_specs, out_shape, scratch_shapes=(), rider=None, aliases=None):
    params = pltpu.CompilerParams(dimension_semantics=("arbitrary",) * len(grid), vmem_limit_bytes=VMEM_LIMIT_BYTES)
    if rider is None:
        return pl.pallas_call(body, name=name, grid=grid, in_specs=in_specs, out_specs=out_specs, out_shape=out_shape,
                              scratch_shapes=list(scratch_shapes), input_output_aliases=aliases or {},
                              compiler_params=params)
    assert not aliases
    in_specs, out_specs, out_shape = list(in_specs), list(out_specs), list(out_shape)
    n_in, n_out, n_scr, nx = len(in_specs), len(out_specs), len(scratch_shapes), rider.n
    any_spec = pl.BlockSpec(memory_space=pl.ANY)

    def at_step(last):
        conds = [pl.program_id(a) == (g - 1 if last else 0) for a, g in enumerate(grid)]
        return functools.reduce(jnp.logical_and, conds)

    def body_with_rider(*refs):
        ins, xsrc = refs[:n_in], refs[n_in:n_in + nx]
        outs, xout = refs[n_in + nx:n_in + nx + n_out], refs[n_in + nx + n_out:n_in + 2 * nx + n_out]
        scr = refs[n_in + 2 * nx + n_out:n_in + 2 * nx + n_out + n_scr]
        sems = refs[n_in + 2 * nx + n_out + n_scr:]

        @pl.when(at_step(False))
        def _():
            rider.start(rider.copies(xsrc, xout, *sems))

        body(*ins, *outs, *scr)

        @pl.when(at_step(True))
        def _():
            rider.wait(rider.copies(xsrc, xout, *sems))

    call = pl.pallas_call(body_with_rider, name=name, grid=grid, in_specs=in_specs + [any_spec] * nx,
                          out_specs=out_specs + [any_spec] * nx,
                          out_shape=out_shape + [jax.ShapeDtypeStruct(s, d) for s, d in rider.out_shapes],
                          scratch_shapes=list(scratch_shapes) + rider.sems(), compiler_params=params)
    return lambda *args: call(*args, *rider.srcs)


def _tile_call(fn, name, grid, ins, outs, rider=None):
    n_in = len(ins)

    def body(*refs):
        res = fn(*[r[...] for r in refs[:n_in]])
        if not isinstance(res, (tuple, list)):
            res = (res,)
        assert len(res) == len(outs), (name, len(res), len(outs))
        for r, v, o in zip(refs[n_in:], res, outs):
            v = v.astype(r.dtype)
            if o[4] is None:
                r[...] = v
            else:
                first = functools.reduce(jnp.logical_and, [pl.program_id(a) == 0 for a in o[4]])

                @pl.when(first)
                def _():
                    r[...] = v

                @pl.when(jnp.logical_not(first))
                def _():
                    r[...] += v

    res = _pcall(body, name=name, grid=grid,
                 in_specs=[pl.BlockSpec(b, m) for _, b, m in ins],
                 out_specs=[pl.BlockSpec(o[2], o[3]) for o in outs],
                 out_shape=[jax.ShapeDtypeStruct(o[0], o[1]) for o in outs], rider=rider)(*[a for a, _, _ in ins])
    return res


def _row(a, width=None, col_block=0, tm=TM):
    w = a.shape[1] if width is None else width
    return (a, (tm, w), lambda i, c=col_block: (i, c))


def _par(a):
    nd = a.ndim
    return (a, a.shape, lambda i, nd=nd: (0,) * nd)


def _orow(t, w, dtype=F32, tm=TM):
    return ((t, w), dtype, (tm, w), lambda i: (i, 0), None)


def _oacc(shape):
    nd = len(shape)
    return (tuple(shape), F32, tuple(shape), lambda i, nd=nd: (0,) * nd, (0,))


def _pick(n, target):
    best = None
    for d in range(LANES, min(n, target) + 1, LANES):
        if n % d == 0:
            best = d
    return n if best is None else best


MM_VMEM_BUDGET = 44 * 1024 * 1024


def _mm_tiles(m, n, k, sa, sb, so, tm, tn, tk):
    for cm, cn in ((1024, 2048), (1024, 1024), (512, 2048), (512, 1024), (512, 512), (256, 1024), (256, 512)):
        pm, pn = _pick(m, cm), _pick(n, cn)
        if 2 * (pm * k * sa + k * pn * sb + pm * pn * so) <= MM_VMEM_BUDGET:
            return pm, pn, k
    return _pick(m, tm), _pick(n, tn), _pick(k, tk)


def _mm(a, b, mode, name, out_dtype=F32, tm=1024, tn=2048, tk=1024, rider=None):
    if mode == "nn":
        (m, k), (_, n) = a.shape, b.shape
    elif mode == "nt":
        (m, k), (n, _) = a.shape, b.shape
    else:
        (k, m), (_, n) = a.shape, b.shape
    tm, tn, tk = _mm_tiles(m, n, k, a.dtype.itemsize, b.dtype.itemsize, jnp.dtype(out_dtype).itemsize, tm, tn, tk)
    nk = k // tk
    dims = {"nn": NN, "nt": NT, "tn": TN}[mode]
    a_spec = (pl.BlockSpec((tk, tm), lambda i, j, l: (l, i)) if mode == "tn"
              else pl.BlockSpec((tm, tk), lambda i, j, l: (i, l)))
    b_spec = (pl.BlockSpec((tn, tk), lambda i, j, l: (j, l)) if mode == "nt"
              else pl.BlockSpec((tk, tn), lambda i, j, l: (l, j)))

    def body(a_ref, b_ref, o_ref, acc_ref):
        part = lax.dot_general(a_ref[...].astype(BF16), b_ref[...].astype(BF16), dims, preferred_element_type=F32)
        if nk == 1:
            o_ref[...] = part.astype(o_ref.dtype)
        else:
            l = pl.program_id(2)

            @pl.when(l == 0)
            def _():
                acc_ref[...] = part

            @pl.when(l > 0)
            def _():
                acc_ref[...] += part

            @pl.when(l == nk - 1)
            def _():
                o_ref[...] = acc_ref[...].astype(o_ref.dtype)

    res = _pcall(body, name=name, grid=(m // tm, n // tn, nk), in_specs=[a_spec, b_spec],
                 out_specs=[pl.BlockSpec((tm, tn), lambda i, j, l: (i, j))],
                 out_shape=[jax.ShapeDtypeStruct((m, n), out_dtype)],
                 scratch_shapes=[pltpu.VMEM((tm, tn) if nk > 1 else (8, LANES), F32)], rider=rider)(a, b)
    return res[0] if rider is None else res


def _dot(x, w, dims):
    return lax.dot_general(x.astype(BF16), w.astype(BF16), dims, preferred_element_type=F32)


@jax.custom_vjp
def bdot(x, w):
    return _dot(x, w, NN)


def _bdot_fwd(x, w):
    return _dot(x, w, NN), (x, w)


def _bdot_bwd(res, g):
    x, w = res
    return _dot(g, w, NT), _dot(x, g, TN)


bdot.defvjp(_bdot_fwd, _bdot_bwd)


def _split3(x):
    x1 = x.astype(BF16)
    r1 = x - x1.astype(F32)
    x2 = r1.astype(BF16)
    x3 = (r1 - x2.astype(F32)).astype(BF16)
    return x1, x2, x3


def _cdot(x, m, dims):
    return sum(lax.dot_general(p, m, dims, preferred_element_type=F32) for p in _split3(x))


def _block_dot(x, m, transposed):
    k_in = m.shape[1] if transposed else m.shape[0]
    dims = NT if transposed else NN
    return jnp.concatenate([_cdot(x[:, b * k_in:(b + 1) * k_in], m, dims) for b in range(x.shape[1] // k_in)], axis=1)


@jax.custom_vjp
def cdot(x, m):
    return _block_dot(x, m, False)


def _cdot_fwd(x, m):
    return _block_dot(x, m, False), m


def _cdot_bwd(m, g):
    return _block_dot(g, m, True), jnp.zeros_like(m)


cdot.defvjp(_cdot_fwd, _cdot_bwd)


@jax.custom_vjp
def tile_heads(y):
    return jnp.concatenate([y] * N_HEADS, axis=1)


def _tile_heads_fwd(y):
    return jnp.concatenate([y] * N_HEADS, axis=1), None


def _tile_heads_bwd(_, g):
    return (sum(g[:, h * HP:(h + 1) * HP] for h in range(N_HEADS)),)


tile_heads.defvjp(_tile_heads_fwd, _tile_heads_bwd)


def _shift_rows_impl(x, s, reverse):
    n = x.shape[0]
    idx = lax.broadcasted_iota(jnp.int32, x.shape, 0)
    if reverse:
        return jnp.where(idx < n - s, pltpu.roll(x, n - s, 0), 0.0)
    return jnp.where(idx >= s, pltpu.roll(x, s, 0), 0.0)


@functools.partial(jax.custom_vjp, nondiff_argnums=(1,))
def shift_rows(x, s):
    return _shift_rows_impl(x, s, False)


def _shift_rows_fwd(x, s):
    return _shift_rows_impl(x, s, False), None


def _shift_rows_bwd(s, _, g):
    return (_shift_rows_impl(g, s, True),)


shift_rows.defvjp(_shift_rows_fwd, _shift_rows_bwd)


def _rms(x, g):
    return x * lax.rsqrt(jnp.mean(x * x, axis=-1, keepdims=True) + NORM_EPS) * g


def _head_rms(x, e_mat, inv_n, g):
    ms = cdot(x * x, e_mat) * inv_n
    return x * lax.rsqrt(ms + NORM_EPS) * g


def _rope(y, p_mat, cos, sin):
    return y * cos + cdot(y, p_mat) * sin


def _constants():
    e = np.ones((HP, HP), np.float32)
    p = np.zeros((HP, HP), np.float32)
    r = np.zeros((LANES, HP), np.float32)
    pl64 = np.zeros((2 * HEAD_V, 2 * HP), np.float32)
    half = MLA_ROPE_DIM // 2
    for i in range(half):
        x1, x2 = MLA_NOPE_DIM + i, MLA_NOPE_DIM + half + i
        p[x2, x1] = -1.0
        p[x1, x2] = 1.0
    for i in range(MLA_ROPE_DIM):
        r[i, MLA_NOPE_DIM + i] = 1.0
    for h in range(2):
        for i in range(HEAD_V):
            pl64[h * HEAD_V + i, h * HP + i] = 1.0
    return tuple(jnp.asarray(a, BF16) for a in (e, p, r, pl64))


def _inv_freq_row():
    inv = ROPE_THETA ** (-jnp.arange(0, MLA_ROPE_DIM, 2, dtype=F32) / MLA_ROPE_DIM)
    head = jnp.concatenate([jnp.zeros((MLA_NOPE_DIM,), F32), inv, inv, jnp.zeros((HP - MLA_QK_DIM,), F32)])
    return jnp.tile(head, N_HEADS).reshape(1, QKW)


def _mixer_pre(c_q, c_kv, kr, fq, fk, qlg, w_uq, kvlg, w_uk, w_uv, mqg, mkg, fqg, fkg, cos, sin, e_mat, p_mat, r_mat, pl_mat):
    q = bdot(_rms(c_q, qlg), w_uq)
    ckvn = _rms(c_kv, kvlg)
    k = bdot(ckvn, w_uk) + tile_heads(cdot(kr, r_mat))
    v = bdot(ckvn, w_uv)
    q = _rope(_head_rms(q, e_mat, 1.0 / MLA_QK_DIM, mqg), p_mat, cos, sin)
    k = _rope(_head_rms(k, e_mat, 1.0 / MLA_QK_DIM, mkg), p_mat, cos, sin)
    qf = _head_rms(cdot(fq, pl_mat), e_mat, 1.0 / HEAD_V, fqg)
    kf = _head_rms(cdot(fk, pl_mat), e_mat, 1.0 / HEAD_V, fkg)
    return q, k, v, qf, kf


def _s5_post(state, u, c_blk, d_row, w_glu, b_glu):
    y = bdot(state, c_blk) + d_row * u
    y = jax.nn.gelu(y)
    return y * jax.nn.sigmoid(bdot(y, w_glu) + b_glu)


def _merge(o_a, o_b, o_c, g_a, g_b, g_c, w_a, w_b, w_c):
    return (jax.nn.sigmoid(g_a) * bdot(o_a, w_a) + jax.nn.sigmoid(g_b) * bdot(o_b, w_b)
            + jax.nn.sigmoid(g_c) * bdot(o_c, w_c))


def _ffn_mid(gate, val, wg0, wg1, wg2, wv0, wv1, wv2):
    cg = wg0 * shift_rows(gate, 2) + wg1 * shift_rows(gate, 1) + wg2 * gate
    cv = wv0 * shift_rows(val, 2) + wv1 * shift_rows(val, 1) + wv2 * val
    return jax.nn.silu(cg) * cv


def _s5_params(lam_re, lam_im, log_step, b_re, b_im):
    step = jnp.exp(log_step)
    zr, zi = lam_re * step, lam_im * step
    mag = jnp.exp(zr)
    lr, li = mag * jnp.cos(zi), mag * jnp.sin(zi)
    nr, ni = lr - 1.0, li
    den = lam_re * lam_re + lam_im * lam_im
    cr = (nr * lam_re + ni * lam_im) / den
    ci = (ni * lam_re - nr * lam_im) / den
    return lr, li, cr * b_re - ci * b_im, cr * b_im + ci * b_re


def _adamw(w, g, m, v):
    m = ADAM_B1 * m + (1.0 - ADAM_B1) * g
    v = ADAM_B2 * v + (1.0 - ADAM_B2) * (g * g)
    m_hat = m / (1.0 - ADAM_B1 ** ADAM_STEP)
    v_hat = v / (1.0 - ADAM_B2 ** ADAM_STEP)
    delta = -ADAM_LR * (m_hat / (jnp.sqrt(v_hat) + ADAM_EPS) + ADAM_WD * w)
    return delta, m, v


SCAN_W = 256


def _scan_seq(xr, xi, ar, ai, reverse):
    n = xr.shape[0]
    s = 1
    while s < n:
        sr, si = _shift_rows_impl(xr, s, reverse), _shift_rows_impl(xi, s, reverse)
        xr, xi = xr + ar * sr - ai * si, xi + ar * si + ai * sr
        ar, ai = ar * ar - ai * ai, 2.0 * ar * ai
        s *= 2
    return xr, xi


def _scan_fwd(bu, lbar, bsz, name, rider=None):
    t = bu.shape[0]
    seq = t // bsz
    nb = bu.shape[1] // (2 * SCAN_W)

    def fn(b, a):
        xr, xi = _scan_seq(b[:, :SCAN_W], b[:, SCAN_W:], a[:, :SCAN_W], a[:, SCAN_W:], False)
        return jnp.concatenate([xr, xi], axis=1)

    blk = (seq, 2 * SCAN_W)
    return _tile_call(fn, name, (bsz, nb),
                      [(bu, blk, lambda i, j: (i, j)), (lbar, (1, 2 * SCAN_W), lambda i, j: (0, j))],
                      [((t, bu.shape[1]), F32, blk, lambda i, j: (i, j), None)], rider=rider)


def _scan_bwd(dstate, state, lbar, bsz, name, rider=None):
    t = state.shape[0]
    seq = t // bsz
    nb = state.shape[1] // (2 * SCAN_W)

    def fn(g, x, a):
        ar, ai = a[:, :SCAN_W], a[:, SCAN_W:]
        gr, gi = _scan_seq(g[:, :SCAN_W], g[:, SCAN_W:], ar, -ai, True)
        pr, pi = _shift_rows_impl(x[:, :SCAN_W], 1, False), _shift_rows_impl(x[:, SCAN_W:], 1, False)
        dar = jnp.sum(gr * pr + gi * pi, axis=0, keepdims=True)
        dai = jnp.sum(gi * pr - gr * pi, axis=0, keepdims=True)
        return jnp.concatenate([gr, gi], axis=1), jnp.concatenate([dar, dai], axis=1)

    blk = (seq, 2 * SCAN_W)
    return _tile_call(fn, name, (nb, bsz),
                      [(dstate, blk, lambda j, i: (i, j)), (state, blk, lambda j, i: (i, j)),
                       (lbar, (1, 2 * SCAN_W), lambda j, i: (0, j))],
                      [((t, state.shape[1]), F32, blk, lambda j, i: (i, j), None),
                       ((1, state.shape[1]), F32, (1, 2 * SCAN_W), lambda j, i: (0, j), (1,))], rider=rider)


def _shift_lanes(x, s, reverse):
    n = x.shape[1]
    idx = lax.broadcasted_iota(jnp.int32, x.shape, 1)
    if reverse:
        return jnp.where(idx < n - s, pltpu.roll(x, n - s, 1), 0.0)
    return jnp.where(idx >= s, pltpu.roll(x, s, 1), 0.0)


def _cumsum_lanes(x, reverse):
    s = 1
    while s < x.shape[1]:
        x = x + _shift_lanes(x, s, reverse)
        s *= 2
    return x


def _forget_fwd(z, bias, name):
    def fn(zv, bv):
        x = zv + bv
        logf = jnp.minimum(x, 0.0) - jnp.log(1.0 + jnp.exp(-jnp.abs(x)))
        return _cumsum_lanes(logf, False)

    return _tile_call(fn, name, (1,), [_par(z), _par(bias)],
                      [(z.shape, F32, z.shape, lambda i: (0, 0), None)])[0]


def _forget_bwd(dc, z, bias, name):
    def fn(dcv, zv, bv):
        x = zv + bv
        dlogf = _cumsum_lanes(dcv, True)
        dz = dlogf * jax.nn.sigmoid(-x)
        return dz, jnp.sum(dz, axis=1, keepdims=True)

    rows = z.shape[0]
    return _tile_call(fn, name, (1,), [_par(dc), _par(z), _par(bias)],
                      [(z.shape, F32, z.shape, lambda i: (0, 0), None),
                       ((rows, 1), F32, (rows, 1), lambda i: (0, 0), None)])


def _col(v2, x):
    idx = lax.broadcasted_iota(jnp.int32, v2.shape, 1)
    return jnp.sum(jnp.where(idx == x, v2, 0.0), axis=1, keepdims=True)


def _two_cols(c0, c1):
    idx = lax.broadcasted_iota(jnp.int32, (c0.shape[0], 2), 1)
    return jnp.where(idx == 0, c0, c1)


def _rows2(r0, r1):
    idx = lax.broadcasted_iota(jnp.int32, (2, r0.shape[1]), 0)
    return jnp.where(idx == 0, r0, r1)


def _attn_specs(seq, v_blk0, has_bias):
    qblk = pl.BlockSpec((None, TQ, 2 * HP), lambda b, p, i: (b, i, p))
    kblk = pl.BlockSpec((None, seq, 2 * HP), lambda b, p, i: (b, 0, p))
    vblk = pl.BlockSpec((None, seq, LANES), lambda b, p, i: (b, 0, v_blk0 + p))
    oblk = pl.BlockSpec((None, TQ, LANES), lambda b, p, i: (b, i, p))
    rowblk = pl.BlockSpec((None, None, 2, TQ), lambda b, p, i: (b, p, 0, i))
    colblk = pl.BlockSpec((None, None, seq, 2), lambda b, p, i: (b, p, 0, 0))
    return qblk, kblk, vblk, oblk, rowblk, colblk


def _attn_fwd(q, k, v, v_blk0, c_cols, c_rows, scale, name, rider=None):
    bsz, seq, _ = q.shape
    nq = seq // TQ
    has_bias = c_cols is not None
    n_in = 5 if has_bias else 3
    nc = TQ // QF

    def body(*refs):
        q_ref, k_ref, v_ref = refs[:3]
        cq_ref, ck_ref = refs[3:5] if has_bias else (None, None)
        o_ref, lse_ref = refs[n_in:]
        qi = pl.program_id(2)
        qb = [[q_ref[c * QF:(c + 1) * QF, x * HP:(x + 1) * HP].astype(BF16) for c in range(nc)] for x in range(2)]
        key_in = lax.broadcasted_iota(jnp.int32, (TQ, QF), 0)
        qry_in = lax.broadcasted_iota(jnp.int32, (TQ, QF), 1)

        def block(j, carry, masked):
            ks = pl.multiple_of(j * TQ, TQ)
            vt = v_ref[pl.ds(ks, TQ), :].T.astype(BF16)
            new = []
            for x in range(2):
                kx = k_ref[pl.ds(ks, TQ), x * HP:(x + 1) * HP].astype(BF16)
                ckx = _col(ck_ref[pl.ds(ks, TQ), :], x) if has_bias else None
                for c in range(nc):
                    m, l, acc = carry[x * nc + c]
                    s = lax.dot_general(kx, qb[x][c], NT, preferred_element_type=F32) * scale
                    if has_bias:
                        s = s + (cq_ref[x:x + 1, c * QF:(c + 1) * QF] - ckx)
                    if masked:
                        s = jnp.where(qry_in + c * QF >= key_in, s, NEG_INF)
                    m_new = jnp.maximum(m, jnp.max(s, axis=0, keepdims=True))
                    alpha = jnp.exp(m - m_new)
                    p = jnp.exp(s - m_new)
                    l = alpha * l + jnp.sum(p, axis=0, keepdims=True)
                    acc = alpha * acc + lax.dot_general(vt, p.astype(BF16), NN, preferred_element_type=F32)
                    new.append((m_new, l, acc))
            return tuple(new)

        init = tuple((jnp.full((1, QF), NEG_INF, F32), jnp.zeros((1, QF), F32), jnp.zeros((LANES, QF), F32))
                     for _ in range(2 * nc))
        carry = lax.fori_loop(0, qi, lambda j, cr: block(j, cr, False), init)
        carry = block(qi, carry, True)
        lane = lax.broadcasted_iota(jnp.int32, (QF, LANES), 1)
        lse_rows = []
        for x in range(2):
            lse_rows.append(jnp.concatenate([carry[x * nc + c][0] + jnp.log(carry[x * nc + c][1]) for c in range(nc)],
                                            axis=1))
        for c in range(nc):
            (_, l0, a0), (_, l1, a1) = carry[c], carry[nc + c]
            o_ref[c * QF:(c + 1) * QF, :] = jnp.where(lane < HEAD_V, (a0 / l0).T, (a1 / l1).T)
        lse_ref[...] = _rows2(lse_rows[0], lse_rows[1])

    qblk, kblk, vblk, oblk, rowblk, colblk = _attn_specs(seq, v_blk0, has_bias)
    in_specs, args = [qblk, kblk, vblk], [q, k, v]
    if has_bias:
        in_specs += [rowblk, colblk]
        args += [c_rows, c_cols]
    return _pcall(body, name=name, grid=(bsz, 2, nq), in_specs=in_specs, out_specs=[oblk, rowblk],
                  out_shape=[jax.ShapeDtypeStruct((bsz, seq, 2 * LANES), F32),
                             jax.ShapeDtypeStruct((bsz, 2, 2, seq), F32)], rider=rider)(*args)


def _attn_bwd(q, k, v, v_blk0, o, do, lse, c_cols, c_rows, scale, name, rider=None):
    bsz, seq, _ = q.shape
    nq = seq // TQ
    has_bias = c_cols is not None
    n_in = 8 if has_bias else 6
    nc = TQ // QC

    def body(*refs):
        q_ref, k_ref, v_ref, o_ref, do_ref, lse_ref = refs[:6]
        cq_ref, ck_ref = refs[6:8] if has_bias else (None, None)
        dq_ref, dk_ref, dv_ref = refs[n_in:n_in + 3]
        dck_ref, dcq_ref = refs[n_in + 3:n_in + 5] if has_bias else (None, None)
        qi = pl.program_id(2)

        @pl.when(qi == 0)
        def _():
            dk_ref[...] = jnp.zeros_like(dk_ref)
            dv_ref[...] = jnp.zeros_like(dv_ref)
            if has_bias:
                dck_ref[...] = jnp.zeros_like(dck_ref)

        lane = lax.broadcasted_iota(jnp.int32, (QC, LANES), 1)
        ones8 = jnp.ones((8, LANES), BF16)
        qb, dob, delta, lse_r = [], [], [], []
        for x in range(2):
            hm = jnp.logical_and(lane >= x * HEAD_V, lane < (x + 1) * HEAD_V)
            qb.append([]), dob.append([]), delta.append([]), lse_r.append([])
            for c in range(nc):
                rows = slice(c * QC, (c + 1) * QC)
                do_c = jnp.where(hm, do_ref[rows, :], 0.0)
                prod = do_c * o_ref[rows, :]
                hi = prod.astype(BF16)
                lo = (prod - hi.astype(F32)).astype(BF16)
                d8 = (lax.dot_general(ones8, hi, NT, preferred_element_type=F32)
                      + lax.dot_general(ones8, lo, NT, preferred_element_type=F32))
                delta[x].append(d8[0:1, :])
                dob[x].append(do_c.astype(BF16))
                qb[x].append(q_ref[rows, x * HP:(x + 1) * HP].astype(BF16))
                lse_r[x].append(lse_ref[x:x + 1, rows])
        key_in = lax.broadcasted_iota(jnp.int32, (TQ, QC), 0)
        qry_in = lax.broadcasted_iota(jnp.int32, (TQ, QC), 1)

        def block(j, carry, masked):
            ks = pl.multiple_of(j * TQ, TQ)
            vb = v_ref[pl.ds(ks, TQ), :].astype(BF16)
            new, key_sums = [], []
            for x in range(2):
                k32 = k_ref[pl.ds(ks, TQ), x * HP:(x + 1) * HP]
                kx, kt = k32.astype(BF16), k32.T.astype(BF16)
                ckx = _col(ck_ref[pl.ds(ks, TQ), :], x) if has_bias else None
                dk_acc = jnp.zeros((TQ, HP), F32)
                dv_acc = jnp.zeros((TQ, LANES), F32)
                key_sum = jnp.zeros((TQ, 1), F32)
                for c in range(nc):
                    dqt, dcq = carry[x * nc + c]
                    s = lax.dot_general(kx, qb[x][c], NT, preferred_element_type=F32) * scale
                    if has_bias:
                        s = s + (cq_ref[x:x + 1, c * QC:(c + 1) * QC] - ckx)
                    if masked:
                        s = jnp.where(qry_in + c * QC >= key_in, s, NEG_INF)
                    p = jnp.exp(s - lse_r[x][c])
                    dp = lax.dot_general(vb, dob[x][c], NT, preferred_element_type=F32)
                    ds = p * (dp - delta[x][c])
                    dsb = ds.astype(BF16)
                    dk_acc = dk_acc + lax.dot_general(dsb, qb[x][c], NN, preferred_element_type=F32)
                    dv_acc = dv_acc + lax.dot_general(p.astype(BF16), dob[x][c], NN, preferred_element_type=F32)
                    dqt = dqt + lax.dot_general(kt, dsb, NN, preferred_element_type=F32)
                    if has_bias:
                        dcq = dcq + jnp.sum(ds, axis=0, keepdims=True)
                        key_sum = key_sum + jnp.sum(ds, axis=1, keepdims=True)
                    new.append((dqt, dcq))
                dk_ref[pl.ds(ks, TQ), x * HP:(x + 1) * HP] += dk_acc * scale
                dv_ref[pl.ds(ks, TQ), :] += dv_acc
                key_sums.append(key_sum)
            if has_bias:
                dck_ref[pl.ds(ks, TQ), :] -= _two_cols(key_sums[0], key_sums[1])
            return tuple(new)

        init = tuple((jnp.zeros((HP, QC), F32), jnp.zeros((1, QC), F32)) for _ in range(2 * nc))
        carry = lax.fori_loop(0, qi, lambda j, cr: block(j, cr, False), init)
        carry = block(qi, carry, True)
        for x in range(2):
            for c in range(nc):
                dq_ref[c * QC:(c + 1) * QC, x * HP:(x + 1) * HP] = carry[x * nc + c][0].T * scale
        if has_bias:
            dcq_ref[...] = _rows2(jnp.concatenate([carry[c][1] for c in range(nc)], axis=1),
                                  jnp.concatenate([carry[nc + c][1] for c in range(nc)], axis=1))

    qblk, kblk, vblk, oblk, rowblk, colblk = _attn_specs(seq, v_blk0, has_bias)
    in_specs = [qblk, kblk, vblk, oblk, oblk, rowblk]
    out_specs = [qblk, kblk, pl.BlockSpec((None, seq, LANES), lambda b, p, i: (b, 0, p))]
    out_shape = [jax.ShapeDtypeStruct((bsz, seq, QKW), F32), jax.ShapeDtypeStruct((bsz, seq, QKW), F32),
                 jax.ShapeDtypeStruct((bsz, seq, 2 * LANES), F32)]
    args = [q, k, v, o, do, lse]
    if has_bias:
        in_specs += [rowblk, colblk]
        out_specs += [colblk, rowblk]
        out_shape += [jax.ShapeDtypeStruct((bsz, 2, seq, 2), F32), jax.ShapeDtypeStruct((bsz, 2, 2, seq), F32)]
        args += [c_rows, c_cols]
    return _pcall(body, name=name, grid=(bsz, 2, nq), in_specs=in_specs, out_specs=out_specs,
                  out_shape=out_shape, rider=rider)(*args)


def _coords(idx):
    return (idx // 4, (idx // 2) % 2, idx % 2)


class _Xfer:
    def __init__(self, srcs, out_shapes, plan):
        self.srcs, self.out_shapes, self.plan = list(srcs), list(out_shapes), plan
        self.n = len(self.srcs)
        me0 = jnp.int32(0)
        n_pieces = [len(plan(a, me0, me0)) for a in range(self.n)]
        self.offs = np.concatenate([[0], np.cumsum(n_pieces)]).astype(int)
        self.total = int(self.offs[-1])

    def sems(self):
        remote = (N_DEV - 1) * self.total
        return [pltpu.SemaphoreType.DMA((remote,)), pltpu.SemaphoreType.DMA((remote,)),
                pltpu.SemaphoreType.DMA((self.total,))]

    def copies(self, src_refs, out_refs, send_sems, recv_sems, local_sems):
        me = 4 * lax.axis_index("x") + 2 * lax.axis_index("y") + lax.axis_index("c")
        out = []
        for a in range(self.n):
            for pi, (si, di) in enumerate(self.plan(a, me, me)):
                out.append((pltpu.make_async_copy(src_refs[a].at[si], out_refs[a].at[di],
                                                  local_sems.at[self.offs[a] + pi]), None))
        for kk in range(1, N_DEV):
            dest = (me + kk) % N_DEV
            src_dev = (me + N_DEV - kk) % N_DEV
            for a in range(self.n):
                pieces = self.plan(a, me, dest)
                landing = self.plan(a, src_dev, me)
                for pi, ((si, di), (_, li)) in enumerate(zip(pieces, landing)):
                    sem = (kk - 1) * self.total + self.offs[a] + pi
                    mk = functools.partial(pltpu.make_async_remote_copy, src_ref=src_refs[a].at[si],
                                           send_sem=send_sems.at[sem], recv_sem=recv_sems.at[sem],
                                           device_id=_coords(dest), device_id_type=pl.DeviceIdType.MESH)
                    out.append((mk(dst_ref=out_refs[a].at[di]), mk(dst_ref=out_refs[a].at[li])))
        return out

    @staticmethod
    def start(copies):
        for cp, _ in copies:
            cp.start()

    @staticmethod
    def wait(copies):
        for cp, rc in copies:
            if rc is None:
                cp.wait()
            else:
                cp.wait_send()
                rc.wait_recv()


def _exchange(name, xf):
    n = xf.n

    def body(*refs):
        copies = xf.copies(refs[:n], refs[n:2 * n], *refs[2 * n:])
        xf.start(copies)
        xf.wait(copies)

    any_spec = pl.BlockSpec(memory_space=pl.ANY)
    return pl.pallas_call(body, name=name, in_specs=[any_spec] * n, out_specs=[any_spec] * n,
                          out_shape=[jax.ShapeDtypeStruct(s, d) for s, d in xf.out_shapes],
                          scratch_shapes=xf.sems(),
                          compiler_params=pltpu.CompilerParams(has_side_effects=True))(*xf.srcs)


class _Gather2(_Xfer):
    def copies(self, src_refs, out_refs, send_sems, recv_sems, local_sems):
        x, y, c = lax.axis_index("x"), lax.axis_index("y"), lax.axis_index("c")
        num = lambda px, py, pc: 4 * px + 2 * py + pc
        me, sib = (x, y, c), (x, y, 1 - c)
        chips = [(1 - x, y), (x, 1 - y), (1 - x, 1 - y)]
        local, first, passed, landing = [], [], [], []
        for a in range(self.n):
            mine = self.plan(a, num(*me), num(*me))
            for pi, (si, di) in enumerate(mine):
                piece = self.offs[a] + pi

                def remote(k, src, dst, to, a=a, piece=piece):
                    return pltpu.make_async_remote_copy(
                        src_ref=src, dst_ref=dst, send_sem=send_sems.at[k * self.total + piece],
                        recv_sem=recv_sems.at[k * self.total + piece], device_id=to, device_id_type=pl.DeviceIdType.MESH)

                def slot(dev, a=a, pi=pi):
                    return out_refs[a].at[self.plan(a, num(*dev), num(*dev))[pi][1]]

                local.append(pltpu.make_async_copy(src_refs[a].at[si], out_refs[a].at[di], local_sems.at[piece]))
                first.append(remote(0, src_refs[a].at[si], out_refs[a].at[di], sib))
                landing.append([remote(0, slot(sib), slot(sib), sib)])
                for j, chip in enumerate(chips):
                    first.append(remote(1 + j, src_refs[a].at[si], out_refs[a].at[di], (*chip, c)))
                    passed.append((remote(1 + j, slot((*chip, c)), slot((*chip, c)), (*chip, c)),
                                   remote(4 + j, slot((*chip, c)), slot((*chip, c)), sib)))
                    landing[-1].append(remote(4 + j, slot((*chip, 1 - c)), slot((*chip, 1 - c)), sib))
        return local, first, passed, landing

    @staticmethod
    def start(copies):
        local, first, _, _ = copies
        for cp in local + first:
            cp.start()

    @staticmethod
    def wait(copies):
        local, first, passed, landing = copies
        for arrival, forward in passed:
            arrival.wait_recv()
            forward.start()
        for group in landing:
            for cp in group:
                cp.wait_recv()
        for cp in first + [fw for _, fw in passed]:
            cp.wait_send()
        for cp in local:
            cp.wait()


def _rows_of(j, n):
    return pl.ds(pl.multiple_of(j * n, n), n)


def _ffn_cols(j, half):
    return pl.multiple_of((2 * (j % 4) + half) * (2 * FF_BLK) + (j // 4) * FF_BLK, FF_BLK)


def _gather_plan(shards):
    names = list(shards)

    def plan(a, src_dev, _dest):
        nm = names[a]
        j = src_dev
        if nm in ("w_in", "w_uq", "w_ukv", "w_glu", "w_out", "w_down"):
            rows = shards[nm].shape[0]
            return [((slice(None), slice(None)), (_rows_of(j, rows), slice(None)))]
        if nm == "w_branch":
            return [((slice(None), slice(None), slice(None)), (slice(None), slice(None), _rows_of(j, LANES)))]
        if nm in ("w_up", "conv"):
            return [((slice(None), pl.ds(h * FF_BLK, FF_BLK)), (slice(None), pl.ds(_ffn_cols(j, h), FF_BLK)))
                    for h in range(2)]
        raise KeyError(nm)

    return names, plan


def _scatter_plan(names, shard_shapes):
    def plan(a, src_dev, dest):
        nm = names[a]
        j = dest
        if nm in ("w_in", "w_uq", "w_ukv", "w_glu", "w_out", "w_down"):
            rows = shard_shapes[nm][0]
            return [((_rows_of(j, rows), slice(None)), (src_dev, slice(None), slice(None)))]
        if nm == "w_branch":
            return [((slice(None), slice(None), _rows_of(j, LANES)), (src_dev, slice(None), slice(None), slice(None)))]
        if nm in ("w_up", "conv"):
            return [((slice(None), pl.ds(_ffn_cols(j, h), FF_BLK)), (src_dev, slice(None), pl.ds(h * FF_BLK, FF_BLK)))
                    for h in range(2)]
        if nm == "small":
            return [((slice(None), slice(None)), (src_dev, slice(None), slice(None)))]
        raise KeyError(nm)

    return plan


def _pad_in_cols(w):
    lead = w.shape[:-1]
    parts, pos = [], 0
    for dst, src, width in IN_SEGS:
        if dst > pos:
            parts.append(jnp.zeros(lead + (dst - pos,), w.dtype))
        parts.append(w[..., src:src + width])
        pos = dst + width
    return jnp.concatenate(parts, axis=-1)


def _unpad_in_cols(w):
    order = sorted(IN_SEGS, key=lambda s: s[1])
    return jnp.concatenate([w[..., dst:dst + width] for dst, _, width in order], axis=-1)


def _pad_ff(w, axis):
    n = w.shape[axis] // FF_HALF
    parts = []
    for h in range(n):
        piece = lax.slice_in_dim(w, h * FF_HALF, (h + 1) * FF_HALF, axis=axis)
        zshape = list(w.shape)
        zshape[axis] = FF_BLK - FF_HALF
        parts += [piece, jnp.zeros(zshape, w.dtype)]
    return jnp.concatenate(parts, axis=axis)


def _unpad_ff(w, axis):
    n = w.shape[axis] // FF_BLK
    return jnp.concatenate([lax.slice_in_dim(w, h * FF_BLK, h * FF_BLK + FF_HALF, axis=axis) for h in range(n)],
                           axis=axis)


def _to_scan_cols(re_part, im_part):
    lead = re_part.shape[:-1]
    nb = S5_N // SCAN_W
    r = re_part.reshape(lead + (nb, SCAN_W))
    i = im_part.reshape(lead + (nb, SCAN_W))
    return jnp.concatenate([r, i], axis=-1).reshape(lead + (2 * S5_N,))


def _from_scan_cols(x):
    lead = x.shape[:-1]
    nb = S5_N // SCAN_W
    y = x.reshape(lead + (nb, 2, SCAN_W))
    return y[..., 0, :].reshape(lead + (S5_N,)), y[..., 1, :].reshape(lead + (S5_N,))


def _block_diag_in(b):
    eye = jnp.eye(S5_GROUPS, dtype=b.dtype)
    return jnp.einsum("gph,gk->ghkp", b, eye).reshape(S5_WIDTH, S5_N)


def _block_diag_in_t(m):
    eye = jnp.eye(S5_GROUPS, dtype=m.dtype)
    return jnp.einsum("ghkp,gk->gph", m.reshape(S5_GROUPS, S5_GROUP_CH, S5_GROUPS, S5_STATE), eye)


def _block_diag_out(c):
    eye = jnp.eye(S5_GROUPS, dtype=c.dtype)
    return jnp.einsum("ghp,gk->gpkh", c, eye).reshape(S5_N, S5_WIDTH)


def _block_diag_out_t(m):
    eye = jnp.eye(S5_GROUPS, dtype=m.dtype)
    return jnp.einsum("gpkh,gk->ghp", m.reshape(S5_GROUPS, S5_STATE, S5_GROUPS, S5_GROUP_CH), eye)


def _pad_heads(g, width):
    g = jnp.broadcast_to(g.reshape(-1, width), (N_HEADS, width))
    return jnp.pad(g, ((0, 0), (0, HP - width))).reshape(1, QKW)


def _prep_layer(l, gw, small, consts):
    e_mat, p_mat, r_mat, pl_mat = consts
    p = {}
    p["attn_g"] = small["attn_norm_g"][l].reshape(1, D_MODEL)
    p["ffn_g"] = small["ffn_norm_g"][l].reshape(1, D_MODEL)
    p["qlg"] = small["q_lat_norm_g"][l].reshape(1, MLA_Q_RANK)
    p["kvlg"] = small["kv_lat_norm_g"][l].reshape(1, MLA_KV_RANK)
    p["mqg"] = _pad_heads(small["mla_q_norm_g"][l], MLA_QK_DIM)
    p["mkg"] = _pad_heads(small["mla_k_norm_g"][l], MLA_QK_DIM)
    p["fqg"] = _pad_heads(small["fox_q_norm_g"][l], HEAD_V)
    p["fkg"] = _pad_heads(small["fox_k_norm_g"][l], HEAD_V)
    w_uq = gw["w_uq"].reshape(MLA_Q_RANK, N_HEADS, MLA_QK_DIM)
    p["w_uq"] = jnp.pad(w_uq, ((0, 0), (0, 0), (0, HP - MLA_QK_DIM))).reshape(MLA_Q_RANK, QKW)
    w_ukv = gw["w_ukv"].reshape(MLA_KV_RANK, N_HEADS, MLA_NOPE_DIM + HEAD_V)
    p["w_uk"] = jnp.pad(w_ukv[..., :MLA_NOPE_DIM], ((0, 0), (0, 0), (0, HP - MLA_NOPE_DIM))).reshape(MLA_KV_RANK, QKW)
    p["w_uv"] = w_ukv[..., MLA_NOPE_DIM:].reshape(MLA_KV_RANK, N_HEADS * HEAD_V)
    p["w_glu"] = gw["w_glu"]
    p["b_glu"] = small["s5_b_glu"][l].reshape(1, S5_WIDTH)
    p["d_row"] = small["s5_d"][l].reshape(1, S5_WIDTH)
    p["lam_re"] = small["s5_lambda_re"][l].reshape(S5_N, 1)
    p["lam_im"] = small["s5_lambda_im"][l].reshape(S5_N, 1)
    p["log_step"] = jnp.repeat(small["s5_log_step"][l], S5_STATE).reshape(S5_N, 1)
    p["b_re"] = small["s5_b_re"][l].reshape(S5_N, S5_GROUP_CH)
    p["b_im"] = small["s5_b_im"][l].reshape(S5_N, S5_GROUP_CH)
    c_re, c_im = small["s5_c_re"][l], small["s5_c_im"][l]
    p["c_blk"] = _to_scan_cols(_block_diag_out(c_re).T, -_block_diag_out(c_im).T).T
    p["fbias"] = small["fox_f_bias"][l]
    for nm in ("w_in", "w_out", "w_branch"):
        p[nm] = gw[nm]
    return p


def _s5_param_call(p, l):
    ins = [p["lam_re"], p["lam_im"], p["log_step"], p["b_re"], p["b_im"]]
    outs = [((S5_N, 1), F32, (S5_N, 1), lambda i: (0, 0), None)] * 2 + \
           [((S5_N, S5_GROUP_CH), F32, (S5_N, S5_GROUP_CH), lambda i: (0, 0), None)] * 2
    return _tile_call(_s5_params, "s5_params", (1,), [_par(a) for a in ins], outs)


def _s5_param_bwd_call(p, cts):
    ins = [p["lam_re"], p["lam_im"], p["log_step"], p["b_re"], p["b_im"]]

    def fn(lr, li, ls, br, bi, g0, g1, g2, g3):
        _, vjp = jax.vjp(_s5_params, lr, li, ls, br, bi)
        return vjp((g0, g1, g2, g3))

    outs = [((S5_N, 1), F32, (S5_N, 1), lambda i: (0, 0), None)] * 3 + \
           [((S5_N, S5_GROUP_CH), F32, (S5_N, S5_GROUP_CH), lambda i: (0, 0), None)] * 2
    return _tile_call(fn, "s5_params_bwd", (1,), [_par(a) for a in ins + list(cts)], outs)


def _layer_fwd(x_prev, f_prev, p, tabs, consts, bsz, riders):
    arrived = {}

    def rider_of(name):
        return riders[name][0] if name in riders else None

    def hosted(name, res, n_out):
        if name in riders:
            arrived.update(zip(riders[name][1], res[n_out:]))
        return res[:n_out]

    t = x_prev.shape[0]
    seq = t // bsz
    nt = t // TM
    cos, sin = tabs
    e_mat, p_mat, r_mat, pl_mat = consts
    sv = {}

    if f_prev is None:
        x = x_prev
        h = _tile_call(lambda xv, g: _rms(xv, g), "norm_first", (nt,), [_row(x), _par(p["attn_g"])],
                       [_orow(t, D_MODEL, BF16)])[0]
    else:
        x, h = _tile_call(lambda xv, fv, g: (xv + fv, _rms(xv + fv, g)), "norm_attn", (nt,),
                          [_row(x_prev), _row(f_prev), _par(p["attn_g"])], [_orow(t, D_MODEL), _orow(t, D_MODEL, BF16)])
    sv["x"], sv["h"] = x, h
    proj = _mm(h, p["w_in"], "nn", "mm_in")
    sv["proj"] = proj

    def pre_fn(pa, cosv, sinv, qlg, w_uq, kvlg, w_uk, w_uv, mqg, mkg, fqg, fkg, em, pm, rm, plm):
        return _mixer_pre(pa[:, O_CQ:O_CQ + MLA_Q_RANK], pa[:, O_CKV:O_CKV + MLA_KV_RANK], pa[:, O_KR:O_KR + LANES],
                          pa[:, O_FQ:O_FQ + 256], pa[:, O_FK:O_FK + 256], qlg, w_uq, kvlg, w_uk, w_uv,
                          mqg, mkg, fqg, fkg, cosv, sinv, em, pm, rm, plm)

    pre_params = [p["qlg"], p["w_uq"], p["kvlg"], p["w_uk"], p["w_uv"], p["mqg"], p["mkg"], p["fqg"], p["fkg"],
                  e_mat, p_mat, r_mat, pl_mat]
    q_m, k_m, v_m, q_f, k_f = hosted("mixer_pre", _tile_call(
        pre_fn, "mixer_pre", (nt,), [_row(proj, PA), _row(cos), _row(sin)] + [_par(a) for a in pre_params],
        [_orow(t, QKW), _orow(t, QKW), _orow(t, 2 * LANES), _orow(t, QKW), _orow(t, QKW)],
        rider=rider_of("mixer_pre")), 5)
    sv.update(q_m=q_m, k_m=k_m, v_m=v_m, q_f=q_f, k_f=k_f)

    z = proj[:, O_FL:O_FL + N_HEADS].reshape(bsz, seq, N_HEADS).transpose(0, 2, 1).reshape(bsz * N_HEADS, seq)
    fb = jnp.tile(p["fbias"], bsz).reshape(bsz * N_HEADS, 1)
    cum = _forget_fwd(z, fb, "forget_fwd")
    ck = cum.reshape(bsz, 2, 2, seq)
    cq = ck.transpose(0, 1, 3, 2)
    sv.update(z=z, fb=fb, cq=cq, ck=ck)

    r3 = lambda a: a.reshape(bsz, seq, a.shape[-1])
    o_m, lse_m = hosted("attn_mla_fwd", _attn_fwd(r3(q_m), r3(k_m), r3(v_m), 0, None, None, MLA_QK_DIM ** -0.5,
                                                  "attn_mla_fwd", rider_of("attn_mla_fwd")), 2)
    o_f, lse_f = hosted("attn_fox_fwd", _attn_fwd(r3(q_f), r3(k_f), r3(proj), O_FV // LANES, cq, ck, HEAD_V ** -0.5,
                                                  "attn_fox_fwd", rider_of("attn_fox_fwd")), 2)
    o_m, o_f = o_m.reshape(t, 2 * LANES), o_f.reshape(t, 2 * LANES)
    sv.update(o_m=o_m, o_f=o_f, lse_m=lse_m, lse_f=lse_f)

    lbar_re, lbar_im, bbar_re, bbar_im = _s5_param_call(p, 0)
    lbar = _to_scan_cols(lbar_re.reshape(1, S5_N), lbar_im.reshape(1, S5_N))
    b_blk = _to_scan_cols(_block_diag_in(bbar_re.reshape(S5_GROUPS, S5_STATE, S5_GROUP_CH)),
                          _block_diag_in(bbar_im.reshape(S5_GROUPS, S5_STATE, S5_GROUP_CH)))
    sv.update(lbar=lbar, b_blk=b_blk)
    bu = _tile_call(lambda u, bb: bdot(u, bb), "s5_bu", (nt,), [_row(proj, S5_WIDTH, O_U // S5_WIDTH), _par(b_blk)],
                    [_orow(t, 2 * S5_N)])[0]
    state = hosted("s5_scan_fwd", _scan_fwd(bu, lbar, bsz, "s5_scan_fwd", rider_of("s5_scan_fwd")), 1)[0]
    sv["state"] = state
    post_params = [p["c_blk"], p["d_row"], p["w_glu"], p["b_glu"]]
    o_s = _tile_call(_s5_post, "s5_post", (nt,),
                     [_row(state), _row(proj, S5_WIDTH, O_U // S5_WIDTH)] + [_par(a) for a in post_params],
                     [_orow(t, S5_WIDTH)])[0]
    sv["o_s"] = o_s

    def merge_fn(oa, ob, oc, ga, gb, gc, wb):
        return _merge(oa, ob, oc, ga, gb, gc, wb[0], wb[1], wb[2])

    gate_specs = [_row(proj, D_MODEL, PA // D_MODEL + n) for n in range(3)]
    merged = _tile_call(merge_fn, "merge", (nt,), [_row(o_m), _row(o_f), _row(o_s)] + gate_specs + [_par(p["w_branch"])],
                        [_orow(t, D_MODEL, BF16)])[0]
    sv["merged"] = merged
    attn_out = _mm(merged, p["w_out"], "nn", "mm_out")

    x1, h2 = _tile_call(lambda xv, av, g: (xv + av, _rms(xv + av, g)), "norm_ffn", (nt,),
                        [_row(x), _row(attn_out), _par(p["ffn_g"])], [_orow(t, D_MODEL), _orow(t, D_MODEL, BF16)])
    sv["x1"], sv["h2"] = x1, h2
    p.update({n: arrived.pop(n) for n in FFN_WEIGHTS})
    up = _mm(h2, p["w_up"], "nn", "mm_up")
    sv["up"] = up

    def ffn_fn(upv, cw):
        return _ffn_mid(upv[:, :FF_BLK], upv[:, FF_BLK:], cw[0:1, :FF_BLK], cw[1:2, :FF_BLK], cw[2:3, :FF_BLK],
                        cw[0:1, FF_BLK:], cw[1:2, FF_BLK:], cw[2:3, FF_BLK:])

    nblk = FF_PW // FF_BLK
    act = _tile_call(ffn_fn, "ffn_mid", (nblk, bsz),
                     [(up, (seq, 2 * FF_BLK), lambda j, b: (b, j)), (p["conv"], (3, 2 * FF_BLK), lambda j, b: (0, j))],
                     [((t, FF_PW), BF16, (seq, FF_BLK), lambda j, b: (b, j), None)])[0]
    sv["act"] = act
    ffn_out = _mm(act, p["w_down"], "nn", "mm_down")
    return x1, ffn_out, sv, arrived


def _layer_bwd(dx2, p, sv, tabs, consts, bsz, scatter, carry):
    lands, carry_lands = {}, {}

    def riding(host, g):
        names = BWD_RIDERS[host]
        return scatter({n: g[n] for n in names}), names

    def landed(host, res, n_out):
        lands.update(zip(BWD_RIDERS[host], res[n_out:]))
        return res[:n_out]

    t = dx2.shape[0]
    seq = t // bsz
    nt = t // TM
    cos, sin = tabs
    e_mat, p_mat, r_mat, pl_mat = consts
    g = {}

    g["w_down"] = _mm(sv["act"], dx2, "tn", "mm_down_dw", out_dtype=BF16)
    dact = _mm(dx2, p["w_down"], "nt", "mm_down_dx")

    def ffn_bwd_fn(upv, cw, da):
        args = (upv[:, :FF_BLK], upv[:, FF_BLK:], cw[0:1, :FF_BLK], cw[1:2, :FF_BLK], cw[2:3, :FF_BLK],
                cw[0:1, FF_BLK:], cw[1:2, FF_BLK:], cw[2:3, FF_BLK:])
        _, vjp = jax.vjp(_ffn_mid, *args)
        dg, dv, g0, g1, g2, v0, v1, v2 = vjp(da)
        return (jnp.concatenate([dg, dv], axis=1), jnp.concatenate([g0, v0], axis=1),
                jnp.concatenate([g1, v1], axis=1), jnp.concatenate([g2, v2], axis=1))

    nblk = FF_PW // FF_BLK
    cw_out = ((1, UP_W), F32, (1, 2 * FF_BLK), lambda j, b: (0, j), (1,))
    res = _tile_call(
        ffn_bwd_fn, "ffn_mid_bwd", (nblk, bsz),
        [(sv["up"], (seq, 2 * FF_BLK), lambda j, b: (b, j)), (p["conv"], (3, 2 * FF_BLK), lambda j, b: (0, j)),
         (dact, (seq, FF_BLK), lambda j, b: (b, j))],
        [((t, UP_W), BF16, (seq, 2 * FF_BLK), lambda j, b: (b, j), None), cw_out, cw_out, cw_out],
        rider=scatter(carry) if carry else None)
    dup, dc0, dc1, dc2 = res[:4]
    carry_lands.update(zip(carry, res[4:]))
    g["conv"] = jnp.concatenate([dc0, dc1, dc2], axis=0)
    g["w_up"] = landed("mm_up_dw", _mm(sv["h2"], dup, "tn", "mm_up_dw", out_dtype=BF16,
                                       rider=riding("mm_up_dw", g)[0]), 1)[0]
    dh2 = _mm(dup, p["w_up"], "nt", "mm_up_dx")

    def norm_bwd_fn(xv, gv, dh, dres):
        _, vjp = jax.vjp(_rms, xv, gv)
        dxv, dg = vjp(dh)
        return dres + dxv, dg

    d1, g["ffn_g"] = _tile_call(norm_bwd_fn, "norm_bwd", (nt,),
                                [_row(sv["x1"]), _par(p["ffn_g"]), _row(dh2), _row(dx2)],
                                [_orow(t, D_MODEL), _oacc((1, D_MODEL))])

    g["w_out"] = _mm(sv["merged"], d1, "tn", "mm_out_dw", out_dtype=BF16)
    dmerged = _mm(d1, p["w_out"], "nt", "mm_out_dx")

    def merge_bwd_fn(oa, ob, oc, ga, gb, gc, wb, dm):
        wf = wb.astype(F32)
        _, vjp = jax.vjp(_merge, oa, ob, oc, ga, gb, gc, wf[0], wf[1], wf[2])
        doa, dob, doc, dga, dgb, dgc, dwa, dwb, dwc = vjp(dm)
        return doa, dob, doc, jnp.concatenate([dga, dgb, dgc], axis=1), jnp.stack([dwa, dwb, dwc])

    proj = sv["proj"]
    gate_specs = [_row(proj, D_MODEL, PA // D_MODEL + n) for n in range(3)]
    do_m, do_f, do_s, dgl, g["w_branch"] = _tile_call(
        merge_bwd_fn, "merge_bwd", (nt,),
        [_row(sv["o_m"]), _row(sv["o_f"]), _row(sv["o_s"])] + gate_specs + [_par(p["w_branch"]), _row(dmerged)],
        [_orow(t, 2 * LANES), _orow(t, 2 * LANES), _orow(t, S5_WIDTH), _orow(t, 3 * D_MODEL),
         _oacc((3, S5_WIDTH, D_MODEL))])

    def post_bwd_fn(st, u, cb, dr, wg, bg, do):
        _, vjp = jax.vjp(_s5_post, st, u, cb.astype(F32), dr, wg.astype(F32), bg)
        return vjp(do)

    u_spec = _row(proj, S5_WIDTH, O_U // S5_WIDTH)
    dstate, du1, dc_blk, g_d, g["w_glu"], g_bglu = _tile_call(
        post_bwd_fn, "s5_post_bwd", (nt,),
        [_row(sv["state"]), u_spec, _par(p["c_blk"]), _par(p["d_row"]), _par(p["w_glu"]), _par(p["b_glu"]), _row(do_s)],
        [_orow(t, 2 * S5_N), _orow(t, S5_WIDTH), _oacc((2 * S5_N, S5_WIDTH)), _oacc((1, S5_WIDTH)),
         _oacc((S5_WIDTH, S5_WIDTH)), _oacc((1, S5_WIDTH))])
    dbu, dlbar = landed("s5_scan_bwd", _scan_bwd(dstate, sv["state"], sv["lbar"], bsz, "s5_scan_bwd",
                                                 riding("s5_scan_bwd", g)[0]), 2)

    def bu_bwd_fn(u, bb, dbv, du_in):
        _, vjp = jax.vjp(bdot, u, bb)
        du, dbb = vjp(dbv)
        return du_in + du, dbb

    du, db_blk = _tile_call(bu_bwd_fn, "s5_bu_bwd", (nt,), [u_spec, _par(sv["b_blk"]), _row(dbu), _row(du1)],
                            [_orow(t, S5_WIDTH), _oacc((S5_WIDTH, 2 * S5_N))])
    dlr, dli = _from_scan_cols(dlbar)
    dbr, dbi = _from_scan_cols(db_blk)
    cts = (dlr.reshape(S5_N, 1), dli.reshape(S5_N, 1),
           _block_diag_in_t(dbr).reshape(S5_N, S5_GROUP_CH), _block_diag_in_t(dbi).reshape(S5_N, S5_GROUP_CH))
    g_lr, g_li, g_ls, g_br, g_bi = _s5_param_bwd_call(p, cts)
    dc_re, dc_im = _from_scan_cols(dc_blk.T)
    g["s5_lambda_re"] = g_lr.reshape(S5_GROUPS, S5_STATE)
    g["s5_lambda_im"] = g_li.reshape(S5_GROUPS, S5_STATE)
    g["s5_log_step"] = jnp.sum(g_ls.reshape(S5_GROUPS, S5_STATE), axis=1)
    g["s5_b_re"] = g_br.reshape(S5_GROUPS, S5_STATE, S5_GROUP_CH)
    g["s5_b_im"] = g_bi.reshape(S5_GROUPS, S5_STATE, S5_GROUP_CH)
    g["s5_c_re"] = _block_diag_out_t(dc_re.T)
    g["s5_c_im"] = -_block_diag_out_t(dc_im.T)
    g["s5_d"] = g_d.reshape(S5_GROUPS, S5_GROUP_CH)
    g["s5_b_glu"] = g_bglu.reshape(S5_WIDTH)

    r3 = lambda a: a.reshape(bsz, seq, a.shape[-1])
    dq_m, dk_m, dv_m = _attn_bwd(
        r3(sv["q_m"]), r3(sv["k_m"]), r3(sv["v_m"]), 0, r3(sv["o_m"]), r3(do_m), sv["lse_m"], None, None,
        MLA_QK_DIM ** -0.5, "attn_mla_bwd")
    dq_f, dk_f, dv_f, dcq, dck = landed("attn_fox_bwd", _attn_bwd(
        r3(sv["q_f"]), r3(sv["k_f"]), r3(proj), O_FV // LANES, r3(sv["o_f"]), r3(do_f), sv["lse_f"], sv["cq"], sv["ck"],
        HEAD_V ** -0.5, "attn_fox_bwd", riding("attn_fox_bwd", g)[0]), 5)
    dcum = (dck + dcq.transpose(0, 1, 3, 2)).reshape(bsz * N_HEADS, seq)
    dz, dfb = _forget_bwd(dcum, sv["z"], sv["fb"], "forget_bwd")
    g["fox_f_bias"] = jnp.sum(dfb.reshape(bsz, N_HEADS), axis=0)
    dfl = jnp.pad(dz.reshape(bsz, N_HEADS, seq).transpose(0, 2, 1).reshape(t, N_HEADS), ((0, 0), (0, LANES - N_HEADS)))

    def pre_bwd_fn(pa, cosv, sinv, qlg, w_uq, kvlg, w_uk, w_uv, mqg, mkg, fqg, fkg, em, pm, rm, plm,
                   gq, gk, gv, gqf, gkf, gvf, gu, gfl, ggl):
        f = functools.partial(_mixer_pre, cos=cosv, sin=sinv, e_mat=em, p_mat=pm, r_mat=rm, pl_mat=plm)
        prim = (pa[:, O_CQ:O_CQ + MLA_Q_RANK], pa[:, O_CKV:O_CKV + MLA_KV_RANK], pa[:, O_KR:O_KR + LANES],
                pa[:, O_FQ:O_FQ + 256], pa[:, O_FK:O_FK + 256], qlg, w_uq.astype(F32), kvlg, w_uk.astype(F32),
                w_uv.astype(F32), mqg, mkg, fqg, fkg)
        _, vjp = jax.vjp(lambda *a: f(*a), *prim)
        dcq_, dckv, dkr, dfq, dfk, dqlg, dwuq, dkvlg, dwuk, dwuv, dmqg, dmkg, dfqg, dfkg = vjp((gq, gk, gv, gqf, gkf))
        zpad = jnp.zeros((pa.shape[0], PA - O_CQ - MLA_Q_RANK), F32)
        dproj = jnp.concatenate([dckv, dfq, dfk, gvf, gu, dkr, gfl, dcq_, zpad, ggl], axis=1)
        return dproj, dqlg, dwuq, dkvlg, dwuk, dwuv, dmqg, dmkg, dfqg, dfkg

    pre_params = [p["qlg"], p["w_uq"], p["kvlg"], p["w_uk"], p["w_uv"], p["mqg"], p["mkg"], p["fqg"], p["fkg"],
                  e_mat, p_mat, r_mat, pl_mat]
    cts_in = [dq_m.reshape(t, QKW), dk_m.reshape(t, QKW), dv_m.reshape(t, 2 * LANES), dq_f.reshape(t, QKW),
              dk_f.reshape(t, QKW), dv_f.reshape(t, 2 * LANES), du, dfl, dgl]
    (dproj, g_qlg, g_wuq, g_kvlg, g_wuk, g_wuv, g_mqg, g_mkg, g_fqg, g_fkg) = _tile_call(
        pre_bwd_fn, "mixer_pre_bwd", (nt,),
        [_row(proj, PA), _row(cos), _row(sin)] + [_par(a) for a in pre_params] + [_row(a) for a in cts_in],
        [_orow(t, PW, BF16), _oacc((1, MLA_Q_RANK)), _oacc((MLA_Q_RANK, QKW)), _oacc((1, MLA_KV_RANK)),
         _oacc((MLA_KV_RANK, QKW)), _oacc((MLA_KV_RANK, N_HEADS * HEAD_V)), _oacc((1, QKW)), _oacc((1, QKW)),
         _oacc((1, QKW)), _oacc((1, QKW))])
    g["q_lat_norm_g"] = g_qlg.reshape(MLA_Q_RANK)
    g["kv_lat_norm_g"] = g_kvlg.reshape(MLA_KV_RANK)
    heads = lambda a, w: jnp.sum(a.reshape(N_HEADS, HP)[:, :w], axis=0)
    g["mla_q_norm_g"], g["mla_k_norm_g"] = heads(g_mqg, MLA_QK_DIM), heads(g_mkg, MLA_QK_DIM)
    g["fox_q_norm_g"], g["fox_k_norm_g"] = heads(g_fqg, HEAD_V), heads(g_fkg, HEAD_V)
    g["w_uq"] = g_wuq.reshape(MLA_Q_RANK, N_HEADS, HP)[..., :MLA_QK_DIM].reshape(MLA_Q_RANK, N_HEADS * MLA_QK_DIM)
    g["w_ukv"] = jnp.concatenate([g_wuk.reshape(MLA_KV_RANK, N_HEADS, HP)[..., :MLA_NOPE_DIM],
                                  g_wuv.reshape(MLA_KV_RANK, N_HEADS, HEAD_V)], axis=-1).reshape(MLA_KV_RANK, QKW)

    g["w_in"] = _mm(sv["h"], dproj, "tn", "mm_in_dw", out_dtype=BF16)
    dh = landed("mm_in_dx", _mm(dproj, p["w_in"], "nt", "mm_in_dx", rider=riding("mm_in_dx", g)[0]), 1)[0]
    dx, g["attn_g"] = _tile_call(norm_bwd_fn, "norm_bwd", (nt,),
                                 [_row(sv["x"]), _par(p["attn_g"]), _row(dh), _row(d1)],
                                 [_orow(t, D_MODEL), _oacc((1, D_MODEL))])
    return dx, g, lands, carry_lands


def _row_tile(r, c, max_elems):
    if r * c <= max_elems:
        return r
    best = None
    for d in range(8, r, 8):
        if r % d == 0 and d * c <= max_elems:
            best = d
    assert best is not None, (r, c)
    return best


def _sum8(land, name):
    _, r, c = land.shape
    tr = _row_tile(r, N_DEV * c, 1 << 20)

    def fn(lv):
        acc = lv[0].astype(F32)
        for i in range(1, N_DEV):
            acc = acc + lv[i].astype(F32)
        return acc

    return _tile_call(fn, name, (r // tr,), [(land, (N_DEV, tr, c), lambda i: (0, i, 0))],
                      [((r, c), F32, (tr, c), lambda i: (i, 0), None)])[0]


def _adamw_layer(w, g, m, v, l, prev, name):
    _, r, c = w.shape
    tr = _row_tile(r, c, 1 << 19)
    lay = pl.BlockSpec((None, tr, c), lambda i: (l, i, 0))

    def body(w_ref, g_ref, m_ref, v_ref, *rest):
        g_out, d_out, m_out, v_out = rest[-4:]
        gv = g_ref[...]
        d_out[...], m_out[...], v_out[...] = _adamw(w_ref[...], gv, m_ref[...], v_ref[...])
        g_out[...] = gv

    in_specs = [lay, pl.BlockSpec((tr, c), lambda i: (i, 0)), lay, lay]
    args = [w, g, m, v]
    aliases = {}
    if prev is not None:
        in_specs += [pl.BlockSpec(memory_space=pl.ANY)] * 4
        args += list(prev)
        aliases = {4 + k: k for k in range(4)}
    return _pcall(body, name=name, grid=(r // tr,), in_specs=in_specs, out_specs=[lay] * 4,
                  out_shape=[jax.ShapeDtypeStruct(w.shape, F32)] * 4, aliases=aliases)(*args)


SMALL_NAMES = ("attn_norm_g", "q_lat_norm_g", "kv_lat_norm_g", "mla_q_norm_g", "mla_k_norm_g", "fox_q_norm_g",
               "fox_k_norm_g", "fox_f_bias", "s5_lambda_re", "s5_lambda_im", "s5_b_re", "s5_b_im", "s5_c_re", "s5_c_im",
               "s5_d", "s5_log_step", "s5_b_glu", "ffn_norm_g")
BIG_NAMES = ("w_in", "w_uq", "w_ukv", "s5_w_glu", "w_branch", "w_out", "w_up", "ffn_conv_w", "w_down")
ALL_NAMES = ("attn_norm_g", "w_in", "q_lat_norm_g", "w_uq", "kv_lat_norm_g", "w_ukv", "mla_q_norm_g", "mla_k_norm_g",
             "fox_q_norm_g", "fox_k_norm_g", "fox_f_bias", "s5_lambda_re", "s5_lambda_im", "s5_b_re", "s5_b_im",
             "s5_c_re", "s5_c_im", "s5_d", "s5_log_step", "s5_w_glu", "s5_b_glu", "w_branch", "w_out", "ffn_norm_g",
             "w_up", "ffn_conv_w", "w_down")


def _pack(arrs):
    tile = 8 * LANES
    parts = []
    for a in arrs:
        n = int(np.prod(a.shape))
        tot = -(-n // tile) * tile
        parts.append(jnp.pad(a.reshape(-1), (0, tot - n)).reshape(tot // LANES, LANES))
    rows = sum(a.shape[0] for a in parts)
    parts.append(jnp.zeros((-rows % 256, LANES), parts[0].dtype))
    return jnp.concatenate(parts, axis=0)


def _unpack(packed, like):
    tile = 8 * LANES
    out, pos = [], 0
    for a in like:
        n = int(np.prod(a.shape))
        rows = -(-n // tile) * 8
        out.append(packed[pos:pos + rows].reshape(-1)[:n].reshape(a.shape))
        pos += rows
    return out


def _rope_tables(positions, t):
    pos = positions.reshape(t, 1)
    inv = _inv_freq_row()

    def fn(pv, iv):
        ang = pv.astype(F32) * iv
        return jnp.cos(ang), jnp.sin(ang)

    return _tile_call(fn, "rope_tables", (t // TM,), [_row(pos), _par(inv)], [_orow(t, QKW), _orow(t, QKW)])


def _loss_call(x1, f, target):
    t = x1.shape[0]

    def fn(xv, fv, tv):
        e = xv + fv - tv
        per_tok = jnp.sum(e * e, axis=1, keepdims=True) * (1.0 / D_MODEL)
        return 0.5 * jnp.sum(per_tok, axis=0, keepdims=True), e * (1.0 / D_MODEL)

    return _tile_call(fn, "loss", (t // TM,), [_row(x1), _row(f), _row(target)], [_oacc((1, 1)), _orow(t, D_MODEL)])


def _local_shards(l, w):
    return {
        "w_in": _pad_in_cols(w["w_in"][l]).astype(BF16),
        "w_uq": w["w_uq"][l].reshape(-1, N_HEADS * MLA_QK_DIM).astype(BF16),
        "w_ukv": w["w_ukv"][l].reshape(-1, QKW).astype(BF16),
        "w_glu": w["s5_w_glu"][l].astype(BF16),
        "w_branch": w["w_branch"][l].astype(BF16),
        "w_out": w["w_out"][l].astype(BF16),
        "w_up": _pad_ff(w["w_up"][l], 1).astype(BF16),
        "conv": _pad_ff(w["ffn_conv_w"][l], 1),
        "w_down": _pad_ff(w["w_down"][l], 0).astype(BF16),
    }


GATHERED_SHAPES = {"w_in": (D_MODEL, PW), "w_uq": (MLA_Q_RANK, N_HEADS * MLA_QK_DIM), "w_ukv": (MLA_KV_RANK, QKW),
                   "w_glu": (S5_WIDTH, S5_WIDTH), "w_branch": (3, S5_WIDTH, D_MODEL), "w_out": (D_MODEL, D_MODEL),
                   "w_up": (D_MODEL, UP_W), "conv": (3, UP_W), "w_down": (FF_PW, D_MODEL)}
FFN_WEIGHTS = ("w_up", "conv", "w_down")
MIXER_WEIGHTS = ("w_in", "w_uq", "w_ukv", "w_glu", "w_out", "w_branch")
FWD_RIDERS_OWN = {"attn_mla_fwd": ("w_up", "conv"), "s5_scan_fwd": ("w_down",)}
FWD_RIDERS_NEXT = {"attn_fox_fwd": ("w_in", "w_uq", "w_ukv", "w_glu"), "mixer_pre": ("w_out", "w_branch")}
BWD_RIDERS = {"mm_up_dw": ("w_down",), "s5_scan_bwd": ("w_up",),
              "attn_fox_bwd": ("w_out", "w_branch", "w_glu", "conv"), "mm_in_dx": ("w_uq", "w_ukv")}
UPDATED_TRANSPOSED = ("w_up",)
WEIGHT_OF = {"w_in": "w_in", "w_uq": "w_uq", "w_ukv": "w_ukv", "w_glu": "s5_w_glu", "w_branch": "w_branch",
             "w_out": "w_out", "w_up": "w_up", "conv": "ffn_conv_w", "w_down": "w_down"}


def _unshard_layout(nm, g2):
    if nm == "w_in":
        return _unpad_in_cols(g2)
    if nm == "w_uq":
        return g2.reshape(-1, N_HEADS, MLA_QK_DIM)
    if nm == "w_ukv":
        return g2.reshape(-1, N_HEADS, MLA_NOPE_DIM + HEAD_V)
    if nm in ("w_up", "conv"):
        return _unpad_ff(g2, 1)
    if nm == "w_down":
        return _unpad_ff(g2, 0)
    return g2


def kernel(x, positions, attn_norm_g, w_in, q_lat_norm_g, w_uq, kv_lat_norm_g, w_ukv, mla_q_norm_g, mla_k_norm_g, fox_q_norm_g, fox_k_norm_g, fox_f_bias, s5_lambda_re, s5_lambda_im, s5_b_re, s5_b_im, s5_c_re, s5_c_im, s5_d, s5_log_step, s5_w_glu, s5_b_glu, w_branch, w_out, ffn_norm_g, w_up, ffn_conv_w, w_down, loss_target, m_attn_norm_g, m_w_in, m_q_lat_norm_g, m_w_uq, m_kv_lat_norm_g, m_w_ukv, m_mla_q_norm_g, m_mla_k_norm_g, m_fox_q_norm_g, m_fox_k_norm_g, m_fox_f_bias, m_s5_lambda_re, m_s5_lambda_im, m_s5_b_re, m_s5_b_im, m_s5_c_re, m_s5_c_im, m_s5_d, m_s5_log_step, m_s5_w_glu, m_s5_b_glu, m_w_branch, m_w_out, m_ffn_norm_g, m_w_up, m_ffn_conv_w, m_w_down, v_attn_norm_g, v_w_in, v_q_lat_norm_g, v_w_uq, v_kv_lat_norm_g, v_w_ukv, v_mla_q_norm_g, v_mla_k_norm_g, v_fox_q_norm_g, v_fox_k_norm_g, v_fox_f_bias, v_s5_lambda_re, v_s5_lambda_im, v_s5_b_re, v_s5_b_im, v_s5_c_re, v_s5_c_im, v_s5_d, v_s5_log_step, v_s5_w_glu, v_s5_b_glu, v_w_branch, v_w_out, v_ffn_norm_g, v_w_up, v_ffn_conv_w, v_w_down):
    args = locals()
    w = {n: args[n] for n in ALL_NAMES}
    m = {n: args["m_" + n] for n in ALL_NAMES}
    v = {n: args["v_" + n] for n in ALL_NAMES}
    depth = w_in.shape[0]
    bsz, seq, _ = x.shape
    t = bsz * seq
    consts = _constants()
    tabs = _rope_tables(positions, t)
    small = {n: w[n] for n in SMALL_NAMES}

    all_names = list(GATHERED_SHAPES)
    all_shards = [_local_shards(l, w) for l in range(depth)]

    def gather_xfer(l, names):
        sub = {n: all_shards[l][n] for n in names}
        order, plan = _gather_plan(sub)
        return _Gather2([sub[n] for n in order], [(GATHERED_SHAPES[n], sub[n].dtype) for n in order], plan)

    def scatter_xfer(g, names):
        shapes = {n: _local_shards_shape(n) for n in names}
        return _Xfer([g[n] for n in names], [((N_DEV,) + shapes[n], g[n].dtype) for n in names],
                     _scatter_plan(list(names), shapes))

    xs = x.reshape(t, D_MODEL)
    f_prev = None
    saved, params = [], []
    gathered = dict(zip(MIXER_WEIGHTS, _exchange("gather_weights", gather_xfer(0, MIXER_WEIGHTS))))
    for l in range(depth):
        p = _prep_layer(l, gathered, small, consts)
        riders = {host: (gather_xfer(l, names), names) for host, names in FWD_RIDERS_OWN.items()}
        if l + 1 < depth:
            riders.update({host: (gather_xfer(l + 1, names), names) for host, names in FWD_RIDERS_NEXT.items()})
        xs, f_prev, sv, gathered = _layer_fwd(xs, f_prev, p, tabs, consts, bsz, riders)
        saved.append(sv)
        params.append(p)

    loss_part, dy = _loss_call(xs, f_prev, loss_target.reshape(t, D_MODEL))
    loss = lax.psum(loss_part[0, 0], ("x", "y", "c"))

    small_grads = {n: [None] * depth for n in SMALL_NAMES}
    big = {n: None for n in BIG_NAMES}

    def finish(l, lands):
        for n in all_names:
            wn, ld = WEIGHT_OF[n], lands[n]
            ld2 = ld.reshape(N_DEV, -1, ld.shape[-1])
            gsum = _unshard_layout(n, _sum8(ld2, "sum_" + n).reshape(_local_shards_shape(n)))
            if n in UPDATED_TRANSPOSED:
                gsum = gsum.T
            c = gsum.shape[-1]
            three = lambda a: (jnp.swapaxes(a, 1, 2) if n in UPDATED_TRANSPOSED else a).reshape(depth, -1, c)
            big[wn] = _adamw_layer(three(w[wn]), gsum.reshape(-1, c), three(m[wn]), three(v[wn]), l, big[wn],
                                   "adamw_" + n)

    def scatter(grads):
        return scatter_xfer(grads, list(grads))

    dx = dy
    carry, lands_above = {}, None
    for l in reversed(range(depth)):
        dx, g, lands, carry_lands = _layer_bwd(dx, params[l], saved[l], tabs, consts, bsz, scatter, carry)
        if lands_above is not None:
            finish(l + 1, {**lands_above, **carry_lands})
        carry, lands_above = {"w_in": g["w_in"]}, lands
        small_grads["attn_norm_g"][l] = g["attn_g"].reshape(D_MODEL)
        small_grads["ffn_norm_g"][l] = g["ffn_g"].reshape(D_MODEL)
        for n in SMALL_NAMES:
            if n not in ("attn_norm_g", "ffn_norm_g"):
                small_grads[n][l] = g[n]
    finish(0, {**lands_above, "w_in": _exchange("scatter_grads", scatter(carry))[0]})

    sg = [jnp.stack(small_grads[n]).reshape(w[n].shape) for n in SMALL_NAMES]
    packed = _pack(sg)
    land = _exchange("gather_small_grads", _Xfer([packed], [((N_DEV,) + packed.shape, F32)],
                                                 _scatter_plan(["small"], {})))[0]
    gs = _sum8(land, "sum_small")
    one = lambda names_of: _pack([names_of[n] for n in SMALL_NAMES])[None]
    small_res = _adamw_layer(one(w), gs, one(m), one(v), 0, None, "adamw_small")
    like = [w[n] for n in SMALL_NAMES]
    small_out = [dict(zip(SMALL_NAMES, _unpack(a[0], like))) for a in small_res]

    def out_of(kind, n):
        if n in SMALL_NAMES:
            return small_out[kind][n]
        if n in UPDATED_TRANSPOSED:
            return jnp.swapaxes(big[n][kind], 1, 2)
        return big[n][kind].reshape(w[n].shape)

    outs = [loss, dx.reshape(bsz, seq, D_MODEL)]
    for kind in range(4):
        outs += [out_of(kind, n) for n in ALL_NAMES]
    return tuple(outs)


def _local_shards_shape(nm):
    return {"w_in": (D_MODEL // N_DEV, PW), "w_uq": (MLA_Q_RANK // N_DEV, N_HEADS * MLA_QK_DIM),
            "w_ukv": (MLA_KV_RANK // N_DEV, QKW), "w_glu": (S5_WIDTH // N_DEV, S5_WIDTH),
            "w_branch": (3, S5_WIDTH, LANES), "w_out": (D_MODEL // N_DEV, D_MODEL), "w_up": (D_MODEL, 2 * FF_BLK),
            "conv": (3, 2 * FF_BLK), "w_down": (FF_BLK, D_MODEL)}[nm]
```

```python
import functools
import math

import numpy as np
import jax
import jax.numpy as jnp
from jax import lax
from jax.experimental import pallas as pl
from jax.experimental.pallas import tpu as pltpu

F32, BF16 = jnp.float32, jnp.bfloat16

D_MODEL = 1024
N_DEV = 8
MLA_Q_RANK, MLA_KV_RANK, MLA_ROPE_DIM, MLA_NOPE_DIM, MLA_QK_DIM = 384, 256, 32, 64, 96
N_HEADS, HEAD_V = 4, 64
S5_GROUPS, S5_GROUP_CH, S5_STATE, S5_WIDTH = 16, 16, 64, 256
S5_N = S5_GROUPS * S5_STATE
D_FF = 2816
D_IN = 4772
ROPE_THETA = 10000.0
NORM_EPS = 1e-6
NEG_INF = -1e30
ADAM_LR, ADAM_B1, ADAM_B2, ADAM_EPS, ADAM_WD, ADAM_STEP = 0.001, 0.9, 0.999, 1e-08, 0.01, 10

VMEM_LIMIT_BYTES = 56 * 1024 * 1024
LANES = 128
TM = 256
TQ = 512
QF = 128
QC = 512
HP = 128
QKW = N_HEADS * HP

PW = 5120
PA = 2048
IN_SEGS = ((0, 384, 256),
           (256, 672, 256),
           (512, 928, 256),
           (768, 1184, 256),
           (1024, 1444, 256),
           (1280, 640, 32),
           (1408, 1440, 4),
           (1536, 0, 384),
           (2048, 1700, 3072))
O_CKV, O_FQ, O_FK, O_FV, O_U, O_KR, O_FL, O_CQ = 0, 256, 512, 768, 1024, 1280, 1408, 1536

FF_BLK = 384
FF_HALF = D_FF // 8
FF_PW = 8 * FF_BLK
UP_W = 2 * FF_PW

NN = (((1,), (0,)), ((), ()))
NT = (((1,), (1,)), ((), ()))
TN = (((0,), (0,)), ((), ()))


def _pcall(body, *, name, grid, in_specs, out_specs, out_shape, scratch_shapes=(), rider=None, aliases=None):
    params = pltpu.CompilerParams(dimension_semantics=("arbitrary",) * len(grid), vmem_limit_bytes=VMEM_LIMIT_BYTES)
    if rider is None:
        return pl.pallas_call(body, name=name, grid=grid, in_specs=in_specs, out_specs=out_specs, out_shape=out_shape,
                              scratch_shapes=list(scratch_shapes), input_output_aliases=aliases or {},
                              compiler_params=params)
    assert not aliases
    in_specs, out_specs, out_shape = list(in_specs), list(out_specs), list(out_shape)
    n_in, n_out, n_scr, nx = len(in_specs), len(out_specs), len(scratch_shapes), rider.n
    any_spec = pl.BlockSpec(memory_space=pl.ANY)

    def at_step(last):
        conds = [pl.program_id(a) == (g - 1 if last else 0) for a, g in enumerate(grid)]
        return functools.reduce(jnp.logical_and, conds)

    def body_with_rider(*refs):
        ins, xsrc = refs[:n_in], refs[n_in:n_in + nx]
        outs, xout = refs[n_in + nx:n_in + nx + n_out], refs[n_in + nx + n_out:n_in + 2 * nx + n_out]
        scr = refs[n_in + 2 * nx + n_out:n_in + 2 * nx + n_out + n_scr]
        sems = refs[n_in + 2 * nx + n_out + n_scr:]

        @pl.when(at_step(False))
        def _():
            rider.start(rider.copies(xsrc, xout, *sems))

        body(*ins, *outs, *scr)

        @pl.when(at_step(True))
        def _():
            rider.wait(rider.copies(xsrc, xout, *sems))

    call = pl.pallas_call(body_with_rider, name=name, grid=grid, in_specs=in_specs + [any_spec] * nx,
                          out_specs=out_specs + [any_spec] * nx,
                          out_shape=out_shape + [jax.ShapeDtypeStruct(s, d) for s, d in rider.out_shapes],
                          scratch_shapes=list(scratch_shapes) + rider.sems(), compiler_params=params)
    return lambda *args: call(*args, *rider.srcs)


def _tile_call(fn, name, grid, ins, outs, rider=None):
    n_in = len(ins)

    def body(*refs):
        res = fn(*[r[...] for r in refs[:n_in]])
        if not isinstance(res, (tuple, list)):
            res = (res,)
        assert len(res) == len(outs), (name, len(res), len(outs))
        for r, v, o in zip(refs[n_in:], res, outs):
            v = v.astype(r.dtype)
            if o[4] is None:
                r[...] = v
            else:
                first = functools.reduce(jnp.logical_and, [pl.program_id(a) == 0 for a in o[4]])

                @pl.when(first)
                def _():
                    r[...] = v

                @pl.when(jnp.logical_not(first))
                def _():
                    r[...] += v

    res = _pcall(body, name=name, grid=grid,
                 in_specs=[pl.BlockSpec(b, m) for _, b, m in ins],
                 out_specs=[pl.BlockSpec(o[2], o[3]) for o in outs],
                 out_shape=[jax.ShapeDtypeStruct(o[0], o[1]) for o in outs], rider=rider)(*[a for a, _, _ in ins])
    return res


def _row(a, width=None, col_block=0, tm=TM):
    w = a.shape[1] if width is None else width
    return (a, (tm, w), lambda i, c=col_block: (i, c))


def _par(a):
    nd = a.ndim
    return (a, a.shape, lambda i, nd=nd: (0,) * nd)


def _orow(t, w, dtype=F32, tm=TM):
    return ((t, w), dtype, (tm, w), lambda i: (i, 0), None)


def _oacc(shape):
    nd = len(shape)
    return (tuple(shape), F32, tuple(shape), lambda i, nd=nd: (0,) * nd, (0,))


def _pick(n, target):
    best = None
    for d in range(LANES, min(n, target) + 1, LANES):
        if n % d == 0:
            best = d
    return n if best is None else best


MM_VMEM_BUDGET = 44 * 1024 * 1024


def _mm_tiles(m, n, k, sa, sb, so, tm, tn, tk):
    for cm, cn in ((1024, 2048), (1024, 1024), (512, 2048), (512, 1024), (512, 512), (256, 1024), (256, 512)):
        pm, pn = _pick(m, cm), _pick(n, cn)
        if 2 * (pm * k * sa + k * pn * sb + pm * pn * so) <= MM_VMEM_BUDGET:
            return pm, pn, k
    return _pick(m, tm), _pick(n, tn), _pick(k, tk)


def _mm(a, b, mode, name, out_dtype=F32, tm=1024, tn=2048, tk=1024, rider=None):
    if mode == "nn":
        (m, k), (_, n) = a.shape, b.shape
    elif mode == "nt":
        (m, k), (n, _) = a.shape, b.shape
    else:
        (k, m), (_, n) = a.shape, b.shape
    tm, tn, tk = _mm_tiles(m, n, k, a.dtype.itemsize, b.dtype.itemsize, jnp.dtype(out_dtype).itemsize, tm, tn, tk)
    nk = k // tk
    dims = {"nn": NN, "nt": NT, "tn": TN}[mode]
    a_spec = (pl.BlockSpec((tk, tm), lambda i, j, l: (l, i)) if mode == "tn"
              else pl.BlockSpec((tm, tk), lambda i, j, l: (i, l)))
    b_spec = (pl.BlockSpec((tn, tk), lambda i, j, l: (j, l)) if mode == "nt"
              else pl.BlockSpec((tk, tn), lambda i, j, l: (l, j)))

    def body(a_ref, b_ref, o_ref, acc_ref):
        part = lax.dot_general(a_ref[...].astype(BF16), b_ref[...].astype(BF16), dims, preferred_element_type=F32)
        if nk == 1:
            o_ref[...] = part.astype(o_ref.dtype)
        else:
            l = pl.program_id(2)

            @pl.when(l == 0)
            def _():
                acc_ref[...] = part

            @pl.when(l > 0)
            def _():
                acc_ref[...] += part

            @pl.when(l == nk - 1)
            def _():
                o_ref[...] = acc_ref[...].astype(o_ref.dtype)

    res = _pcall(body, name=name, grid=(m // tm, n // tn, nk), in_specs=[a_spec, b_spec],
                 out_specs=[pl.BlockSpec((tm, tn), lambda i, j, l: (i, j))],
                 out_shape=[jax.ShapeDtypeStruct((m, n), out_dtype)],
                 scratch_shapes=[pltpu.VMEM((tm, tn) if nk > 1 else (8, LANES), F32)], rider=rider)(a, b)
    return res[0] if rider is None else res


def _dot(x, w, dims):
    return lax.dot_general(x.astype(BF16), w.astype(BF16), dims, preferred_element_type=F32)


@jax.custom_vjp
def bdot(x, w):
    return _dot(x, w, NN)


def _bdot_fwd(x, w):
    return _dot(x, w, NN), (x, w)


def _bdot_bwd(res, g):
    x, w = res
    return _dot(g, w, NT), _dot(x, g, TN)


bdot.defvjp(_bdot_fwd, _bdot_bwd)


def _split3(x):
    x1 = x.astype(BF16)
    r1 = x - x1.astype(F32)
    x2 = r1.astype(BF16)
    x3 = (r1 - x2.astype(F32)).astype(BF16)
    return x1, x2, x3


def _cdot(x, m, dims):
    return sum(lax.dot_general(p, m, dims, preferred_element_type=F32) for p in _split3(x))


def _block_dot(x, m, transposed):
    k_in = m.shape[1] if transposed else m.shape[0]
    dims = NT if transposed else NN
    return jnp.concatenate([_cdot(x[:, b * k_in:(b + 1) * k_in], m, dims) for b in range(x.shape[1] // k_in)], axis=1)


@jax.custom_vjp
def cdot(x, m):
    return _block_dot(x, m, False)


def _cdot_fwd(x, m):
    return _block_dot(x, m, False), m


def _cdot_bwd(m, g):
    return _block_dot(g, m, True), jnp.zeros_like(m)


cdot.defvjp(_cdot_fwd, _cdot_bwd)


@jax.custom_vjp
def tile_heads(y):
    return jnp.concatenate([y] * N_HEADS, axis=1)


def _tile_heads_fwd(y):
    return jnp.concatenate([y] * N_HEADS, axis=1), None


def _tile_heads_bwd(_, g):
    return (sum(g[:, h * HP:(h + 1) * HP] for h in range(N_HEADS)),)


tile_heads.defvjp(_tile_heads_fwd, _tile_heads_bwd)


def _shift_rows_impl(x, s, reverse):
    n = x.shape[0]
    idx = lax.broadcasted_iota(jnp.int32, x.shape, 0)
    if reverse:
        return jnp.where(idx < n - s, pltpu.roll(x, n - s, 0), 0.0)
    return jnp.where(idx >= s, pltpu.roll(x, s, 0), 0.0)


@functools.partial(jax.custom_vjp, nondiff_argnums=(1,))
def shift_rows(x, s):
    return _shift_rows_impl(x, s, False)


def _shift_rows_fwd(x, s):
    return _shift_rows_impl(x, s, False), None


def _shift_rows_bwd(s, _, g):
    return (_shift_rows_impl(g, s, True),)


shift_rows.defvjp(_shift_rows_fwd, _shift_rows_bwd)


def _rms(x, g):
    return x * lax.rsqrt(jnp.mean(x * x, axis=-1, keepdims=True) + NORM_EPS) * g


def _head_rms(x, e_mat, inv_n, g):
    ms = cdot(x * x, e_mat) * inv_n
    return x * lax.rsqrt(ms + NORM_EPS) * g


def _rope(y, p_mat, cos, sin):
    return y * cos + cdot(y, p_mat) * sin


def _constants():
    e = np.ones((HP, HP), np.float32)
    p = np.zeros((HP, HP), np.float32)
    r = np.zeros((LANES, HP), np.float32)
    pl64 = np.zeros((2 * HEAD_V, 2 * HP), np.float32)
    half = MLA_ROPE_DIM // 2
    for i in range(half):
        x1, x2 = MLA_NOPE_DIM + i, MLA_NOPE_DIM + half + i
        p[x2, x1] = -1.0
        p[x1, x2] = 1.0
    for i in range(MLA_ROPE_DIM):
        r[i, MLA_NOPE_DIM + i] = 1.0
    for h in range(2):
        for i in range(HEAD_V):
            pl64[h * HEAD_V + i, h * HP + i] = 1.0
    return tuple(jnp.asarray(a, BF16) for a in (e, p, r, pl64))


def _inv_freq_row():
    inv = ROPE_THETA ** (-jnp.arange(0, MLA_ROPE_DIM, 2, dtype=F32) / MLA_ROPE_DIM)
    head = jnp.concatenate([jnp.zeros((MLA_NOPE_DIM,), F32), inv, inv, jnp.zeros((HP - MLA_QK_DIM,), F32)])
    return jnp.tile(head, N_HEADS).reshape(1, QKW)


def _mixer_pre(c_q, c_kv, kr, fq, fk, qlg, w_uq, kvlg, w_uk, w_uv, mqg, mkg, fqg, fkg, cos, sin, e_mat, p_mat, r_mat, pl_mat):
    q = bdot(_rms(c_q, qlg), w_uq)
    ckvn = _rms(c_kv, kvlg)
    k = bdot(ckvn, w_uk) + tile_heads(cdot(kr, r_mat))
    v = bdot(ckvn, w_uv)
    q = _rope(_head_rms(q, e_mat, 1.0 / MLA_QK_DIM, mqg), p_mat, cos, sin)
    k = _rope(_head_rms(k, e_mat, 1.0 / MLA_QK_DIM, mkg), p_mat, cos, sin)
    qf = _head_rms(cdot(fq, pl_mat), e_mat, 1.0 / HEAD_V, fqg)
    kf = _head_rms(cdot(fk, pl_mat), e_mat, 1.0 / HEAD_V, fkg)
    return q, k, v, qf, kf


def _s5_post(state, u, c_blk, d_row, w_glu, b_glu):
    y = bdot(state, c_blk) + d_row * u
    y = jax.nn.gelu(y)
    return y * jax.nn.sigmoid(bdot(y, w_glu) + b_glu)


def _merge(o_a, o_b, o_c, g_a, g_b, g_c, w_a, w_b, w_c):
    return (jax.nn.sigmoid(g_a) * bdot(o_a, w_a) + jax.nn.sigmoid(g_b) * bdot(o_b, w_b)
            + jax.nn.sigmoid(g_c) * bdot(o_c, w_c))


def _ffn_mid(gate, val, wg0, wg1, wg2, wv0, wv1, wv2):
    cg = wg0 * shift_rows(gate, 2) + wg1 * shift_rows(gate, 1) + wg2 * gate
    cv = wv0 * shift_rows(val, 2) + wv1 * shift_rows(val, 1) + wv2 * val
    return jax.nn.silu(cg) * cv


def _s5_params(lam_re, lam_im, log_step, b_re, b_im):
    step = jnp.exp(log_step)
    zr, zi = lam_re * step, lam_im * step
    mag = jnp.exp(zr)
    lr, li = mag * jnp.cos(zi), mag * jnp.sin(zi)
    nr, ni = lr - 1.0, li
    den = lam_re * lam_re + lam_im * lam_im
    cr = (nr * lam_re + ni * lam_im) / den
    ci = (ni * lam_re - nr * lam_im) / den
    return lr, li, cr * b_re - ci * b_im, cr * b_im + ci * b_re


def _adamw(w, g, m, v):
    m = ADAM_B1 * m + (1.0 - ADAM_B1) * g
    v = ADAM_B2 * v + (1.0 - ADAM_B2) * (g * g)
    m_hat = m / (1.0 - ADAM_B1 ** ADAM_STEP)
    v_hat = v / (1.0 - ADAM_B2 ** ADAM_STEP)
    delta = -ADAM_LR * (m_hat / (jnp.sqrt(v_hat) + ADAM_EPS) + ADAM_WD * w)
    return delta, m, v


SCAN_W = 256


def _scan_seq(xr, xi, ar, ai, reverse):
    n = xr.shape[0]
    s = 1
    while s < n:
        sr, si = _shift_rows_impl(xr, s, reverse), _shift_rows_impl(xi, s, reverse)
        xr, xi = xr + ar * sr - ai * si, xi + ar * si + ai * sr
        ar, ai = ar * ar - ai * ai, 2.0 * ar * ai
        s *= 2
    return xr, xi


def _scan_fwd(bu, lbar, bsz, name, rider=None):
    t = bu.shape[0]
    seq = t // bsz
    nb = bu.shape[1] // (2 * SCAN_W)

    def fn(b, a):
        xr, xi = _scan_seq(b[:, :SCAN_W], b[:, SCAN_W:], a[:, :SCAN_W], a[:, SCAN_W:], False)
        return jnp.concatenate([xr, xi], axis=1)

    blk = (seq, 2 * SCAN_W)
    return _tile_call(fn, name, (bsz, nb),
                      [(bu, blk, lambda i, j: (i, j)), (lbar, (1, 2 * SCAN_W), lambda i, j: (0, j))],
                      [((t, bu.shape[1]), F32, blk, lambda i, j: (i, j), None)], rider=rider)


def _scan_bwd(dstate, state, lbar, bsz, name, rider=None):
    t = state.shape[0]
    seq = t // bsz
    nb = state.shape[1] // (2 * SCAN_W)

    def fn(g, x, a):
        ar, ai = a[:, :SCAN_W], a[:, SCAN_W:]
        gr, gi = _scan_seq(g[:, :SCAN_W], g[:, SCAN_W:], ar, -ai, True)
        pr, pi = _shift_rows_impl(x[:, :SCAN_W], 1, False), _shift_rows_impl(x[:, SCAN_W:], 1, False)
        dar = jnp.sum(gr * pr + gi * pi, axis=0, keepdims=True)
        dai = jnp.sum(gi * pr - gr * pi, axis=0, keepdims=True)
        return jnp.concatenate([gr, gi], axis=1), jnp.concatenate([dar, dai], axis=1)

    blk = (seq, 2 * SCAN_W)
    return _tile_call(fn, name, (nb, bsz),
                      [(dstate, blk, lambda j, i: (i, j)), (state, blk, lambda j, i: (i, j)),
                       (lbar, (1, 2 * SCAN_W), lambda j, i: (0, j))],
                      [((t, state.shape[1]), F32, blk, lambda j, i: (i, j), None),
                       ((1, state.shape[1]), F32, (1, 2 * SCAN_W), lambda j, i: (0, j), (1,))], rider=rider)


def _shift_lanes(x, s, reverse):
    n = x.shape[1]
    idx = lax.broadcasted_iota(jnp.int32, x.shape, 1)
    if reverse:
        return jnp.where(idx < n - s, pltpu.roll(x, n - s, 1), 0.0)
    return jnp.where(idx >= s, pltpu.roll(x, s, 1), 0.0)


def _cumsum_lanes(x, reverse):
    s = 1
    while s < x.shape[1]:
        x = x + _shift_lanes(x, s, reverse)
        s *= 2
    return x


def _forget_fwd(z, bias, name):
    def fn(zv, bv):
        x = zv + bv
        logf = jnp.minimum(x, 0.0) - jnp.log(1.0 + jnp.exp(-jnp.abs(x)))
        return _cumsum_lanes(logf, False)

    return _tile_call(fn, name, (1,), [_par(z), _par(bias)],
                      [(z.shape, F32, z.shape, lambda i: (0, 0), None)])[0]


def _forget_bwd(dc, z, bias, name):
    def fn(dcv, zv, bv):
        x = zv + bv
        dlogf = _cumsum_lanes(dcv, True)
        dz = dlogf * jax.nn.sigmoid(-x)
        return dz, jnp.sum(dz, axis=1, keepdims=True)

    rows = z.shape[0]
    return _tile_call(fn, name, (1,), [_par(dc), _par(z), _par(bias)],
                      [(z.shape, F32, z.shape, lambda i: (0, 0), None),
                       ((rows, 1), F32, (rows, 1), lambda i: (0, 0), None)])


def _col(v2, x):
    idx = lax.broadcasted_iota(jnp.int32, v2.shape, 1)
    return jnp.sum(jnp.where(idx == x, v2, 0.0), axis=1, keepdims=True)


def _two_cols(c0, c1):
    idx = lax.broadcasted_iota(jnp.int32, (c0.shape[0], 2), 1)
    return jnp.where(idx == 0, c0, c1)


def _rows2(r0, r1):
    idx = lax.broadcasted_iota(jnp.int32, (2, r0.shape[1]), 0)
    return jnp.where(idx == 0, r0, r1)


def _attn_specs(seq, v_blk0, has_bias):
    qblk = pl.BlockSpec((None, TQ, 2 * HP), lambda b, p, i: (b, i, p))
    kblk = pl.BlockSpec((None, seq, 2 * HP), lambda b, p, i: (b, 0, p))
    vblk = pl.BlockSpec((None, seq, LANES), lambda b, p, i: (b, 0, v_blk0 + p))
    oblk = pl.BlockSpec((None, TQ, LANES), lambda b, p, i: (b, i, p))
    rowblk = pl.BlockSpec((None, None, 2, TQ), lambda b, p, i: (b, p, 0, i))
    colblk = pl.BlockSpec((None, None, seq, 2), lambda b, p, i: (b, p, 0, 0))
    return qblk, kblk, vblk, oblk, rowblk, colblk


def _attn_fwd(q, k, v, v_blk0, c_cols, c_rows, scale, name, rider=None):
    bsz, seq, _ = q.shape
    nq = seq // TQ
    has_bias = c_cols is not None
    n_in = 5 if has_bias else 3
    nc = TQ // QF

    def body(*refs):
        q_ref, k_ref, v_ref = refs[:3]
        cq_ref, ck_ref = refs[3:5] if has_bias else (None, None)
        o_ref, lse_ref = refs[n_in:]
        qi = pl.program_id(2)
        qb = [[q_ref[c * QF:(c + 1) * QF, x * HP:(x + 1) * HP].astype(BF16) for c in range(nc)] for x in range(2)]
        key_in = lax.broadcasted_iota(jnp.int32, (TQ, QF), 0)
        qry_in = lax.broadcasted_iota(jnp.int32, (TQ, QF), 1)

        def block(j, carry, masked):
            ks = pl.multiple_of(j * TQ, TQ)
            vt = v_ref[pl.ds(ks, TQ), :].T.astype(BF16)
            new = []
            for x in range(2):
                kx = k_ref[pl.ds(ks, TQ), x * HP:(x + 1) * HP].astype(BF16)
                ckx = _col(ck_ref[pl.ds(ks, TQ), :], x) if has_bias else None
                for c in range(nc):
                    m, l, acc = carry[x * nc + c]
                    s = lax.dot_general(kx, qb[x][c], NT, preferred_element_type=F32) * scale
                    if has_bias:
                        s = s + (cq_ref[x:x + 1, c * QF:(c + 1) * QF] - ckx)
                    if masked:
                        s = jnp.where(qry_in + c * QF >= key_in, s, NEG_INF)
                    m_new = jnp.maximum(m, jnp.max(s, axis=0, keepdims=True))
                    alpha = jnp.exp(m - m_new)
                    p = jnp.exp(s - m_new)
                    l = alpha * l + jnp.sum(p, axis=0, keepdims=True)
                    acc = alpha * acc + lax.dot_general(vt, p.astype(BF16), NN, preferred_element_type=F32)
                    new.append((m_new, l, acc))
            return tuple(new)

        init = tuple((jnp.full((1, QF), NEG_INF, F32), jnp.zeros((1, QF), F32), jnp.zeros((LANES, QF), F32))
                     for _ in range(2 * nc))
        carry = lax.fori_loop(0, qi, lambda j, cr: block(j, cr, False), init)
        carry = block(qi, carry, True)
        lane = lax.broadcasted_iota(jnp.int32, (QF, LANES), 1)
        lse_rows = []
        for x in range(2):
            lse_rows.append(jnp.concatenate([carry[x * nc + c][0] + jnp.log(carry[x * nc + c][1]) for c in range(nc)],
                                            axis=1))
        for c in range(nc):
            (_, l0, a0), (_, l1, a1) = carry[c], carry[nc + c]
            o_ref[c * QF:(c + 1) * QF, :] = jnp.where(lane < HEAD_V, (a0 / l0).T, (a1 / l1).T)
        lse_ref[...] = _rows2(lse_rows[0], lse_rows[1])

    qblk, kblk, vblk, oblk, rowblk, colblk = _attn_specs(seq, v_blk0, has_bias)
    in_specs, args = [qblk, kblk, vblk], [q, k, v]
    if has_bias:
        in_specs += [rowblk, colblk]
        args += [c_rows, c_cols]
    return _pcall(body, name=name, grid=(bsz, 2, nq), in_specs=in_specs, out_specs=[oblk, rowblk],
                  out_shape=[jax.ShapeDtypeStruct((bsz, seq, 2 * LANES), F32),
                             jax.ShapeDtypeStruct((bsz, 2, 2, seq), F32)], rider=rider)(*args)


def _attn_bwd(q, k, v, v_blk0, o, do, lse, c_cols, c_rows, scale, name, rider=None):
    bsz, seq, _ = q.shape
    nq = seq // TQ
    has_bias = c_cols is not None
    n_in = 8 if has_bias else 6
    nc = TQ // QC

    def body(*refs):
        q_ref, k_ref, v_ref, o_ref, do_ref, lse_ref = refs[:6]
        cq_ref, ck_ref = refs[6:8] if has_bias else (None, None)
        dq_ref, dk_ref, dv_ref = refs[n_in:n_in + 3]
        dck_ref, dcq_ref = refs[n_in + 3:n_in + 5] if has_bias else (None, None)
        qi = pl.program_id(2)

        @pl.when(qi == 0)
        def _():
            dk_ref[...] = jnp.zeros_like(dk_ref)
            dv_ref[...] = jnp.zeros_like(dv_ref)
            if has_bias:
                dck_ref[...] = jnp.zeros_like(dck_ref)

        lane = lax.broadcasted_iota(jnp.int32, (QC, LANES), 1)
        ones8 = jnp.ones((8, LANES), BF16)
        qb, dob, delta, lse_r = [], [], [], []
        for x in range(2):
            hm = jnp.logical_and(lane >= x * HEAD_V, lane < (x + 1) * HEAD_V)
            qb.append([]), dob.append([]), delta.append([]), lse_r.append([])
            for c in range(nc):
                rows = slice(c * QC, (c + 1) * QC)
                do_c = jnp.where(hm, do_ref[rows, :], 0.0)
                prod = do_c * o_ref[rows, :]
                hi = prod.astype(BF16)
                lo = (prod - hi.astype(F32)).astype(BF16)
                d8 = (lax.dot_general(ones8, hi, NT, preferred_element_type=F32)
                      + lax.dot_general(ones8, lo, NT, preferred_element_type=F32))
                delta[x].append(d8[0:1, :])
                dob[x].append(do_c.astype(BF16))
                qb[x].append(q_ref[rows, x * HP:(x + 1) * HP].astype(BF16))
                lse_r[x].append(lse_ref[x:x + 1, rows])
        key_in = lax.broadcasted_iota(jnp.int32, (TQ, QC), 0)
        qry_in = lax.broadcasted_iota(jnp.int32, (TQ, QC), 1)

        def block(j, carry, masked):
            ks = pl.multiple_of(j * TQ, TQ)
            vb = v_ref[pl.ds(ks, TQ), :].astype(BF16)
            new, key_sums = [], []
            for x in range(2):
                k32 = k_ref[pl.ds(ks, TQ), x * HP:(x + 1) * HP]
                kx, kt = k32.astype(BF16), k32.T.astype(BF16)
                ckx = _col(ck_ref[pl.ds(ks, TQ), :], x) if has_bias else None
                dk_acc = jnp.zeros((TQ, HP), F32)
                dv_acc = jnp.zeros((TQ, LANES), F32)
                key_sum = jnp.zeros((TQ, 1), F32)
                for c in range(nc):
                    dqt, dcq = carry[x * nc + c]
                    s = lax.dot_general(kx, qb[x][c], NT, preferred_element_type=F32) * scale
                    if has_bias:
                        s = s + (cq_ref[x:x + 1, c * QC:(c + 1) * QC] - ckx)
                    if masked:
                        s = jnp.where(qry_in + c * QC >= key_in, s, NEG_INF)
                    p = jnp.exp(s - lse_r[x][c])
                    dp = lax.dot_general(vb, dob[x][c], NT, preferred_element_type=F32)
                    ds = p * (dp - delta[x][c])
                    dsb = ds.astype(BF16)
                    dk_acc = dk_acc + lax.dot_general(dsb, qb[x][c], NN, preferred_element_type=F32)
                    dv_acc = dv_acc + lax.dot_general(p.astype(BF16), dob[x][c], NN, preferred_element_type=F32)
                    dqt = dqt + lax.dot_general(kt, dsb, NN, preferred_element_type=F32)
                    if has_bias:
                        dcq = dcq + jnp.sum(ds, axis=0, keepdims=True)
                        key_sum = key_sum + jnp.sum(ds, axis=1, keepdims=True)
                    new.append((dqt, dcq))
                dk_ref[pl.ds(ks, TQ), x * HP:(x + 1) * HP] += dk_acc * scale
                dv_ref[pl.ds(ks, TQ), :] += dv_acc
                key_sums.append(key_sum)
            if has_bias:
                dck_ref[pl.ds(ks, TQ), :] -= _two_cols(key_sums[0], key_sums[1])
            return tuple(new)

        init = tuple((jnp.zeros((HP, QC), F32), jnp.zeros((1, QC), F32)) for _ in range(2 * nc))
        carry = lax.fori_loop(0, qi, lambda j, cr: block(j, cr, False), init)
        carry = block(qi, carry, True)
        for x in range(2):
            for c in range(nc):
                dq_ref[c * QC:(c + 1) * QC, x * HP:(x + 1) * HP] = carry[x * nc + c][0].T * scale
        if has_bias:
            dcq_ref[...] = _rows2(jnp.concatenate([carry[c][1] for c in range(nc)], axis=1),
                                  jnp.concatenate([carry[nc + c][1] for c in range(nc)], axis=1))

    qblk, kblk, vblk, oblk, rowblk, colblk = _attn_specs(seq, v_blk0, has_bias)
    in_specs = [qblk, kblk, vblk, oblk, oblk, rowblk]
    out_specs = [qblk, kblk, pl.BlockSpec((None, seq, LANES), lambda b, p, i: (b, 0, p))]
    out_shape = [jax.ShapeDtypeStruct((bsz, seq, QKW), F32), jax.ShapeDtypeStruct((bsz, seq, QKW), F32),
                 jax.ShapeDtypeStruct((bsz, seq, 2 * LANES), F32)]
    args = [q, k, v, o, do, lse]
    if has_bias:
        in_specs += [rowblk, colblk]
        out_specs += [colblk, rowblk]
        out_shape += [jax.ShapeDtypeStruct((bsz, 2, seq, 2), F32), jax.ShapeDtypeStruct((bsz, 2, 2, seq), F32)]
        args += [c_rows, c_cols]
    return _pcall(body, name=name, grid=(bsz, 2, nq), in_specs=in_specs, out_specs=out_specs,
                  out_shape=out_shape, rider=rider)(*args)


def _coords(idx):
    return (idx // 4, (idx // 2) % 2, idx % 2)


class _Xfer:
    def __init__(self, srcs, out_shapes, plan):
        self.srcs, self.out_shapes, self.plan = list(srcs), list(out_shapes), plan
        self.n = len(self.srcs)
        me0 = jnp.int32(0)
        n_pieces = [len(plan(a, me0, me0)) for a in range(self.n)]
        self.offs = np.concatenate([[0], np.cumsum(n_pieces)]).astype(int)
        self.total = int(self.offs[-1])

    def sems(self):
        remote = (N_DEV - 1) * self.total
        return [pltpu.SemaphoreType.DMA((remote,)), pltpu.SemaphoreType.DMA((remote,)),
                pltpu.SemaphoreType.DMA((self.total,))]

    def copies(self, src_refs, out_refs, send_sems, recv_sems, local_sems):
        me = 4 * lax.axis_index("x") + 2 * lax.axis_index("y") + lax.axis_index("c")
        out = []
        for a in range(self.n):
            for pi, (si, di) in enumerate(self.plan(a, me, me)):
                out.append((pltpu.make_async_copy(src_refs[a].at[si], out_refs[a].at[di],
                                                  local_sems.at[self.offs[a] + pi]), None))
        for kk in range(1, N_DEV):
            dest = (me + kk) % N_DEV
            src_dev = (me + N_DEV - kk) % N_DEV
            for a in range(self.n):
                pieces = self.plan(a, me, dest)
                landing = self.plan(a, src_dev, me)
                for pi, ((si, di), (_, li)) in enumerate(zip(pieces, landing)):
                    sem = (kk - 1) * self.total + self.offs[a] + pi
                    mk = functools.partial(pltpu.make_async_remote_copy, src_ref=src_refs[a].at[si],
                                           send_sem=send_sems.at[sem], recv_sem=recv_sems.at[sem],
                                           device_id=_coords(dest), device_id_type=pl.DeviceIdType.MESH)
                    out.append((mk(dst_ref=out_refs[a].at[di]), mk(dst_ref=out_refs[a].at[li])))
        return out

    @staticmethod
    def start(copies):
        for cp, _ in copies:
            cp.start()

    @staticmethod
    def wait(copies):
        for cp, rc in copies:
            if rc is None:
                cp.wait()
            else:
                cp.wait_send()
                rc.wait_recv()


def _exchange(name, xf):
    n = xf.n

    def body(*refs):
        copies = xf.copies(refs[:n], refs[n:2 * n], *refs[2 * n:])
        xf.start(copies)
        xf.wait(copies)

    any_spec = pl.BlockSpec(memory_space=pl.ANY)
    return pl.pallas_call(body, name=name, in_specs=[any_spec] * n, out_specs=[any_spec] * n,
                          out_shape=[jax.ShapeDtypeStruct(s, d) for s, d in xf.out_shapes],
                          scratch_shapes=xf.sems(),
                          compiler_params=pltpu.CompilerParams(has_side_effects=True))(*xf.srcs)


class _Gather2(_Xfer):
    def copies(self, src_refs, out_refs, send_sems, recv_sems, local_sems):
        x, y, c = lax.axis_index("x"), lax.axis_index("y"), lax.axis_index("c")
        num = lambda px, py, pc: 4 * px + 2 * py + pc
        me, sib = (x, y, c), (x, y, 1 - c)
        chips = [(1 - x, y), (x, 1 - y), (1 - x, 1 - y)]
        local, first, passed, landing = [], [], [], []
        for a in range(self.n):
            mine = self.plan(a, num(*me), num(*me))
            for pi, (si, di) in enumerate(mine):
                piece = self.offs[a] + pi

                def remote(k, src, dst, to, a=a, piece=piece):
                    return pltpu.make_async_remote_copy(
                        src_ref=src, dst_ref=dst, send_sem=send_sems.at[k * self.total + piece],
                        recv_sem=recv_sems.at[k * self.total + piece], device_id=to, device_id_type=pl.DeviceIdType.MESH)

                def slot(dev, a=a, pi=pi):
                    return out_refs[a].at[self.plan(a, num(*dev), num(*dev))[pi][1]]

                local.append(pltpu.make_async_copy(src_refs[a].at[si], out_refs[a].at[di], local_sems.at[piece]))
                first.append(remote(0, src_refs[a].at[si], out_refs[a].at[di], sib))
                landing.append([remote(0, slot(sib), slot(sib), sib)])
                for j, chip in enumerate(chips):
                    first.append(remote(1 + j, src_refs[a].at[si], out_refs[a].at[di], (*chip, c)))
                    passed.append((remote(1 + j, slot((*chip, c)), slot((*chip, c)), (*chip, c)),
                                   remote(4 + j, slot((*chip, c)), slot((*chip, c)), sib)))
                    landing[-1].append(remote(4 + j, slot((*chip, 1 - c)), slot((*chip, 1 - c)), sib))
        return local, first, passed, landing

    @staticmethod
    def start(copies):
        local, first, _, _ = copies
        for cp in local + first:
            cp.start()

    @staticmethod
    def wait(copies):
        local, first, passed, landing = copies
        for arrival, forward in passed:
            arrival.wait_recv()
            forward.start()
        for group in landing:
            for cp in group:
                cp.wait_recv()
        for cp in first + [fw for _, fw in passed]:
            cp.wait_send()
        for cp in local:
            cp.wait()


def _rows_of(j, n):
    return pl.ds(pl.multiple_of(j * n, n), n)


def _ffn_cols(j, half):
    return pl.multiple_of((2 * (j % 4) + half) * (2 * FF_BLK) + (j // 4) * FF_BLK, FF_BLK)


def _gather_plan(shards):
    names = list(shards)

    def plan(a, src_dev, _dest):
        nm = names[a]
        j = src_dev
        if nm in ("w_in", "w_uq", "w_ukv", "w_glu", "w_out", "w_down"):
            rows = shards[nm].shape[0]
            return [((slice(None), slice(None)), (_rows_of(j, rows), slice(None)))]
        if nm == "w_branch":
            return [((slice(None), slice(None), slice(None)), (slice(None), slice(None), _rows_of(j, LANES)))]
        if nm in ("w_up", "conv"):
            return [((slice(None), pl.ds(h * FF_BLK, FF_BLK)), (slice(None), pl.ds(_ffn_cols(j, h), FF_BLK)))
                    for h in range(2)]
        raise KeyError(nm)

    return names, plan


def _scatter_plan(names, shard_shapes):
    def plan(a, src_dev, dest):
        nm = names[a]
        j = dest
        if nm in ("w_in", "w_uq", "w_ukv", "w_glu", "w_out", "w_down"):
            rows = shard_shapes[nm][0]
            return [((_rows_of(j, rows), slice(None)), (src_dev, slice(None), slice(None)))]
        if nm == "w_branch":
            return [((slice(None), slice(None), _rows_of(j, LANES)), (src_dev, slice(None), slice(None), slice(None)))]
        if nm in ("w_up", "conv"):
            return [((slice(None), pl.ds(_ffn_cols(j, h), FF_BLK)), (src_dev, slice(None), pl.ds(h * FF_BLK, FF_BLK)))
                    for h in range(2)]
        if nm in ("w_up:0", "w_up:1"):
            h = int(nm[-1])
            return [((slice(None), pl.ds(_ffn_cols(j, h), FF_BLK)), (src_dev, slice(None), slice(None)))]
        if nm == "small":
            return [((slice(None), slice(None)), (src_dev, slice(None), slice(None)))]
        raise KeyError(nm)

    return plan


def _pad_in_cols(w):
    lead = w.shape[:-1]
    parts, pos = [], 0
    for dst, src, width in IN_SEGS:
        if dst > pos:
            parts.append(jnp.zeros(lead + (dst - pos,), w.dtype))
        parts.append(w[..., src:src + width])
        pos = dst + width
    return jnp.concatenate(parts, axis=-1)


def _unpad_in_cols(w):
    order = sorted(IN_SEGS, key=lambda s: s[1])
    return jnp.concatenate([w[..., dst:dst + width] for dst, _, width in order], axis=-1)


def _pad_ff(w, axis):
    n = w.shape[axis] // FF_HALF
    parts = []
    for h in range(n):
        piece = lax.slice_in_dim(w, h * FF_HALF, (h + 1) * FF_HALF, axis=axis)
        zshape = list(w.shape)
        zshape[axis] = FF_BLK - FF_HALF
        parts += [piece, jnp.zeros(zshape, w.dtype)]
    return jnp.concatenate(parts, axis=axis)


def _unpad_ff(w, axis):
    n = w.shape[axis] // FF_BLK
    return jnp.concatenate([lax.slice_in_dim(w, h * FF_BLK, h * FF_BLK + FF_HALF, axis=axis) for h in range(n)],
                           axis=axis)


def _to_scan_cols(re_part, im_part):
    lead = re_part.shape[:-1]
    nb = S5_N // SCAN_W
    r = re_part.reshape(lead + (nb, SCAN_W))
    i = im_part.reshape(lead + (nb, SCAN_W))
    return jnp.concatenate([r, i], axis=-1).reshape(lead + (2 * S5_N,))


def _from_scan_cols(x):
    lead = x.shape[:-1]
    nb = S5_N // SCAN_W
    y = x.reshape(lead + (nb, 2, SCAN_W))
    return y[..., 0, :].reshape(lead + (S5_N,)), y[..., 1, :].reshape(lead + (S5_N,))


def _block_diag_in(b):
    eye = jnp.eye(S5_GROUPS, dtype=b.dtype)
    return jnp.einsum("gph,gk->ghkp", b, eye).reshape(S5_WIDTH, S5_N)


def _block_diag_in_t(m):
    eye = jnp.eye(S5_GROUPS, dtype=m.dtype)
    return jnp.einsum("ghkp,gk->gph", m.reshape(S5_GROUPS, S5_GROUP_CH, S5_GROUPS, S5_STATE), eye)


def _block_diag_out(c):
    eye = jnp.eye(S5_GROUPS, dtype=c.dtype)
    return jnp.einsum("ghp,gk->gpkh", c, eye).reshape(S5_N, S5_WIDTH)


def _block_diag_out_t(m):
    eye = jnp.eye(S5_GROUPS, dtype=m.dtype)
    return jnp.einsum("gpkh,gk->ghp", m.reshape(S5_GROUPS, S5_STATE, S5_GROUPS, S5_GROUP_CH), eye)


def _pad_heads(g, width):
    g = jnp.broadcast_to(g.reshape(-1, width), (N_HEADS, width))
    return jnp.pad(g, ((0, 0), (0, HP - width))).reshape(1, QKW)


def _prep_layer(l, gw, small, consts):
    e_mat, p_mat, r_mat, pl_mat = consts
    p = {}
    p["attn_g"] = small["attn_norm_g"][l].reshape(1, D_MODEL)
    p["ffn_g"] = small["ffn_norm_g"][l].reshape(1, D_MODEL)
    p["qlg"] = small["q_lat_norm_g"][l].reshape(1, MLA_Q_RANK)
    p["kvlg"] = small["kv_lat_norm_g"][l].reshape(1, MLA_KV_RANK)
    p["mqg"] = _pad_heads(small["mla_q_norm_g"][l], MLA_QK_DIM)
    p["mkg"] = _pad_heads(small["mla_k_norm_g"][l], MLA_QK_DIM)
    p["fqg"] = _pad_heads(small["fox_q_norm_g"][l], HEAD_V)
    p["fkg"] = _pad_heads(small["fox_k_norm_g"][l], HEAD_V)
    w_uq = gw["w_uq"].reshape(MLA_Q_RANK, N_HEADS, MLA_QK_DIM)
    p["w_uq"] = jnp.pad(w_uq, ((0, 0), (0, 0), (0, HP - MLA_QK_DIM))).reshape(MLA_Q_RANK, QKW)
    w_ukv = gw["w_ukv"].reshape(MLA_KV_RANK, N_HEADS, MLA_NOPE_DIM + HEAD_V)
    p["w_uk"] = jnp.pad(w_ukv[..., :MLA_NOPE_DIM], ((0, 0), (0, 0), (0, HP - MLA_NOPE_DIM))).reshape(MLA_KV_RANK, QKW)
    p["w_uv"] = w_ukv[..., MLA_NOPE_DIM:].reshape(MLA_KV_RANK, N_HEADS * HEAD_V)
    p["w_glu"] = gw["w_glu"]
    p["b_glu"] = small["s5_b_glu"][l].reshape(1, S5_WIDTH)
    p["d_row"] = small["s5_d"][l].reshape(1, S5_WIDTH)
    p["lam_re"] = small["s5_lambda_re"][l].reshape(S5_N, 1)
    p["lam_im"] = small["s5_lambda_im"][l].reshape(S5_N, 1)
    p["log_step"] = jnp.repeat(small["s5_log_step"][l], S5_STATE).reshape(S5_N, 1)
    p["b_re"] = small["s5_b_re"][l].reshape(S5_N, S5_GROUP_CH)
    p["b_im"] = small["s5_b_im"][l].reshape(S5_N, S5_GROUP_CH)
    c_re, c_im = small["s5_c_re"][l], small["s5_c_im"][l]
    p["c_blk"] = _to_scan_cols(_block_diag_out(c_re).T, -_block_diag_out(c_im).T).T
    p["fbias"] = small["fox_f_bias"][l]
    for nm in ("w_in", "w_out", "w_branch"):
        p[nm] = gw[nm]
    return p


def _s5_param_call(p, l):
    ins = [p["lam_re"], p["lam_im"], p["log_step"], p["b_re"], p["b_im"]]
    outs = [((S5_N, 1), F32, (S5_N, 1), lambda i: (0, 0), None)] * 2 + \
           [((S5_N, S5_GROUP_CH), F32, (S5_N, S5_GROUP_CH), lambda i: (0, 0), None)] * 2
    return _tile_call(_s5_params, "s5_params", (1,), [_par(a) for a in ins], outs)


def _s5_param_bwd_call(p, cts):
    ins = [p["lam_re"], p["lam_im"], p["log_step"], p["b_re"], p["b_im"]]

    def fn(lr, li, ls, br, bi, g0, g1, g2, g3):
        _, vjp = jax.vjp(_s5_params, lr, li, ls, br, bi)
        return vjp((g0, g1, g2, g3))

    outs = [((S5_N, 1), F32, (S5_N, 1), lambda i: (0, 0), None)] * 3 + \
           [((S5_N, S5_GROUP_CH), F32, (S5_N, S5_GROUP_CH), lambda i: (0, 0), None)] * 2
    return _tile_call(fn, "s5_params_bwd", (1,), [_par(a) for a in ins + list(cts)], outs)


def _layer_fwd(x_prev, f_prev, p, tabs, consts, bsz, riders):
    arrived = {}

    def rider_of(name):
        return riders[name][0] if name in riders else None

    def hosted(name, res, n_out):
        if name in riders:
            arrived.update(zip(riders[name][1], res[n_out:]))
        return res[:n_out]

    t = x_prev.shape[0]
    seq = t // bsz
    nt = t // TM
    cos, sin = tabs
    e_mat, p_mat, r_mat, pl_mat = consts
    sv = {}

    if f_prev is None:
        x = x_prev
        h = _tile_call(lambda xv, g: _rms(xv, g), "norm_first", (nt,), [_row(x), _par(p["attn_g"])],
                       [_orow(t, D_MODEL, BF16)])[0]
    else:
        x, h = _tile_call(lambda xv, fv, g: (xv + fv, _rms(xv + fv, g)), "norm_attn", (nt,),
                          [_row(x_prev), _row(f_prev), _par(p["attn_g"])], [_orow(t, D_MODEL), _orow(t, D_MODEL, BF16)])
    sv["x"], sv["h"] = x, h
    proj = _mm(h, p["w_in"], "nn", "mm_in")
    sv["proj"] = proj

    def pre_fn(pa, cosv, sinv, qlg, w_uq, kvlg, w_uk, w_uv, mqg, mkg, fqg, fkg, em, pm, rm, plm):
        return _mixer_pre(pa[:, O_CQ:O_CQ + MLA_Q_RANK], pa[:, O_CKV:O_CKV + MLA_KV_RANK], pa[:, O_KR:O_KR + LANES],
                          pa[:, O_FQ:O_FQ + 256], pa[:, O_FK:O_FK + 256], qlg, w_uq, kvlg, w_uk, w_uv,
                          mqg, mkg, fqg, fkg, cosv, sinv, em, pm, rm, plm)

    pre_params = [p["qlg"], p["w_uq"], p["kvlg"], p["w_uk"], p["w_uv"], p["mqg"], p["mkg"], p["fqg"], p["fkg"],
                  e_mat, p_mat, r_mat, pl_mat]
    q_m, k_m, v_m, q_f, k_f = hosted("mixer_pre", _tile_call(
        pre_fn, "mixer_pre", (nt,), [_row(proj, PA), _row(cos), _row(sin)] + [_par(a) for a in pre_params],
        [_orow(t, QKW), _orow(t, QKW), _orow(t, 2 * LANES), _orow(t, QKW), _orow(t, QKW)],
        rider=rider_of("mixer_pre")), 5)
    sv.update(q_m=q_m, k_m=k_m, v_m=v_m, q_f=q_f, k_f=k_f)

    z = proj[:, O_FL:O_FL + N_HEADS].reshape(bsz, seq, N_HEADS).transpose(0, 2, 1).reshape(bsz * N_HEADS, seq)
    fb = jnp.tile(p["fbias"], bsz).reshape(bsz * N_HEADS, 1)
    cum = _forget_fwd(z, fb, "forget_fwd")
    ck = cum.reshape(bsz, 2, 2, seq)
    cq = ck.transpose(0, 1, 3, 2)
    sv.update(z=z, fb=fb, cq=cq, ck=ck)

    r3 = lambda a: a.reshape(bsz, seq, a.shape[-1])
    o_m, lse_m = hosted("attn_mla_fwd", _attn_fwd(r3(q_m), r3(k_m), r3(v_m), 0, None, None, MLA_QK_DIM ** -0.5,
                                                  "attn_mla_fwd", rider_of("attn_mla_fwd")), 2)
    o_f, lse_f = hosted("attn_fox_fwd", _attn_fwd(r3(q_f), r3(k_f), r3(proj), O_FV // LANES, cq, ck, HEAD_V ** -0.5,
                                                  "attn_fox_fwd", rider_of("attn_fox_fwd")), 2)
    o_m, o_f = o_m.reshape(t, 2 * LANES), o_f.reshape(t, 2 * LANES)
    sv.update(o_m=o_m, o_f=o_f, lse_m=lse_m, lse_f=lse_f)

    lbar_re, lbar_im, bbar_re, bbar_im = _s5_param_call(p, 0)
    lbar = _to_scan_cols(lbar_re.reshape(1, S5_N), lbar_im.reshape(1, S5_N))
    b_blk = _to_scan_cols(_block_diag_in(bbar_re.reshape(S5_GROUPS, S5_STATE, S5_GROUP_CH)),
                          _block_diag_in(bbar_im.reshape(S5_GROUPS, S5_STATE, S5_GROUP_CH)))
    sv.update(lbar=lbar, b_blk=b_blk)
    bu = _tile_call(lambda u, bb: bdot(u, bb), "s5_bu", (nt,), [_row(proj, S5_WIDTH, O_U // S5_WIDTH), _par(b_blk)],
                    [_orow(t, 2 * S5_N)])[0]
    state = hosted("s5_scan_fwd", _scan_fwd(bu, lbar, bsz, "s5_scan_fwd", rider_of("s5_scan_fwd")), 1)[0]
    sv["state"] = state
    post_params = [p["c_blk"], p["d_row"], p["w_glu"], p["b_glu"]]
    o_s = _tile_call(_s5_post, "s5_post", (nt,),
                     [_row(state), _row(proj, S5_WIDTH, O_U // S5_WIDTH)] + [_par(a) for a in post_params],
                     [_orow(t, S5_WIDTH)])[0]
    sv["o_s"] = o_s

    def merge_fn(oa, ob, oc, ga, gb, gc, wb):
        return _merge(oa, ob, oc, ga, gb, gc, wb[0], wb[1], wb[2])

    gate_specs = [_row(proj, D_MODEL, PA // D_MODEL + n) for n in range(3)]
    merged = _tile_call(merge_fn, "merge", (nt,), [_row(o_m), _row(o_f), _row(o_s)] + gate_specs + [_par(p["w_branch"])],
                        [_orow(t, D_MODEL, BF16)])[0]
    sv["merged"] = merged
    attn_out = _mm(merged, p["w_out"], "nn", "mm_out")

    x1, h2 = _tile_call(lambda xv, av, g: (xv + av, _rms(xv + av, g)), "norm_ffn", (nt,),
                        [_row(x), _row(attn_out), _par(p["ffn_g"])], [_orow(t, D_MODEL), _orow(t, D_MODEL, BF16)])
    sv["x1"], sv["h2"] = x1, h2
    p.update({n: arrived.pop(n) for n in FFN_WEIGHTS})
    up = _mm(h2, p["w_up"], "nn", "mm_up")
    sv["up"] = up

    def ffn_fn(upv, cw):
        return _ffn_mid(upv[:, :FF_BLK], upv[:, FF_BLK:], cw[0:1, :FF_BLK], cw[1:2, :FF_BLK], cw[2:3, :FF_BLK],
                        cw[0:1, FF_BLK:], cw[1:2, FF_BLK:], cw[2:3, FF_BLK:])

    nblk = FF_PW // FF_BLK
    act = _tile_call(ffn_fn, "ffn_mid", (nblk, bsz),
                     [(up, (seq, 2 * FF_BLK), lambda j, b: (b, j)), (p["conv"], (3, 2 * FF_BLK), lambda j, b: (0, j))],
                     [((t, FF_PW), BF16, (seq, FF_BLK), lambda j, b: (b, j), None)])[0]
    sv["act"] = act
    ffn_out = _mm(act, p["w_down"], "nn", "mm_down")
    return x1, ffn_out, sv, arrived


def _layer_bwd(dx2, p, sv, tabs, consts, bsz, scatter, carry, last_layer):
    lands, carry_lands = {}, {}

    def riding(host, g):
        names = BWD_RIDERS[host] + (("w_in",) if host == "mm_in_dx" and last_layer else ())
        return scatter({n: g[n.split(":")[0]] for n in names}), names

    def landed(host, res, n_out):
        names = BWD_RIDERS[host] + (("w_in",) if host == "mm_in_dx" and last_layer else ())
        lands.update(zip(names, res[n_out:]))
        return res[:n_out]

    t = dx2.shape[0]
    seq = t // bsz
    nt = t // TM
    cos, sin = tabs
    e_mat, p_mat, r_mat, pl_mat = consts
    g = {}

    g["w_down"] = _mm(sv["act"], dx2, "tn", "mm_down_dw", out_dtype=BF16)
    dact = _mm(dx2, p["w_down"], "nt", "mm_down_dx")

    def ffn_bwd_fn(upv, cw, da):
        args = (upv[:, :FF_BLK], upv[:, FF_BLK:], cw[0:1, :FF_BLK], cw[1:2, :FF_BLK], cw[2:3, :FF_BLK],
                cw[0:1, FF_BLK:], cw[1:2, FF_BLK:], cw[2:3, FF_BLK:])
        _, vjp = jax.vjp(_ffn_mid, *args)
        dg, dv, g0, g1, g2, v0, v1, v2 = vjp(da)
        return (jnp.concatenate([dg, dv], axis=1), jnp.concatenate([g0, v0], axis=1),
                jnp.concatenate([g1, v1], axis=1), jnp.concatenate([g2, v2], axis=1))

    nblk = FF_PW // FF_BLK
    cw_out = ((1, UP_W), F32, (1, 2 * FF_BLK), lambda j, b: (0, j), (1,))
    res = _tile_call(
        ffn_bwd_fn, "ffn_mid_bwd", (nblk, bsz),
        [(sv["up"], (seq, 2 * FF_BLK), lambda j, b: (b, j)), (p["conv"], (3, 2 * FF_BLK), lambda j, b: (0, j)),
         (dact, (seq, FF_BLK), lambda j, b: (b, j))],
        [((t, UP_W), BF16, (seq, 2 * FF_BLK), lambda j, b: (b, j), None), cw_out, cw_out, cw_out],
        rider=scatter(carry) if carry else None)
    dup, dc0, dc1, dc2 = res[:4]
    carry_lands.update(zip(carry, res[4:]))
    g["conv"] = jnp.concatenate([dc0, dc1, dc2], axis=0)
    g["w_up"] = _mm(sv["h2"], dup, "tn", "mm_up_dw", out_dtype=BF16)
    dh2 = landed("mm_up_dx", _mm(dup, p["w_up"], "nt", "mm_up_dx", rider=riding("mm_up_dx", g)[0]), 1)[0]

    def norm_bwd_fn(xv, gv, dh, dres):
        _, vjp = jax.vjp(_rms, xv, gv)
        dxv, dg = vjp(dh)
        return dres + dxv, dg

    d1, g["ffn_g"] = _tile_call(norm_bwd_fn, "norm_bwd", (nt,),
                                [_row(sv["x1"]), _par(p["ffn_g"]), _row(dh2), _row(dx2)],
                                [_orow(t, D_MODEL), _oacc((1, D_MODEL))])

    g["w_out"] = _mm(sv["merged"], d1, "tn", "mm_out_dw", out_dtype=BF16)
    dmerged = _mm(d1, p["w_out"], "nt", "mm_out_dx")

    def merge_bwd_fn(oa, ob, oc, ga, gb, gc, wb, dm):
        wf = wb.astype(F32)
        _, vjp = jax.vjp(_merge, oa, ob, oc, ga, gb, gc, wf[0], wf[1], wf[2])
        doa, dob, doc, dga, dgb, dgc, dwa, dwb, dwc = vjp(dm)
        return doa, dob, doc, jnp.concatenate([dga, dgb, dgc], axis=1), jnp.stack([dwa, dwb, dwc])

    proj = sv["proj"]
    gate_specs = [_row(proj, D_MODEL, PA // D_MODEL + n) for n in range(3)]
    do_m, do_f, do_s, dgl, g["w_branch"] = _tile_call(
        merge_bwd_fn, "merge_bwd", (nt,),
        [_row(sv["o_m"]), _row(sv["o_f"]), _row(sv["o_s"])] + gate_specs + [_par(p["w_branch"]), _row(dmerged)],
        [_orow(t, 2 * LANES), _orow(t, 2 * LANES), _orow(t, S5_WIDTH), _orow(t, 3 * D_MODEL),
         _oacc((3, S5_WIDTH, D_MODEL))])

    def post_bwd_fn(st, u, cb, dr, wg, bg, do):
        _, vjp = jax.vjp(_s5_post, st, u, cb.astype(F32), dr, wg.astype(F32), bg)
        return vjp(do)

    u_spec = _row(proj, S5_WIDTH, O_U // S5_WIDTH)
    dstate, du1, dc_blk, g_d, g["w_glu"], g_bglu = _tile_call(
        post_bwd_fn, "s5_post_bwd", (nt,),
        [_row(sv["state"]), u_spec, _par(p["c_blk"]), _par(p["d_row"]), _par(p["w_glu"]), _par(p["b_glu"]), _row(do_s)],
        [_orow(t, 2 * S5_N), _orow(t, S5_WIDTH), _oacc((2 * S5_N, S5_WIDTH)), _oacc((1, S5_WIDTH)),
         _oacc((S5_WIDTH, S5_WIDTH)), _oacc((1, S5_WIDTH))])
    dbu, dlbar = landed("s5_scan_bwd", _scan_bwd(dstate, sv["state"], sv["lbar"], bsz, "s5_scan_bwd",
                                                 riding("s5_scan_bwd", g)[0]), 2)

    def bu_bwd_fn(u, bb, dbv, du_in):
        _, vjp = jax.vjp(bdot, u, bb)
        du, dbb = vjp(dbv)
        return du_in + du, dbb

    du, db_blk = _tile_call(bu_bwd_fn, "s5_bu_bwd", (nt,), [u_spec, _par(sv["b_blk"]), _row(dbu), _row(du1)],
                            [_orow(t, S5_WIDTH), _oacc((S5_WIDTH, 2 * S5_N))])
    dlr, dli = _from_scan_cols(dlbar)
    dbr, dbi = _from_scan_cols(db_blk)
    cts = (dlr.reshape(S5_N, 1), dli.reshape(S5_N, 1),
           _block_diag_in_t(dbr).reshape(S5_N, S5_GROUP_CH), _block_diag_in_t(dbi).reshape(S5_N, S5_GROUP_CH))
    g_lr, g_li, g_ls, g_br, g_bi = _s5_param_bwd_call(p, cts)
    dc_re, dc_im = _from_scan_cols(dc_blk.T)
    g["s5_lambda_re"] = g_lr.reshape(S5_GROUPS, S5_STATE)
    g["s5_lambda_im"] = g_li.reshape(S5_GROUPS, S5_STATE)
    g["s5_log_step"] = jnp.sum(g_ls.reshape(S5_GROUPS, S5_STATE), axis=1)
    g["s5_b_re"] = g_br.reshape(S5_GROUPS, S5_STATE, S5_GROUP_CH)
    g["s5_b_im"] = g_bi.reshape(S5_GROUPS, S5_STATE, S5_GROUP_CH)
    g["s5_c_re"] = _block_diag_out_t(dc_re.T)
    g["s5_c_im"] = -_block_diag_out_t(dc_im.T)
    g["s5_d"] = g_d.reshape(S5_GROUPS, S5_GROUP_CH)
    g["s5_b_glu"] = g_bglu.reshape(S5_WIDTH)

    r3 = lambda a: a.reshape(bsz, seq, a.shape[-1])
    dq_m, dk_m, dv_m = landed("attn_mla_bwd", _attn_bwd(
        r3(sv["q_m"]), r3(sv["k_m"]), r3(sv["v_m"]), 0, r3(sv["o_m"]), r3(do_m), sv["lse_m"], None, None,
        MLA_QK_DIM ** -0.5, "attn_mla_bwd", riding("attn_mla_bwd", g)[0]), 3)
    dq_f, dk_f, dv_f, dcq, dck = landed("attn_fox_bwd", _attn_bwd(
        r3(sv["q_f"]), r3(sv["k_f"]), r3(proj), O_FV // LANES, r3(sv["o_f"]), r3(do_f), sv["lse_f"], sv["cq"], sv["ck"],
        HEAD_V ** -0.5, "attn_fox_bwd", riding("attn_fox_bwd", g)[0]), 5)
    dcum = (dck + dcq.transpose(0, 1, 3, 2)).reshape(bsz * N_HEADS, seq)
    dz, dfb = _forget_bwd(dcum, sv["z"], sv["fb"], "forget_bwd")
    g["fox_f_bias"] = jnp.sum(dfb.reshape(bsz, N_HEADS), axis=0)
    dfl = jnp.pad(dz.reshape(bsz, N_HEADS, seq).transpose(0, 2, 1).reshape(t, N_HEADS), ((0, 0), (0, LANES - N_HEADS)))

    def pre_bwd_fn(pa, cosv, sinv, qlg, w_uq, kvlg, w_uk, w_uv, mqg, mkg, fqg, fkg, em, pm, rm, plm,
                   gq, gk, gv, gqf, gkf, gvf, gu, gfl, ggl):
        f = functools.partial(_mixer_pre, cos=cosv, sin=sinv, e_mat=em, p_mat=pm, r_mat=rm, pl_mat=plm)
        prim = (pa[:, O_CQ:O_CQ + MLA_Q_RANK], pa[:, O_CKV:O_CKV + MLA_KV_RANK], pa[:, O_KR:O_KR + LANES],
                pa[:, O_FQ:O_FQ + 256], pa[:, O_FK:O_FK + 256], qlg, w_uq.astype(F32), kvlg, w_uk.astype(F32),
                w_uv.astype(F32), mqg, mkg, fqg, fkg)
        _, vjp = jax.vjp(lambda *a: f(*a), *prim)
        dcq_, dckv, dkr, dfq, dfk, dqlg, dwuq, dkvlg, dwuk, dwuv, dmqg, dmkg, dfqg, dfkg = vjp((gq, gk, gv, gqf, gkf))
        zpad = jnp.zeros((pa.shape[0], PA - O_CQ - MLA_Q_RANK), F32)
        dproj = jnp.concatenate([dckv, dfq, dfk, gvf, gu, dkr, gfl, dcq_, zpad, ggl], axis=1)
        return dproj, dqlg, dwuq, dkvlg, dwuk, dwuv, dmqg, dmkg, dfqg, dfkg

    pre_params = [p["qlg"], p["w_uq"], p["kvlg"], p["w_uk"], p["w_uv"], p["mqg"], p["mkg"], p["fqg"], p["fkg"],
                  e_mat, p_mat, r_mat, pl_mat]
    cts_in = [dq_m.reshape(t, QKW), dk_m.reshape(t, QKW), dv_m.reshape(t, 2 * LANES), dq_f.reshape(t, QKW),
              dk_f.reshape(t, QKW), dv_f.reshape(t, 2 * LANES), du, dfl, dgl]
    (dproj, g_qlg, g_wuq, g_kvlg, g_wuk, g_wuv, g_mqg, g_mkg, g_fqg, g_fkg) = _tile_call(
        pre_bwd_fn, "mixer_pre_bwd", (nt,),
        [_row(proj, PA), _row(cos), _row(sin)] + [_par(a) for a in pre_params] + [_row(a) for a in cts_in],
        [_orow(t, PW, BF16), _oacc((1, MLA_Q_RANK)), _oacc((MLA_Q_RANK, QKW)), _oacc((1, MLA_KV_RANK)),
         _oacc((MLA_KV_RANK, QKW)), _oacc((MLA_KV_RANK, N_HEADS * HEAD_V)), _oacc((1, QKW)), _oacc((1, QKW)),
         _oacc((1, QKW)), _oacc((1, QKW))])
    g["q_lat_norm_g"] = g_qlg.reshape(MLA_Q_RANK)
    g["kv_lat_norm_g"] = g_kvlg.reshape(MLA_KV_RANK)
    heads = lambda a, w: jnp.sum(a.reshape(N_HEADS, HP)[:, :w], axis=0)
    g["mla_q_norm_g"], g["mla_k_norm_g"] = heads(g_mqg, MLA_QK_DIM), heads(g_mkg, MLA_QK_DIM)
    g["fox_q_norm_g"], g["fox_k_norm_g"] = heads(g_fqg, HEAD_V), heads(g_fkg, HEAD_V)
    g["w_uq"] = g_wuq.reshape(MLA_Q_RANK, N_HEADS, HP)[..., :MLA_QK_DIM].reshape(MLA_Q_RANK, N_HEADS * MLA_QK_DIM)
    g["w_ukv"] = jnp.concatenate([g_wuk.reshape(MLA_KV_RANK, N_HEADS, HP)[..., :MLA_NOPE_DIM],
                                  g_wuv.reshape(MLA_KV_RANK, N_HEADS, HEAD_V)], axis=-1).reshape(MLA_KV_RANK, QKW)

    g["w_in"] = _mm(sv["h"], dproj, "tn", "mm_in_dw", out_dtype=BF16)
    dh = landed("mm_in_dx", _mm(dproj, p["w_in"], "nt", "mm_in_dx", rider=riding("mm_in_dx", g)[0]), 1)[0]
    dx, g["attn_g"] = _tile_call(norm_bwd_fn, "norm_bwd", (nt,),
                                 [_row(sv["x"]), _par(p["attn_g"]), _row(dh), _row(d1)],
                                 [_orow(t, D_MODEL), _oacc((1, D_MODEL))])
    return dx, g, lands, carry_lands


def _row_tile(r, c, max_elems):
    if r * c <= max_elems:
        return r
    best = None
    for d in range(8, r, 8):
        if r % d == 0 and d * c <= max_elems:
            best = d
    assert best is not None, (r, c)
    return best


def _sum8(land, name):
    _, r, c = land.shape
    tr = _row_tile(r, N_DEV * c, 1 << 20)

    def fn(lv):
        acc = lv[0].astype(F32)
        for i in range(1, N_DEV):
            acc = acc + lv[i].astype(F32)
        return acc

    return _tile_call(fn, name, (r // tr,), [(land, (N_DEV, tr, c), lambda i: (0, i, 0))],
                      [((r, c), F32, (tr, c), lambda i: (i, 0), None)])[0]


def _adamw_layer(w, g, m, v, l, prev, name):
    _, r, c = w.shape
    tr = _row_tile(r, c, 1 << 19)
    lay = pl.BlockSpec((None, tr, c), lambda i: (l, i, 0))

    def body(w_ref, g_ref, m_ref, v_ref, *rest):
        g_out, d_out, m_out, v_out = rest[-4:]
        gv = g_ref[...]
        d_out[...], m_out[...], v_out[...] = _adamw(w_ref[...], gv, m_ref[...], v_ref[...])
        g_out[...] = gv

    in_specs = [lay, pl.BlockSpec((tr, c), lambda i: (i, 0)), lay, lay]
    args = [w, g, m, v]
    aliases = {}
    if prev is not None:
        in_specs += [pl.BlockSpec(memory_space=pl.ANY)] * 4
        args += list(prev)
        aliases = {4 + k: k for k in range(4)}
    return _pcall(body, name=name, grid=(r // tr,), in_specs=in_specs, out_specs=[lay] * 4,
                  out_shape=[jax.ShapeDtypeStruct(w.shape, F32)] * 4, aliases=aliases)(*args)


SMALL_NAMES = ("attn_norm_g", "q_lat_norm_g", "kv_lat_norm_g", "mla_q_norm_g", "mla_k_norm_g", "fox_q_norm_g",
               "fox_k_norm_g", "fox_f_bias", "s5_lambda_re", "s5_lambda_im", "s5_b_re", "s5_b_im", "s5_c_re", "s5_c_im",
               "s5_d", "s5_log_step", "s5_b_glu", "ffn_norm_g")
BIG_NAMES = ("w_in", "w_uq", "w_ukv", "s5_w_glu", "w_branch", "w_out", "w_up", "ffn_conv_w", "w_down")
ALL_NAMES = ("attn_norm_g", "w_in", "q_lat_norm_g", "w_uq", "kv_lat_norm_g", "w_ukv", "mla_q_norm_g", "mla_k_norm_g",
             "fox_q_norm_g", "fox_k_norm_g", "fox_f_bias", "s5_lambda_re", "s5_lambda_im", "s5_b_re", "s5_b_im",
             "s5_c_re", "s5_c_im", "s5_d", "s5_log_step", "s5_w_glu", "s5_b_glu", "w_branch", "w_out", "ffn_norm_g",
             "w_up", "ffn_conv_w", "w_down")


def _pack(arrs):
    tile = 8 * LANES
    parts = []
    for a in arrs:
        n = int(np.prod(a.shape))
        tot = -(-n // tile) * tile
        parts.append(jnp.pad(a.reshape(-1), (0, tot - n)).reshape(tot // LANES, LANES))
    rows = sum(a.shape[0] for a in parts)
    parts.append(jnp.zeros((-rows % 256, LANES), parts[0].dtype))
    return jnp.concatenate(parts, axis=0)


def _unpack(packed, like):
    tile = 8 * LANES
    out, pos = [], 0
    for a in like:
        n = int(np.prod(a.shape))
        rows = -(-n // tile) * 8
        out.append(packed[pos:pos + rows].reshape(-1)[:n].reshape(a.shape))
        pos += rows
    return out


def _rope_tables(positions, t):
    pos = positions.reshape(t, 1)
    inv = _inv_freq_row()

    def fn(pv, iv):
        ang = pv.astype(F32) * iv
        return jnp.cos(ang), jnp.sin(ang)

    return _tile_call(fn, "rope_tables", (t // TM,), [_row(pos), _par(inv)], [_orow(t, QKW), _orow(t, QKW)])


def _loss_call(x1, f, target):
    t = x1.shape[0]

    def fn(xv, fv, tv):
        e = xv + fv - tv
        per_tok = jnp.sum(e * e, axis=1, keepdims=True) * (1.0 / D_MODEL)
        return 0.5 * jnp.sum(per_tok, axis=0, keepdims=True), e * (1.0 / D_MODEL)

    return _tile_call(fn, "loss", (t // TM,), [_row(x1), _row(f), _row(target)], [_oacc((1, 1)), _orow(t, D_MODEL)])


def _local_shards(l, w):
    return {
        "w_in": _pad_in_cols(w["w_in"][l]).astype(BF16),
        "w_uq": w["w_uq"][l].reshape(-1, N_HEADS * MLA_QK_DIM).astype(BF16),
        "w_ukv": w["w_ukv"][l].reshape(-1, QKW).astype(BF16),
        "w_glu": w["s5_w_glu"][l].astype(BF16),
        "w_branch": w["w_branch"][l].astype(BF16),
        "w_out": w["w_out"][l].astype(BF16),
        "w_up": _pad_ff(w["w_up"][l], 1).astype(BF16),
        "conv": _pad_ff(w["ffn_conv_w"][l], 1),
        "w_down": _pad_ff(w["w_down"][l], 0).astype(BF16),
    }


GATHERED_SHAPES = {"w_in": (D_MODEL, PW), "w_uq": (MLA_Q_RANK, N_HEADS * MLA_QK_DIM), "w_ukv": (MLA_KV_RANK, QKW),
                   "w_glu": (S5_WIDTH, S5_WIDTH), "w_branch": (3, S5_WIDTH, D_MODEL), "w_out": (D_MODEL, D_MODEL),
                   "w_up": (D_MODEL, UP_W), "conv": (3, UP_W), "w_down": (FF_PW, D_MODEL)}
FFN_WEIGHTS = ("w_up", "conv", "w_down")
MIXER_WEIGHTS = ("w_in", "w_uq", "w_ukv", "w_glu", "w_out", "w_branch")
FWD_RIDERS_OWN = {"attn_mla_fwd": ("w_up", "conv"), "s5_scan_fwd": ("w_down",)}
FWD_RIDERS_NEXT = {"attn_fox_fwd": ("w_in", "w_uq", "w_ukv", "w_glu"), "mixer_pre": ("w_out", "w_branch")}
BWD_RIDERS = {"mm_up_dx": ("w_up:0",), "s5_scan_bwd": ("w_down",), "attn_mla_bwd": ("w_up:1",),
              "attn_fox_bwd": ("w_out", "w_branch", "w_glu", "conv"), "mm_in_dx": ("w_uq", "w_ukv")}
UPDATED_TRANSPOSED = ("w_up",)
WEIGHT_OF = {"w_in": "w_in", "w_uq": "w_uq", "w_ukv": "w_ukv", "w_glu": "s5_w_glu", "w_branch": "w_branch",
             "w_out": "w_out", "w_up": "w_up", "conv": "ffn_conv_w", "w_down": "w_down"}


def _unshard_layout(nm, g2):
    if nm == "w_in":
        return _unpad_in_cols(g2)
    if nm == "w_uq":
        return g2.reshape(-1, N_HEADS, MLA_QK_DIM)
    if nm == "w_ukv":
        return g2.reshape(-1, N_HEADS, MLA_NOPE_DIM + HEAD_V)
    if nm in ("w_up", "conv"):
        return _unpad_ff(g2, 1)
    if nm == "w_down":
        return _unpad_ff(g2, 0)
    return g2


def kernel(x, positions, attn_norm_g, w_in, q_lat_norm_g, w_uq, kv_lat_norm_g, w_ukv, mla_q_norm_g, mla_k_norm_g, fox_q_norm_g, fox_k_norm_g, fox_f_bias, s5_lambda_re, s5_lambda_im, s5_b_re, s5_b_im, s5_c_re, s5_c_im, s5_d, s5_log_step, s5_w_glu, s5_b_glu, w_branch, w_out, ffn_norm_g, w_up, ffn_conv_w, w_down, loss_target, m_attn_norm_g, m_w_in, m_q_lat_norm_g, m_w_uq, m_kv_lat_norm_g, m_w_ukv, m_mla_q_norm_g, m_mla_k_norm_g, m_fox_q_norm_g, m_fox_k_norm_g, m_fox_f_bias, m_s5_lambda_re, m_s5_lambda_im, m_s5_b_re, m_s5_b_im, m_s5_c_re, m_s5_c_im, m_s5_d, m_s5_log_step, m_s5_w_glu, m_s5_b_glu, m_w_branch, m_w_out, m_ffn_norm_g, m_w_up, m_ffn_conv_w, m_w_down, v_attn_norm_g, v_w_in, v_q_lat_norm_g, v_w_uq, v_kv_lat_norm_g, v_w_ukv, v_mla_q_norm_g, v_mla_k_norm_g, v_fox_q_norm_g, v_fox_k_norm_g, v_fox_f_bias, v_s5_lambda_re, v_s5_lambda_im, v_s5_b_re, v_s5_b_im, v_s5_c_re, v_s5_c_im, v_s5_d, v_s5_log_step, v_s5_w_glu, v_s5_b_glu, v_w_branch, v_w_out, v_ffn_norm_g, v_w_up, v_ffn_conv_w, v_w_down):
    args = locals()
    w = {n: args[n] for n in ALL_NAMES}
    m = {n: args["m_" + n] for n in ALL_NAMES}
    v = {n: args["v_" + n] for n in ALL_NAMES}
    depth = w_in.shape[0]
    bsz, seq, _ = x.shape
    t = bsz * seq
    consts = _constants()
    tabs = _rope_tables(positions, t)
    small = {n: w[n] for n in SMALL_NAMES}

    all_names = list(GATHERED_SHAPES)
    all_shards = [_local_shards(l, w) for l in range(depth)]

    def gather_xfer(l, names):
        sub = {n: all_shards[l][n] for n in names}
        order, plan = _gather_plan(sub)
        return _Gather2([sub[n] for n in order], [(GATHERED_SHAPES[n], sub[n].dtype) for n in order], plan)

    def scatter_xfer(g, names):
        shapes = {n: _local_shards_shape(n) for n in names}
        return _Xfer([g[n] for n in names], [((N_DEV,) + shapes[n], g[n].dtype) for n in names],
                     _scatter_plan(list(names), shapes))

    xs = x.reshape(t, D_MODEL)
    f_prev = None
    saved, params = [], []
    gathered = dict(zip(MIXER_WEIGHTS, _exchange("gather_weights", gather_xfer(0, MIXER_WEIGHTS))))
    for l in range(depth):
        p = _prep_layer(l, gathered, small, consts)
        riders = {host: (gather_xfer(l, names), names) for host, names in FWD_RIDERS_OWN.items()}
        if l + 1 < depth:
            riders.update({host: (gather_xfer(l + 1, names), names) for host, names in FWD_RIDERS_NEXT.items()})
        xs, f_prev, sv, gathered = _layer_fwd(xs, f_prev, p, tabs, consts, bsz, riders)
        saved.append(sv)
        params.append(p)

    loss_part, dy = _loss_call(xs, f_prev, loss_target.reshape(t, D_MODEL))
    loss = lax.psum(loss_part[0, 0], ("x", "y", "c"))

    small_grads = {n: [None] * depth for n in SMALL_NAMES}
    big = {n: None for n in BIG_NAMES}

    def finish(l, lands):
        for n in all_names:
            wn = WEIGHT_OF[n]
            if n == "w_up":
                gsum = jnp.concatenate([_sum8(lands["w_up:%d" % h], "sum_w_up") for h in range(2)], axis=1)
            else:
                ld = lands[n]
                gsum = _sum8(ld.reshape(N_DEV, -1, ld.shape[-1]), "sum_" + n).reshape(_local_shards_shape(n))
            gsum = _unshard_layout(n, gsum)
            if n in UPDATED_TRANSPOSED:
                gsum = gsum.T
            c = gsum.shape[-1]
            three = lambda a: (jnp.swapaxes(a, 1, 2) if n in UPDATED_TRANSPOSED else a).reshape(depth, -1, c)
            big[wn] = _adamw_layer(three(w[wn]), gsum.reshape(-1, c), three(m[wn]), three(v[wn]), l, big[wn],
                                   "adamw_" + n)

    def scatter(grads):
        return scatter_xfer(grads, list(grads))

    dx = dy
    carry, lands_above = {}, None
    for l in reversed(range(depth)):
        dx, g, lands, carry_lands = _layer_bwd(dx, params[l], saved[l], tabs, consts, bsz, scatter, carry, l == 0)
        if lands_above is not None:
            finish(l + 1, {**lands_above, **carry_lands})
        carry, lands_above = {"w_in": g["w_in"]}, lands
        small_grads["attn_norm_g"][l] = g["attn_g"].reshape(D_MODEL)
        small_grads["ffn_norm_g"][l] = g["ffn_g"].reshape(D_MODEL)
        for n in SMALL_NAMES:
            if n not in ("attn_norm_g", "ffn_norm_g"):
                small_grads[n][l] = g[n]
    finish(0, lands_above)

    sg = [jnp.stack(small_grads[n]).reshape(w[n].shape) for n in SMALL_NAMES]
    packed = _pack(sg)
    land = _exchange("gather_small_grads", _Xfer([packed], [((N_DEV,) + packed.shape, F32)],
                                                 _scatter_plan(["small"], {})))[0]
    gs = _sum8(land, "sum_small")
    one = lambda names_of: _pack([names_of[n] for n in SMALL_NAMES])[None]
    small_res = _adamw_layer(one(w), gs, one(m), one(v), 0, None, "adamw_small")
    like = [w[n] for n in SMALL_NAMES]
    small_out = [dict(zip(SMALL_NAMES, _unpack(a[0], like))) for a in small_res]

    def out_of(kind, n):
        if n in SMALL_NAMES:
            return small_out[kind][n]
        if n in UPDATED_TRANSPOSED:
            return jnp.swapaxes(big[n][kind], 1, 2)
        return big[n][kind].reshape(w[n].shape)

    outs = [loss, dx.reshape(bsz, seq, D_MODEL)]
    for kind in range(4):
        outs += [out_of(kind, n) for n in ALL_NAMES]
    return tuple(outs)


def _local_shards_shape(nm):
    return {"w_in": (D_MODEL // N_DEV, PW), "w_uq": (MLA_Q_RANK // N_DEV, N_HEADS * MLA_QK_DIM),
            "w_ukv": (MLA_KV_RANK // N_DEV, QKW), "w_glu": (S5_WIDTH // N_DEV, S5_WIDTH),
            "w_branch": (3, S5_WIDTH, LANES), "w_out": (D_MODEL // N_DEV, D_MODEL), "w_up": (D_MODEL, 2 * FF_BLK),
            "conv": (3, 2 * FF_BLK), "w_down": (FF_BLK, D_MODEL), "w_up:0": (D_MODEL, FF_BLK),
            "w_up:1": (D_MODEL, FF_BLK)}[nm]
```

```python
import functools
import math

import numpy as np
import jax
import jax.numpy as jnp
from jax import lax
from jax.experimental import pallas as pl
from jax.experimental.pallas import tpu as pltpu

F32, BF16 = jnp.float32, jnp.bfloat16

D_MODEL = 1024
N_DEV = 8
MLA_Q_RANK, MLA_KV_RANK, MLA_ROPE_DIM, MLA_NOPE_DIM, MLA_QK_DIM = 384, 256, 32, 64, 96
N_HEADS, HEAD_V = 4, 64
S5_GROUPS, S5_GROUP_CH, S5_STATE, S5_WIDTH = 16, 16, 64, 256
S5_N = S5_GROUPS * S5_STATE
D_FF = 2816
D_IN = 4772
ROPE_THETA = 10000.0
NORM_EPS = 1e-6
NEG_INF = -1e30
ADAM_LR, ADAM_B1, ADAM_B2, ADAM_EPS, ADAM_WD, ADAM_STEP = 0.001, 0.9, 0.999, 1e-08, 0.01, 10

VMEM_LIMIT_BYTES = 56 * 1024 * 1024
LANES = 128
TM = 256
TQ = 512
QF = 128
QC = 512
HP = 128
QKW = N_HEADS * HP

PW = 5120
PA = 2048
IN_SEGS = ((0, 384, 256),
           (256, 672, 256),
           (512, 928, 256),
           (768, 1184, 256),
           (1024, 1444, 256),
           (1280, 640, 32),
           (1408, 1440, 4),
           (1536, 0, 384),
           (2048, 1700, 3072))
O_CKV, O_FQ, O_FK, O_FV, O_U, O_KR, O_FL, O_CQ = 0, 256, 512, 768, 1024, 1280, 1408, 1536

FF_BLK = 384
FF_HALF = D_FF // 8
FF_PW = 8 * FF_BLK
UP_W = 2 * FF_PW

NN = (((1,), (0,)), ((), ()))
NT = (((1,), (1,)), ((), ()))
TN = (((0,), (0,)), ((), ()))


def _pcall(body, *, name, grid, in_specs, out_specs, out_shape, scratch_shapes=(), rider=None, aliases=None):
    params = pltpu.CompilerParams(dimension_semantics=("arbitrary",) * len(grid), vmem_limit_bytes=VMEM_LIMIT_BYTES)
    if rider is None:
        return pl.pallas_call(body, name=name, grid=grid, in_specs=in_specs, out_specs=out_specs, out_shape=out_shape,
                              scratch_shapes=list(scratch_shapes), input_output_aliases=aliases or {},
                              compiler_params=params)
    assert not aliases
    in_specs, out_specs, out_shape = list(in_specs), list(out_specs), list(out_shape)
    n_in, n_out, n_scr, nx = len(in_specs), len(out_specs), len(scratch_shapes), rider.n
    any_spec = pl.BlockSpec(memory_space=pl.ANY)

    def at_step(last):
        conds = [pl.program_id(a) == (g - 1 if last else 0) for a, g in enumerate(grid)]
        return functools.reduce(jnp.logical_and, conds)

    def body_with_rider(*refs):
        ins, xsrc = refs[:n_in], refs[n_in:n_in + nx]
        outs, xout = refs[n_in + nx:n_in + nx + n_out], refs[n_in + nx + n_out:n_in + 2 * nx + n_out]
        scr = refs[n_in + 2 * nx + n_out:n_in + 2 * nx + n_out + n_scr]
        sems = refs[n_in + 2 * nx + n_out + n_scr:]

        @pl.when(at_step(False))
        def _():
            rider.start(rider.copies(xsrc, xout, *sems))

        body(*ins, *outs, *scr)

        @pl.when(at_step(True))
        def _():
            rider.wait(rider.copies(xsrc, xout, *sems))

    call = pl.pallas_call(body_with_rider, name=name, grid=grid, in_specs=in_specs + [any_spec] * nx,
                          out_specs=out_specs + [any_spec] * nx,
                          out_shape=out_shape + [jax.ShapeDtypeStruct(s, d) for s, d in rider.out_shapes],
                          scratch_shapes=list(scratch_shapes) + rider.sems(), compiler_params=params)
    return lambda *args: call(*args, *rider.srcs)


def _tile_call(fn, name, grid, ins, outs, rider=None):
    n_in = len(ins)

    def body(*refs):
        res = fn(*[r[...] for r in refs[:n_in]])
        if not isinstance(res, (tuple, list)):
            res = (res,)
        assert len(res) == len(outs), (name, len(res), len(outs))
        for r, v, o in zip(refs[n_in:], res, outs):
            v = v.astype(r.dtype)
            if o[4] is None:
                r[...] = v
            else:
                first = functools.reduce(jnp.logical_and, [pl.program_id(a) == 0 for a in o[4]])

                @pl.when(first)
                def _():
                    r[...] = v

                @pl.when(jnp.logical_not(first))
                def _():
                    r[...] += v

    res = _pcall(body, name=name, grid=grid,
                 in_specs=[pl.BlockSpec(b, m) for _, b, m in ins],
                 out_specs=[pl.BlockSpec(o[2], o[3]) for o in outs],
                 out_shape=[jax.ShapeDtypeStruct(o[0], o[1]) for o in outs], rider=rider)(*[a for a, _, _ in ins])
    return res


def _row(a, width=None, col_block=0, tm=TM):
    w = a.shape[1] if width is None else width
    return (a, (tm, w), lambda i, c=col_block: (i, c))


def _par(a):
    nd = a.ndim
    return (a, a.shape, lambda i, nd=nd: (0,) * nd)


def _orow(t, w, dtype=F32, tm=TM):
    return ((t, w), dtype, (tm, w), lambda i: (i, 0), None)


def _oacc(shape):
    nd = len(shape)
    return (tuple(shape), F32, tuple(shape), lambda i, nd=nd: (0,) * nd, (0,))


def _pick(n, target):
    best = None
    for d in range(LANES, min(n, target) + 1, LANES):
        if n % d == 0:
            best = d
    return n if best is None else best


MM_VMEM_BUDGET = 44 * 1024 * 1024


def _mm_tiles(m, n, k, sa, sb, so, tm, tn, tk):
    for cm, cn in ((1024, 2048), (1024, 1024), (512, 2048), (512, 1024), (512, 512), (256, 1024), (256, 512)):
        pm, pn = _pick(m, cm), _pick(n, cn)
        if 2 * (pm * k * sa + k * pn * sb + pm * pn * so) <= MM_VMEM_BUDGET:
            return pm, pn, k
    return _pick(m, tm), _pick(n, tn), _pick(k, tk)


def _mm(a, b, mode, name, out_dtype=F32, tm=1024, tn=2048, tk=1024, rider=None):
    if mode == "nn":
        (m, k), (_, n) = a.shape, b.shape
    elif mode == "nt":
        (m, k), (n, _) = a.shape, b.shape
    else:
        (k, m), (_, n) = a.shape, b.shape
    tm, tn, tk = _mm_tiles(m, n, k, a.dtype.itemsize, b.dtype.itemsize, jnp.dtype(out_dtype).itemsize, tm, tn, tk)
    nk = k // tk
    dims = {"nn": NN, "nt": NT, "tn": TN}[mode]
    a_spec = (pl.BlockSpec((tk, tm), lambda i, j, l: (l, i)) if mode == "tn"
              else pl.BlockSpec((tm, tk), lambda i, j, l: (i, l)))
    b_spec = (pl.BlockSpec((tn, tk), lambda i, j, l: (j, l)) if mode == "nt"
              else pl.BlockSpec((tk, tn), lambda i, j, l: (l, j)))

    def body(a_ref, b_ref, o_ref, acc_ref):
        part = lax.dot_general(a_ref[...].astype(BF16), b_ref[...].astype(BF16), dims, preferred_element_type=F32)
        if nk == 1:
            o_ref[...] = part.astype(o_ref.dtype)
        else:
            l = pl.program_id(2)

            @pl.when(l == 0)
            def _():
                acc_ref[...] = part

            @pl.when(l > 0)
            def _():
                acc_ref[...] += part

            @pl.when(l == nk - 1)
            def _():
                o_ref[...] = acc_ref[...].astype(o_ref.dtype)

    res = _pcall(body, name=name, grid=(m // tm, n // tn, nk), in_specs=[a_spec, b_spec],
                 out_specs=[pl.BlockSpec((tm, tn), lambda i, j, l: (i, j))],
                 out_shape=[jax.ShapeDtypeStruct((m, n), out_dtype)],
                 scratch_shapes=[pltpu.VMEM((tm, tn) if nk > 1 else (8, LANES), F32)], rider=rider)(a, b)
    return res[0] if rider is None else res


def _dot(x, w, dims):
    return lax.dot_general(x.astype(BF16), w.astype(BF16), dims, preferred_element_type=F32)


@jax.custom_vjp
def bdot(x, w):
    return _dot(x, w, NN)


def _bdot_fwd(x, w):
    return _dot(x, w, NN), (x, w)


def _bdot_bwd(res, g):
    x, w = res
    return _dot(g, w, NT), _dot(x, g, TN)


bdot.defvjp(_bdot_fwd, _bdot_bwd)


def _split3(x):
    x1 = x.astype(BF16)
    r1 = x - x1.astype(F32)
    x2 = r1.astype(BF16)
    x3 = (r1 - x2.astype(F32)).astype(BF16)
    return x1, x2, x3


def _cdot(x, m, dims):
    return sum(lax.dot_general(p, m, dims, preferred_element_type=F32) for p in _split3(x))


def _block_dot(x, m, transposed):
    k_in = m.shape[1] if transposed else m.shape[0]
    dims = NT if transposed else NN
    return jnp.concatenate([_cdot(x[:, b * k_in:(b + 1) * k_in], m, dims) for b in range(x.shape[1] // k_in)], axis=1)


@jax.custom_vjp
def cdot(x, m):
    return _block_dot(x, m, False)


def _cdot_fwd(x, m):
    return _block_dot(x, m, False), m


def _cdot_bwd(m, g):
    return _block_dot(g, m, True), jnp.zeros_like(m)


cdot.defvjp(_cdot_fwd, _cdot_bwd)


@jax.custom_vjp
def tile_heads(y):
    return jnp.concatenate([y] * N_HEADS, axis=1)


def _tile_heads_fwd(y):
    return jnp.concatenate([y] * N_HEADS, axis=1), None


def _tile_heads_bwd(_, g):
    return (sum(g[:, h * HP:(h + 1) * HP] for h in range(N_HEADS)),)


tile_heads.defvjp(_tile_heads_fwd, _tile_heads_bwd)


def _shift_rows_impl(x, s, reverse):
    n = x.shape[0]
    idx = lax.broadcasted_iota(jnp.int32, x.shape, 0)
    if reverse:
        return jnp.where(idx < n - s, pltpu.roll(x, n - s, 0), 0.0)
    return jnp.where(idx >= s, pltpu.roll(x, s, 0), 0.0)


@functools.partial(jax.custom_vjp, nondiff_argnums=(1,))
def shift_rows(x, s):
    return _shift_rows_impl(x, s, False)


def _shift_rows_fwd(x, s):
    return _shift_rows_impl(x, s, False), None


def _shift_rows_bwd(s, _, g):
    return (_shift_rows_impl(g, s, True),)


shift_rows.defvjp(_shift_rows_fwd, _shift_rows_bwd)


def _rms(x, g):
    return x * lax.rsqrt(jnp.mean(x * x, axis=-1, keepdims=True) + NORM_EPS) * g


def _head_rms(x, e_mat, inv_n, g):
    ms = cdot(x * x, e_mat) * inv_n
    return x * lax.rsqrt(ms + NORM_EPS) * g


def _rope(y, p_mat, cos, sin):
    return y * cos + cdot(y, p_mat) * sin


def _constants():
    e = np.ones((HP, HP), np.float32)
    p = np.zeros((HP, HP), np.float32)
    r = np.zeros((LANES, HP), np.float32)
    pl64 = np.zeros((2 * HEAD_V, 2 * HP), np.float32)
    half = MLA_ROPE_DIM // 2
    for i in range(half):
        x1, x2 = MLA_NOPE_DIM + i, MLA_NOPE_DIM + half + i
        p[x2, x1] = -1.0
        p[x1, x2] = 1.0
    for i in range(MLA_ROPE_DIM):
        r[i, MLA_NOPE_DIM + i] = 1.0
    for h in range(2):
        for i in range(HEAD_V):
            pl64[h * HEAD_V + i, h * HP + i] = 1.0
    return tuple(jnp.asarray(a, BF16) for a in (e, p, r, pl64))


def _inv_freq_row():
    inv = ROPE_THETA ** (-jnp.arange(0, MLA_ROPE_DIM, 2, dtype=F32) / MLA_ROPE_DIM)
    head = jnp.concatenate([jnp.zeros((MLA_NOPE_DIM,), F32), inv, inv, jnp.zeros((HP - MLA_QK_DIM,), F32)])
    return jnp.tile(head, N_HEADS).reshape(1, QKW)


def _mixer_pre(c_q, c_kv, kr, fq, fk, qlg, w_uq, kvlg, w_uk, w_uv, mqg, mkg, fqg, fkg, cos, sin, e_mat, p_mat, r_mat, pl_mat):
    q = bdot(_rms(c_q, qlg), w_uq)
    ckvn = _rms(c_kv, kvlg)
    k = bdot(ckvn, w_uk) + tile_heads(cdot(kr, r_mat))
    v = bdot(ckvn, w_uv)
    q = _rope(_head_rms(q, e_mat, 1.0 / MLA_QK_DIM, mqg), p_mat, cos, sin)
    k = _rope(_head_rms(k, e_mat, 1.0 / MLA_QK_DIM, mkg), p_mat, cos, sin)
    qf = _head_rms(cdot(fq, pl_mat), e_mat, 1.0 / HEAD_V, fqg)
    kf = _head_rms(cdot(fk, pl_mat), e_mat, 1.0 / HEAD_V, fkg)
    return q, k, v, qf, kf


def _s5_post(state, u, c_blk, d_row, w_glu, b_glu):
    y = bdot(state, c_blk) + d_row * u
    y = jax.nn.gelu(y)
    return y * jax.nn.sigmoid(bdot(y, w_glu) + b_glu)


def _merge(o_a, o_b, o_c, g_a, g_b, g_c, w_a, w_b, w_c):
    return (jax.nn.sigmoid(g_a) * bdot(o_a, w_a) + jax.nn.sigmoid(g_b) * bdot(o_b, w_b)
            + jax.nn.sigmoid(g_c) * bdot(o_c, w_c))


def _ffn_mid(gate, val, wg0, wg1, wg2, wv0, wv1, wv2):
    cg = wg0 * shift_rows(gate, 2) + wg1 * shift_rows(gate, 1) + wg2 * gate
    cv = wv0 * shift_rows(val, 2) + wv1 * shift_rows(val, 1) + wv2 * val
    return jax.nn.silu(cg) * cv


def _s5_params(lam_re, lam_im, log_step, b_re, b_im):
    step = jnp.exp(log_step)
    zr, zi = lam_re * step, lam_im * step
    mag = jnp.exp(zr)
    lr, li = mag * jnp.cos(zi), mag * jnp.sin(zi)
    nr, ni = lr - 1.0, li
    den = lam_re * lam_re + lam_im * lam_im
    cr = (nr * lam_re + ni * lam_im) / den
    ci = (ni * lam_re - nr * lam_im) / den
    return lr, li, cr * b_re - ci * b_im, cr * b_im + ci * b_re


def _adamw(w, g, m, v):
    m = ADAM_B1 * m + (1.0 - ADAM_B1) * g
    v = ADAM_B2 * v + (1.0 - ADAM_B2) * (g * g)
    m_hat = m / (1.0 - ADAM_B1 ** ADAM_STEP)
    v_hat = v / (1.0 - ADAM_B2 ** ADAM_STEP)
    delta = -ADAM_LR * (m_hat / (jnp.sqrt(v_hat) + ADAM_EPS) + ADAM_WD * w)
    return delta, m, v


SCAN_W = 256


def _scan_seq(xr, xi, ar, ai, reverse):
    n = xr.shape[0]
    s = 1
    while s < n:
        sr, si = _shift_rows_impl(xr, s, reverse), _shift_rows_impl(xi, s, reverse)
        xr, xi = xr + ar * sr - ai * si, xi + ar * si + ai * sr
        ar, ai = ar * ar - ai * ai, 2.0 * ar * ai
        s *= 2
    return xr, xi


def _scan_fwd(bu, lbar, bsz, name, rider=None):
    t = bu.shape[0]
    seq = t // bsz
    nb = bu.shape[1] // (2 * SCAN_W)

    def fn(b, a):
        xr, xi = _scan_seq(b[:, :SCAN_W], b[:, SCAN_W:], a[:, :SCAN_W], a[:, SCAN_W:], False)
        return jnp.concatenate([xr, xi], axis=1)

    blk = (seq, 2 * SCAN_W)
    return _tile_call(fn, name, (bsz, nb),
                      [(bu, blk, lambda i, j: (i, j)), (lbar, (1, 2 * SCAN_W), lambda i, j: (0, j))],
                      [((t, bu.shape[1]), F32, blk, lambda i, j: (i, j), None)], rider=rider)


def _scan_bwd(dstate, state, lbar, bsz, name, rider=None):
    t = state.shape[0]
    seq = t // bsz
    nb = state.shape[1] // (2 * SCAN_W)

    def fn(g, x, a):
        ar, ai = a[:, :SCAN_W], a[:, SCAN_W:]
        gr, gi = _scan_seq(g[:, :SCAN_W], g[:, SCAN_W:], ar, -ai, True)
        pr, pi = _shift_rows_impl(x[:, :SCAN_W], 1, False), _shift_rows_impl(x[:, SCAN_W:], 1, False)
        dar = jnp.sum(gr * pr + gi * pi, axis=0, keepdims=True)
        dai = jnp.sum(gi * pr - gr * pi, axis=0, keepdims=True)
        return jnp.concatenate([gr, gi], axis=1), jnp.concatenate([dar, dai], axis=1)

    blk = (seq, 2 * SCAN_W)
    return _tile_call(fn, name, (nb, bsz),
                      [(dstate, blk, lambda j, i: (i, j)), (state, blk, lambda j, i: (i, j)),
                       (lbar, (1, 2 * SCAN_W), lambda j, i: (0, j))],
                      [((t, state.shape[1]), F32, blk, lambda j, i: (i, j), None),
                       ((1, state.shape[1]), F32, (1, 2 * SCAN_W), lambda j, i: (0, j), (1,))], rider=rider)


def _shift_lanes(x, s, reverse):
    n = x.shape[1]
    idx = lax.broadcasted_iota(jnp.int32, x.shape, 1)
    if reverse:
        return jnp.where(idx < n - s, pltpu.roll(x, n - s, 1), 0.0)
    return jnp.where(idx >= s, pltpu.roll(x, s, 1), 0.0)


def _cumsum_lanes(x, reverse):
    s = 1
    while s < x.shape[1]:
        x = x + _shift_lanes(x, s, reverse)
        s *= 2
    return x


def _forget_fwd(z, bias, name):
    def fn(zv, bv):
        x = zv + bv
        logf = jnp.minimum(x, 0.0) - jnp.log(1.0 + jnp.exp(-jnp.abs(x)))
        return _cumsum_lanes(logf, False)

    return _tile_call(fn, name, (1,), [_par(z), _par(bias)],
                      [(z.shape, F32, z.shape, lambda i: (0, 0), None)])[0]


def _forget_bwd(dc, z, bias, name):
    def fn(dcv, zv, bv):
        x = zv + bv
        dlogf = _cumsum_lanes(dcv, True)
        dz = dlogf * jax.nn.sigmoid(-x)
        return dz, jnp.sum(dz, axis=1, keepdims=True)

    rows = z.shape[0]
    return _tile_call(fn, name, (1,), [_par(dc), _par(z), _par(bias)],
                      [(z.shape, F32, z.shape, lambda i: (0, 0), None),
                       ((rows, 1), F32, (rows, 1), lambda i: (0, 0), None)])


def _col(v2, x):
    idx = lax.broadcasted_iota(jnp.int32, v2.shape, 1)
    return jnp.sum(jnp.where(idx == x, v2, 0.0), axis=1, keepdims=True)


def _two_cols(c0, c1):
    idx = lax.broadcasted_iota(jnp.int32, (c0.shape[0], 2), 1)
    return jnp.where(idx == 0, c0, c1)


def _rows2(r0, r1):
    idx = lax.broadcasted_iota(jnp.int32, (2, r0.shape[1]), 0)
    return jnp.where(idx == 0, r0, r1)


def _attn_specs(seq, v_blk0, has_bias):
    qblk = pl.BlockSpec((None, TQ, 2 * HP), lambda b, p, i: (b, i, p))
    kblk = pl.BlockSpec((None, seq, 2 * HP), lambda b, p, i: (b, 0, p))
    vblk = pl.BlockSpec((None, seq, LANES), lambda b, p, i: (b, 0, v_blk0 + p))
    oblk = pl.BlockSpec((None, TQ, LANES), lambda b, p, i: (b, i, p))
    rowblk = pl.BlockSpec((None, None, 2, TQ), lambda b, p, i: (b, p, 0, i))
    colblk = pl.BlockSpec((None, None, seq, 2), lambda b, p, i: (b, p, 0, 0))
    return qblk, kblk, vblk, oblk, rowblk, colblk


def _attn_fwd(q, k, v, v_blk0, c_cols, c_rows, scale, name, rider=None):
    bsz, seq, _ = q.shape
    nq = seq // TQ
    has_bias = c_cols is not None
    n_in = 5 if has_bias else 3
    nc = TQ // QF

    def body(*refs):
        q_ref, k_ref, v_ref = refs[:3]
        cq_ref, ck_ref = refs[3:5] if has_bias else (None, None)
        o_ref, lse_ref = refs[n_in:]
        qi = pl.program_id(2)
        qb = [[q_ref[c * QF:(c + 1) * QF, x * HP:(x + 1) * HP].astype(BF16) for c in range(nc)] for x in range(2)]
        key_in = lax.broadcasted_iota(jnp.int32, (TQ, QF), 0)
        qry_in = lax.broadcasted_iota(jnp.int32, (TQ, QF), 1)

        def block(j, carry, masked):
            ks = pl.multiple_of(j * TQ, TQ)
            vt = v_ref[pl.ds(ks, TQ), :].T.astype(BF16)
            new = []
            for x in range(2):
                kx = k_ref[pl.ds(ks, TQ), x * HP:(x + 1) * HP].astype(BF16)
                ckx = _col(ck_ref[pl.ds(ks, TQ), :], x) if has_bias else None
                for c in range(nc):
                    m, l, acc = carry[x * nc + c]
                    s = lax.dot_general(kx, qb[x][c], NT, preferred_element_type=F32) * scale
                    if has_bias:
                        s = s + (cq_ref[x:x + 1, c * QF:(c + 1) * QF] - ckx)
                    if masked:
                        s = jnp.where(qry_in + c * QF >= key_in, s, NEG_INF)
                    m_new = jnp.maximum(m, jnp.max(s, axis=0, keepdims=True))
                    alpha = jnp.exp(m - m_new)
                    p = jnp.exp(s - m_new)
                    l = alpha * l + jnp.sum(p, axis=0, keepdims=True)
                    acc = alpha * acc + lax.dot_general(vt, p.astype(BF16), NN, preferred_element_type=F32)
                    new.append((m_new, l, acc))
            return tuple(new)

        init = tuple((jnp.full((1, QF), NEG_INF, F32), jnp.zeros((1, QF), F32), jnp.zeros((LANES, QF), F32))
                     for _ in range(2 * nc))
        carry = lax.fori_loop(0, qi, lambda j, cr: block(j, cr, False), init)
        carry = block(qi, carry, True)
        lane = lax.broadcasted_iota(jnp.int32, (QF, LANES), 1)
        lse_rows = []
        for x in range(2):
            lse_rows.append(jnp.concatenate([carry[x * nc + c][0] + jnp.log(carry[x * nc + c][1]) for c in range(nc)],
                                            axis=1))
        for c in range(nc):
            (_, l0, a0), (_, l1, a1) = carry[c], carry[nc + c]
            o_ref[c * QF:(c + 1) * QF, :] = jnp.where(lane < HEAD_V, (a0 / l0).T, (a1 / l1).T)
        lse_ref[...] = _rows2(lse_rows[0], lse_rows[1])

    qblk, kblk, vblk, oblk, rowblk, colblk = _attn_specs(seq, v_blk0, has_bias)
    in_specs, args = [qblk, kblk, vblk], [q, k, v]
    if has_bias:
        in_specs += [rowblk, colblk]
        args += [c_rows, c_cols]
    return _pcall(body, name=name, grid=(bsz, 2, nq), in_specs=in_specs, out_specs=[oblk, rowblk],
                  out_shape=[jax.ShapeDtypeStruct((bsz, seq, 2 * LANES), F32),
                             jax.ShapeDtypeStruct((bsz, 2, 2, seq), F32)], rider=rider)(*args)


def _attn_bwd(q, k, v, v_blk0, o, do, lse, c_cols, c_rows, scale, name, rider=None):
    bsz, seq, _ = q.shape
    nq = seq // TQ
    has_bias = c_cols is not None
    n_in = 8 if has_bias else 6
    nc = TQ // QC

    def body(*refs):
        q_ref, k_ref, v_ref, o_ref, do_ref, lse_ref = refs[:6]
        cq_ref, ck_ref = refs[6:8] if has_bias else (None, None)
        dq_ref, dk_ref, dv_ref = refs[n_in:n_in + 3]
        dck_ref, dcq_ref = refs[n_in + 3:n_in + 5] if has_bias else (None, None)
        qi = pl.program_id(2)

        @pl.when(qi == 0)
        def _():
            dk_ref[...] = jnp.zeros_like(dk_ref)
            dv_ref[...] = jnp.zeros_like(dv_ref)
            if has_bias:
                dck_ref[...] = jnp.zeros_like(dck_ref)

        lane = lax.broadcasted_iota(jnp.int32, (QC, LANES), 1)
        ones8 = jnp.ones((8, LANES), BF16)
        qb, dob, delta, lse_r = [], [], [], []
        for x in range(2):
            hm = jnp.logical_and(lane >= x * HEAD_V, lane < (x + 1) * HEAD_V)
            qb.append([]), dob.append([]), delta.append([]), lse_r.append([])
            for c in range(nc):
                rows = slice(c * QC, (c + 1) * QC)
                do_c = jnp.where(hm, do_ref[rows, :], 0.0)
                prod = do_c * o_ref[rows, :]
                hi = prod.astype(BF16)
                lo = (prod - hi.astype(F32)).astype(BF16)
                d8 = (lax.dot_general(ones8, hi, NT, preferred_element_type=F32)
                      + lax.dot_general(ones8, lo, NT, preferred_element_type=F32))
                delta[x].append(d8[0:1, :])
                dob[x].append(do_c.astype(BF16))
                qb[x].append(q_ref[rows, x * HP:(x + 1) * HP].astype(BF16))
                lse_r[x].append(lse_ref[x:x + 1, rows])
        key_in = lax.broadcasted_iota(jnp.int32, (TQ, QC), 0)
        qry_in = lax.broadcasted_iota(jnp.int32, (TQ, QC), 1)

        def block(j, carry, masked):
            ks = pl.multiple_of(j * TQ, TQ)
            vb = v_ref[pl.ds(ks, TQ), :].astype(BF16)
            new, key_sums = [], []
            for x in range(2):
                k32 = k_ref[pl.ds(ks, TQ), x * HP:(x + 1) * HP]
                kx, kt = k32.astype(BF16), k32.T.astype(BF16)
                ckx = _col(ck_ref[pl.ds(ks, TQ), :], x) if has_bias else None
                dk_acc = jnp.zeros((TQ, HP), F32)
                dv_acc = jnp.zeros((TQ, LANES), F32)
                key_sum = jnp.zeros((TQ, 1), F32)
                for c in range(nc):
                    dqt, dcq = carry[x * nc + c]
                    s = lax.dot_general(kx, qb[x][c], NT, preferred_element_type=F32) * scale
                    if has_bias:
                        s = s + (cq_ref[x:x + 1, c * QC:(c + 1) * QC] - ckx)
                    if masked:
                        s = jnp.where(qry_in + c * QC >= key_in, s, NEG_INF)
                    p = jnp.exp(s - lse_r[x][c])
                    dp = lax.dot_general(vb, dob[x][c], NT, preferred_element_type=F32)
                    ds = p * (dp - delta[x][c])
                    dsb = ds.astype(BF16)
                    dk_acc = dk_acc + lax.dot_general(dsb, qb[x][c], NN, preferred_element_type=F32)
                    dv_acc = dv_acc + lax.dot_general(p.astype(BF16), dob[x][c], NN, preferred_element_type=F32)
                    dqt = dqt + lax.dot_general(kt, dsb, NN, preferred_element_type=F32)
                    if has_bias:
                        dcq = dcq + jnp.sum(ds, axis=0, keepdims=True)
                        key_sum = key_sum + jnp.sum(ds, axis=1, keepdims=True)
                    new.append((dqt, dcq))
                dk_ref[pl.ds(ks, TQ), x * HP:(x + 1) * HP] += dk_acc * scale
                dv_ref[pl.ds(ks, TQ), :] += dv_acc
                key_sums.append(key_sum)
            if has_bias:
                dck_ref[pl.ds(ks, TQ), :] -= _two_cols(key_sums[0], key_sums[1])
            return tuple(new)

        init = tuple((jnp.zeros((HP, QC), F32), jnp.zeros((1, QC), F32)) for _ in range(2 * nc))
        carry = lax.fori_loop(0, qi, lambda j, cr: block(j, cr, False), init)
        carry = block(qi, carry, True)
        for x in range(2):
            for c in range(nc):
                dq_ref[c * QC:(c + 1) * QC, x * HP:(x + 1) * HP] = carry[x * nc + c][0].T * scale
        if has_bias:
            dcq_ref[...] = _rows2(jnp.concatenate([carry[c][1] for c in range(nc)], axis=1),
                                  jnp.concatenate([carry[nc + c][1] for c in range(nc)], axis=1))

    qblk, kblk, vblk, oblk, rowblk, colblk = _attn_specs(seq, v_blk0, has_bias)
    in_specs = [qblk, kblk, vblk, oblk, oblk, rowblk]
    out_specs = [qblk, kblk, pl.BlockSpec((None, seq, LANES), lambda b, p, i: (b, 0, p))]
    out_shape = [jax.ShapeDtypeStruct((bsz, seq, QKW), F32), jax.ShapeDtypeStruct((bsz, seq, QKW), F32),
                 jax.ShapeDtypeStruct((bsz, seq, 2 * LANES), F32)]
    args = [q, k, v, o, do, lse]
    if has_bias:
        in_specs += [rowblk, colblk]
        out_specs += [colblk, rowblk]
        out_shape += [jax.ShapeDtypeStruct((bsz, 2, seq, 2), F32), jax.ShapeDtypeStruct((bsz, 2, 2, seq), F32)]
        args += [c_rows, c_cols]
    return _pcall(body, name=name, grid=(bsz, 2, nq), in_specs=in_specs, out_specs=out_specs,
                  out_shape=out_shape, rider=rider)(*args)


def _coords(idx):
    return (idx // 4, (idx // 2) % 2, idx % 2)


class _Xfer:
    def __init__(self, srcs, out_shapes, plan):
        self.srcs, self.out_shapes, self.plan = list(srcs), list(out_shapes), plan
        self.n = len(self.srcs)
        me0 = jnp.int32(0)
        n_pieces = [len(plan(a, me0, me0)) for a in range(self.n)]
        self.offs = np.concatenate([[0], np.cumsum(n_pieces)]).astype(int)
        self.total = int(self.offs[-1])

    def sems(self):
        remote = (N_DEV - 1) * self.total
        return [pltpu.SemaphoreType.DMA((remote,)), pltpu.SemaphoreType.DMA((remote,)),
                pltpu.SemaphoreType.DMA((self.total,))]

    def copies(self, src_refs, out_refs, send_sems, recv_sems, local_sems):
        me = 4 * lax.axis_index("x") + 2 * lax.axis_index("y") + lax.axis_index("c")
        out = []
        for a in range(self.n):
            for pi, (si, di) in enumerate(self.plan(a, me, me)):
                out.append((pltpu.make_async_copy(src_refs[a].at[si], out_refs[a].at[di],
                                                  local_sems.at[self.offs[a] + pi]), None))
        for kk in range(1, N_DEV):
            dest = (me + kk) % N_DEV
            src_dev = (me + N_DEV - kk) % N_DEV
            for a in range(self.n):
                pieces = self.plan(a, me, dest)
                landing = self.plan(a, src_dev, me)
                for pi, ((si, di), (_, li)) in enumerate(zip(pieces, landing)):
                    sem = (kk - 1) * self.total + self.offs[a] + pi
                    mk = functools.partial(pltpu.make_async_remote_copy, src_ref=src_refs[a].at[si],
                                           send_sem=send_sems.at[sem], recv_sem=recv_sems.at[sem],
                                           device_id=_coords(dest), device_id_type=pl.DeviceIdType.MESH)
                    out.append((mk(dst_ref=out_refs[a].at[di]), mk(dst_ref=out_refs[a].at[li])))
        return out

    @staticmethod
    def start(copies):
        for cp, _ in copies:
            cp.start()

    @staticmethod
    def wait(copies):
        for cp, rc in copies:
            if rc is None:
                cp.wait()
            else:
                cp.wait_send()
                rc.wait_recv()


def _exchange(name, xf):
    n = xf.n

    def body(*refs):
        copies = xf.copies(refs[:n], refs[n:2 * n], *refs[2 * n:])
        xf.start(copies)
        xf.wait(copies)

    any_spec = pl.BlockSpec(memory_space=pl.ANY)
    return pl.pallas_call(body, name=name, in_specs=[any_spec] * n, out_specs=[any_spec] * n,
                          out_shape=[jax.ShapeDtypeStruct(s, d) for s, d in xf.out_shapes],
                          scratch_shapes=xf.sems(),
                          compiler_params=pltpu.CompilerParams(has_side_effects=True))(*xf.srcs)


class _Gather2(_Xfer):
    def copies(self, src_refs, out_refs, send_sems, recv_sems, local_sems):
        x, y, c = lax.axis_index("x"), lax.axis_index("y"), lax.axis_index("c")
        num = lambda px, py, pc: 4 * px + 2 * py + pc
        me, sib = (x, y, c), (x, y, 1 - c)
        chips = [(1 - x, y), (x, 1 - y), (1 - x, 1 - y)]
        local, first, passed, landing = [], [], [], []
        for a in range(self.n):
            mine = self.plan(a, num(*me), num(*me))
            for pi, (si, di) in enumerate(mine):
                piece = self.offs[a] + pi

                def remote(k, src, dst, to, a=a, piece=piece):
                    return pltpu.make_async_remote_copy(
                        src_ref=src, dst_ref=dst, send_sem=send_sems.at[k * self.total + piece],
                        recv_sem=recv_sems.at[k * self.total + piece], device_id=to, device_id_type=pl.DeviceIdType.MESH)

                def slot(dev, a=a, pi=pi):
                    return out_refs[a].at[self.plan(a, num(*dev), num(*dev))[pi][1]]

                local.append(pltpu.make_async_copy(src_refs[a].at[si], out_refs[a].at[di], local_sems.at[piece]))
                first.append(remote(0, src_refs[a].at[si], out_refs[a].at[di], sib))
                landing.append([remote(0, slot(sib), slot(sib), sib)])
                for j, chip in enumerate(chips):
                    first.append(remote(1 + j, src_refs[a].at[si], out_refs[a].at[di], (*chip, c)))
                    passed.append((remote(1 + j, slot((*chip, c)), slot((*chip, c)), (*chip, c)),
                                   remote(4 + j, slot((*chip, c)), slot((*chip, c)), sib)))
                    landing[-1].append(remote(4 + j, slot((*chip, 1 - c)), slot((*chip, 1 - c)), sib))
        return local, first, passed, landing

    @staticmethod
    def start(copies):
        local, first, _, _ = copies
        for cp in local + first:
            cp.start()

    @staticmethod
    def wait(copies):
        local, first, passed, landing = copies
        for arrival, forward in passed:
            arrival.wait_recv()
            forward.start()
        for group in landing:
            for cp in group:
                cp.wait_recv()
        for cp in first + [fw for _, fw in passed]:
            cp.wait_send()
        for cp in local:
            cp.wait()


def _rows_of(j, n):
    return pl.ds(pl.multiple_of(j * n, n), n)


def _ffn_cols(j, half):
    return pl.multiple_of((2 * (j % 4) + half) * (2 * FF_BLK) + (j // 4) * FF_BLK, FF_BLK)


def _gather_plan(shards):
    names = list(shards)

    def plan(a, src_dev, _dest):
        nm = names[a]
        j = src_dev
        if nm in ("w_in", "w_uq", "w_ukv", "w_glu", "w_out", "w_down"):
            rows = shards[nm].shape[0]
            return [((slice(None), slice(None)), (_rows_of(j, rows), slice(None)))]
        if nm == "w_branch":
            return [((slice(None), slice(None), slice(None)), (slice(None), slice(None), _rows_of(j, LANES)))]
        if nm in ("w_up", "conv"):
            return [((slice(None), pl.ds(h * FF_BLK, FF_BLK)), (slice(None), pl.ds(_ffn_cols(j, h), FF_BLK)))
                    for h in range(2)]
        raise KeyError(nm)

    return names, plan


def _scatter_plan(names, shard_shapes):
    def plan(a, src_dev, dest):
        nm = names[a]
        j = dest
        if nm in ("w_in", "w_uq", "w_ukv", "w_glu", "w_out", "w_down"):
            rows = shard_shapes[nm][0]
            return [((_rows_of(j, rows), slice(None)), (src_dev, slice(None), slice(None)))]
        if nm == "w_branch":
            return [((slice(None), slice(None), _rows_of(j, LANES)), (src_dev, slice(None), slice(None), slice(None)))]
        if nm in ("w_up", "conv"):
            return [((slice(None), pl.ds(_ffn_cols(j, h), FF_BLK)), (src_dev, slice(None), pl.ds(h * FF_BLK, FF_BLK)))
                    for h in range(2)]
        if nm in ("w_up:0", "w_up:1"):
            h = int(nm[-1])
            return [((slice(None), pl.ds(_ffn_cols(j, h), FF_BLK)), (src_dev, slice(None), slice(None)))]
        if nm == "small":
            return [((slice(None), slice(None)), (src_dev, slice(None), slice(None)))]
        raise KeyError(nm)

    return plan


def _pad_in_cols(w):
    lead = w.shape[:-1]
    parts, pos = [], 0
    for dst, src, width in IN_SEGS:
        if dst > pos:
            parts.append(jnp.zeros(lead + (dst - pos,), w.dtype))
        parts.append(w[..., src:src + width])
        pos = dst + width
    return jnp.concatenate(parts, axis=-1)


def _unpad_in_cols(w):
    order = sorted(IN_SEGS, key=lambda s: s[1])
    return jnp.concatenate([w[..., dst:dst + width] for dst, _, width in order], axis=-1)


def _pad_ff(w, axis):
    n = w.shape[axis] // FF_HALF
    parts = []
    for h in range(n):
        piece = lax.slice_in_dim(w, h * FF_HALF, (h + 1) * FF_HALF, axis=axis)
        zshape = list(w.shape)
        zshape[axis] = FF_BLK - FF_HALF
        parts += [piece, jnp.zeros(zshape, w.dtype)]
    return jnp.concatenate(parts, axis=axis)


def _unpad_ff(w, axis):
    n = w.shape[axis] // FF_BLK
    return jnp.concatenate([lax.slice_in_dim(w, h * FF_BLK, h * FF_BLK + FF_HALF, axis=axis) for h in range(n)],
                           axis=axis)


def _to_scan_cols(re_part, im_part):
    lead = re_part.shape[:-1]
    nb = S5_N // SCAN_W
    r = re_part.reshape(lead + (nb, SCAN_W))
    i = im_part.reshape(lead + (nb, SCAN_W))
    return jnp.concatenate([r, i], axis=-1).reshape(lead + (2 * S5_N,))


def _from_scan_cols(x):
    lead = x.shape[:-1]
    nb = S5_N // SCAN_W
    y = x.reshape(lead + (nb, 2, SCAN_W))
    return y[..., 0, :].reshape(lead + (S5_N,)), y[..., 1, :].reshape(lead + (S5_N,))


def _block_diag_in(b):
    eye = jnp.eye(S5_GROUPS, dtype=b.dtype)
    return jnp.einsum("gph,gk->ghkp", b, eye).reshape(S5_WIDTH, S5_N)


def _block_diag_in_t(m):
    eye = jnp.eye(S5_GROUPS, dtype=m.dtype)
    return jnp.einsum("ghkp,gk->gph", m.reshape(S5_GROUPS, S5_GROUP_CH, S5_GROUPS, S5_STATE), eye)


def _block_diag_out(c):
    eye = jnp.eye(S5_GROUPS, dtype=c.dtype)
    return jnp.einsum("ghp,gk->gpkh", c, eye).reshape(S5_N, S5_WIDTH)


def _block_diag_out_t(m):
    eye = jnp.eye(S5_GROUPS, dtype=m.dtype)
    return jnp.einsum("gpkh,gk->ghp", m.reshape(S5_GROUPS, S5_STATE, S5_GROUPS, S5_GROUP_CH), eye)


def _pad_heads(g, width):
    g = jnp.broadcast_to(g.reshape(-1, width), (N_HEADS, width))
    return jnp.pad(g, ((0, 0), (0, HP - width))).reshape(1, QKW)


def _prep_layer(l, gw, small, consts):
    e_mat, p_mat, r_mat, pl_mat = consts
    p = {}
    p["attn_g"] = small["attn_norm_g"][l].reshape(1, D_MODEL)
    p["ffn_g"] = small["ffn_norm_g"][l].reshape(1, D_MODEL)
    p["qlg"] = small["q_lat_norm_g"][l].reshape(1, MLA_Q_RANK)
    p["kvlg"] = small["kv_lat_norm_g"][l].reshape(1, MLA_KV_RANK)
    p["mqg"] = _pad_heads(small["mla_q_norm_g"][l], MLA_QK_DIM)
    p["mkg"] = _pad_heads(small["mla_k_norm_g"][l], MLA_QK_DIM)
    p["fqg"] = _pad_heads(small["fox_q_norm_g"][l], HEAD_V)
    p["fkg"] = _pad_heads(small["fox_k_norm_g"][l], HEAD_V)
    w_uq = gw["w_uq"].reshape(MLA_Q_RANK, N_HEADS, MLA_QK_DIM)
    p["w_uq"] = jnp.pad(w_uq, ((0, 0), (0, 0), (0, HP - MLA_QK_DIM))).reshape(MLA_Q_RANK, QKW)
    w_ukv = gw["w_ukv"].reshape(MLA_KV_RANK, N_HEADS, MLA_NOPE_DIM + HEAD_V)
    p["w_uk"] = jnp.pad(w_ukv[..., :MLA_NOPE_DIM], ((0, 0), (0, 0), (0, HP - MLA_NOPE_DIM))).reshape(MLA_KV_RANK, QKW)
    p["w_uv"] = w_ukv[..., MLA_NOPE_DIM:].reshape(MLA_KV_RANK, N_HEADS * HEAD_V)
    p["w_glu"] = gw["w_glu"]
    p["b_glu"] = small["s5_b_glu"][l].reshape(1, S5_WIDTH)
    p["d_row"] = small["s5_d"][l].reshape(1, S5_WIDTH)
    p["lam_re"] = small["s5_lambda_re"][l].reshape(S5_N, 1)
    p["lam_im"] = small["s5_lambda_im"][l].reshape(S5_N, 1)
    p["log_step"] = jnp.repeat(small["s5_log_step"][l], S5_STATE).reshape(S5_N, 1)
    p["b_re"] = small["s5_b_re"][l].reshape(S5_N, S5_GROUP_CH)
    p["b_im"] = small["s5_b_im"][l].reshape(S5_N, S5_GROUP_CH)
    c_re, c_im = small["s5_c_re"][l], small["s5_c_im"][l]
    p["c_blk"] = _to_scan_cols(_block_diag_out(c_re).T, -_block_diag_out(c_im).T).T
    p["fbias"] = small["fox_f_bias"][l]
    for nm in ("w_in", "w_out", "w_branch"):
        p[nm] = gw[nm]
    return p


def _s5_param_call(p, l):
    ins = [p["lam_re"], p["lam_im"], p["log_step"], p["b_re"], p["b_im"]]
    outs = [((S5_N, 1), F32, (S5_N, 1), lambda i: (0, 0), None)] * 2 + \
           [((S5_N, S5_GROUP_CH), F32, (S5_N, S5_GROUP_CH), lambda i: (0, 0), None)] * 2
    return _tile_call(_s5_params, "s5_params", (1,), [_par(a) for a in ins], outs)


def _s5_param_bwd_call(p, cts):
    ins = [p["lam_re"], p["lam_im"], p["log_step"], p["b_re"], p["b_im"]]

    def fn(lr, li, ls, br, bi, g0, g1, g2, g3):
        _, vjp = jax.vjp(_s5_params, lr, li, ls, br, bi)
        return vjp((g0, g1, g2, g3))

    outs = [((S5_N, 1), F32, (S5_N, 1), lambda i: (0, 0), None)] * 3 + \
           [((S5_N, S5_GROUP_CH), F32, (S5_N, S5_GROUP_CH), lambda i: (0, 0), None)] * 2
    return _tile_call(fn, "s5_params_bwd", (1,), [_par(a) for a in ins + list(cts)], outs)


def _layer_fwd(x_prev, f_prev, p, tabs, consts, bsz, riders):
    arrived = {}

    def rider_of(name):
        return riders[name][0] if name in riders else None

    def hosted(name, res, n_out):
        if name in riders:
            arrived.update(zip(riders[name][1], res[n_out:]))
        return res[:n_out]

    t = x_prev.shape[0]
    seq = t // bsz
    nt = t // TM
    cos, sin = tabs
    e_mat, p_mat, r_mat, pl_mat = consts
    sv = {}

    if f_prev is None:
        x = x_prev
        h = _tile_call(lambda xv, g: _rms(xv, g), "norm_first", (nt,), [_row(x), _par(p["attn_g"])],
                       [_orow(t, D_MODEL, BF16)])[0]
    else:
        x, h = _tile_call(lambda xv, fv, g: (xv + fv, _rms(xv + fv, g)), "norm_attn", (nt,),
                          [_row(x_prev), _row(f_prev), _par(p["attn_g"])], [_orow(t, D_MODEL), _orow(t, D_MODEL, BF16)])
    sv["x"], sv["h"] = x, h
    proj = _mm(h, p["w_in"], "nn", "mm_in")
    sv["proj"] = proj

    def pre_fn(pa, cosv, sinv, qlg, w_uq, kvlg, w_uk, w_uv, mqg, mkg, fqg, fkg, em, pm, rm, plm):
        return _mixer_pre(pa[:, O_CQ:O_CQ + MLA_Q_RANK], pa[:, O_CKV:O_CKV + MLA_KV_RANK], pa[:, O_KR:O_KR + LANES],
                          pa[:, O_FQ:O_FQ + 256], pa[:, O_FK:O_FK + 256], qlg, w_uq, kvlg, w_uk, w_uv,
                          mqg, mkg, fqg, fkg, cosv, sinv, em, pm, rm, plm)

    pre_params = [p["qlg"], p["w_uq"], p["kvlg"], p["w_uk"], p["w_uv"], p["mqg"], p["mkg"], p["fqg"], p["fkg"],
                  e_mat, p_mat, r_mat, pl_mat]
    q_m, k_m, v_m, q_f, k_f = hosted("mixer_pre", _tile_call(
        pre_fn, "mixer_pre", (nt,), [_row(proj, PA), _row(cos), _row(sin)] + [_par(a) for a in pre_params],
        [_orow(t, QKW), _orow(t, QKW), _orow(t, 2 * LANES), _orow(t, QKW), _orow(t, QKW)],
        rider=rider_of("mixer_pre")), 5)
    sv.update(q_m=q_m, k_m=k_m, v_m=v_m, q_f=q_f, k_f=k_f)

    z = proj[:, O_FL:O_FL + N_HEADS].reshape(bsz, seq, N_HEADS).transpose(0, 2, 1).reshape(bsz * N_HEADS, seq)
    fb = jnp.tile(p["fbias"], bsz).reshape(bsz * N_HEADS, 1)
    cum = _forget_fwd(z, fb, "forget_fwd")
    ck = cum.reshape(bsz, 2, 2, seq)
    cq = ck.transpose(0, 1, 3, 2)
    sv.update(z=z, fb=fb, cq=cq, ck=ck)

    r3 = lambda a: a.reshape(bsz, seq, a.shape[-1])
    o_m, lse_m = hosted("attn_mla_fwd", _attn_fwd(r3(q_m), r3(k_m), r3(v_m), 0, None, None, MLA_QK_DIM ** -0.5,
                                                  "attn_mla_fwd", rider_of("attn_mla_fwd")), 2)
    o_f, lse_f = hosted("attn_fox_fwd", _attn_fwd(r3(q_f), r3(k_f), r3(proj), O_FV // LANES, cq, ck, HEAD_V ** -0.5,
                                                  "attn_fox_fwd", rider_of("attn_fox_fwd")), 2)
    o_m, o_f = o_m.reshape(t, 2 * LANES), o_f.reshape(t, 2 * LANES)
    sv.update(o_m=o_m, o_f=o_f, lse_m=lse_m, lse_f=lse_f)

    lbar_re, lbar_im, bbar_re, bbar_im = _s5_param_call(p, 0)
    lbar = _to_scan_cols(lbar_re.reshape(1, S5_N), lbar_im.reshape(1, S5_N))
    b_blk = _to_scan_cols(_block_diag_in(bbar_re.reshape(S5_GROUPS, S5_STATE, S5_GROUP_CH)),
                          _block_diag_in(bbar_im.reshape(S5_GROUPS, S5_STATE, S5_GROUP_CH)))
    sv.update(lbar=lbar, b_blk=b_blk)
    bu = _tile_call(lambda u, bb: bdot(u, bb), "s5_bu", (nt,), [_row(proj, S5_WIDTH, O_U // S5_WIDTH), _par(b_blk)],
                    [_orow(t, 2 * S5_N)])[0]
    state = hosted("s5_scan_fwd", _scan_fwd(bu, lbar, bsz, "s5_scan_fwd", rider_of("s5_scan_fwd")), 1)[0]
    sv["state"] = state
    post_params = [p["c_blk"], p["d_row"], p["w_glu"], p["b_glu"]]
    o_s = _tile_call(_s5_post, "s5_post", (nt,),
                     [_row(state), _row(proj, S5_WIDTH, O_U // S5_WIDTH)] + [_par(a) for a in post_params],
                     [_orow(t, S5_WIDTH)])[0]
    sv["o_s"] = o_s

    def merge_fn(oa, ob, oc, ga, gb, gc, wb):
        return _merge(oa, ob, oc, ga, gb, gc, wb[0], wb[1], wb[2])

    gate_specs = [_row(proj, D_MODEL, PA // D_MODEL + n) for n in range(3)]
    merged = _tile_call(merge_fn, "merge", (nt,), [_row(o_m), _row(o_f), _row(o_s)] + gate_specs + [_par(p["w_branch"])],
                        [_orow(t, D_MODEL, BF16)])[0]
    sv["merged"] = merged
    attn_out = _mm(merged, p["w_out"], "nn", "mm_out")

    x1, h2 = _tile_call(lambda xv, av, g: (xv + av, _rms(xv + av, g)), "norm_ffn", (nt,),
                        [_row(x), _row(attn_out), _par(p["ffn_g"])], [_orow(t, D_MODEL), _orow(t, D_MODEL, BF16)])
    sv["x1"], sv["h2"] = x1, h2
    p.update({n: arrived.pop(n) for n in FFN_WEIGHTS})
    up = _mm(h2, p["w_up"], "nn", "mm_up")
    sv["up"] = up

    def ffn_fn(upv, cw):
        return _ffn_mid(upv[:, :FF_BLK], upv[:, FF_BLK:], cw[0:1, :FF_BLK], cw[1:2, :FF_BLK], cw[2:3, :FF_BLK],
                        cw[0:1, FF_BLK:], cw[1:2, FF_BLK:], cw[2:3, FF_BLK:])

    nblk = FF_PW // FF_BLK
    act = _tile_call(ffn_fn, "ffn_mid", (nblk, bsz),
                     [(up, (seq, 2 * FF_BLK), lambda j, b: (b, j)), (p["conv"], (3, 2 * FF_BLK), lambda j, b: (0, j))],
                     [((t, FF_PW), BF16, (seq, FF_BLK), lambda j, b: (b, j), None)])[0]
    sv["act"] = act
    ffn_out = _mm(act, p["w_down"], "nn", "mm_down")
    return x1, ffn_out, sv, arrived


def _layer_bwd(dx2, p, sv, tabs, consts, bsz, scatter, carry, last_layer):
    lands, carry_lands = {}, {}

    def riding(host, g):
        names = BWD_RIDERS[host] + (("w_in",) if host == "mm_in_dx" and last_layer else ())
        return scatter({n: g[n.split(":")[0]] for n in names}), names

    def landed(host, res, n_out):
        names = BWD_RIDERS[host] + (("w_in",) if host == "mm_in_dx" and last_layer else ())
        lands.update(zip(names, res[n_out:]))
        return res[:n_out]

    t = dx2.shape[0]
    seq = t // bsz
    nt = t // TM
    cos, sin = tabs
    e_mat, p_mat, r_mat, pl_mat = consts
    g = {}

    g["w_down"] = _mm(sv["act"], dx2, "tn", "mm_down_dw", out_dtype=BF16)
    dact = _mm(dx2, p["w_down"], "nt", "mm_down_dx")

    def ffn_bwd_fn(upv, cw, da):
        args = (upv[:, :FF_BLK], upv[:, FF_BLK:], cw[0:1, :FF_BLK], cw[1:2, :FF_BLK], cw[2:3, :FF_BLK],
                cw[0:1, FF_BLK:], cw[1:2, FF_BLK:], cw[2:3, FF_BLK:])
        _, vjp = jax.vjp(_ffn_mid, *args)
        dg, dv, g0, g1, g2, v0, v1, v2 = vjp(da)
        return (jnp.concatenate([dg, dv], axis=1), jnp.concatenate([g0, v0], axis=1),
                jnp.concatenate([g1, v1], axis=1), jnp.concatenate([g2, v2], axis=1))

    nblk = FF_PW // FF_BLK
    cw_out = ((1, UP_W), F32, (1, 2 * FF_BLK), lambda j, b: (0, j), (1,))
    res = _tile_call(
        ffn_bwd_fn, "ffn_mid_bwd", (nblk, bsz),
        [(sv["up"], (seq, 2 * FF_BLK), lambda j, b: (b, j)), (p["conv"], (3, 2 * FF_BLK), lambda j, b: (0, j)),
         (dact, (seq, FF_BLK), lambda j, b: (b, j))],
        [((t, UP_W), BF16, (seq, 2 * FF_BLK), lambda j, b: (b, j), None), cw_out, cw_out, cw_out],
        rider=scatter(carry) if carry else None)
    dup, dc0, dc1, dc2 = res[:4]
    carry_lands.update(zip(carry, res[4:]))
    g["conv"] = jnp.concatenate([dc0, dc1, dc2], axis=0)
    g["w_up"] = _mm(sv["h2"], dup, "tn", "mm_up_dw", out_dtype=BF16)
    dh2 = landed("mm_up_dx", _mm(dup, p["w_up"], "nt", "mm_up_dx", rider=riding("mm_up_dx", g)[0]), 1)[0]

    def norm_bwd_fn(xv, gv, dh, dres):
        _, vjp = jax.vjp(_rms, xv, gv)
        dxv, dg = vjp(dh)
        return dres + dxv, dg

    d1, g["ffn_g"] = _tile_call(norm_bwd_fn, "norm_bwd", (nt,),
                                [_row(sv["x1"]), _par(p["ffn_g"]), _row(dh2), _row(dx2)],
                                [_orow(t, D_MODEL), _oacc((1, D_MODEL))])

    g["w_out"] = _mm(sv["merged"], d1, "tn", "mm_out_dw", out_dtype=BF16)
    dmerged = _mm(d1, p["w_out"], "nt", "mm_out_dx")

    def merge_bwd_fn(oa, ob, oc, ga, gb, gc, wb, dm):
        wf = wb.astype(F32)
        _, vjp = jax.vjp(_merge, oa, ob, oc, ga, gb, gc, wf[0], wf[1], wf[2])
        doa, dob, doc, dga, dgb, dgc, dwa, dwb, dwc = vjp(dm)
        return doa, dob, doc, jnp.concatenate([dga, dgb, dgc], axis=1), jnp.stack([dwa, dwb, dwc])

    proj = sv["proj"]
    gate_specs = [_row(proj, D_MODEL, PA // D_MODEL + n) for n in range(3)]
    do_m, do_f, do_s, dgl, g["w_branch"] = _tile_call(
        merge_bwd_fn, "merge_bwd", (nt,),
        [_row(sv["o_m"]), _row(sv["o_f"]), _row(sv["o_s"])] + gate_specs + [_par(p["w_branch"]), _row(dmerged)],
        [_orow(t, 2 * LANES), _orow(t, 2 * LANES), _orow(t, S5_WIDTH), _orow(t, 3 * D_MODEL),
         _oacc((3, S5_WIDTH, D_MODEL))])

    def post_bwd_fn(st, u, cb, dr, wg, bg, do):
        _, vjp = jax.vjp(_s5_post, st, u, cb.astype(F32), dr, wg.astype(F32), bg)
        return vjp(do)

    u_spec = _row(proj, S5_WIDTH, O_U // S5_WIDTH)
    dstate, du1, dc_blk, g_d, g["w_glu"], g_bglu = _tile_call(
        post_bwd_fn, "s5_post_bwd", (nt,),
        [_row(sv["state"]), u_spec, _par(p["c_blk"]), _par(p["d_row"]), _par(p["w_glu"]), _par(p["b_glu"]), _row(do_s)],
        [_orow(t, 2 * S5_N), _orow(t, S5_WIDTH), _oacc((2 * S5_N, S5_WIDTH)), _oacc((1, S5_WIDTH)),
         _oacc((S5_WIDTH, S5_WIDTH)), _oacc((1, S5_WIDTH))])
    dbu, dlbar = landed("s5_scan_bwd", _scan_bwd(dstate, sv["state"], sv["lbar"], bsz, "s5_scan_bwd",
                                                 riding("s5_scan_bwd", g)[0]), 2)

    def bu_bwd_fn(u, bb, dbv, du_in):
        _, vjp = jax.vjp(bdot, u, bb)
        du, dbb = vjp(dbv)
        return du_in + du, dbb

    du, db_blk = _tile_call(bu_bwd_fn, "s5_bu_bwd", (nt,), [u_spec, _par(sv["b_blk"]), _row(dbu), _row(du1)],
                            [_orow(t, S5_WIDTH), _oacc((S5_WIDTH, 2 * S5_N))])
    dlr, dli = _from_scan_cols(dlbar)
    dbr, dbi = _from_scan_cols(db_blk)
    cts = (dlr.reshape(S5_N, 1), dli.reshape(S5_N, 1),
           _block_diag_in_t(dbr).reshape(S5_N, S5_GROUP_CH), _block_diag_in_t(dbi).reshape(S5_N, S5_GROUP_CH))
    g_lr, g_li, g_ls, g_br, g_bi = _s5_param_bwd_call(p, cts)
    dc_re, dc_im = _from_scan_cols(dc_blk.T)
    g["s5_lambda_re"] = g_lr.reshape(S5_GROUPS, S5_STATE)
    g["s5_lambda_im"] = g_li.reshape(S5_GROUPS, S5_STATE)
    g["s5_log_step"] = jnp.sum(g_ls.reshape(S5_GROUPS, S5_STATE), axis=1)
    g["s5_b_re"] = g_br.reshape(S5_GROUPS, S5_STATE, S5_GROUP_CH)
    g["s5_b_im"] = g_bi.reshape(S5_GROUPS, S5_STATE, S5_GROUP_CH)
    g["s5_c_re"] = _block_diag_out_t(dc_re.T)
    g["s5_c_im"] = -_block_diag_out_t(dc_im.T)
    g["s5_d"] = g_d.reshape(S5_GROUPS, S5_GROUP_CH)
    g["s5_b_glu"] = g_bglu.reshape(S5_WIDTH)

    r3 = lambda a: a.reshape(bsz, seq, a.shape[-1])
    dq_m, dk_m, dv_m = landed("attn_mla_bwd", _attn_bwd(
        r3(sv["q_m"]), r3(sv["k_m"]), r3(sv["v_m"]), 0, r3(sv["o_m"]), r3(do_m), sv["lse_m"], None, None,
        MLA_QK_DIM ** -0.5, "attn_mla_bwd", riding("attn_mla_bwd", g)[0]), 3)
    dq_f, dk_f, dv_f, dcq, dck = landed("attn_fox_bwd", _attn_bwd(
        r3(sv["q_f"]), r3(sv["k_f"]), r3(proj), O_FV // LANES, r3(sv["o_f"]), r3(do_f), sv["lse_f"], sv["cq"], sv["ck"],
        HEAD_V ** -0.5, "attn_fox_bwd", riding("attn_fox_bwd", g)[0]), 5)
    dcum = (dck + dcq.transpose(0, 1, 3, 2)).reshape(bsz * N_HEADS, seq)
    dz, dfb = _forget_bwd(dcum, sv["z"], sv["fb"], "forget_bwd")
    g["fox_f_bias"] = jnp.sum(dfb.reshape(bsz, N_HEADS), axis=0)
    dfl = jnp.pad(dz.reshape(bsz, N_HEADS, seq).transpose(0, 2, 1).reshape(t, N_HEADS), ((0, 0), (0, LANES - N_HEADS)))

    def pre_bwd_fn(pa, cosv, sinv, qlg, w_uq, kvlg, w_uk, w_uv, mqg, mkg, fqg, fkg, em, pm, rm, plm,
                   gq, gk, gv, gqf, gkf, gvf, gu, gfl, ggl):
        f = functools.partial(_mixer_pre, cos=cosv, sin=sinv, e_mat=em, p_mat=pm, r_mat=rm, pl_mat=plm)
        prim = (pa[:, O_CQ:O_CQ + MLA_Q_RANK], pa[:, O_CKV:O_CKV + MLA_KV_RANK], pa[:, O_KR:O_KR + LANES],
                pa[:, O_FQ:O_FQ + 256], pa[:, O_FK:O_FK + 256], qlg, w_uq.astype(F32), kvlg, w_uk.astype(F32),
                w_uv.astype(F32), mqg, mkg, fqg, fkg)
        _, vjp = jax.vjp(lambda *a: f(*a), *prim)
        dcq_, dckv, dkr, dfq, dfk, dqlg, dwuq, dkvlg, dwuk, dwuv, dmqg, dmkg, dfqg, dfkg = vjp((gq, gk, gv, gqf, gkf))
        zpad = jnp.zeros((pa.shape[0], PA - O_CQ - MLA_Q_RANK), F32)
        dproj = jnp.concatenate([dckv, dfq, dfk, gvf, gu, dkr, gfl, dcq_, zpad, ggl], axis=1)
        return dproj, dqlg, dwuq, dkvlg, dwuk, dwuv, dmqg, dmkg, dfqg, dfkg

    pre_params = [p["qlg"], p["w_uq"], p["kvlg"], p["w_uk"], p["w_uv"], p["mqg"], p["mkg"], p["fqg"], p["fkg"],
                  e_mat, p_mat, r_mat, pl_mat]
    cts_in = [dq_m.reshape(t, QKW), dk_m.reshape(t, QKW), dv_m.reshape(t, 2 * LANES), dq_f.reshape(t, QKW),
              dk_f.reshape(t, QKW), dv_f.reshape(t, 2 * LANES), du, dfl, dgl]
    (dproj, g_qlg, g_wuq, g_kvlg, g_wuk, g_wuv, g_mqg, g_mkg, g_fqg, g_fkg) = _tile_call(
        pre_bwd_fn, "mixer_pre_bwd", (nt,),
        [_row(proj, PA), _row(cos), _row(sin)] + [_par(a) for a in pre_params] + [_row(a) for a in cts_in],
        [_orow(t, PW, BF16), _oacc((1, MLA_Q_RANK)), _oacc((MLA_Q_RANK, QKW)), _oacc((1, MLA_KV_RANK)),
         _oacc((MLA_KV_RANK, QKW)), _oacc((MLA_KV_RANK, N_HEADS * HEAD_V)), _oacc((1, QKW)), _oacc((1, QKW)),
         _oacc((1, QKW)), _oacc((1, QKW))])
    g["q_lat_norm_g"] = g_qlg.reshape(MLA_Q_RANK)
    g["kv_lat_norm_g"] = g_kvlg.reshape(MLA_KV_RANK)
    heads = lambda a, w: jnp.sum(a.reshape(N_HEADS, HP)[:, :w], axis=0)
    g["mla_q_norm_g"], g["mla_k_norm_g"] = heads(g_mqg, MLA_QK_DIM), heads(g_mkg, MLA_QK_DIM)
    g["fox_q_norm_g"], g["fox_k_norm_g"] = heads(g_fqg, HEAD_V), heads(g_fkg, HEAD_V)
    g["w_uq"] = g_wuq.reshape(MLA_Q_RANK, N_HEADS, HP)[..., :MLA_QK_DIM].reshape(MLA_Q_RANK, N_HEADS * MLA_QK_DIM)
    g["w_ukv"] = jnp.concatenate([g_wuk.reshape(MLA_KV_RANK, N_HEADS, HP)[..., :MLA_NOPE_DIM],
                                  g_wuv.reshape(MLA_KV_RANK, N_HEADS, HEAD_V)], axis=-1).reshape(MLA_KV_RANK, QKW)

    g["w_in"] = _mm(sv["h"], dproj, "tn", "mm_in_dw", out_dtype=BF16)
    dh = landed("mm_in_dx", _mm(dproj, p["w_in"], "nt", "mm_in_dx", rider=riding("mm_in_dx", g)[0]), 1)[0]
    dx, g["attn_g"] = _tile_call(norm_bwd_fn, "norm_bwd", (nt,),
                                 [_row(sv["x"]), _par(p["attn_g"]), _row(dh), _row(d1)],
                                 [_orow(t, D_MODEL), _oacc((1, D_MODEL))])
    return dx, g, lands, carry_lands


def _row_tile(r, c, max_elems):
    if r * c <= max_elems:
        return r
    best = None
    for d in range(8, r, 8):
        if r % d == 0 and d * c <= max_elems:
            best = d
    assert best is not None, (r, c)
    return best


def _sum8(land, name):
    _, r, c = land.shape
    tr = _row_tile(r, N_DEV * c, 1 << 20)

    def fn(lv):
        acc = lv[0].astype(F32)
        for i in range(1, N_DEV):
            acc = acc + lv[i].astype(F32)
        return acc

    return _tile_call(fn, name, (r // tr,), [(land, (N_DEV, tr, c), lambda i: (0, i, 0))],
                      [((r, c), F32, (tr, c), lambda i: (i, 0), None)])[0]


def _adamw_layer(w, g, m, v, l, prev, name):
    _, r, c = w.shape
    tr = _row_tile(r, c, 1 << 19)
    lay = pl.BlockSpec((None, tr, c), lambda i: (l, i, 0))

    def body(w_ref, g_ref, m_ref, v_ref, *rest):
        g_out, d_out, m_out, v_out = rest[-4:]
        gv = g_ref[...]
        d_out[...], m_out[...], v_out[...] = _adamw(w_ref[...], gv, m_ref[...], v_ref[...])
        g_out[...] = gv

    in_specs = [lay, pl.BlockSpec((tr, c), lambda i: (i, 0)), lay, lay]
    args = [w, g, m, v]
    aliases = {}
    if prev is not None:
        in_specs += [pl.BlockSpec(memory_space=pl.ANY)] * 4
        args += list(prev)
        aliases = {4 + k: k for k in range(4)}
    return _pcall(body, name=name, grid=(r // tr,), in_specs=in_specs, out_specs=[lay] * 4,
                  out_shape=[jax.ShapeDtypeStruct(w.shape, F32)] * 4, aliases=aliases)(*args)


SMALL_NAMES = ("attn_norm_g", "q_lat_norm_g", "kv_lat_norm_g", "mla_q_norm_g", "mla_k_norm_g", "fox_q_norm_g",
               "fox_k_norm_g", "fox_f_bias", "s5_lambda_re", "s5_lambda_im", "s5_b_re", "s5_b_im", "s5_c_re", "s5_c_im",
               "s5_d", "s5_log_step", "s5_b_glu", "ffn_norm_g")
BIG_NAMES = ("w_in", "w_uq", "w_ukv", "s5_w_glu", "w_branch", "w_out", "w_up", "ffn_conv_w", "w_down")
ALL_NAMES = ("attn_norm_g", "w_in", "q_lat_norm_g", "w_uq", "kv_lat_norm_g", "w_ukv", "mla_q_norm_g", "mla_k_norm_g",
             "fox_q_norm_g", "fox_k_norm_g", "fox_f_bias", "s5_lambda_re", "s5_lambda_im", "s5_b_re", "s5_b_im",
             "s5_c_re", "s5_c_im", "s5_d", "s5_log_step", "s5_w_glu", "s5_b_glu", "w_branch", "w_out", "ffn_norm_g",
             "w_up", "ffn_conv_w", "w_down")


def _pack(arrs):
    tile = 8 * LANES
    parts = []
    for a in arrs:
        n = int(np.prod(a.shape))
        tot = -(-n // tile) * tile
        parts.append(jnp.pad(a.reshape(-1), (0, tot - n)).reshape(tot // LANES, LANES))
    rows = sum(a.shape[0] for a in parts)
    parts.append(jnp.zeros((-rows % 256, LANES), parts[0].dtype))
    return jnp.concatenate(parts, axis=0)


def _unpack(packed, like):
    tile = 8 * LANES
    out, pos = [], 0
    for a in like:
        n = int(np.prod(a.shape))
        rows = -(-n // tile) * 8
        out.append(packed[pos:pos + rows].reshape(-1)[:n].reshape(a.shape))
        pos += rows
    return out


def _rope_tables(positions, t):
    pos = positions.reshape(t, 1)
    inv = _inv_freq_row()

    def fn(pv, iv):
        ang = pv.astype(F32) * iv
        return jnp.cos(ang), jnp.sin(ang)

    return _tile_call(fn, "rope_tables", (t // TM,), [_row(pos), _par(inv)], [_orow(t, QKW), _orow(t, QKW)])


def _loss_call(x1, f, target):
    t = x1.shape[0]

    def fn(xv, fv, tv):
        e = xv + fv - tv
        per_tok = jnp.sum(e * e, axis=1, keepdims=True) * (1.0 / D_MODEL)
        return 0.5 * jnp.sum(per_tok, axis=0, keepdims=True), e * (1.0 / D_MODEL)

    return _tile_call(fn, "loss", (t // TM,), [_row(x1), _row(f), _row(target)], [_oacc((1, 1)), _orow(t, D_MODEL)])


def _local_shards(l, w):
    return {
        "w_in": _pad_in_cols(w["w_in"][l]).astype(BF16),
        "w_uq": w["w_uq"][l].reshape(-1, N_HEADS * MLA_QK_DIM).astype(BF16),
        "w_ukv": w["w_ukv"][l].reshape(-1, QKW).astype(BF16),
        "w_glu": w["s5_w_glu"][l].astype(BF16),
        "w_branch": w["w_branch"][l].astype(BF16),
        "w_out": w["w_out"][l].astype(BF16),
        "w_up": _pad_ff(w["w_up"][l], 1).astype(BF16),
        "conv": _pad_ff(w["ffn_conv_w"][l], 1),
        "w_down": _pad_ff(w["w_down"][l], 0).astype(BF16),
    }


GATHERED_SHAPES = {"w_in": (D_MODEL, PW), "w_uq": (MLA_Q_RANK, N_HEADS * MLA_QK_DIM), "w_ukv": (MLA_KV_RANK, QKW),
                   "w_glu": (S5_WIDTH, S5_WIDTH), "w_branch": (3, S5_WIDTH, D_MODEL), "w_out": (D_MODEL, D_MODEL),
                   "w_up": (D_MODEL, UP_W), "conv": (3, UP_W), "w_down": (FF_PW, D_MODEL)}
FFN_WEIGHTS = ("w_up", "conv", "w_down")
MIXER_WEIGHTS = ("w_in", "w_uq", "w_ukv", "w_glu", "w_out", "w_branch")
FWD_RIDERS_OWN = {"attn_mla_fwd": ("w_up", "conv"), "s5_scan_fwd": ("w_down",)}
FWD_RIDERS_NEXT = {"attn_fox_fwd": ("w_in", "w_uq", "w_ukv", "w_glu"), "mixer_pre": ("w_out", "w_branch")}
BWD_RIDERS = {"mm_up_dx": ("w_up:0",), "s5_scan_bwd": ("w_down",), "attn_mla_bwd": ("w_up:1",),
              "attn_fox_bwd": ("w_out", "w_branch", "w_glu", "conv"), "mm_in_dx": ("w_uq", "w_ukv")}
UPDATED_TRANSPOSED = ("w_up",)
WEIGHT_OF = {"w_in": "w_in", "w_uq": "w_uq", "w_ukv": "w_ukv", "w_glu": "s5_w_glu", "w_branch": "w_branch",
             "w_out": "w_out", "w_up": "w_up", "conv": "ffn_conv_w", "w_down": "w_down"}


def _unshard_layout(nm, g2):
    if nm == "w_in":
        return _unpad_in_cols(g2)
    if nm == "w_uq":
        return g2.reshape(-1, N_HEADS, MLA_QK_DIM)
    if nm == "w_ukv":
        return g2.reshape(-1, N_HEADS, MLA_NOPE_DIM + HEAD_V)
    if nm in ("w_up", "conv"):
        return _unpad_ff(g2, 1)
    if nm == "w_down":
        return _unpad_ff(g2, 0)
    return g2


def kernel(x, positions, attn_norm_g, w_in, q_lat_norm_g, w_uq, kv_lat_norm_g, w_ukv, mla_q_norm_g, mla_k_norm_g, fox_q_norm_g, fox_k_norm_g, fox_f_bias, s5_lambda_re, s5_lambda_im, s5_b_re, s5_b_im, s5_c_re, s5_c_im, s5_d, s5_log_step, s5_w_glu, s5_b_glu, w_branch, w_out, ffn_norm_g, w_up, ffn_conv_w, w_down, loss_target, m_attn_norm_g, m_w_in, m_q_lat_norm_g, m_w_uq, m_kv_lat_norm_g, m_w_ukv, m_mla_q_norm_g, m_mla_k_norm_g, m_fox_q_norm_g, m_fox_k_norm_g, m_fox_f_bias, m_s5_lambda_re, m_s5_lambda_im, m_s5_b_re, m_s5_b_im, m_s5_c_re, m_s5_c_im, m_s5_d, m_s5_log_step, m_s5_w_glu, m_s5_b_glu, m_w_branch, m_w_out, m_ffn_norm_g, m_w_up, m_ffn_conv_w, m_w_down, v_attn_norm_g, v_w_in, v_q_lat_norm_g, v_w_uq, v_kv_lat_norm_g, v_w_ukv, v_mla_q_norm_g, v_mla_k_norm_g, v_fox_q_norm_g, v_fox_k_norm_g, v_fox_f_bias, v_s5_lambda_re, v_s5_lambda_im, v_s5_b_re, v_s5_b_im, v_s5_c_re, v_s5_c_im, v_s5_d, v_s5_log_step, v_s5_w_glu, v_s5_b_glu, v_w_branch, v_w_out, v_ffn_norm_g, v_w_up, v_ffn_conv_w, v_w_down):
    args = locals()
    w = {n: args[n] for n in ALL_NAMES}
    m = {n: args["m_" + n] for n in ALL_NAMES}
    v = {n: args["v_" + n] for n in ALL_NAMES}
    depth = w_in.shape[0]
    bsz, seq, _ = x.shape
    t = bsz * seq
    consts = _constants()
    tabs = _rope_tables(positions, t)
    small = {n: w[n] for n in SMALL_NAMES}

    all_names = list(GATHERED_SHAPES)
    all_shards = [_local_shards(l, w) for l in range(depth)]

    def gather_xfer(l, names):
        sub = {n: all_shards[l][n] for n in names}
        order, plan = _gather_plan(sub)
        return _Gather2([sub[n] for n in order], [(GATHERED_SHAPES[n], sub[n].dtype) for n in order], plan)

    def scatter_xfer(g, names):
        shapes = {n: _local_shards_shape(n) for n in names}
        return _Xfer([g[n] for n in names], [((N_DEV,) + shapes[n], g[n].dtype) for n in names],
                     _scatter_plan(list(names), shapes))

    xs = x.reshape(t, D_MODEL)
    f_prev = None
    saved, params = [], []
    gathered = dict(zip(MIXER_WEIGHTS, _exchange("gather_weights", gather_xfer(0, MIXER_WEIGHTS))))
    for l in range(depth):
        p = _prep_layer(l, gathered, small, consts)
        riders = {host: (gather_xfer(l, names), names) for host, names in FWD_RIDERS_OWN.items()}
        if l + 1 < depth:
            riders.update({host: (gather_xfer(l + 1, names), names) for host, names in FWD_RIDERS_NEXT.items()})
        xs, f_prev, sv, gathered = _layer_fwd(xs, f_prev, p, tabs, consts, bsz, riders)
        saved.append(sv)
        params.append(p)

    loss_part, dy = _loss_call(xs, f_prev, loss_target.reshape(t, D_MODEL))
    loss = lax.psum(loss_part[0, 0], ("x", "y", "c"))

    small_grads = {n: [None] * depth for n in SMALL_NAMES}
    big = {n: None for n in BIG_NAMES}

    def finish(l, lands):
        for n in all_names:
            wn = WEIGHT_OF[n]
            if n == "w_up":
                gsum = jnp.concatenate([_sum8(lands["w_up:%d" % h], "sum_w_up") for h in range(2)], axis=1)
            else:
                ld = lands[n]
                gsum = _sum8(ld.reshape(N_DEV, -1, ld.shape[-1]), "sum_" + n).reshape(_local_shards_shape(n))
            gsum = _unshard_layout(n, gsum)
            if n in UPDATED_TRANSPOSED:
                gsum = gsum.T
            c = gsum.shape[-1]
            three = lambda a: (jnp.swapaxes(a, 1, 2) if n in UPDATED_TRANSPOSED else a).reshape(depth, -1, c)
            big[wn] = _adamw_layer(three(w[wn]), gsum.reshape(-1, c), three(m[wn]), three(v[wn]), l, big[wn],
                                   "adamw_" + n)

    def scatter(grads):
        return scatter_xfer(grads, list(grads))

    dx = dy
    carry, lands_above = {}, None
    for l in reversed(range(depth)):
        dx, g, lands, carry_lands = _layer_bwd(dx, params[l], saved[l], tabs, consts, bsz, scatter, carry, l == 0)
        if lands_above is not None:
            finish(l + 1, {**lands_above, **carry_lands})
        carry, lands_above = {"w_in": g["w_in"]}, lands
        small_grads["attn_norm_g"][l] = g["attn_g"].reshape(D_MODEL)
        small_grads["ffn_norm_g"][l] = g["ffn_g"].reshape(D_MODEL)
        for n in SMALL_NAMES:
            if n not in ("attn_norm_g", "ffn_norm_g"):
                small_grads[n][l] = g[n]
    finish(0, lands_above)

    sg = [jnp.stack(small_grads[n]).reshape(w[n].shape) for n in SMALL_NAMES]
    packed = _pack(sg)
    land = _exchange("gather_small_grads", _Gather2([packed], [((N_DEV,) + packed.shape, F32)],
                                                    _scatter_plan(["small"], {})))[0]
    gs = _sum8(land, "sum_small")
    one = lambda names_of: _pack([names_of[n] for n in SMALL_NAMES])[None]
    small_res = _adamw_layer(one(w), gs, one(m), one(v), 0, None, "adamw_small")
    like = [w[n] for n in SMALL_NAMES]
    small_out = [dict(zip(SMALL_NAMES, _unpack(a[0], like))) for a in small_res]

    def out_of(kind, n):
        if n in SMALL_NAMES:
            return small_out[kind][n]
        if n in UPDATED_TRANSPOSED:
            return jnp.swapaxes(big[n][kind], 1, 2)
        return big[n][kind].reshape(w[n].shape)

    outs = [loss, dx.reshape(bsz, seq, D_MODEL)]
    for kind in range(4):
        outs += [out_of(kind, n) for n in ALL_NAMES]
    return tuple(outs)


def _local_shards_shape(nm):
    return {"w_in": (D_MODEL // N_DEV, PW), "w_uq": (MLA_Q_RANK // N_DEV, N_HEADS * MLA_QK_DIM),
            "w_ukv": (MLA_KV_RANK // N_DEV, QKW), "w_glu": (S5_WIDTH // N_DEV, S5_WIDTH),
            "w_branch": (3, S5_WIDTH, LANES), "w_out": (D_MODEL // N_DEV, D_MODEL), "w_up": (D_MODEL, 2 * FF_BLK),
            "conv": (3, 2 * FF_BLK), "w_down": (FF_BLK, D_MODEL), "w_up:0": (D_MODEL, FF_BLK),
            "w_up:1": (D_MODEL, FF_BLK)}[nm]
```

```python
import functools
import math

import numpy as np
import jax
import jax.numpy as jnp
from jax import lax
from jax.experimental import pallas as pl
from jax.experimental.pallas import tpu as pltpu

F32, BF16 = jnp.float32, jnp.bfloat16

D_MODEL = 1024
N_DEV = 8
MLA_Q_RANK, MLA_KV_RANK, MLA_ROPE_DIM, MLA_NOPE_DIM, MLA_QK_DIM = 384, 256, 32, 64, 96
N_HEADS, HEAD_V = 4, 64
S5_GROUPS, S5_GROUP_CH, S5_STATE, S5_WIDTH = 16, 16, 64, 256
S5_N = S5_GROUPS * S5_STATE
D_FF = 2816
D_IN = 4772
ROPE_THETA = 10000.0
NORM_EPS = 1e-6
NEG_INF = -1e30
ADAM_LR, ADAM_B1, ADAM_B2, ADAM_EPS, ADAM_WD, ADAM_STEP = 0.001, 0.9, 0.999, 1e-08, 0.01, 10

VMEM_LIMIT_BYTES = 56 * 1024 * 1024
LANES = 128
TM = 512
TM_H = 256
TQ = 512
QF = 128
QC = 512
HP = 128
QKW = N_HEADS * HP

PW = 5120
PA = 2048
IN_SEGS = ((0, 384, 256),
           (256, 672, 256),
           (512, 928, 256),
           (768, 1184, 256),
           (1024, 1444, 256),
           (1280, 640, 32),
           (1408, 1440, 4),
           (1536, 0, 384),
           (2048, 1700, 3072))
O_CKV, O_FQ, O_FK, O_FV, O_U, O_KR, O_FL, O_CQ = 0, 256, 512, 768, 1024, 1280, 1408, 1536

FF_BLK = 384
FF_HALF = D_FF // 8
FF_PW = 8 * FF_BLK
UP_W = 2 * FF_PW

NN = (((1,), (0,)), ((), ()))
NT = (((1,), (1,)), ((), ()))
TN = (((0,), (0,)), ((), ()))


def _pcall(body, *, name, grid, in_specs, out_specs, out_shape, scratch_shapes=(), rider=None, aliases=None):
    params = pltpu.CompilerParams(dimension_semantics=("arbitrary",) * len(grid), vmem_limit_bytes=VMEM_LIMIT_BYTES)
    if rider is None:
        return pl.pallas_call(body, name=name, grid=grid, in_specs=in_specs, out_specs=out_specs, out_shape=out_shape,
                              scratch_shapes=list(scratch_shapes), input_output_aliases=aliases or {},
                              compiler_params=params)
    assert not aliases
    in_specs, out_specs, out_shape = list(in_specs), list(out_specs), list(out_shape)
    n_in, n_out, n_scr, nx = len(in_specs), len(out_specs), len(scratch_shapes), rider.n
    any_spec = pl.BlockSpec(memory_space=pl.ANY)

    def at_step(last):
        conds = [pl.program_id(a) == (g - 1 if last else 0) for a, g in enumerate(grid)]
        return functools.reduce(jnp.logical_and, conds)

    def body_with_rider(*refs):
        ins, xsrc = refs[:n_in], refs[n_in:n_in + nx]
        outs, xout = refs[n_in + nx:n_in + nx + n_out], refs[n_in + nx + n_out:n_in + 2 * nx + n_out]
        scr = refs[n_in + 2 * nx + n_out:n_in + 2 * nx + n_out + n_scr]
        sems = refs[n_in + 2 * nx + n_out + n_scr:]

        @pl.when(at_step(False))
        def _():
            rider.start(rider.copies(xsrc, xout, *sems))

        body(*ins, *outs, *scr)

        @pl.when(at_step(True))
        def _():
            rider.wait(rider.copies(xsrc, xout, *sems))

    call = pl.pallas_call(body_with_rider, name=name, grid=grid, in_specs=in_specs + [any_spec] * nx,
                          out_specs=out_specs + [any_spec] * nx,
                          out_shape=out_shape + [jax.ShapeDtypeStruct(s, d) for s, d in rider.out_shapes],
                          scratch_shapes=list(scratch_shapes) + rider.sems(), compiler_params=params)
    return lambda *args: call(*args, *rider.srcs)


def _tile_call(fn, name, grid, ins, outs, rider=None):
    n_in = len(ins)

    def body(*refs):
        res = fn(*[r[...] for r in refs[:n_in]])
        if not isinstance(res, (tuple, list)):
            res = (res,)
        assert len(res) == len(outs), (name, len(res), len(outs))
        for r, v, o in zip(refs[n_in:], res, outs):
            v = v.astype(r.dtype)
            if o[4] is None:
                r[...] = v
            else:
                first = functools.reduce(jnp.logical_and, [pl.program_id(a) == 0 for a in o[4]])

                @pl.when(first)
                def _():
                    r[...] = v

                @pl.when(jnp.logical_not(first))
                def _():
                    r[...] += v

    res = _pcall(body, name=name, grid=grid,
                 in_specs=[pl.BlockSpec(b, m) for _, b, m in ins],
                 out_specs=[pl.BlockSpec(o[2], o[3]) for o in outs],
                 out_shape=[jax.ShapeDtypeStruct(o[0], o[1]) for o in outs], rider=rider)(*[a for a, _, _ in ins])
    return res


def _row(a, width=None, col_block=0, tm=TM):
    w = a.shape[1] if width is None else width
    return (a, (tm, w), lambda i, c=col_block: (i, c))


def _par(a):
    nd = a.ndim
    return (a, a.shape, lambda i, nd=nd: (0,) * nd)


def _orow(t, w, dtype=F32, tm=TM):
    return ((t, w), dtype, (tm, w), lambda i: (i, 0), None)


def _row_h(a, width=None, col_block=0):
    return _row(a, width, col_block, tm=TM_H)


def _orow_h(t, w, dtype=F32):
    return _orow(t, w, dtype, tm=TM_H)


def _oacc(shape):
    nd = len(shape)
    return (tuple(shape), F32, tuple(shape), lambda i, nd=nd: (0,) * nd, (0,))


def _pick(n, target):
    best = None
    for d in range(LANES, min(n, target) + 1, LANES):
        if n % d == 0:
            best = d
    return n if best is None else best


MM_VMEM_BUDGET = 44 * 1024 * 1024


def _mm_tiles(m, n, k, sa, sb, so, tm, tn, tk):
    for cm, cn in ((1024, 2048), (1024, 1024), (512, 2048), (512, 1024), (512, 512), (256, 1024), (256, 512)):
        pm, pn = _pick(m, cm), _pick(n, cn)
        if 2 * (pm * k * sa + k * pn * sb + pm * pn * so) <= MM_VMEM_BUDGET:
            return pm, pn, k
    return _pick(m, tm), _pick(n, tn), _pick(k, tk)


def _mm(a, b, mode, name, out_dtype=F32, tm=1024, tn=2048, tk=1024, rider=None):
    if mode == "nn":
        (m, k), (_, n) = a.shape, b.shape
    elif mode == "nt":
        (m, k), (n, _) = a.shape, b.shape
    else:
        (k, m), (_, n) = a.shape, b.shape
    tm, tn, tk = _mm_tiles(m, n, k, a.dtype.itemsize, b.dtype.itemsize, jnp.dtype(out_dtype).itemsize, tm, tn, tk)
    nk = k // tk
    dims = {"nn": NN, "nt": NT, "tn": TN}[mode]
    a_spec = (pl.BlockSpec((tk, tm), lambda i, j, l: (l, i)) if mode == "tn"
              else pl.BlockSpec((tm, tk), lambda i, j, l: (i, l)))
    b_spec = (pl.BlockSpec((tn, tk), lambda i, j, l: (j, l)) if mode == "nt"
              else pl.BlockSpec((tk, tn), lambda i, j, l: (l, j)))

    def body(a_ref, b_ref, o_ref, acc_ref):
        part = lax.dot_general(a_ref[...].astype(BF16), b_ref[...].astype(BF16), dims, preferred_element_type=F32)
        if nk == 1:
            o_ref[...] = part.astype(o_ref.dtype)
        else:
            l = pl.program_id(2)

            @pl.when(l == 0)
            def _():
                acc_ref[...] = part

            @pl.when(l > 0)
            def _():
                acc_ref[...] += part

            @pl.when(l == nk - 1)
            def _():
                o_ref[...] = acc_ref[...].astype(o_ref.dtype)

    res = _pcall(body, name=name, grid=(m // tm, n // tn, nk), in_specs=[a_spec, b_spec],
                 out_specs=[pl.BlockSpec((tm, tn), lambda i, j, l: (i, j))],
                 out_shape=[jax.ShapeDtypeStruct((m, n), out_dtype)],
                 scratch_shapes=[pltpu.VMEM((tm, tn) if nk > 1 else (8, LANES), F32)], rider=rider)(a, b)
    return res[0] if rider is None else res


def _dot(x, w, dims):
    return lax.dot_general(x.astype(BF16), w.astype(BF16), dims, preferred_element_type=F32)


@jax.custom_vjp
def bdot(x, w):
    return _dot(x, w, NN)


def _bdot_fwd(x, w):
    return _dot(x, w, NN), (x, w)


def _bdot_bwd(res, g):
    x, w = res
    return _dot(g, w, NT), _dot(x, g, TN)


bdot.defvjp(_bdot_fwd, _bdot_bwd)


def _split3(x):
    x1 = x.astype(BF16)
    r1 = x - x1.astype(F32)
    x2 = r1.astype(BF16)
    x3 = (r1 - x2.astype(F32)).astype(BF16)
    return x1, x2, x3


def _cdot(x, m, dims):
    return sum(lax.dot_general(p, m, dims, preferred_element_type=F32) for p in _split3(x))


def _block_dot(x, m, transposed):
    k_in = m.shape[1] if transposed else m.shape[0]
    dims = NT if transposed else NN
    return jnp.concatenate([_cdot(x[:, b * k_in:(b + 1) * k_in], m, dims) for b in range(x.shape[1] // k_in)], axis=1)


@jax.custom_vjp
def cdot(x, m):
    return _block_dot(x, m, False)


def _cdot_fwd(x, m):
    return _block_dot(x, m, False), m


def _cdot_bwd(m, g):
    return _block_dot(g, m, True), jnp.zeros_like(m)


cdot.defvjp(_cdot_fwd, _cdot_bwd)


@jax.custom_vjp
def tile_heads(y):
    return jnp.concatenate([y] * N_HEADS, axis=1)


def _tile_heads_fwd(y):
    return jnp.concatenate([y] * N_HEADS, axis=1), None


def _tile_heads_bwd(_, g):
    return (sum(g[:, h * HP:(h + 1) * HP] for h in range(N_HEADS)),)


tile_heads.defvjp(_tile_heads_fwd, _tile_heads_bwd)


def _shift_rows_impl(x, s, reverse):
    n = x.shape[0]
    idx = lax.broadcasted_iota(jnp.int32, x.shape, 0)
    if reverse:
        return jnp.where(idx < n - s, pltpu.roll(x, n - s, 0), 0.0)
    return jnp.where(idx >= s, pltpu.roll(x, s, 0), 0.0)


@functools.partial(jax.custom_vjp, nondiff_argnums=(1,))
def shift_rows(x, s):
    return _shift_rows_impl(x, s, False)


def _shift_rows_fwd(x, s):
    return _shift_rows_impl(x, s, False), None


def _shift_rows_bwd(s, _, g):
    return (_shift_rows_impl(g, s, True),)


shift_rows.defvjp(_shift_rows_fwd, _shift_rows_bwd)


def _rms(x, g):
    return x * lax.rsqrt(jnp.mean(x * x, axis=-1, keepdims=True) + NORM_EPS) * g


def _head_rms(x, e_mat, inv_n, g):
    ms = cdot(x * x, e_mat) * inv_n
    return x * lax.rsqrt(ms + NORM_EPS) * g


def _rope(y, p_mat, cos, sin):
    return y * cos + cdot(y, p_mat) * sin


def _constants():
    e = np.ones((HP, HP), np.float32)
    p = np.zeros((HP, HP), np.float32)
    r = np.zeros((LANES, HP), np.float32)
    pl64 = np.zeros((2 * HEAD_V, 2 * HP), np.float32)
    half = MLA_ROPE_DIM // 2
    for i in range(half):
        x1, x2 = MLA_NOPE_DIM + i, MLA_NOPE_DIM + half + i
        p[x2, x1] = -1.0
        p[x1, x2] = 1.0
    for i in range(MLA_ROPE_DIM):
        r[i, MLA_NOPE_DIM + i] = 1.0
    for h in range(2):
        for i in range(HEAD_V):
            pl64[h * HEAD_V + i, h * HP + i] = 1.0
    return tuple(jnp.asarray(a, BF16) for a in (e, p, r, pl64))


def _inv_freq_row():
    inv = ROPE_THETA ** (-jnp.arange(0, MLA_ROPE_DIM, 2, dtype=F32) / MLA_ROPE_DIM)
    head = jnp.concatenate([jnp.zeros((MLA_NOPE_DIM,), F32), inv, inv, jnp.zeros((HP - MLA_QK_DIM,), F32)])
    return jnp.tile(head, N_HEADS).reshape(1, QKW)


def _mixer_pre(c_q, c_kv, kr, fq, fk, qlg, w_uq, kvlg, w_uk, w_uv, mqg, mkg, fqg, fkg, cos, sin, e_mat, p_mat, r_mat, pl_mat):
    q = bdot(_rms(c_q, qlg), w_uq)
    ckvn = _rms(c_kv, kvlg)
    k = bdot(ckvn, w_uk) + tile_heads(cdot(kr, r_mat))
    v = bdot(ckvn, w_uv)
    q = _rope(_head_rms(q, e_mat, 1.0 / MLA_QK_DIM, mqg), p_mat, cos, sin)
    k = _rope(_head_rms(k, e_mat, 1.0 / MLA_QK_DIM, mkg), p_mat, cos, sin)
    qf = _head_rms(cdot(fq, pl_mat), e_mat, 1.0 / HEAD_V, fqg)
    kf = _head_rms(cdot(fk, pl_mat), e_mat, 1.0 / HEAD_V, fkg)
    return q, k, v, qf, kf


def _s5_post(state, u, c_blk, d_row, w_glu, b_glu):
    y = bdot(state, c_blk) + d_row * u
    y = jax.nn.gelu(y)
    return y * jax.nn.sigmoid(bdot(y, w_glu) + b_glu)


def _merge(o_a, o_b, o_c, g_a, g_b, g_c, w_a, w_b, w_c):
    return (jax.nn.sigmoid(g_a) * bdot(o_a, w_a) + jax.nn.sigmoid(g_b) * bdot(o_b, w_b)
            + jax.nn.sigmoid(g_c) * bdot(o_c, w_c))


def _ffn_mid(gate, val, wg0, wg1, wg2, wv0, wv1, wv2):
    cg = wg0 * shift_rows(gate, 2) + wg1 * shift_rows(gate, 1) + wg2 * gate
    cv = wv0 * shift_rows(val, 2) + wv1 * shift_rows(val, 1) + wv2 * val
    return jax.nn.silu(cg) * cv


def _s5_params(lam_re, lam_im, log_step, b_re, b_im):
    step = jnp.exp(log_step)
    zr, zi = lam_re * step, lam_im * step
    mag = jnp.exp(zr)
    lr, li = mag * jnp.cos(zi), mag * jnp.sin(zi)
    nr, ni = lr - 1.0, li
    den = lam_re * lam_re + lam_im * lam_im
    cr = (nr * lam_re + ni * lam_im) / den
    ci = (ni * lam_re - nr * lam_im) / den
    return lr, li, cr * b_re - ci * b_im, cr * b_im + ci * b_re


def _adamw(w, g, m, v):
    m = ADAM_B1 * m + (1.0 - ADAM_B1) * g
    v = ADAM_B2 * v + (1.0 - ADAM_B2) * (g * g)
    m_hat = m / (1.0 - ADAM_B1 ** ADAM_STEP)
    v_hat = v / (1.0 - ADAM_B2 ** ADAM_STEP)
    delta = -ADAM_LR * (m_hat / (jnp.sqrt(v_hat) + ADAM_EPS) + ADAM_WD * w)
    return delta, m, v


SCAN_W = 256


def _scan_seq(xr, xi, ar, ai, reverse):
    n = xr.shape[0]
    s = 1
    while s < n:
        sr, si = _shift_rows_impl(xr, s, reverse), _shift_rows_impl(xi, s, reverse)
        xr, xi = xr + ar * sr - ai * si, xi + ar * si + ai * sr
        ar, ai = ar * ar - ai * ai, 2.0 * ar * ai
        s *= 2
    return xr, xi


def _scan_fwd(bu, lbar, bsz, name, rider=None):
    t = bu.shape[0]
    seq = t // bsz
    nb = bu.shape[1] // (2 * SCAN_W)

    def fn(b, a):
        xr, xi = _scan_seq(b[:, :SCAN_W], b[:, SCAN_W:], a[:, :SCAN_W], a[:, SCAN_W:], False)
        return jnp.concatenate([xr, xi], axis=1)

    blk = (seq, 2 * SCAN_W)
    return _tile_call(fn, name, (bsz, nb),
                      [(bu, blk, lambda i, j: (i, j)), (lbar, (1, 2 * SCAN_W), lambda i, j: (0, j))],
                      [((t, bu.shape[1]), F32, blk, lambda i, j: (i, j), None)], rider=rider)


def _scan_bwd(dstate, state, lbar, bsz, name, rider=None):
    t = state.shape[0]
    seq = t // bsz
    nb = state.shape[1] // (2 * SCAN_W)

    def fn(g, x, a):
        ar, ai = a[:, :SCAN_W], a[:, SCAN_W:]
        gr, gi = _scan_seq(g[:, :SCAN_W], g[:, SCAN_W:], ar, -ai, True)
        pr, pi = _shift_rows_impl(x[:, :SCAN_W], 1, False), _shift_rows_impl(x[:, SCAN_W:], 1, False)
        dar = jnp.sum(gr * pr + gi * pi, axis=0, keepdims=True)
        dai = jnp.sum(gi * pr - gr * pi, axis=0, keepdims=True)
        return jnp.concatenate([gr, gi], axis=1), jnp.concatenate([dar, dai], axis=1)

    blk = (seq, 2 * SCAN_W)
    return _tile_call(fn, name, (nb, bsz),
                      [(dstate, blk, lambda j, i: (i, j)), (state, blk, lambda j, i: (i, j)),
                       (lbar, (1, 2 * SCAN_W), lambda j, i: (0, j))],
                      [((t, state.shape[1]), F32, blk, lambda j, i: (i, j), None),
                       ((1, state.shape[1]), F32, (1, 2 * SCAN_W), lambda j, i: (0, j), (1,))], rider=rider)


def _shift_lanes(x, s, reverse):
    n = x.shape[1]
    idx = lax.broadcasted_iota(jnp.int32, x.shape, 1)
    if reverse:
        return jnp.where(idx < n - s, pltpu.roll(x, n - s, 1), 0.0)
    return jnp.where(idx >= s, pltpu.roll(x, s, 1), 0.0)


def _cumsum_lanes(x, reverse):
    s = 1
    while s < x.shape[1]:
        x = x + _shift_lanes(x, s, reverse)
        s *= 2
    return x


def _forget_fwd(z, bias, name):
    def fn(zv, bv):
        x = zv + bv
        logf = jnp.minimum(x, 0.0) - jnp.log(1.0 + jnp.exp(-jnp.abs(x)))
        return _cumsum_lanes(logf, False)

    return _tile_call(fn, name, (1,), [_par(z), _par(bias)],
                      [(z.shape, F32, z.shape, lambda i: (0, 0), None)])[0]


def _forget_bwd(dc, z, bias, name):
    def fn(dcv, zv, bv):
        x = zv + bv
        dlogf = _cumsum_lanes(dcv, True)
        dz = dlogf * jax.nn.sigmoid(-x)
        return dz, jnp.sum(dz, axis=1, keepdims=True)

    rows = z.shape[0]
    return _tile_call(fn, name, (1,), [_par(dc), _par(z), _par(bias)],
                      [(z.shape, F32, z.shape, lambda i: (0, 0), None),
                       ((rows, 1), F32, (rows, 1), lambda i: (0, 0), None)])


def _col(v2, x):
    idx = lax.broadcasted_iota(jnp.int32, v2.shape, 1)
    return jnp.sum(jnp.where(idx == x, v2, 0.0), axis=1, keepdims=True)


def _two_cols(c0, c1):
    idx = lax.broadcasted_iota(jnp.int32, (c0.shape[0], 2), 1)
    return jnp.where(idx == 0, c0, c1)


def _rows2(r0, r1):
    idx = lax.broadcasted_iota(jnp.int32, (2, r0.shape[1]), 0)
    return jnp.where(idx == 0, r0, r1)


def _attn_specs(seq, v_blk0, has_bias):
    qblk = pl.BlockSpec((None, TQ, 2 * HP), lambda b, p, i: (b, i, p))
    kblk = pl.BlockSpec((None, seq, 2 * HP), lambda b, p, i: (b, 0, p))
    vblk = pl.BlockSpec((None, seq, LANES), lambda b, p, i: (b, 0, v_blk0 + p))
    oblk = pl.BlockSpec((None, TQ, LANES), lambda b, p, i: (b, i, p))
    rowblk = pl.BlockSpec((None, None, 2, TQ), lambda b, p, i: (b, p, 0, i))
    colblk = pl.BlockSpec((None, None, seq, 2), lambda b, p, i: (b, p, 0, 0))
    return qblk, kblk, vblk, oblk, rowblk, colblk


def _attn_fwd(q, k, v, v_blk0, c_cols, c_rows, scale, name, rider=None):
    bsz, seq, _ = q.shape
    nq = seq // TQ
    has_bias = c_cols is not None
    n_in = 5 if has_bias else 3
    nc = TQ // QF

    def body(*refs):
        q_ref, k_ref, v_ref = refs[:3]
        cq_ref, ck_ref = refs[3:5] if has_bias else (None, None)
        o_ref, lse_ref = refs[n_in:]
        qi = pl.program_id(2)
        qb = [[q_ref[c * QF:(c + 1) * QF, x * HP:(x + 1) * HP].astype(BF16) for c in range(nc)] for x in range(2)]
        key_in = lax.broadcasted_iota(jnp.int32, (TQ, QF), 0)
        qry_in = lax.broadcasted_iota(jnp.int32, (TQ, QF), 1)

        def block(j, carry, masked):
            ks = pl.multiple_of(j * TQ, TQ)
            vt = v_ref[pl.ds(ks, TQ), :].T.astype(BF16)
            new = []
            for x in range(2):
                kx = k_ref[pl.ds(ks, TQ), x * HP:(x + 1) * HP].astype(BF16)
                ckx = _col(ck_ref[pl.ds(ks, TQ), :], x) if has_bias else None
                for c in range(nc):
                    m, l, acc = carry[x * nc + c]
                    s = lax.dot_general(kx, qb[x][c], NT, preferred_element_type=F32) * scale
                    if has_bias:
                        s = s + (cq_ref[x:x + 1, c * QF:(c + 1) * QF] - ckx)
                    if masked:
                        s = jnp.where(qry_in + c * QF >= key_in, s, NEG_INF)
                    m_new = jnp.maximum(m, jnp.max(s, axis=0, keepdims=True))
                    alpha = jnp.exp(m - m_new)
                    p = jnp.exp(s - m_new)
                    l = alpha * l + jnp.sum(p, axis=0, keepdims=True)
                    acc = alpha * acc + lax.dot_general(vt, p.astype(BF16), NN, preferred_element_type=F32)
                    new.append((m_new, l, acc))
            return tuple(new)

        init = tuple((jnp.full((1, QF), NEG_INF, F32), jnp.zeros((1, QF), F32), jnp.zeros((LANES, QF), F32))
                     for _ in range(2 * nc))
        carry = lax.fori_loop(0, qi, lambda j, cr: block(j, cr, False), init)
        carry = block(qi, carry, True)
        lane = lax.broadcasted_iota(jnp.int32, (QF, LANES), 1)
        lse_rows = []
        for x in range(2):
            lse_rows.append(jnp.concatenate([carry[x * nc + c][0] + jnp.log(carry[x * nc + c][1]) for c in range(nc)],
                                            axis=1))
        for c in range(nc):
            (_, l0, a0), (_, l1, a1) = carry[c], carry[nc + c]
            o_ref[c * QF:(c + 1) * QF, :] = jnp.where(lane < HEAD_V, (a0 / l0).T, (a1 / l1).T)
        lse_ref[...] = _rows2(lse_rows[0], lse_rows[1])

    qblk, kblk, vblk, oblk, rowblk, colblk = _attn_specs(seq, v_blk0, has_bias)
    in_specs, args = [qblk, kblk, vblk], [q, k, v]
    if has_bias:
        in_specs += [rowblk, colblk]
        args += [c_rows, c_cols]
    return _pcall(body, name=name, grid=(bsz, 2, nq), in_specs=in_specs, out_specs=[oblk, rowblk],
                  out_shape=[jax.ShapeDtypeStruct((bsz, seq, 2 * LANES), F32),
                             jax.ShapeDtypeStruct((bsz, 2, 2, seq), F32)], rider=rider)(*args)


def _attn_bwd(q, k, v, v_blk0, o, do, lse, c_cols, c_rows, scale, name, rider=None):
    bsz, seq, _ = q.shape
    nq = seq // TQ
    has_bias = c_cols is not None
    n_in = 8 if has_bias else 6
    nc = TQ // QC

    def body(*refs):
        q_ref, k_ref, v_ref, o_ref, do_ref, lse_ref = refs[:6]
        cq_ref, ck_ref = refs[6:8] if has_bias else (None, None)
        dq_ref, dk_ref, dv_ref = refs[n_in:n_in + 3]
        dck_ref, dcq_ref = refs[n_in + 3:n_in + 5] if has_bias else (None, None)
        qi = pl.program_id(2)

        @pl.when(qi == 0)
        def _():
            dk_ref[...] = jnp.zeros_like(dk_ref)
            dv_ref[...] = jnp.zeros_like(dv_ref)
            if has_bias:
                dck_ref[...] = jnp.zeros_like(dck_ref)

        lane = lax.broadcasted_iota(jnp.int32, (QC, LANES), 1)
        ones8 = jnp.ones((8, LANES), BF16)
        qb, dob, delta, lse_r = [], [], [], []
        for x in range(2):
            hm = jnp.logical_and(lane >= x * HEAD_V, lane < (x + 1) * HEAD_V)
            qb.append([]), dob.append([]), delta.append([]), lse_r.append([])
            for c in range(nc):
                rows = slice(c * QC, (c + 1) * QC)
                do_c = jnp.where(hm, do_ref[rows, :], 0.0)
                prod = do_c * o_ref[rows, :]
                hi = prod.astype(BF16)
                lo = (prod - hi.astype(F32)).astype(BF16)
                d8 = (lax.dot_general(ones8, hi, NT, preferred_element_type=F32)
                      + lax.dot_general(ones8, lo, NT, preferred_element_type=F32))
                delta[x].append(d8[0:1, :])
                dob[x].append(do_c.astype(BF16))
                qb[x].append(q_ref[rows, x * HP:(x + 1) * HP].astype(BF16))
                lse_r[x].append(lse_ref[x:x + 1, rows])
        key_in = lax.broadcasted_iota(jnp.int32, (TQ, QC), 0)
        qry_in = lax.broadcasted_iota(jnp.int32, (TQ, QC), 1)

        def block(j, carry, masked):
            ks = pl.multiple_of(j * TQ, TQ)
            vb = v_ref[pl.ds(ks, TQ), :].astype(BF16)
            new, key_sums = [], []
            for x in range(2):
                k32 = k_ref[pl.ds(ks, TQ), x * HP:(x + 1) * HP]
                kx, kt = k32.astype(BF16), k32.T.astype(BF16)
                ckx = _col(ck_ref[pl.ds(ks, TQ), :], x) if has_bias else None
                dk_acc = jnp.zeros((TQ, HP), F32)
                dv_acc = jnp.zeros((TQ, LANES), F32)
                key_sum = jnp.zeros((TQ, 1), F32)
                for c in range(nc):
                    dqt, dcq = carry[x * nc + c]
                    s = lax.dot_general(kx, qb[x][c], NT, preferred_element_type=F32) * scale
                    if has_bias:
                        s = s + (cq_ref[x:x + 1, c * QC:(c + 1) * QC] - ckx)
                    if masked:
                        s = jnp.where(qry_in + c * QC >= key_in, s, NEG_INF)
                    p = jnp.exp(s - lse_r[x][c])
                    dp = lax.dot_general(vb, dob[x][c], NT, preferred_element_type=F32)
                    ds = p * (dp - delta[x][c])
                    dsb = ds.astype(BF16)
                    dk_acc = dk_acc + lax.dot_general(dsb, qb[x][c], NN, preferred_element_type=F32)
                    dv_acc = dv_acc + lax.dot_general(p.astype(BF16), dob[x][c], NN, preferred_element_type=F32)
                    dqt = dqt + lax.dot_general(kt, dsb, NN, preferred_element_type=F32)
                    if has_bias:
                        dcq = dcq + jnp.sum(ds, axis=0, keepdims=True)
                        key_sum = key_sum + jnp.sum(ds, axis=1, keepdims=True)
                    new.append((dqt, dcq))
                dk_ref[pl.ds(ks, TQ), x * HP:(x + 1) * HP] += dk_acc * scale
                dv_ref[pl.ds(ks, TQ), :] += dv_acc
                key_sums.append(key_sum)
            if has_bias:
                dck_ref[pl.ds(ks, TQ), :] -= _two_cols(key_sums[0], key_sums[1])
            return tuple(new)

        init = tuple((jnp.zeros((HP, QC), F32), jnp.zeros((1, QC), F32)) for _ in range(2 * nc))
        carry = lax.fori_loop(0, qi, lambda j, cr: block(j, cr, False), init)
        carry = block(qi, carry, True)
        for x in range(2):
            for c in range(nc):
                dq_ref[c * QC:(c + 1) * QC, x * HP:(x + 1) * HP] = carry[x * nc + c][0].T * scale
        if has_bias:
            dcq_ref[...] = _rows2(jnp.concatenate([carry[c][1] for c in range(nc)], axis=1),
                                  jnp.concatenate([carry[nc + c][1] for c in range(nc)], axis=1))

    qblk, kblk, vblk, oblk, rowblk, colblk = _attn_specs(seq, v_blk0, has_bias)
    in_specs = [qblk, kblk, vblk, oblk, oblk, rowblk]
    out_specs = [qblk, kblk, pl.BlockSpec((None, seq, LANES), lambda b, p, i: (b, 0, p))]
    out_shape = [jax.ShapeDtypeStruct((bsz, seq, QKW), F32), jax.ShapeDtypeStruct((bsz, seq, QKW), F32),
                 jax.ShapeDtypeStruct((bsz, seq, 2 * LANES), F32)]
    args = [q, k, v, o, do, lse]
    if has_bias:
        in_specs += [rowblk, colblk]
        out_specs += [colblk, rowblk]
        out_shape += [jax.ShapeDtypeStruct((bsz, 2, seq, 2), F32), jax.ShapeDtypeStruct((bsz, 2, 2, seq), F32)]
        args += [c_rows, c_cols]
    return _pcall(body, name=name, grid=(bsz, 2, nq), in_specs=in_specs, out_specs=out_specs,
                  out_shape=out_shape, rider=rider)(*args)


def _coords(idx):
    return (idx // 4, (idx // 2) % 2, idx % 2)


class _Xfer:
    def __init__(self, srcs, out_shapes, plan):
        self.srcs, self.out_shapes, self.plan = list(srcs), list(out_shapes), plan
        self.n = len(self.srcs)
        me0 = jnp.int32(0)
        n_pieces = [len(plan(a, me0, me0)) for a in range(self.n)]
        self.offs = np.concatenate([[0], np.cumsum(n_pieces)]).astype(int)
        self.total = int(self.offs[-1])

    def sems(self):
        remote = (N_DEV - 1) * self.total
        return [pltpu.SemaphoreType.DMA((remote,)), pltpu.SemaphoreType.DMA((remote,)),
                pltpu.SemaphoreType.DMA((self.total,))]

    def copies(self, src_refs, out_refs, send_sems, recv_sems, local_sems):
        me = 4 * lax.axis_index("x") + 2 * lax.axis_index("y") + lax.axis_index("c")
        out = []
        for a in range(self.n):
            for pi, (si, di) in enumerate(self.plan(a, me, me)):
                out.append((pltpu.make_async_copy(src_refs[a].at[si], out_refs[a].at[di],
                                                  local_sems.at[self.offs[a] + pi]), None))
        for kk in range(1, N_DEV):
            dest = (me + kk) % N_DEV
            src_dev = (me + N_DEV - kk) % N_DEV
            for a in range(self.n):
                pieces = self.plan(a, me, dest)
                landing = self.plan(a, src_dev, me)
                for pi, ((si, di), (_, li)) in enumerate(zip(pieces, landing)):
                    sem = (kk - 1) * self.total + self.offs[a] + pi
                    mk = functools.partial(pltpu.make_async_remote_copy, src_ref=src_refs[a].at[si],
                                           send_sem=send_sems.at[sem], recv_sem=recv_sems.at[sem],
                                           device_id=_coords(dest), device_id_type=pl.DeviceIdType.MESH)
                    out.append((mk(dst_ref=out_refs[a].at[di]), mk(dst_ref=out_refs[a].at[li])))
        return out

    @staticmethod
    def start(copies):
        for cp, _ in copies:
            cp.start()

    @staticmethod
    def wait(copies):
        for cp, rc in copies:
            if rc is None:
                cp.wait()
            else:
                cp.wait_send()
                rc.wait_recv()


def _exchange(name, xf):
    n = xf.n

    def body(*refs):
        copies = xf.copies(refs[:n], refs[n:2 * n], *refs[2 * n:])
        xf.start(copies)
        xf.wait(copies)

    any_spec = pl.BlockSpec(memory_space=pl.ANY)
    return pl.pallas_call(body, name=name, in_specs=[any_spec] * n, out_specs=[any_spec] * n,
                          out_shape=[jax.ShapeDtypeStruct(s, d) for s, d in xf.out_shapes],
                          scratch_shapes=xf.sems(),
                          compiler_params=pltpu.CompilerParams(has_side_effects=True))(*xf.srcs)


class _Gather2(_Xfer):
    def copies(self, src_refs, out_refs, send_sems, recv_sems, local_sems):
        x, y, c = lax.axis_index("x"), lax.axis_index("y"), lax.axis_index("c")
        num = lambda px, py, pc: 4 * px + 2 * py + pc
        me, sib = (x, y, c), (x, y, 1 - c)
        chips = [(1 - x, y), (x, 1 - y), (1 - x, 1 - y)]
        local, first, passed, landing = [], [], [], []
        for a in range(self.n):
            mine = self.plan(a, num(*me), num(*me))
            for pi, (si, di) in enumerate(mine):
                piece = self.offs[a] + pi

                def remote(k, src, dst, to, a=a, piece=piece):
                    return pltpu.make_async_remote_copy(
                        src_ref=src, dst_ref=dst, send_sem=send_sems.at[k * self.total + piece],
                        recv_sem=recv_sems.at[k * self.total + piece], device_id=to, device_id_type=pl.DeviceIdType.MESH)

                def slot(dev, a=a, pi=pi):
                    return out_refs[a].at[self.plan(a, num(*dev), num(*dev))[pi][1]]

                local.append(pltpu.make_async_copy(src_refs[a].at[si], out_refs[a].at[di], local_sems.at[piece]))
                first.append(remote(0, src_refs[a].at[si], out_refs[a].at[di], sib))
                landing.append([remote(0, slot(sib), slot(sib), sib)])
                for j, chip in enumerate(chips):
                    first.append(remote(1 + j, src_refs[a].at[si], out_refs[a].at[di], (*chip, c)))
                    passed.append((remote(1 + j, slot((*chip, c)), slot((*chip, c)), (*chip, c)),
                                   remote(4 + j, slot((*chip, c)), slot((*chip, c)), sib)))
                    landing[-1].append(remote(4 + j, slot((*chip, 1 - c)), slot((*chip, 1 - c)), sib))
        return local, first, passed, landing

    @staticmethod
    def start(copies):
        local, first, _, _ = copies
        for cp in local + first:
            cp.start()

    @staticmethod
    def wait(copies):
        local, first, passed, landing = copies
        for arrival, forward in passed:
            arrival.wait_recv()
            forward.start()
        for group in landing:
            for cp in group:
                cp.wait_recv()
        for cp in first + [fw for _, fw in passed]:
            cp.wait_send()
        for cp in local:
            cp.wait()


def _rows_of(j, n):
    return pl.ds(pl.multiple_of(j * n, n), n)


def _ffn_cols(j, half):
    return pl.multiple_of((2 * (j % 4) + half) * (2 * FF_BLK) + (j // 4) * FF_BLK, FF_BLK)


def _gather_plan(shards):
    names = list(shards)

    def plan(a, src_dev, _dest):
        nm = names[a]
        j = src_dev
        if nm in ("w_in", "w_uq", "w_ukv", "w_glu", "w_out", "w_down"):
            rows = shards[nm].shape[0]
            return [((slice(None), slice(None)), (_rows_of(j, rows), slice(None)))]
        if nm == "w_branch":
            return [((slice(None), slice(None), slice(None)), (slice(None), slice(None), _rows_of(j, LANES)))]
        if nm in ("w_up", "conv"):
            return [((slice(None), pl.ds(h * FF_BLK, FF_BLK)), (slice(None), pl.ds(_ffn_cols(j, h), FF_BLK)))
                    for h in range(2)]
        raise KeyError(nm)

    return names, plan


def _scatter_plan(names, shard_shapes):
    def plan(a, src_dev, dest):
        nm = names[a]
        j = dest
        if nm in ("w_in", "w_uq", "w_ukv", "w_glu", "w_out", "w_down"):
            rows = shard_shapes[nm][0]
            return [((_rows_of(j, rows), slice(None)), (src_dev, slice(None), slice(None)))]
        if nm == "w_branch":
            return [((slice(None), slice(None), _rows_of(j, LANES)), (src_dev, slice(None), slice(None), slice(None)))]
        if nm in ("w_up", "conv"):
            return [((slice(None), pl.ds(_ffn_cols(j, h), FF_BLK)), (src_dev, slice(None), pl.ds(h * FF_BLK, FF_BLK)))
                    for h in range(2)]
        if nm in ("w_up:0", "w_up:1"):
            h = int(nm[-1])
            return [((slice(None), pl.ds(_ffn_cols(j, h), FF_BLK)), (src_dev, slice(None), slice(None)))]
        if nm == "small":
            return [((slice(None), slice(None)), (src_dev, slice(None), slice(None)))]
        raise KeyError(nm)

    return plan


def _pad_in_cols(w):
    lead = w.shape[:-1]
    parts, pos = [], 0
    for dst, src, width in IN_SEGS:
        if dst > pos:
            parts.append(jnp.zeros(lead + (dst - pos,), w.dtype))
        parts.append(w[..., src:src + width])
        pos = dst + width
    return jnp.concatenate(parts, axis=-1)


def _unpad_in_cols(w):
    order = sorted(IN_SEGS, key=lambda s: s[1])
    return jnp.concatenate([w[..., dst:dst + width] for dst, _, width in order], axis=-1)


def _pad_ff(w, axis):
    n = w.shape[axis] // FF_HALF
    parts = []
    for h in range(n):
        piece = lax.slice_in_dim(w, h * FF_HALF, (h + 1) * FF_HALF, axis=axis)
        zshape = list(w.shape)
        zshape[axis] = FF_BLK - FF_HALF
        parts += [piece, jnp.zeros(zshape, w.dtype)]
    return jnp.concatenate(parts, axis=axis)


def _unpad_ff(w, axis):
    n = w.shape[axis] // FF_BLK
    return jnp.concatenate([lax.slice_in_dim(w, h * FF_BLK, h * FF_BLK + FF_HALF, axis=axis) for h in range(n)],
                           axis=axis)


def _to_scan_cols(re_part, im_part):
    lead = re_part.shape[:-1]
    nb = S5_N // SCAN_W
    r = re_part.reshape(lead + (nb, SCAN_W))
    i = im_part.reshape(lead + (nb, SCAN_W))
    return jnp.concatenate([r, i], axis=-1).reshape(lead + (2 * S5_N,))


def _from_scan_cols(x):
    lead = x.shape[:-1]
    nb = S5_N // SCAN_W
    y = x.reshape(lead + (nb, 2, SCAN_W))
    return y[..., 0, :].reshape(lead + (S5_N,)), y[..., 1, :].reshape(lead + (S5_N,))


def _block_diag_in(b):
    eye = jnp.eye(S5_GROUPS, dtype=b.dtype)
    return jnp.einsum("gph,gk->ghkp", b, eye).reshape(S5_WIDTH, S5_N)


def _block_diag_in_t(m):
    eye = jnp.eye(S5_GROUPS, dtype=m.dtype)
    return jnp.einsum("ghkp,gk->gph", m.reshape(S5_GROUPS, S5_GROUP_CH, S5_GROUPS, S5_STATE), eye)


def _block_diag_out(c):
    eye = jnp.eye(S5_GROUPS, dtype=c.dtype)
    return jnp.einsum("ghp,gk->gpkh", c, eye).reshape(S5_N, S5_WIDTH)


def _block_diag_out_t(m):
    eye = jnp.eye(S5_GROUPS, dtype=m.dtype)
    return jnp.einsum("gpkh,gk->ghp", m.reshape(S5_GROUPS, S5_STATE, S5_GROUPS, S5_GROUP_CH), eye)


def _pad_heads(g, width):
    g = jnp.broadcast_to(g.reshape(-1, width), (N_HEADS, width))
    return jnp.pad(g, ((0, 0), (0, HP - width))).reshape(1, QKW)


def _prep_layer(l, gw, small, consts):
    e_mat, p_mat, r_mat, pl_mat = consts
    p = {}
    p["attn_g"] = small["attn_norm_g"][l].reshape(1, D_MODEL)
    p["ffn_g"] = small["ffn_norm_g"][l].reshape(1, D_MODEL)
    p["qlg"] = small["q_lat_norm_g"][l].reshape(1, MLA_Q_RANK)
    p["kvlg"] = small["kv_lat_norm_g"][l].reshape(1, MLA_KV_RANK)
    p["mqg"] = _pad_heads(small["mla_q_norm_g"][l], MLA_QK_DIM)
    p["mkg"] = _pad_heads(small["mla_k_norm_g"][l], MLA_QK_DIM)
    p["fqg"] = _pad_heads(small["fox_q_norm_g"][l], HEAD_V)
    p["fkg"] = _pad_heads(small["fox_k_norm_g"][l], HEAD_V)
    w_uq = gw["w_uq"].reshape(MLA_Q_RANK, N_HEADS, MLA_QK_DIM)
    p["w_uq"] = jnp.pad(w_uq, ((0, 0), (0, 0), (0, HP - MLA_QK_DIM))).reshape(MLA_Q_RANK, QKW)
    w_ukv = gw["w_ukv"].reshape(MLA_KV_RANK, N_HEADS, MLA_NOPE_DIM + HEAD_V)
    p["w_uk"] = jnp.pad(w_ukv[..., :MLA_NOPE_DIM], ((0, 0), (0, 0), (0, HP - MLA_NOPE_DIM))).reshape(MLA_KV_RANK, QKW)
    p["w_uv"] = w_ukv[..., MLA_NOPE_DIM:].reshape(MLA_KV_RANK, N_HEADS * HEAD_V)
    p["w_glu"] = gw["w_glu"]
    p["b_glu"] = small["s5_b_glu"][l].reshape(1, S5_WIDTH)
    p["d_row"] = small["s5_d"][l].reshape(1, S5_WIDTH)
    p["lam_re"] = small["s5_lambda_re"][l].reshape(S5_N, 1)
    p["lam_im"] = small["s5_lambda_im"][l].reshape(S5_N, 1)
    p["log_step"] = jnp.repeat(small["s5_log_step"][l], S5_STATE).reshape(S5_N, 1)
    p["b_re"] = small["s5_b_re"][l].reshape(S5_N, S5_GROUP_CH)
    p["b_im"] = small["s5_b_im"][l].reshape(S5_N, S5_GROUP_CH)
    c_re, c_im = small["s5_c_re"][l], small["s5_c_im"][l]
    p["c_blk"] = _to_scan_cols(_block_diag_out(c_re).T, -_block_diag_out(c_im).T).T
    p["fbias"] = small["fox_f_bias"][l]
    for nm in ("w_in", "w_out", "w_branch"):
        p[nm] = gw[nm]
    return p


def _s5_param_call(p, l):
    ins = [p["lam_re"], p["lam_im"], p["log_step"], p["b_re"], p["b_im"]]
    outs = [((S5_N, 1), F32, (S5_N, 1), lambda i: (0, 0), None)] * 2 + \
           [((S5_N, S5_GROUP_CH), F32, (S5_N, S5_GROUP_CH), lambda i: (0, 0), None)] * 2
    return _tile_call(_s5_params, "s5_params", (1,), [_par(a) for a in ins], outs)


def _s5_param_bwd_call(p, cts):
    ins = [p["lam_re"], p["lam_im"], p["log_step"], p["b_re"], p["b_im"]]

    def fn(lr, li, ls, br, bi, g0, g1, g2, g3):
        _, vjp = jax.vjp(_s5_params, lr, li, ls, br, bi)
        return vjp((g0, g1, g2, g3))

    outs = [((S5_N, 1), F32, (S5_N, 1), lambda i: (0, 0), None)] * 3 + \
           [((S5_N, S5_GROUP_CH), F32, (S5_N, S5_GROUP_CH), lambda i: (0, 0), None)] * 2
    return _tile_call(fn, "s5_params_bwd", (1,), [_par(a) for a in ins + list(cts)], outs)


def _layer_fwd(x_prev, f_prev, p, tabs, consts, bsz, riders):
    arrived = {}

    def rider_of(name):
        return riders[name][0] if name in riders else None

    def hosted(name, res, n_out):
        if name in riders:
            arrived.update(zip(riders[name][1], res[n_out:]))
        return res[:n_out]

    t = x_prev.shape[0]
    seq = t // bsz
    nt = t // TM
    cos, sin = tabs
    e_mat, p_mat, r_mat, pl_mat = consts
    sv = {}

    if f_prev is None:
        x = x_prev
        h = _tile_call(lambda xv, g: _rms(xv, g), "norm_first", (nt,), [_row(x), _par(p["attn_g"])],
                       [_orow(t, D_MODEL, BF16)])[0]
    else:
        x, h = _tile_call(lambda xv, fv, g: (xv + fv, _rms(xv + fv, g)), "norm_attn", (nt,),
                          [_row(x_prev), _row(f_prev), _par(p["attn_g"])], [_orow(t, D_MODEL), _orow(t, D_MODEL, BF16)])
    sv["x"], sv["h"] = x, h
    proj = _mm(h, p["w_in"], "nn", "mm_in")
    sv["proj"] = proj

    def pre_fn(pa, cosv, sinv, qlg, w_uq, kvlg, w_uk, w_uv, mqg, mkg, fqg, fkg, em, pm, rm, plm):
        return _mixer_pre(pa[:, O_CQ:O_CQ + MLA_Q_RANK], pa[:, O_CKV:O_CKV + MLA_KV_RANK], pa[:, O_KR:O_KR + LANES],
                          pa[:, O_FQ:O_FQ + 256], pa[:, O_FK:O_FK + 256], qlg, w_uq, kvlg, w_uk, w_uv,
                          mqg, mkg, fqg, fkg, cosv, sinv, em, pm, rm, plm)

    pre_params = [p["qlg"], p["w_uq"], p["kvlg"], p["w_uk"], p["w_uv"], p["mqg"], p["mkg"], p["fqg"], p["fkg"],
                  e_mat, p_mat, r_mat, pl_mat]
    q_m, k_m, v_m, q_f, k_f = hosted("mixer_pre", _tile_call(
        pre_fn, "mixer_pre", (t // TM_H,), [_row_h(proj, PA), _row_h(cos), _row_h(sin)] + [_par(a) for a in pre_params],
        [_orow_h(t, QKW), _orow_h(t, QKW), _orow_h(t, 2 * LANES), _orow_h(t, QKW), _orow_h(t, QKW)],
        rider=rider_of("mixer_pre")), 5)
    sv.update(q_m=q_m, k_m=k_m, v_m=v_m, q_f=q_f, k_f=k_f)

    z = proj[:, O_FL:O_FL + N_HEADS].reshape(bsz, seq, N_HEADS).transpose(0, 2, 1).reshape(bsz * N_HEADS, seq)
    fb = jnp.tile(p["fbias"], bsz).reshape(bsz * N_HEADS, 1)
    cum = _forget_fwd(z, fb, "forget_fwd")
    ck = cum.reshape(bsz, 2, 2, seq)
    cq = ck.transpose(0, 1, 3, 2)
    sv.update(z=z, fb=fb, cq=cq, ck=ck)

    r3 = lambda a: a.reshape(bsz, seq, a.shape[-1])
    o_m, lse_m = hosted("attn_mla_fwd", _attn_fwd(r3(q_m), r3(k_m), r3(v_m), 0, None, None, MLA_QK_DIM ** -0.5,
                                                  "attn_mla_fwd", rider_of("attn_mla_fwd")), 2)
    o_f, lse_f = hosted("attn_fox_fwd", _attn_fwd(r3(q_f), r3(k_f), r3(proj), O_FV // LANES, cq, ck, HEAD_V ** -0.5,
                                                  "attn_fox_fwd", rider_of("attn_fox_fwd")), 2)
    o_m, o_f = o_m.reshape(t, 2 * LANES), o_f.reshape(t, 2 * LANES)
    sv.update(o_m=o_m, o_f=o_f, lse_m=lse_m, lse_f=lse_f)

    lbar_re, lbar_im, bbar_re, bbar_im = _s5_param_call(p, 0)
    lbar = _to_scan_cols(lbar_re.reshape(1, S5_N), lbar_im.reshape(1, S5_N))
    b_blk = _to_scan_cols(_block_diag_in(bbar_re.reshape(S5_GROUPS, S5_STATE, S5_GROUP_CH)),
                          _block_diag_in(bbar_im.reshape(S5_GROUPS, S5_STATE, S5_GROUP_CH)))
    sv.update(lbar=lbar, b_blk=b_blk)
    bu = _tile_call(lambda u, bb: bdot(u, bb), "s5_bu", (nt,), [_row(proj, S5_WIDTH, O_U // S5_WIDTH), _par(b_blk)],
                    [_orow(t, 2 * S5_N)])[0]
    state = hosted("s5_scan_fwd", _scan_fwd(bu, lbar, bsz, "s5_scan_fwd", rider_of("s5_scan_fwd")), 1)[0]
    sv["state"] = state
    post_params = [p["c_blk"], p["d_row"], p["w_glu"], p["b_glu"]]
    o_s = _tile_call(_s5_post, "s5_post", (nt,),
                     [_row(state), _row(proj, S5_WIDTH, O_U // S5_WIDTH)] + [_par(a) for a in post_params],
                     [_orow(t, S5_WIDTH)])[0]
    sv["o_s"] = o_s

    def merge_fn(oa, ob, oc, ga, gb, gc, wb):
        return _merge(oa, ob, oc, ga, gb, gc, wb[0], wb[1], wb[2])

    gate_specs = [_row(proj, D_MODEL, PA // D_MODEL + n) for n in range(3)]
    merged = _tile_call(merge_fn, "merge", (nt,), [_row(o_m), _row(o_f), _row(o_s)] + gate_specs + [_par(p["w_branch"])],
                        [_orow(t, D_MODEL, BF16)])[0]
    sv["merged"] = merged
    attn_out = _mm(merged, p["w_out"], "nn", "mm_out")

    x1, h2 = _tile_call(lambda xv, av, g: (xv + av, _rms(xv + av, g)), "norm_ffn", (nt,),
                        [_row(x), _row(attn_out), _par(p["ffn_g"])], [_orow(t, D_MODEL), _orow(t, D_MODEL, BF16)])
    sv["x1"], sv["h2"] = x1, h2
    p.update({n: arrived.pop(n) for n in FFN_WEIGHTS})
    up = _mm(h2, p["w_up"], "nn", "mm_up")
    sv["up"] = up

    def ffn_fn(upv, cw):
        return _ffn_mid(upv[:, :FF_BLK], upv[:, FF_BLK:], cw[0:1, :FF_BLK], cw[1:2, :FF_BLK], cw[2:3, :FF_BLK],
                        cw[0:1, FF_BLK:], cw[1:2, FF_BLK:], cw[2:3, FF_BLK:])

    nblk = FF_PW // FF_BLK
    act = _tile_call(ffn_fn, "ffn_mid", (nblk, bsz),
                     [(up, (seq, 2 * FF_BLK), lambda j, b: (b, j)), (p["conv"], (3, 2 * FF_BLK), lambda j, b: (0, j))],
                     [((t, FF_PW), BF16, (seq, FF_BLK), lambda j, b: (b, j), None)])[0]
    sv["act"] = act
    ffn_out = _mm(act, p["w_down"], "nn", "mm_down")
    return x1, ffn_out, sv, arrived


def _layer_bwd(dx2, p, sv, tabs, consts, bsz, scatter, carry, last_layer):
    lands, carry_lands = {}, {}

    def riding(host, g):
        names = BWD_RIDERS[host] + (("w_in",) if host == "mm_in_dx" and last_layer else ())
        return scatter({n: g[n.split(":")[0]] for n in names}), names

    def landed(host, res, n_out):
        names = BWD_RIDERS[host] + (("w_in",) if host == "mm_in_dx" and last_layer else ())
        lands.update(zip(names, res[n_out:]))
        return res[:n_out]

    t = dx2.shape[0]
    seq = t // bsz
    nt = t // TM
    cos, sin = tabs
    e_mat, p_mat, r_mat, pl_mat = consts
    g = {}

    g["w_down"] = _mm(sv["act"], dx2, "tn", "mm_down_dw", out_dtype=BF16)
    dact = _mm(dx2, p["w_down"], "nt", "mm_down_dx")

    def ffn_bwd_fn(upv, cw, da):
        args = (upv[:, :FF_BLK], upv[:, FF_BLK:], cw[0:1, :FF_BLK], cw[1:2, :FF_BLK], cw[2:3, :FF_BLK],
                cw[0:1, FF_BLK:], cw[1:2, FF_BLK:], cw[2:3, FF_BLK:])
        _, vjp = jax.vjp(_ffn_mid, *args)
        dg, dv, g0, g1, g2, v0, v1, v2 = vjp(da)
        return (jnp.concatenate([dg, dv], axis=1), jnp.concatenate([g0, v0], axis=1),
                jnp.concatenate([g1, v1], axis=1), jnp.concatenate([g2, v2], axis=1))

    nblk = FF_PW // FF_BLK
    cw_out = ((1, UP_W), F32, (1, 2 * FF_BLK), lambda j, b: (0, j), (1,))
    res = _tile_call(
        ffn_bwd_fn, "ffn_mid_bwd", (nblk, bsz),
        [(sv["up"], (seq, 2 * FF_BLK), lambda j, b: (b, j)), (p["conv"], (3, 2 * FF_BLK), lambda j, b: (0, j)),
         (dact, (seq, FF_BLK), lambda j, b: (b, j))],
        [((t, UP_W), BF16, (seq, 2 * FF_BLK), lambda j, b: (b, j), None), cw_out, cw_out, cw_out],
        rider=scatter(carry) if carry else None)
    dup, dc0, dc1, dc2 = res[:4]
    carry_lands.update(zip(carry, res[4:]))
    g["conv"] = jnp.concatenate([dc0, dc1, dc2], axis=0)
    g["w_up"] = _mm(sv["h2"], dup, "tn", "mm_up_dw", out_dtype=BF16)
    dh2 = landed("mm_up_dx", _mm(dup, p["w_up"], "nt", "mm_up_dx", rider=riding("mm_up_dx", g)[0]), 1)[0]

    def norm_bwd_fn(xv, gv, dh, dres):
        _, vjp = jax.vjp(_rms, xv, gv)
        dxv, dg = vjp(dh)
        return dres + dxv, dg

    d1, g["ffn_g"] = _tile_call(norm_bwd_fn, "norm_bwd", (nt,),
                                [_row(sv["x1"]), _par(p["ffn_g"]), _row(dh2), _row(dx2)],
                                [_orow(t, D_MODEL), _oacc((1, D_MODEL))])

    g["w_out"] = _mm(sv["merged"], d1, "tn", "mm_out_dw", out_dtype=BF16)
    dmerged = _mm(d1, p["w_out"], "nt", "mm_out_dx")

    def merge_bwd_fn(oa, ob, oc, ga, gb, gc, wb, dm):
        wf = wb.astype(F32)
        _, vjp = jax.vjp(_merge, oa, ob, oc, ga, gb, gc, wf[0], wf[1], wf[2])
        doa, dob, doc, dga, dgb, dgc, dwa, dwb, dwc = vjp(dm)
        return doa, dob, doc, jnp.concatenate([dga, dgb, dgc], axis=1), jnp.stack([dwa, dwb, dwc])

    proj = sv["proj"]
    gate_specs = [_row_h(proj, D_MODEL, PA // D_MODEL + n) for n in range(3)]
    do_m, do_f, do_s, dgl, g["w_branch"] = _tile_call(
        merge_bwd_fn, "merge_bwd", (t // TM_H,),
        [_row_h(sv["o_m"]), _row_h(sv["o_f"]), _row_h(sv["o_s"])] + gate_specs + [_par(p["w_branch"]), _row_h(dmerged)],
        [_orow_h(t, 2 * LANES), _orow_h(t, 2 * LANES), _orow_h(t, S5_WIDTH), _orow_h(t, 3 * D_MODEL),
         _oacc((3, S5_WIDTH, D_MODEL))])

    def post_bwd_fn(st, u, cb, dr, wg, bg, do):
        _, vjp = jax.vjp(_s5_post, st, u, cb.astype(F32), dr, wg.astype(F32), bg)
        return vjp(do)

    u_spec = _row(proj, S5_WIDTH, O_U // S5_WIDTH)
    dstate, du1, dc_blk, g_d, g["w_glu"], g_bglu = _tile_call(
        post_bwd_fn, "s5_post_bwd", (t // TM_H,),
        [_row_h(sv["state"]), _row_h(proj, S5_WIDTH, O_U // S5_WIDTH), _par(p["c_blk"]), _par(p["d_row"]),
         _par(p["w_glu"]), _par(p["b_glu"]), _row_h(do_s)],
        [_orow_h(t, 2 * S5_N), _orow_h(t, S5_WIDTH), _oacc((2 * S5_N, S5_WIDTH)), _oacc((1, S5_WIDTH)),
         _oacc((S5_WIDTH, S5_WIDTH)), _oacc((1, S5_WIDTH))])
    dbu, dlbar = landed("s5_scan_bwd", _scan_bwd(dstate, sv["state"], sv["lbar"], bsz, "s5_scan_bwd",
                                                 riding("s5_scan_bwd", g)[0]), 2)

    def bu_bwd_fn(u, bb, dbv, du_in):
        _, vjp = jax.vjp(bdot, u, bb)
        du, dbb = vjp(dbv)
        return du_in + du, dbb

    du, db_blk = _tile_call(bu_bwd_fn, "s5_bu_bwd", (nt,), [u_spec, _par(sv["b_blk"]), _row(dbu), _row(du1)],
                            [_orow(t, S5_WIDTH), _oacc((S5_WIDTH, 2 * S5_N))])
    dlr, dli = _from_scan_cols(dlbar)
    dbr, dbi = _from_scan_cols(db_blk)
    cts = (dlr.reshape(S5_N, 1), dli.reshape(S5_N, 1),
           _block_diag_in_t(dbr).reshape(S5_N, S5_GROUP_CH), _block_diag_in_t(dbi).reshape(S5_N, S5_GROUP_CH))
    g_lr, g_li, g_ls, g_br, g_bi = _s5_param_bwd_call(p, cts)
    dc_re, dc_im = _from_scan_cols(dc_blk.T)
    g["s5_lambda_re"] = g_lr.reshape(S5_GROUPS, S5_STATE)
    g["s5_lambda_im"] = g_li.reshape(S5_GROUPS, S5_STATE)
    g["s5_log_step"] = jnp.sum(g_ls.reshape(S5_GROUPS, S5_STATE), axis=1)
    g["s5_b_re"] = g_br.reshape(S5_GROUPS, S5_STATE, S5_GROUP_CH)
    g["s5_b_im"] = g_bi.reshape(S5_GROUPS, S5_STATE, S5_GROUP_CH)
    g["s5_c_re"] = _block_diag_out_t(dc_re.T)
    g["s5_c_im"] = -_block_diag_out_t(dc_im.T)
    g["s5_d"] = g_d.reshape(S5_GROUPS, S5_GROUP_CH)
    g["s5_b_glu"] = g_bglu.reshape(S5_WIDTH)

    r3 = lambda a: a.reshape(bsz, seq, a.shape[-1])
    dq_m, dk_m, dv_m = landed("attn_mla_bwd", _attn_bwd(
        r3(sv["q_m"]), r3(sv["k_m"]), r3(sv["v_m"]), 0, r3(sv["o_m"]), r3(do_m), sv["lse_m"], None, None,
        MLA_QK_DIM ** -0.5, "attn_mla_bwd", riding("attn_mla_bwd", g)[0]), 3)
    dq_f, dk_f, dv_f, dcq, dck = landed("attn_fox_bwd", _attn_bwd(
        r3(sv["q_f"]), r3(sv["k_f"]), r3(proj), O_FV // LANES, r3(sv["o_f"]), r3(do_f), sv["lse_f"], sv["cq"], sv["ck"],
        HEAD_V ** -0.5, "attn_fox_bwd", riding("attn_fox_bwd", g)[0]), 5)
    dcum = (dck + dcq.transpose(0, 1, 3, 2)).reshape(bsz * N_HEADS, seq)
    dz, dfb = _forget_bwd(dcum, sv["z"], sv["fb"], "forget_bwd")
    g["fox_f_bias"] = jnp.sum(dfb.reshape(bsz, N_HEADS), axis=0)
    dfl = jnp.pad(dz.reshape(bsz, N_HEADS, seq).transpose(0, 2, 1).reshape(t, N_HEADS), ((0, 0), (0, LANES - N_HEADS)))

    def pre_bwd_fn(pa, cosv, sinv, qlg, w_uq, kvlg, w_uk, w_uv, mqg, mkg, fqg, fkg, em, pm, rm, plm,
                   gq, gk, gv, gqf, gkf, gvf, gu, gfl, ggl):
        f = functools.partial(_mixer_pre, cos=cosv, sin=sinv, e_mat=em, p_mat=pm, r_mat=rm, pl_mat=plm)
        prim = (pa[:, O_CQ:O_CQ + MLA_Q_RANK], pa[:, O_CKV:O_CKV + MLA_KV_RANK], pa[:, O_KR:O_KR + LANES],
                pa[:, O_FQ:O_FQ + 256], pa[:, O_FK:O_FK + 256], qlg, w_uq.astype(F32), kvlg, w_uk.astype(F32),
                w_uv.astype(F32), mqg, mkg, fqg, fkg)
        _, vjp = jax.vjp(lambda *a: f(*a), *prim)
        dcq_, dckv, dkr, dfq, dfk, dqlg, dwuq, dkvlg, dwuk, dwuv, dmqg, dmkg, dfqg, dfkg = vjp((gq, gk, gv, gqf, gkf))
        zpad = jnp.zeros((pa.shape[0], PA - O_CQ - MLA_Q_RANK), F32)
        dproj = jnp.concatenate([dckv, dfq, dfk, gvf, gu, dkr, gfl, dcq_, zpad, ggl], axis=1)
        return dproj, dqlg, dwuq, dkvlg, dwuk, dwuv, dmqg, dmkg, dfqg, dfkg

    pre_params = [p["qlg"], p["w_uq"], p["kvlg"], p["w_uk"], p["w_uv"], p["mqg"], p["mkg"], p["fqg"], p["fkg"],
                  e_mat, p_mat, r_mat, pl_mat]
    cts_in = [dq_m.reshape(t, QKW), dk_m.reshape(t, QKW), dv_m.reshape(t, 2 * LANES), dq_f.reshape(t, QKW),
              dk_f.reshape(t, QKW), dv_f.reshape(t, 2 * LANES), du, dfl, dgl]
    (dproj, g_qlg, g_wuq, g_kvlg, g_wuk, g_wuv, g_mqg, g_mkg, g_fqg, g_fkg) = _tile_call(
        pre_bwd_fn, "mixer_pre_bwd", (t // TM_H,),
        [_row_h(proj, PA), _row_h(cos), _row_h(sin)] + [_par(a) for a in pre_params] + [_row_h(a) for a in cts_in],
        [_orow_h(t, PW, BF16), _oacc((1, MLA_Q_RANK)), _oacc((MLA_Q_RANK, QKW)), _oacc((1, MLA_KV_RANK)),
         _oacc((MLA_KV_RANK, QKW)), _oacc((MLA_KV_RANK, N_HEADS * HEAD_V)), _oacc((1, QKW)), _oacc((1, QKW)),
         _oacc((1, QKW)), _oacc((1, QKW))])
    g["q_lat_norm_g"] = g_qlg.reshape(MLA_Q_RANK)
    g["kv_lat_norm_g"] = g_kvlg.reshape(MLA_KV_RANK)
    heads = lambda a, w: jnp.sum(a.reshape(N_HEADS, HP)[:, :w], axis=0)
    g["mla_q_norm_g"], g["mla_k_norm_g"] = heads(g_mqg, MLA_QK_DIM), heads(g_mkg, MLA_QK_DIM)
    g["fox_q_norm_g"], g["fox_k_norm_g"] = heads(g_fqg, HEAD_V), heads(g_fkg, HEAD_V)
    g["w_uq"] = g_wuq.reshape(MLA_Q_RANK, N_HEADS, HP)[..., :MLA_QK_DIM].reshape(MLA_Q_RANK, N_HEADS * MLA_QK_DIM)
    g["w_ukv"] = jnp.concatenate([g_wuk.reshape(MLA_KV_RANK, N_HEADS, HP)[..., :MLA_NOPE_DIM],
                                  g_wuv.reshape(MLA_KV_RANK, N_HEADS, HEAD_V)], axis=-1).reshape(MLA_KV_RANK, QKW)

    g["w_in"] = _mm(sv["h"], dproj, "tn", "mm_in_dw", out_dtype=BF16)
    dh = landed("mm_in_dx", _mm(dproj, p["w_in"], "nt", "mm_in_dx", rider=riding("mm_in_dx", g)[0]), 1)[0]
    dx, g["attn_g"] = _tile_call(norm_bwd_fn, "norm_bwd", (nt,),
                                 [_row(sv["x"]), _par(p["attn_g"]), _row(dh), _row(d1)],
                                 [_orow(t, D_MODEL), _oacc((1, D_MODEL))])
    return dx, g, lands, carry_lands


def _row_tile(r, c, max_elems):
    if r * c <= max_elems:
        return r
    best = None
    for d in range(8, r, 8):
        if r % d == 0 and d * c <= max_elems:
            best = d
    assert best is not None, (r, c)
    return best


def _sum8(land, name):
    _, r, c = land.shape
    tr = _row_tile(r, N_DEV * c, 1 << 20)

    def fn(lv):
        acc = lv[0].astype(F32)
        for i in range(1, N_DEV):
            acc = acc + lv[i].astype(F32)
        return acc

    return _tile_call(fn, name, (r // tr,), [(land, (N_DEV, tr, c), lambda i: (0, i, 0))],
                      [((r, c), F32, (tr, c), lambda i: (i, 0), None)])[0]


def _adamw_layer(w, g, m, v, l, prev, name):
    _, r, c = w.shape
    tr = _row_tile(r, c, 1 << 19)
    lay = pl.BlockSpec((None, tr, c), lambda i: (l, i, 0))

    def body(w_ref, g_ref, m_ref, v_ref, *rest):
        g_out, d_out, m_out, v_out = rest[-4:]
        gv = g_ref[...]
        d_out[...], m_out[...], v_out[...] = _adamw(w_ref[...], gv, m_ref[...], v_ref[...])
        g_out[...] = gv

    in_specs = [lay, pl.BlockSpec((tr, c), lambda i: (i, 0)), lay, lay]
    args = [w, g, m, v]
    aliases = {}
    if prev is not None:
        in_specs += [pl.BlockSpec(memory_space=pl.ANY)] * 4
        args += list(prev)
        aliases = {4 + k: k for k in range(4)}
    return _pcall(body, name=name, grid=(r // tr,), in_specs=in_specs, out_specs=[lay] * 4,
                  out_shape=[jax.ShapeDtypeStruct(w.shape, F32)] * 4, aliases=aliases)(*args)


SMALL_NAMES = ("attn_norm_g", "q_lat_norm_g", "kv_lat_norm_g", "mla_q_norm_g", "mla_k_norm_g", "fox_q_norm_g",
               "fox_k_norm_g", "fox_f_bias", "s5_lambda_re", "s5_lambda_im", "s5_b_re", "s5_b_im", "s5_c_re", "s5_c_im",
               "s5_d", "s5_log_step", "s5_b_glu", "ffn_norm_g")
BIG_NAMES = ("w_in", "w_uq", "w_ukv", "s5_w_glu", "w_branch", "w_out", "w_up", "ffn_conv_w", "w_down")
ALL_NAMES = ("attn_norm_g", "w_in", "q_lat_norm_g", "w_uq", "kv_lat_norm_g", "w_ukv", "mla_q_norm_g", "mla_k_norm_g",
             "fox_q_norm_g", "fox_k_norm_g", "fox_f_bias", "s5_lambda_re", "s5_lambda_im", "s5_b_re", "s5_b_im",
             "s5_c_re", "s5_c_im", "s5_d", "s5_log_step", "s5_w_glu", "s5_b_glu", "w_branch", "w_out", "ffn_norm_g",
             "w_up", "ffn_conv_w", "w_down")


def _pack(arrs):
    tile = 8 * LANES
    parts = []
    for a in arrs:
        n = int(np.prod(a.shape))
        tot = -(-n // tile) * tile
        parts.append(jnp.pad(a.reshape(-1), (0, tot - n)).reshape(tot // LANES, LANES))
    rows = sum(a.shape[0] for a in parts)
    parts.append(jnp.zeros((-rows % 256, LANES), parts[0].dtype))
    return jnp.concatenate(parts, axis=0)


def _unpack(packed, like):
    tile = 8 * LANES
    out, pos = [], 0
    for a in like:
        n = int(np.prod(a.shape))
        rows = -(-n // tile) * 8
        out.append(packed[pos:pos + rows].reshape(-1)[:n].reshape(a.shape))
        pos += rows
    return out


def _rope_tables(positions, t):
    pos = positions.reshape(t, 1)
    inv = _inv_freq_row()

    def fn(pv, iv):
        ang = pv.astype(F32) * iv
        return jnp.cos(ang), jnp.sin(ang)

    return _tile_call(fn, "rope_tables", (t // TM,), [_row(pos), _par(inv)], [_orow(t, QKW), _orow(t, QKW)])


def _loss_call(x1, f, target):
    t = x1.shape[0]

    def fn(xv, fv, tv):
        e = xv + fv - tv
        per_tok = jnp.sum(e * e, axis=1, keepdims=True) * (1.0 / D_MODEL)
        return 0.5 * jnp.sum(per_tok, axis=0, keepdims=True), e * (1.0 / D_MODEL)

    return _tile_call(fn, "loss", (t // TM,), [_row(x1), _row(f), _row(target)], [_oacc((1, 1)), _orow(t, D_MODEL)])


def _local_shards(l, w):
    return {
        "w_in": _pad_in_cols(w["w_in"][l]).astype(BF16),
        "w_uq": w["w_uq"][l].reshape(-1, N_HEADS * MLA_QK_DIM).astype(BF16),
        "w_ukv": w["w_ukv"][l].reshape(-1, QKW).astype(BF16),
        "w_glu": w["s5_w_glu"][l].astype(BF16),
        "w_branch": w["w_branch"][l].astype(BF16),
        "w_out": w["w_out"][l].astype(BF16),
        "w_up": _pad_ff(w["w_up"][l], 1).astype(BF16),
        "conv": _pad_ff(w["ffn_conv_w"][l], 1),
        "w_down": _pad_ff(w["w_down"][l], 0).astype(BF16),
    }


GATHERED_SHAPES = {"w_in": (D_MODEL, PW), "w_uq": (MLA_Q_RANK, N_HEADS * MLA_QK_DIM), "w_ukv": (MLA_KV_RANK, QKW),
                   "w_glu": (S5_WIDTH, S5_WIDTH), "w_branch": (3, S5_WIDTH, D_MODEL), "w_out": (D_MODEL, D_MODEL),
                   "w_up": (D_MODEL, UP_W), "conv": (3, UP_W), "w_down": (FF_PW, D_MODEL)}
FFN_WEIGHTS = ("w_up", "conv", "w_down")
MIXER_WEIGHTS = ("w_in", "w_uq", "w_ukv", "w_glu", "w_out", "w_branch")
FWD_RIDERS_OWN = {"attn_mla_fwd": ("w_up", "conv"), "s5_scan_fwd": ("w_down",)}
FWD_RIDERS_NEXT = {"attn_fox_fwd": ("w_in", "w_uq", "w_ukv", "w_glu"), "mixer_pre": ("w_out", "w_branch")}
BWD_RIDERS = {"mm_up_dx": ("w_up:0",), "s5_scan_bwd": ("w_down",), "attn_mla_bwd": ("w_up:1",),
              "attn_fox_bwd": ("w_out", "w_branch", "w_glu", "conv"), "mm_in_dx": ("w_uq", "w_ukv")}
UPDATED_TRANSPOSED = ("w_up",)
WEIGHT_OF = {"w_in": "w_in", "w_uq": "w_uq", "w_ukv": "w_ukv", "w_glu": "s5_w_glu", "w_branch": "w_branch",
             "w_out": "w_out", "w_up": "w_up", "conv": "ffn_conv_w", "w_down": "w_down"}


def _unshard_layout(nm, g2):
    if nm == "w_in":
        return _unpad_in_cols(g2)
    if nm == "w_uq":
        return g2.reshape(-1, N_HEADS, MLA_QK_DIM)
    if nm == "w_ukv":
        return g2.reshape(-1, N_HEADS, MLA_NOPE_DIM + HEAD_V)
    if nm in ("w_up", "conv"):
        return _unpad_ff(g2, 1)
    if nm == "w_down":
        return _unpad_ff(g2, 0)
    return g2


def kernel(x, positions, attn_norm_g, w_in, q_lat_norm_g, w_uq, kv_lat_norm_g, w_ukv, mla_q_norm_g, mla_k_norm_g, fox_q_norm_g, fox_k_norm_g, fox_f_bias, s5_lambda_re, s5_lambda_im, s5_b_re, s5_b_im, s5_c_re, s5_c_im, s5_d, s5_log_step, s5_w_glu, s5_b_glu, w_branch, w_out, ffn_norm_g, w_up, ffn_conv_w, w_down, loss_target, m_attn_norm_g, m_w_in, m_q_lat_norm_g, m_w_uq, m_kv_lat_norm_g, m_w_ukv, m_mla_q_norm_g, m_mla_k_norm_g, m_fox_q_norm_g, m_fox_k_norm_g, m_fox_f_bias, m_s5_lambda_re, m_s5_lambda_im, m_s5_b_re, m_s5_b_im, m_s5_c_re, m_s5_c_im, m_s5_d, m_s5_log_step, m_s5_w_glu, m_s5_b_glu, m_w_branch, m_w_out, m_ffn_norm_g, m_w_up, m_ffn_conv_w, m_w_down, v_attn_norm_g, v_w_in, v_q_lat_norm_g, v_w_uq, v_kv_lat_norm_g, v_w_ukv, v_mla_q_norm_g, v_mla_k_norm_g, v_fox_q_norm_g, v_fox_k_norm_g, v_fox_f_bias, v_s5_lambda_re, v_s5_lambda_im, v_s5_b_re, v_s5_b_im, v_s5_c_re, v_s5_c_im, v_s5_d, v_s5_log_step, v_s5_w_glu, v_s5_b_glu, v_w_branch, v_w_out, v_ffn_norm_g, v_w_up, v_ffn_conv_w, v_w_down):
    args = locals()
    w = {n: args[n] for n in ALL_NAMES}
    m = {n: args["m_" + n] for n in ALL_NAMES}
    v = {n: args["v_" + n] for n in ALL_NAMES}
    depth = w_in.shape[0]
    bsz, seq, _ = x.shape
    t = bsz * seq
    consts = _constants()
    tabs = _rope_tables(positions, t)
    small = {n: w[n] for n in SMALL_NAMES}

    all_names = list(GATHERED_SHAPES)
    all_shards = [_local_shards(l, w) for l in range(depth)]

    def gather_xfer(l, names):
        sub = {n: all_shards[l][n] for n in names}
        order, plan = _gather_plan(sub)
        return _Gather2([sub[n] for n in order], [(GATHERED_SHAPES[n], sub[n].dtype) for n in order], plan)

    def scatter_xfer(g, names):
        shapes = {n: _local_shards_shape(n) for n in names}
        return _Xfer([g[n] for n in names], [((N_DEV,) + shapes[n], g[n].dtype) for n in names],
                     _scatter_plan(list(names), shapes))

    xs = x.reshape(t, D_MODEL)
    f_prev = None
    saved, params = [], []
    gathered = dict(zip(MIXER_WEIGHTS, _exchange("gather_weights", gather_xfer(0, MIXER_WEIGHTS))))
    for l in range(depth):
        p = _prep_layer(l, gathered, small, consts)
        riders = {host: (gather_xfer(l, names), names) for host, names in FWD_RIDERS_OWN.items()}
        if l + 1 < depth:
            riders.update({host: (gather_xfer(l + 1, names), names) for host, names in FWD_RIDERS_NEXT.items()})
        xs, f_prev, sv, gathered = _layer_fwd(xs, f_prev, p, tabs, consts, bsz, riders)
        saved.append(sv)
        params.append(p)

    loss_part, dy = _loss_call(xs, f_prev, loss_target.reshape(t, D_MODEL))
    loss = lax.psum(loss_part[0, 0], ("x", "y", "c"))

    small_grads = {n: [None] * depth for n in SMALL_NAMES}
    big = {n: None for n in BIG_NAMES}

    def finish(l, lands):
        for n in all_names:
            wn = WEIGHT_OF[n]
            if n == "w_up":
                gsum = jnp.concatenate([_sum8(lands["w_up:%d" % h], "sum_w_up") for h in range(2)], axis=1)
            else:
                ld = lands[n]
                gsum = _sum8(ld.reshape(N_DEV, -1, ld.shape[-1]), "sum_" + n).reshape(_local_shards_shape(n))
            gsum = _unshard_layout(n, gsum)
            if n in UPDATED_TRANSPOSED:
                gsum = gsum.T
            c = gsum.shape[-1]
            three = lambda a: (jnp.swapaxes(a, 1, 2) if n in UPDATED_TRANSPOSED else a).reshape(depth, -1, c)
            big[wn] = _adamw_layer(three(w[wn]), gsum.reshape(-1, c), three(m[wn]), three(v[wn]), l, big[wn],
                                   "adamw_" + n)

    def scatter(grads):
        return scatter_xfer(grads, list(grads))

    dx = dy
    carry, lands_above = {}, None
    for l in reversed(range(depth)):
        dx, g, lands, carry_lands = _layer_bwd(dx, params[l], saved[l], tabs, consts, bsz, scatter, carry, l == 0)
        if lands_above is not None:
            finish(l + 1, {**lands_above, **carry_lands})
        carry, lands_above = {"w_in": g["w_in"]}, lands
        small_grads["attn_norm_g"][l] = g["attn_g"].reshape(D_MODEL)
        small_grads["ffn_norm_g"][l] = g["ffn_g"].reshape(D_MODEL)
        for n in SMALL_NAMES:
            if n not in ("attn_norm_g", "ffn_norm_g"):
                small_grads[n][l] = g[n]
    finish(0, lands_above)

    sg = [jnp.stack(small_grads[n]).reshape(w[n].shape) for n in SMALL_NAMES]
    packed = _pack(sg)
    land = _exchange("gather_small_grads", _Gather2([packed], [((N_DEV,) + packed.shape, F32)],
                                                    _scatter_plan(["small"], {})))[0]
    gs = _sum8(land, "sum_small")
    one = lambda names_of: _pack([names_of[n] for n in SMALL_NAMES])[None]
    small_res = _adamw_layer(one(w), gs, one(m), one(v), 0, None, "adamw_small")
    like = [w[n] for n in SMALL_NAMES]
    small_out = [dict(zip(SMALL_NAMES, _unpack(a[0], like))) for a in small_res]

    def out_of(kind, n):
        if n in SMALL_NAMES:
            return small_out[kind][n]
        if n in UPDATED_TRANSPOSED:
            return jnp.swapaxes(big[n][kind], 1, 2)
        return big[n][kind].reshape(w[n].shape)

    outs = [loss, dx.reshape(bsz, seq, D_MODEL)]
    for kind in range(4):
        outs += [out_of(kind, n) for n in ALL_NAMES]
    return tuple(outs)


def _local_shards_shape(nm):
    return {"w_in": (D_MODEL // N_DEV, PW), "w_uq": (MLA_Q_RANK // N_DEV, N_HEADS * MLA_QK_DIM),
            "w_ukv": (MLA_KV_RANK // N_DEV, QKW), "w_glu": (S5_WIDTH // N_DEV, S5_WIDTH),
            "w_branch": (3, S5_WIDTH, LANES), "w_out": (D_MODEL // N_DEV, D_MODEL), "w_up": (D_MODEL, 2 * FF_BLK),
            "conv": (3, 2 * FF_BLK), "w_down": (FF_BLK, D_MODEL), "w_up:0": (D_MODEL, FF_BLK),
            "w_up:1": (D_MODEL, FF_BLK)}[nm]
```

```python
import functools
import math

import numpy as np
import jax
import jax.numpy as jnp
from jax import lax
from jax.experimental import pallas as pl
from jax.experimental.pallas import tpu as pltpu

F32, BF16 = jnp.float32, jnp.bfloat16

D_MODEL = 1024
N_DEV = 8
MLA_Q_RANK, MLA_KV_RANK, MLA_ROPE_DIM, MLA_NOPE_DIM, MLA_QK_DIM = 384, 256, 32, 64, 96
N_HEADS, HEAD_V = 4, 64
S5_GROUPS, S5_GROUP_CH, S5_STATE, S5_WIDTH = 16, 16, 64, 256
S5_N = S5_GROUPS * S5_STATE
D_FF = 2816
D_IN = 4772
ROPE_THETA = 10000.0
NORM_EPS = 1e-6
NEG_INF = -1e30
ADAM_LR, ADAM_B1, ADAM_B2, ADAM_EPS, ADAM_WD, ADAM_STEP = 0.001, 0.9, 0.999, 1e-08, 0.01, 10

VMEM_LIMIT_BYTES = 56 * 1024 * 1024
LANES = 128
TM = 512
TM_H = 256
TQ = 512
QF = 128
QC = 512
HP = 128
QKW = N_HEADS * HP

PW = 5120
PA = 2048
IN_SEGS = ((0, 384, 256),
           (256, 672, 256),
           (512, 928, 256),
           (768, 1184, 256),
           (1024, 1444, 256),
           (1280, 640, 32),
           (1408, 1440, 4),
           (1536, 0, 384),
           (2048, 1700, 3072))
O_CKV, O_FQ, O_FK, O_FV, O_U, O_KR, O_FL, O_CQ = 0, 256, 512, 768, 1024, 1280, 1408, 1536

FF_BLK = 384
FF_HALF = D_FF // 8
FF_PW = 8 * FF_BLK
UP_W = 2 * FF_PW

NN = (((1,), (0,)), ((), ()))
NT = (((1,), (1,)), ((), ()))
TN = (((0,), (0,)), ((), ()))


def _pcall(body, *, name, grid, in_specs, out_specs, out_shape, scratch_shapes=(), rider=None, aliases=None):
    params = pltpu.CompilerParams(dimension_semantics=("arbitrary",) * len(grid), vmem_limit_bytes=VMEM_LIMIT_BYTES)
    if rider is None:
        return pl.pallas_call(body, name=name, grid=grid, in_specs=in_specs, out_specs=out_specs, out_shape=out_shape,
                              scratch_shapes=list(scratch_shapes), input_output_aliases=aliases or {},
                              compiler_params=params)
    assert not aliases
    in_specs, out_specs, out_shape = list(in_specs), list(out_specs), list(out_shape)
    n_in, n_out, n_scr, nx = len(in_specs), len(out_specs), len(scratch_shapes), rider.n
    any_spec = pl.BlockSpec(memory_space=pl.ANY)

    def at_step(last):
        conds = [pl.program_id(a) == (g - 1 if last else 0) for a, g in enumerate(grid)]
        return functools.reduce(jnp.logical_and, conds)

    def body_with_rider(*refs):
        ins, xsrc = refs[:n_in], refs[n_in:n_in + nx]
        outs, xout = refs[n_in + nx:n_in + nx + n_out], refs[n_in + nx + n_out:n_in + 2 * nx + n_out]
        scr = refs[n_in + 2 * nx + n_out:n_in + 2 * nx + n_out + n_scr]
        sems = refs[n_in + 2 * nx + n_out + n_scr:]

        @pl.when(at_step(False))
        def _():
            rider.start(rider.copies(xsrc, xout, *sems))

        body(*ins, *outs, *scr)

        @pl.when(at_step(True))
        def _():
            rider.wait(rider.copies(xsrc, xout, *sems))

    call = pl.pallas_call(body_with_rider, name=name, grid=grid, in_specs=in_specs + [any_spec] * nx,
                          out_specs=out_specs + [any_spec] * nx,
                          out_shape=out_shape + [jax.ShapeDtypeStruct(s, d) for s, d in rider.out_shapes],
                          scratch_shapes=list(scratch_shapes) + rider.sems(), compiler_params=params)
    return lambda *args: call(*args, *rider.srcs)


def _tile_call(fn, name, grid, ins, outs, rider=None):
    n_in = len(ins)

    def body(*refs):
        res = fn(*[r[...] for r in refs[:n_in]])
        if not isinstance(res, (tuple, list)):
            res = (res,)
        assert len(res) == len(outs), (name, len(res), len(outs))
        for r, v, o in zip(refs[n_in:], res, outs):
            v = v.astype(r.dtype)
            if o[4] is None:
                r[...] = v
            else:
                first = functools.reduce(jnp.logical_and, [pl.program_id(a) == 0 for a in o[4]])

                @pl.when(first)
                def _():
                    r[...] = v

                @pl.when(jnp.logical_not(first))
                def _():
                    r[...] += v

    res = _pcall(body, name=name, grid=grid,
                 in_specs=[pl.BlockSpec(b, m) for _, b, m in ins],
                 out_specs=[pl.BlockSpec(o[2], o[3]) for o in outs],
                 out_shape=[jax.ShapeDtypeStruct(o[0], o[1]) for o in outs], rider=rider)(*[a for a, _, _ in ins])
    return res


def _row(a, width=None, col_block=0, tm=TM):
    w = a.shape[1] if width is None else width
    return (a, (tm, w), lambda i, c=col_block: (i, c))


def _par(a):
    nd = a.ndim
    return (a, a.shape, lambda i, nd=nd: (0,) * nd)


def _orow(t, w, dtype=F32, tm=TM):
    return ((t, w), dtype, (tm, w), lambda i: (i, 0), None)


def _row_h(a, width=None, col_block=0):
    return _row(a, width, col_block, tm=TM_H)


def _orow_h(t, w, dtype=F32):
    return _orow(t, w, dtype, tm=TM_H)


def _oacc(shape):
    nd = len(shape)
    return (tuple(shape), F32, tuple(shape), lambda i, nd=nd: (0,) * nd, (0,))


def _pick(n, target):
    best = None
    for d in range(LANES, min(n, target) + 1, LANES):
        if n % d == 0:
            best = d
    return n if best is None else best


MM_VMEM_BUDGET = 44 * 1024 * 1024


def _mm_tiles(m, n, k, sa, sb, so, tm, tn, tk):
    for cm, cn in ((1024, 2048), (1024, 1024), (512, 2048), (512, 1024), (512, 512), (256, 1024), (256, 512)):
        pm, pn = _pick(m, cm), _pick(n, cn)
        if 2 * (pm * k * sa + k * pn * sb + pm * pn * so) <= MM_VMEM_BUDGET:
            return pm, pn, k
    return _pick(m, tm), _pick(n, tn), _pick(k, tk)


def _mm(a, b, mode, name, out_dtype=F32, tm=1024, tn=2048, tk=1024, rider=None):
    if mode == "nn":
        (m, k), (_, n) = a.shape, b.shape
    elif mode == "nt":
        (m, k), (n, _) = a.shape, b.shape
    else:
        (k, m), (_, n) = a.shape, b.shape
    tm, tn, tk = _mm_tiles(m, n, k, a.dtype.itemsize, b.dtype.itemsize, jnp.dtype(out_dtype).itemsize, tm, tn, tk)
    nk = k // tk
    dims = {"nn": NN, "nt": NT, "tn": TN}[mode]
    a_spec = (pl.BlockSpec((tk, tm), lambda i, j, l: (l, i)) if mode == "tn"
              else pl.BlockSpec((tm, tk), lambda i, j, l: (i, l)))
    b_spec = (pl.BlockSpec((tn, tk), lambda i, j, l: (j, l)) if mode == "nt"
              else pl.BlockSpec((tk, tn), lambda i, j, l: (l, j)))

    def body(a_ref, b_ref, o_ref, acc_ref):
        part = lax.dot_general(a_ref[...].astype(BF16), b_ref[...].astype(BF16), dims, preferred_element_type=F32)
        if nk == 1:
            o_ref[...] = part.astype(o_ref.dtype)
        else:
            l = pl.program_id(2)

            @pl.when(l == 0)
            def _():
                acc_ref[...] = part

            @pl.when(l > 0)
            def _():
                acc_ref[...] += part

            @pl.when(l == nk - 1)
            def _():
                o_ref[...] = acc_ref[...].astype(o_ref.dtype)

    res = _pcall(body, name=name, grid=(m // tm, n // tn, nk), in_specs=[a_spec, b_spec],
                 out_specs=[pl.BlockSpec((tm, tn), lambda i, j, l: (i, j))],
                 out_shape=[jax.ShapeDtypeStruct((m, n), out_dtype)],
                 scratch_shapes=[pltpu.VMEM((tm, tn) if nk > 1 else (8, LANES), F32)], rider=rider)(a, b)
    return res[0] if rider is None else res


def _dot(x, w, dims):
    return lax.dot_general(x.astype(BF16), w.astype(BF16), dims, preferred_element_type=F32)


@jax.custom_vjp
def bdot(x, w):
    return _dot(x, w, NN)


def _bdot_fwd(x, w):
    return _dot(x, w, NN), (x, w)


def _bdot_bwd(res, g):
    x, w = res
    return _dot(g, w, NT), _dot(x, g, TN)


bdot.defvjp(_bdot_fwd, _bdot_bwd)


def _split3(x):
    x1 = x.astype(BF16)
    r1 = x - x1.astype(F32)
    x2 = r1.astype(BF16)
    x3 = (r1 - x2.astype(F32)).astype(BF16)
    return x1, x2, x3


def _cdot(x, m, dims):
    return sum(lax.dot_general(p, m, dims, preferred_element_type=F32) for p in _split3(x))


def _block_dot(x, m, transposed):
    k_in = m.shape[1] if transposed else m.shape[0]
    dims = NT if transposed else NN
    return jnp.concatenate([_cdot(x[:, b * k_in:(b + 1) * k_in], m, dims) for b in range(x.shape[1] // k_in)], axis=1)


@jax.custom_vjp
def cdot(x, m):
    return _block_dot(x, m, False)


def _cdot_fwd(x, m):
    return _block_dot(x, m, False), m


def _cdot_bwd(m, g):
    return _block_dot(g, m, True), jnp.zeros_like(m)


cdot.defvjp(_cdot_fwd, _cdot_bwd)


@jax.custom_vjp
def tile_heads(y):
    return jnp.concatenate([y] * N_HEADS, axis=1)


def _tile_heads_fwd(y):
    return jnp.concatenate([y] * N_HEADS, axis=1), None


def _tile_heads_bwd(_, g):
    return (sum(g[:, h * HP:(h + 1) * HP] for h in range(N_HEADS)),)


tile_heads.defvjp(_tile_heads_fwd, _tile_heads_bwd)


def _shift_rows_impl(x, s, reverse):
    n = x.shape[0]
    idx = lax.broadcasted_iota(jnp.int32, x.shape, 0)
    if reverse:
        return jnp.where(idx < n - s, pltpu.roll(x, n - s, 0), 0.0)
    return jnp.where(idx >= s, pltpu.roll(x, s, 0), 0.0)


@functools.partial(jax.custom_vjp, nondiff_argnums=(1,))
def shift_rows(x, s):
    return _shift_rows_impl(x, s, False)


def _shift_rows_fwd(x, s):
    return _shift_rows_impl(x, s, False), None


def _shift_rows_bwd(s, _, g):
    return (_shift_rows_impl(g, s, True),)


shift_rows.defvjp(_shift_rows_fwd, _shift_rows_bwd)


def _rms(x, g):
    return x * lax.rsqrt(jnp.mean(x * x, axis=-1, keepdims=True) + NORM_EPS) * g


def _head_rms(x, e_mat, inv_n, g):
    ms = cdot(x * x, e_mat) * inv_n
    return x * lax.rsqrt(ms + NORM_EPS) * g


def _rope(y, p_mat, cos, sin):
    return y * cos + cdot(y, p_mat) * sin


def _constants():
    e = np.ones((HP, HP), np.float32)
    p = np.zeros((HP, HP), np.float32)
    r = np.zeros((LANES, HP), np.float32)
    pl64 = np.zeros((2 * HEAD_V, 2 * HP), np.float32)
    half = MLA_ROPE_DIM // 2
    for i in range(half):
        x1, x2 = MLA_NOPE_DIM + i, MLA_NOPE_DIM + half + i
        p[x2, x1] = -1.0
        p[x1, x2] = 1.0
    for i in range(MLA_ROPE_DIM):
        r[i, MLA_NOPE_DIM + i] = 1.0
    for h in range(2):
        for i in range(HEAD_V):
            pl64[h * HEAD_V + i, h * HP + i] = 1.0
    return tuple(jnp.asarray(a, BF16) for a in (e, p, r, pl64))


def _inv_freq_row():
    inv = ROPE_THETA ** (-jnp.arange(0, MLA_ROPE_DIM, 2, dtype=F32) / MLA_ROPE_DIM)
    head = jnp.concatenate([jnp.zeros((MLA_NOPE_DIM,), F32), inv, inv, jnp.zeros((HP - MLA_QK_DIM,), F32)])
    return jnp.tile(head, N_HEADS).reshape(1, QKW)


def _mixer_pre(c_q, c_kv, kr, fq, fk, qlg, w_uq, kvlg, w_uk, w_uv, mqg, mkg, fqg, fkg, cos, sin, e_mat, p_mat, r_mat, pl_mat):
    q = bdot(_rms(c_q, qlg), w_uq)
    ckvn = _rms(c_kv, kvlg)
    k = bdot(ckvn, w_uk) + tile_heads(cdot(kr, r_mat))
    v = bdot(ckvn, w_uv)
    q = _rope(_head_rms(q, e_mat, 1.0 / MLA_QK_DIM, mqg), p_mat, cos, sin)
    k = _rope(_head_rms(k, e_mat, 1.0 / MLA_QK_DIM, mkg), p_mat, cos, sin)
    qf = _head_rms(cdot(fq, pl_mat), e_mat, 1.0 / HEAD_V, fqg)
    kf = _head_rms(cdot(fk, pl_mat), e_mat, 1.0 / HEAD_V, fkg)
    return q, k, v, qf, kf


def _s5_post(state, u, c_blk, d_row, w_glu, b_glu):
    y = bdot(state, c_blk) + d_row * u
    y = jax.nn.gelu(y)
    return y * jax.nn.sigmoid(bdot(y, w_glu) + b_glu)


def _merge(o_a, o_b, o_c, g_a, g_b, g_c, w_a, w_b, w_c):
    return (jax.nn.sigmoid(g_a) * bdot(o_a, w_a) + jax.nn.sigmoid(g_b) * bdot(o_b, w_b)
            + jax.nn.sigmoid(g_c) * bdot(o_c, w_c))


def _ffn_mid(gate, val, wg0, wg1, wg2, wv0, wv1, wv2):
    cg = wg0 * shift_rows(gate, 2) + wg1 * shift_rows(gate, 1) + wg2 * gate
    cv = wv0 * shift_rows(val, 2) + wv1 * shift_rows(val, 1) + wv2 * val
    return jax.nn.silu(cg) * cv


def _s5_params(lam_re, lam_im, log_step, b_re, b_im):
    step = jnp.exp(log_step)
    zr, zi = lam_re * step, lam_im * step
    mag = jnp.exp(zr)
    lr, li = mag * jnp.cos(zi), mag * jnp.sin(zi)
    nr, ni = lr - 1.0, li
    den = lam_re * lam_re + lam_im * lam_im
    cr = (nr * lam_re + ni * lam_im) / den
    ci = (ni * lam_re - nr * lam_im) / den
    return lr, li, cr * b_re - ci * b_im, cr * b_im + ci * b_re


def _adamw(w, g, m, v):
    m = ADAM_B1 * m + (1.0 - ADAM_B1) * g
    v = ADAM_B2 * v + (1.0 - ADAM_B2) * (g * g)
    m_hat = m / (1.0 - ADAM_B1 ** ADAM_STEP)
    v_hat = v / (1.0 - ADAM_B2 ** ADAM_STEP)
    delta = -ADAM_LR * (m_hat / (jnp.sqrt(v_hat) + ADAM_EPS) + ADAM_WD * w)
    return delta, m, v


SCAN_W = 256


def _scan_seq(xr, xi, ar, ai, reverse):
    n = xr.shape[0]
    s = 1
    while s < n:
        sr, si = _shift_rows_impl(xr, s, reverse), _shift_rows_impl(xi, s, reverse)
        xr, xi = xr + ar * sr - ai * si, xi + ar * si + ai * sr
        ar, ai = ar * ar - ai * ai, 2.0 * ar * ai
        s *= 2
    return xr, xi


def _scan_fwd(bu, lbar, bsz, name, rider=None):
    t = bu.shape[0]
    seq = t // bsz
    nb = bu.shape[1] // (2 * SCAN_W)

    def fn(b, a):
        xr, xi = _scan_seq(b[:, :SCAN_W], b[:, SCAN_W:], a[:, :SCAN_W], a[:, SCAN_W:], False)
        return jnp.concatenate([xr, xi], axis=1)

    blk = (seq, 2 * SCAN_W)
    return _tile_call(fn, name, (bsz, nb),
                      [(bu, blk, lambda i, j: (i, j)), (lbar, (1, 2 * SCAN_W), lambda i, j: (0, j))],
                      [((t, bu.shape[1]), F32, blk, lambda i, j: (i, j), None)], rider=rider)


def _scan_bwd(dstate, state, lbar, bsz, name, rider=None):
    t = state.shape[0]
    seq = t // bsz
    nb = state.shape[1] // (2 * SCAN_W)

    def fn(g, x, a):
        ar, ai = a[:, :SCAN_W], a[:, SCAN_W:]
        gr, gi = _scan_seq(g[:, :SCAN_W], g[:, SCAN_W:], ar, -ai, True)
        pr, pi = _shift_rows_impl(x[:, :SCAN_W], 1, False), _shift_rows_impl(x[:, SCAN_W:], 1, False)
        dar = jnp.sum(gr * pr + gi * pi, axis=0, keepdims=True)
        dai = jnp.sum(gi * pr - gr * pi, axis=0, keepdims=True)
        return jnp.concatenate([gr, gi], axis=1), jnp.concatenate([dar, dai], axis=1)

    blk = (seq, 2 * SCAN_W)
    return _tile_call(fn, name, (nb, bsz),
                      [(dstate, blk, lambda j, i: (i, j)), (state, blk, lambda j, i: (i, j)),
                       (lbar, (1, 2 * SCAN_W), lambda j, i: (0, j))],
                      [((t, state.shape[1]), F32, blk, lambda j, i: (i, j), None),
                       ((1, state.shape[1]), F32, (1, 2 * SCAN_W), lambda j, i: (0, j), (1,))], rider=rider)


def _shift_lanes(x, s, reverse):
    n = x.shape[1]
    idx = lax.broadcasted_iota(jnp.int32, x.shape, 1)
    if reverse:
        return jnp.where(idx < n - s, pltpu.roll(x, n - s, 1), 0.0)
    return jnp.where(idx >= s, pltpu.roll(x, s, 1), 0.0)


def _cumsum_lanes(x, reverse):
    s = 1
    while s < x.shape[1]:
        x = x + _shift_lanes(x, s, reverse)
        s *= 2
    return x


def _forget_fwd(z, bias, name):
    def fn(zv, bv):
        x = zv + bv
        logf = jnp.minimum(x, 0.0) - jnp.log(1.0 + jnp.exp(-jnp.abs(x)))
        return _cumsum_lanes(logf, False)

    return _tile_call(fn, name, (1,), [_par(z), _par(bias)],
                      [(z.shape, F32, z.shape, lambda i: (0, 0), None)])[0]


def _forget_bwd(dc, z, bias, name):
    def fn(dcv, zv, bv):
        x = zv + bv
        dlogf = _cumsum_lanes(dcv, True)
        dz = dlogf * jax.nn.sigmoid(-x)
        return dz, jnp.sum(dz, axis=1, keepdims=True)

    rows = z.shape[0]
    return _tile_call(fn, name, (1,), [_par(dc), _par(z), _par(bias)],
                      [(z.shape, F32, z.shape, lambda i: (0, 0), None),
                       ((rows, 1), F32, (rows, 1), lambda i: (0, 0), None)])


def _col(v2, x):
    idx = lax.broadcasted_iota(jnp.int32, v2.shape, 1)
    return jnp.sum(jnp.where(idx == x, v2, 0.0), axis=1, keepdims=True)


def _two_cols(c0, c1):
    idx = lax.broadcasted_iota(jnp.int32, (c0.shape[0], 2), 1)
    return jnp.where(idx == 0, c0, c1)


def _rows2(r0, r1):
    idx = lax.broadcasted_iota(jnp.int32, (2, r0.shape[1]), 0)
    return jnp.where(idx == 0, r0, r1)


def _attn_specs(seq, v_blk0, has_bias):
    qblk = pl.BlockSpec((None, TQ, 2 * HP), lambda b, p, i: (b, i, p))
    kblk = pl.BlockSpec((None, seq, 2 * HP), lambda b, p, i: (b, 0, p))
    vblk = pl.BlockSpec((None, seq, LANES), lambda b, p, i: (b, 0, v_blk0 + p))
    oblk = pl.BlockSpec((None, TQ, LANES), lambda b, p, i: (b, i, p))
    rowblk = pl.BlockSpec((None, None, 2, TQ), lambda b, p, i: (b, p, 0, i))
    colblk = pl.BlockSpec((None, None, seq, 2), lambda b, p, i: (b, p, 0, 0))
    return qblk, kblk, vblk, oblk, rowblk, colblk


def _attn_fwd(q, k, v, v_blk0, c_cols, c_rows, scale, name, rider=None):
    bsz, seq, _ = q.shape
    nq = seq // TQ
    has_bias = c_cols is not None
    n_in = 5 if has_bias else 3
    nc = TQ // QF

    def body(*refs):
        q_ref, k_ref, v_ref = refs[:3]
        cq_ref, ck_ref = refs[3:5] if has_bias else (None, None)
        o_ref, lse_ref = refs[n_in:]
        qi = pl.program_id(2)
        qb = [[q_ref[c * QF:(c + 1) * QF, x * HP:(x + 1) * HP].astype(BF16) for c in range(nc)] for x in range(2)]
        key_in = lax.broadcasted_iota(jnp.int32, (TQ, QF), 0)
        qry_in = lax.broadcasted_iota(jnp.int32, (TQ, QF), 1)

        def block(j, carry, masked):
            ks = pl.multiple_of(j * TQ, TQ)
            vt = v_ref[pl.ds(ks, TQ), :].T.astype(BF16)
            new = []
            for x in range(2):
                kx = k_ref[pl.ds(ks, TQ), x * HP:(x + 1) * HP].astype(BF16)
                ckx = _col(ck_ref[pl.ds(ks, TQ), :], x) if has_bias else None
                for c in range(nc):
                    m, l, acc = carry[x * nc + c]
                    s = lax.dot_general(kx, qb[x][c], NT, preferred_element_type=F32) * scale
                    if has_bias:
                        s = s + (cq_ref[x:x + 1, c * QF:(c + 1) * QF] - ckx)
                    if masked:
                        s = jnp.where(qry_in + c * QF >= key_in, s, NEG_INF)
                    m_new = jnp.maximum(m, jnp.max(s, axis=0, keepdims=True))
                    alpha = jnp.exp(m - m_new)
                    p = jnp.exp(s - m_new)
                    l = alpha * l + jnp.sum(p, axis=0, keepdims=True)
                    acc = alpha * acc + lax.dot_general(vt, p.astype(BF16), NN, preferred_element_type=F32)
                    new.append((m_new, l, acc))
            return tuple(new)

        init = tuple((jnp.full((1, QF), NEG_INF, F32), jnp.zeros((1, QF), F32), jnp.zeros((LANES, QF), F32))
                     for _ in range(2 * nc))
        carry = lax.fori_loop(0, qi, lambda j, cr: block(j, cr, False), init)
        carry = block(qi, carry, True)
        lane = lax.broadcasted_iota(jnp.int32, (QF, LANES), 1)
        lse_rows = []
        for x in range(2):
            lse_rows.append(jnp.concatenate([carry[x * nc + c][0] + jnp.log(carry[x * nc + c][1]) for c in range(nc)],
                                            axis=1))
        for c in range(nc):
            (_, l0, a0), (_, l1, a1) = carry[c], carry[nc + c]
            o_ref[c * QF:(c + 1) * QF, :] = jnp.where(lane < HEAD_V, (a0 / l0).T, (a1 / l1).T)
        lse_ref[...] = _rows2(lse_rows[0], lse_rows[1])

    qblk, kblk, vblk, oblk, rowblk, colblk = _attn_specs(seq, v_blk0, has_bias)
    in_specs, args = [qblk, kblk, vblk], [q, k, v]
    if has_bias:
        in_specs += [rowblk, colblk]
        args += [c_rows, c_cols]
    return _pcall(body, name=name, grid=(bsz, 2, nq), in_specs=in_specs, out_specs=[oblk, rowblk],
                  out_shape=[jax.ShapeDtypeStruct((bsz, seq, 2 * LANES), F32),
                             jax.ShapeDtypeStruct((bsz, 2, 2, seq), F32)], rider=rider)(*args)


def _attn_bwd(q, k, v, v_blk0, o, do, lse, c_cols, c_rows, scale, name, rider=None):
    bsz, seq, _ = q.shape
    nq = seq // TQ
    has_bias = c_cols is not None
    n_in = 8 if has_bias else 6
    nc = TQ // QC

    def body(*refs):
        q_ref, k_ref, v_ref, o_ref, do_ref, lse_ref = refs[:6]
        cq_ref, ck_ref = refs[6:8] if has_bias else (None, None)
        dq_ref, dk_ref, dv_ref = refs[n_in:n_in + 3]
        dck_ref, dcq_ref = refs[n_in + 3:n_in + 5] if has_bias else (None, None)
        qi = pl.program_id(2)

        @pl.when(qi == 0)
        def _():
            dk_ref[...] = jnp.zeros_like(dk_ref)
            dv_ref[...] = jnp.zeros_like(dv_ref)
            if has_bias:
                dck_ref[...] = jnp.zeros_like(dck_ref)

        lane = lax.broadcasted_iota(jnp.int32, (QC, LANES), 1)
        ones8 = jnp.ones((8, LANES), BF16)
        qb, dob, delta, lse_r = [], [], [], []
        for x in range(2):
            hm = jnp.logical_and(lane >= x * HEAD_V, lane < (x + 1) * HEAD_V)
            qb.append([]), dob.append([]), delta.append([]), lse_r.append([])
            for c in range(nc):
                rows = slice(c * QC, (c + 1) * QC)
                do_c = jnp.where(hm, do_ref[rows, :], 0.0)
                prod = do_c * o_ref[rows, :]
                hi = prod.astype(BF16)
                lo = (prod - hi.astype(F32)).astype(BF16)
                d8 = (lax.dot_general(ones8, hi, NT, preferred_element_type=F32)
                      + lax.dot_general(ones8, lo, NT, preferred_element_type=F32))
                delta[x].append(d8[0:1, :])
                dob[x].append(do_c.astype(BF16))
                qb[x].append(q_ref[rows, x * HP:(x + 1) * HP].astype(BF16))
                lse_r[x].append(lse_ref[x:x + 1, rows])
        key_in = lax.broadcasted_iota(jnp.int32, (TQ, QC), 0)
        qry_in = lax.broadcasted_iota(jnp.int32, (TQ, QC), 1)

        def block(j, carry, masked):
            ks = pl.multiple_of(j * TQ, TQ)
            vb = v_ref[pl.ds(ks, TQ), :].astype(BF16)
            new, key_sums = [], []
            for x in range(2):
                k32 = k_ref[pl.ds(ks, TQ), x * HP:(x + 1) * HP]
                kx, kt = k32.astype(BF16), k32.T.astype(BF16)
                ckx = _col(ck_ref[pl.ds(ks, TQ), :], x) if has_bias else None
                dk_acc = jnp.zeros((TQ, HP), F32)
                dv_acc = jnp.zeros((TQ, LANES), F32)
                key_sum = jnp.zeros((TQ, 1), F32)
                for c in range(nc):
                    dqt, dcq = carry[x * nc + c]
                    s = lax.dot_general(kx, qb[x][c], NT, preferred_element_type=F32) * scale
                    if has_bias:
                        s = s + (cq_ref[x:x + 1, c * QC:(c + 1) * QC] - ckx)
                    if masked:
                        s = jnp.where(qry_in + c * QC >= key_in, s, NEG_INF)
                    p = jnp.exp(s - lse_r[x][c])
                    dp = lax.dot_general(vb, dob[x][c], NT, preferred_element_type=F32)
                    ds = p * (dp - delta[x][c])
                    dsb = ds.astype(BF16)
                    dk_acc = dk_acc + lax.dot_general(dsb, qb[x][c], NN, preferred_element_type=F32)
                    dv_acc = dv_acc + lax.dot_general(p.astype(BF16), dob[x][c], NN, preferred_element_type=F32)
                    dqt = dqt + lax.dot_general(kt, dsb, NN, preferred_element_type=F32)
                    if has_bias:
                        dcq = dcq + jnp.sum(ds, axis=0, keepdims=True)
                        key_sum = key_sum + jnp.sum(ds, axis=1, keepdims=True)
                    new.append((dqt, dcq))
                dk_ref[pl.ds(ks, TQ), x * HP:(x + 1) * HP] += dk_acc * scale
                dv_ref[pl.ds(ks, TQ), :] += dv_acc
                key_sums.append(key_sum)
            if has_bias:
                dck_ref[pl.ds(ks, TQ), :] -= _two_cols(key_sums[0], key_sums[1])
            return tuple(new)

        init = tuple((jnp.zeros((HP, QC), F32), jnp.zeros((1, QC), F32)) for _ in range(2 * nc))
        carry = lax.fori_loop(0, qi, lambda j, cr: block(j, cr, False), init)
        carry = block(qi, carry, True)
        for x in range(2):
            for c in range(nc):
                dq_ref[c * QC:(c + 1) * QC, x * HP:(x + 1) * HP] = carry[x * nc + c][0].T * scale
        if has_bias:
            dcq_ref[...] = _rows2(jnp.concatenate([carry[c][1] for c in range(nc)], axis=1),
                                  jnp.concatenate([carry[nc + c][1] for c in range(nc)], axis=1))

    qblk, kblk, vblk, oblk, rowblk, colblk = _attn_specs(seq, v_blk0, has_bias)
    in_specs = [qblk, kblk, vblk, oblk, oblk, rowblk]
    out_specs = [qblk, kblk, pl.BlockSpec((None, seq, LANES), lambda b, p, i: (b, 0, p))]
    out_shape = [jax.ShapeDtypeStruct((bsz, seq, QKW), F32), jax.ShapeDtypeStruct((bsz, seq, QKW), F32),
                 jax.ShapeDtypeStruct((bsz, seq, 2 * LANES), F32)]
    args = [q, k, v, o, do, lse]
    if has_bias:
        in_specs += [rowblk, colblk]
        out_specs += [colblk, rowblk]
        out_shape += [jax.ShapeDtypeStruct((bsz, 2, seq, 2), F32), jax.ShapeDtypeStruct((bsz, 2, 2, seq), F32)]
        args += [c_rows, c_cols]
    return _pcall(body, name=name, grid=(bsz, 2, nq), in_specs=in_specs, out_specs=out_specs,
                  out_shape=out_shape, rider=rider)(*args)


def _coords(idx):
    return (idx // 4, (idx // 2) % 2, idx % 2)


class _Xfer:
    def __init__(self, srcs, out_shapes, plan):
        self.srcs, self.out_shapes, self.plan = list(srcs), list(out_shapes), plan
        self.n = len(self.srcs)
        me0 = jnp.int32(0)
        n_pieces = [len(plan(a, me0, me0)) for a in range(self.n)]
        self.offs = np.concatenate([[0], np.cumsum(n_pieces)]).astype(int)
        self.total = int(self.offs[-1])

    def sems(self):
        remote = (N_DEV - 1) * self.total
        return [pltpu.SemaphoreType.DMA((remote,)), pltpu.SemaphoreType.DMA((remote,)),
                pltpu.SemaphoreType.DMA((self.total,))]

    def copies(self, src_refs, out_refs, send_sems, recv_sems, local_sems):
        me = 4 * lax.axis_index("x") + 2 * lax.axis_index("y") + lax.axis_index("c")
        out = []
        for a in range(self.n):
            for pi, (si, di) in enumerate(self.plan(a, me, me)):
                out.append((pltpu.make_async_copy(src_refs[a].at[si], out_refs[a].at[di],
                                                  local_sems.at[self.offs[a] + pi]), None))
        for kk in range(1, N_DEV):
            dest = (me + kk) % N_DEV
            src_dev = (me + N_DEV - kk) % N_DEV
            for a in range(self.n):
                pieces = self.plan(a, me, dest)
                landing = self.plan(a, src_dev, me)
                for pi, ((si, di), (_, li)) in enumerate(zip(pieces, landing)):
                    sem = (kk - 1) * self.total + self.offs[a] + pi
                    mk = functools.partial(pltpu.make_async_remote_copy, src_ref=src_refs[a].at[si],
                                           send_sem=send_sems.at[sem], recv_sem=recv_sems.at[sem],
                                           device_id=_coords(dest), device_id_type=pl.DeviceIdType.MESH)
                    out.append((mk(dst_ref=out_refs[a].at[di]), mk(dst_ref=out_refs[a].at[li])))
        return out

    @staticmethod
    def start(copies):
        for cp, _ in copies:
            cp.start()

    @staticmethod
    def wait(copies):
        for cp, rc in copies:
            if rc is None:
                cp.wait()
            else:
                cp.wait_send()
                rc.wait_recv()


def _exchange(name, xf):
    n = xf.n

    def body(*refs):
        copies = xf.copies(refs[:n], refs[n:2 * n], *refs[2 * n:])
        xf.start(copies)
        xf.wait(copies)

    any_spec = pl.BlockSpec(memory_space=pl.ANY)
    return pl.pallas_call(body, name=name, in_specs=[any_spec] * n, out_specs=[any_spec] * n,
                          out_shape=[jax.ShapeDtypeStruct(s, d) for s, d in xf.out_shapes],
                          scratch_shapes=xf.sems(),
                          compiler_params=pltpu.CompilerParams(has_side_effects=True))(*xf.srcs)


class _Gather2(_Xfer):
    def copies(self, src_refs, out_refs, send_sems, recv_sems, local_sems):
        x, y, c = lax.axis_index("x"), lax.axis_index("y"), lax.axis_index("c")
        num = lambda px, py, pc: 4 * px + 2 * py + pc
        me, sib = (x, y, c), (x, y, 1 - c)
        chips = [(1 - x, y), (x, 1 - y), (1 - x, 1 - y)]
        local, first, passed, landing = [], [], [], []
        for a in range(self.n):
            mine = self.plan(a, num(*me), num(*me))
            for pi, (si, di) in enumerate(mine):
                piece = self.offs[a] + pi

                def remote(k, src, dst, to, a=a, piece=piece):
                    return pltpu.make_async_remote_copy(
                        src_ref=src, dst_ref=dst, send_sem=send_sems.at[k * self.total + piece],
                        recv_sem=recv_sems.at[k * self.total + piece], device_id=to, device_id_type=pl.DeviceIdType.MESH)

                def slot(dev, a=a, pi=pi):
                    return out_refs[a].at[self.plan(a, num(*dev), num(*dev))[pi][1]]

                local.append(pltpu.make_async_copy(src_refs[a].at[si], out_refs[a].at[di], local_sems.at[piece]))
                first.append(remote(0, src_refs[a].at[si], out_refs[a].at[di], sib))
                landing.append([remote(0, slot(sib), slot(sib), sib)])
                for j, chip in enumerate(chips):
                    first.append(remote(1 + j, src_refs[a].at[si], out_refs[a].at[di], (*chip, c)))
                    passed.append((remote(1 + j, slot((*chip, c)), slot((*chip, c)), (*chip, c)),
                                   remote(4 + j, slot((*chip, c)), slot((*chip, c)), sib)))
                    landing[-1].append(remote(4 + j, slot((*chip, 1 - c)), slot((*chip, 1 - c)), sib))
        return local, first, passed, landing

    @staticmethod
    def start(copies):
        local, first, _, _ = copies
        for cp in local + first:
            cp.start()

    @staticmethod
    def wait(copies):
        local, first, passed, landing = copies
        for arrival, forward in passed:
            arrival.wait_recv()
            forward.start()
        for group in landing:
            for cp in group:
                cp.wait_recv()
        for cp in first + [fw for _, fw in passed]:
            cp.wait_send()
        for cp in local:
            cp.wait()


def _rows_of(j, n):
    return pl.ds(pl.multiple_of(j * n, n), n)


def _ffn_cols(j, half):
    return pl.multiple_of((2 * (j % 4) + half) * (2 * FF_BLK) + (j // 4) * FF_BLK, FF_BLK)


def _gather_plan(shards):
    names = list(shards)

    def plan(a, src_dev, _dest):
        nm = names[a]
        j = src_dev
        if nm in ("w_in", "w_uq", "w_ukv", "w_glu", "w_out", "w_down"):
            rows = shards[nm].shape[0]
            return [((slice(None), slice(None)), (_rows_of(j, rows), slice(None)))]
        if nm == "w_branch":
            return [((slice(None), slice(None), slice(None)), (slice(None), slice(None), _rows_of(j, LANES)))]
        if nm in ("w_up", "conv"):
            return [((slice(None), pl.ds(h * FF_BLK, FF_BLK)), (slice(None), pl.ds(_ffn_cols(j, h), FF_BLK)))
                    for h in range(2)]
        raise KeyError(nm)

    return names, plan


def _scatter_plan(names, shard_shapes):
    def plan(a, src_dev, dest):
        nm = names[a]
        j = dest
        if nm in ("w_in", "w_uq", "w_ukv", "w_glu", "w_out", "w_down"):
            rows = shard_shapes[nm][0]
            return [((_rows_of(j, rows), slice(None)), (src_dev, slice(None), slice(None)))]
        if nm == "w_branch":
            return [((slice(None), slice(None), _rows_of(j, LANES)), (src_dev, slice(None), slice(None), slice(None)))]
        if nm in ("w_up", "conv"):
            return [((slice(None), pl.ds(_ffn_cols(j, h), FF_BLK)), (src_dev, slice(None), pl.ds(h * FF_BLK, FF_BLK)))
                    for h in range(2)]
        if nm in ("w_up:0", "w_up:1"):
            h = int(nm[-1])
            return [((slice(None), pl.ds(_ffn_cols(j, h), FF_BLK)), (src_dev, slice(None), slice(None)))]
        if nm == "small":
            return [((slice(None), slice(None)), (src_dev, slice(None), slice(None)))]
        raise KeyError(nm)

    return plan


def _pad_in_cols(w):
    lead = w.shape[:-1]
    parts, pos = [], 0
    for dst, src, width in IN_SEGS:
        if dst > pos:
            parts.append(jnp.zeros(lead + (dst - pos,), w.dtype))
        parts.append(w[..., src:src + width])
        pos = dst + width
    return jnp.concatenate(parts, axis=-1)


def _unpad_in_cols(w):
    order = sorted(IN_SEGS, key=lambda s: s[1])
    return jnp.concatenate([w[..., dst:dst + width] for dst, _, width in order], axis=-1)


def _pad_ff(w, axis):
    n = w.shape[axis] // FF_HALF
    parts = []
    for h in range(n):
        piece = lax.slice_in_dim(w, h * FF_HALF, (h + 1) * FF_HALF, axis=axis)
        zshape = list(w.shape)
        zshape[axis] = FF_BLK - FF_HALF
        parts += [piece, jnp.zeros(zshape, w.dtype)]
    return jnp.concatenate(parts, axis=axis)


def _unpad_ff(w, axis):
    n = w.shape[axis] // FF_BLK
    return jnp.concatenate([lax.slice_in_dim(w, h * FF_BLK, h * FF_BLK + FF_HALF, axis=axis) for h in range(n)],
                           axis=axis)


def _to_scan_cols(re_part, im_part):
    lead = re_part.shape[:-1]
    nb = S5_N // SCAN_W
    r = re_part.reshape(lead + (nb, SCAN_W))
    i = im_part.reshape(lead + (nb, SCAN_W))
    return jnp.concatenate([r, i], axis=-1).reshape(lead + (2 * S5_N,))


def _from_scan_cols(x):
    lead = x.shape[:-1]
    nb = S5_N // SCAN_W
    y = x.reshape(lead + (nb, 2, SCAN_W))
    return y[..., 0, :].reshape(lead + (S5_N,)), y[..., 1, :].reshape(lead + (S5_N,))


def _block_diag_in(b):
    eye = jnp.eye(S5_GROUPS, dtype=b.dtype)
    return jnp.einsum("gph,gk->ghkp", b, eye).reshape(S5_WIDTH, S5_N)


def _block_diag_in_t(m):
    eye = jnp.eye(S5_GROUPS, dtype=m.dtype)
    return jnp.einsum("ghkp,gk->gph", m.reshape(S5_GROUPS, S5_GROUP_CH, S5_GROUPS, S5_STATE), eye)


def _block_diag_out(c):
    eye = jnp.eye(S5_GROUPS, dtype=c.dtype)
    return jnp.einsum("ghp,gk->gpkh", c, eye).reshape(S5_N, S5_WIDTH)


def _block_diag_out_t(m):
    eye = jnp.eye(S5_GROUPS, dtype=m.dtype)
    return jnp.einsum("gpkh,gk->ghp", m.reshape(S5_GROUPS, S5_STATE, S5_GROUPS, S5_GROUP_CH), eye)


def _pad_heads(g, width):
    g = jnp.broadcast_to(g.reshape(-1, width), (N_HEADS, width))
    return jnp.pad(g, ((0, 0), (0, HP - width))).reshape(1, QKW)


def _prep_layer(l, gw, small, consts):
    e_mat, p_mat, r_mat, pl_mat = consts
    p = {}
    p["attn_g"] = small["attn_norm_g"][l].reshape(1, D_MODEL)
    p["ffn_g"] = small["ffn_norm_g"][l].reshape(1, D_MODEL)
    p["qlg"] = small["q_lat_norm_g"][l].reshape(1, MLA_Q_RANK)
    p["kvlg"] = small["kv_lat_norm_g"][l].reshape(1, MLA_KV_RANK)
    p["mqg"] = _pad_heads(small["mla_q_norm_g"][l], MLA_QK_DIM)
    p["mkg"] = _pad_heads(small["mla_k_norm_g"][l], MLA_QK_DIM)
    p["fqg"] = _pad_heads(small["fox_q_norm_g"][l], HEAD_V)
    p["fkg"] = _pad_heads(small["fox_k_norm_g"][l], HEAD_V)
    w_uq = gw["w_uq"].reshape(MLA_Q_RANK, N_HEADS, MLA_QK_DIM)
    p["w_uq"] = jnp.pad(w_uq, ((0, 0), (0, 0), (0, HP - MLA_QK_DIM))).reshape(MLA_Q_RANK, QKW)
    w_ukv = gw["w_ukv"].reshape(MLA_KV_RANK, N_HEADS, MLA_NOPE_DIM + HEAD_V)
    p["w_uk"] = jnp.pad(w_ukv[..., :MLA_NOPE_DIM], ((0, 0), (0, 0), (0, HP - MLA_NOPE_DIM))).reshape(MLA_KV_RANK, QKW)
    p["w_uv"] = w_ukv[..., MLA_NOPE_DIM:].reshape(MLA_KV_RANK, N_HEADS * HEAD_V)
    p["w_glu"] = gw["w_glu"]
    p["b_glu"] = small["s5_b_glu"][l].reshape(1, S5_WIDTH)
    p["d_row"] = small["s5_d"][l].reshape(1, S5_WIDTH)
    p["lam_re"] = small["s5_lambda_re"][l].reshape(S5_N, 1)
    p["lam_im"] = small["s5_lambda_im"][l].reshape(S5_N, 1)
    p["log_step"] = jnp.repeat(small["s5_log_step"][l], S5_STATE).reshape(S5_N, 1)
    p["b_re"] = small["s5_b_re"][l].reshape(S5_N, S5_GROUP_CH)
    p["b_im"] = small["s5_b_im"][l].reshape(S5_N, S5_GROUP_CH)
    c_re, c_im = small["s5_c_re"][l], small["s5_c_im"][l]
    p["c_blk"] = _to_scan_cols(_block_diag_out(c_re).T, -_block_diag_out(c_im).T).T
    p["fbias"] = small["fox_f_bias"][l]
    for nm in ("w_in", "w_out", "w_branch"):
        if nm in gw:
            p[nm] = gw[nm]
    return p


def _s5_param_call(p, l):
    ins = [p["lam_re"], p["lam_im"], p["log_step"], p["b_re"], p["b_im"]]
    outs = [((S5_N, 1), F32, (S5_N, 1), lambda i: (0, 0), None)] * 2 + \
           [((S5_N, S5_GROUP_CH), F32, (S5_N, S5_GROUP_CH), lambda i: (0, 0), None)] * 2
    return _tile_call(_s5_params, "s5_params", (1,), [_par(a) for a in ins], outs)


def _s5_param_bwd_call(p, cts):
    ins = [p["lam_re"], p["lam_im"], p["log_step"], p["b_re"], p["b_im"]]

    def fn(lr, li, ls, br, bi, g0, g1, g2, g3):
        _, vjp = jax.vjp(_s5_params, lr, li, ls, br, bi)
        return vjp((g0, g1, g2, g3))

    outs = [((S5_N, 1), F32, (S5_N, 1), lambda i: (0, 0), None)] * 3 + \
           [((S5_N, S5_GROUP_CH), F32, (S5_N, S5_GROUP_CH), lambda i: (0, 0), None)] * 2
    return _tile_call(fn, "s5_params_bwd", (1,), [_par(a) for a in ins + list(cts)], outs)


def _layer_fwd(x_prev, f_prev, p, tabs, consts, bsz, riders):
    arrived = {}

    def rider_of(name):
        return riders[name][0] if name in riders else None

    def hosted(name, res, n_out):
        if name in riders:
            arrived.update(zip(riders[name][1], res[n_out:]))
        return res[:n_out]

    t = x_prev.shape[0]
    seq = t // bsz
    nt = t // TM
    cos, sin = tabs
    e_mat, p_mat, r_mat, pl_mat = consts
    sv = {}

    if f_prev is None:
        x = x_prev
        h = hosted("norm_first", _tile_call(lambda xv, g: _rms(xv, g), "norm_first", (nt,),
                                            [_row(x), _par(p["attn_g"])], [_orow(t, D_MODEL, BF16)],
                                            rider=rider_of("norm_first")), 1)[0]
        p.update({n: arrived.pop(n) for n in FIRST_LATE_WEIGHTS})
    else:
        x, h = _tile_call(lambda xv, fv, g: (xv + fv, _rms(xv + fv, g)), "norm_attn", (nt,),
                          [_row(x_prev), _row(f_prev), _par(p["attn_g"])], [_orow(t, D_MODEL), _orow(t, D_MODEL, BF16)])
    sv["x"], sv["h"] = x, h
    proj = _mm(h, p["w_in"], "nn", "mm_in")
    sv["proj"] = proj

    def pre_fn(pa, cosv, sinv, qlg, w_uq, kvlg, w_uk, w_uv, mqg, mkg, fqg, fkg, em, pm, rm, plm):
        return _mixer_pre(pa[:, O_CQ:O_CQ + MLA_Q_RANK], pa[:, O_CKV:O_CKV + MLA_KV_RANK], pa[:, O_KR:O_KR + LANES],
                          pa[:, O_FQ:O_FQ + 256], pa[:, O_FK:O_FK + 256], qlg, w_uq, kvlg, w_uk, w_uv,
                          mqg, mkg, fqg, fkg, cosv, sinv, em, pm, rm, plm)

    pre_params = [p["qlg"], p["w_uq"], p["kvlg"], p["w_uk"], p["w_uv"], p["mqg"], p["mkg"], p["fqg"], p["fkg"],
                  e_mat, p_mat, r_mat, pl_mat]
    q_m, k_m, v_m, q_f, k_f = hosted("mixer_pre", _tile_call(
        pre_fn, "mixer_pre", (t // TM_H,), [_row_h(proj, PA), _row_h(cos), _row_h(sin)] + [_par(a) for a in pre_params],
        [_orow_h(t, QKW), _orow_h(t, QKW), _orow_h(t, 2 * LANES), _orow_h(t, QKW), _orow_h(t, QKW)],
        rider=rider_of("mixer_pre")), 5)
    sv.update(q_m=q_m, k_m=k_m, v_m=v_m, q_f=q_f, k_f=k_f)

    z = proj[:, O_FL:O_FL + N_HEADS].reshape(bsz, seq, N_HEADS).transpose(0, 2, 1).reshape(bsz * N_HEADS, seq)
    fb = jnp.tile(p["fbias"], bsz).reshape(bsz * N_HEADS, 1)
    cum = _forget_fwd(z, fb, "forget_fwd")
    ck = cum.reshape(bsz, 2, 2, seq)
    cq = ck.transpose(0, 1, 3, 2)
    sv.update(z=z, fb=fb, cq=cq, ck=ck)

    r3 = lambda a: a.reshape(bsz, seq, a.shape[-1])
    o_m, lse_m = hosted("attn_mla_fwd", _attn_fwd(r3(q_m), r3(k_m), r3(v_m), 0, None, None, MLA_QK_DIM ** -0.5,
                                                  "attn_mla_fwd", rider_of("attn_mla_fwd")), 2)
    o_f, lse_f = hosted("attn_fox_fwd", _attn_fwd(r3(q_f), r3(k_f), r3(proj), O_FV // LANES, cq, ck, HEAD_V ** -0.5,
                                                  "attn_fox_fwd", rider_of("attn_fox_fwd")), 2)
    o_m, o_f = o_m.reshape(t, 2 * LANES), o_f.reshape(t, 2 * LANES)
    sv.update(o_m=o_m, o_f=o_f, lse_m=lse_m, lse_f=lse_f)

    lbar_re, lbar_im, bbar_re, bbar_im = _s5_param_call(p, 0)
    lbar = _to_scan_cols(lbar_re.reshape(1, S5_N), lbar_im.reshape(1, S5_N))
    b_blk = _to_scan_cols(_block_diag_in(bbar_re.reshape(S5_GROUPS, S5_STATE, S5_GROUP_CH)),
                          _block_diag_in(bbar_im.reshape(S5_GROUPS, S5_STATE, S5_GROUP_CH)))
    sv.update(lbar=lbar, b_blk=b_blk)
    bu = _tile_call(lambda u, bb: bdot(u, bb), "s5_bu", (nt,), [_row(proj, S5_WIDTH, O_U // S5_WIDTH), _par(b_blk)],
                    [_orow(t, 2 * S5_N)])[0]
    state = hosted("s5_scan_fwd", _scan_fwd(bu, lbar, bsz, "s5_scan_fwd", rider_of("s5_scan_fwd")), 1)[0]
    sv["state"] = state
    post_params = [p["c_blk"], p["d_row"], p["w_glu"], p["b_glu"]]
    o_s = _tile_call(_s5_post, "s5_post", (nt,),
                     [_row(state), _row(proj, S5_WIDTH, O_U // S5_WIDTH)] + [_par(a) for a in post_params],
                     [_orow(t, S5_WIDTH)])[0]
    sv["o_s"] = o_s

    def merge_fn(oa, ob, oc, ga, gb, gc, wb):
        return _merge(oa, ob, oc, ga, gb, gc, wb[0], wb[1], wb[2])

    gate_specs = [_row(proj, D_MODEL, PA // D_MODEL + n) for n in range(3)]
    merged = _tile_call(merge_fn, "merge", (nt,), [_row(o_m), _row(o_f), _row(o_s)] + gate_specs + [_par(p["w_branch"])],
                        [_orow(t, D_MODEL, BF16)])[0]
    sv["merged"] = merged
    attn_out = _mm(merged, p["w_out"], "nn", "mm_out")

    x1, h2 = _tile_call(lambda xv, av, g: (xv + av, _rms(xv + av, g)), "norm_ffn", (nt,),
                        [_row(x), _row(attn_out), _par(p["ffn_g"])], [_orow(t, D_MODEL), _orow(t, D_MODEL, BF16)])
    sv["x1"], sv["h2"] = x1, h2
    p.update({n: arrived.pop(n) for n in FFN_WEIGHTS})
    up = _mm(h2, p["w_up"], "nn", "mm_up")
    sv["up"] = up

    def ffn_fn(upv, cw):
        return _ffn_mid(upv[:, :FF_BLK], upv[:, FF_BLK:], cw[0:1, :FF_BLK], cw[1:2, :FF_BLK], cw[2:3, :FF_BLK],
                        cw[0:1, FF_BLK:], cw[1:2, FF_BLK:], cw[2:3, FF_BLK:])

    nblk = FF_PW // FF_BLK
    act = _tile_call(ffn_fn, "ffn_mid", (nblk, bsz),
                     [(up, (seq, 2 * FF_BLK), lambda j, b: (b, j)), (p["conv"], (3, 2 * FF_BLK), lambda j, b: (0, j))],
                     [((t, FF_PW), BF16, (seq, FF_BLK), lambda j, b: (b, j), None)])[0]
    sv["act"] = act
    ffn_out = _mm(act, p["w_down"], "nn", "mm_down")
    return x1, ffn_out, sv, arrived


def _layer_bwd(dx2, p, sv, tabs, consts, bsz, scatter, carry, last_layer):
    lands, carry_lands = {}, {}

    def riding(host, g):
        names = BWD_RIDERS[host] + (("w_in",) if host == "mm_in_dx" and last_layer else ())
        return scatter({n: g[n.split(":")[0]] for n in names}), names

    def landed(host, res, n_out):
        names = BWD_RIDERS[host] + (("w_in",) if host == "mm_in_dx" and last_layer else ())
        lands.update(zip(names, res[n_out:]))
        return res[:n_out]

    t = dx2.shape[0]
    seq = t // bsz
    nt = t // TM
    cos, sin = tabs
    e_mat, p_mat, r_mat, pl_mat = consts
    g = {}

    g["w_down"] = _mm(sv["act"], dx2, "tn", "mm_down_dw", out_dtype=BF16)
    dact = _mm(dx2, p["w_down"], "nt", "mm_down_dx")

    def ffn_bwd_fn(upv, cw, da):
        args = (upv[:, :FF_BLK], upv[:, FF_BLK:], cw[0:1, :FF_BLK], cw[1:2, :FF_BLK], cw[2:3, :FF_BLK],
                cw[0:1, FF_BLK:], cw[1:2, FF_BLK:], cw[2:3, FF_BLK:])
        _, vjp = jax.vjp(_ffn_mid, *args)
        dg, dv, g0, g1, g2, v0, v1, v2 = vjp(da)
        return (jnp.concatenate([dg, dv], axis=1), jnp.concatenate([g0, v0], axis=1),
                jnp.concatenate([g1, v1], axis=1), jnp.concatenate([g2, v2], axis=1))

    nblk = FF_PW // FF_BLK
    cw_out = ((1, UP_W), F32, (1, 2 * FF_BLK), lambda j, b: (0, j), (1,))
    res = _tile_call(
        ffn_bwd_fn, "ffn_mid_bwd", (nblk, bsz),
        [(sv["up"], (seq, 2 * FF_BLK), lambda j, b: (b, j)), (p["conv"], (3, 2 * FF_BLK), lambda j, b: (0, j)),
         (dact, (seq, FF_BLK), lambda j, b: (b, j))],
        [((t, UP_W), BF16, (seq, 2 * FF_BLK), lambda j, b: (b, j), None), cw_out, cw_out, cw_out],
        rider=scatter(carry) if carry else None)
    dup, dc0, dc1, dc2 = res[:4]
    carry_lands.update(zip(carry, res[4:]))
    g["conv"] = jnp.concatenate([dc0, dc1, dc2], axis=0)
    g["w_up"] = _mm(sv["h2"], dup, "tn", "mm_up_dw", out_dtype=BF16)
    dh2 = landed("mm_up_dx", _mm(dup, p["w_up"], "nt", "mm_up_dx", rider=riding("mm_up_dx", g)[0]), 1)[0]

    def norm_bwd_fn(xv, gv, dh, dres):
        _, vjp = jax.vjp(_rms, xv, gv)
        dxv, dg = vjp(dh)
        return dres + dxv, dg

    d1, g["ffn_g"] = _tile_call(norm_bwd_fn, "norm_bwd", (nt,),
                                [_row(sv["x1"]), _par(p["ffn_g"]), _row(dh2), _row(dx2)],
                                [_orow(t, D_MODEL), _oacc((1, D_MODEL))])

    g["w_out"] = _mm(sv["merged"], d1, "tn", "mm_out_dw", out_dtype=BF16)
    dmerged = _mm(d1, p["w_out"], "nt", "mm_out_dx")

    def merge_bwd_fn(oa, ob, oc, ga, gb, gc, wb, dm):
        wf = wb.astype(F32)
        _, vjp = jax.vjp(_merge, oa, ob, oc, ga, gb, gc, wf[0], wf[1], wf[2])
        doa, dob, doc, dga, dgb, dgc, dwa, dwb, dwc = vjp(dm)
        return doa, dob, doc, jnp.concatenate([dga, dgb, dgc], axis=1), jnp.stack([dwa, dwb, dwc])

    proj = sv["proj"]
    gate_specs = [_row_h(proj, D_MODEL, PA // D_MODEL + n) for n in range(3)]
    do_m, do_f, do_s, dgl, g["w_branch"] = _tile_call(
        merge_bwd_fn, "merge_bwd", (t // TM_H,),
        [_row_h(sv["o_m"]), _row_h(sv["o_f"]), _row_h(sv["o_s"])] + gate_specs + [_par(p["w_branch"]), _row_h(dmerged)],
        [_orow_h(t, 2 * LANES), _orow_h(t, 2 * LANES), _orow_h(t, S5_WIDTH), _orow_h(t, 3 * D_MODEL),
         _oacc((3, S5_WIDTH, D_MODEL))])

    def post_bwd_fn(st, u, cb, dr, wg, bg, do):
        _, vjp = jax.vjp(_s5_post, st, u, cb.astype(F32), dr, wg.astype(F32), bg)
        return vjp(do)

    u_spec = _row(proj, S5_WIDTH, O_U // S5_WIDTH)
    dstate, du1, dc_blk, g_d, g["w_glu"], g_bglu = _tile_call(
        post_bwd_fn, "s5_post_bwd", (t // TM_H,),
        [_row_h(sv["state"]), _row_h(proj, S5_WIDTH, O_U // S5_WIDTH), _par(p["c_blk"]), _par(p["d_row"]),
         _par(p["w_glu"]), _par(p["b_glu"]), _row_h(do_s)],
        [_orow_h(t, 2 * S5_N), _orow_h(t, S5_WIDTH), _oacc((2 * S5_N, S5_WIDTH)), _oacc((1, S5_WIDTH)),
         _oacc((S5_WIDTH, S5_WIDTH)), _oacc((1, S5_WIDTH))])
    dbu, dlbar = landed("s5_scan_bwd", _scan_bwd(dstate, sv["state"], sv["lbar"], bsz, "s5_scan_bwd",
                                                 riding("s5_scan_bwd", g)[0]), 2)

    def bu_bwd_fn(u, bb, dbv, du_in):
        _, vjp = jax.vjp(bdot, u, bb)
        du, dbb = vjp(dbv)
        return du_in + du, dbb

    du, db_blk = _tile_call(bu_bwd_fn, "s5_bu_bwd", (nt,), [u_spec, _par(sv["b_blk"]), _row(dbu), _row(du1)],
                            [_orow(t, S5_WIDTH), _oacc((S5_WIDTH, 2 * S5_N))])
    dlr, dli = _from_scan_cols(dlbar)
    dbr, dbi = _from_scan_cols(db_blk)
    cts = (dlr.reshape(S5_N, 1), dli.reshape(S5_N, 1),
           _block_diag_in_t(dbr).reshape(S5_N, S5_GROUP_CH), _block_diag_in_t(dbi).reshape(S5_N, S5_GROUP_CH))
    g_lr, g_li, g_ls, g_br, g_bi = _s5_param_bwd_call(p, cts)
    dc_re, dc_im = _from_scan_cols(dc_blk.T)
    g["s5_lambda_re"] = g_lr.reshape(S5_GROUPS, S5_STATE)
    g["s5_lambda_im"] = g_li.reshape(S5_GROUPS, S5_STATE)
    g["s5_log_step"] = jnp.sum(g_ls.reshape(S5_GROUPS, S5_STATE), axis=1)
    g["s5_b_re"] = g_br.reshape(S5_GROUPS, S5_STATE, S5_GROUP_CH)
    g["s5_b_im"] = g_bi.reshape(S5_GROUPS, S5_STATE, S5_GROUP_CH)
    g["s5_c_re"] = _block_diag_out_t(dc_re.T)
    g["s5_c_im"] = -_block_diag_out_t(dc_im.T)
    g["s5_d"] = g_d.reshape(S5_GROUPS, S5_GROUP_CH)
    g["s5_b_glu"] = g_bglu.reshape(S5_WIDTH)

    r3 = lambda a: a.reshape(bsz, seq, a.shape[-1])
    dq_m, dk_m, dv_m = landed("attn_mla_bwd", _attn_bwd(
        r3(sv["q_m"]), r3(sv["k_m"]), r3(sv["v_m"]), 0, r3(sv["o_m"]), r3(do_m), sv["lse_m"], None, None,
        MLA_QK_DIM ** -0.5, "attn_mla_bwd", riding("attn_mla_bwd", g)[0]), 3)
    dq_f, dk_f, dv_f, dcq, dck = landed("attn_fox_bwd", _attn_bwd(
        r3(sv["q_f"]), r3(sv["k_f"]), r3(proj), O_FV // LANES, r3(sv["o_f"]), r3(do_f), sv["lse_f"], sv["cq"], sv["ck"],
        HEAD_V ** -0.5, "attn_fox_bwd", riding("attn_fox_bwd", g)[0]), 5)
    dcum = (dck + dcq.transpose(0, 1, 3, 2)).reshape(bsz * N_HEADS, seq)
    dz, dfb = _forget_bwd(dcum, sv["z"], sv["fb"], "forget_bwd")
    g["fox_f_bias"] = jnp.sum(dfb.reshape(bsz, N_HEADS), axis=0)
    dfl = jnp.pad(dz.reshape(bsz, N_HEADS, seq).transpose(0, 2, 1).reshape(t, N_HEADS), ((0, 0), (0, LANES - N_HEADS)))

    def pre_bwd_fn(pa, cosv, sinv, qlg, w_uq, kvlg, w_uk, w_uv, mqg, mkg, fqg, fkg, em, pm, rm, plm,
                   gq, gk, gv, gqf, gkf, gvf, gu, gfl, ggl):
        f = functools.partial(_mixer_pre, cos=cosv, sin=sinv, e_mat=em, p_mat=pm, r_mat=rm, pl_mat=plm)
        prim = (pa[:, O_CQ:O_CQ + MLA_Q_RANK], pa[:, O_CKV:O_CKV + MLA_KV_RANK], pa[:, O_KR:O_KR + LANES],
                pa[:, O_FQ:O_FQ + 256], pa[:, O_FK:O_FK + 256], qlg, w_uq.astype(F32), kvlg, w_uk.astype(F32),
                w_uv.astype(F32), mqg, mkg, fqg, fkg)
        _, vjp = jax.vjp(lambda *a: f(*a), *prim)
        dcq_, dckv, dkr, dfq, dfk, dqlg, dwuq, dkvlg, dwuk, dwuv, dmqg, dmkg, dfqg, dfkg = vjp((gq, gk, gv, gqf, gkf))
        zpad = jnp.zeros((pa.shape[0], PA - O_CQ - MLA_Q_RANK), F32)
        dproj = jnp.concatenate([dckv, dfq, dfk, gvf, gu, dkr, gfl, dcq_, zpad, ggl], axis=1)
        return dproj, dqlg, dwuq, dkvlg, dwuk, dwuv, dmqg, dmkg, dfqg, dfkg

    pre_params = [p["qlg"], p["w_uq"], p["kvlg"], p["w_uk"], p["w_uv"], p["mqg"], p["mkg"], p["fqg"], p["fkg"],
                  e_mat, p_mat, r_mat, pl_mat]
    cts_in = [dq_m.reshape(t, QKW), dk_m.reshape(t, QKW), dv_m.reshape(t, 2 * LANES), dq_f.reshape(t, QKW),
              dk_f.reshape(t, QKW), dv_f.reshape(t, 2 * LANES), du, dfl, dgl]
    (dproj, g_qlg, g_wuq, g_kvlg, g_wuk, g_wuv, g_mqg, g_mkg, g_fqg, g_fkg) = _tile_call(
        pre_bwd_fn, "mixer_pre_bwd", (t // TM_H,),
        [_row_h(proj, PA), _row_h(cos), _row_h(sin)] + [_par(a) for a in pre_params] + [_row_h(a) for a in cts_in],
        [_orow_h(t, PW, BF16), _oacc((1, MLA_Q_RANK)), _oacc((MLA_Q_RANK, QKW)), _oacc((1, MLA_KV_RANK)),
         _oacc((MLA_KV_RANK, QKW)), _oacc((MLA_KV_RANK, N_HEADS * HEAD_V)), _oacc((1, QKW)), _oacc((1, QKW)),
         _oacc((1, QKW)), _oacc((1, QKW))])
    g["q_lat_norm_g"] = g_qlg.reshape(MLA_Q_RANK)
    g["kv_lat_norm_g"] = g_kvlg.reshape(MLA_KV_RANK)
    heads = lambda a, w: jnp.sum(a.reshape(N_HEADS, HP)[:, :w], axis=0)
    g["mla_q_norm_g"], g["mla_k_norm_g"] = heads(g_mqg, MLA_QK_DIM), heads(g_mkg, MLA_QK_DIM)
    g["fox_q_norm_g"], g["fox_k_norm_g"] = heads(g_fqg, HEAD_V), heads(g_fkg, HEAD_V)
    g["w_uq"] = g_wuq.reshape(MLA_Q_RANK, N_HEADS, HP)[..., :MLA_QK_DIM].reshape(MLA_Q_RANK, N_HEADS * MLA_QK_DIM)
    g["w_ukv"] = jnp.concatenate([g_wuk.reshape(MLA_KV_RANK, N_HEADS, HP)[..., :MLA_NOPE_DIM],
                                  g_wuv.reshape(MLA_KV_RANK, N_HEADS, HEAD_V)], axis=-1).reshape(MLA_KV_RANK, QKW)

    g["w_in"] = _mm(sv["h"], dproj, "tn", "mm_in_dw", out_dtype=BF16)
    dh = landed("mm_in_dx", _mm(dproj, p["w_in"], "nt", "mm_in_dx", rider=riding("mm_in_dx", g)[0]), 1)[0]
    dx, g["attn_g"] = _tile_call(norm_bwd_fn, "norm_bwd", (nt,),
                                 [_row(sv["x"]), _par(p["attn_g"]), _row(dh), _row(d1)],
                                 [_orow(t, D_MODEL), _oacc((1, D_MODEL))])
    return dx, g, lands, carry_lands


def _row_tile(r, c, max_elems):
    if r * c <= max_elems:
        return r
    best = None
    for d in range(8, r, 8):
        if r % d == 0 and d * c <= max_elems:
            best = d
    assert best is not None, (r, c)
    return best


def _sum8(land, name):
    _, r, c = land.shape
    tr = _row_tile(r, N_DEV * c, 1 << 20)

    def fn(lv):
        acc = lv[0].astype(F32)
        for i in range(1, N_DEV):
            acc = acc + lv[i].astype(F32)
        return acc

    return _tile_call(fn, name, (r // tr,), [(land, (N_DEV, tr, c), lambda i: (0, i, 0))],
                      [((r, c), F32, (tr, c), lambda i: (i, 0), None)])[0]


def _adamw_layer(w, g, m, v, l, prev, name):
    _, r, c = w.shape
    tr = _row_tile(r, c, 1 << 19)
    lay = pl.BlockSpec((None, tr, c), lambda i: (l, i, 0))

    def body(w_ref, g_ref, m_ref, v_ref, *rest):
        g_out, d_out, m_out, v_out = rest[-4:]
        gv = g_ref[...]
        d_out[...], m_out[...], v_out[...] = _adamw(w_ref[...], gv, m_ref[...], v_ref[...])
        g_out[...] = gv

    in_specs = [lay, pl.BlockSpec((tr, c), lambda i: (i, 0)), lay, lay]
    args = [w, g, m, v]
    aliases = {}
    if prev is not None:
        in_specs += [pl.BlockSpec(memory_space=pl.ANY)] * 4
        args += list(prev)
        aliases = {4 + k: k for k in range(4)}
    return _pcall(body, name=name, grid=(r // tr,), in_specs=in_specs, out_specs=[lay] * 4,
                  out_shape=[jax.ShapeDtypeStruct(w.shape, F32)] * 4, aliases=aliases)(*args)


SMALL_NAMES = ("attn_norm_g", "q_lat_norm_g", "kv_lat_norm_g", "mla_q_norm_g", "mla_k_norm_g", "fox_q_norm_g",
               "fox_k_norm_g", "fox_f_bias", "s5_lambda_re", "s5_lambda_im", "s5_b_re", "s5_b_im", "s5_c_re", "s5_c_im",
               "s5_d", "s5_log_step", "s5_b_glu", "ffn_norm_g")
BIG_NAMES = ("w_in", "w_uq", "w_ukv", "s5_w_glu", "w_branch", "w_out", "w_up", "ffn_conv_w", "w_down")
ALL_NAMES = ("attn_norm_g", "w_in", "q_lat_norm_g", "w_uq", "kv_lat_norm_g", "w_ukv", "mla_q_norm_g", "mla_k_norm_g",
             "fox_q_norm_g", "fox_k_norm_g", "fox_f_bias", "s5_lambda_re", "s5_lambda_im", "s5_b_re", "s5_b_im",
             "s5_c_re", "s5_c_im", "s5_d", "s5_log_step", "s5_w_glu", "s5_b_glu", "w_branch", "w_out", "ffn_norm_g",
             "w_up", "ffn_conv_w", "w_down")


def _pack(arrs):
    tile = 8 * LANES
    parts = []
    for a in arrs:
        n = int(np.prod(a.shape))
        tot = -(-n // tile) * tile
        parts.append(jnp.pad(a.reshape(-1), (0, tot - n)).reshape(tot // LANES, LANES))
    rows = sum(a.shape[0] for a in parts)
    parts.append(jnp.zeros((-rows % 256, LANES), parts[0].dtype))
    return jnp.concatenate(parts, axis=0)


def _unpack(packed, like):
    tile = 8 * LANES
    out, pos = [], 0
    for a in like:
        n = int(np.prod(a.shape))
        rows = -(-n // tile) * 8
        out.append(packed[pos:pos + rows].reshape(-1)[:n].reshape(a.shape))
        pos += rows
    return out


def _rope_tables(positions, t, rider=None):
    pos = positions.reshape(t, 1)
    inv = _inv_freq_row()

    def fn(pv, iv):
        ang = pv.astype(F32) * iv
        return jnp.cos(ang), jnp.sin(ang)

    return _tile_call(fn, "rope_tables", (t // TM,), [_row(pos), _par(inv)], [_orow(t, QKW), _orow(t, QKW)],
                      rider=rider)


def _loss_call(x1, f, target):
    t = x1.shape[0]

    def fn(xv, fv, tv):
        e = xv + fv - tv
        per_tok = jnp.sum(e * e, axis=1, keepdims=True) * (1.0 / D_MODEL)
        return 0.5 * jnp.sum(per_tok, axis=0, keepdims=True), e * (1.0 / D_MODEL)

    return _tile_call(fn, "loss", (t // TM,), [_row(x1), _row(f), _row(target)], [_oacc((1, 1)), _orow(t, D_MODEL)])


def _local_shards(l, w):
    return {
        "w_in": _pad_in_cols(w["w_in"][l]).astype(BF16),
        "w_uq": w["w_uq"][l].reshape(-1, N_HEADS * MLA_QK_DIM).astype(BF16),
        "w_ukv": w["w_ukv"][l].reshape(-1, QKW).astype(BF16),
        "w_glu": w["s5_w_glu"][l].astype(BF16),
        "w_branch": w["w_branch"][l].astype(BF16),
        "w_out": w["w_out"][l].astype(BF16),
        "w_up": _pad_ff(w["w_up"][l], 1).astype(BF16),
        "conv": _pad_ff(w["ffn_conv_w"][l], 1),
        "w_down": _pad_ff(w["w_down"][l], 0).astype(BF16),
    }


GATHERED_SHAPES = {"w_in": (D_MODEL, PW), "w_uq": (MLA_Q_RANK, N_HEADS * MLA_QK_DIM), "w_ukv": (MLA_KV_RANK, QKW),
                   "w_glu": (S5_WIDTH, S5_WIDTH), "w_branch": (3, S5_WIDTH, D_MODEL), "w_out": (D_MODEL, D_MODEL),
                   "w_up": (D_MODEL, UP_W), "conv": (3, UP_W), "w_down": (FF_PW, D_MODEL)}
FFN_WEIGHTS = ("w_up", "conv", "w_down")
FIRST_EARLY_WEIGHTS = ("w_in", "w_uq", "w_ukv", "w_glu")
FIRST_LATE_WEIGHTS = ("w_out", "w_branch")
MIXER_WEIGHTS = ("w_in", "w_uq", "w_ukv", "w_glu", "w_out", "w_branch")
FWD_RIDERS_OWN = {"attn_mla_fwd": ("w_up", "conv"), "s5_scan_fwd": ("w_down",)}
FWD_RIDERS_NEXT = {"attn_fox_fwd": ("w_in", "w_uq", "w_ukv", "w_glu"), "mixer_pre": ("w_out", "w_branch")}
BWD_RIDERS = {"mm_up_dx": ("w_up:0",), "s5_scan_bwd": ("w_down",), "attn_mla_bwd": ("w_up:1",),
              "attn_fox_bwd": ("w_out", "w_branch", "w_glu", "conv"), "mm_in_dx": ("w_uq", "w_ukv")}
UPDATED_TRANSPOSED = ("w_up",)
WEIGHT_OF = {"w_in": "w_in", "w_uq": "w_uq", "w_ukv": "w_ukv", "w_glu": "s5_w_glu", "w_branch": "w_branch",
             "w_out": "w_out", "w_up": "w_up", "conv": "ffn_conv_w", "w_down": "w_down"}


def _unshard_layout(nm, g2):
    if nm == "w_in":
        return _unpad_in_cols(g2)
    if nm == "w_uq":
        return g2.reshape(-1, N_HEADS, MLA_QK_DIM)
    if nm == "w_ukv":
        return g2.reshape(-1, N_HEADS, MLA_NOPE_DIM + HEAD_V)
    if nm in ("w_up", "conv"):
        return _unpad_ff(g2, 1)
    if nm == "w_down":
        return _unpad_ff(g2, 0)
    return g2


def kernel(x, positions, attn_norm_g, w_in, q_lat_norm_g, w_uq, kv_lat_norm_g, w_ukv, mla_q_norm_g, mla_k_norm_g, fox_q_norm_g, fox_k_norm_g, fox_f_bias, s5_lambda_re, s5_lambda_im, s5_b_re, s5_b_im, s5_c_re, s5_c_im, s5_d, s5_log_step, s5_w_glu, s5_b_glu, w_branch, w_out, ffn_norm_g, w_up, ffn_conv_w, w_down, loss_target, m_attn_norm_g, m_w_in, m_q_lat_norm_g, m_w_uq, m_kv_lat_norm_g, m_w_ukv, m_mla_q_norm_g, m_mla_k_norm_g, m_fox_q_norm_g, m_fox_k_norm_g, m_fox_f_bias, m_s5_lambda_re, m_s5_lambda_im, m_s5_b_re, m_s5_b_im, m_s5_c_re, m_s5_c_im, m_s5_d, m_s5_log_step, m_s5_w_glu, m_s5_b_glu, m_w_branch, m_w_out, m_ffn_norm_g, m_w_up, m_ffn_conv_w, m_w_down, v_attn_norm_g, v_w_in, v_q_lat_norm_g, v_w_uq, v_kv_lat_norm_g, v_w_ukv, v_mla_q_norm_g, v_mla_k_norm_g, v_fox_q_norm_g, v_fox_k_norm_g, v_fox_f_bias, v_s5_lambda_re, v_s5_lambda_im, v_s5_b_re, v_s5_b_im, v_s5_c_re, v_s5_c_im, v_s5_d, v_s5_log_step, v_s5_w_glu, v_s5_b_glu, v_w_branch, v_w_out, v_ffn_norm_g, v_w_up, v_ffn_conv_w, v_w_down):
    args = locals()
    w = {n: args[n] for n in ALL_NAMES}
    m = {n: args["m_" + n] for n in ALL_NAMES}
    v = {n: args["v_" + n] for n in ALL_NAMES}
    depth = w_in.shape[0]
    bsz, seq, _ = x.shape
    t = bsz * seq
    consts = _constants()
    all_shards = [_local_shards(l, w) for l in range(depth)]

    def gather_xfer(l, names):
        sub = {n: all_shards[l][n] for n in names}
        order, plan = _gather_plan(sub)
        return _Gather2([sub[n] for n in order], [(GATHERED_SHAPES[n], sub[n].dtype) for n in order], plan)

    res = _rope_tables(positions, t, gather_xfer(0, FIRST_EARLY_WEIGHTS))
    tabs, gathered = res[:2], dict(zip(FIRST_EARLY_WEIGHTS, res[2:]))
    small = {n: w[n] for n in SMALL_NAMES}

    all_names = list(GATHERED_SHAPES)
    def scatter_xfer(g, names):
        shapes = {n: _local_shards_shape(n) for n in names}
        return _Xfer([g[n] for n in names], [((N_DEV,) + shapes[n], g[n].dtype) for n in names],
                     _scatter_plan(list(names), shapes))

    xs = x.reshape(t, D_MODEL)
    f_prev = None
    saved, params = [], []
    for l in range(depth):
        p = _prep_layer(l, gathered, small, consts)
        riders = {host: (gather_xfer(l, names), names) for host, names in FWD_RIDERS_OWN.items()}
        if l == 0:
            riders["norm_first"] = (gather_xfer(0, FIRST_LATE_WEIGHTS), FIRST_LATE_WEIGHTS)
        if l + 1 < depth:
            riders.update({host: (gather_xfer(l + 1, names), names) for host, names in FWD_RIDERS_NEXT.items()})
        xs, f_prev, sv, gathered = _layer_fwd(xs, f_prev, p, tabs, consts, bsz, riders)
        saved.append(sv)
        params.append(p)

    loss_part, dy = _loss_call(xs, f_prev, loss_target.reshape(t, D_MODEL))
    loss = lax.psum(loss_part[0, 0], ("x", "y", "c"))

    small_grads = {n: [None] * depth for n in SMALL_NAMES}
    big = {n: None for n in BIG_NAMES}

    def finish(l, lands):
        for n in all_names:
            wn = WEIGHT_OF[n]
            if n == "w_up":
                gsum = jnp.concatenate([_sum8(lands["w_up:%d" % h], "sum_w_up") for h in range(2)], axis=1)
            else:
                ld = lands[n]
                gsum = _sum8(ld.reshape(N_DEV, -1, ld.shape[-1]), "sum_" + n).reshape(_local_shards_shape(n))
            gsum = _unshard_layout(n, gsum)
            if n in UPDATED_TRANSPOSED:
                gsum = gsum.T
            c = gsum.shape[-1]
            three = lambda a: (jnp.swapaxes(a, 1, 2) if n in UPDATED_TRANSPOSED else a).reshape(depth, -1, c)
            big[wn] = _adamw_layer(three(w[wn]), gsum.reshape(-1, c), three(m[wn]), three(v[wn]), l, big[wn],
                                   "adamw_" + n)

    def scatter(grads):
        return scatter_xfer(grads, list(grads))

    dx = dy
    carry, lands_above = {}, None
    for l in reversed(range(depth)):
        dx, g, lands, carry_lands = _layer_bwd(dx, params[l], saved[l], tabs, consts, bsz, scatter, carry, l == 0)
        if lands_above is not None:
            finish(l + 1, {**lands_above, **carry_lands})
        carry, lands_above = {"w_in": g["w_in"]}, lands
        small_grads["attn_norm_g"][l] = g["attn_g"].reshape(D_MODEL)
        small_grads["ffn_norm_g"][l] = g["ffn_g"].reshape(D_MODEL)
        for n in SMALL_NAMES:
            if n not in ("attn_norm_g", "ffn_norm_g"):
                small_grads[n][l] = g[n]
    finish(0, lands_above)

    sg = [jnp.stack(small_grads[n]).reshape(w[n].shape) for n in SMALL_NAMES]
    packed = _pack(sg)
    land = _exchange("gather_small_grads", _Gather2([packed], [((N_DEV,) + packed.shape, F32)],
                                                    _scatter_plan(["small"], {})))[0]
    gs = _sum8(land, "sum_small")
    one = lambda names_of: _pack([names_of[n] for n in SMALL_NAMES])[None]
    small_res = _adamw_layer(one(w), gs, one(m), one(v), 0, None, "adamw_small")
    like = [w[n] for n in SMALL_NAMES]
    small_out = [dict(zip(SMALL_NAMES, _unpack(a[0], like))) for a in small_res]

    def out_of(kind, n):
        if n in SMALL_NAMES:
            return small_out[kind][n]
        if n in UPDATED_TRANSPOSED:
            return jnp.swapaxes(big[n][kind], 1, 2)
        return big[n][kind].reshape(w[n].shape)

    outs = [loss, dx.reshape(bsz, seq, D_MODEL)]
    for kind in range(4):
        outs += [out_of(kind, n) for n in ALL_NAMES]
    return tuple(outs)


def _local_shards_shape(nm):
    return {"w_in": (D_MODEL // N_DEV, PW), "w_uq": (MLA_Q_RANK // N_DEV, N_HEADS * MLA_QK_DIM),
            "w_ukv": (MLA_KV_RANK // N_DEV, QKW), "w_glu": (S5_WIDTH // N_DEV, S5_WIDTH),
            "w_branch": (3, S5_WIDTH, LANES), "w_out": (D_MODEL // N_DEV, D_MODEL), "w_up": (D_MODEL, 2 * FF_BLK),
            "conv": (3, 2 * FF_BLK), "w_down": (FF_BLK, D_MODEL), "w_up:0": (D_MODEL, FF_BLK),
            "w_up:1": (D_MODEL, FF_BLK)}[nm]
```

```python
import functools
import math

import numpy as np
import jax
import jax.numpy as jnp
from jax import lax
from jax.experimental import pallas as pl
from jax.experimental.pallas import tpu as pltpu

F32, BF16 = jnp.float32, jnp.bfloat16

D_MODEL = 1024
N_DEV = 8
MLA_Q_RANK, MLA_KV_RANK, MLA_ROPE_DIM, MLA_NOPE_DIM, MLA_QK_DIM = 384, 256, 32, 64, 96
N_HEADS, HEAD_V = 4, 64
S5_GROUPS, S5_GROUP_CH, S5_STATE, S5_WIDTH = 16, 16, 64, 256
S5_N = S5_GROUPS * S5_STATE
D_FF = 2816
D_IN = 4772
ROPE_THETA = 10000.0
NORM_EPS = 1e-6
NEG_INF = -1e30
ADAM_LR, ADAM_B1, ADAM_B2, ADAM_EPS, ADAM_WD, ADAM_STEP = 0.001, 0.9, 0.999, 1e-08, 0.01, 10

VMEM_LIMIT_BYTES = 56 * 1024 * 1024
LANES = 128
TM = 512
TM_H = 256
TQ = 512
QF = 128
QC = 512
HP = 128
QKW = N_HEADS * HP

PW = 5120
PA = 2048
IN_SEGS = ((0, 384, 256),
           (256, 672, 256),
           (512, 928, 256),
           (768, 1184, 256),
           (1024, 1444, 256),
           (1280, 640, 32),
           (1408, 1440, 4),
           (1536, 0, 384),
           (2048, 1700, 3072))
O_CKV, O_FQ, O_FK, O_FV, O_U, O_KR, O_FL, O_CQ = 0, 256, 512, 768, 1024, 1280, 1408, 1536

FF_BLK = 384
FF_HALF = D_FF // 8
FF_PW = 8 * FF_BLK
UP_W = 2 * FF_PW

NN = (((1,), (0,)), ((), ()))
NT = (((1,), (1,)), ((), ()))
TN = (((0,), (0,)), ((), ()))


def _pcall(body, *, name, grid, in_specs, out_specs, out_shape, scratch_shapes=(), rider=None, aliases=None):
    params = pltpu.CompilerParams(dimension_semantics=("arbitrary",) * len(grid), vmem_limit_bytes=VMEM_LIMIT_BYTES)
    if rider is None:
        return pl.pallas_call(body, name=name, grid=grid, in_specs=in_specs, out_specs=out_specs, out_shape=out_shape,
                              scratch_shapes=list(scratch_shapes), input_output_aliases=aliases or {},
                              compiler_params=params)
    assert not aliases
    in_specs, out_specs, out_shape = list(in_specs), list(out_specs), list(out_shape)
    n_in, n_out, n_scr, nx = len(in_specs), len(out_specs), len(scratch_shapes), rider.n
    any_spec = pl.BlockSpec(memory_space=pl.ANY)

    def at_step(last):
        conds = [pl.program_id(a) == (g - 1 if last else 0) for a, g in enumerate(grid)]
        return functools.reduce(jnp.logical_and, conds)

    def body_with_rider(*refs):
        ins, xsrc = refs[:n_in], refs[n_in:n_in + nx]
        outs, xout = refs[n_in + nx:n_in + nx + n_out], refs[n_in + nx + n_out:n_in + 2 * nx + n_out]
        scr = refs[n_in + 2 * nx + n_out:n_in + 2 * nx + n_out + n_scr]
        sems = refs[n_in + 2 * nx + n_out + n_scr:]

        @pl.when(at_step(False))
        def _():
            rider.start(rider.copies(xsrc, xout, *sems))

        body(*ins, *outs, *scr)

        @pl.when(at_step(True))
        def _():
            rider.wait(rider.copies(xsrc, xout, *sems))

    call = pl.pallas_call(body_with_rider, name=name, grid=grid, in_specs=in_specs + [any_spec] * nx,
                          out_specs=out_specs + [any_spec] * nx,
                          out_shape=out_shape + [jax.ShapeDtypeStruct(s, d) for s, d in rider.out_shapes],
                          scratch_shapes=list(scratch_shapes) + rider.sems(), compiler_params=params)
    return lambda *args: call(*args, *rider.srcs)


def _tile_call(fn, name, grid, ins, outs, rider=None):
    n_in = len(ins)

    def body(*refs):
        res = fn(*[r[...] for r in refs[:n_in]])
        if not isinstance(res, (tuple, list)):
            res = (res,)
        assert len(res) == len(outs), (name, len(res), len(outs))
        for r, v, o in zip(refs[n_in:], res, outs):
            v = v.astype(r.dtype)
            if o[4] is None:
                r[...] = v
            else:
                first = functools.reduce(jnp.logical_and, [pl.program_id(a) == 0 for a in o[4]])

                @pl.when(first)
                def _():
                    r[...] = v

                @pl.when(jnp.logical_not(first))
                def _():
                    r[...] += v

    res = _pcall(body, name=name, grid=grid,
                 in_specs=[pl.BlockSpec(b, m) for _, b, m in ins],
                 out_specs=[pl.BlockSpec(o[2], o[3]) for o in outs],
                 out_shape=[jax.ShapeDtypeStruct(o[0], o[1]) for o in outs], rider=rider)(*[a for a, _, _ in ins])
    return res


def _row(a, width=None, col_block=0, tm=TM):
    w = a.shape[1] if width is None else width
    return (a, (tm, w), lambda i, c=col_block: (i, c))


def _par(a):
    nd = a.ndim
    return (a, a.shape, lambda i, nd=nd: (0,) * nd)


def _orow(t, w, dtype=F32, tm=TM):
    return ((t, w), dtype, (tm, w), lambda i: (i, 0), None)


def _row_h(a, width=None, col_block=0):
    return _row(a, width, col_block, tm=TM_H)


def _orow_h(t, w, dtype=F32):
    return _orow(t, w, dtype, tm=TM_H)


def _oacc(shape):
    nd = len(shape)
    return (tuple(shape), F32, tuple(shape), lambda i, nd=nd: (0,) * nd, (0,))


def _pick(n, target):
    best = None
    for d in range(LANES, min(n, target) + 1, LANES):
        if n % d == 0:
            best = d
    return n if best is None else best


MM_VMEM_BUDGET = 44 * 1024 * 1024


def _mm_tiles(m, n, k, sa, sb, so, tm, tn, tk):
    for cm, cn in ((1024, 2048), (1024, 1024), (512, 2048), (512, 1024), (512, 512), (256, 1024), (256, 512)):
        pm, pn = _pick(m, cm), _pick(n, cn)
        if 2 * (pm * k * sa + k * pn * sb + pm * pn * so) <= MM_VMEM_BUDGET:
            return pm, pn, k
    return _pick(m, tm), _pick(n, tn), _pick(k, tk)


def _mm(a, b, mode, name, out_dtype=F32, tm=1024, tn=2048, tk=1024, rider=None):
    if mode == "nn":
        (m, k), (_, n) = a.shape, b.shape
    elif mode == "nt":
        (m, k), (n, _) = a.shape, b.shape
    else:
        (k, m), (_, n) = a.shape, b.shape
    tm, tn, tk = _mm_tiles(m, n, k, a.dtype.itemsize, b.dtype.itemsize, jnp.dtype(out_dtype).itemsize, tm, tn, tk)
    nk = k // tk
    dims = {"nn": NN, "nt": NT, "tn": TN}[mode]
    a_spec = (pl.BlockSpec((tk, tm), lambda i, j, l: (l, i)) if mode == "tn"
              else pl.BlockSpec((tm, tk), lambda i, j, l: (i, l)))
    b_spec = (pl.BlockSpec((tn, tk), lambda i, j, l: (j, l)) if mode == "nt"
              else pl.BlockSpec((tk, tn), lambda i, j, l: (l, j)))

    def body(a_ref, b_ref, o_ref, acc_ref):
        part = lax.dot_general(a_ref[...].astype(BF16), b_ref[...].astype(BF16), dims, preferred_element_type=F32)
        if nk == 1:
            o_ref[...] = part.astype(o_ref.dtype)
        else:
            l = pl.program_id(2)

            @pl.when(l == 0)
            def _():
                acc_ref[...] = part

            @pl.when(l > 0)
            def _():
                acc_ref[...] += part

            @pl.when(l == nk - 1)
            def _():
                o_ref[...] = acc_ref[...].astype(o_ref.dtype)

    res = _pcall(body, name=name, grid=(m // tm, n // tn, nk), in_specs=[a_spec, b_spec],
                 out_specs=[pl.BlockSpec((tm, tn), lambda i, j, l: (i, j))],
                 out_shape=[jax.ShapeDtypeStruct((m, n), out_dtype)],
                 scratch_shapes=[pltpu.VMEM((tm, tn) if nk > 1 else (8, LANES), F32)], rider=rider)(a, b)
    return res[0] if rider is None else res


def _dot(x, w, dims):
    return lax.dot_general(x.astype(BF16), w.astype(BF16), dims, preferred_element_type=F32)


@jax.custom_vjp
def bdot(x, w):
    return _dot(x, w, NN)


def _bdot_fwd(x, w):
    return _dot(x, w, NN), (x, w)


def _bdot_bwd(res, g):
    x, w = res
    return _dot(g, w, NT), _dot(x, g, TN)


bdot.defvjp(_bdot_fwd, _bdot_bwd)


def _split3(x):
    x1 = x.astype(BF16)
    r1 = x - x1.astype(F32)
    x2 = r1.astype(BF16)
    x3 = (r1 - x2.astype(F32)).astype(BF16)
    return x1, x2, x3


def _cdot(x, m, dims):
    return sum(lax.dot_general(p, m, dims, preferred_element_type=F32) for p in _split3(x))


def _block_dot(x, m, transposed):
    k_in = m.shape[1] if transposed else m.shape[0]
    dims = NT if transposed else NN
    return jnp.concatenate([_cdot(x[:, b * k_in:(b + 1) * k_in], m, dims) for b in range(x.shape[1] // k_in)], axis=1)


@jax.custom_vjp
def cdot(x, m):
    return _block_dot(x, m, False)


def _cdot_fwd(x, m):
    return _block_dot(x, m, False), m


def _cdot_bwd(m, g):
    return _block_dot(g, m, True), jnp.zeros_like(m)


cdot.defvjp(_cdot_fwd, _cdot_bwd)


@jax.custom_vjp
def tile_heads(y):
    return jnp.concatenate([y] * N_HEADS, axis=1)


def _tile_heads_fwd(y):
    return jnp.concatenate([y] * N_HEADS, axis=1), None


def _tile_heads_bwd(_, g):
    return (sum(g[:, h * HP:(h + 1) * HP] for h in range(N_HEADS)),)


tile_heads.defvjp(_tile_heads_fwd, _tile_heads_bwd)


def _shift_rows_impl(x, s, reverse):
    n = x.shape[0]
    idx = lax.broadcasted_iota(jnp.int32, x.shape, 0)
    if reverse:
        return jnp.where(idx < n - s, pltpu.roll(x, n - s, 0), 0.0)
    return jnp.where(idx >= s, pltpu.roll(x, s, 0), 0.0)


@functools.partial(jax.custom_vjp, nondiff_argnums=(1,))
def shift_rows(x, s):
    return _shift_rows_impl(x, s, False)


def _shift_rows_fwd(x, s):
    return _shift_rows_impl(x, s, False), None


def _shift_rows_bwd(s, _, g):
    return (_shift_rows_impl(g, s, True),)


shift_rows.defvjp(_shift_rows_fwd, _shift_rows_bwd)


def _rms(x, g):
    return x * lax.rsqrt(jnp.mean(x * x, axis=-1, keepdims=True) + NORM_EPS) * g


def _head_rms(x, e_mat, inv_n, g):
    ms = cdot(x * x, e_mat) * inv_n
    return x * lax.rsqrt(ms + NORM_EPS) * g


def _rope(y, p_mat, cos, sin):
    return y * cos + cdot(y, p_mat) * sin


def _constants():
    e = np.ones((HP, HP), np.float32)
    p = np.zeros((HP, HP), np.float32)
    r = np.zeros((LANES, HP), np.float32)
    pl64 = np.zeros((2 * HEAD_V, 2 * HP), np.float32)
    half = MLA_ROPE_DIM // 2
    for i in range(half):
        x1, x2 = MLA_NOPE_DIM + i, MLA_NOPE_DIM + half + i
        p[x2, x1] = -1.0
        p[x1, x2] = 1.0
    for i in range(MLA_ROPE_DIM):
        r[i, MLA_NOPE_DIM + i] = 1.0
    for h in range(2):
        for i in range(HEAD_V):
            pl64[h * HEAD_V + i, h * HP + i] = 1.0
    return tuple(jnp.asarray(a, BF16) for a in (e, p, r, pl64))


def _inv_freq_row():
    inv = ROPE_THETA ** (-jnp.arange(0, MLA_ROPE_DIM, 2, dtype=F32) / MLA_ROPE_DIM)
    head = jnp.concatenate([jnp.zeros((MLA_NOPE_DIM,), F32), inv, inv, jnp.zeros((HP - MLA_QK_DIM,), F32)])
    return jnp.tile(head, N_HEADS).reshape(1, QKW)


def _mixer_pre(c_q, c_kv, kr, fq, fk, qlg, w_uq, kvlg, w_uk, w_uv, mqg, mkg, fqg, fkg, cos, sin, e_mat, p_mat, r_mat, pl_mat):
    q = bdot(_rms(c_q, qlg), w_uq)
    ckvn = _rms(c_kv, kvlg)
    k = bdot(ckvn, w_uk) + tile_heads(cdot(kr, r_mat))
    v = bdot(ckvn, w_uv)
    q = _rope(_head_rms(q, e_mat, 1.0 / MLA_QK_DIM, mqg), p_mat, cos, sin)
    k = _rope(_head_rms(k, e_mat, 1.0 / MLA_QK_DIM, mkg), p_mat, cos, sin)
    qf = _head_rms(cdot(fq, pl_mat), e_mat, 1.0 / HEAD_V, fqg)
    kf = _head_rms(cdot(fk, pl_mat), e_mat, 1.0 / HEAD_V, fkg)
    return q, k, v, qf, kf


def _s5_post(state, u, c_blk, d_row, w_glu, b_glu):
    y = bdot(state, c_blk) + d_row * u
    y = jax.nn.gelu(y)
    return y * jax.nn.sigmoid(bdot(y, w_glu) + b_glu)


def _merge(o_a, o_b, o_c, g_a, g_b, g_c, w_a, w_b, w_c):
    return (jax.nn.sigmoid(g_a) * bdot(o_a, w_a) + jax.nn.sigmoid(g_b) * bdot(o_b, w_b)
            + jax.nn.sigmoid(g_c) * bdot(o_c, w_c))


def _ffn_mid(gate, val, wg0, wg1, wg2, wv0, wv1, wv2):
    cg = wg0 * shift_rows(gate, 2) + wg1 * shift_rows(gate, 1) + wg2 * gate
    cv = wv0 * shift_rows(val, 2) + wv1 * shift_rows(val, 1) + wv2 * val
    return jax.nn.silu(cg) * cv


def _s5_params(lam_re, lam_im, log_step, b_re, b_im):
    step = jnp.exp(log_step)
    zr, zi = lam_re * step, lam_im * step
    mag = jnp.exp(zr)
    lr, li = mag * jnp.cos(zi), mag * jnp.sin(zi)
    nr, ni = lr - 1.0, li
    den = lam_re * lam_re + lam_im * lam_im
    cr = (nr * lam_re + ni * lam_im) / den
    ci = (ni * lam_re - nr * lam_im) / den
    return lr, li, cr * b_re - ci * b_im, cr * b_im + ci * b_re


def _adamw(w, g, m, v):
    m = ADAM_B1 * m + (1.0 - ADAM_B1) * g
    v = ADAM_B2 * v + (1.0 - ADAM_B2) * (g * g)
    m_hat = m / (1.0 - ADAM_B1 ** ADAM_STEP)
    v_hat = v / (1.0 - ADAM_B2 ** ADAM_STEP)
    delta = -ADAM_LR * (m_hat / (jnp.sqrt(v_hat) + ADAM_EPS) + ADAM_WD * w)
    return delta, m, v


SCAN_W = 256


def _scan_seq(xr, xi, ar, ai, reverse):
    n = xr.shape[0]
    s = 1
    while s < n:
        sr, si = _shift_rows_impl(xr, s, reverse), _shift_rows_impl(xi, s, reverse)
        xr, xi = xr + ar * sr - ai * si, xi + ar * si + ai * sr
        ar, ai = ar * ar - ai * ai, 2.0 * ar * ai
        s *= 2
    return xr, xi


def _scan_fwd(bu, lbar, bsz, name, rider=None):
    t = bu.shape[0]
    seq = t // bsz
    nb = bu.shape[1] // (2 * SCAN_W)

    def fn(b, a):
        xr, xi = _scan_seq(b[:, :SCAN_W], b[:, SCAN_W:], a[:, :SCAN_W], a[:, SCAN_W:], False)
        return jnp.concatenate([xr, xi], axis=1)

    blk = (seq, 2 * SCAN_W)
    return _tile_call(fn, name, (bsz, nb),
                      [(bu, blk, lambda i, j: (i, j)), (lbar, (1, 2 * SCAN_W), lambda i, j: (0, j))],
                      [((t, bu.shape[1]), F32, blk, lambda i, j: (i, j), None)], rider=rider)


def _scan_bwd(dstate, state, lbar, bsz, name, rider=None):
    t = state.shape[0]
    seq = t // bsz
    nb = state.shape[1] // (2 * SCAN_W)

    def fn(g, x, a):
        ar, ai = a[:, :SCAN_W], a[:, SCAN_W:]
        gr, gi = _scan_seq(g[:, :SCAN_W], g[:, SCAN_W:], ar, -ai, True)
        pr, pi = _shift_rows_impl(x[:, :SCAN_W], 1, False), _shift_rows_impl(x[:, SCAN_W:], 1, False)
        dar = jnp.sum(gr * pr + gi * pi, axis=0, keepdims=True)
        dai = jnp.sum(gi * pr - gr * pi, axis=0, keepdims=True)
        return jnp.concatenate([gr, gi], axis=1), jnp.concatenate([dar, dai], axis=1)

    blk = (seq, 2 * SCAN_W)
    return _tile_call(fn, name, (nb, bsz),
                      [(dstate, blk, lambda j, i: (i, j)), (state, blk, lambda j, i: (i, j)),
                       (lbar, (1, 2 * SCAN_W), lambda j, i: (0, j))],
                      [((t, state.shape[1]), F32, blk, lambda j, i: (i, j), None),
                       ((1, state.shape[1]), F32, (1, 2 * SCAN_W), lambda j, i: (0, j), (1,))], rider=rider)


def _shift_lanes(x, s, reverse):
    n = x.shape[1]
    idx = lax.broadcasted_iota(jnp.int32, x.shape, 1)
    if reverse:
        return jnp.where(idx < n - s, pltpu.roll(x, n - s, 1), 0.0)
    return jnp.where(idx >= s, pltpu.roll(x, s, 1), 0.0)


def _cumsum_lanes(x, reverse):
    s = 1
    while s < x.shape[1]:
        x = x + _shift_lanes(x, s, reverse)
        s *= 2
    return x


def _forget_fwd(z, bias, name):
    def fn(zv, bv):
        x = zv + bv
        logf = jnp.minimum(x, 0.0) - jnp.log(1.0 + jnp.exp(-jnp.abs(x)))
        return _cumsum_lanes(logf, False)

    return _tile_call(fn, name, (1,), [_par(z), _par(bias)],
                      [(z.shape, F32, z.shape, lambda i: (0, 0), None)])[0]


def _forget_bwd(dc, z, bias, name):
    def fn(dcv, zv, bv):
        x = zv + bv
        dlogf = _cumsum_lanes(dcv, True)
        dz = dlogf * jax.nn.sigmoid(-x)
        return dz, jnp.sum(dz, axis=1, keepdims=True)

    rows = z.shape[0]
    return _tile_call(fn, name, (1,), [_par(dc), _par(z), _par(bias)],
                      [(z.shape, F32, z.shape, lambda i: (0, 0), None),
                       ((rows, 1), F32, (rows, 1), lambda i: (0, 0), None)])


def _col(v2, x):
    idx = lax.broadcasted_iota(jnp.int32, v2.shape, 1)
    return jnp.sum(jnp.where(idx == x, v2, 0.0), axis=1, keepdims=True)


def _two_cols(c0, c1):
    idx = lax.broadcasted_iota(jnp.int32, (c0.shape[0], 2), 1)
    return jnp.where(idx == 0, c0, c1)


def _rows2(r0, r1):
    idx = lax.broadcasted_iota(jnp.int32, (2, r0.shape[1]), 0)
    return jnp.where(idx == 0, r0, r1)


def _attn_specs(seq, v_blk0, has_bias):
    qblk = pl.BlockSpec((None, TQ, 2 * HP), lambda b, p, i: (b, i, p))
    kblk = pl.BlockSpec((None, seq, 2 * HP), lambda b, p, i: (b, 0, p))
    vblk = pl.BlockSpec((None, seq, LANES), lambda b, p, i: (b, 0, v_blk0 + p))
    oblk = pl.BlockSpec((None, TQ, LANES), lambda b, p, i: (b, i, p))
    rowblk = pl.BlockSpec((None, None, 2, TQ), lambda b, p, i: (b, p, 0, i))
    colblk = pl.BlockSpec((None, None, seq, 2), lambda b, p, i: (b, p, 0, 0))
    return qblk, kblk, vblk, oblk, rowblk, colblk


def _attn_fwd(q, k, v, v_blk0, c_cols, c_rows, scale, name, rider=None):
    bsz, seq, _ = q.shape
    nq = seq // TQ
    has_bias = c_cols is not None
    n_in = 5 if has_bias else 3
    nc = TQ // QF

    def body(*refs):
        q_ref, k_ref, v_ref = refs[:3]
        cq_ref, ck_ref = refs[3:5] if has_bias else (None, None)
        o_ref, lse_ref = refs[n_in:]
        qi = pl.program_id(2)
        qb = [[q_ref[c * QF:(c + 1) * QF, x * HP:(x + 1) * HP].astype(BF16) for c in range(nc)] for x in range(2)]
        key_in = lax.broadcasted_iota(jnp.int32, (TQ, QF), 0)
        qry_in = lax.broadcasted_iota(jnp.int32, (TQ, QF), 1)

        def block(j, carry, masked):
            ks = pl.multiple_of(j * TQ, TQ)
            vt = v_ref[pl.ds(ks, TQ), :].T.astype(BF16)
            new = []
            for x in range(2):
                kx = k_ref[pl.ds(ks, TQ), x * HP:(x + 1) * HP].astype(BF16)
                ckx = _col(ck_ref[pl.ds(ks, TQ), :], x) if has_bias else None
                for c in range(nc):
                    m, l, acc = carry[x * nc + c]
                    nk = (c + 1) * QF if masked else TQ
                    s = lax.dot_general(kx[:nk], qb[x][c], NT, preferred_element_type=F32) * scale
                    if has_bias:
                        s = s + (cq_ref[x:x + 1, c * QF:(c + 1) * QF] - ckx[:nk])
                    if masked:
                        s = jnp.where(qry_in[:nk] + c * QF >= key_in[:nk], s, NEG_INF)
                    m_new = jnp.maximum(m, jnp.max(s, axis=0, keepdims=True))
                    alpha = jnp.exp(m - m_new)
                    p = jnp.exp(s - m_new)
                    l = alpha * l + jnp.sum(p, axis=0, keepdims=True)
                    acc = alpha * acc + lax.dot_general(vt[:, :nk], p.astype(BF16), NN, preferred_element_type=F32)
                    new.append((m_new, l, acc))
            return tuple(new)

        init = tuple((jnp.full((1, QF), NEG_INF, F32), jnp.zeros((1, QF), F32), jnp.zeros((LANES, QF), F32))
                     for _ in range(2 * nc))
        carry = lax.fori_loop(0, qi, lambda j, cr: block(j, cr, False), init)
        carry = block(qi, carry, True)
        lane = lax.broadcasted_iota(jnp.int32, (QF, LANES), 1)
        lse_rows = []
        for x in range(2):
            lse_rows.append(jnp.concatenate([carry[x * nc + c][0] + jnp.log(carry[x * nc + c][1]) for c in range(nc)],
                                            axis=1))
        for c in range(nc):
            (_, l0, a0), (_, l1, a1) = carry[c], carry[nc + c]
            o_ref[c * QF:(c + 1) * QF, :] = jnp.where(lane < HEAD_V, (a0 / l0).T, (a1 / l1).T)
        lse_ref[...] = _rows2(lse_rows[0], lse_rows[1])

    qblk, kblk, vblk, oblk, rowblk, colblk = _attn_specs(seq, v_blk0, has_bias)
    in_specs, args = [qblk, kblk, vblk], [q, k, v]
    if has_bias:
        in_specs += [rowblk, colblk]
        args += [c_rows, c_cols]
    return _pcall(body, name=name, grid=(bsz, 2, nq), in_specs=in_specs, out_specs=[oblk, rowblk],
                  out_shape=[jax.ShapeDtypeStruct((bsz, seq, 2 * LANES), F32),
                             jax.ShapeDtypeStruct((bsz, 2, 2, seq), F32)], rider=rider)(*args)


def _attn_bwd(q, k, v, v_blk0, o, do, lse, c_cols, c_rows, scale, name, rider=None):
    bsz, seq, _ = q.shape
    nq = seq // TQ
    has_bias = c_cols is not None
    n_in = 8 if has_bias else 6
    nc = TQ // QC

    def body(*refs):
        q_ref, k_ref, v_ref, o_ref, do_ref, lse_ref = refs[:6]
        cq_ref, ck_ref = refs[6:8] if has_bias else (None, None)
        dq_ref, dk_ref, dv_ref = refs[n_in:n_in + 3]
        dck_ref, dcq_ref = refs[n_in + 3:n_in + 5] if has_bias else (None, None)
        qi = pl.program_id(2)

        @pl.when(qi == 0)
        def _():
            dk_ref[...] = jnp.zeros_like(dk_ref)
            dv_ref[...] = jnp.zeros_like(dv_ref)
            if has_bias:
                dck_ref[...] = jnp.zeros_like(dck_ref)

        lane = lax.broadcasted_iota(jnp.int32, (QC, LANES), 1)
        ones8 = jnp.ones((8, LANES), BF16)
        qb, dob, delta, lse_r = [], [], [], []
        for x in range(2):
            hm = jnp.logical_and(lane >= x * HEAD_V, lane < (x + 1) * HEAD_V)
            qb.append([]), dob.append([]), delta.append([]), lse_r.append([])
            for c in range(nc):
                rows = slice(c * QC, (c + 1) * QC)
                do_c = jnp.where(hm, do_ref[rows, :], 0.0)
                prod = do_c * o_ref[rows, :]
                hi = prod.astype(BF16)
                lo = (prod - hi.astype(F32)).astype(BF16)
                d8 = (lax.dot_general(ones8, hi, NT, preferred_element_type=F32)
                      + lax.dot_general(ones8, lo, NT, preferred_element_type=F32))
                delta[x].append(d8[0:1, :])
                dob[x].append(do_c.astype(BF16))
                qb[x].append(q_ref[rows, x * HP:(x + 1) * HP].astype(BF16))
                lse_r[x].append(lse_ref[x:x + 1, rows])
        key_in = lax.broadcasted_iota(jnp.int32, (TQ, QC), 0)
        qry_in = lax.broadcasted_iota(jnp.int32, (TQ, QC), 1)

        def block(j, carry, masked):
            ks = pl.multiple_of(j * TQ, TQ)
            vb = v_ref[pl.ds(ks, TQ), :].astype(BF16)
            new, key_sums = [], []
            for x in range(2):
                k32 = k_ref[pl.ds(ks, TQ), x * HP:(x + 1) * HP]
                kx, kt = k32.astype(BF16), k32.T.astype(BF16)
                ckx = _col(ck_ref[pl.ds(ks, TQ), :], x) if has_bias else None
                dk_acc = jnp.zeros((TQ, HP), F32)
                dv_acc = jnp.zeros((TQ, LANES), F32)
                key_sum = jnp.zeros((TQ, 1), F32)
                for c in range(nc):
                    dqt, dcq = carry[x * nc + c]
                    s = lax.dot_general(kx, qb[x][c], NT, preferred_element_type=F32) * scale
                    if has_bias:
                        s = s + (cq_ref[x:x + 1, c * QC:(c + 1) * QC] - ckx)
                    if masked:
                        s = jnp.where(qry_in + c * QC >= key_in, s, NEG_INF)
                    p = jnp.exp(s - lse_r[x][c])
                    dp = lax.dot_general(vb, dob[x][c], NT, preferred_element_type=F32)
                    ds = p * (dp - delta[x][c])
                    dsb = ds.astype(BF16)
                    dk_acc = dk_acc + lax.dot_general(dsb, qb[x][c], NN, preferred_element_type=F32)
                    dv_acc = dv_acc + lax.dot_general(p.astype(BF16), dob[x][c], NN, preferred_element_type=F32)
                    dqt = dqt + lax.dot_general(kt, dsb, NN, preferred_element_type=F32)
                    if has_bias:
                        dcq = dcq + jnp.sum(ds, axis=0, keepdims=True)
                        key_sum = key_sum + jnp.sum(ds, axis=1, keepdims=True)
                    new.append((dqt, dcq))
                dk_ref[pl.ds(ks, TQ), x * HP:(x + 1) * HP] += dk_acc * scale
                dv_ref[pl.ds(ks, TQ), :] += dv_acc
                key_sums.append(key_sum)
            if has_bias:
                dck_ref[pl.ds(ks, TQ), :] -= _two_cols(key_sums[0], key_sums[1])
            return tuple(new)

        init = tuple((jnp.zeros((HP, QC), F32), jnp.zeros((1, QC), F32)) for _ in range(2 * nc))
        carry = lax.fori_loop(0, qi, lambda j, cr: block(j, cr, False), init)
        carry = block(qi, carry, True)
        for x in range(2):
            for c in range(nc):
                dq_ref[c * QC:(c + 1) * QC, x * HP:(x + 1) * HP] = carry[x * nc + c][0].T * scale
        if has_bias:
            dcq_ref[...] = _rows2(jnp.concatenate([carry[c][1] for c in range(nc)], axis=1),
                                  jnp.concatenate([carry[nc + c][1] for c in range(nc)], axis=1))

    qblk, kblk, vblk, oblk, rowblk, colblk = _attn_specs(seq, v_blk0, has_bias)
    in_specs = [qblk, kblk, vblk, oblk, oblk, rowblk]
    out_specs = [qblk, kblk, pl.BlockSpec((None, seq, LANES), lambda b, p, i: (b, 0, p))]
    out_shape = [jax.ShapeDtypeStruct((bsz, seq, QKW), F32), jax.ShapeDtypeStruct((bsz, seq, QKW), F32),
                 jax.ShapeDtypeStruct((bsz, seq, 2 * LANES), F32)]
    args = [q, k, v, o, do, lse]
    if has_bias:
        in_specs += [rowblk, colblk]
        out_specs += [colblk, rowblk]
        out_shape += [jax.ShapeDtypeStruct((bsz, 2, seq, 2), F32), jax.ShapeDtypeStruct((bsz, 2, 2, seq), F32)]
        args += [c_rows, c_cols]
    return _pcall(body, name=name, grid=(bsz, 2, nq), in_specs=in_specs, out_specs=out_specs,
                  out_shape=out_shape, rider=rider)(*args)


def _coords(idx):
    return (idx // 4, (idx // 2) % 2, idx % 2)


class _Xfer:
    def __init__(self, srcs, out_shapes, plan):
        self.srcs, self.out_shapes, self.plan = list(srcs), list(out_shapes), plan
        self.n = len(self.srcs)
        me0 = jnp.int32(0)
        n_pieces = [len(plan(a, me0, me0)) for a in range(self.n)]
        self.offs = np.concatenate([[0], np.cumsum(n_pieces)]).astype(int)
        self.total = int(self.offs[-1])

    def sems(self):
        remote = (N_DEV - 1) * self.total
        return [pltpu.SemaphoreType.DMA((remote,)), pltpu.SemaphoreType.DMA((remote,)),
                pltpu.SemaphoreType.DMA((self.total,))]

    def copies(self, src_refs, out_refs, send_sems, recv_sems, local_sems):
        me = 4 * lax.axis_index("x") + 2 * lax.axis_index("y") + lax.axis_index("c")
        out = []
        for a in range(self.n):
            for pi, (si, di) in enumerate(self.plan(a, me, me)):
                out.append((pltpu.make_async_copy(src_refs[a].at[si], out_refs[a].at[di],
                                                  local_sems.at[self.offs[a] + pi]), None))
        for kk in range(1, N_DEV):
            dest = (me + kk) % N_DEV
            src_dev = (me + N_DEV - kk) % N_DEV
            for a in range(self.n):
                pieces = self.plan(a, me, dest)
                landing = self.plan(a, src_dev, me)
                for pi, ((si, di), (_, li)) in enumerate(zip(pieces, landing)):
                    sem = (kk - 1) * self.total + self.offs[a] + pi
                    mk = functools.partial(pltpu.make_async_remote_copy, src_ref=src_refs[a].at[si],
                                           send_sem=send_sems.at[sem], recv_sem=recv_sems.at[sem],
                                           device_id=_coords(dest), device_id_type=pl.DeviceIdType.MESH)
                    out.append((mk(dst_ref=out_refs[a].at[di]), mk(dst_ref=out_refs[a].at[li])))
        return out

    @staticmethod
    def start(copies):
        for cp, _ in copies:
            cp.start()

    @staticmethod
    def wait(copies):
        for cp, rc in copies:
            if rc is None:
                cp.wait()
            else:
                cp.wait_send()
                rc.wait_recv()


def _exchange(name, xf):
    n = xf.n

    def body(*refs):
        copies = xf.copies(refs[:n], refs[n:2 * n], *refs[2 * n:])
        xf.start(copies)
        xf.wait(copies)

    any_spec = pl.BlockSpec(memory_space=pl.ANY)
    return pl.pallas_call(body, name=name, in_specs=[any_spec] * n, out_specs=[any_spec] * n,
                          out_shape=[jax.ShapeDtypeStruct(s, d) for s, d in xf.out_shapes],
                          scratch_shapes=xf.sems(),
                          compiler_params=pltpu.CompilerParams(has_side_effects=True))(*xf.srcs)


class _Gather2(_Xfer):
    def copies(self, src_refs, out_refs, send_sems, recv_sems, local_sems):
        x, y, c = lax.axis_index("x"), lax.axis_index("y"), lax.axis_index("c")
        num = lambda px, py, pc: 4 * px + 2 * py + pc
        me, sib = (x, y, c), (x, y, 1 - c)
        chips = [(1 - x, y), (x, 1 - y), (1 - x, 1 - y)]
        local, first, passed, landing = [], [], [], []
        for a in range(self.n):
            mine = self.plan(a, num(*me), num(*me))
            for pi, (si, di) in enumerate(mine):
                piece = self.offs[a] + pi

                def remote(k, src, dst, to, a=a, piece=piece):
                    return pltpu.make_async_remote_copy(
                        src_ref=src, dst_ref=dst, send_sem=send_sems.at[k * self.total + piece],
                        recv_sem=recv_sems.at[k * self.total + piece], device_id=to, device_id_type=pl.DeviceIdType.MESH)

                def slot(dev, a=a, pi=pi):
                    return out_refs[a].at[self.plan(a, num(*dev), num(*dev))[pi][1]]

                local.append(pltpu.make_async_copy(src_refs[a].at[si], out_refs[a].at[di], local_sems.at[piece]))
                first.append(remote(0, src_refs[a].at[si], out_refs[a].at[di], sib))
                landing.append([remote(0, slot(sib), slot(sib), sib)])
                for j, chip in enumerate(chips):
                    first.append(remote(1 + j, src_refs[a].at[si], out_refs[a].at[di], (*chip, c)))
                    passed.append((remote(1 + j, slot((*chip, c)), slot((*chip, c)), (*chip, c)),
                                   remote(4 + j, slot((*chip, c)), slot((*chip, c)), sib)))
                    landing[-1].append(remote(4 + j, slot((*chip, 1 - c)), slot((*chip, 1 - c)), sib))
        return local, first, passed, landing

    @staticmethod
    def start(copies):
        local, first, _, _ = copies
        for cp in local + first:
            cp.start()

    @staticmethod
    def wait(copies):
        local, first, passed, landing = copies
        for arrival, forward in passed:
            arrival.wait_recv()
            forward.start()
        for group in landing:
            for cp in group:
                cp.wait_recv()
        for cp in first + [fw for _, fw in passed]:
            cp.wait_send()
        for cp in local:
            cp.wait()


def _rows_of(j, n):
    return pl.ds(pl.multiple_of(j * n, n), n)


def _ffn_cols(j, half):
    return pl.multiple_of((2 * (j % 4) + half) * (2 * FF_BLK) + (j // 4) * FF_BLK, FF_BLK)


def _gather_plan(shards):
    names = list(shards)

    def plan(a, src_dev, _dest):
        nm = names[a]
        j = src_dev
        if nm in ("w_in", "w_uq", "w_ukv", "w_glu", "w_out", "w_down"):
            rows = shards[nm].shape[0]
            return [((slice(None), slice(None)), (_rows_of(j, rows), slice(None)))]
        if nm == "w_branch":
            return [((slice(None), slice(None), slice(None)), (slice(None), slice(None), _rows_of(j, LANES)))]
        if nm in ("w_up", "conv"):
            return [((slice(None), pl.ds(h * FF_BLK, FF_BLK)), (slice(None), pl.ds(_ffn_cols(j, h), FF_BLK)))
                    for h in range(2)]
        raise KeyError(nm)

    return names, plan


def _scatter_plan(names, shard_shapes):
    def plan(a, src_dev, dest):
        nm = names[a]
        j = dest
        if nm in ("w_in", "w_uq", "w_ukv", "w_glu", "w_out", "w_down"):
            rows = shard_shapes[nm][0]
            return [((_rows_of(j, rows), slice(None)), (src_dev, slice(None), slice(None)))]
        if nm == "w_branch":
            return [((slice(None), slice(None), _rows_of(j, LANES)), (src_dev, slice(None), slice(None), slice(None)))]
        if nm in ("w_up", "conv"):
            return [((slice(None), pl.ds(_ffn_cols(j, h), FF_BLK)), (src_dev, slice(None), pl.ds(h * FF_BLK, FF_BLK)))
                    for h in range(2)]
        if nm in ("w_up:0", "w_up:1"):
            h = int(nm[-1])
            return [((slice(None), pl.ds(_ffn_cols(j, h), FF_BLK)), (src_dev, slice(None), slice(None)))]
        if nm == "small":
            return [((slice(None), slice(None)), (src_dev, slice(None), slice(None)))]
        raise KeyError(nm)

    return plan


def _pad_in_cols(w):
    lead = w.shape[:-1]
    parts, pos = [], 0
    for dst, src, width in IN_SEGS:
        if dst > pos:
            parts.append(jnp.zeros(lead + (dst - pos,), w.dtype))
        parts.append(w[..., src:src + width])
        pos = dst + width
    return jnp.concatenate(parts, axis=-1)


def _unpad_in_cols(w):
    order = sorted(IN_SEGS, key=lambda s: s[1])
    return jnp.concatenate([w[..., dst:dst + width] for dst, _, width in order], axis=-1)


def _pad_ff(w, axis):
    n = w.shape[axis] // FF_HALF
    parts = []
    for h in range(n):
        piece = lax.slice_in_dim(w, h * FF_HALF, (h + 1) * FF_HALF, axis=axis)
        zshape = list(w.shape)
        zshape[axis] = FF_BLK - FF_HALF
        parts += [piece, jnp.zeros(zshape, w.dtype)]
    return jnp.concatenate(parts, axis=axis)


def _unpad_ff(w, axis):
    n = w.shape[axis] // FF_BLK
    return jnp.concatenate([lax.slice_in_dim(w, h * FF_BLK, h * FF_BLK + FF_HALF, axis=axis) for h in range(n)],
                           axis=axis)


def _to_scan_cols(re_part, im_part):
    lead = re_part.shape[:-1]
    nb = S5_N // SCAN_W
    r = re_part.reshape(lead + (nb, SCAN_W))
    i = im_part.reshape(lead + (nb, SCAN_W))
    return jnp.concatenate([r, i], axis=-1).reshape(lead + (2 * S5_N,))


def _from_scan_cols(x):
    lead = x.shape[:-1]
    nb = S5_N // SCAN_W
    y = x.reshape(lead + (nb, 2, SCAN_W))
    return y[..., 0, :].reshape(lead + (S5_N,)), y[..., 1, :].reshape(lead + (S5_N,))


def _block_diag_in(b):
    eye = jnp.eye(S5_GROUPS, dtype=b.dtype)
    return jnp.einsum("gph,gk->ghkp", b, eye).reshape(S5_WIDTH, S5_N)


def _block_diag_in_t(m):
    eye = jnp.eye(S5_GROUPS, dtype=m.dtype)
    return jnp.einsum("ghkp,gk->gph", m.reshape(S5_GROUPS, S5_GROUP_CH, S5_GROUPS, S5_STATE), eye)


def _block_diag_out(c):
    eye = jnp.eye(S5_GROUPS, dtype=c.dtype)
    return jnp.einsum("ghp,gk->gpkh", c, eye).reshape(S5_N, S5_WIDTH)


def _block_diag_out_t(m):
    eye = jnp.eye(S5_GROUPS, dtype=m.dtype)
    return jnp.einsum("gpkh,gk->ghp", m.reshape(S5_GROUPS, S5_STATE, S5_GROUPS, S5_GROUP_CH), eye)


def _pad_heads(g, width):
    g = jnp.broadcast_to(g.reshape(-1, width), (N_HEADS, width))
    return jnp.pad(g, ((0, 0), (0, HP - width))).reshape(1, QKW)


def _prep_layer(l, gw, small, consts):
    e_mat, p_mat, r_mat, pl_mat = consts
    p = {}
    p["attn_g"] = small["attn_norm_g"][l].reshape(1, D_MODEL)
    p["ffn_g"] = small["ffn_norm_g"][l].reshape(1, D_MODEL)
    p["qlg"] = small["q_lat_norm_g"][l].reshape(1, MLA_Q_RANK)
    p["kvlg"] = small["kv_lat_norm_g"][l].reshape(1, MLA_KV_RANK)
    p["mqg"] = _pad_heads(small["mla_q_norm_g"][l], MLA_QK_DIM)
    p["mkg"] = _pad_heads(small["mla_k_norm_g"][l], MLA_QK_DIM)
    p["fqg"] = _pad_heads(small["fox_q_norm_g"][l], HEAD_V)
    p["fkg"] = _pad_heads(small["fox_k_norm_g"][l], HEAD_V)
    w_uq = gw["w_uq"].reshape(MLA_Q_RANK, N_HEADS, MLA_QK_DIM)
    p["w_uq"] = jnp.pad(w_uq, ((0, 0), (0, 0), (0, HP - MLA_QK_DIM))).reshape(MLA_Q_RANK, QKW)
    w_ukv = gw["w_ukv"].reshape(MLA_KV_RANK, N_HEADS, MLA_NOPE_DIM + HEAD_V)
    p["w_uk"] = jnp.pad(w_ukv[..., :MLA_NOPE_DIM], ((0, 0), (0, 0), (0, HP - MLA_NOPE_DIM))).reshape(MLA_KV_RANK, QKW)
    p["w_uv"] = w_ukv[..., MLA_NOPE_DIM:].reshape(MLA_KV_RANK, N_HEADS * HEAD_V)
    p["w_glu"] = gw["w_glu"]
    p["b_glu"] = small["s5_b_glu"][l].reshape(1, S5_WIDTH)
    p["d_row"] = small["s5_d"][l].reshape(1, S5_WIDTH)
    p["lam_re"] = small["s5_lambda_re"][l].reshape(S5_N, 1)
    p["lam_im"] = small["s5_lambda_im"][l].reshape(S5_N, 1)
    p["log_step"] = jnp.repeat(small["s5_log_step"][l], S5_STATE).reshape(S5_N, 1)
    p["b_re"] = small["s5_b_re"][l].reshape(S5_N, S5_GROUP_CH)
    p["b_im"] = small["s5_b_im"][l].reshape(S5_N, S5_GROUP_CH)
    c_re, c_im = small["s5_c_re"][l], small["s5_c_im"][l]
    p["c_blk"] = _to_scan_cols(_block_diag_out(c_re).T, -_block_diag_out(c_im).T).T
    p["fbias"] = small["fox_f_bias"][l]
    for nm in ("w_in", "w_out", "w_branch"):
        if nm in gw:
            p[nm] = gw[nm]
    return p


def _s5_param_call(p, l):
    ins = [p["lam_re"], p["lam_im"], p["log_step"], p["b_re"], p["b_im"]]
    outs = [((S5_N, 1), F32, (S5_N, 1), lambda i: (0, 0), None)] * 2 + \
           [((S5_N, S5_GROUP_CH), F32, (S5_N, S5_GROUP_CH), lambda i: (0, 0), None)] * 2
    return _tile_call(_s5_params, "s5_params", (1,), [_par(a) for a in ins], outs)


def _s5_param_bwd_call(p, cts):
    ins = [p["lam_re"], p["lam_im"], p["log_step"], p["b_re"], p["b_im"]]

    def fn(lr, li, ls, br, bi, g0, g1, g2, g3):
        _, vjp = jax.vjp(_s5_params, lr, li, ls, br, bi)
        return vjp((g0, g1, g2, g3))

    outs = [((S5_N, 1), F32, (S5_N, 1), lambda i: (0, 0), None)] * 3 + \
           [((S5_N, S5_GROUP_CH), F32, (S5_N, S5_GROUP_CH), lambda i: (0, 0), None)] * 2
    return _tile_call(fn, "s5_params_bwd", (1,), [_par(a) for a in ins + list(cts)], outs)


def _layer_fwd(x_prev, f_prev, p, tabs, consts, bsz, riders):
    arrived = {}

    def rider_of(name):
        return riders[name][0] if name in riders else None

    def hosted(name, res, n_out):
        if name in riders:
            arrived.update(zip(riders[name][1], res[n_out:]))
        return res[:n_out]

    t = x_prev.shape[0]
    seq = t // bsz
    nt = t // TM
    cos, sin = tabs
    e_mat, p_mat, r_mat, pl_mat = consts
    sv = {}

    if f_prev is None:
        x = x_prev
        h = hosted("norm_first", _tile_call(lambda xv, g: _rms(xv, g), "norm_first", (nt,),
                                            [_row(x), _par(p["attn_g"])], [_orow(t, D_MODEL, BF16)],
                                            rider=rider_of("norm_first")), 1)[0]
        p.update({n: arrived.pop(n) for n in FIRST_LATE_WEIGHTS})
    else:
        x, h = _tile_call(lambda xv, fv, g: (xv + fv, _rms(xv + fv, g)), "norm_attn", (nt,),
                          [_row(x_prev), _row(f_prev), _par(p["attn_g"])], [_orow(t, D_MODEL), _orow(t, D_MODEL, BF16)])
    sv["x"], sv["h"] = x, h
    proj = _mm(h, p["w_in"], "nn", "mm_in")
    sv["proj"] = proj

    def pre_fn(pa, cosv, sinv, qlg, w_uq, kvlg, w_uk, w_uv, mqg, mkg, fqg, fkg, em, pm, rm, plm):
        return _mixer_pre(pa[:, O_CQ:O_CQ + MLA_Q_RANK], pa[:, O_CKV:O_CKV + MLA_KV_RANK], pa[:, O_KR:O_KR + LANES],
                          pa[:, O_FQ:O_FQ + 256], pa[:, O_FK:O_FK + 256], qlg, w_uq, kvlg, w_uk, w_uv,
                          mqg, mkg, fqg, fkg, cosv, sinv, em, pm, rm, plm)

    pre_params = [p["qlg"], p["w_uq"], p["kvlg"], p["w_uk"], p["w_uv"], p["mqg"], p["mkg"], p["fqg"], p["fkg"],
                  e_mat, p_mat, r_mat, pl_mat]
    q_m, k_m, v_m, q_f, k_f = hosted("mixer_pre", _tile_call(
        pre_fn, "mixer_pre", (t // TM_H,), [_row_h(proj, PA), _row_h(cos), _row_h(sin)] + [_par(a) for a in pre_params],
        [_orow_h(t, QKW), _orow_h(t, QKW), _orow_h(t, 2 * LANES), _orow_h(t, QKW), _orow_h(t, QKW)],
        rider=rider_of("mixer_pre")), 5)
    sv.update(q_m=q_m, k_m=k_m, v_m=v_m, q_f=q_f, k_f=k_f)

    z = proj[:, O_FL:O_FL + N_HEADS].reshape(bsz, seq, N_HEADS).transpose(0, 2, 1).reshape(bsz * N_HEADS, seq)
    fb = jnp.tile(p["fbias"], bsz).reshape(bsz * N_HEADS, 1)
    cum = _forget_fwd(z, fb, "forget_fwd")
    ck = cum.reshape(bsz, 2, 2, seq)
    cq = ck.transpose(0, 1, 3, 2)
    sv.update(z=z, fb=fb, cq=cq, ck=ck)

    r3 = lambda a: a.reshape(bsz, seq, a.shape[-1])
    o_m, lse_m = hosted("attn_mla_fwd", _attn_fwd(r3(q_m), r3(k_m), r3(v_m), 0, None, None, MLA_QK_DIM ** -0.5,
                                                  "attn_mla_fwd", rider_of("attn_mla_fwd")), 2)
    o_f, lse_f = hosted("attn_fox_fwd", _attn_fwd(r3(q_f), r3(k_f), r3(proj), O_FV // LANES, cq, ck, HEAD_V ** -0.5,
                                                  "attn_fox_fwd", rider_of("attn_fox_fwd")), 2)
    o_m, o_f = o_m.reshape(t, 2 * LANES), o_f.reshape(t, 2 * LANES)
    sv.update(o_m=o_m, o_f=o_f, lse_m=lse_m, lse_f=lse_f)

    lbar_re, lbar_im, bbar_re, bbar_im = _s5_param_call(p, 0)
    lbar = _to_scan_cols(lbar_re.reshape(1, S5_N), lbar_im.reshape(1, S5_N))
    b_blk = _to_scan_cols(_block_diag_in(bbar_re.reshape(S5_GROUPS, S5_STATE, S5_GROUP_CH)),
                          _block_diag_in(bbar_im.reshape(S5_GROUPS, S5_STATE, S5_GROUP_CH)))
    sv.update(lbar=lbar, b_blk=b_blk)
    bu = _tile_call(lambda u, bb: bdot(u, bb), "s5_bu", (nt,), [_row(proj, S5_WIDTH, O_U // S5_WIDTH), _par(b_blk)],
                    [_orow(t, 2 * S5_N)])[0]
    state = hosted("s5_scan_fwd", _scan_fwd(bu, lbar, bsz, "s5_scan_fwd", rider_of("s5_scan_fwd")), 1)[0]
    sv["state"] = state
    post_params = [p["c_blk"], p["d_row"], p["w_glu"], p["b_glu"]]
    o_s = _tile_call(_s5_post, "s5_post", (nt,),
                     [_row(state), _row(proj, S5_WIDTH, O_U // S5_WIDTH)] + [_par(a) for a in post_params],
                     [_orow(t, S5_WIDTH)])[0]
    sv["o_s"] = o_s

    def merge_fn(oa, ob, oc, ga, gb, gc, wb):
        return _merge(oa, ob, oc, ga, gb, gc, wb[0], wb[1], wb[2])

    gate_specs = [_row(proj, D_MODEL, PA // D_MODEL + n) for n in range(3)]
    merged = _tile_call(merge_fn, "merge", (nt,), [_row(o_m), _row(o_f), _row(o_s)] + gate_specs + [_par(p["w_branch"])],
                        [_orow(t, D_MODEL, BF16)])[0]
    sv["merged"] = merged
    attn_out = _mm(merged, p["w_out"], "nn", "mm_out")

    x1, h2 = _tile_call(lambda xv, av, g: (xv + av, _rms(xv + av, g)), "norm_ffn", (nt,),
                        [_row(x), _row(attn_out), _par(p["ffn_g"])], [_orow(t, D_MODEL), _orow(t, D_MODEL, BF16)])
    sv["x1"], sv["h2"] = x1, h2
    p.update({n: arrived.pop(n) for n in FFN_WEIGHTS})
    up = _mm(h2, p["w_up"], "nn", "mm_up")
    sv["up"] = up

    def ffn_fn(upv, cw):
        return _ffn_mid(upv[:, :FF_BLK], upv[:, FF_BLK:], cw[0:1, :FF_BLK], cw[1:2, :FF_BLK], cw[2:3, :FF_BLK],
                        cw[0:1, FF_BLK:], cw[1:2, FF_BLK:], cw[2:3, FF_BLK:])

    nblk = FF_PW // FF_BLK
    act = _tile_call(ffn_fn, "ffn_mid", (nblk, bsz),
                     [(up, (seq, 2 * FF_BLK), lambda j, b: (b, j)), (p["conv"], (3, 2 * FF_BLK), lambda j, b: (0, j))],
                     [((t, FF_PW), BF16, (seq, FF_BLK), lambda j, b: (b, j), None)])[0]
    sv["act"] = act
    ffn_out = _mm(act, p["w_down"], "nn", "mm_down")
    return x1, ffn_out, sv, arrived


def _layer_bwd(dx2, p, sv, tabs, consts, bsz, scatter, carry, last_layer):
    lands, carry_lands = {}, {}

    def riding(host, g):
        names = BWD_RIDERS[host] + (("w_in",) if host == "mm_in_dx" and last_layer else ())
        return scatter({n: g[n.split(":")[0]] for n in names}), names

    def landed(host, res, n_out):
        names = BWD_RIDERS[host] + (("w_in",) if host == "mm_in_dx" and last_layer else ())
        lands.update(zip(names, res[n_out:]))
        return res[:n_out]

    t = dx2.shape[0]
    seq = t // bsz
    nt = t // TM
    cos, sin = tabs
    e_mat, p_mat, r_mat, pl_mat = consts
    g = {}

    g["w_down"] = _mm(sv["act"], dx2, "tn", "mm_down_dw", out_dtype=BF16)
    dact = _mm(dx2, p["w_down"], "nt", "mm_down_dx")

    def ffn_bwd_fn(upv, cw, da):
        args = (upv[:, :FF_BLK], upv[:, FF_BLK:], cw[0:1, :FF_BLK], cw[1:2, :FF_BLK], cw[2:3, :FF_BLK],
                cw[0:1, FF_BLK:], cw[1:2, FF_BLK:], cw[2:3, FF_BLK:])
        _, vjp = jax.vjp(_ffn_mid, *args)
        dg, dv, g0, g1, g2, v0, v1, v2 = vjp(da)
        return (jnp.concatenate([dg, dv], axis=1), jnp.concatenate([g0, v0], axis=1),
                jnp.concatenate([g1, v1], axis=1), jnp.concatenate([g2, v2], axis=1))

    nblk = FF_PW // FF_BLK
    cw_out = ((1, UP_W), F32, (1, 2 * FF_BLK), lambda j, b: (0, j), (1,))
    res = _tile_call(
        ffn_bwd_fn, "ffn_mid_bwd", (nblk, bsz),
        [(sv["up"], (seq, 2 * FF_BLK), lambda j, b: (b, j)), (p["conv"], (3, 2 * FF_BLK), lambda j, b: (0, j)),
         (dact, (seq, FF_BLK), lambda j, b: (b, j))],
        [((t, UP_W), BF16, (seq, 2 * FF_BLK), lambda j, b: (b, j), None), cw_out, cw_out, cw_out],
        rider=scatter(carry) if carry else None)
    dup, dc0, dc1, dc2 = res[:4]
    carry_lands.update(zip(carry, res[4:]))
    g["conv"] = jnp.concatenate([dc0, dc1, dc2], axis=0)
    g["w_up"] = _mm(sv["h2"], dup, "tn", "mm_up_dw", out_dtype=BF16)
    dh2 = landed("mm_up_dx", _mm(dup, p["w_up"], "nt", "mm_up_dx", rider=riding("mm_up_dx", g)[0]), 1)[0]

    def norm_bwd_fn(xv, gv, dh, dres):
        _, vjp = jax.vjp(_rms, xv, gv)
        dxv, dg = vjp(dh)
        return dres + dxv, dg

    d1, g["ffn_g"] = _tile_call(norm_bwd_fn, "norm_bwd", (nt,),
                                [_row(sv["x1"]), _par(p["ffn_g"]), _row(dh2), _row(dx2)],
                                [_orow(t, D_MODEL), _oacc((1, D_MODEL))])

    g["w_out"] = _mm(sv["merged"], d1, "tn", "mm_out_dw", out_dtype=BF16)
    dmerged = _mm(d1, p["w_out"], "nt", "mm_out_dx")

    def merge_bwd_fn(oa, ob, oc, ga, gb, gc, wb, dm):
        wf = wb.astype(F32)
        _, vjp = jax.vjp(_merge, oa, ob, oc, ga, gb, gc, wf[0], wf[1], wf[2])
        doa, dob, doc, dga, dgb, dgc, dwa, dwb, dwc = vjp(dm)
        return doa, dob, doc, jnp.concatenate([dga, dgb, dgc], axis=1), jnp.stack([dwa, dwb, dwc])

    proj = sv["proj"]
    gate_specs = [_row_h(proj, D_MODEL, PA // D_MODEL + n) for n in range(3)]
    do_m, do_f, do_s, dgl, g["w_branch"] = _tile_call(
        merge_bwd_fn, "merge_bwd", (t // TM_H,),
        [_row_h(sv["o_m"]), _row_h(sv["o_f"]), _row_h(sv["o_s"])] + gate_specs + [_par(p["w_branch"]), _row_h(dmerged)],
        [_orow_h(t, 2 * LANES), _orow_h(t, 2 * LANES), _orow_h(t, S5_WIDTH), _orow_h(t, 3 * D_MODEL),
         _oacc((3, S5_WIDTH, D_MODEL))])

    def post_bwd_fn(st, u, cb, dr, wg, bg, do):
        _, vjp = jax.vjp(_s5_post, st, u, cb.astype(F32), dr, wg.astype(F32), bg)
        return vjp(do)

    u_spec = _row(proj, S5_WIDTH, O_U // S5_WIDTH)
    dstate, du1, dc_blk, g_d, g["w_glu"], g_bglu = _tile_call(
        post_bwd_fn, "s5_post_bwd", (t // TM_H,),
        [_row_h(sv["state"]), _row_h(proj, S5_WIDTH, O_U // S5_WIDTH), _par(p["c_blk"]), _par(p["d_row"]),
         _par(p["w_glu"]), _par(p["b_glu"]), _row_h(do_s)],
        [_orow_h(t, 2 * S5_N), _orow_h(t, S5_WIDTH), _oacc((2 * S5_N, S5_WIDTH)), _oacc((1, S5_WIDTH)),
         _oacc((S5_WIDTH, S5_WIDTH)), _oacc((1, S5_WIDTH))])
    dbu, dlbar = landed("s5_scan_bwd", _scan_bwd(dstate, sv["state"], sv["lbar"], bsz, "s5_scan_bwd",
                                                 riding("s5_scan_bwd", g)[0]), 2)

    def bu_bwd_fn(u, bb, dbv, du_in):
        _, vjp = jax.vjp(bdot, u, bb)
        du, dbb = vjp(dbv)
        return du_in + du, dbb

    du, db_blk = _tile_call(bu_bwd_fn, "s5_bu_bwd", (nt,), [u_spec, _par(sv["b_blk"]), _row(dbu), _row(du1)],
                            [_orow(t, S5_WIDTH), _oacc((S5_WIDTH, 2 * S5_N))])
    dlr, dli = _from_scan_cols(dlbar)
    dbr, dbi = _from_scan_cols(db_blk)
    cts = (dlr.reshape(S5_N, 1), dli.reshape(S5_N, 1),
           _block_diag_in_t(dbr).reshape(S5_N, S5_GROUP_CH), _block_diag_in_t(dbi).reshape(S5_N, S5_GROUP_CH))
    g_lr, g_li, g_ls, g_br, g_bi = _s5_param_bwd_call(p, cts)
    dc_re, dc_im = _from_scan_cols(dc_blk.T)
    g["s5_lambda_re"] = g_lr.reshape(S5_GROUPS, S5_STATE)
    g["s5_lambda_im"] = g_li.reshape(S5_GROUPS, S5_STATE)
    g["s5_log_step"] = jnp.sum(g_ls.reshape(S5_GROUPS, S5_STATE), axis=1)
    g["s5_b_re"] = g_br.reshape(S5_GROUPS, S5_STATE, S5_GROUP_CH)
    g["s5_b_im"] = g_bi.reshape(S5_GROUPS, S5_STATE, S5_GROUP_CH)
    g["s5_c_re"] = _block_diag_out_t(dc_re.T)
    g["s5_c_im"] = -_block_diag_out_t(dc_im.T)
    g["s5_d"] = g_d.reshape(S5_GROUPS, S5_GROUP_CH)
    g["s5_b_glu"] = g_bglu.reshape(S5_WIDTH)

    r3 = lambda a: a.reshape(bsz, seq, a.shape[-1])
    dq_m, dk_m, dv_m = landed("attn_mla_bwd", _attn_bwd(
        r3(sv["q_m"]), r3(sv["k_m"]), r3(sv["v_m"]), 0, r3(sv["o_m"]), r3(do_m), sv["lse_m"], None, None,
        MLA_QK_DIM ** -0.5, "attn_mla_bwd", riding("attn_mla_bwd", g)[0]), 3)
    dq_f, dk_f, dv_f, dcq, dck = landed("attn_fox_bwd", _attn_bwd(
        r3(sv["q_f"]), r3(sv["k_f"]), r3(proj), O_FV // LANES, r3(sv["o_f"]), r3(do_f), sv["lse_f"], sv["cq"], sv["ck"],
        HEAD_V ** -0.5, "attn_fox_bwd", riding("attn_fox_bwd", g)[0]), 5)
    dcum = (dck + dcq.transpose(0, 1, 3, 2)).reshape(bsz * N_HEADS, seq)
    dz, dfb = _forget_bwd(dcum, sv["z"], sv["fb"], "forget_bwd")
    g["fox_f_bias"] = jnp.sum(dfb.reshape(bsz, N_HEADS), axis=0)
    dfl = jnp.pad(dz.reshape(bsz, N_HEADS, seq).transpose(0, 2, 1).reshape(t, N_HEADS), ((0, 0), (0, LANES - N_HEADS)))

    def pre_bwd_fn(pa, cosv, sinv, qlg, w_uq, kvlg, w_uk, w_uv, mqg, mkg, fqg, fkg, em, pm, rm, plm,
                   gq, gk, gv, gqf, gkf, gvf, gu, gfl, ggl):
        f = functools.partial(_mixer_pre, cos=cosv, sin=sinv, e_mat=em, p_mat=pm, r_mat=rm, pl_mat=plm)
        prim = (pa[:, O_CQ:O_CQ + MLA_Q_RANK], pa[:, O_CKV:O_CKV + MLA_KV_RANK], pa[:, O_KR:O_KR + LANES],
                pa[:, O_FQ:O_FQ + 256], pa[:, O_FK:O_FK + 256], qlg, w_uq.astype(F32), kvlg, w_uk.astype(F32),
                w_uv.astype(F32), mqg, mkg, fqg, fkg)
        _, vjp = jax.vjp(lambda *a: f(*a), *prim)
        dcq_, dckv, dkr, dfq, dfk, dqlg, dwuq, dkvlg, dwuk, dwuv, dmqg, dmkg, dfqg, dfkg = vjp((gq, gk, gv, gqf, gkf))
        zpad = jnp.zeros((pa.shape[0], PA - O_CQ - MLA_Q_RANK), F32)
        dproj = jnp.concatenate([dckv, dfq, dfk, gvf, gu, dkr, gfl, dcq_, zpad, ggl], axis=1)
        return dproj, dqlg, dwuq, dkvlg, dwuk, dwuv, dmqg, dmkg, dfqg, dfkg

    pre_params = [p["qlg"], p["w_uq"], p["kvlg"], p["w_uk"], p["w_uv"], p["mqg"], p["mkg"], p["fqg"], p["fkg"],
                  e_mat, p_mat, r_mat, pl_mat]
    cts_in = [dq_m.reshape(t, QKW), dk_m.reshape(t, QKW), dv_m.reshape(t, 2 * LANES), dq_f.reshape(t, QKW),
              dk_f.reshape(t, QKW), dv_f.reshape(t, 2 * LANES), du, dfl, dgl]
    (dproj, g_qlg, g_wuq, g_kvlg, g_wuk, g_wuv, g_mqg, g_mkg, g_fqg, g_fkg) = _tile_call(
        pre_bwd_fn, "mixer_pre_bwd", (t // TM_H,),
        [_row_h(proj, PA), _row_h(cos), _row_h(sin)] + [_par(a) for a in pre_params] + [_row_h(a) for a in cts_in],
        [_orow_h(t, PW, BF16), _oacc((1, MLA_Q_RANK)), _oacc((MLA_Q_RANK, QKW)), _oacc((1, MLA_KV_RANK)),
         _oacc((MLA_KV_RANK, QKW)), _oacc((MLA_KV_RANK, N_HEADS * HEAD_V)), _oacc((1, QKW)), _oacc((1, QKW)),
         _oacc((1, QKW)), _oacc((1, QKW))])
    g["q_lat_norm_g"] = g_qlg.reshape(MLA_Q_RANK)
    g["kv_lat_norm_g"] = g_kvlg.reshape(MLA_KV_RANK)
    heads = lambda a, w: jnp.sum(a.reshape(N_HEADS, HP)[:, :w], axis=0)
    g["mla_q_norm_g"], g["mla_k_norm_g"] = heads(g_mqg, MLA_QK_DIM), heads(g_mkg, MLA_QK_DIM)
    g["fox_q_norm_g"], g["fox_k_norm_g"] = heads(g_fqg, HEAD_V), heads(g_fkg, HEAD_V)
    g["w_uq"] = g_wuq.reshape(MLA_Q_RANK, N_HEADS, HP)[..., :MLA_QK_DIM].reshape(MLA_Q_RANK, N_HEADS * MLA_QK_DIM)
    g["w_ukv"] = jnp.concatenate([g_wuk.reshape(MLA_KV_RANK, N_HEADS, HP)[..., :MLA_NOPE_DIM],
                                  g_wuv.reshape(MLA_KV_RANK, N_HEADS, HEAD_V)], axis=-1).reshape(MLA_KV_RANK, QKW)

    g["w_in"] = _mm(sv["h"], dproj, "tn", "mm_in_dw", out_dtype=BF16)
    dh = landed("mm_in_dx", _mm(dproj, p["w_in"], "nt", "mm_in_dx", rider=riding("mm_in_dx", g)[0]), 1)[0]
    dx, g["attn_g"] = _tile_call(norm_bwd_fn, "norm_bwd", (nt,),
                                 [_row(sv["x"]), _par(p["attn_g"]), _row(dh), _row(d1)],
                                 [_orow(t, D_MODEL), _oacc((1, D_MODEL))])
    return dx, g, lands, carry_lands


def _row_tile(r, c, max_elems):
    if r * c <= max_elems:
        return r
    best = None
    for d in range(8, r, 8):
        if r % d == 0 and d * c <= max_elems:
            best = d
    assert best is not None, (r, c)
    return best


def _sum8(land, name):
    _, r, c = land.shape
    tr = _row_tile(r, N_DEV * c, 1 << 20)

    def fn(lv):
        acc = lv[0].astype(F32)
        for i in range(1, N_DEV):
            acc = acc + lv[i].astype(F32)
        return acc

    return _tile_call(fn, name, (r // tr,), [(land, (N_DEV, tr, c), lambda i: (0, i, 0))],
                      [((r, c), F32, (tr, c), lambda i: (i, 0), None)])[0]


def _adamw_layer(w, g, m, v, l, prev, name):
    _, r, c = w.shape
    tr = _row_tile(r, c, 1 << 19)
    lay = pl.BlockSpec((None, tr, c), lambda i: (l, i, 0))

    def body(w_ref, g_ref, m_ref, v_ref, *rest):
        g_out, d_out, m_out, v_out = rest[-4:]
        gv = g_ref[...]
        d_out[...], m_out[...], v_out[...] = _adamw(w_ref[...], gv, m_ref[...], v_ref[...])
        g_out[...] = gv

    in_specs = [lay, pl.BlockSpec((tr, c), lambda i: (i, 0)), lay, lay]
    args = [w, g, m, v]
    aliases = {}
    if prev is not None:
        in_specs += [pl.BlockSpec(memory_space=pl.ANY)] * 4
        args += list(prev)
        aliases = {4 + k: k for k in range(4)}
    return _pcall(body, name=name, grid=(r // tr,), in_specs=in_specs, out_specs=[lay] * 4,
                  out_shape=[jax.ShapeDtypeStruct(w.shape, F32)] * 4, aliases=aliases)(*args)


SMALL_NAMES = ("attn_norm_g", "q_lat_norm_g", "kv_lat_norm_g", "mla_q_norm_g", "mla_k_norm_g", "fox_q_norm_g",
               "fox_k_norm_g", "fox_f_bias", "s5_lambda_re", "s5_lambda_im", "s5_b_re", "s5_b_im", "s5_c_re", "s5_c_im",
               "s5_d", "s5_log_step", "s5_b_glu", "ffn_norm_g")
BIG_NAMES = ("w_in", "w_uq", "w_ukv", "s5_w_glu", "w_branch", "w_out", "w_up", "ffn_conv_w", "w_down")
ALL_NAMES = ("attn_norm_g", "w_in", "q_lat_norm_g", "w_uq", "kv_lat_norm_g", "w_ukv", "mla_q_norm_g", "mla_k_norm_g",
             "fox_q_norm_g", "fox_k_norm_g", "fox_f_bias", "s5_lambda_re", "s5_lambda_im", "s5_b_re", "s5_b_im",
             "s5_c_re", "s5_c_im", "s5_d", "s5_log_step", "s5_w_glu", "s5_b_glu", "w_branch", "w_out", "ffn_norm_g",
             "w_up", "ffn_conv_w", "w_down")


def _pack(arrs):
    tile = 8 * LANES
    parts = []
    for a in arrs:
        n = int(np.prod(a.shape))
        tot = -(-n // tile) * tile
        parts.append(jnp.pad(a.reshape(-1), (0, tot - n)).reshape(tot // LANES, LANES))
    rows = sum(a.shape[0] for a in parts)
    parts.append(jnp.zeros((-rows % 256, LANES), parts[0].dtype))
    return jnp.concatenate(parts, axis=0)


def _unpack(packed, like):
    tile = 8 * LANES
    out, pos = [], 0
    for a in like:
        n = int(np.prod(a.shape))
        rows = -(-n // tile) * 8
        out.append(packed[pos:pos + rows].reshape(-1)[:n].reshape(a.shape))
        pos += rows
    return out


def _rope_tables(positions, t, rider=None):
    pos = positions.reshape(t, 1)
    inv = _inv_freq_row()

    def fn(pv, iv):
        ang = pv.astype(F32) * iv
        return jnp.cos(ang), jnp.sin(ang)

    return _tile_call(fn, "rope_tables", (t // TM,), [_row(pos), _par(inv)], [_orow(t, QKW), _orow(t, QKW)],
                      rider=rider)


def _loss_call(x1, f, target):
    t = x1.shape[0]

    def fn(xv, fv, tv):
        e = xv + fv - tv
        per_tok = jnp.sum(e * e, axis=1, keepdims=True) * (1.0 / D_MODEL)
        return 0.5 * jnp.sum(per_tok, axis=0, keepdims=True), e * (1.0 / D_MODEL)

    return _tile_call(fn, "loss", (t // TM,), [_row(x1), _row(f), _row(target)], [_oacc((1, 1)), _orow(t, D_MODEL)])


def _local_shards(l, w):
    return {
        "w_in": _pad_in_cols(w["w_in"][l]).astype(BF16),
        "w_uq": w["w_uq"][l].reshape(-1, N_HEADS * MLA_QK_DIM).astype(BF16),
        "w_ukv": w["w_ukv"][l].reshape(-1, QKW).astype(BF16),
        "w_glu": w["s5_w_glu"][l].astype(BF16),
        "w_branch": w["w_branch"][l].astype(BF16),
        "w_out": w["w_out"][l].astype(BF16),
        "w_up": _pad_ff(w["w_up"][l], 1).astype(BF16),
        "conv": _pad_ff(w["ffn_conv_w"][l], 1),
        "w_down": _pad_ff(w["w_down"][l], 0).astype(BF16),
    }


GATHERED_SHAPES = {"w_in": (D_MODEL, PW), "w_uq": (MLA_Q_RANK, N_HEADS * MLA_QK_DIM), "w_ukv": (MLA_KV_RANK, QKW),
                   "w_glu": (S5_WIDTH, S5_WIDTH), "w_branch": (3, S5_WIDTH, D_MODEL), "w_out": (D_MODEL, D_MODEL),
                   "w_up": (D_MODEL, UP_W), "conv": (3, UP_W), "w_down": (FF_PW, D_MODEL)}
FFN_WEIGHTS = ("w_up", "conv", "w_down")
FIRST_EARLY_WEIGHTS = ("w_in", "w_uq", "w_ukv", "w_glu")
FIRST_LATE_WEIGHTS = ("w_out", "w_branch")
FWD_RIDERS_OWN = {"attn_mla_fwd": ("w_up", "conv"), "s5_scan_fwd": ("w_down",)}
FWD_RIDERS_NEXT = {"attn_fox_fwd": ("w_in", "w_uq", "w_ukv", "w_glu"), "mixer_pre": ("w_out", "w_branch")}
BWD_RIDERS = {"mm_up_dx": ("w_up:0",), "s5_scan_bwd": ("w_down",), "attn_mla_bwd": ("w_up:1",),
              "attn_fox_bwd": ("w_out", "w_branch", "w_glu", "conv"), "mm_in_dx": ("w_uq", "w_ukv")}
UPDATED_TRANSPOSED = ("w_up",)
WEIGHT_OF = {"w_in": "w_in", "w_uq": "w_uq", "w_ukv": "w_ukv", "w_glu": "s5_w_glu", "w_branch": "w_branch",
             "w_out": "w_out", "w_up": "w_up", "conv": "ffn_conv_w", "w_down": "w_down"}


def _unshard_layout(nm, g2):
    if nm == "w_in":
        return _unpad_in_cols(g2)
    if nm == "w_uq":
        return g2.reshape(-1, N_HEADS, MLA_QK_DIM)
    if nm == "w_ukv":
        return g2.reshape(-1, N_HEADS, MLA_NOPE_DIM + HEAD_V)
    if nm in ("w_up", "conv"):
        return _unpad_ff(g2, 1)
    if nm == "w_down":
        return _unpad_ff(g2, 0)
    return g2


def kernel(x, positions, attn_norm_g, w_in, q_lat_norm_g, w_uq, kv_lat_norm_g, w_ukv, mla_q_norm_g, mla_k_norm_g, fox_q_norm_g, fox_k_norm_g, fox_f_bias, s5_lambda_re, s5_lambda_im, s5_b_re, s5_b_im, s5_c_re, s5_c_im, s5_d, s5_log_step, s5_w_glu, s5_b_glu, w_branch, w_out, ffn_norm_g, w_up, ffn_conv_w, w_down, loss_target, m_attn_norm_g, m_w_in, m_q_lat_norm_g, m_w_uq, m_kv_lat_norm_g, m_w_ukv, m_mla_q_norm_g, m_mla_k_norm_g, m_fox_q_norm_g, m_fox_k_norm_g, m_fox_f_bias, m_s5_lambda_re, m_s5_lambda_im, m_s5_b_re, m_s5_b_im, m_s5_c_re, m_s5_c_im, m_s5_d, m_s5_log_step, m_s5_w_glu, m_s5_b_glu, m_w_branch, m_w_out, m_ffn_norm_g, m_w_up, m_ffn_conv_w, m_w_down, v_attn_norm_g, v_w_in, v_q_lat_norm_g, v_w_uq, v_kv_lat_norm_g, v_w_ukv, v_mla_q_norm_g, v_mla_k_norm_g, v_fox_q_norm_g, v_fox_k_norm_g, v_fox_f_bias, v_s5_lambda_re, v_s5_lambda_im, v_s5_b_re, v_s5_b_im, v_s5_c_re, v_s5_c_im, v_s5_d, v_s5_log_step, v_s5_w_glu, v_s5_b_glu, v_w_branch, v_w_out, v_ffn_norm_g, v_w_up, v_ffn_conv_w, v_w_down):
    args = locals()
    w = {n: args[n] for n in ALL_NAMES}
    m = {n: args["m_" + n] for n in ALL_NAMES}
    v = {n: args["v_" + n] for n in ALL_NAMES}
    depth = w_in.shape[0]
    bsz, seq, _ = x.shape
    t = bsz * seq
    consts = _constants()
    all_shards = [_local_shards(l, w) for l in range(depth)]

    def gather_xfer(l, names):
        sub = {n: all_shards[l][n] for n in names}
        order, plan = _gather_plan(sub)
        return _Gather2([sub[n] for n in order], [(GATHERED_SHAPES[n], sub[n].dtype) for n in order], plan)

    res = _rope_tables(positions, t, gather_xfer(0, FIRST_EARLY_WEIGHTS))
    tabs, gathered = res[:2], dict(zip(FIRST_EARLY_WEIGHTS, res[2:]))
    small = {n: w[n] for n in SMALL_NAMES}

    all_names = list(GATHERED_SHAPES)
    def scatter_xfer(g, names):
        shapes = {n: _local_shards_shape(n) for n in names}
        return _Xfer([g[n] for n in names], [((N_DEV,) + shapes[n], g[n].dtype) for n in names],
                     _scatter_plan(list(names), shapes))

    xs = x.reshape(t, D_MODEL)
    f_prev = None
    saved, params = [], []
    for l in range(depth):
        p = _prep_layer(l, gathered, small, consts)
        riders = {host: (gather_xfer(l, names), names) for host, names in FWD_RIDERS_OWN.items()}
        if l == 0:
            riders["norm_first"] = (gather_xfer(0, FIRST_LATE_WEIGHTS), FIRST_LATE_WEIGHTS)
        if l + 1 < depth:
            riders.update({host: (gather_xfer(l + 1, names), names) for host, names in FWD_RIDERS_NEXT.items()})
        xs, f_prev, sv, gathered = _layer_fwd(xs, f_prev, p, tabs, consts, bsz, riders)
        saved.append(sv)
        params.append(p)

    loss_part, dy = _loss_call(xs, f_prev, loss_target.reshape(t, D_MODEL))
    loss = lax.psum(loss_part[0, 0], ("x", "y", "c"))

    small_grads = {n: [None] * depth for n in SMALL_NAMES}
    big = {n: None for n in BIG_NAMES}

    def finish(l, lands):
        for n in all_names:
            wn = WEIGHT_OF[n]
            if n == "w_up":
                gsum = jnp.concatenate([_sum8(lands["w_up:%d" % h], "sum_w_up") for h in range(2)], axis=1)
            else:
                ld = lands[n]
                gsum = _sum8(ld.reshape(N_DEV, -1, ld.shape[-1]), "sum_" + n).reshape(_local_shards_shape(n))
            gsum = _unshard_layout(n, gsum)
            if n in UPDATED_TRANSPOSED:
                gsum = gsum.T
            c = gsum.shape[-1]
            three = lambda a: (jnp.swapaxes(a, 1, 2) if n in UPDATED_TRANSPOSED else a).reshape(depth, -1, c)
            big[wn] = _adamw_layer(three(w[wn]), gsum.reshape(-1, c), three(m[wn]), three(v[wn]), l, big[wn],
                                   "adamw_" + n)

    def scatter(grads):
        return scatter_xfer(grads, list(grads))

    dx = dy
    carry, lands_above = {}, None
    for l in reversed(range(depth)):
        dx, g, lands, carry_lands = _layer_bwd(dx, params[l], saved[l], tabs, consts, bsz, scatter, carry, l == 0)
        if lands_above is not None:
            finish(l + 1, {**lands_above, **carry_lands})
        carry, lands_above = {"w_in": g["w_in"]}, lands
        small_grads["attn_norm_g"][l] = g["attn_g"].reshape(D_MODEL)
        small_grads["ffn_norm_g"][l] = g["ffn_g"].reshape(D_MODEL)
        for n in SMALL_NAMES:
            if n not in ("attn_norm_g", "ffn_norm_g"):
                small_grads[n][l] = g[n]
    finish(0, lands_above)

    sg = [jnp.stack(small_grads[n]).reshape(w[n].shape) for n in SMALL_NAMES]
    packed = _pack(sg)
    land = _exchange("gather_small_grads", _Gather2([packed], [((N_DEV,) + packed.shape, F32)],
                                                    _scatter_plan(["small"], {})))[0]
    gs = _sum8(land, "sum_small")
    one = lambda names_of: _pack([names_of[n] for n in SMALL_NAMES])[None]
    small_res = _adamw_layer(one(w), gs, one(m), one(v), 0, None, "adamw_small")
    like = [w[n] for n in SMALL_NAMES]
    small_out = [dict(zip(SMALL_NAMES, _unpack(a[0], like))) for a in small_res]

    def out_of(kind, n):
        if n in SMALL_NAMES:
            return small_out[kind][n]
        if n in UPDATED_TRANSPOSED:
            return jnp.swapaxes(big[n][kind], 1, 2)
        return big[n][kind].reshape(w[n].shape)

    outs = [loss, dx.reshape(bsz, seq, D_MODEL)]
    for kind in range(4):
        outs += [out_of(kind, n) for n in ALL_NAMES]
    return tuple(outs)


def _local_shards_shape(nm):
    return {"w_in": (D_MODEL // N_DEV, PW), "w_uq": (MLA_Q_RANK // N_DEV, N_HEADS * MLA_QK_DIM),
            "w_ukv": (MLA_KV_RANK // N_DEV, QKW), "w_glu": (S5_WIDTH // N_DEV, S5_WIDTH),
            "w_branch": (3, S5_WIDTH, LANES), "w_out": (D_MODEL // N_DEV, D_MODEL), "w_up": (D_MODEL, 2 * FF_BLK),
            "conv": (3, 2 * FF_BLK), "w_down": (FF_BLK, D_MODEL), "w_up:0": (D_MODEL, FF_BLK),
            "w_up:1": (D_MODEL, FF_BLK)}[nm]
```

```python
import functools
import math

import numpy as np
import jax
import jax.numpy as jnp
from jax import lax
from jax.experimental import pallas as pl
from jax.experimental.pallas import tpu as pltpu

F32, BF16 = jnp.float32, jnp.bfloat16

D_MODEL = 1024
N_DEV = 8
MLA_Q_RANK, MLA_KV_RANK, MLA_ROPE_DIM, MLA_NOPE_DIM, MLA_QK_DIM = 384, 256, 32, 64, 96
N_HEADS, HEAD_V = 4, 64
S5_GROUPS, S5_GROUP_CH, S5_STATE, S5_WIDTH = 16, 16, 64, 256
S5_N = S5_GROUPS * S5_STATE
D_FF = 2816
D_IN = 4772
ROPE_THETA = 10000.0
NORM_EPS = 1e-6
NEG_INF = -1e30
ADAM_LR, ADAM_B1, ADAM_B2, ADAM_EPS, ADAM_WD, ADAM_STEP = 0.001, 0.9, 0.999, 1e-08, 0.01, 10

VMEM_LIMIT_BYTES = 56 * 1024 * 1024
LANES = 128
TM = 512
TM_H = 256
TQ = 512
QF = 128
QC = 512
HP = 128
QKW = N_HEADS * HP

PW = 5120
PA = 2048
IN_SEGS = ((0, 384, 256),
           (256, 672, 256),
           (512, 928, 256),
           (768, 1184, 256),
           (1024, 1444, 256),
           (1280, 640, 32),
           (1408, 1440, 4),
           (1536, 0, 384),
           (2048, 1700, 3072))
O_CKV, O_FQ, O_FK, O_FV, O_U, O_KR, O_FL, O_CQ = 0, 256, 512, 768, 1024, 1280, 1408, 1536

FF_BLK = 384
FF_HALF = D_FF // 8
FF_PW = 8 * FF_BLK
UP_W = 2 * FF_PW

NN = (((1,), (0,)), ((), ()))
NT = (((1,), (1,)), ((), ()))
TN = (((0,), (0,)), ((), ()))


def _pcall(body, *, name, grid, in_specs, out_specs, out_shape, scratch_shapes=(), rider=None, aliases=None):
    params = pltpu.CompilerParams(dimension_semantics=("arbitrary",) * len(grid), vmem_limit_bytes=VMEM_LIMIT_BYTES)
    if rider is None:
        return pl.pallas_call(body, name=name, grid=grid, in_specs=in_specs, out_specs=out_specs, out_shape=out_shape,
                              scratch_shapes=list(scratch_shapes), input_output_aliases=aliases or {},
                              compiler_params=params)
    assert not aliases
    in_specs, out_specs, out_shape = list(in_specs), list(out_specs), list(out_shape)
    n_in, n_out, n_scr, nx = len(in_specs), len(out_specs), len(scratch_shapes), rider.n
    any_spec = pl.BlockSpec(memory_space=pl.ANY)

    def at_step(last):
        conds = [pl.program_id(a) == (g - 1 if last else 0) for a, g in enumerate(grid)]
        return functools.reduce(jnp.logical_and, conds)

    def body_with_rider(*refs):
        ins, xsrc = refs[:n_in], refs[n_in:n_in + nx]
        outs, xout = refs[n_in + nx:n_in + nx + n_out], refs[n_in + nx + n_out:n_in + 2 * nx + n_out]
        scr = refs[n_in + 2 * nx + n_out:n_in + 2 * nx + n_out + n_scr]
        sems = refs[n_in + 2 * nx + n_out + n_scr:]

        @pl.when(at_step(False))
        def _():
            rider.start(rider.copies(xsrc, xout, *sems))

        body(*ins, *outs, *scr)

        @pl.when(at_step(True))
        def _():
            rider.wait(rider.copies(xsrc, xout, *sems))

    call = pl.pallas_call(body_with_rider, name=name, grid=grid, in_specs=in_specs + [any_spec] * nx,
                          out_specs=out_specs + [any_spec] * nx,
                          out_shape=out_shape + [jax.ShapeDtypeStruct(s, d) for s, d in rider.out_shapes],
                          scratch_shapes=list(scratch_shapes) + rider.sems(), compiler_params=params)
    return lambda *args: call(*args, *rider.srcs)


def _tile_call(fn, name, grid, ins, outs, rider=None):
    n_in = len(ins)

    def body(*refs):
        res = fn(*[r[...] for r in refs[:n_in]])
        if not isinstance(res, (tuple, list)):
            res = (res,)
        assert len(res) == len(outs), (name, len(res), len(outs))
        for r, v, o in zip(refs[n_in:], res, outs):
            v = v.astype(r.dtype)
            if o[4] is None:
                r[...] = v
            else:
                first = functools.reduce(jnp.logical_and, [pl.program_id(a) == 0 for a in o[4]])

                @pl.when(first)
                def _():
                    r[...] = v

                @pl.when(jnp.logical_not(first))
                def _():
                    r[...] += v

    res = _pcall(body, name=name, grid=grid,
                 in_specs=[pl.BlockSpec(b, m) for _, b, m in ins],
                 out_specs=[pl.BlockSpec(o[2], o[3]) for o in outs],
                 out_shape=[jax.ShapeDtypeStruct(o[0], o[1]) for o in outs], rider=rider)(*[a for a, _, _ in ins])
    return res


def _row(a, width=None, col_block=0, tm=TM):
    w = a.shape[1] if width is None else width
    return (a, (tm, w), lambda i, c=col_block: (i, c))


def _par(a):
    nd = a.ndim
    return (a, a.shape, lambda i, nd=nd: (0,) * nd)


def _orow(t, w, dtype=F32, tm=TM):
    return ((t, w), dtype, (tm, w), lambda i: (i, 0), None)


def _row_h(a, width=None, col_block=0):
    return _row(a, width, col_block, tm=TM_H)


def _orow_h(t, w, dtype=F32):
    return _orow(t, w, dtype, tm=TM_H)


def _oacc(shape):
    nd = len(shape)
    return (tuple(shape), F32, tuple(shape), lambda i, nd=nd: (0,) * nd, (0,))


def _pick(n, target):
    best = None
    for d in range(LANES, min(n, target) + 1, LANES):
        if n % d == 0:
            best = d
    return n if best is None else best


MM_VMEM_BUDGET = 44 * 1024 * 1024


def _mm_tiles(m, n, k, sa, sb, so, tm, tn, tk):
    for cm, cn in ((1024, 2048), (1024, 1024), (512, 2048), (512, 1024), (512, 512), (256, 1024), (256, 512)):
        pm, pn = _pick(m, cm), _pick(n, cn)
        if 2 * (pm * k * sa + k * pn * sb + pm * pn * so) <= MM_VMEM_BUDGET:
            return pm, pn, k
    return _pick(m, tm), _pick(n, tn), _pick(k, tk)


def _mm(a, b, mode, name, out_dtype=F32, tm=1024, tn=2048, tk=1024, rider=None):
    if mode == "nn":
        (m, k), (_, n) = a.shape, b.shape
    elif mode == "nt":
        (m, k), (n, _) = a.shape, b.shape
    else:
        (k, m), (_, n) = a.shape, b.shape
    tm, tn, tk = _mm_tiles(m, n, k, a.dtype.itemsize, b.dtype.itemsize, jnp.dtype(out_dtype).itemsize, tm, tn, tk)
    nk = k // tk
    dims = {"nn": NN, "nt": NT, "tn": TN}[mode]
    a_spec = (pl.BlockSpec((tk, tm), lambda i, j, l: (l, i)) if mode == "tn"
              else pl.BlockSpec((tm, tk), lambda i, j, l: (i, l)))
    b_spec = (pl.BlockSpec((tn, tk), lambda i, j, l: (j, l)) if mode == "nt"
              else pl.BlockSpec((tk, tn), lambda i, j, l: (l, j)))

    def body(a_ref, b_ref, o_ref, acc_ref):
        part = lax.dot_general(a_ref[...].astype(BF16), b_ref[...].astype(BF16), dims, preferred_element_type=F32)
        if nk == 1:
            o_ref[...] = part.astype(o_ref.dtype)
        else:
            l = pl.program_id(2)

            @pl.when(l == 0)
            def _():
                acc_ref[...] = part

            @pl.when(l > 0)
            def _():
                acc_ref[...] += part

            @pl.when(l == nk - 1)
            def _():
                o_ref[...] = acc_ref[...].astype(o_ref.dtype)

    res = _pcall(body, name=name, grid=(m // tm, n // tn, nk), in_specs=[a_spec, b_spec],
                 out_specs=[pl.BlockSpec((tm, tn), lambda i, j, l: (i, j))],
                 out_shape=[jax.ShapeDtypeStruct((m, n), out_dtype)],
                 scratch_shapes=[pltpu.VMEM((tm, tn) if nk > 1 else (8, LANES), F32)], rider=rider)(a, b)
    return res[0] if rider is None else res


def _dot(x, w, dims):
    return lax.dot_general(x.astype(BF16), w.astype(BF16), dims, preferred_element_type=F32)


@jax.custom_vjp
def bdot(x, w):
    return _dot(x, w, NN)


def _bdot_fwd(x, w):
    return _dot(x, w, NN), (x, w)


def _bdot_bwd(res, g):
    x, w = res
    return _dot(g, w, NT), _dot(x, g, TN)


bdot.defvjp(_bdot_fwd, _bdot_bwd)


def _split3(x):
    x1 = x.astype(BF16)
    r1 = x - x1.astype(F32)
    x2 = r1.astype(BF16)
    x3 = (r1 - x2.astype(F32)).astype(BF16)
    return x1, x2, x3


def _cdot(x, m, dims):
    return sum(lax.dot_general(p, m, dims, preferred_element_type=F32) for p in _split3(x))


def _block_dot(x, m, transposed):
    k_in = m.shape[1] if transposed else m.shape[0]
    dims = NT if transposed else NN
    return jnp.concatenate([_cdot(x[:, b * k_in:(b + 1) * k_in], m, dims) for b in range(x.shape[1] // k_in)], axis=1)


@jax.custom_vjp
def cdot(x, m):
    return _block_dot(x, m, False)


def _cdot_fwd(x, m):
    return _block_dot(x, m, False), m


def _cdot_bwd(m, g):
    return _block_dot(g, m, True), jnp.zeros_like(m)


cdot.defvjp(_cdot_fwd, _cdot_bwd)


@jax.custom_vjp
def tile_heads(y):
    return jnp.concatenate([y] * N_HEADS, axis=1)


def _tile_heads_fwd(y):
    return jnp.concatenate([y] * N_HEADS, axis=1), None


def _tile_heads_bwd(_, g):
    return (sum(g[:, h * HP:(h + 1) * HP] for h in range(N_HEADS)),)


tile_heads.defvjp(_tile_heads_fwd, _tile_heads_bwd)


def _shift_rows_impl(x, s, reverse):
    n = x.shape[0]
    idx = lax.broadcasted_iota(jnp.int32, x.shape, 0)
    if reverse:
        return jnp.where(idx < n - s, pltpu.roll(x, n - s, 0), 0.0)
    return jnp.where(idx >= s, pltpu.roll(x, s, 0), 0.0)


@functools.partial(jax.custom_vjp, nondiff_argnums=(1,))
def shift_rows(x, s):
    return _shift_rows_impl(x, s, False)


def _shift_rows_fwd(x, s):
    return _shift_rows_impl(x, s, False), None


def _shift_rows_bwd(s, _, g):
    return (_shift_rows_impl(g, s, True),)


shift_rows.defvjp(_shift_rows_fwd, _shift_rows_bwd)


def _rms(x, g):
    return x * lax.rsqrt(jnp.mean(x * x, axis=-1, keepdims=True) + NORM_EPS) * g


def _head_rms(x, e_mat, inv_n, g):
    ms = cdot(x * x, e_mat) * inv_n
    return x * lax.rsqrt(ms + NORM_EPS) * g


def _rope(y, p_mat, cos, sin):
    return y * cos + cdot(y, p_mat) * sin


def _constants():
    e = np.ones((HP, HP), np.float32)
    p = np.zeros((HP, HP), np.float32)
    r = np.zeros((LANES, HP), np.float32)
    pl64 = np.zeros((2 * HEAD_V, 2 * HP), np.float32)
    half = MLA_ROPE_DIM // 2
    for i in range(half):
        x1, x2 = MLA_NOPE_DIM + i, MLA_NOPE_DIM + half + i
        p[x2, x1] = -1.0
        p[x1, x2] = 1.0
    for i in range(MLA_ROPE_DIM):
        r[i, MLA_NOPE_DIM + i] = 1.0
    for h in range(2):
        for i in range(HEAD_V):
            pl64[h * HEAD_V + i, h * HP + i] = 1.0
    return tuple(jnp.asarray(a, BF16) for a in (e, p, r, pl64))


def _inv_freq_row():
    inv = ROPE_THETA ** (-jnp.arange(0, MLA_ROPE_DIM, 2, dtype=F32) / MLA_ROPE_DIM)
    head = jnp.concatenate([jnp.zeros((MLA_NOPE_DIM,), F32), inv, inv, jnp.zeros((HP - MLA_QK_DIM,), F32)])
    return jnp.tile(head, N_HEADS).reshape(1, QKW)


def _mixer_pre(c_q, c_kv, kr, fq, fk, qlg, w_uq, kvlg, w_uk, w_uv, mqg, mkg, fqg, fkg, cos, sin, e_mat, p_mat, r_mat, pl_mat):
    q = bdot(_rms(c_q, qlg), w_uq)
    ckvn = _rms(c_kv, kvlg)
    k = bdot(ckvn, w_uk) + tile_heads(cdot(kr, r_mat))
    v = bdot(ckvn, w_uv)
    q = _rope(_head_rms(q, e_mat, 1.0 / MLA_QK_DIM, mqg), p_mat, cos, sin)
    k = _rope(_head_rms(k, e_mat, 1.0 / MLA_QK_DIM, mkg), p_mat, cos, sin)
    qf = _head_rms(cdot(fq, pl_mat), e_mat, 1.0 / HEAD_V, fqg)
    kf = _head_rms(cdot(fk, pl_mat), e_mat, 1.0 / HEAD_V, fkg)
    return q, k, v, qf, kf


def _s5_post(state, u, c_blk, d_row, w_glu, b_glu):
    y = bdot(state, c_blk) + d_row * u
    y = jax.nn.gelu(y)
    return y * jax.nn.sigmoid(bdot(y, w_glu) + b_glu)


def _merge(o_a, o_b, o_c, g_a, g_b, g_c, w_a, w_b, w_c):
    return (jax.nn.sigmoid(g_a) * bdot(o_a, w_a) + jax.nn.sigmoid(g_b) * bdot(o_b, w_b)
            + jax.nn.sigmoid(g_c) * bdot(o_c, w_c))


def _ffn_mid(gate, val, wg0, wg1, wg2, wv0, wv1, wv2):
    cg = wg0 * shift_rows(gate, 2) + wg1 * shift_rows(gate, 1) + wg2 * gate
    cv = wv0 * shift_rows(val, 2) + wv1 * shift_rows(val, 1) + wv2 * val
    return jax.nn.silu(cg) * cv


def _s5_params(lam_re, lam_im, log_step, b_re, b_im):
    step = jnp.exp(log_step)
    zr, zi = lam_re * step, lam_im * step
    mag = jnp.exp(zr)
    lr, li = mag * jnp.cos(zi), mag * jnp.sin(zi)
    nr, ni = lr - 1.0, li
    den = lam_re * lam_re + lam_im * lam_im
    cr = (nr * lam_re + ni * lam_im) / den
    ci = (ni * lam_re - nr * lam_im) / den
    return lr, li, cr * b_re - ci * b_im, cr * b_im + ci * b_re


def _adamw(w, g, m, v):
    m = ADAM_B1 * m + (1.0 - ADAM_B1) * g
    v = ADAM_B2 * v + (1.0 - ADAM_B2) * (g * g)
    m_hat = m / (1.0 - ADAM_B1 ** ADAM_STEP)
    v_hat = v / (1.0 - ADAM_B2 ** ADAM_STEP)
    delta = -ADAM_LR * (m_hat / (jnp.sqrt(v_hat) + ADAM_EPS) + ADAM_WD * w)
    return delta, m, v


SCAN_W = 256


def _scan_seq(xr, xi, ar, ai, reverse):
    n = xr.shape[0]
    s = 1
    while s < n:
        sr, si = _shift_rows_impl(xr, s, reverse), _shift_rows_impl(xi, s, reverse)
        xr, xi = xr + ar * sr - ai * si, xi + ar * si + ai * sr
        ar, ai = ar * ar - ai * ai, 2.0 * ar * ai
        s *= 2
    return xr, xi


def _scan_fwd(bu, lbar, bsz, name, rider=None):
    t = bu.shape[0]
    seq = t // bsz
    nb = bu.shape[1] // (2 * SCAN_W)

    def fn(b, a):
        xr, xi = _scan_seq(b[:, :SCAN_W], b[:, SCAN_W:], a[:, :SCAN_W], a[:, SCAN_W:], False)
        return jnp.concatenate([xr, xi], axis=1)

    blk = (seq, 2 * SCAN_W)
    return _tile_call(fn, name, (bsz, nb),
                      [(bu, blk, lambda i, j: (i, j)), (lbar, (1, 2 * SCAN_W), lambda i, j: (0, j))],
                      [((t, bu.shape[1]), F32, blk, lambda i, j: (i, j), None)], rider=rider)


def _scan_bwd(dstate, state, lbar, bsz, name, rider=None):
    t = state.shape[0]
    seq = t // bsz
    nb = state.shape[1] // (2 * SCAN_W)

    def fn(g, x, a):
        ar, ai = a[:, :SCAN_W], a[:, SCAN_W:]
        gr, gi = _scan_seq(g[:, :SCAN_W], g[:, SCAN_W:], ar, -ai, True)
        pr, pi = _shift_rows_impl(x[:, :SCAN_W], 1, False), _shift_rows_impl(x[:, SCAN_W:], 1, False)
        dar = jnp.sum(gr * pr + gi * pi, axis=0, keepdims=True)
        dai = jnp.sum(gi * pr - gr * pi, axis=0, keepdims=True)
        return jnp.concatenate([gr, gi], axis=1), jnp.concatenate([dar, dai], axis=1)

    blk = (seq, 2 * SCAN_W)
    return _tile_call(fn, name, (nb, bsz),
                      [(dstate, blk, lambda j, i: (i, j)), (state, blk, lambda j, i: (i, j)),
                       (lbar, (1, 2 * SCAN_W), lambda j, i: (0, j))],
                      [((t, state.shape[1]), F32, blk, lambda j, i: (i, j), None),
                       ((1, state.shape[1]), F32, (1, 2 * SCAN_W), lambda j, i: (0, j), (1,))], rider=rider)


def _shift_lanes(x, s, reverse):
    n = x.shape[1]
    idx = lax.broadcasted_iota(jnp.int32, x.shape, 1)
    if reverse:
        return jnp.where(idx < n - s, pltpu.roll(x, n - s, 1), 0.0)
    return jnp.where(idx >= s, pltpu.roll(x, s, 1), 0.0)


def _cumsum_lanes(x, reverse):
    s = 1
    while s < x.shape[1]:
        x = x + _shift_lanes(x, s, reverse)
        s *= 2
    return x


def _forget_fwd(z, bias, name):
    def fn(zv, bv):
        x = zv + bv
        logf = jnp.minimum(x, 0.0) - jnp.log(1.0 + jnp.exp(-jnp.abs(x)))
        return _cumsum_lanes(logf, False)

    return _tile_call(fn, name, (1,), [_par(z), _par(bias)],
                      [(z.shape, F32, z.shape, lambda i: (0, 0), None)])[0]


def _forget_bwd(dc, z, bias, name):
    def fn(dcv, zv, bv):
        x = zv + bv
        dlogf = _cumsum_lanes(dcv, True)
        dz = dlogf * jax.nn.sigmoid(-x)
        return dz, jnp.sum(dz, axis=1, keepdims=True)

    rows = z.shape[0]
    return _tile_call(fn, name, (1,), [_par(dc), _par(z), _par(bias)],
                      [(z.shape, F32, z.shape, lambda i: (0, 0), None),
                       ((rows, 1), F32, (rows, 1), lambda i: (0, 0), None)])


def _col(v2, x):
    idx = lax.broadcasted_iota(jnp.int32, v2.shape, 1)
    return jnp.sum(jnp.where(idx == x, v2, 0.0), axis=1, keepdims=True)


def _two_cols(c0, c1):
    idx = lax.broadcasted_iota(jnp.int32, (c0.shape[0], 2), 1)
    return jnp.where(idx == 0, c0, c1)


def _rows2(r0, r1):
    idx = lax.broadcasted_iota(jnp.int32, (2, r0.shape[1]), 0)
    return jnp.where(idx == 0, r0, r1)


def _attn_specs(seq, v_blk0, has_bias):
    qblk = pl.BlockSpec((None, TQ, 2 * HP), lambda b, p, i: (b, i, p))
    kblk = pl.BlockSpec((None, seq, 2 * HP), lambda b, p, i: (b, 0, p))
    vblk = pl.BlockSpec((None, seq, LANES), lambda b, p, i: (b, 0, v_blk0 + p))
    oblk = pl.BlockSpec((None, TQ, LANES), lambda b, p, i: (b, i, p))
    rowblk = pl.BlockSpec((None, None, 2, TQ), lambda b, p, i: (b, p, 0, i))
    colblk = pl.BlockSpec((None, None, seq, 2), lambda b, p, i: (b, p, 0, 0))
    return qblk, kblk, vblk, oblk, rowblk, colblk


def _attn_fwd(q, k, v, v_blk0, c_cols, c_rows, scale, name, rider=None):
    bsz, seq, _ = q.shape
    nq = seq // TQ
    has_bias = c_cols is not None
    n_in = 5 if has_bias else 3
    nc = TQ // QF

    def body(*refs):
        q_ref, k_ref, v_ref = refs[:3]
        cq_ref, ck_ref = refs[3:5] if has_bias else (None, None)
        o_ref, lse_ref = refs[n_in:]
        qi = pl.program_id(2)
        qb = [[q_ref[c * QF:(c + 1) * QF, x * HP:(x + 1) * HP].astype(BF16) for c in range(nc)] for x in range(2)]
        key_in = lax.broadcasted_iota(jnp.int32, (TQ, QF), 0)
        qry_in = lax.broadcasted_iota(jnp.int32, (TQ, QF), 1)

        def block(j, carry, masked):
            ks = pl.multiple_of(j * TQ, TQ)
            vt = v_ref[pl.ds(ks, TQ), :].T.astype(BF16)
            new = []
            for x in range(2):
                kx = k_ref[pl.ds(ks, TQ), x * HP:(x + 1) * HP].astype(BF16)
                ckx = _col(ck_ref[pl.ds(ks, TQ), :], x) if has_bias else None
                for c in range(nc):
                    m, l, acc = carry[x * nc + c]
                    s = lax.dot_general(kx, qb[x][c], NT, preferred_element_type=F32) * scale
                    if has_bias:
                        s = s + (cq_ref[x:x + 1, c * QF:(c + 1) * QF] - ckx)
                    if masked:
                        s = jnp.where(qry_in + c * QF >= key_in, s, NEG_INF)
                    m_new = jnp.maximum(m, jnp.max(s, axis=0, keepdims=True))
                    alpha = jnp.exp(m - m_new)
                    p = jnp.exp(s - m_new)
                    l = alpha * l + jnp.sum(p, axis=0, keepdims=True)
                    acc = alpha * acc + lax.dot_general(vt, p.astype(BF16), NN, preferred_element_type=F32)
                    new.append((m_new, l, acc))
            return tuple(new)

        init = tuple((jnp.full((1, QF), NEG_INF, F32), jnp.zeros((1, QF), F32), jnp.zeros((LANES, QF), F32))
                     for _ in range(2 * nc))
        carry = lax.fori_loop(0, qi, lambda j, cr: block(j, cr, False), init)
        carry = block(qi, carry, True)
        lane = lax.broadcasted_iota(jnp.int32, (QF, LANES), 1)
        lse_rows = []
        for x in range(2):
            lse_rows.append(jnp.concatenate([carry[x * nc + c][0] + jnp.log(carry[x * nc + c][1]) for c in range(nc)],
                                            axis=1))
        for c in range(nc):
            (_, l0, a0), (_, l1, a1) = carry[c], carry[nc + c]
            o_ref[c * QF:(c + 1) * QF, :] = jnp.where(lane < HEAD_V, (a0 / l0).T, (a1 / l1).T)
        lse_ref[...] = _rows2(lse_rows[0], lse_rows[1])

    qblk, kblk, vblk, oblk, rowblk, colblk = _attn_specs(seq, v_blk0, has_bias)
    in_specs, args = [qblk, kblk, vblk], [q, k, v]
    if has_bias:
        in_specs += [rowblk, colblk]
        args += [c_rows, c_cols]
    return _pcall(body, name=name, grid=(bsz, 2, nq), in_specs=in_specs, out_specs=[oblk, rowblk],
                  out_shape=[jax.ShapeDtypeStruct((bsz, seq, 2 * LANES), F32),
                             jax.ShapeDtypeStruct((bsz, 2, 2, seq), F32)], rider=rider)(*args)


def _attn_bwd(q, k, v, v_blk0, o, do, lse, c_cols, c_rows, scale, name, rider=None):
    bsz, seq, _ = q.shape
    nq = seq // TQ
    has_bias = c_cols is not None
    n_in = 8 if has_bias else 6
    nc = TQ // QC

    def body(*refs):
        q_ref, k_ref, v_ref, o_ref, do_ref, lse_ref = refs[:6]
        cq_ref, ck_ref = refs[6:8] if has_bias else (None, None)
        dq_ref, dk_ref, dv_ref = refs[n_in:n_in + 3]
        dck_ref, dcq_ref = refs[n_in + 3:n_in + 5] if has_bias else (None, None)
        qi = pl.program_id(2)

        @pl.when(qi == 0)
        def _():
            dk_ref[...] = jnp.zeros_like(dk_ref)
            dv_ref[...] = jnp.zeros_like(dv_ref)
            if has_bias:
                dck_ref[...] = jnp.zeros_like(dck_ref)

        lane = lax.broadcasted_iota(jnp.int32, (QC, LANES), 1)
        ones8 = jnp.ones((8, LANES), BF16)
        qb, dob, delta, lse_r = [], [], [], []
        for x in range(2):
            hm = jnp.logical_and(lane >= x * HEAD_V, lane < (x + 1) * HEAD_V)
            qb.append([]), dob.append([]), delta.append([]), lse_r.append([])
            for c in range(nc):
                rows = slice(c * QC, (c + 1) * QC)
                do_c = jnp.where(hm, do_ref[rows, :], 0.0)
                prod = do_c * o_ref[rows, :]
                hi = prod.astype(BF16)
                lo = (prod - hi.astype(F32)).astype(BF16)
                d8 = (lax.dot_general(ones8, hi, NT, preferred_element_type=F32)
                      + lax.dot_general(ones8, lo, NT, preferred_element_type=F32))
                delta[x].append(d8[0:1, :])
                dob[x].append(do_c.astype(BF16))
                qb[x].append(q_ref[rows, x * HP:(x + 1) * HP].astype(BF16))
                lse_r[x].append(lse_ref[x:x + 1, rows])
        key_in = lax.broadcasted_iota(jnp.int32, (TQ, QC), 0)
        qry_in = lax.broadcasted_iota(jnp.int32, (TQ, QC), 1)

        def block(j, carry, masked):
            ks = pl.multiple_of(j * TQ, TQ)
            vb = v_ref[pl.ds(ks, TQ), :].astype(BF16)
            new, key_sums = [], []
            for x in range(2):
                k32 = k_ref[pl.ds(ks, TQ), x * HP:(x + 1) * HP]
                kx, kt = k32.astype(BF16), k32.T.astype(BF16)
                ckx = _col(ck_ref[pl.ds(ks, TQ), :], x) if has_bias else None
                dk_acc = jnp.zeros((TQ, HP), F32)
                dv_acc = jnp.zeros((TQ, LANES), F32)
                key_sum = jnp.zeros((TQ, 1), F32)
                for c in range(nc):
                    dqt, dcq = carry[x * nc + c]
                    s = lax.dot_general(kx, qb[x][c], NT, preferred_element_type=F32) * scale
                    if has_bias:
                        s = s + (cq_ref[x:x + 1, c * QC:(c + 1) * QC] - ckx)
                    if masked:
                        s = jnp.where(qry_in + c * QC >= key_in, s, NEG_INF)
                    p = jnp.exp(s - lse_r[x][c])
                    dp = lax.dot_general(vb, dob[x][c], NT, preferred_element_type=F32)
                    ds = p * (dp - delta[x][c])
                    dsb = ds.astype(BF16)
                    dk_acc = dk_acc + lax.dot_general(dsb, qb[x][c], NN, preferred_element_type=F32)
                    dv_acc = dv_acc + lax.dot_general(p.astype(BF16), dob[x][c], NN, preferred_element_type=F32)
                    dqt = dqt + lax.dot_general(kt, dsb, NN, preferred_element_type=F32)
                    if has_bias:
                        dcq = dcq + jnp.sum(ds, axis=0, keepdims=True)
                        key_sum = key_sum + jnp.sum(ds, axis=1, keepdims=True)
                    new.append((dqt, dcq))
                dk_ref[pl.ds(ks, TQ), x * HP:(x + 1) * HP] += dk_acc * scale
                dv_ref[pl.ds(ks, TQ), :] += dv_acc
                key_sums.append(key_sum)
            if has_bias:
                dck_ref[pl.ds(ks, TQ), :] -= _two_cols(key_sums[0], key_sums[1])
            return tuple(new)

        init = tuple((jnp.zeros((HP, QC), F32), jnp.zeros((1, QC), F32)) for _ in range(2 * nc))
        carry = lax.fori_loop(0, qi, lambda j, cr: block(j, cr, False), init)
        carry = block(qi, carry, True)
        for x in range(2):
            for c in range(nc):
                dq_ref[c * QC:(c + 1) * QC, x * HP:(x + 1) * HP] = carry[x * nc + c][0].T * scale
        if has_bias:
            dcq_ref[...] = _rows2(jnp.concatenate([carry[c][1] for c in range(nc)], axis=1),
                                  jnp.concatenate([carry[nc + c][1] for c in range(nc)], axis=1))

    qblk, kblk, vblk, oblk, rowblk, colblk = _attn_specs(seq, v_blk0, has_bias)
    in_specs = [qblk, kblk, vblk, oblk, oblk, rowblk]
    out_specs = [qblk, kblk, pl.BlockSpec((None, seq, LANES), lambda b, p, i: (b, 0, p))]
    out_shape = [jax.ShapeDtypeStruct((bsz, seq, QKW), F32), jax.ShapeDtypeStruct((bsz, seq, QKW), F32),
                 jax.ShapeDtypeStruct((bsz, seq, 2 * LANES), F32)]
    args = [q, k, v, o, do, lse]
    if has_bias:
        in_specs += [rowblk, colblk]
        out_specs += [colblk, rowblk]
        out_shape += [jax.ShapeDtypeStruct((bsz, 2, seq, 2), F32), jax.ShapeDtypeStruct((bsz, 2, 2, seq), F32)]
        args += [c_rows, c_cols]
    return _pcall(body, name=name, grid=(bsz, 2, nq), in_specs=in_specs, out_specs=out_specs,
                  out_shape=out_shape, rider=rider)(*args)


def _coords(idx):
    return (idx // 4, (idx // 2) % 2, idx % 2)


class _Xfer:
    def __init__(self, srcs, out_shapes, plan):
        self.srcs, self.out_shapes, self.plan = list(srcs), list(out_shapes), plan
        self.n = len(self.srcs)
        me0 = jnp.int32(0)
        n_pieces = [len(plan(a, me0, me0)) for a in range(self.n)]
        self.offs = np.concatenate([[0], np.cumsum(n_pieces)]).astype(int)
        self.total = int(self.offs[-1])

    def sems(self):
        remote = (N_DEV - 1) * self.total
        return [pltpu.SemaphoreType.DMA((remote,)), pltpu.SemaphoreType.DMA((remote,)),
                pltpu.SemaphoreType.DMA((self.total,))]

    def copies(self, src_refs, out_refs, send_sems, recv_sems, local_sems):
        me = 4 * lax.axis_index("x") + 2 * lax.axis_index("y") + lax.axis_index("c")
        out = []
        for a in range(self.n):
            for pi, (si, di) in enumerate(self.plan(a, me, me)):
                out.append((pltpu.make_async_copy(src_refs[a].at[si], out_refs[a].at[di],
                                                  local_sems.at[self.offs[a] + pi]), None))
        for kk in range(1, N_DEV):
            dest = (me + kk) % N_DEV
            src_dev = (me + N_DEV - kk) % N_DEV
            for a in range(self.n):
                pieces = self.plan(a, me, dest)
                landing = self.plan(a, src_dev, me)
                for pi, ((si, di), (_, li)) in enumerate(zip(pieces, landing)):
                    sem = (kk - 1) * self.total + self.offs[a] + pi
                    mk = functools.partial(pltpu.make_async_remote_copy, src_ref=src_refs[a].at[si],
                                           send_sem=send_sems.at[sem], recv_sem=recv_sems.at[sem],
                                           device_id=_coords(dest), device_id_type=pl.DeviceIdType.MESH)
                    out.append((mk(dst_ref=out_refs[a].at[di]), mk(dst_ref=out_refs[a].at[li])))
        return out

    @staticmethod
    def start(copies):
        for cp, _ in copies:
            cp.start()

    @staticmethod
    def wait(copies):
        for cp, rc in copies:
            if rc is None:
                cp.wait()
            else:
                cp.wait_send()
                rc.wait_recv()


def _exchange(name, xf):
    n = xf.n

    def body(*refs):
        copies = xf.copies(refs[:n], refs[n:2 * n], *refs[2 * n:])
        xf.start(copies)
        xf.wait(copies)

    any_spec = pl.BlockSpec(memory_space=pl.ANY)
    return pl.pallas_call(body, name=name, in_specs=[any_spec] * n, out_specs=[any_spec] * n,
                          out_shape=[jax.ShapeDtypeStruct(s, d) for s, d in xf.out_shapes],
                          scratch_shapes=xf.sems(),
                          compiler_params=pltpu.CompilerParams(has_side_effects=True))(*xf.srcs)


class _Gather2(_Xfer):
    def copies(self, src_refs, out_refs, send_sems, recv_sems, local_sems):
        x, y, c = lax.axis_index("x"), lax.axis_index("y"), lax.axis_index("c")
        num = lambda px, py, pc: 4 * px + 2 * py + pc
        me, sib = (x, y, c), (x, y, 1 - c)
        chips = [(1 - x, y), (x, 1 - y), (1 - x, 1 - y)]
        local, first, passed, landing = [], [], [], []
        for a in range(self.n):
            mine = self.plan(a, num(*me), num(*me))
            for pi, (si, di) in enumerate(mine):
                piece = self.offs[a] + pi

                def remote(k, src, dst, to, a=a, piece=piece):
                    return pltpu.make_async_remote_copy(
                        src_ref=src, dst_ref=dst, send_sem=send_sems.at[k * self.total + piece],
                        recv_sem=recv_sems.at[k * self.total + piece], device_id=to, device_id_type=pl.DeviceIdType.MESH)

                def slot(dev, a=a, pi=pi):
                    return out_refs[a].at[self.plan(a, num(*dev), num(*dev))[pi][1]]

                local.append(pltpu.make_async_copy(src_refs[a].at[si], out_refs[a].at[di], local_sems.at[piece]))
                first.append(remote(0, src_refs[a].at[si], out_refs[a].at[di], sib))
                landing.append([remote(0, slot(sib), slot(sib), sib)])
                for j, chip in enumerate(chips):
                    first.append(remote(1 + j, src_refs[a].at[si], out_refs[a].at[di], (*chip, c)))
                    passed.append((remote(1 + j, slot((*chip, c)), slot((*chip, c)), (*chip, c)),
                                   remote(4 + j, slot((*chip, c)), slot((*chip, c)), sib)))
                    landing[-1].append(remote(4 + j, slot((*chip, 1 - c)), slot((*chip, 1 - c)), sib))
        return local, first, passed, landing

    @staticmethod
    def start(copies):
        local, first, _, _ = copies
        for cp in local + first:
            cp.start()

    @staticmethod
    def wait(copies):
        local, first, passed, landing = copies
        for arrival, forward in passed:
            arrival.wait_recv()
            forward.start()
        for group in landing:
            for cp in group:
                cp.wait_recv()
        for cp in first + [fw for _, fw in passed]:
            cp.wait_send()
        for cp in local:
            cp.wait()


def _rows_of(j, n):
    return pl.ds(pl.multiple_of(j * n, n), n)


def _ffn_cols(j, half):
    return pl.multiple_of((2 * (j % 4) + half) * (2 * FF_BLK) + (j // 4) * FF_BLK, FF_BLK)


def _gather_plan(shards):
    names = list(shards)

    def plan(a, src_dev, _dest):
        nm = names[a]
        j = src_dev
        if nm in ("w_in", "w_uq", "w_ukv", "w_glu", "w_out", "w_down"):
            rows = shards[nm].shape[0]
            return [((slice(None), slice(None)), (_rows_of(j, rows), slice(None)))]
        if nm == "w_branch":
            return [((slice(None), slice(None), slice(None)), (slice(None), slice(None), _rows_of(j, LANES)))]
        if nm in ("w_up", "conv"):
            return [((slice(None), pl.ds(h * FF_BLK, FF_BLK)), (slice(None), pl.ds(_ffn_cols(j, h), FF_BLK)))
                    for h in range(2)]
        raise KeyError(nm)

    return names, plan


def _scatter_plan(names, shard_shapes):
    def plan(a, src_dev, dest):
        nm = names[a]
        j = dest
        if nm in ("w_in", "w_uq", "w_ukv", "w_glu", "w_out", "w_down"):
            rows = shard_shapes[nm][0]
            return [((_rows_of(j, rows), slice(None)), (src_dev, slice(None), slice(None)))]
        if nm == "w_branch":
            return [((slice(None), slice(None), _rows_of(j, LANES)), (src_dev, slice(None), slice(None), slice(None)))]
        if nm in ("w_up", "conv"):
            return [((slice(None), pl.ds(_ffn_cols(j, h), FF_BLK)), (src_dev, slice(None), pl.ds(h * FF_BLK, FF_BLK)))
                    for h in range(2)]
        if nm in ("w_up:0", "w_up:1"):
            h = int(nm[-1])
            return [((slice(None), pl.ds(_ffn_cols(j, h), FF_BLK)), (src_dev, slice(None), slice(None)))]
        if nm == "small":
            return [((slice(None), slice(None)), (src_dev, slice(None), slice(None)))]
        raise KeyError(nm)

    return plan


def _pad_in_cols(w):
    lead = w.shape[:-1]
    parts, pos = [], 0
    for dst, src, width in IN_SEGS:
        if dst > pos:
            parts.append(jnp.zeros(lead + (dst - pos,), w.dtype))
        parts.append(w[..., src:src + width])
        pos = dst + width
    return jnp.concatenate(parts, axis=-1)


def _unpad_in_cols(w):
    order = sorted(IN_SEGS, key=lambda s: s[1])
    return jnp.concatenate([w[..., dst:dst + width] for dst, _, width in order], axis=-1)


def _pad_ff(w, axis):
    n = w.shape[axis] // FF_HALF
    parts = []
    for h in range(n):
        piece = lax.slice_in_dim(w, h * FF_HALF, (h + 1) * FF_HALF, axis=axis)
        zshape = list(w.shape)
        zshape[axis] = FF_BLK - FF_HALF
        parts += [piece, jnp.zeros(zshape, w.dtype)]
    return jnp.concatenate(parts, axis=axis)


def _unpad_ff(w, axis):
    n = w.shape[axis] // FF_BLK
    return jnp.concatenate([lax.slice_in_dim(w, h * FF_BLK, h * FF_BLK + FF_HALF, axis=axis) for h in range(n)],
                           axis=axis)


def _to_scan_cols(re_part, im_part):
    lead = re_part.shape[:-1]
    nb = S5_N // SCAN_W
    r = re_part.reshape(lead + (nb, SCAN_W))
    i = im_part.reshape(lead + (nb, SCAN_W))
    return jnp.concatenate([r, i], axis=-1).reshape(lead + (2 * S5_N,))


def _from_scan_cols(x):
    lead = x.shape[:-1]
    nb = S5_N // SCAN_W
    y = x.reshape(lead + (nb, 2, SCAN_W))
    return y[..., 0, :].reshape(lead + (S5_N,)), y[..., 1, :].reshape(lead + (S5_N,))


def _block_diag_in(b):
    eye = jnp.eye(S5_GROUPS, dtype=b.dtype)
    return jnp.einsum("gph,gk->ghkp", b, eye).reshape(S5_WIDTH, S5_N)


def _block_diag_in_t(m):
    eye = jnp.eye(S5_GROUPS, dtype=m.dtype)
    return jnp.einsum("ghkp,gk->gph", m.reshape(S5_GROUPS, S5_GROUP_CH, S5_GROUPS, S5_STATE), eye)


def _block_diag_out(c):
    eye = jnp.eye(S5_GROUPS, dtype=c.dtype)
    return jnp.einsum("ghp,gk->gpkh", c, eye).reshape(S5_N, S5_WIDTH)


def _block_diag_out_t(m):
    eye = jnp.eye(S5_GROUPS, dtype=m.dtype)
    return jnp.einsum("gpkh,gk->ghp", m.reshape(S5_GROUPS, S5_STATE, S5_GROUPS, S5_GROUP_CH), eye)


def _pad_heads(g, width):
    g = jnp.broadcast_to(g.reshape(-1, width), (N_HEADS, width))
    return jnp.pad(g, ((0, 0), (0, HP - width))).reshape(1, QKW)


def _prep_layer(l, gw, small, consts):
    e_mat, p_mat, r_mat, pl_mat = consts
    p = {}
    p["attn_g"] = small["attn_norm_g"][l].reshape(1, D_MODEL)
    p["ffn_g"] = small["ffn_norm_g"][l].reshape(1, D_MODEL)
    p["qlg"] = small["q_lat_norm_g"][l].reshape(1, MLA_Q_RANK)
    p["kvlg"] = small["kv_lat_norm_g"][l].reshape(1, MLA_KV_RANK)
    p["mqg"] = _pad_heads(small["mla_q_norm_g"][l], MLA_QK_DIM)
    p["mkg"] = _pad_heads(small["mla_k_norm_g"][l], MLA_QK_DIM)
    p["fqg"] = _pad_heads(small["fox_q_norm_g"][l], HEAD_V)
    p["fkg"] = _pad_heads(small["fox_k_norm_g"][l], HEAD_V)
    w_uq = gw["w_uq"].reshape(MLA_Q_RANK, N_HEADS, MLA_QK_DIM)
    p["w_uq"] = jnp.pad(w_uq, ((0, 0), (0, 0), (0, HP - MLA_QK_DIM))).reshape(MLA_Q_RANK, QKW)
    w_ukv = gw["w_ukv"].reshape(MLA_KV_RANK, N_HEADS, MLA_NOPE_DIM + HEAD_V)
    p["w_uk"] = jnp.pad(w_ukv[..., :MLA_NOPE_DIM], ((0, 0), (0, 0), (0, HP - MLA_NOPE_DIM))).reshape(MLA_KV_RANK, QKW)
    p["w_uv"] = w_ukv[..., MLA_NOPE_DIM:].reshape(MLA_KV_RANK, N_HEADS * HEAD_V)
    p["w_glu"] = gw["w_glu"]
    p["b_glu"] = small["s5_b_glu"][l].reshape(1, S5_WIDTH)
    p["d_row"] = small["s5_d"][l].reshape(1, S5_WIDTH)
    p["lam_re"] = small["s5_lambda_re"][l].reshape(S5_N, 1)
    p["lam_im"] = small["s5_lambda_im"][l].reshape(S5_N, 1)
    p["log_step"] = jnp.repeat(small["s5_log_step"][l], S5_STATE).reshape(S5_N, 1)
    p["b_re"] = small["s5_b_re"][l].reshape(S5_N, S5_GROUP_CH)
    p["b_im"] = small["s5_b_im"][l].reshape(S5_N, S5_GROUP_CH)
    c_re, c_im = small["s5_c_re"][l], small["s5_c_im"][l]
    p["c_blk"] = _to_scan_cols(_block_diag_out(c_re).T, -_block_diag_out(c_im).T).T
    p["fbias"] = small["fox_f_bias"][l]
    for nm in ("w_in", "w_out", "w_branch"):
        if nm in gw:
            p[nm] = gw[nm]
    return p


def _s5_param_call(p, l):
    ins = [p["lam_re"], p["lam_im"], p["log_step"], p["b_re"], p["b_im"]]
    outs = [((S5_N, 1), F32, (S5_N, 1), lambda i: (0, 0), None)] * 2 + \
           [((S5_N, S5_GROUP_CH), F32, (S5_N, S5_GROUP_CH), lambda i: (0, 0), None)] * 2
    return _tile_call(_s5_params, "s5_params", (1,), [_par(a) for a in ins], outs)


def _s5_param_bwd_call(p, cts):
    ins = [p["lam_re"], p["lam_im"], p["log_step"], p["b_re"], p["b_im"]]

    def fn(lr, li, ls, br, bi, g0, g1, g2, g3):
        _, vjp = jax.vjp(_s5_params, lr, li, ls, br, bi)
        return vjp((g0, g1, g2, g3))

    outs = [((S5_N, 1), F32, (S5_N, 1), lambda i: (0, 0), None)] * 3 + \
           [((S5_N, S5_GROUP_CH), F32, (S5_N, S5_GROUP_CH), lambda i: (0, 0), None)] * 2
    return _tile_call(fn, "s5_params_bwd", (1,), [_par(a) for a in ins + list(cts)], outs)


def _layer_fwd(x_prev, f_prev, p, tabs, consts, bsz, riders):
    arrived = {}

    def rider_of(name):
        return riders[name][0] if name in riders else None

    def hosted(name, res, n_out):
        if name in riders:
            arrived.update(zip(riders[name][1], res[n_out:]))
        return res[:n_out]

    t = x_prev.shape[0]
    seq = t // bsz
    nt = t // TM
    cos, sin = tabs
    e_mat, p_mat, r_mat, pl_mat = consts
    sv = {}

    if f_prev is None:
        x = x_prev
        h = hosted("norm_first", _tile_call(lambda xv, g: _rms(xv, g), "norm_first", (nt,),
                                            [_row(x), _par(p["attn_g"])], [_orow(t, D_MODEL, BF16)],
                                            rider=rider_of("norm_first")), 1)[0]
        p.update({n: arrived.pop(n) for n in FIRST_LATE_WEIGHTS})
    else:
        x, h = _tile_call(lambda xv, fv, g: (xv + fv, _rms(xv + fv, g)), "norm_attn", (nt,),
                          [_row(x_prev), _row(f_prev), _par(p["attn_g"])], [_orow(t, D_MODEL), _orow(t, D_MODEL, BF16)])
    sv["x"], sv["h"] = x, h
    proj = _mm(h, p["w_in"], "nn", "mm_in")
    sv["proj"] = proj

    def pre_fn(pa, cosv, sinv, qlg, w_uq, kvlg, w_uk, w_uv, mqg, mkg, fqg, fkg, em, pm, rm, plm):
        return _mixer_pre(pa[:, O_CQ:O_CQ + MLA_Q_RANK], pa[:, O_CKV:O_CKV + MLA_KV_RANK], pa[:, O_KR:O_KR + LANES],
                          pa[:, O_FQ:O_FQ + 256], pa[:, O_FK:O_FK + 256], qlg, w_uq, kvlg, w_uk, w_uv,
                          mqg, mkg, fqg, fkg, cosv, sinv, em, pm, rm, plm)

    pre_params = [p["qlg"], p["w_uq"], p["kvlg"], p["w_uk"], p["w_uv"], p["mqg"], p["mkg"], p["fqg"], p["fkg"],
                  e_mat, p_mat, r_mat, pl_mat]
    q_m, k_m, v_m, q_f, k_f = hosted("mixer_pre", _tile_call(
        pre_fn, "mixer_pre", (t // TM_H,), [_row_h(proj, PA), _row_h(cos), _row_h(sin)] + [_par(a) for a in pre_params],
        [_orow_h(t, QKW), _orow_h(t, QKW), _orow_h(t, 2 * LANES), _orow_h(t, QKW), _orow_h(t, QKW)],
        rider=rider_of("mixer_pre")), 5)
    sv.update(q_m=q_m, k_m=k_m, v_m=v_m, q_f=q_f, k_f=k_f)

    z = proj[:, O_FL:O_FL + N_HEADS].reshape(bsz, seq, N_HEADS).transpose(0, 2, 1).reshape(bsz * N_HEADS, seq)
    fb = jnp.tile(p["fbias"], bsz).reshape(bsz * N_HEADS, 1)
    cum = _forget_fwd(z, fb, "forget_fwd")
    ck = cum.reshape(bsz, 2, 2, seq)
    cq = ck.transpose(0, 1, 3, 2)
    sv.update(z=z, fb=fb, cq=cq, ck=ck)

    r3 = lambda a: a.reshape(bsz, seq, a.shape[-1])
    o_m, lse_m = hosted("attn_mla_fwd", _attn_fwd(r3(q_m), r3(k_m), r3(v_m), 0, None, None, MLA_QK_DIM ** -0.5,
                                                  "attn_mla_fwd", rider_of("attn_mla_fwd")), 2)
    o_f, lse_f = hosted("attn_fox_fwd", _attn_fwd(r3(q_f), r3(k_f), r3(proj), O_FV // LANES, cq, ck, HEAD_V ** -0.5,
                                                  "attn_fox_fwd", rider_of("attn_fox_fwd")), 2)
    o_m, o_f = o_m.reshape(t, 2 * LANES), o_f.reshape(t, 2 * LANES)
    sv.update(o_m=o_m, o_f=o_f, lse_m=lse_m, lse_f=lse_f)

    lbar_re, lbar_im, bbar_re, bbar_im = _s5_param_call(p, 0)
    lbar = _to_scan_cols(lbar_re.reshape(1, S5_N), lbar_im.reshape(1, S5_N))
    b_blk = _to_scan_cols(_block_diag_in(bbar_re.reshape(S5_GROUPS, S5_STATE, S5_GROUP_CH)),
                          _block_diag_in(bbar_im.reshape(S5_GROUPS, S5_STATE, S5_GROUP_CH)))
    sv.update(lbar=lbar, b_blk=b_blk)
    bu = _tile_call(lambda u, bb: bdot(u, bb), "s5_bu", (nt,), [_row(proj, S5_WIDTH, O_U // S5_WIDTH), _par(b_blk)],
                    [_orow(t, 2 * S5_N)])[0]
    state = hosted("s5_scan_fwd", _scan_fwd(bu, lbar, bsz, "s5_scan_fwd", rider_of("s5_scan_fwd")), 1)[0]
    sv["state"] = state
    post_params = [p["c_blk"], p["d_row"], p["w_glu"], p["b_glu"]]
    o_s = _tile_call(_s5_post, "s5_post", (nt,),
                     [_row(state), _row(proj, S5_WIDTH, O_U // S5_WIDTH)] + [_par(a) for a in post_params],
                     [_orow(t, S5_WIDTH)])[0]
    sv["o_s"] = o_s

    def merge_fn(oa, ob, oc, ga, gb, gc, wb):
        return _merge(oa, ob, oc, ga, gb, gc, wb[0], wb[1], wb[2])

    gate_specs = [_row(proj, D_MODEL, PA // D_MODEL + n) for n in range(3)]
    merged = _tile_call(merge_fn, "merge", (nt,), [_row(o_m), _row(o_f), _row(o_s)] + gate_specs + [_par(p["w_branch"])],
                        [_orow(t, D_MODEL, BF16)])[0]
    sv["merged"] = merged
    attn_out = _mm(merged, p["w_out"], "nn", "mm_out")

    x1, h2 = _tile_call(lambda xv, av, g: (xv + av, _rms(xv + av, g)), "norm_ffn", (nt,),
                        [_row(x), _row(attn_out), _par(p["ffn_g"])], [_orow(t, D_MODEL), _orow(t, D_MODEL, BF16)])
    sv["x1"], sv["h2"] = x1, h2
    p.update({n: arrived.pop(n) for n in FFN_WEIGHTS})
    up = _mm(h2, p["w_up"], "nn", "mm_up")
    sv["up"] = up

    def ffn_fn(upv, cw):
        return _ffn_mid(upv[:, :FF_BLK], upv[:, FF_BLK:], cw[0:1, :FF_BLK], cw[1:2, :FF_BLK], cw[2:3, :FF_BLK],
                        cw[0:1, FF_BLK:], cw[1:2, FF_BLK:], cw[2:3, FF_BLK:])

    nblk = FF_PW // FF_BLK
    act = _tile_call(ffn_fn, "ffn_mid", (nblk, bsz),
                     [(up, (seq, 2 * FF_BLK), lambda j, b: (b, j)), (p["conv"], (3, 2 * FF_BLK), lambda j, b: (0, j))],
                     [((t, FF_PW), BF16, (seq, FF_BLK), lambda j, b: (b, j), None)])[0]
    sv["act"] = act
    ffn_out = _mm(act, p["w_down"], "nn", "mm_down")
    return x1, ffn_out, sv, arrived


def _layer_bwd(dx2, p, sv, tabs, consts, bsz, scatter, carry, last_layer):
    lands, carry_lands = {}, {}

    def riding(host, g):
        names = BWD_RIDERS[host] + (("w_in",) if host == "mm_in_dx" and last_layer else ())
        return scatter({n: g[n.split(":")[0]] for n in names}), names

    def landed(host, res, n_out):
        names = BWD_RIDERS[host] + (("w_in",) if host == "mm_in_dx" and last_layer else ())
        lands.update(zip(names, res[n_out:]))
        return res[:n_out]

    t = dx2.shape[0]
    seq = t // bsz
    nt = t // TM
    cos, sin = tabs
    e_mat, p_mat, r_mat, pl_mat = consts
    g = {}

    g["w_down"] = _mm(sv["act"], dx2, "tn", "mm_down_dw", out_dtype=BF16)
    dact = _mm(dx2, p["w_down"], "nt", "mm_down_dx")

    def ffn_bwd_fn(upv, cw, da):
        args = (upv[:, :FF_BLK], upv[:, FF_BLK:], cw[0:1, :FF_BLK], cw[1:2, :FF_BLK], cw[2:3, :FF_BLK],
                cw[0:1, FF_BLK:], cw[1:2, FF_BLK:], cw[2:3, FF_BLK:])
        _, vjp = jax.vjp(_ffn_mid, *args)
        dg, dv, g0, g1, g2, v0, v1, v2 = vjp(da)
        return (jnp.concatenate([dg, dv], axis=1), jnp.concatenate([g0, v0], axis=1),
                jnp.concatenate([g1, v1], axis=1), jnp.concatenate([g2, v2], axis=1))

    nblk = FF_PW // FF_BLK
    cw_out = ((1, UP_W), F32, (1, 2 * FF_BLK), lambda j, b: (0, j), (1,))
    res = _tile_call(
        ffn_bwd_fn, "ffn_mid_bwd", (nblk, bsz),
        [(sv["up"], (seq, 2 * FF_BLK), lambda j, b: (b, j)), (p["conv"], (3, 2 * FF_BLK), lambda j, b: (0, j)),
         (dact, (seq, FF_BLK), lambda j, b: (b, j))],
        [((t, UP_W), BF16, (seq, 2 * FF_BLK), lambda j, b: (b, j), None), cw_out, cw_out, cw_out],
        rider=scatter(carry) if carry else None)
    dup, dc0, dc1, dc2 = res[:4]
    carry_lands.update(zip(carry, res[4:]))
    g["conv"] = jnp.concatenate([dc0, dc1, dc2], axis=0)
    g["w_up"] = _mm(sv["h2"], dup, "tn", "mm_up_dw", out_dtype=BF16)
    dh2 = landed("mm_up_dx", _mm(dup, p["w_up"], "nt", "mm_up_dx", rider=riding("mm_up_dx", g)[0]), 1)[0]

    def norm_bwd_fn(xv, gv, dh, dres):
        _, vjp = jax.vjp(_rms, xv, gv)
        dxv, dg = vjp(dh)
        return dres + dxv, dg

    d1, g["ffn_g"] = _tile_call(norm_bwd_fn, "norm_bwd", (nt,),
                                [_row(sv["x1"]), _par(p["ffn_g"]), _row(dh2), _row(dx2)],
                                [_orow(t, D_MODEL), _oacc((1, D_MODEL))])

    g["w_out"] = _mm(sv["merged"], d1, "tn", "mm_out_dw", out_dtype=BF16)
    dmerged = _mm(d1, p["w_out"], "nt", "mm_out_dx")

    def merge_bwd_fn(oa, ob, oc, ga, gb, gc, wb, dm):
        wf = wb.astype(F32)
        _, vjp = jax.vjp(_merge, oa, ob, oc, ga, gb, gc, wf[0], wf[1], wf[2])
        doa, dob, doc, dga, dgb, dgc, dwa, dwb, dwc = vjp(dm)
        return doa, dob, doc, jnp.concatenate([dga, dgb, dgc], axis=1), jnp.stack([dwa, dwb, dwc])

    proj = sv["proj"]
    gate_specs = [_row_h(proj, D_MODEL, PA // D_MODEL + n) for n in range(3)]
    do_m, do_f, do_s, dgl, g["w_branch"] = _tile_call(
        merge_bwd_fn, "merge_bwd", (t // TM_H,),
        [_row_h(sv["o_m"]), _row_h(sv["o_f"]), _row_h(sv["o_s"])] + gate_specs + [_par(p["w_branch"]), _row_h(dmerged)],
        [_orow_h(t, 2 * LANES), _orow_h(t, 2 * LANES), _orow_h(t, S5_WIDTH), _orow_h(t, 3 * D_MODEL),
         _oacc((3, S5_WIDTH, D_MODEL))])

    def post_bwd_fn(st, u, cb, dr, wg, bg, do):
        _, vjp = jax.vjp(_s5_post, st, u, cb.astype(F32), dr, wg.astype(F32), bg)
        return vjp(do)

    u_spec = _row(proj, S5_WIDTH, O_U // S5_WIDTH)
    dstate, du1, dc_blk, g_d, g["w_glu"], g_bglu = _tile_call(
        post_bwd_fn, "s5_post_bwd", (t // TM_H,),
        [_row_h(sv["state"]), _row_h(proj, S5_WIDTH, O_U // S5_WIDTH), _par(p["c_blk"]), _par(p["d_row"]),
         _par(p["w_glu"]), _par(p["b_glu"]), _row_h(do_s)],
        [_orow_h(t, 2 * S5_N), _orow_h(t, S5_WIDTH), _oacc((2 * S5_N, S5_WIDTH)), _oacc((1, S5_WIDTH)),
         _oacc((S5_WIDTH, S5_WIDTH)), _oacc((1, S5_WIDTH))])
    dbu, dlbar = landed("s5_scan_bwd", _scan_bwd(dstate, sv["state"], sv["lbar"], bsz, "s5_scan_bwd",
                                                 riding("s5_scan_bwd", g)[0]), 2)

    def bu_bwd_fn(u, bb, dbv, du_in):
        _, vjp = jax.vjp(bdot, u, bb)
        du, dbb = vjp(dbv)
        return du_in + du, dbb

    du, db_blk = _tile_call(bu_bwd_fn, "s5_bu_bwd", (nt,), [u_spec, _par(sv["b_blk"]), _row(dbu), _row(du1)],
                            [_orow(t, S5_WIDTH), _oacc((S5_WIDTH, 2 * S5_N))])
    dlr, dli = _from_scan_cols(dlbar)
    dbr, dbi = _from_scan_cols(db_blk)
    cts = (dlr.reshape(S5_N, 1), dli.reshape(S5_N, 1),
           _block_diag_in_t(dbr).reshape(S5_N, S5_GROUP_CH), _block_diag_in_t(dbi).reshape(S5_N, S5_GROUP_CH))
    g_lr, g_li, g_ls, g_br, g_bi = _s5_param_bwd_call(p, cts)
    dc_re, dc_im = _from_scan_cols(dc_blk.T)
    g["s5_lambda_re"] = g_lr.reshape(S5_GROUPS, S5_STATE)
    g["s5_lambda_im"] = g_li.reshape(S5_GROUPS, S5_STATE)
    g["s5_log_step"] = jnp.sum(g_ls.reshape(S5_GROUPS, S5_STATE), axis=1)
    g["s5_b_re"] = g_br.reshape(S5_GROUPS, S5_STATE, S5_GROUP_CH)
    g["s5_b_im"] = g_bi.reshape(S5_GROUPS, S5_STATE, S5_GROUP_CH)
    g["s5_c_re"] = _block_diag_out_t(dc_re.T)
    g["s5_c_im"] = -_block_diag_out_t(dc_im.T)
    g["s5_d"] = g_d.reshape(S5_GROUPS, S5_GROUP_CH)
    g["s5_b_glu"] = g_bglu.reshape(S5_WIDTH)

    r3 = lambda a: a.reshape(bsz, seq, a.shape[-1])
    dq_m, dk_m, dv_m = landed("attn_mla_bwd", _attn_bwd(
        r3(sv["q_m"]), r3(sv["k_m"]), r3(sv["v_m"]), 0, r3(sv["o_m"]), r3(do_m), sv["lse_m"], None, None,
        MLA_QK_DIM ** -0.5, "attn_mla_bwd", riding("attn_mla_bwd", g)[0]), 3)
    dq_f, dk_f, dv_f, dcq, dck = landed("attn_fox_bwd", _attn_bwd(
        r3(sv["q_f"]), r3(sv["k_f"]), r3(proj), O_FV // LANES, r3(sv["o_f"]), r3(do_f), sv["lse_f"], sv["cq"], sv["ck"],
        HEAD_V ** -0.5, "attn_fox_bwd", riding("attn_fox_bwd", g)[0]), 5)
    dcum = (dck + dcq.transpose(0, 1, 3, 2)).reshape(bsz * N_HEADS, seq)
    dz, dfb = _forget_bwd(dcum, sv["z"], sv["fb"], "forget_bwd")
    g["fox_f_bias"] = jnp.sum(dfb.reshape(bsz, N_HEADS), axis=0)
    dfl = jnp.pad(dz.reshape(bsz, N_HEADS, seq).transpose(0, 2, 1).reshape(t, N_HEADS), ((0, 0), (0, LANES - N_HEADS)))

    def pre_bwd_fn(pa, cosv, sinv, qlg, w_uq, kvlg, w_uk, w_uv, mqg, mkg, fqg, fkg, em, pm, rm, plm,
                   gq, gk, gv, gqf, gkf, gvf, gu, gfl, ggl):
        f = functools.partial(_mixer_pre, cos=cosv, sin=sinv, e_mat=em, p_mat=pm, r_mat=rm, pl_mat=plm)
        prim = (pa[:, O_CQ:O_CQ + MLA_Q_RANK], pa[:, O_CKV:O_CKV + MLA_KV_RANK], pa[:, O_KR:O_KR + LANES],
                pa[:, O_FQ:O_FQ + 256], pa[:, O_FK:O_FK + 256], qlg, w_uq.astype(F32), kvlg, w_uk.astype(F32),
                w_uv.astype(F32), mqg, mkg, fqg, fkg)
        _, vjp = jax.vjp(lambda *a: f(*a), *prim)
        dcq_, dckv, dkr, dfq, dfk, dqlg, dwuq, dkvlg, dwuk, dwuv, dmqg, dmkg, dfqg, dfkg = vjp((gq, gk, gv, gqf, gkf))
        zpad = jnp.zeros((pa.shape[0], PA - O_CQ - MLA_Q_RANK), F32)
        dproj = jnp.concatenate([dckv, dfq, dfk, gvf, gu, dkr, gfl, dcq_, zpad, ggl], axis=1)
        return dproj, dqlg, dwuq, dkvlg, dwuk, dwuv, dmqg, dmkg, dfqg, dfkg

    pre_params = [p["qlg"], p["w_uq"], p["kvlg"], p["w_uk"], p["w_uv"], p["mqg"], p["mkg"], p["fqg"], p["fkg"],
                  e_mat, p_mat, r_mat, pl_mat]
    cts_in = [dq_m.reshape(t, QKW), dk_m.reshape(t, QKW), dv_m.reshape(t, 2 * LANES), dq_f.reshape(t, QKW),
              dk_f.reshape(t, QKW), dv_f.reshape(t, 2 * LANES), du, dfl, dgl]
    (dproj, g_qlg, g_wuq, g_kvlg, g_wuk, g_wuv, g_mqg, g_mkg, g_fqg, g_fkg) = _tile_call(
        pre_bwd_fn, "mixer_pre_bwd", (t // TM_H,),
        [_row_h(proj, PA), _row_h(cos), _row_h(sin)] + [_par(a) for a in pre_params] + [_row_h(a) for a in cts_in],
        [_orow_h(t, PW, BF16), _oacc((1, MLA_Q_RANK)), _oacc((MLA_Q_RANK, QKW)), _oacc((1, MLA_KV_RANK)),
         _oacc((MLA_KV_RANK, QKW)), _oacc((MLA_KV_RANK, N_HEADS * HEAD_V)), _oacc((1, QKW)), _oacc((1, QKW)),
         _oacc((1, QKW)), _oacc((1, QKW))])
    g["q_lat_norm_g"] = g_qlg.reshape(MLA_Q_RANK)
    g["kv_lat_norm_g"] = g_kvlg.reshape(MLA_KV_RANK)
    heads = lambda a, w: jnp.sum(a.reshape(N_HEADS, HP)[:, :w], axis=0)
    g["mla_q_norm_g"], g["mla_k_norm_g"] = heads(g_mqg, MLA_QK_DIM), heads(g_mkg, MLA_QK_DIM)
    g["fox_q_norm_g"], g["fox_k_norm_g"] = heads(g_fqg, HEAD_V), heads(g_fkg, HEAD_V)
    g["w_uq"] = g_wuq.reshape(MLA_Q_RANK, N_HEADS, HP)[..., :MLA_QK_DIM].reshape(MLA_Q_RANK, N_HEADS * MLA_QK_DIM)
    g["w_ukv"] = jnp.concatenate([g_wuk.reshape(MLA_KV_RANK, N_HEADS, HP)[..., :MLA_NOPE_DIM],
                                  g_wuv.reshape(MLA_KV_RANK, N_HEADS, HEAD_V)], axis=-1).reshape(MLA_KV_RANK, QKW)

    g["w_in"] = _mm(sv["h"], dproj, "tn", "mm_in_dw", out_dtype=BF16)
    dh = landed("mm_in_dx", _mm(dproj, p["w_in"], "nt", "mm_in_dx", rider=riding("mm_in_dx", g)[0]), 1)[0]
    dx, g["attn_g"] = _tile_call(norm_bwd_fn, "norm_bwd", (nt,),
                                 [_row(sv["x"]), _par(p["attn_g"]), _row(dh), _row(d1)],
                                 [_orow(t, D_MODEL), _oacc((1, D_MODEL))])
    return dx, g, lands, carry_lands


def _row_tile(r, c, max_elems):
    if r * c <= max_elems:
        return r
    best = None
    for d in range(8, r, 8):
        if r % d == 0 and d * c <= max_elems:
            best = d
    assert best is not None, (r, c)
    return best


def _sum8(land, name):
    _, r, c = land.shape
    tr = _row_tile(r, N_DEV * c, 1 << 20)

    def fn(lv):
        acc = lv[0].astype(F32)
        for i in range(1, N_DEV):
            acc = acc + lv[i].astype(F32)
        return acc

    return _tile_call(fn, name, (r // tr,), [(land, (N_DEV, tr, c), lambda i: (0, i, 0))],
                      [((r, c), F32, (tr, c), lambda i: (i, 0), None)])[0]


def _adamw_layer(w, g, m, v, l, prev, name):
    _, r, c = w.shape
    tr = _row_tile(r, c, 1 << 19)
    lay = pl.BlockSpec((None, tr, c), lambda i: (l, i, 0))

    def body(w_ref, g_ref, m_ref, v_ref, *rest):
        g_out, d_out, m_out, v_out = rest[-4:]
        gv = g_ref[...]
        d_out[...], m_out[...], v_out[...] = _adamw(w_ref[...], gv, m_ref[...], v_ref[...])
        g_out[...] = gv

    in_specs = [lay, pl.BlockSpec((tr, c), lambda i: (i, 0)), lay, lay]
    args = [w, g, m, v]
    aliases = {}
    if prev is not None:
        in_specs += [pl.BlockSpec(memory_space=pl.ANY)] * 4
        args += list(prev)
        aliases = {4 + k: k for k in range(4)}
    return _pcall(body, name=name, grid=(r // tr,), in_specs=in_specs, out_specs=[lay] * 4,
                  out_shape=[jax.ShapeDtypeStruct(w.shape, F32)] * 4, aliases=aliases)(*args)


SMALL_NAMES = ("attn_norm_g", "q_lat_norm_g", "kv_lat_norm_g", "mla_q_norm_g", "mla_k_norm_g", "fox_q_norm_g",
               "fox_k_norm_g", "fox_f_bias", "s5_lambda_re", "s5_lambda_im", "s5_b_re", "s5_b_im", "s5_c_re", "s5_c_im",
               "s5_d", "s5_log_step", "s5_b_glu", "ffn_norm_g")
BIG_NAMES = ("w_in", "w_uq", "w_ukv", "s5_w_glu", "w_branch", "w_out", "w_up", "ffn_conv_w", "w_down")
ALL_NAMES = ("attn_norm_g", "w_in", "q_lat_norm_g", "w_uq", "kv_lat_norm_g", "w_ukv", "mla_q_norm_g", "mla_k_norm_g",
             "fox_q_norm_g", "fox_k_norm_g", "fox_f_bias", "s5_lambda_re", "s5_lambda_im", "s5_b_re", "s5_b_im",
             "s5_c_re", "s5_c_im", "s5_d", "s5_log_step", "s5_w_glu", "s5_b_glu", "w_branch", "w_out", "ffn_norm_g",
             "w_up", "ffn_conv_w", "w_down")


def _pack(arrs):
    tile = 8 * LANES
    parts = []
    for a in arrs:
        n = int(np.prod(a.shape))
        tot = -(-n // tile) * tile
        parts.append(jnp.pad(a.reshape(-1), (0, tot - n)).reshape(tot // LANES, LANES))
    rows = sum(a.shape[0] for a in parts)
    parts.append(jnp.zeros((-rows % 256, LANES), parts[0].dtype))
    return jnp.concatenate(parts, axis=0)


def _unpack(packed, like):
    tile = 8 * LANES
    out, pos = [], 0
    for a in like:
        n = int(np.prod(a.shape))
        rows = -(-n // tile) * 8
        out.append(packed[pos:pos + rows].reshape(-1)[:n].reshape(a.shape))
        pos += rows
    return out


def _rope_tables(positions, t, rider=None):
    pos = positions.reshape(t, 1)
    inv = _inv_freq_row()

    def fn(pv, iv):
        ang = pv.astype(F32) * iv
        return jnp.cos(ang), jnp.sin(ang)

    return _tile_call(fn, "rope_tables", (t // TM,), [_row(pos), _par(inv)], [_orow(t, QKW), _orow(t, QKW)],
                      rider=rider)


def _loss_call(x1, f, target):
    t = x1.shape[0]

    def fn(xv, fv, tv):
        e = xv + fv - tv
        per_tok = jnp.sum(e * e, axis=1, keepdims=True) * (1.0 / D_MODEL)
        return 0.5 * jnp.sum(per_tok, axis=0, keepdims=True), e * (1.0 / D_MODEL)

    return _tile_call(fn, "loss", (t // TM,), [_row(x1), _row(f), _row(target)], [_oacc((1, 1)), _orow(t, D_MODEL)])


def _local_shards(l, w):
    return {
        "w_in": _pad_in_cols(w["w_in"][l]).astype(BF16),
        "w_uq": w["w_uq"][l].reshape(-1, N_HEADS * MLA_QK_DIM).astype(BF16),
        "w_ukv": w["w_ukv"][l].reshape(-1, QKW).astype(BF16),
        "w_glu": w["s5_w_glu"][l].astype(BF16),
        "w_branch": w["w_branch"][l].astype(BF16),
        "w_out": w["w_out"][l].astype(BF16),
        "w_up": _pad_ff(w["w_up"][l], 1).astype(BF16),
        "conv": _pad_ff(w["ffn_conv_w"][l], 1),
        "w_down": _pad_ff(w["w_down"][l], 0).astype(BF16),
    }


GATHERED_SHAPES = {"w_in": (D_MODEL, PW), "w_uq": (MLA_Q_RANK, N_HEADS * MLA_QK_DIM), "w_ukv": (MLA_KV_RANK, QKW),
                   "w_glu": (S5_WIDTH, S5_WIDTH), "w_branch": (3, S5_WIDTH, D_MODEL), "w_out": (D_MODEL, D_MODEL),
                   "w_up": (D_MODEL, UP_W), "conv": (3, UP_W), "w_down": (FF_PW, D_MODEL)}
FFN_WEIGHTS = ("w_up", "conv", "w_down")
FIRST_EARLY_WEIGHTS = ("w_in", "w_uq", "w_ukv", "w_glu")
FIRST_LATE_WEIGHTS = ("w_out", "w_branch")
MIXER_WEIGHTS = ("w_in", "w_uq", "w_ukv", "w_glu", "w_out", "w_branch")
FWD_RIDERS_OWN = {"attn_mla_fwd": ("w_down",), "s5_scan_fwd": ("w_up", "conv")}
FWD_RIDERS_NEXT = {"attn_fox_fwd": ("w_in", "w_uq", "w_ukv", "w_glu"), "mixer_pre": ("w_out", "w_branch")}
BWD_RIDERS = {"mm_up_dx": ("w_up:0",), "s5_scan_bwd": ("w_down",), "attn_mla_bwd": ("w_up:1",),
              "attn_fox_bwd": ("w_out", "w_branch", "w_glu", "conv"), "mm_in_dx": ("w_uq", "w_ukv")}
UPDATED_TRANSPOSED = ("w_up",)
WEIGHT_OF = {"w_in": "w_in", "w_uq": "w_uq", "w_ukv": "w_ukv", "w_glu": "s5_w_glu", "w_branch": "w_branch",
             "w_out": "w_out", "w_up": "w_up", "conv": "ffn_conv_w", "w_down": "w_down"}


def _unshard_layout(nm, g2):
    if nm == "w_in":
        return _unpad_in_cols(g2)
    if nm == "w_uq":
        return g2.reshape(-1, N_HEADS, MLA_QK_DIM)
    if nm == "w_ukv":
        return g2.reshape(-1, N_HEADS, MLA_NOPE_DIM + HEAD_V)
    if nm in ("w_up", "conv"):
        return _unpad_ff(g2, 1)
    if nm == "w_down":
        return _unpad_ff(g2, 0)
    return g2


def kernel(x, positions, attn_norm_g, w_in, q_lat_norm_g, w_uq, kv_lat_norm_g, w_ukv, mla_q_norm_g, mla_k_norm_g, fox_q_norm_g, fox_k_norm_g, fox_f_bias, s5_lambda_re, s5_lambda_im, s5_b_re, s5_b_im, s5_c_re, s5_c_im, s5_d, s5_log_step, s5_w_glu, s5_b_glu, w_branch, w_out, ffn_norm_g, w_up, ffn_conv_w, w_down, loss_target, m_attn_norm_g, m_w_in, m_q_lat_norm_g, m_w_uq, m_kv_lat_norm_g, m_w_ukv, m_mla_q_norm_g, m_mla_k_norm_g, m_fox_q_norm_g, m_fox_k_norm_g, m_fox_f_bias, m_s5_lambda_re, m_s5_lambda_im, m_s5_b_re, m_s5_b_im, m_s5_c_re, m_s5_c_im, m_s5_d, m_s5_log_step, m_s5_w_glu, m_s5_b_glu, m_w_branch, m_w_out, m_ffn_norm_g, m_w_up, m_ffn_conv_w, m_w_down, v_attn_norm_g, v_w_in, v_q_lat_norm_g, v_w_uq, v_kv_lat_norm_g, v_w_ukv, v_mla_q_norm_g, v_mla_k_norm_g, v_fox_q_norm_g, v_fox_k_norm_g, v_fox_f_bias, v_s5_lambda_re, v_s5_lambda_im, v_s5_b_re, v_s5_b_im, v_s5_c_re, v_s5_c_im, v_s5_d, v_s5_log_step, v_s5_w_glu, v_s5_b_glu, v_w_branch, v_w_out, v_ffn_norm_g, v_w_up, v_ffn_conv_w, v_w_down):
    args = locals()
    w = {n: args[n] for n in ALL_NAMES}
    m = {n: args["m_" + n] for n in ALL_NAMES}
    v = {n: args["v_" + n] for n in ALL_NAMES}
    depth = w_in.shape[0]
    bsz, seq, _ = x.shape
    t = bsz * seq
    consts = _constants()
    all_shards = [_local_shards(l, w) for l in range(depth)]

    def gather_xfer(l, names):
        sub = {n: all_shards[l][n] for n in names}
        order, plan = _gather_plan(sub)
        return _Gather2([sub[n] for n in order], [(GATHERED_SHAPES[n], sub[n].dtype) for n in order], plan)

    res = _rope_tables(positions, t, gather_xfer(0, FIRST_EARLY_WEIGHTS))
    tabs, gathered = res[:2], dict(zip(FIRST_EARLY_WEIGHTS, res[2:]))
    small = {n: w[n] for n in SMALL_NAMES}

    all_names = list(GATHERED_SHAPES)
    def scatter_xfer(g, names):
        shapes = {n: _local_shards_shape(n) for n in names}
        return _Xfer([g[n] for n in names], [((N_DEV,) + shapes[n], g[n].dtype) for n in names],
                     _scatter_plan(list(names), shapes))

    xs = x.reshape(t, D_MODEL)
    f_prev = None
    saved, params = [], []
    for l in range(depth):
        p = _prep_layer(l, gathered, small, consts)
        riders = {host: (gather_xfer(l, names), names) for host, names in FWD_RIDERS_OWN.items()}
        if l == 0:
            riders["norm_first"] = (gather_xfer(0, FIRST_LATE_WEIGHTS), FIRST_LATE_WEIGHTS)
        if l + 1 < depth:
            riders.update({host: (gather_xfer(l + 1, names), names) for host, names in FWD_RIDERS_NEXT.items()})
        xs, f_prev, sv, gathered = _layer_fwd(xs, f_prev, p, tabs, consts, bsz, riders)
        saved.append(sv)
        params.append(p)

    loss_part, dy = _loss_call(xs, f_prev, loss_target.reshape(t, D_MODEL))
    loss = lax.psum(loss_part[0, 0], ("x", "y", "c"))

    small_grads = {n: [None] * depth for n in SMALL_NAMES}
    big = {n: None for n in BIG_NAMES}

    def finish(l, lands):
        for n in all_names:
            wn = WEIGHT_OF[n]
            if n == "w_up":
                gsum = jnp.concatenate([_sum8(lands["w_up:%d" % h], "sum_w_up") for h in range(2)], axis=1)
            else:
                ld = lands[n]
                gsum = _sum8(ld.reshape(N_DEV, -1, ld.shape[-1]), "sum_" + n).reshape(_local_shards_shape(n))
            gsum = _unshard_layout(n, gsum)
            if n in UPDATED_TRANSPOSED:
                gsum = gsum.T
            c = gsum.shape[-1]
            three = lambda a: (jnp.swapaxes(a, 1, 2) if n in UPDATED_TRANSPOSED else a).reshape(depth, -1, c)
            big[wn] = _adamw_layer(three(w[wn]), gsum.reshape(-1, c), three(m[wn]), three(v[wn]), l, big[wn],
                                   "adamw_" + n)

    def scatter(grads):
        return scatter_xfer(grads, list(grads))

    dx = dy
    carry, lands_above = {}, None
    for l in reversed(range(depth)):
        dx, g, lands, carry_lands = _layer_bwd(dx, params[l], saved[l], tabs, consts, bsz, scatter, carry, l == 0)
        if lands_above is not None:
            finish(l + 1, {**lands_above, **carry_lands})
        carry, lands_above = {"w_in": g["w_in"]}, lands
        small_grads["attn_norm_g"][l] = g["attn_g"].reshape(D_MODEL)
        small_grads["ffn_norm_g"][l] = g["ffn_g"].reshape(D_MODEL)
        for n in SMALL_NAMES:
            if n not in ("attn_norm_g", "ffn_norm_g"):
                small_grads[n][l] = g[n]
    finish(0, lands_above)

    sg = [jnp.stack(small_grads[n]).reshape(w[n].shape) for n in SMALL_NAMES]
    packed = _pack(sg)
    land = _exchange("gather_small_grads", _Gather2([packed], [((N_DEV,) + packed.shape, F32)],
                                                    _scatter_plan(["small"], {})))[0]
    gs = _sum8(land, "sum_small")
    one = lambda names_of: _pack([names_of[n] for n in SMALL_NAMES])[None]
    small_res = _adamw_layer(one(w), gs, one(m), one(v), 0, None, "adamw_small")
    like = [w[n] for n in SMALL_NAMES]
    small_out = [dict(zip(SMALL_NAMES, _unpack(a[0], like))) for a in small_res]

    def out_of(kind, n):
        if n in SMALL_NAMES:
            return small_out[kind][n]
        if n in UPDATED_TRANSPOSED:
            return jnp.swapaxes(big[n][kind], 1, 2)
        return big[n][kind].reshape(w[n].shape)

    outs = [loss, dx.reshape(bsz, seq, D_MODEL)]
    for kind in range(4):
        outs += [out_of(kind, n) for n in ALL_NAMES]
    return tuple(outs)


def _local_shards_shape(nm):
    return {"w_in": (D_MODEL // N_DEV, PW), "w_uq": (MLA_Q_RANK // N_DEV, N_HEADS * MLA_QK_DIM),
            "w_ukv": (MLA_KV_RANK // N_DEV, QKW), "w_glu": (S5_WIDTH // N_DEV, S5_WIDTH),
            "w_branch": (3, S5_WIDTH, LANES), "w_out": (D_MODEL // N_DEV, D_MODEL), "w_up": (D_MODEL, 2 * FF_BLK),
            "conv": (3, 2 * FF_BLK), "w_down": (FF_BLK, D_MODEL), "w_up:0": (D_MODEL, FF_BLK),
            "w_up:1": (D_MODEL, FF_BLK)}[nm]
```
